```python
import jax, jax.numpy as jnp
from jax import lax
import numpy as np

D_MODEL = 1024
BATCH = 8
SEQ = 2048
DEPTH = 1
DEC_BATCH = 8
DEC_SEQ = 64
PAST_LEN = 4096

CHUNK = 64
HEAD_DIM = 64
N_HEADS_FOX = 8
N_HEADS_BAND = 8
W_FOX = N_HEADS_FOX * HEAD_DIM
W_BAND = N_HEADS_BAND * HEAD_DIM
MIX_WIDTH = W_FOX + W_BAND
PROJ_COLS = 3 * W_FOX + 3 * W_BAND + N_HEADS_FOX
LEFT_CHUNKS = 8
LEFT = LEFT_CHUNKS * CHUNK
BAND = (LEFT_CHUNKS + 1) * CHUNK
REL_CLIP = 128
N_MEM = 256
N_HEADS_MEM = 4
HEAD_DIM_MEM = 128
W_MEM = N_HEADS_MEM * HEAD_DIM_MEM
N_GROUPS = 4
EXPERTS_PER_GROUP = 8
N_EXPERTS = N_GROUPS * EXPERTS_PER_GROUP
TOP_K_INNER = 2
D_EXPERT = 512
Q_BLOCK = 128
EPS = 1e-6
NEG_INF = -1e30
FORGET_BIAS_INIT = 4.0
ATTN_SCALE = HEAD_DIM ** -0.5
MEM_SCALE = HEAD_DIM_MEM ** -0.5

kernel_name = "fox_band_hier_moe_stream_step"


def rmsnorm(x, g):
    xf = x.astype(jnp.float32)
    y = xf * lax.rsqrt(jnp.mean(xf * xf, axis=-1, keepdims=True) + EPS)
    return (y * g.astype(jnp.float32)).astype(x.dtype)


def mixer_proj(h, w_in, b_f):
    z = jnp.einsum('bsd,de->bse', h, w_in)
    B, S, _ = z.shape
    idx = [W_FOX, 2 * W_FOX, 3 * W_FOX, 3 * W_FOX + W_BAND, 3 * W_FOX + 2 * W_BAND, 3 * W_FOX + 3 * W_BAND]
    qa, ka, va, qb, kb, vb, fl = jnp.split(z, idx, axis=-1)
    hd = lambda t, n: t.reshape(B, S, n, HEAD_DIM)
    logf = jax.nn.log_sigmoid(fl.astype(jnp.float32) + b_f.astype(jnp.float32))
    return (hd(qa, N_HEADS_FOX), hd(ka, N_HEADS_FOX), hd(va, N_HEADS_FOX),
            hd(qb, N_HEADS_BAND), hd(kb, N_HEADS_BAND), hd(vb, N_HEADS_BAND), logf)


def fox_prompt(q, k, v, logf):
    B, S, H, Dh = q.shape
    nb = S // Q_BLOCK
    c = jnp.cumsum(logf, axis=1)
    cT = c.transpose(0, 2, 1)
    qb = q.reshape(B, nb, Q_BLOCK, H, Dh).transpose(1, 0, 2, 3, 4)
    cb = c.reshape(B, nb, Q_BLOCK, H).transpose(1, 0, 3, 2)
    pos_k = jnp.arange(S)

    def block(args):
        qi, ci, bi = args
        s = jnp.einsum('bqhd,bkhd->bhqk', qi, k).astype(jnp.float32) * ATTN_SCALE
        s = s + ci[..., None] - cT[:, :, None, :]
        pos_q = bi * Q_BLOCK + jnp.arange(Q_BLOCK)
        s = jnp.where(pos_k[None, :] <= pos_q[:, None], s, NEG_INF)
        p = jax.nn.softmax(s, axis=-1).astype(v.dtype)
        return jnp.einsum('bhqk,bkhd->bqhd', p, v)

    out = lax.map(block, (qb, cb, jnp.arange(nb)))
    return out.transpose(1, 0, 2, 3, 4).reshape(B, S, H, Dh)


def fox_sample(q, k, v, logf, ck, cv, clogf):
    P = ck.shape[1]
    T = q.shape[1]
    k_all = jnp.concatenate([ck, k], axis=1)
    v_all = jnp.concatenate([cv, v], axis=1)
    c = jnp.cumsum(jnp.concatenate([clogf.astype(jnp.float32), logf], axis=1), axis=1)
    cT = c.transpose(0, 2, 1)
    s = jnp.einsum('bqhd,bkhd->bhqk', q, k_all).astype(jnp.float32) * ATTN_SCALE
    s = s + cT[:, :, P:, None] - cT[:, :, None, :]
    mask = jnp.arange(P + T)[None, :] <= P + jnp.arange(T)[:, None]
    s = jnp.where(mask, s, NEG_INF)
    p = jax.nn.softmax(s, axis=-1).astype(v_all.dtype)
    return jnp.einsum('bhqk,bkhd->bqhd', p, v_all)


def rel_bias(table, rel):
    idx = jnp.clip(rel, -REL_CLIP, REL_CLIP) + REL_CLIP
    return table[idx].transpose(2, 0, 1).astype(jnp.float32)


def band_prompt(q, k, v, table):
    B, S, H, Dh = q.shape
    nc = S // CHUNK
    pad = ((0, 0), (LEFT, 0), (0, 0), (0, 0))
    kp = jnp.pad(k, pad).reshape(B, nc + LEFT_CHUNKS, CHUNK, H, Dh)
    vp = jnp.pad(v, pad).reshape(B, nc + LEFT_CHUNKS, CHUNK, H, Dh)
    kb = jnp.concatenate([kp[:, j:j + nc] for j in range(LEFT_CHUNKS + 1)], axis=2)
    vb = jnp.concatenate([vp[:, j:j + nc] for j in range(LEFT_CHUNKS + 1)], axis=2)
    qc = q.reshape(B, nc, CHUNK, H, Dh)
    s = jnp.einsum('bnqhd,bnkhd->bnhqk', qc, kb).astype(jnp.float32) * ATTN_SCALE
    rel = jnp.arange(CHUNK)[:, None] + LEFT - jnp.arange(BAND)[None, :]
    s = s + rel_bias(table, rel)[None, None]
    key_pos = (jnp.arange(nc)[:, None] - LEFT_CHUNKS) * CHUNK + jnp.arange(BAND)[None, :]
    s = jnp.where((key_pos >= 0)[None, :, None, None, :], s, NEG_INF)
    p = jax.nn.softmax(s, axis=-1).astype(vb.dtype)
    return jnp.einsum('bnhqk,bnkhd->bnqhd', p, vb).reshape(B, S, H, Dh)


def band_sample(q, k, v, ck, cv, table):
    BP = ck.shape[1]
    T = q.shape[1]
    k_all = jnp.concatenate([ck, k], axis=1)
    v_all = jnp.concatenate([cv, v], axis=1)
    s = jnp.einsum('bqhd,bkhd->bhqk', q, k_all).astype(jnp.float32) * ATTN_SCALE
    rel = jnp.arange(T)[:, None] + BP - jnp.arange(BP + T)[None, :]
    s = s + rel_bias(table, rel)[None]
    p = jax.nn.softmax(s, axis=-1).astype(v_all.dtype)
    o = jnp.einsum('bhqk,bkhd->bqhd', p, v_all)
    return o, k_all[:, T:], v_all[:, T:]


def merge_groups(o_fox, o_band, g_of, g_ob, w_out):
    B, S = o_fox.shape[:2]
    a = rmsnorm(o_fox.reshape(B, S, W_FOX), g_of)
    b = rmsnorm(o_band.reshape(B, S, W_BAND), g_ob)
    return jnp.einsum('bse,ed->bsd', jnp.concatenate([a, b], axis=-1), w_out)


def mem_kv(mem, g_mem, w_ck, w_cv):
    B = mem.shape[0]
    m = rmsnorm(mem, g_mem)
    mk = jnp.einsum('bmd,de->bme', m, w_ck).reshape(B, -1, N_HEADS_MEM, HEAD_DIM_MEM)
    mv = jnp.einsum('bmd,de->bme', m, w_cv).reshape(B, -1, N_HEADS_MEM, HEAD_DIM_MEM)
    return mk, mv


def cross_attn(h, mk, mv, w_cq, w_co):
    B, S, _ = h.shape
    q = jnp.einsum('bsd,de->bse', h, w_cq).reshape(B, S, N_HEADS_MEM, HEAD_DIM_MEM)
    s = jnp.einsum('bqhd,bmhd->bhqm', q, mk).astype(jnp.float32) * MEM_SCALE
    p = jax.nn.softmax(s, axis=-1).astype(mv.dtype)
    o = jnp.einsum('bhqm,bmhd->bqhd', p, mv).reshape(B, S, W_MEM)
    return jnp.einsum('bse,ed->bsd', o, w_co)


def hier_moe(h, w_r1, b_r1, w_r2, b_r2, w_g, w_u, w_d):
    B, S, Dm = h.shape
    x = h.reshape(B * S, Dm)
    l1 = jnp.einsum('td,dg->tg', x, w_r1).astype(jnp.float32) + b_r1.astype(jnp.float32)
    p1 = jax.nn.softmax(l1, axis=-1)
    grp = jnp.argmax(l1, axis=-1)
    wg = jnp.take_along_axis(p1, grp[:, None], axis=-1)[:, 0]
    l2 = (jnp.einsum('td,de->te', x, w_r2).astype(jnp.float32) + b_r2.astype(jnp.float32))
    l2 = l2.reshape(-1, N_GROUPS, EXPERTS_PER_GROUP)
    l2g = jnp.take_along_axis(l2, grp[:, None, None], axis=1)[:, 0]
    top_v, top_i = lax.top_k(jax.nn.softmax(l2g, axis=-1), TOP_K_INNER)
    top_v = top_v / jnp.sum(top_v, axis=-1, keepdims=True)
    eid = grp[:, None] * EXPERTS_PER_GROUP + top_i
    gate = jnp.sum(jax.nn.one_hot(eid, N_EXPERTS, dtype=jnp.float32)
                   * (wg[:, None] * top_v)[..., None], axis=1)
    y = jnp.zeros((B * S, Dm), jnp.float32)
    for e in range(N_EXPERTS):
        a = jax.nn.silu(x @ w_g[e]) * (x @ w_u[e])
        y = y + gate[:, e:e + 1] * (a @ w_d[e]).astype(jnp.float32)
    return y.astype(h.dtype).reshape(B, S, Dm)


def setup_inputs(seed: int = 0) -> dict:
    key = jax.random.key(seed)
    ks = iter(jax.random.split(key, 40))
    nrm = lambda shape, s=1.0: s * jax.random.normal(next(ks), shape, jnp.float32)
    gain = lambda shape: 1.0 + nrm(shape, 0.02)
    bp = min(LEFT, PAST_LEN)
    return {
        "x_prompt": nrm((BATCH, SEQ, D_MODEL)),
        "x_sample": nrm((DEC_BATCH, DEC_SEQ, D_MODEL)),
        "cache_fox_k": nrm((DEPTH, DEC_BATCH, PAST_LEN, N_HEADS_FOX, HEAD_DIM)),
        "cache_fox_v": nrm((DEPTH, DEC_BATCH, PAST_LEN, N_HEADS_FOX, HEAD_DIM)),
        "cache_fox_logf": jax.nn.log_sigmoid(FORGET_BIAS_INIT + nrm((DEPTH, DEC_BATCH, PAST_LEN, N_HEADS_FOX), 0.5)),
        "cache_band_k": nrm((DEPTH, DEC_BATCH, bp, N_HEADS_BAND, HEAD_DIM)),
        "cache_band_v": nrm((DEPTH, DEC_BATCH, bp, N_HEADS_BAND, HEAD_DIM)),
        "cache_mem_k": nrm((DEPTH, DEC_BATCH, N_MEM, N_HEADS_MEM, HEAD_DIM_MEM)),
        "cache_mem_v": nrm((DEPTH, DEC_BATCH, N_MEM, N_HEADS_MEM, HEAD_DIM_MEM)),
        "mem_prompt": nrm((BATCH, N_MEM, D_MODEL)),
        "g_mix": gain((DEPTH, D_MODEL)),
        "w_in": nrm((DEPTH, D_MODEL, PROJ_COLS), D_MODEL ** -0.5),
        "b_forget": FORGET_BIAS_INIT + nrm((DEPTH, N_HEADS_FOX), 0.5),
        "g_out_fox": gain((DEPTH, W_FOX)),
        "g_out_band": gain((DEPTH, W_BAND)),
        "rel_table": nrm((DEPTH, 2 * REL_CLIP + 1, N_HEADS_BAND), 0.2),
        "w_out": nrm((DEPTH, MIX_WIDTH, D_MODEL), MIX_WIDTH ** -0.5),
        "g_cross": gain((DEPTH, D_MODEL)),
        "g_mem": gain((DEPTH, D_MODEL)),
        "w_cq": nrm((DEPTH, D_MODEL, W_MEM), D_MODEL ** -0.5),
        "w_ck": nrm((DEPTH, D_MODEL, W_MEM), D_MODEL ** -0.5),
        "w_cv": nrm((DEPTH, D_MODEL, W_MEM), D_MODEL ** -0.5),
        "w_co": nrm((DEPTH, W_MEM, D_MODEL), W_MEM ** -0.5),
        "g_ffn": gain((DEPTH, D_MODEL)),
        "w_router1": nrm((DEPTH, D_MODEL, N_GROUPS), D_MODEL ** -0.5),
        "b_router1": nrm((DEPTH, N_GROUPS), 0.01),
        "w_router2": nrm((DEPTH, D_MODEL, N_EXPERTS), D_MODEL ** -0.5),
        "b_router2": nrm((DEPTH, N_EXPERTS), 0.01),
        "w_exp_gate": nrm((DEPTH, N_EXPERTS, D_MODEL, D_EXPERT), D_MODEL ** -0.5),
        "w_exp_up": nrm((DEPTH, N_EXPERTS, D_MODEL, D_EXPERT), D_MODEL ** -0.5),
        "w_exp_down": nrm((DEPTH, N_EXPERTS, D_EXPERT, D_MODEL), D_EXPERT ** -0.5),
        "g_final": gain((D_MODEL,)),
    }


def reference(x_prompt, x_sample, cache_fox_k, cache_fox_v, cache_fox_logf, cache_band_k, cache_band_v,
              cache_mem_k, cache_mem_v, mem_prompt, g_mix, w_in, b_forget, g_out_fox, g_out_band, rel_table,
              w_out, g_cross, g_mem, w_cq, w_ck, w_cv, w_co, g_ffn, w_router1, b_router1, w_router2,
              b_router2, w_exp_gate, w_exp_up, w_exp_down, g_final):
    xp, xs = x_prompt, x_sample
    fkp, fvp, flp, bkp, bvp, mkp, mvp = [], [], [], [], [], [], []
    fks, fvs, fls, bks, bvs = [], [], [], [], []
    for l in range(DEPTH):
        h = rmsnorm(xp, g_mix[l])
        qa, ka, va, qb, kb, vb, lf = mixer_proj(h, w_in[l], b_forget[l])
        oa = fox_prompt(qa, ka, va, lf)
        ob = band_prompt(qb, kb, vb, rel_table[l])
        xp = xp + merge_groups(oa, ob, g_out_fox[l], g_out_band[l], w_out[l])
        mk, mv = mem_kv(mem_prompt, g_mem[l], w_ck[l], w_cv[l])
        xp = xp + cross_attn(rmsnorm(xp, g_cross[l]), mk, mv, w_cq[l], w_co[l])
        xp = xp + hier_moe(rmsnorm(xp, g_ffn[l]), w_router1[l], b_router1[l], w_router2[l], b_router2[l],
                           w_exp_gate[l], w_exp_up[l], w_exp_down[l])
        nkeep = min(LEFT, kb.shape[1])
        fkp.append(ka); fvp.append(va); flp.append(lf)
        bkp.append(kb[:, kb.shape[1] - nkeep:]); bvp.append(vb[:, vb.shape[1] - nkeep:])
        mkp.append(mk); mvp.append(mv)
        h = rmsnorm(xs, g_mix[l])
        qa, ka, va, qb, kb, vb, lf = mixer_proj(h, w_in[l], b_forget[l])
        oa = fox_sample(qa, ka, va, lf, cache_fox_k[l], cache_fox_v[l], cache_fox_logf[l])
        ob, nbk, nbv = band_sample(qb, kb, vb, cache_band_k[l], cache_band_v[l], rel_table[l])
        xs = xs + merge_groups(oa, ob, g_out_fox[l], g_out_band[l], w_out[l])
        xs = xs + cross_attn(rmsnorm(xs, g_cross[l]), cache_mem_k[l], cache_mem_v[l], w_cq[l], w_co[l])
        xs = xs + hier_moe(rmsnorm(xs, g_ffn[l]), w_router1[l], b_router1[l], w_router2[l], b_router2[l],
                           w_exp_gate[l], w_exp_up[l], w_exp_down[l])
        fks.append(ka); fvs.append(va); fls.append(lf.astype(cache_fox_logf.dtype))
        bks.append(nbk); bvs.append(nbv)
    y_prompt = rmsnorm(xp, g_final)
    y_sample = rmsnorm(xs, g_final)
    fox_k_p = jnp.stack(fkp); fox_v_p = jnp.stack(fvp); fox_logf_p = jnp.stack(flp)
    band_k_p = jnp.stack(bkp); band_v_p = jnp.stack(bvp)
    mem_k_p = jnp.stack(mkp); mem_v_p = jnp.stack(mvp)
    fox_k_s = jnp.stack(fks); fox_v_s = jnp.stack(fvs); fox_logf_s = jnp.stack(fls)
    band_k_s = jnp.stack(bks); band_v_s = jnp.stack(bvs)
    return (y_prompt, y_sample, fox_k_p, fox_v_p, fox_logf_p, band_k_p, band_v_p, mem_k_p, mem_v_p,
            fox_k_s, fox_v_s, fox_logf_s, band_k_s, band_v_s)
```

```python
import functools

import jax
import jax.numpy as jnp
from jax import lax
from jax.experimental import pallas as pl
from jax.experimental.pallas import tpu as pltpu

F32 = jnp.float32
BF16 = jnp.bfloat16

D_MODEL = 1024
HEAD_DIM = 64
N_HEADS = 8
W_GROUP = N_HEADS * HEAD_DIM
N_PAIRS = N_HEADS // 2
CHUNK = 64
LEFT_CHUNKS = 8
LEFT = LEFT_CHUNKS * CHUNK
REL_CLIP = 128
EPS = 1e-6
NEG_INF = -1e30
ATTN_SCALE = HEAD_DIM ** -0.5
LANES = 128
PROJ_PAD = 3 * W_GROUP * 2 + LANES
VMEM_LIMIT = 56 * 1024 * 1024


def _rms(x, g):
    ms = jnp.mean(x * x, axis=-1, keepdims=True)
    return x * lax.rsqrt(ms + EPS) * g


def _log_sigmoid(x):
    return -(jnp.maximum(-x, 0.0) + jnp.log1p(jnp.exp(-jnp.abs(x))))


def _lane_cumsum(x):
    n = x.shape[-1]
    lane = lax.broadcasted_iota(jnp.int32, x.shape, 1)
    k = 1
    while k < n:
        x = x + jnp.where(lane >= k, pltpu.roll(x, k, axis=1), 0.0)
        k *= 2
    return x


def _proj_body(x_ref, g_ref, w_ref, bf_ref,
               qa_ref, ka_ref, va_ref, qb_ref, kb_ref, vb_ref,
               kaf_ref, vaf_ref, kbf_ref, vbf_ref, logf_ref, *rest, tm, cumsum, keep_tiles):
    s = pl.program_id(1)
    ns = pl.num_programs(1)
    h = _rms(x_ref[...], g_ref[...]).astype(BF16)
    z = jnp.dot(h, w_ref[...], preferred_element_type=F32)
    w = W_GROUP
    qa_ref[...] = (z[:, 0:w] * ATTN_SCALE).astype(BF16)
    ka = z[:, w:2 * w]
    va = z[:, 2 * w:3 * w]
    ka_ref[...] = ka.astype(BF16)
    va_ref[...] = va.astype(BF16)
    kaf_ref[...] = ka
    vaf_ref[...] = va
    qb_ref[...] = (z[:, 3 * w:4 * w] * ATTN_SCALE).astype(BF16)
    kb = z[:, 4 * w:5 * w]
    vb = z[:, 5 * w:6 * w]
    kb_ref[...] = kb.astype(BF16)
    vb_ref[...] = vb.astype(BF16)

    @pl.when(s >= ns - keep_tiles)
    def _():
        kbf_ref[...] = kb
        vbf_ref[...] = vb

    logf = _log_sigmoid(z[:, 6 * w:6 * w + LANES] + bf_ref[...])
    logf_ref[...] = logf[:, :N_HEADS]
    lt = logf.T[:N_HEADS, :]
    if not cumsum:
        rest[0][...] = lt
    else:
        c_ref, ct_ref, carry_ref = rest

        @pl.when(s == 0)
        def _():
            carry_ref[...] = jnp.zeros_like(carry_ref)

        ct = _lane_cumsum(lt) + carry_ref[:, 0:1]
        carry_ref[...] = jnp.broadcast_to(ct[:, tm - 1:tm], carry_ref.shape)
        ct_ref[...] = ct
        ct_pad = jnp.concatenate([ct, jnp.zeros((LANES - N_HEADS, tm), F32)], axis=0)
        c_ref[...] = ct_pad.T[:, :N_HEADS]


def _proj(x, g_mix, w_pad, bf_pad, *, tm, cumsum):
    b, s, d = x.shape
    ns = s // tm
    keep = min(LEFT, s)
    assert s % tm == 0 and keep % tm == 0
    keep_tiles = keep // tm
    row = lambda: pl.BlockSpec((None, tm, W_GROUP), lambda i, j: (i, j, 0))
    keep_spec = lambda: pl.BlockSpec(
        (None, tm, W_GROUP), lambda i, j: (i, jnp.maximum(j - (ns - keep_tiles), 0), 0))
    const = lambda shape: pl.BlockSpec(shape, lambda i, j: (0,) * len(shape))
    out_shape = [jax.ShapeDtypeStruct((b, s, W_GROUP), BF16)] * 6
    out_shape += [jax.ShapeDtypeStruct((b, s, W_GROUP), F32)] * 2
    out_shape += [jax.ShapeDtypeStruct((b, keep, W_GROUP), F32)] * 2
    out_shape += [jax.ShapeDtypeStruct((b, s, N_HEADS), F32)]
    out_specs = [row() for _ in range(8)] + [keep_spec(), keep_spec()]
    out_specs += [pl.BlockSpec((None, tm, N_HEADS), lambda i, j: (i, j, 0))]
    scratch = []
    if cumsum:
        out_shape += [jax.ShapeDtypeStruct((b, s, N_HEADS), F32), jax.ShapeDtypeStruct((b, N_HEADS, s), F32)]
        out_specs += [pl.BlockSpec((None, tm, N_HEADS), lambda i, j: (i, j, 0)),
                      pl.BlockSpec((None, N_HEADS, tm), lambda i, j: (i, 0, j))]
        scratch = [pltpu.VMEM((N_HEADS, LANES), F32)]
    else:
        out_shape += [jax.ShapeDtypeStruct((b, N_HEADS, s), F32)]
        out_specs += [pl.BlockSpec((None, N_HEADS, tm), lambda i, j: (i, 0, j))]
    return pl.pallas_call(
        functools.partial(_proj_body, tm=tm, cumsum=cumsum, keep_tiles=keep_tiles),
        grid=(b, ns),
        in_specs=[pl.BlockSpec((None, tm, d), lambda i, j: (i, j, 0)),
                  const((1, d)), const(w_pad.shape), const((1, LANES))],
        out_specs=out_specs,
        out_shape=out_shape,
        scratch_shapes=scratch,
        compiler_params=pltpu.CompilerParams(
            dimension_semantics=("parallel", "arbitrary"), vmem_limit_bytes=VMEM_LIMIT),
        name="proj",
    )(x, g_mix, w_pad, bf_pad)


def _prep_proj(w_in, b_forget, g_mix):
    cols = w_in.shape[-1]
    w_pad = jnp.pad(w_in, ((0, 0), (0, PROJ_PAD - cols))).astype(BF16)
    bf_pad = jnp.pad(b_forget.reshape(1, -1), ((0, 0), (0, LANES - N_HEADS))).astype(F32)
    return w_pad, bf_pad, g_mix.reshape(1, -1)


def _pair_masks():
    lane = lax.broadcasted_iota(jnp.int32, (1, LANES), 1)
    return lane < HEAD_DIM


def _head_q(q128, even_lanes, parity):
    keep = even_lanes if parity == 0 else jnp.logical_not(even_lanes)
    return jnp.where(keep, q128, jnp.zeros_like(q128))


def _head_v(v128, even_lanes, parity):
    keep = even_lanes if parity == 0 else jnp.logical_not(even_lanes)
    return jnp.where(keep, v128, jnp.ones_like(v128))


def _head_out(acc_even, acc_odd, even_lanes):
    inv_e = 1.0 / acc_even[:, HEAD_DIM:HEAD_DIM + 1]
    inv_o = 1.0 / acc_odd[:, 0:1]
    return jnp.where(even_lanes, acc_even * inv_e, acc_odd * inv_o)


_NT = (((1,), (1,)), ((), ()))


def _fox_body(q_ref, k_ref, v_ref, c_ref, ct_ref, g_ref, o_ref, o_scr, *, tq):
    qi = pl.program_id(1)
    even = _pair_masks()
    row = lax.broadcasted_iota(jnp.int32, (tq, tq), 0)
    col = lax.broadcasted_iota(jnp.int32, (tq, tq), 1)
    causal = col <= row
    for hp in range(N_PAIRS):
        lanes = slice(hp * LANES, (hp + 1) * LANES)
        q128 = q_ref[:, lanes]
        accs = []
        for parity in range(2):
            h = 2 * hp + parity
            qh = _head_q(q128, even, parity)
            c_col = c_ref[:, h:h + 1]

            def step(j, carry, masked, qh=qh, c_col=c_col, h=h, lanes=lanes, parity=parity):
                m, acc = carry
                start = pl.multiple_of(j * tq, tq)
                k = k_ref[pl.ds(start, tq), lanes]
                v = _head_v(v_ref[pl.ds(start, tq), lanes], even, parity)
                s = lax.dot_general(qh, k, _NT, preferred_element_type=F32)
                s = s + c_col - ct_ref[h:h + 1, pl.ds(start, tq)]
                if masked:
                    s = jnp.where(causal, s, NEG_INF)
                m_new = jnp.maximum(m, jnp.max(s, axis=-1, keepdims=True))
                p = jnp.exp(s - m_new).astype(BF16)
                acc = acc * jnp.exp(m - m_new) + jnp.dot(p, v, preferred_element_type=F32)
                return m_new, acc

            init = (jnp.full((tq, 1), NEG_INF, F32), jnp.zeros((tq, LANES), F32))
            carry = lax.fori_loop(0, qi, functools.partial(step, masked=False), init)
            _, acc = step(qi, carry, True)
            accs.append(acc)
        o_scr[:, lanes] = _head_out(accs[0], accs[1], even)
    o_ref[...] = _rms(o_scr[...], g_ref[...]).astype(BF16)


def _fox_prompt(qa, ka, va, c, ct, g_out, *, tq):
    b, s, w = qa.shape
    return pl.pallas_call(
        functools.partial(_fox_body, tq=tq),
        grid=(b, s // tq),
        in_specs=[pl.BlockSpec((None, tq, w), lambda i, j: (i, j, 0)),
                  pl.BlockSpec((None, s, w), lambda i, j: (i, 0, 0)),
                  pl.BlockSpec((None, s, w), lambda i, j: (i, 0, 0)),
                  pl.BlockSpec((None, tq, N_HEADS), lambda i, j: (i, j, 0)),
                  pl.BlockSpec((None, N_HEADS, s), lambda i, j: (i, 0, 0)),
                  pl.BlockSpec((1, w), lambda i, j: (0, 0))],
        out_specs=pl.BlockSpec((None, tq, w), lambda i, j: (i, j, 0)),
        out_shape=jax.ShapeDtypeStruct((b, s, w), BF16),
        scratch_shapes=[pltpu.VMEM((tq, w), F32)],
        compiler_params=pltpu.CompilerParams(
            dimension_semantics=("parallel", "arbitrary"), vmem_limit_bytes=VMEM_LIMIT),
        name="fox_prompt",
    )(qa, ka, va, c, ct, g_out)


BAND_Q = 2 * CHUNK
BAND_K = (LEFT_CHUNKS + 2) * CHUNK
BIAS_ROW = BAND_K + BAND_Q


def _prep_band_bias_row(rel_table):
    n_flat = BIAS_ROW - 2 * REL_CLIP + 1
    flat = jnp.broadcast_to(rel_table[2 * REL_CLIP:], (n_flat, N_HEADS))
    return jnp.concatenate([flat, rel_table[2 * REL_CLIP - 1:0:-1]], axis=0).T.reshape(N_HEADS, 1, BIAS_ROW)


def _band_bias_body(row_ref, o_ref):
    rows = jnp.broadcast_to(row_ref[...], (BAND_Q, BIAS_ROW))
    skew = pltpu.roll(rows, 0, axis=1, stride=1, stride_axis=0)
    bias = skew[:, BAND_Q:]
    qc = lax.broadcasted_iota(jnp.int32, (BAND_Q, BAND_K), 0) // CHUNK
    kc = lax.broadcasted_iota(jnp.int32, (BAND_Q, BAND_K), 1) // CHUNK
    o_ref[...] = jnp.where((kc >= qc) & (kc <= qc + LEFT_CHUNKS), bias, NEG_INF)


def _band_bias(bias_row):
    return pl.pallas_call(
        _band_bias_body,
        grid=(N_HEADS,),
        in_specs=[pl.BlockSpec((None, 1, BIAS_ROW), lambda h: (h, 0, 0))],
        out_specs=pl.BlockSpec((None, BAND_Q, BAND_K), lambda h: (h, 0, 0)),
        out_shape=jax.ShapeDtypeStruct((N_HEADS, BAND_Q, BAND_K), F32),
        name="band_bias",
    )(bias_row)


def _band_body(q_ref, k_ref, v_ref, bias_ref, g_ref, o_ref, kpad, vpad, o_scr, *, s_len):
    n2 = pl.program_id(1)

    @pl.when(n2 == 0)
    def _():
        kpad[0:LEFT, :] = jnp.zeros((LEFT, W_GROUP), BF16)
        vpad[0:LEFT, :] = jnp.zeros((LEFT, W_GROUP), BF16)
        kpad[LEFT:LEFT + s_len, :] = k_ref[...]
        vpad[LEFT:LEFT + s_len, :] = v_ref[...]

    even = _pair_masks()
    start = pl.multiple_of(n2 * BAND_Q, BAND_Q)
    col = lax.broadcasted_iota(jnp.int32, (1, BAND_K), 1)
    valid = col >= LEFT - n2 * BAND_Q
    for hp in range(N_PAIRS):
        lanes = slice(hp * LANES, (hp + 1) * LANES)
        q128 = q_ref[:, lanes]
        kwin = kpad[pl.ds(start, BAND_K), lanes]
        vwin = vpad[pl.ds(start, BAND_K), lanes]
        accs = []
        for parity in range(2):
            h = 2 * hp + parity
            s = lax.dot_general(_head_q(q128, even, parity), kwin, _NT, preferred_element_type=F32)
            s = jnp.where(valid, s + bias_ref[h], NEG_INF)
            m = jnp.max(s, axis=-1, keepdims=True)
            p = jnp.exp(s - m).astype(BF16)
            accs.append(jnp.dot(p, _head_v(vwin, even, parity), preferred_element_type=F32))
        o_scr[:, lanes] = _head_out(accs[0], accs[1], even)
    o_ref[...] = _rms(o_scr[...], g_ref[...]).astype(BF16)


def _band_prompt(qb, kb, vb, bias, g_out):
    b, s, w = qb.shape
    return pl.pallas_call(
        functools.partial(_band_body, s_len=s),
        grid=(b, s // BAND_Q),
        in_specs=[pl.BlockSpec((None, BAND_Q, w), lambda i, j: (i, j, 0)),
                  pl.BlockSpec((None, s, w), lambda i, j: (i, 0, 0)),
                  pl.BlockSpec((None, s, w), lambda i, j: (i, 0, 0)),
                  pl.BlockSpec(bias.shape, lambda i, j: (0, 0, 0)),
                  pl.BlockSpec((1, w), lambda i, j: (0, 0))],
        out_specs=pl.BlockSpec((None, BAND_Q, w), lambda i, j: (i, j, 0)),
        out_shape=jax.ShapeDtypeStruct((b, s, w), BF16),
        scratch_shapes=[pltpu.VMEM((LEFT + s, w), BF16), pltpu.VMEM((LEFT + s, w), BF16),
                        pltpu.VMEM((BAND_Q, w), F32)],
        compiler_params=pltpu.CompilerParams(
            dimension_semantics=("parallel", "arbitrary"), vmem_limit_bytes=VMEM_LIMIT),
        name="band_prompt",
    )(qb, kb, vb, bias, g_out)


def _row_to_col(row):
    n = row.shape[-1]
    r = lax.broadcasted_iota(jnp.int32, (n, n), 0)
    c = lax.broadcasted_iota(jnp.int32, (n, n), 1)
    return jnp.sum(jnp.where(r == c, jnp.broadcast_to(row, (n, n)), 0.0), axis=-1, keepdims=True)


def _fox_sample_body(q_ref, kn_ref, vn_ref, lft_ref, kc_ref, vc_ref, clft_ref, g_ref, o_ref,
                     cct_scr, cn_scr, m_scr, acc_scr, o_scr, *, t_new, pt):
    p_idx = pl.program_id(1)
    n_p = pl.num_programs(1)
    even = _pair_masks()

    @pl.when(p_idx == 0)
    def _():
        cct = _lane_cumsum(clft_ref[...])
        cct_scr[...] = cct
        cn_scr[...] = _lane_cumsum(lft_ref[...]) + cct[:, cct.shape[1] - 1:]
        m_scr[...] = jnp.full(m_scr.shape, NEG_INF, F32)
        acc_scr[...] = jnp.zeros(acc_scr.shape, F32)

    def update(h, s, v):
        m = m_scr[h]
        m_new = jnp.maximum(m, jnp.max(s, axis=-1, keepdims=True))
        p = jnp.exp(s - m_new).astype(BF16)
        acc_scr[h] = acc_scr[h] * jnp.exp(m - m_new) + jnp.dot(p, v, preferred_element_type=F32)
        m_scr[h] = m_new

    start = pl.multiple_of(p_idx * pt, pt)
    for hp in range(N_PAIRS):
        lanes = slice(hp * LANES, (hp + 1) * LANES)
        q128 = q_ref[:, lanes]
        kc = kc_ref[:, lanes].astype(BF16)
        vc = vc_ref[:, lanes].astype(BF16)
        for parity in range(2):
            h = 2 * hp + parity
            cq = _row_to_col(cn_scr[h:h + 1, :])
            s = lax.dot_general(_head_q(q128, even, parity), kc, _NT, preferred_element_type=F32)
            update(h, s + cq - cct_scr[h:h + 1, pl.ds(start, pt)], _head_v(vc, even, parity))

    @pl.when(p_idx == n_p - 1)
    def _():
        row = lax.broadcasted_iota(jnp.int32, (t_new, t_new), 0)
        col = lax.broadcasted_iota(jnp.int32, (t_new, t_new), 1)
        for hp in range(N_PAIRS):
            lanes = slice(hp * LANES, (hp + 1) * LANES)
            q128 = q_ref[:, lanes]
            kn = kn_ref[:, lanes]
            vn = vn_ref[:, lanes]
            for parity in range(2):
                h = 2 * hp + parity
                cn_row = cn_scr[h:h + 1, :]
                s = lax.dot_general(_head_q(q128, even, parity), kn, _NT, preferred_element_type=F32)
                s = jnp.where(col <= row, s + _row_to_col(cn_row) - cn_row, NEG_INF)
                update(h, s, _head_v(vn, even, parity))
            o_scr[:, lanes] = _head_out(acc_scr[2 * hp], acc_scr[2 * hp + 1], even)
        o_ref[...] = _rms(o_scr[...], g_ref[...]).astype(BF16)


def _fox_sample(q, kn, vn, lft, kc, vc, clft, g_out, *, pt):
    b, t, w = q.shape
    p_len = kc.shape[1]
    new = lambda: pl.BlockSpec((None, t, w), lambda i, j: (i, 0, 0))
    cache = lambda: pl.BlockSpec((None, pt, w), lambda i, j: (i, j, 0))
    return pl.pallas_call(
        functools.partial(_fox_sample_body, t_new=t, pt=pt),
        grid=(b, p_len // pt),
        in_specs=[new(), new(), new(),
                  pl.BlockSpec((None, N_HEADS, t), lambda i, j: (i, 0, 0)),
                  cache(), cache(),
                  pl.BlockSpec((None, N_HEADS, p_len), lambda i, j: (i, 0, 0)),
                  pl.BlockSpec((1, w), lambda i, j: (0, 0))],
        out_specs=new(),
        out_shape=jax.ShapeDtypeStruct((b, t, w), BF16),
        scratch_shapes=[pltpu.VMEM((N_HEADS, p_len), F32), pltpu.VMEM((N_HEADS, t), F32),
                        pltpu.VMEM((N_HEADS, t, 1), F32), pltpu.VMEM((N_HEADS, t, LANES), F32),
                        pltpu.VMEM((t, w), F32)],
        compiler_params=pltpu.CompilerParams(
            dimension_semantics=("parallel", "arbitrary"), vmem_limit_bytes=VMEM_LIMIT),
        name="fox_sample",
    )(q, kn, vn, lft, kc, vc, clft, g_out)


def _band_sample_body(q_ref, kn_ref, vn_ref, knf_ref, vnf_ref, kc_ref, vc_ref, bias_ref, g_ref,
                      o_ref, nk_ref, nv_ref, kcat, vcat, o_scr, *, t_new, bp):
    kcat[0:bp, :] = kc_ref[...].astype(BF16)
    vcat[0:bp, :] = vc_ref[...].astype(BF16)
    kcat[bp:bp + t_new, :] = kn_ref[...]
    vcat[bp:bp + t_new, :] = vn_ref[...]
    nk_ref[0:bp - t_new, :] = kc_ref[t_new:bp, :]
    nv_ref[0:bp - t_new, :] = vc_ref[t_new:bp, :]
    nk_ref[bp - t_new:bp, :] = knf_ref[...]
    nv_ref[bp - t_new:bp, :] = vnf_ref[...]
    even = _pair_masks()
    for hp in range(N_PAIRS):
        lanes = slice(hp * LANES, (hp + 1) * LANES)
        q128 = q_ref[:, lanes]
        k = kcat[:, lanes]
        v = vcat[:, lanes]
        accs = []
        for parity in range(2):
            h = 2 * hp + parity
            s = lax.dot_general(_head_q(q128, even, parity), k, _NT, preferred_element_type=F32)
            s = s + bias_ref[h, 0:t_new, 0:bp + t_new]
            p = jnp.exp(s - jnp.max(s, axis=-1, keepdims=True)).astype(BF16)
            accs.append(jnp.dot(p, _head_v(v, even, parity), preferred_element_type=F32))
        o_scr[:, lanes] = _head_out(accs[0], accs[1], even)
    o_ref[...] = _rms(o_scr[...], g_ref[...]).astype(BF16)


def _band_sample(q, kn, vn, knf, vnf, kc, vc, bias, g_out):
    b, t, w = q.shape
    bp = kc.shape[1]
    assert t == CHUNK and bp == LEFT
    new = lambda: pl.BlockSpec((None, t, w), lambda i: (i, 0, 0))
    buf = lambda: pl.BlockSpec((None, bp, w), lambda i: (i, 0, 0))
    return pl.pallas_call(
        functools.partial(_band_sample_body, t_new=t, bp=bp),
        grid=(b,),
        in_specs=[new(), new(), new(), new(), new(), buf(), buf(),
                  pl.BlockSpec(bias.shape, lambda i: (0, 0, 0)),
                  pl.BlockSpec((1, w), lambda i: (0, 0))],
        out_specs=[new(), buf(), buf()],
        out_shape=[jax.ShapeDtypeStruct((b, t, w), BF16), jax.ShapeDtypeStruct((b, bp, w), F32),
                   jax.ShapeDtypeStruct((b, bp, w), F32)],
        scratch_shapes=[pltpu.VMEM((bp + t, w), BF16), pltpu.VMEM((bp + t, w), BF16), pltpu.VMEM((t, w), F32)],
        compiler_params=pltpu.CompilerParams(dimension_semantics=("parallel",), vmem_limit_bytes=VMEM_LIMIT),
        name="band_sample",
    )(q, kn, vn, knf, vnf, kc, vc, bias, g_out)


N_HEADS_MEM = 4
HEAD_DIM_MEM = 128
W_MEM = N_HEADS_MEM * HEAD_DIM_MEM
MEM_SCALE = HEAD_DIM_MEM ** -0.5


def _memkv_body(m_ref, g_ref, w_ref, kf_ref, vf_ref, k_ref, v_ref):
    h = _rms(m_ref[...], g_ref[...]).astype(BF16)
    z = jnp.dot(h, w_ref[...], preferred_element_type=F32)
    kf_ref[...] = z[:, :W_MEM]
    vf_ref[...] = z[:, W_MEM:]
    k_ref[...] = z[:, :W_MEM].astype(BF16)
    v_ref[...] = z[:, W_MEM:].astype(BF16)


def _mem_kv(mem, g_mem, w_ckv):
    b, n, d = mem.shape
    blk = lambda: pl.BlockSpec((None, n, W_MEM), lambda i: (i, 0, 0))
    return pl.pallas_call(
        _memkv_body,
        grid=(b,),
        in_specs=[pl.BlockSpec((None, n, d), lambda i: (i, 0, 0)),
                  pl.BlockSpec((1, d), lambda i: (0, 0)),
                  pl.BlockSpec(w_ckv.shape, lambda i: (0, 0))],
        out_specs=[blk(), blk(), blk(), blk()],
        out_shape=[jax.ShapeDtypeStruct((b, n, W_MEM), F32)] * 2 + [jax.ShapeDtypeStruct((b, n, W_MEM), BF16)] * 2,
        compiler_params=pltpu.CompilerParams(dimension_semantics=("parallel",), vmem_limit_bytes=VMEM_LIMIT),
        name="mem_kv",
    )(mem, g_mem, w_ckv)


N_GROUPS = 4
EXPERTS_PER_GROUP = 8
N_EXPERTS = N_GROUPS * EXPERTS_PER_GROUP
ROUTE_L2 = N_GROUPS
ROUTE_ROWS = 8
R_EID0, R_EID1, R_RANK0, R_RANK1, R_GATE0, R_GATE1 = range(6)


def _lane_max(x, mask):
    return jnp.max(jnp.where(mask, x, -jnp.inf), axis=-1, keepdims=True)


def _first_lane(mask, lane):
    return jnp.min(jnp.where(mask, lane, LANES), axis=-1, keepdims=True)


def _route(logits, lane):
    is_l1 = lane < N_GROUPS
    m1 = _lane_max(logits, is_l1)
    grp = _first_lane(is_l1 & (logits == m1), lane)
    wg = 1.0 / jnp.sum(jnp.where(is_l1, jnp.exp(logits - m1), 0.0), axis=-1, keepdims=True)
    lo = ROUTE_L2 + grp * EXPERTS_PER_GROUP
    in_grp = (lane >= lo) & (lane < lo + EXPERTS_PER_GROUP)
    v0 = _lane_max(logits, in_grp)
    i0 = _first_lane(in_grp & (logits == v0), lane)
    rest = in_grp & (lane != i0)
    v1 = _lane_max(logits, rest)
    i1 = _first_lane(rest & (logits == v1), lane)
    e1 = jnp.exp(v1 - v0)
    den = 1.0 / (1.0 + e1)
    return i0, i1, wg * den, wg * e1 * den


def _post_body(x_ref, a_ref, b_ref, mk_ref, mv_ref, woa_ref, wob_ref, gc_ref, wcq_ref, wco_ref,
               gf_ref, wrt_ref, brt_ref, cnt_in_ref,
               x2_ref, h3_ref, route_ref, routet_ref, cnt_out_ref, cnt_scr, o_scr, *, tm, nsub):
    first = (pl.program_id(0) == 0) & (pl.program_id(1) == 0)

    @pl.when(first)
    def _():
        cnt_scr[...] = cnt_in_ref[...]

    x1 = (x_ref[...] + jnp.dot(a_ref[...], woa_ref[...], preferred_element_type=F32)
          + jnp.dot(b_ref[...], wob_ref[...], preferred_element_type=F32))
    h2 = _rms(x1, gc_ref[...]).astype(BF16)
    qc = (jnp.dot(h2, wcq_ref[...], preferred_element_type=F32) * MEM_SCALE).astype(BF16)
    rows = tm // nsub
    for sub in range(nsub):
        rs = slice(sub * rows, (sub + 1) * rows)
        for hm in range(N_HEADS_MEM):
            lanes = slice(hm * HEAD_DIM_MEM, (hm + 1) * HEAD_DIM_MEM)
            s = lax.dot_general(qc[rs, lanes], mk_ref[sub, :, lanes], _NT, preferred_element_type=F32)
            p = jnp.exp(s - jnp.max(s, axis=-1, keepdims=True))
            inv = 1.0 / jnp.sum(p, axis=-1, keepdims=True)
            o_scr[rs, lanes] = jnp.dot(p.astype(BF16), mv_ref[sub, :, lanes], preferred_element_type=F32) * inv
    x2 = x1 + jnp.dot(o_scr[...].astype(BF16), wco_ref[...], preferred_element_type=F32)
    x2_ref[...] = x2
    h3 = _rms(x2, gf_ref[...]).astype(BF16)
    h3_ref[...] = _pack_rows(h3).reshape(tm, 1, HALF)

    logits = jnp.dot(h3, wrt_ref[...], preferred_element_type=F32) + brt_ref[...]
    lane = lax.broadcasted_iota(jnp.int32, (tm, LANES), 1)
    i0, i1, g0, g1 = _route(logits, lane)
    e0 = i0 - ROUTE_L2
    e1 = i1 - ROUTE_L2
    hit0 = lane == e0
    hit1 = lane == e1
    onehot = jnp.where(hit0 | hit1, 1.0, 0.0)
    row = lax.broadcasted_iota(jnp.int32, (tm, tm), 0)
    col = lax.broadcasted_iota(jnp.int32, (tm, tm), 1)
    before = jnp.where(col < row, 1.0, 0.0).astype(BF16)
    seen = jnp.dot(before, onehot.astype(BF16), preferred_element_type=F32) + cnt_scr[...]
    rank0 = jnp.sum(jnp.where(hit0, seen, 0.0), axis=-1, keepdims=True)
    rank1 = jnp.sum(jnp.where(hit1, seen, 0.0), axis=-1, keepdims=True)
    cnt_scr[...] = cnt_scr[...] + jnp.sum(onehot, axis=0, keepdims=True)
    cnt_out_ref[...] = cnt_scr[...]

    rec = jnp.zeros((tm, LANES), F32)
    for idx, val in ((R_EID0, e0.astype(F32)), (R_EID1, e1.astype(F32)), (R_RANK0, rank0),
                     (R_RANK1, rank1), (R_GATE0, g0), (R_GATE1, g1)):
        rec = jnp.where(lane == idx, val, rec)
    route_ref[...] = rec[:, :ROUTE_ROWS]
    routet_ref[...] = rec.T[:ROUTE_ROWS, :]


def _post_block(x, a_n, b_n, mk, mv, weights, cnt_in, *, tm):
    b, s, d = x.shape
    if s >= tm:
        nsub, grid = 1, (b, s // tm)
        tok = lambda i, j: (i, j, 0)
        flat = lambda i, j: i * (s // tm) + j
    else:
        nsub = tm // s
        assert b % nsub == 0
        x, a_n, b_n = (t.reshape(b // nsub, tm, t.shape[-1]) for t in (x, a_n, b_n))
        grid = (b // nsub, 1)
        tok = lambda i, j: (i, 0, 0)
        flat = lambda i, j: i
    mem = lambda i, j: (i, 0, 0)
    const = lambda arr: pl.BlockSpec(arr.shape, lambda i, j: (0,) * arr.ndim)
    in_specs = [pl.BlockSpec((None, tm, d), tok),
                pl.BlockSpec((None, tm, W_GROUP), tok), pl.BlockSpec((None, tm, W_GROUP), tok),
                pl.BlockSpec((nsub, mk.shape[1], W_MEM), mem), pl.BlockSpec((nsub, mv.shape[1], W_MEM), mem)]
    in_specs += [const(w) for w in weights] + [const(cnt_in)]
    n = b * s
    out_shape = [jax.ShapeDtypeStruct((n, d), F32), jax.ShapeDtypeStruct((n, 1, HALF), jnp.uint32),
                 jax.ShapeDtypeStruct((n, ROUTE_ROWS), F32), jax.ShapeDtypeStruct((ROUTE_ROWS, n), F32),
                 jax.ShapeDtypeStruct((1, LANES), F32)]
    out_specs = [pl.BlockSpec((tm, d), lambda i, j: (flat(i, j), 0)),
                 pl.BlockSpec((tm, 1, HALF), lambda i, j: (flat(i, j), 0, 0)),
                 pl.BlockSpec((tm, ROUTE_ROWS), lambda i, j: (flat(i, j), 0)),
                 pl.BlockSpec((ROUTE_ROWS, tm), lambda i, j: (0, flat(i, j))),
                 pl.BlockSpec((1, LANES), lambda i, j: (0, 0))]
    return pl.pallas_call(
        functools.partial(_post_body, tm=tm, nsub=nsub),
        grid=grid,
        in_specs=in_specs,
        out_specs=out_specs,
        out_shape=out_shape,
        scratch_shapes=[pltpu.VMEM((1, LANES), F32), pltpu.VMEM((tm, W_MEM), F32)],
        compiler_params=pltpu.CompilerParams(
            dimension_semantics=("arbitrary", "arbitrary"), vmem_limit_bytes=VMEM_LIMIT),
        name="post_block",
    )(x, a_n, b_n, mk, mv, *weights, cnt_in)


def _prep_post(w_out, g_cross, w_cq, w_co, g_ffn, w_r1, b_r1, w_r2, b_r2):
    pad = LANES - N_GROUPS - N_EXPERTS
    w_rt = jnp.pad(jnp.concatenate([w_r1, w_r2], axis=1), ((0, 0), (0, pad))).astype(BF16)
    b_rt = jnp.pad(jnp.concatenate([b_r1, b_r2]).reshape(1, -1), ((0, 0), (0, pad))).astype(F32)
    return [w_out[:W_GROUP].astype(BF16), w_out[W_GROUP:].astype(BF16), g_cross.reshape(1, -1),
            w_cq.astype(BF16), w_co.astype(BF16), g_ffn.reshape(1, -1), w_rt, b_rt]


D_EXPERT = 512
TOP_K = 2
ROW_TILE = 256
HALF = D_MODEL // 2


def _n_row_tiles(n_tokens):
    return (n_tokens * TOP_K + N_EXPERTS * (ROW_TILE - 1)) // ROW_TILE


def _pack_rows(h_bf16):
    hi = pltpu.bitcast(h_bf16[:, :HALF].astype(F32), jnp.uint32)
    lo = pltpu.bitcast(h_bf16[:, HALF:].astype(F32), jnp.uint32)
    return (hi & jnp.uint32(0xFFFF0000)) | (lo >> 16)


def _unpack_rows(w):
    hi = pltpu.bitcast(w & jnp.uint32(0xFFFF0000), F32).astype(BF16)
    lo = pltpu.bitcast(w << 16, F32).astype(BF16)
    return hi, lo


def _plan_body(rt_ref, cnt_ref, dest_ref, offs_ref, te_ref):
    lane = lax.broadcasted_iota(jnp.int32, (ROUTE_ROWS, LANES), 1)
    cnt = jnp.broadcast_to(cnt_ref[...], (ROUTE_ROWS, LANES)).astype(jnp.int32)
    padded = (cnt + (ROW_TILE - 1)) & ~(ROW_TILE - 1)
    ends = _lane_cumsum(padded.astype(F32)).astype(jnp.int32)
    offs = ends - padded
    offs_ref[...] = offs[0:1, :]
    eid0 = rt_ref[R_EID0].astype(jnp.int32)
    eid1 = rt_ref[R_EID1].astype(jnp.int32)
    d0 = rt_ref[R_RANK0].astype(jnp.int32)
    d1 = rt_ref[R_RANK1].astype(jnp.int32)
    tile_start = lax.broadcasted_iota(jnp.int32, te_ref.shape, 1) * ROW_TILE
    te = jnp.zeros(te_ref.shape, jnp.int32)
    for e in range(N_EXPERTS):
        off_e = jnp.sum(jnp.where(lane[0:1] == e, offs[0:1], 0), axis=-1, keepdims=True)
        end_e = jnp.sum(jnp.where(lane[0:1] == e, ends[0:1], 0), axis=-1, keepdims=True)
        d0 = d0 + jnp.where(eid0 == e, off_e, 0)
        d1 = d1 + jnp.where(eid1 == e, off_e, 0)
        te = te + jnp.where(end_e <= tile_start, 1, 0)
    dest_ref[0] = d0
    dest_ref[1] = d1
    te_ref[...] = jnp.minimum(te, N_EXPERTS - 1)


def _plan(route_t, cnt):
    n = route_t.shape[1]
    assert n % LANES == 0 and _n_row_tiles(n) <= 2 * LANES
    rt = route_t.reshape(ROUTE_ROWS, n // LANES, LANES)
    dest, offs, te = pl.pallas_call(
        _plan_body,
        out_shape=[jax.ShapeDtypeStruct((2, n // LANES, LANES), jnp.int32),
                   jax.ShapeDtypeStruct((1, LANES), jnp.int32),
                   jax.ShapeDtypeStruct((1, 2 * LANES), jnp.int32)],
        name="moe_plan",
    )(rt, cnt)
    return dest.reshape(2, n), offs.reshape(LANES), te.reshape(2 * LANES)


def _row_copy(src_ref, src_row, dst_ref, dst_row, sem):
    return pltpu.make_async_copy(src_ref.at[pl.ds(src_row, 1)], dst_ref.at[pl.ds(dst_row, 1)], sem)


WAIT_CHUNK = 64


def _wait_rows(copy, n):
    assert n % WAIT_CHUNK == 0

    def chunk(c, carry):
        for _ in range(WAIT_CHUNK):
            copy.wait()
        return carry

    lax.fori_loop(0, n // WAIT_CHUNK, chunk, None)


def _dispatch_body(offs_ref, d0_ref, d1_ref, hp_ref, hs_ref, xs_ref, zeros, sem, zsem, *, tm, n_prompt_tiles):
    i = pl.program_id(0)
    n_tiles = xs_ref.shape[0] // ROW_TILE

    @pl.when(i == 0)
    def _():
        zeros[...] = jnp.zeros(zeros.shape, zeros.dtype)
        zero_tile = lambda row: pltpu.make_async_copy(zeros, xs_ref.at[pl.ds(row, ROW_TILE)], zsem)
        n_used = offs_ref[N_EXPERTS] // ROW_TILE

        def tail(j, carry, op):
            op(zero_tile(j * ROW_TILE))
            return carry

        for op in (lambda c: c.start(), lambda c: c.wait()):
            for e in range(N_EXPERTS):
                @pl.when(offs_ref[e + 1] > offs_ref[e])
                def _():
                    op(zero_tile(offs_ref[e + 1] - ROW_TILE))
            lax.fori_loop(n_used, n_tiles, functools.partial(tail, op=op), None)

    def copy_tile(src_ref, base):
        def start(t, carry):
            _row_copy(src_ref, base + t, xs_ref, d0_ref[0, t], sem).start()
            _row_copy(src_ref, base + t, xs_ref, d1_ref[0, t], sem).start()
            return carry

        lax.fori_loop(0, tm, start, None, unroll=8)
        _wait_rows(_row_copy(src_ref, 0, xs_ref, 0, sem), TOP_K * tm)

    @pl.when(i < n_prompt_tiles)
    def _():
        copy_tile(hp_ref, i * tm)

    @pl.when(i >= n_prompt_tiles)
    def _():
        copy_tile(hs_ref, (i - n_prompt_tiles) * tm)


def _dispatch(h3p_prompt, h3p_sample, dest, offs, *, tm):
    n_p, n_s = h3p_prompt.shape[0], h3p_sample.shape[0]
    assert n_p % tm == 0 and n_s % tm == 0
    n = n_p + n_s
    nt = n // tm
    d0 = dest[0].reshape(nt, 1, tm)
    d1 = dest[1].reshape(nt, 1, tm)
    n_rows = _n_row_tiles(n) * ROW_TILE
    smem_tile = lambda: pl.BlockSpec((None, 1, tm), lambda i, offs: (i, 0, 0), memory_space=pltpu.SMEM)
    return pl.pallas_call(
        functools.partial(_dispatch_body, tm=tm, n_prompt_tiles=n_p // tm),
        grid_spec=pltpu.PrefetchScalarGridSpec(
            num_scalar_prefetch=1,
            grid=(nt,),
            in_specs=[smem_tile(), smem_tile(), pl.BlockSpec(memory_space=pl.ANY),
                      pl.BlockSpec(memory_space=pl.ANY)],
            out_specs=pl.BlockSpec(memory_space=pl.ANY),
            scratch_shapes=[pltpu.VMEM((ROW_TILE, 1, HALF), jnp.uint32),
                            pltpu.SemaphoreType.DMA(()), pltpu.SemaphoreType.DMA(())]),
        out_shape=jax.ShapeDtypeStruct((n_rows, 1, HALF), jnp.uint32),
        compiler_params=pltpu.CompilerParams(dimension_semantics=("arbitrary",), vmem_limit_bytes=VMEM_LIMIT),
        name="moe_dispatch",
    )(offs, d0, d1, h3p_prompt, h3p_sample)


def _experts_body(te_ref, offs_ref, xs_ref, wg_ref, wu_ref, wd_ref, ys_ref, wg_bf, wu_bf, wd_bf):
    i = pl.program_id(0)
    n_used = offs_ref[N_EXPERTS] // ROW_TILE

    @pl.when(i < n_used)
    def _():
        @pl.when((i == 0) | (te_ref[i] != te_ref[jnp.maximum(i - 1, 0)]))
        def _():
            wg_bf[...] = wg_ref[...].astype(BF16)
            wu_bf[...] = wu_ref[...].astype(BF16)
            wd_bf[...] = wd_ref[...].astype(BF16)

        hi, lo = _unpack_rows(xs_ref[...].reshape(ROW_TILE, HALF))
        dot = functools.partial(jnp.dot, preferred_element_type=F32)
        gate = dot(hi, wg_bf[0:HALF, :]) + dot(lo, wg_bf[HALF:, :])
        up = dot(hi, wu_bf[0:HALF, :]) + dot(lo, wu_bf[HALF:, :])
        act = (gate * jax.nn.sigmoid(gate) * up).astype(BF16)
        ys_ref[...] = dot(act, wd_bf[...]).reshape(ROW_TILE, 1, D_MODEL)

    @pl.when(i >= n_used)
    def _():
        ys_ref[...] = jnp.zeros(ys_ref.shape, ys_ref.dtype)


def _experts(xs, te, offs, w_gate, w_up, w_down):
    n_rows = xs.shape[0]
    last = lambda i, te, offs: jnp.minimum(i, offs[N_EXPERTS] // ROW_TILE - 1)
    wspec = lambda shape: pl.BlockSpec((None,) + shape, lambda i, te, offs: (te[last(i, te, offs)], 0, 0))
    return pl.pallas_call(
        _experts_body,
        grid_spec=pltpu.PrefetchScalarGridSpec(
            num_scalar_prefetch=2,
            grid=(n_rows // ROW_TILE,),
            in_specs=[pl.BlockSpec((ROW_TILE, 1, HALF), lambda i, te, offs: (last(i, te, offs), 0, 0)),
                      wspec((D_MODEL, D_EXPERT)), wspec((D_MODEL, D_EXPERT)), wspec((D_EXPERT, D_MODEL))],
            out_specs=pl.BlockSpec((ROW_TILE, 1, D_MODEL), lambda i, te, offs: (i, 0, 0)),
            scratch_shapes=[pltpu.VMEM((D_MODEL, D_EXPERT), BF16), pltpu.VMEM((D_MODEL, D_EXPERT), BF16),
                            pltpu.VMEM((D_EXPERT, D_MODEL), BF16)]),
        out_shape=jax.ShapeDtypeStruct((n_rows, 1, D_MODEL), F32),
        compiler_params=pltpu.CompilerParams(dimension_semantics=("arbitrary",), vmem_limit_bytes=VMEM_LIMIT),
        name="moe_experts",
    )(te, offs, xs, w_gate, w_up, w_down)


def _combine_body(d0_ref, d1_ref, x2p_ref, x2s_ref, route_ref, ys_ref, g_ref, yp_ref, ysm_ref, rows0, rows1, sem,
                  *, tm, n_prompt_tiles):
    i = pl.program_id(0)

    def start(t, carry):
        _row_copy(ys_ref, d0_ref[0, t], rows0, t, sem).start()
        _row_copy(ys_ref, d1_ref[0, t], rows1, t, sem).start()
        return carry

    lax.fori_loop(0, tm, start, None, unroll=8)
    _wait_rows(_row_copy(ys_ref, 0, rows0, 0, sem), TOP_K * tm)
    g0 = route_ref[:, R_GATE0:R_GATE0 + 1]
    g1 = route_ref[:, R_GATE1:R_GATE1 + 1]
    d = x2p_ref.shape[-1]
    moe = g0 * rows0[...].reshape(tm, d) + g1 * rows1[...].reshape(tm, d)

    @pl.when(i < n_prompt_tiles)
    def _():
        yp_ref[...] = _rms(x2p_ref[...] + moe, g_ref[...])

    @pl.when(i >= n_prompt_tiles)
    def _():
        ysm_ref[...] = _rms(x2s_ref[...] + moe, g_ref[...])


def _combine(x2_prompt, x2_sample, route, dest, ys, g_final, *, tm):
    (n_p, d), n_s = x2_prompt.shape, x2_sample.shape[0]
    assert n_p % tm == 0 and n_s % tm == 0
    npt = n_p // tm
    nt = npt + n_s // tm
    d0 = dest[0].reshape(nt, 1, tm)
    d1 = dest[1].reshape(nt, 1, tm)
    smem_tile = lambda: pl.BlockSpec((None, 1, tm), lambda i: (i, 0, 0), memory_space=pltpu.SMEM)
    prompt_tile = lambda: pl.BlockSpec((tm, d), lambda i: (jnp.minimum(i, npt - 1), 0))
    sample_tile = lambda: pl.BlockSpec((tm, d), lambda i: (jnp.maximum(i - npt, 0), 0))
    return pl.pallas_call(
        functools.partial(_combine_body, tm=tm, n_prompt_tiles=npt),
        grid=(nt,),
        in_specs=[smem_tile(), smem_tile(), prompt_tile(), sample_tile(),
                  pl.BlockSpec((tm, ROUTE_ROWS), lambda i: (i, 0)),
                  pl.BlockSpec(memory_space=pl.ANY),
                  pl.BlockSpec((1, d), lambda i: (0, 0))],
        out_specs=[prompt_tile(), sample_tile()],
        out_shape=[jax.ShapeDtypeStruct((n_p, d), F32), jax.ShapeDtypeStruct((n_s, d), F32)],
        scratch_shapes=[pltpu.VMEM((tm, 1, d), F32), pltpu.VMEM((tm, 1, d), F32), pltpu.SemaphoreType.DMA(())],
        compiler_params=pltpu.CompilerParams(dimension_semantics=("arbitrary",), vmem_limit_bytes=VMEM_LIMIT),
        name="moe_combine",
    )(d0, d1, x2_prompt, x2_sample, route, ys, g_final)


TOKEN_TILE = 512
FOX_Q_TILE = 256
FOX_CACHE_TILE = 1024


def kernel(x_prompt, x_sample, cache_fox_k, cache_fox_v, cache_fox_logf, cache_band_k, cache_band_v, cache_mem_k, cache_mem_v, mem_prompt, g_mix, w_in, b_forget, g_out_fox, g_out_band, rel_table, w_out, g_cross, g_mem, w_cq, w_ck, w_cv, w_co, g_ffn, w_router1, b_router1, w_router2, b_router2, w_exp_gate, w_exp_up, w_exp_down, g_final):
    assert g_mix.shape[0] == 1, "single-layer model"
    bsz, seq, d = x_prompt.shape
    sb, st, _ = x_sample.shape
    n_s = sb * st
    past = cache_fox_k.shape[2]
    n_mem = mem_prompt.shape[1]
    row = lambda g: g.reshape(1, -1)

    w_pad, bf_pad, g_mix_r = _prep_proj(w_in[0], b_forget[0], g_mix[0])
    g_of, g_ob = row(g_out_fox[0]), row(g_out_band[0])
    bias = _band_bias(_prep_band_bias_row(rel_table[0]))

    qa, ka, va, qb, kb, vb, kaf, vaf, kbf, vbf, logf, c, ct = _proj(
        x_prompt, g_mix_r, w_pad, bf_pad, tm=TOKEN_TILE, cumsum=True)
    a_p = _fox_prompt(qa, ka, va, c, ct, g_of, tq=FOX_Q_TILE)
    b_p = _band_prompt(qb, kb, vb, bias, g_ob)

    s_out = _proj(x_sample.reshape(1, n_s, d), g_mix_r, w_pad, bf_pad, tm=n_s, cumsum=False)
    sqa, ska, sva, sqb, skb, svb, skaf, svaf, skbf, svbf = (t.reshape(sb, st, W_GROUP) for t in s_out[:10])
    slogf = s_out[10].reshape(sb, st, N_HEADS)
    slft = s_out[11].reshape(N_HEADS, sb, st).transpose(1, 0, 2)
    a_s = _fox_sample(sqa, ska, sva, slft,
                      cache_fox_k[0].reshape(sb, past, W_GROUP), cache_fox_v[0].reshape(sb, past, W_GROUP),
                      cache_fox_logf[0].transpose(0, 2, 1), g_of, pt=FOX_CACHE_TILE)
    bp = cache_band_k.shape[2]
    b_s, nbk, nbv = _band_sample(sqb, skb, svb, skbf, svbf,
                                 cache_band_k[0].reshape(sb, bp, W_GROUP), cache_band_v[0].reshape(sb, bp, W_GROUP),
                                 bias, g_ob)

    w_ckv = jnp.concatenate([w_ck[0], w_cv[0]], axis=1).astype(BF16)
    mkf, mvf, mk, mv = _mem_kv(mem_prompt, row(g_mem[0]), w_ckv)
    post_w = _prep_post(w_out[0], g_cross[0], w_cq[0], w_co[0], g_ffn[0],
                        w_router1[0], b_router1[0], w_router2[0], b_router2[0])
    cnt0 = jnp.zeros((1, LANES), F32)
    x2_p, h3_p, route_p, routet_p, cnt1 = _post_block(x_prompt, a_p, b_p, mk, mv, post_w, cnt0, tm=TOKEN_TILE)
    cmk = cache_mem_k[0].reshape(sb, n_mem, W_MEM).astype(BF16)
    cmv = cache_mem_v[0].reshape(sb, n_mem, W_MEM).astype(BF16)
    x2_s, h3_s, route_s, routet_s, cnt2 = _post_block(x_sample, a_s, b_s, cmk, cmv, post_w, cnt1, tm=TOKEN_TILE)
    route = jnp.concatenate([route_p, route_s], axis=0)
    route_t = jnp.concatenate([routet_p, routet_s], axis=1)

    dest, offs, tile_expert = _plan(route_t, cnt2)
    xs = _dispatch(h3_p, h3_s, dest, offs, tm=TOKEN_TILE)
    ys = _experts(xs, tile_expert, offs, w_exp_gate[0], w_exp_up[0], w_exp_down[0])
    y_p, y_s = _combine(x2_p, x2_s, route, dest, ys, row(g_final), tm=TOKEN_TILE)

    heads = lambda t, n: t.reshape(1, n, -1, N_HEADS, HEAD_DIM)
    mem_heads = lambda t: t.reshape(1, bsz, n_mem, N_HEADS_MEM, HEAD_DIM_MEM)
    return (y_p.reshape(bsz, seq, d), y_s.reshape(sb, st, d),
            heads(kaf, bsz), heads(vaf, bsz), logf.reshape(1, bsz, seq, N_HEADS),
            heads(kbf, bsz), heads(vbf, bsz), mem_heads(mkf), mem_heads(mvf),
            heads(skaf, sb), heads(svaf, sb), slogf.reshape(1, sb, st, N_HEADS),
            heads(nbk, sb), heads(nbv, sb))
```

```python
import functools

import jax
import jax.numpy as jnp
from jax import lax
from jax.experimental import pallas as pl
from jax.experimental.pallas import tpu as pltpu

F32 = jnp.float32
BF16 = jnp.bfloat16

D_MODEL = 1024
HEAD_DIM = 64
N_HEADS = 8
W_GROUP = N_HEADS * HEAD_DIM
N_PAIRS = N_HEADS // 2
CHUNK = 64
LEFT_CHUNKS = 8
LEFT = LEFT_CHUNKS * CHUNK
REL_CLIP = 128
EPS = 1e-6
NEG_INF = -1e30
ATTN_SCALE = HEAD_DIM ** -0.5
LANES = 128
PROJ_PAD = 3 * W_GROUP * 2 + LANES
VMEM_LIMIT = 56 * 1024 * 1024


def _rms(x, g):
    ms = jnp.mean(x * x, axis=-1, keepdims=True)
    return x * lax.rsqrt(ms + EPS) * g


def _log_sigmoid(x):
    return -(jnp.maximum(-x, 0.0) + jnp.log1p(jnp.exp(-jnp.abs(x))))


def _lane_cumsum(x):
    n = x.shape[-1]
    lane = lax.broadcasted_iota(jnp.int32, x.shape, 1)
    k = 1
    while k < n:
        x = x + jnp.where(lane >= k, pltpu.roll(x, k, axis=1), 0.0)
        k *= 2
    return x


LOG2E = 1.4426950408889634
SCALE_BASE2 = ATTN_SCALE * LOG2E


def _split3(x):
    hi = x.astype(BF16).astype(F32)
    mid = (x - hi).astype(BF16).astype(F32)
    lo = x - hi - mid
    return hi, mid, lo


def _extra_lane(parity):
    return HEAD_DIM if parity == 0 else 0


def _fox_extras(c3t, hp, tm):
    row = lax.broadcasted_iota(jnp.int32, (8, tm), 0)

    def group(h, q_side):
        hi, mid, lo = (p[h:h + 1, :] for p in c3t)
        if q_side:
            return jnp.where(row < 3, 1.0, jnp.where(row == 3, hi, jnp.where(row == 4, mid, jnp.where(row == 5, lo, 0.0))))
        return jnp.where(row == 0, -hi, jnp.where(row == 1, -mid, jnp.where(row == 2, -lo, jnp.where(row < 6, 1.0, 0.0))))

    gap = jnp.zeros((HEAD_DIM - 8, tm), F32)
    sides = []
    for q_side in (True, False):
        t = jnp.concatenate([group(2 * hp + 1, q_side), gap, group(2 * hp, q_side), gap], axis=0)
        sides.append(t.T)
    return sides


def _head_blocks(x128, extras, lane):
    return (jnp.where(lane < HEAD_DIM, x128, extras).astype(BF16),
            jnp.where(lane >= HEAD_DIM, x128, extras).astype(BF16))


def _proj_common(x_ref, g_ref, w_ref, bf_ref, kaf_ref, vaf_ref, kbf_ref, vbf_ref, logf_ref, keep_tiles):
    s = pl.program_id(1)
    ns = pl.num_programs(1)
    h = _rms(x_ref[...], g_ref[...]).astype(BF16)
    z = jnp.dot(h, w_ref[...], preferred_element_type=F32)
    w = W_GROUP
    kaf_ref[...] = z[:, w:2 * w]
    vaf_ref[...] = z[:, 2 * w:3 * w]

    @pl.when(s >= ns - keep_tiles)
    def _():
        kbf_ref[...] = z[:, 4 * w:5 * w]
        vbf_ref[...] = z[:, 5 * w:6 * w]

    logf = _log_sigmoid(z[:, 6 * w:6 * w + LANES] + bf_ref[...])
    logf_ref[...] = logf[:, :N_HEADS]
    return z, logf


def _proj_prompt_body(x_ref, g_ref, w_ref, bf_ref, qx_ref, kx_ref, vat_ref, qxb_ref, kxb_ref, vbt_ref,
                      kaf_ref, vaf_ref, kbf_ref, vbf_ref, logf_ref, carry_ref, *, tm, keep_tiles):
    z, logf = _proj_common(x_ref, g_ref, w_ref, bf_ref, kaf_ref, vaf_ref, kbf_ref, vbf_ref, logf_ref, keep_tiles)
    w = W_GROUP
    vat_ref[...] = z[:, 2 * w:3 * w].T.astype(BF16)
    vbt_ref[...] = z[:, 5 * w:6 * w].T.astype(BF16)

    @pl.when(pl.program_id(1) == 0)
    def _():
        carry_ref[...] = jnp.zeros_like(carry_ref)

    ct = _lane_cumsum(logf.T[:N_HEADS, :]) + carry_ref[:, 0:1]
    carry_ref[...] = jnp.broadcast_to(ct[:, tm - 1:tm], carry_ref.shape)
    c3t = _split3(ct * LOG2E)
    lane = lax.broadcasted_iota(jnp.int32, (tm, LANES), 1)
    band_q_extras = jnp.where((lane == _extra_lane(0)) | (lane == _extra_lane(1)), 1.0, 0.0)
    band_k_extras = jnp.zeros((tm, LANES), F32)
    for hp in range(N_PAIRS):
        blocks = slice(2 * hp * LANES, 2 * (hp + 1) * LANES)
        blk = lambda base, hp=hp: z[:, base * w + hp * LANES:base * w + (hp + 1) * LANES]
        q_extras, k_extras = _fox_extras(c3t, hp, tm)
        qx_ref[:, blocks] = jnp.concatenate(_head_blocks(blk(0) * SCALE_BASE2, q_extras, lane), axis=1)
        kx_ref[:, blocks] = jnp.concatenate(_head_blocks(blk(1), k_extras, lane), axis=1)
        qxb_ref[:, blocks] = jnp.concatenate(_head_blocks(blk(3) * SCALE_BASE2, band_q_extras, lane), axis=1)
        kxb_ref[:, blocks] = jnp.concatenate(_head_blocks(blk(4), band_k_extras, lane), axis=1)


def _proj_sample_body(x_ref, g_ref, w_ref, bf_ref, qa_ref, ka_ref, va_ref, qb_ref, kb_ref, vb_ref,
                      kaf_ref, vaf_ref, kbf_ref, vbf_ref, logf_ref, lt_ref, *, keep_tiles):
    z, logf = _proj_common(x_ref, g_ref, w_ref, bf_ref, kaf_ref, vaf_ref, kbf_ref, vbf_ref, logf_ref, keep_tiles)
    w = W_GROUP
    qa_ref[...] = (z[:, 0:w] * ATTN_SCALE).astype(BF16)
    ka_ref[...] = z[:, w:2 * w].astype(BF16)
    va_ref[...] = z[:, 2 * w:3 * w].astype(BF16)
    qb_ref[...] = (z[:, 3 * w:4 * w] * ATTN_SCALE).astype(BF16)
    kb_ref[...] = z[:, 4 * w:5 * w].astype(BF16)
    vb_ref[...] = z[:, 5 * w:6 * w].astype(BF16)
    lt_ref[...] = logf.T[:N_HEADS, :]


def _proj(x, g_mix, w_pad, bf_pad, *, tm, prompt):
    b, s, d = x.shape
    ns = s // tm
    keep = min(LEFT, s)
    assert s % tm == 0 and keep % tm == 0
    keep_tiles = keep // tm
    row = pl.BlockSpec((None, tm, W_GROUP), lambda i, j: (i, j, 0))
    wide = pl.BlockSpec((None, tm, N_HEADS * LANES), lambda i, j: (i, j, 0))
    col = pl.BlockSpec((None, W_GROUP, tm), lambda i, j: (i, 0, j))
    keep_spec = pl.BlockSpec((None, tm, W_GROUP), lambda i, j: (i, jnp.maximum(j - (ns - keep_tiles), 0), 0))
    heads_row = pl.BlockSpec((None, tm, N_HEADS), lambda i, j: (i, j, 0))
    heads_col = pl.BlockSpec((None, N_HEADS, tm), lambda i, j: (i, 0, j))
    const = lambda shape: pl.BlockSpec(shape, lambda i, j: (0,) * len(shape))
    rows_bf = jax.ShapeDtypeStruct((b, s, W_GROUP), BF16)
    wide_bf = jax.ShapeDtypeStruct((b, s, N_HEADS * LANES), BF16)
    cols_bf = jax.ShapeDtypeStruct((b, W_GROUP, s), BF16)
    f32_tail = [jax.ShapeDtypeStruct((b, s, W_GROUP), F32)] * 2
    f32_tail += [jax.ShapeDtypeStruct((b, keep, W_GROUP), F32)] * 2
    f32_tail += [jax.ShapeDtypeStruct((b, s, N_HEADS), F32)]
    tail_specs = [row, row, keep_spec, keep_spec, heads_row]
    if prompt:
        body = functools.partial(_proj_prompt_body, tm=tm, keep_tiles=keep_tiles)
        out_shape = [wide_bf, wide_bf, cols_bf, wide_bf, wide_bf, cols_bf] + f32_tail
        out_specs = [wide, wide, col, wide, wide, col] + tail_specs
        scratch = [pltpu.VMEM((N_HEADS, LANES), F32)]
    else:
        body = functools.partial(_proj_sample_body, keep_tiles=keep_tiles)
        out_shape = [rows_bf] * 6 + f32_tail + [jax.ShapeDtypeStruct((b, N_HEADS, s), F32)]
        out_specs = [row] * 6 + tail_specs + [heads_col]
        scratch = []
    return pl.pallas_call(
        body,
        grid=(b, ns),
        in_specs=[pl.BlockSpec((None, tm, d), lambda i, j: (i, j, 0)),
                  const((1, d)), const(w_pad.shape), const((1, LANES))],
        out_specs=out_specs,
        out_shape=out_shape,
        scratch_shapes=scratch,
        compiler_params=pltpu.CompilerParams(
            dimension_semantics=("parallel", "arbitrary"), vmem_limit_bytes=VMEM_LIMIT),
        name="proj",
    )(x, g_mix, w_pad, bf_pad)


def _prep_proj(w_in, b_forget, g_mix):
    cols = w_in.shape[-1]
    w_pad = jnp.pad(w_in, ((0, 0), (0, PROJ_PAD - cols))).astype(BF16)
    bf_pad = jnp.pad(b_forget.reshape(1, -1), ((0, 0), (0, LANES - N_HEADS))).astype(F32)
    return w_pad, bf_pad, g_mix.reshape(1, -1)


def _pair_masks():
    lane = lax.broadcasted_iota(jnp.int32, (1, LANES), 1)
    return lane < HEAD_DIM


def _head_q(q128, even_lanes, parity):
    keep = even_lanes if parity == 0 else jnp.logical_not(even_lanes)
    return jnp.where(keep, q128, jnp.zeros_like(q128))


def _head_v(v128, even_lanes, parity):
    keep = even_lanes if parity == 0 else jnp.logical_not(even_lanes)
    return jnp.where(keep, v128, jnp.ones_like(v128))


def _head_out(acc_even, acc_odd, even_lanes):
    inv_e = 1.0 / acc_even[:, HEAD_DIM:HEAD_DIM + 1]
    inv_o = 1.0 / acc_odd[:, 0:1]
    return jnp.where(even_lanes, acc_even * inv_e, acc_odd * inv_o)


_NT = (((1,), (1,)), ((), ()))


def _pair_rows():
    row = lax.broadcasted_iota(jnp.int32, (LANES, 1), 0)
    return row < HEAD_DIM


def _head_vt(vt128, even_rows, parity):
    keep = even_rows if parity == 0 else jnp.logical_not(even_rows)
    return jnp.where(keep, vt128, jnp.ones_like(vt128))


def _head_out_t(acc_even, acc_odd, even_rows):
    inv_e = 1.0 / acc_even[HEAD_DIM:HEAD_DIM + 1, :]
    inv_o = 1.0 / acc_odd[0:1, :]
    return jnp.where(even_rows, acc_even * inv_e, acc_odd * inv_o)


def _fox_body(qx_ref, kx_ref, vt_ref, g_ref, o_ref, s_scr, p_scr, m_scr, alpha_scr, acc_scr, ot_scr, *, tq):
    qi = pl.program_id(1)
    even_rows = _pair_rows()
    m_scr[...] = jnp.full(m_scr.shape, NEG_INF, F32)
    acc_scr[...] = jnp.zeros(acc_scr.shape, F32)
    key = lax.broadcasted_iota(jnp.int32, (tq, tq), 0)
    qry = lax.broadcasted_iota(jnp.int32, (tq, tq), 1)
    causal = key <= qry

    def tile(j, masked):
        start = pl.multiple_of(j * tq, tq)
        for h in range(N_HEADS):
            head = slice(h * LANES, (h + 1) * LANES)
            s_scr[h] = lax.dot_general(kx_ref[pl.ds(start, tq), head], qx_ref[:, head], _NT,
                                       preferred_element_type=F32)
        for h in range(N_HEADS):
            st = s_scr[h]
            if masked:
                st = jnp.where(causal, st, NEG_INF)
            m_old = m_scr[h:h + 1, :]
            m_new = jnp.maximum(m_old, jnp.max(st, axis=0, keepdims=True))
            p_scr[h] = jnp.exp2(st - m_new).astype(BF16)
            alpha_scr[h:h + 1, :] = jnp.exp2(m_old - m_new)
            m_scr[h:h + 1, :] = m_new
        for h in range(N_HEADS):
            pair = slice((h // 2) * LANES, (h // 2 + 1) * LANES)
            vt = _head_vt(vt_ref[pair, pl.ds(start, tq)], even_rows, h % 2)
            acc_scr[h] = acc_scr[h] * alpha_scr[h:h + 1, :] + jnp.dot(vt, p_scr[h], preferred_element_type=F32)

    def body(j, carry):
        tile(j, False)
        return carry

    lax.fori_loop(0, qi, body, None)
    tile(qi, True)
    for hp in range(N_PAIRS):
        ot_scr[hp * LANES:(hp + 1) * LANES, :] = _head_out_t(acc_scr[2 * hp], acc_scr[2 * hp + 1], even_rows)
    o_ref[...] = _rms(ot_scr[...].T, g_ref[...]).astype(BF16)


def _fox_prompt(qx, kx, vat, g_out, *, tq):
    b, s, wx = qx.shape
    w = vat.shape[1]
    return pl.pallas_call(
        functools.partial(_fox_body, tq=tq),
        grid=(b, s // tq),
        in_specs=[pl.BlockSpec((None, tq, wx), lambda i, j: (i, j, 0)),
                  pl.BlockSpec((None, s, wx), lambda i, j: (i, 0, 0)),
                  pl.BlockSpec((None, w, s), lambda i, j: (i, 0, 0)),
                  pl.BlockSpec((1, w), lambda i, j: (0, 0))],
        out_specs=pl.BlockSpec((None, tq, w), lambda i, j: (i, j, 0)),
        out_shape=jax.ShapeDtypeStruct((b, s, w), BF16),
        scratch_shapes=[pltpu.VMEM((N_HEADS, tq, tq), F32), pltpu.VMEM((N_HEADS, tq, tq), BF16),
                        pltpu.VMEM((N_HEADS, tq), F32), pltpu.VMEM((N_HEADS, tq), F32),
                        pltpu.VMEM((N_HEADS, LANES, tq), F32), pltpu.VMEM((w, tq), F32)],
        compiler_params=pltpu.CompilerParams(
            dimension_semantics=("parallel", "arbitrary"), vmem_limit_bytes=VMEM_LIMIT),
        name="fox_prompt",
    )(qx, kx, vat, g_out)


BAND_CHUNKS = 4
BAND_Q = BAND_CHUNKS * CHUNK
BAND_K = (LEFT_CHUNKS + BAND_CHUNKS) * CHUNK
BIAS_ROW = BAND_K + BAND_Q
BAND_K_SAMPLE = (LEFT + CHUNK + LANES - 1) // LANES * LANES


def _prep_band_bias_row(rel_table):
    pivot = LEFT + BAND_Q
    n_hi = pivot - REL_CLIP + 1
    n_mid = min(2 * REL_CLIP, BIAS_ROW - n_hi)
    n_lo = BIAS_ROW - n_hi - n_mid
    parts = [jnp.broadcast_to(rel_table[2 * REL_CLIP:], (n_hi, N_HEADS)),
             rel_table[2 * REL_CLIP - 1::-1][:n_mid],
             jnp.broadcast_to(rel_table[:1], (n_lo, N_HEADS))]
    return jnp.concatenate(parts, axis=0).T.reshape(N_HEADS, 1, BIAS_ROW)


def _band_bias_body(row_ref, bt_ref, bs_ref):
    rows = jnp.broadcast_to(row_ref[...], (BAND_Q, BIAS_ROW))
    skew = pltpu.roll(rows, 0, axis=1, stride=1, stride_axis=0)
    bias = skew[:, BAND_Q:]
    bs_ref[...] = bias[:CHUNK, :BAND_K_SAMPLE]
    qc = lax.broadcasted_iota(jnp.int32, (BAND_Q, BAND_K), 0) // CHUNK
    kc = lax.broadcasted_iota(jnp.int32, (BAND_Q, BAND_K), 1) // CHUNK
    bt_ref[...] = jnp.where((kc >= qc) & (kc <= qc + LEFT_CHUNKS), bias * LOG2E, NEG_INF).T


def _band_bias(bias_row):
    return pl.pallas_call(
        _band_bias_body,
        grid=(N_HEADS,),
        in_specs=[pl.BlockSpec((None, 1, BIAS_ROW), lambda h: (h, 0, 0))],
        out_specs=[pl.BlockSpec((None, BAND_K, BAND_Q), lambda h: (h, 0, 0)),
                   pl.BlockSpec((None, CHUNK, BAND_K_SAMPLE), lambda h: (h, 0, 0))],
        out_shape=[jax.ShapeDtypeStruct((N_HEADS, BAND_K, BAND_Q), F32),
                   jax.ShapeDtypeStruct((N_HEADS, CHUNK, BAND_K_SAMPLE), F32)],
        name="band_bias",
    )(bias_row)


def _band_body(qx_ref, kx_ref, vt_ref, bias_ref, g_ref, o_ref, kpad, vtpad, s_scr, p_scr, ot_scr, *, s_len):
    step = pl.program_id(1)

    @pl.when(step == 0)
    def _():
        lane = lax.broadcasted_iota(jnp.int32, (LEFT, 2 * LANES), 1)
        flags = (lane == _extra_lane(0)) | (lane == LANES + _extra_lane(1))
        pad_pair = jnp.where(flags, NEG_INF, 0.0).astype(BF16)
        for hp in range(N_PAIRS):
            kpad[0:LEFT, 2 * hp * LANES:2 * (hp + 1) * LANES] = pad_pair
        vtpad[:, 0:LEFT] = jnp.zeros((W_GROUP, LEFT), BF16)
        kpad[LEFT:LEFT + s_len, :] = kx_ref[...]
        vtpad[:, LEFT:LEFT + s_len] = vt_ref[...]

    even_rows = _pair_rows()
    start = pl.multiple_of(step * BAND_Q, BAND_Q)
    for h in range(N_HEADS):
        head = slice(h * LANES, (h + 1) * LANES)
        s_scr[h] = lax.dot_general(kpad[pl.ds(start, BAND_K), head], qx_ref[:, head], _NT,
                                   preferred_element_type=F32)
    for h in range(N_HEADS):
        st = s_scr[h] + bias_ref[h]
        p_scr[h] = jnp.exp2(st - jnp.max(st, axis=0, keepdims=True)).astype(BF16)
    for hp in range(N_PAIRS):
        pair = slice(hp * LANES, (hp + 1) * LANES)
        vtwin = vtpad[pair, pl.ds(start, BAND_K)]
        accs = [jnp.dot(_head_vt(vtwin, even_rows, parity), p_scr[2 * hp + parity], preferred_element_type=F32)
                for parity in range(2)]
        ot_scr[pair, :] = _head_out_t(accs[0], accs[1], even_rows)
    o_ref[...] = _rms(ot_scr[...].T, g_ref[...]).astype(BF16)


def _band_prompt(qxb, kxb, vbt, bias_t, g_out):
    b, s, wx = qxb.shape
    w = vbt.shape[1]
    return pl.pallas_call(
        functools.partial(_band_body, s_len=s),
        grid=(b, s // BAND_Q),
        in_specs=[pl.BlockSpec((None, BAND_Q, wx), lambda i, j: (i, j, 0)),
                  pl.BlockSpec((None, s, wx), lambda i, j: (i, 0, 0)),
                  pl.BlockSpec((None, w, s), lambda i, j: (i, 0, 0)),
                  pl.BlockSpec(bias_t.shape, lambda i, j: (0, 0, 0)),
                  pl.BlockSpec((1, w), lambda i, j: (0, 0))],
        out_specs=pl.BlockSpec((None, BAND_Q, w), lambda i, j: (i, j, 0)),
        out_shape=jax.ShapeDtypeStruct((b, s, w), BF16),
        scratch_shapes=[pltpu.VMEM((LEFT + s, wx), BF16), pltpu.VMEM((w, LEFT + s), BF16),
                        pltpu.VMEM((N_HEADS, BAND_K, BAND_Q), F32), pltpu.VMEM((N_HEADS, BAND_K, BAND_Q), BF16),
                        pltpu.VMEM((w, BAND_Q), F32)],
        compiler_params=pltpu.CompilerParams(
            dimension_semantics=("parallel", "arbitrary"), vmem_limit_bytes=VMEM_LIMIT),
        name="band_prompt",
    )(qxb, kxb, vbt, bias_t, g_out)


def _row_to_col(row):
    n = row.shape[-1]
    r = lax.broadcasted_iota(jnp.int32, (n, n), 0)
    c = lax.broadcasted_iota(jnp.int32, (n, n), 1)
    return jnp.sum(jnp.where(r == c, jnp.broadcast_to(row, (n, n)), 0.0), axis=-1, keepdims=True)


def _fox_sample_body(q_ref, kn_ref, vn_ref, lft_ref, kc_ref, vc_ref, clft_ref, g_ref, o_ref,
                     cct_scr, cn_scr, m_scr, acc_scr, o_scr, *, t_new, pt):
    p_idx = pl.program_id(1)
    n_p = pl.num_programs(1)
    even = _pair_masks()

    @pl.when(p_idx == 0)
    def _():
        cct = _lane_cumsum(clft_ref[...])
        cct_scr[...] = cct
        cn_scr[...] = _lane_cumsum(lft_ref[...]) + cct[:, cct.shape[1] - 1:]
        m_scr[...] = jnp.full(m_scr.shape, NEG_INF, F32)
        acc_scr[...] = jnp.zeros(acc_scr.shape, F32)

    def update(h, s, v):
        m = m_scr[h]
        m_new = jnp.maximum(m, jnp.max(s, axis=-1, keepdims=True))
        p = jnp.exp(s - m_new).astype(BF16)
        acc_scr[h] = acc_scr[h] * jnp.exp(m - m_new) + jnp.dot(p, v, preferred_element_type=F32)
        m_scr[h] = m_new

    start = pl.multiple_of(p_idx * pt, pt)
    for hp in range(N_PAIRS):
        lanes = slice(hp * LANES, (hp + 1) * LANES)
        q128 = q_ref[:, lanes]
        kc = kc_ref[:, lanes].astype(BF16)
        vc = vc_ref[:, lanes].astype(BF16)
        for parity in range(2):
            h = 2 * hp + parity
            cq = _row_to_col(cn_scr[h:h + 1, :])
            s = lax.dot_general(_head_q(q128, even, parity), kc, _NT, preferred_element_type=F32)
            update(h, s + cq - cct_scr[h:h + 1, pl.ds(start, pt)], _head_v(vc, even, parity))

    @pl.when(p_idx == n_p - 1)
    def _():
        row = lax.broadcasted_iota(jnp.int32, (t_new, t_new), 0)
        col = lax.broadcasted_iota(jnp.int32, (t_new, t_new), 1)
        for hp in range(N_PAIRS):
            lanes = slice(hp * LANES, (hp + 1) * LANES)
            q128 = q_ref[:, lanes]
            kn = kn_ref[:, lanes]
            vn = vn_ref[:, lanes]
            for parity in range(2):
                h = 2 * hp + parity
                cn_row = cn_scr[h:h + 1, :]
                s = lax.dot_general(_head_q(q128, even, parity), kn, _NT, preferred_element_type=F32)
                s = jnp.where(col <= row, s + _row_to_col(cn_row) - cn_row, NEG_INF)
                update(h, s, _head_v(vn, even, parity))
            o_scr[:, lanes] = _head_out(acc_scr[2 * hp], acc_scr[2 * hp + 1], even)
        o_ref[...] = _rms(o_scr[...], g_ref[...]).astype(BF16)


def _fox_sample(q, kn, vn, lft, kc, vc, clft, g_out, *, pt):
    b, t, w = q.shape
    p_len = kc.shape[1]
    new = lambda: pl.BlockSpec((None, t, w), lambda i, j: (i, 0, 0))
    cache = lambda: pl.BlockSpec((None, pt, w), lambda i, j: (i, j, 0))
    return pl.pallas_call(
        functools.partial(_fox_sample_body, t_new=t, pt=pt),
        grid=(b, p_len // pt),
        in_specs=[new(), new(), new(),
                  pl.BlockSpec((None, N_HEADS, t), lambda i, j: (i, 0, 0)),
                  cache(), cache(),
                  pl.BlockSpec((None, N_HEADS, p_len), lambda i, j: (i, 0, 0)),
                  pl.BlockSpec((1, w), lambda i, j: (0, 0))],
        out_specs=new(),
        out_shape=jax.ShapeDtypeStruct((b, t, w), BF16),
        scratch_shapes=[pltpu.VMEM((N_HEADS, p_len), F32), pltpu.VMEM((N_HEADS, t), F32),
                        pltpu.VMEM((N_HEADS, t, 1), F32), pltpu.VMEM((N_HEADS, t, LANES), F32),
                        pltpu.VMEM((t, w), F32)],
        compiler_params=pltpu.CompilerParams(
            dimension_semantics=("parallel", "arbitrary"), vmem_limit_bytes=VMEM_LIMIT),
        name="fox_sample",
    )(q, kn, vn, lft, kc, vc, clft, g_out)


def _band_sample_body(q_ref, kn_ref, vn_ref, knf_ref, vnf_ref, kc_ref, vc_ref, bias_ref, g_ref,
                      o_ref, nk_ref, nv_ref, kcat, vcat, o_scr, *, t_new, bp):
    kcat[0:bp, :] = kc_ref[...].astype(BF16)
    vcat[0:bp, :] = vc_ref[...].astype(BF16)
    kcat[bp:bp + t_new, :] = kn_ref[...]
    vcat[bp:bp + t_new, :] = vn_ref[...]
    nk_ref[0:bp - t_new, :] = kc_ref[t_new:bp, :]
    nv_ref[0:bp - t_new, :] = vc_ref[t_new:bp, :]
    nk_ref[bp - t_new:bp, :] = knf_ref[...]
    nv_ref[bp - t_new:bp, :] = vnf_ref[...]
    even = _pair_masks()
    for hp in range(N_PAIRS):
        lanes = slice(hp * LANES, (hp + 1) * LANES)
        q128 = q_ref[:, lanes]
        k = kcat[:, lanes]
        v = vcat[:, lanes]
        accs = []
        for parity in range(2):
            h = 2 * hp + parity
            s = lax.dot_general(_head_q(q128, even, parity), k, _NT, preferred_element_type=F32)
            s = s + bias_ref[h, 0:t_new, 0:bp + t_new]
            p = jnp.exp(s - jnp.max(s, axis=-1, keepdims=True)).astype(BF16)
            accs.append(jnp.dot(p, _head_v(v, even, parity), preferred_element_type=F32))
        o_scr[:, lanes] = _head_out(accs[0], accs[1], even)
    o_ref[...] = _rms(o_scr[...], g_ref[...]).astype(BF16)


def _band_sample(q, kn, vn, knf, vnf, kc, vc, bias, g_out):
    b, t, w = q.shape
    bp = kc.shape[1]
    assert t == CHUNK and bp == LEFT
    new = lambda: pl.BlockSpec((None, t, w), lambda i: (i, 0, 0))
    buf = lambda: pl.BlockSpec((None, bp, w), lambda i: (i, 0, 0))
    return pl.pallas_call(
        functools.partial(_band_sample_body, t_new=t, bp=bp),
        grid=(b,),
        in_specs=[new(), new(), new(), new(), new(), buf(), buf(),
                  pl.BlockSpec(bias.shape, lambda i: (0, 0, 0)),
                  pl.BlockSpec((1, w), lambda i: (0, 0))],
        out_specs=[new(), buf(), buf()],
        out_shape=[jax.ShapeDtypeStruct((b, t, w), BF16), jax.ShapeDtypeStruct((b, bp, w), F32),
                   jax.ShapeDtypeStruct((b, bp, w), F32)],
        scratch_shapes=[pltpu.VMEM((bp + t, w), BF16), pltpu.VMEM((bp + t, w), BF16), pltpu.VMEM((t, w), F32)],
        compiler_params=pltpu.CompilerParams(dimension_semantics=("parallel",), vmem_limit_bytes=VMEM_LIMIT),
        name="band_sample",
    )(q, kn, vn, knf, vnf, kc, vc, bias, g_out)


N_HEADS_MEM = 4
HEAD_DIM_MEM = 128
W_MEM = N_HEADS_MEM * HEAD_DIM_MEM
MEM_SCALE = HEAD_DIM_MEM ** -0.5


def _memkv_body(m_ref, g_ref, w_ref, kf_ref, vf_ref, k_ref, v_ref):
    h = _rms(m_ref[...], g_ref[...]).astype(BF16)
    z = jnp.dot(h, w_ref[...], preferred_element_type=F32)
    kf_ref[...] = z[:, :W_MEM]
    vf_ref[...] = z[:, W_MEM:]
    k_ref[...] = z[:, :W_MEM].astype(BF16)
    v_ref[...] = z[:, W_MEM:].astype(BF16)


def _mem_kv(mem, g_mem, w_ckv):
    b, n, d = mem.shape
    blk = lambda: pl.BlockSpec((None, n, W_MEM), lambda i: (i, 0, 0))
    return pl.pallas_call(
        _memkv_body,
        grid=(b,),
        in_specs=[pl.BlockSpec((None, n, d), lambda i: (i, 0, 0)),
                  pl.BlockSpec((1, d), lambda i: (0, 0)),
                  pl.BlockSpec(w_ckv.shape, lambda i: (0, 0))],
        out_specs=[blk(), blk(), blk(), blk()],
        out_shape=[jax.ShapeDtypeStruct((b, n, W_MEM), F32)] * 2 + [jax.ShapeDtypeStruct((b, n, W_MEM), BF16)] * 2,
        compiler_params=pltpu.CompilerParams(dimension_semantics=("parallel",), vmem_limit_bytes=VMEM_LIMIT),
        name="mem_kv",
    )(mem, g_mem, w_ckv)


N_GROUPS = 4
EXPERTS_PER_GROUP = 8
N_EXPERTS = N_GROUPS * EXPERTS_PER_GROUP
ROUTE_L2 = N_GROUPS
ROUTE_ROWS = 8
R_EID0, R_EID1, R_RANK0, R_RANK1, R_GATE0, R_GATE1 = range(6)


def _lane_max(x, mask):
    return jnp.max(jnp.where(mask, x, -jnp.inf), axis=-1, keepdims=True)


def _first_lane(mask, lane):
    return jnp.min(jnp.where(mask, lane, LANES), axis=-1, keepdims=True)


def _route(logits, lane):
    is_l1 = lane < N_GROUPS
    m1 = _lane_max(logits, is_l1)
    grp = _first_lane(is_l1 & (logits == m1), lane)
    wg = 1.0 / jnp.sum(jnp.where(is_l1, jnp.exp(logits - m1), 0.0), axis=-1, keepdims=True)
    lo = ROUTE_L2 + grp * EXPERTS_PER_GROUP
    in_grp = (lane >= lo) & (lane < lo + EXPERTS_PER_GROUP)
    v0 = _lane_max(logits, in_grp)
    i0 = _first_lane(in_grp & (logits == v0), lane)
    rest = in_grp & (lane != i0)
    v1 = _lane_max(logits, rest)
    i1 = _first_lane(rest & (logits == v1), lane)
    e1 = jnp.exp(v1 - v0)
    den = 1.0 / (1.0 + e1)
    return i0, i1, wg * den, wg * e1 * den


def _post_body(x_ref, a_ref, b_ref, mk_ref, mv_ref, woa_ref, wob_ref, gc_ref, wcq_ref, wco_ref,
               gf_ref, wrt_ref, brt_ref, cnt_in_ref,
               x2_ref, h3_ref, route_ref, routet_ref, cnt_out_ref, cnt_scr, o_scr, *, tm, nsub):
    first = (pl.program_id(0) == 0) & (pl.program_id(1) == 0)

    @pl.when(first)
    def _():
        cnt_scr[...] = cnt_in_ref[...]

    x1 = (x_ref[...] + jnp.dot(a_ref[...], woa_ref[...], preferred_element_type=F32)
          + jnp.dot(b_ref[...], wob_ref[...], preferred_element_type=F32))
    h2 = _rms(x1, gc_ref[...]).astype(BF16)
    qc = (jnp.dot(h2, wcq_ref[...], preferred_element_type=F32) * MEM_SCALE).astype(BF16)
    rows = tm // nsub
    for sub in range(nsub):
        rs = slice(sub * rows, (sub + 1) * rows)
        for hm in range(N_HEADS_MEM):
            lanes = slice(hm * HEAD_DIM_MEM, (hm + 1) * HEAD_DIM_MEM)
            s = lax.dot_general(qc[rs, lanes], mk_ref[sub, :, lanes], _NT, preferred_element_type=F32)
            p = jnp.exp(s - jnp.max(s, axis=-1, keepdims=True))
            inv = 1.0 / jnp.sum(p, axis=-1, keepdims=True)
            o_scr[rs, lanes] = jnp.dot(p.astype(BF16), mv_ref[sub, :, lanes], preferred_element_type=F32) * inv
    x2 = x1 + jnp.dot(o_scr[...].astype(BF16), wco_ref[...], preferred_element_type=F32)
    x2_ref[...] = x2
    h3f = _rms(x2, gf_ref[...])
    h3_ref[...] = h3f.reshape(tm, 1, x2.shape[-1])
    h3 = h3f.astype(BF16)

    logits = jnp.dot(h3, wrt_ref[...], preferred_element_type=F32) + brt_ref[...]
    lane = lax.broadcasted_iota(jnp.int32, (tm, LANES), 1)
    i0, i1, g0, g1 = _route(logits, lane)
    e0 = i0 - ROUTE_L2
    e1 = i1 - ROUTE_L2
    hit0 = lane == e0
    hit1 = lane == e1
    onehot = jnp.where(hit0 | hit1, 1.0, 0.0)
    row = lax.broadcasted_iota(jnp.int32, (tm, tm), 0)
    col = lax.broadcasted_iota(jnp.int32, (tm, tm), 1)
    before = jnp.where(col < row, 1.0, 0.0).astype(BF16)
    seen = jnp.dot(before, onehot.astype(BF16), preferred_element_type=F32) + cnt_scr[...]
    rank0 = jnp.sum(jnp.where(hit0, seen, 0.0), axis=-1, keepdims=True)
    rank1 = jnp.sum(jnp.where(hit1, seen, 0.0), axis=-1, keepdims=True)
    cnt_scr[...] = cnt_scr[...] + jnp.sum(onehot, axis=0, keepdims=True)
    cnt_out_ref[...] = cnt_scr[...]

    rec = jnp.zeros((tm, LANES), F32)
    for idx, val in ((R_EID0, e0.astype(F32)), (R_EID1, e1.astype(F32)), (R_RANK0, rank0),
                     (R_RANK1, rank1), (R_GATE0, g0), (R_GATE1, g1)):
        rec = jnp.where(lane == idx, val, rec)
    route_ref[...] = rec[:, :ROUTE_ROWS]
    routet_ref[...] = rec.T[:ROUTE_ROWS, :]


def _post_block(x, a_n, b_n, mk, mv, weights, cnt_in, *, tm):
    b, s, d = x.shape
    if s >= tm:
        nsub, grid = 1, (b, s // tm)
        tok = lambda i, j: (i, j, 0)
        flat = lambda i, j: i * (s // tm) + j
    else:
        nsub = tm // s
        assert b % nsub == 0
        x, a_n, b_n = (t.reshape(b // nsub, tm, t.shape[-1]) for t in (x, a_n, b_n))
        grid = (b // nsub, 1)
        tok = lambda i, j: (i, 0, 0)
        flat = lambda i, j: i
    mem = lambda i, j: (i, 0, 0)
    const = lambda arr: pl.BlockSpec(arr.shape, lambda i, j: (0,) * arr.ndim)
    in_specs = [pl.BlockSpec((None, tm, d), tok),
                pl.BlockSpec((None, tm, W_GROUP), tok), pl.BlockSpec((None, tm, W_GROUP), tok),
                pl.BlockSpec((nsub, mk.shape[1], W_MEM), mem), pl.BlockSpec((nsub, mv.shape[1], W_MEM), mem)]
    in_specs += [const(w) for w in weights] + [const(cnt_in)]
    n = b * s
    out_shape = [jax.ShapeDtypeStruct((n, d), F32), jax.ShapeDtypeStruct((n, 1, d), F32),
                 jax.ShapeDtypeStruct((n, ROUTE_ROWS), F32), jax.ShapeDtypeStruct((ROUTE_ROWS, n), F32),
                 jax.ShapeDtypeStruct((1, LANES), F32)]
    out_specs = [pl.BlockSpec((tm, d), lambda i, j: (flat(i, j), 0)),
                 pl.BlockSpec((tm, 1, d), lambda i, j: (flat(i, j), 0, 0)),
                 pl.BlockSpec((tm, ROUTE_ROWS), lambda i, j: (flat(i, j), 0)),
                 pl.BlockSpec((ROUTE_ROWS, tm), lambda i, j: (0, flat(i, j))),
                 pl.BlockSpec((1, LANES), lambda i, j: (0, 0))]
    return pl.pallas_call(
        functools.partial(_post_body, tm=tm, nsub=nsub),
        grid=grid,
        in_specs=in_specs,
        out_specs=out_specs,
        out_shape=out_shape,
        scratch_shapes=[pltpu.VMEM((1, LANES), F32), pltpu.VMEM((tm, W_MEM), F32)],
        compiler_params=pltpu.CompilerParams(
            dimension_semantics=("arbitrary", "arbitrary"), vmem_limit_bytes=VMEM_LIMIT),
        name="post_block",
    )(x, a_n, b_n, mk, mv, *weights, cnt_in)


def _prep_post(w_out, g_cross, w_cq, w_co, g_ffn, w_r1, b_r1, w_r2, b_r2):
    pad = LANES - N_GROUPS - N_EXPERTS
    w_rt = jnp.pad(jnp.concatenate([w_r1, w_r2], axis=1), ((0, 0), (0, pad))).astype(BF16)
    b_rt = jnp.pad(jnp.concatenate([b_r1, b_r2]).reshape(1, -1), ((0, 0), (0, pad))).astype(F32)
    return [w_out[:W_GROUP].astype(BF16), w_out[W_GROUP:].astype(BF16), g_cross.reshape(1, -1),
            w_cq.astype(BF16), w_co.astype(BF16), g_ffn.reshape(1, -1), w_rt, b_rt]


D_EXPERT = 512
TOP_K = 2
ROW_TILE = 256


def _n_row_tiles(n_tokens):
    return (n_tokens * TOP_K + N_EXPERTS * (ROW_TILE - 1)) // ROW_TILE


def _plan_body(rt_ref, cnt_ref, dest_ref, offs_ref, te_ref):
    lane = lax.broadcasted_iota(jnp.int32, (ROUTE_ROWS, LANES), 1)
    cnt = jnp.broadcast_to(cnt_ref[...], (ROUTE_ROWS, LANES)).astype(jnp.int32)
    padded = (cnt + (ROW_TILE - 1)) & ~(ROW_TILE - 1)
    ends = _lane_cumsum(padded.astype(F32)).astype(jnp.int32)
    offs = ends - padded
    offs_ref[...] = offs[0:1, :]
    eid0 = rt_ref[R_EID0].astype(jnp.int32)
    eid1 = rt_ref[R_EID1].astype(jnp.int32)
    d0 = rt_ref[R_RANK0].astype(jnp.int32)
    d1 = rt_ref[R_RANK1].astype(jnp.int32)
    tile_start = lax.broadcasted_iota(jnp.int32, te_ref.shape, 1) * ROW_TILE
    te = jnp.zeros(te_ref.shape, jnp.int32)
    for e in range(N_EXPERTS):
        off_e = jnp.sum(jnp.where(lane[0:1] == e, offs[0:1], 0), axis=-1, keepdims=True)
        end_e = jnp.sum(jnp.where(lane[0:1] == e, ends[0:1], 0), axis=-1, keepdims=True)
        d0 = d0 + jnp.where(eid0 == e, off_e, 0)
        d1 = d1 + jnp.where(eid1 == e, off_e, 0)
        te = te + jnp.where(end_e <= tile_start, 1, 0)
    dest_ref[0] = d0
    dest_ref[1] = d1
    te_ref[...] = jnp.minimum(te, N_EXPERTS - 1)


def _plan(route_t, cnt):
    n = route_t.shape[1]
    assert n % LANES == 0 and _n_row_tiles(n) <= 2 * LANES
    rt = route_t.reshape(ROUTE_ROWS, n // LANES, LANES)
    dest, offs, te = pl.pallas_call(
        _plan_body,
        out_shape=[jax.ShapeDtypeStruct((2, n // LANES, LANES), jnp.int32),
                   jax.ShapeDtypeStruct((1, LANES), jnp.int32),
                   jax.ShapeDtypeStruct((1, 2 * LANES), jnp.int32)],
        name="moe_plan",
    )(rt, cnt)
    return dest.reshape(2, n), offs.reshape(LANES), te.reshape(2 * LANES)


def _row_copy(src_ref, src_row, dst_ref, dst_row, sem):
    return pltpu.make_async_copy(src_ref.at[pl.ds(src_row, 1)], dst_ref.at[pl.ds(dst_row, 1)], sem)


WAIT_CHUNK = 64


def _wait_rows(copy, n):
    assert n % WAIT_CHUNK == 0

    def chunk(c, carry):
        for _ in range(WAIT_CHUNK):
            copy.wait()
        return carry

    lax.fori_loop(0, n // WAIT_CHUNK, chunk, None)


def _dispatch_body(offs_ref, d0_ref, d1_ref, hp_ref, hs_ref, xs_ref, zeros, sem, zsem, *, tm, n_prompt_tiles):
    i = pl.program_id(0)
    n_tiles = xs_ref.shape[0] // ROW_TILE

    @pl.when(i == 0)
    def _():
        zeros[...] = jnp.zeros(zeros.shape, zeros.dtype)
        zero_tile = lambda row: pltpu.make_async_copy(zeros, xs_ref.at[pl.ds(row, ROW_TILE)], zsem)
        n_used = offs_ref[N_EXPERTS] // ROW_TILE

        def tail(j, carry, op):
            op(zero_tile(j * ROW_TILE))
            return carry

        for op in (lambda c: c.start(), lambda c: c.wait()):
            for e in range(N_EXPERTS):
                @pl.when(offs_ref[e + 1] > offs_ref[e])
                def _():
                    op(zero_tile(offs_ref[e + 1] - ROW_TILE))
            lax.fori_loop(n_used, n_tiles, functools.partial(tail, op=op), None)

    def copy_tile(src_ref):
        def start(t, carry):
            _row_copy(src_ref, t, xs_ref, d0_ref[0, t], sem).start()
            _row_copy(src_ref, t, xs_ref, d1_ref[0, t], sem).start()
            return carry

        lax.fori_loop(0, tm, start, None, unroll=8)
        _wait_rows(_row_copy(src_ref, 0, xs_ref, 0, sem), TOP_K * tm)

    @pl.when(i < n_prompt_tiles)
    def _():
        copy_tile(hp_ref)

    @pl.when(i >= n_prompt_tiles)
    def _():
        copy_tile(hs_ref)


def _dispatch(h3_prompt, h3_sample, dest, offs, *, tm):
    n_p, n_s, d = h3_prompt.shape[0], h3_sample.shape[0], h3_prompt.shape[-1]
    assert n_p % tm == 0 and n_s % tm == 0
    n = n_p + n_s
    nt = n // tm
    d0 = dest[0].reshape(nt, 1, tm)
    d1 = dest[1].reshape(nt, 1, tm)
    n_rows = _n_row_tiles(n) * ROW_TILE
    npt = n_p // tm
    smem_tile = lambda: pl.BlockSpec((None, 1, tm), lambda i, offs: (i, 0, 0), memory_space=pltpu.SMEM)
    return pl.pallas_call(
        functools.partial(_dispatch_body, tm=tm, n_prompt_tiles=npt),
        grid_spec=pltpu.PrefetchScalarGridSpec(
            num_scalar_prefetch=1,
            grid=(nt,),
            in_specs=[smem_tile(), smem_tile(),
                      pl.BlockSpec((tm, 1, d), lambda i, offs: (jnp.minimum(i, npt - 1), 0, 0)),
                      pl.BlockSpec((tm, 1, d), lambda i, offs: (jnp.maximum(i - npt, 0), 0, 0))],
            out_specs=pl.BlockSpec(memory_space=pl.ANY),
            scratch_shapes=[pltpu.VMEM((ROW_TILE, 1, d), F32),
                            pltpu.SemaphoreType.DMA(()), pltpu.SemaphoreType.DMA(())]),
        out_shape=jax.ShapeDtypeStruct((n_rows, 1, d), F32),
        compiler_params=pltpu.CompilerParams(dimension_semantics=("arbitrary",), vmem_limit_bytes=VMEM_LIMIT),
        name="moe_dispatch",
    )(offs, d0, d1, h3_prompt, h3_sample)


def _experts_body(te_ref, offs_ref, xs_ref, wg_ref, wu_ref, wd_ref, ys_ref, wg_bf, wu_bf, wd_bf):
    i = pl.program_id(0)
    n_used = offs_ref[N_EXPERTS] // ROW_TILE

    @pl.when(i < n_used)
    def _():
        @pl.when((i == 0) | (te_ref[i] != te_ref[jnp.maximum(i - 1, 0)]))
        def _():
            wg_bf[...] = wg_ref[...].astype(BF16)
            wu_bf[...] = wu_ref[...].astype(BF16)
            wd_bf[...] = wd_ref[...].astype(BF16)

        x = xs_ref[...].reshape(ROW_TILE, D_MODEL).astype(BF16)
        dot = functools.partial(jnp.dot, preferred_element_type=F32)
        gate = dot(x, wg_bf[...])
        up = dot(x, wu_bf[...])
        act = (gate * jax.nn.sigmoid(gate) * up).astype(BF16)
        ys_ref[...] = dot(act, wd_bf[...]).reshape(ROW_TILE, 1, D_MODEL)

    @pl.when(i >= n_used)
    def _():
        ys_ref[...] = jnp.zeros(ys_ref.shape, ys_ref.dtype)


def _experts(xs, te, offs, w_gate, w_up, w_down):
    n_rows = xs.shape[0]
    last = lambda i, te, offs: jnp.minimum(i, offs[N_EXPERTS] // ROW_TILE - 1)
    wspec = lambda shape: pl.BlockSpec((None,) + shape, lambda i, te, offs: (te[last(i, te, offs)], 0, 0))
    return pl.pallas_call(
        _experts_body,
        grid_spec=pltpu.PrefetchScalarGridSpec(
            num_scalar_prefetch=2,
            grid=(n_rows // ROW_TILE,),
            in_specs=[pl.BlockSpec((ROW_TILE, 1, D_MODEL), lambda i, te, offs: (last(i, te, offs), 0, 0)),
                      wspec((D_MODEL, D_EXPERT)), wspec((D_MODEL, D_EXPERT)), wspec((D_EXPERT, D_MODEL))],
            out_specs=pl.BlockSpec((ROW_TILE, 1, D_MODEL), lambda i, te, offs: (i, 0, 0)),
            scratch_shapes=[pltpu.VMEM((D_MODEL, D_EXPERT), BF16), pltpu.VMEM((D_MODEL, D_EXPERT), BF16),
                            pltpu.VMEM((D_EXPERT, D_MODEL), BF16)]),
        out_shape=jax.ShapeDtypeStruct((n_rows, 1, D_MODEL), F32),
        compiler_params=pltpu.CompilerParams(dimension_semantics=("arbitrary",), vmem_limit_bytes=VMEM_LIMIT),
        name="moe_experts",
    )(te, offs, xs, w_gate, w_up, w_down)


def _combine_body(d0_ref, d1_ref, x2p_ref, x2s_ref, route_ref, ys_ref, g_ref, yp_ref, ysm_ref, rows0, rows1, sem,
                  *, tm, n_prompt_tiles):
    i = pl.program_id(0)

    def start(t, carry):
        _row_copy(ys_ref, d0_ref[0, t], rows0, t, sem).start()
        _row_copy(ys_ref, d1_ref[0, t], rows1, t, sem).start()
        return carry

    lax.fori_loop(0, tm, start, None, unroll=8)
    _wait_rows(_row_copy(ys_ref, 0, rows0, 0, sem), TOP_K * tm)
    g0 = route_ref[:, R_GATE0:R_GATE0 + 1]
    g1 = route_ref[:, R_GATE1:R_GATE1 + 1]
    d = x2p_ref.shape[-1]
    moe = g0 * rows0[...].reshape(tm, d) + g1 * rows1[...].reshape(tm, d)

    @pl.when(i < n_prompt_tiles)
    def _():
        yp_ref[...] = _rms(x2p_ref[...] + moe, g_ref[...])

    @pl.when(i >= n_prompt_tiles)
    def _():
        ysm_ref[...] = _rms(x2s_ref[...] + moe, g_ref[...])


def _combine(x2_prompt, x2_sample, route, dest, ys, g_final, *, tm):
    (n_p, d), n_s = x2_prompt.shape, x2_sample.shape[0]
    assert n_p % tm == 0 and n_s % tm == 0
    npt = n_p // tm
    nt = npt + n_s // tm
    d0 = dest[0].reshape(nt, 1, tm)
    d1 = dest[1].reshape(nt, 1, tm)
    smem_tile = lambda: pl.BlockSpec((None, 1, tm), lambda i: (i, 0, 0), memory_space=pltpu.SMEM)
    prompt_tile = lambda: pl.BlockSpec((tm, d), lambda i: (jnp.minimum(i, npt - 1), 0))
    sample_tile = lambda: pl.BlockSpec((tm, d), lambda i: (jnp.maximum(i - npt, 0), 0))
    return pl.pallas_call(
        functools.partial(_combine_body, tm=tm, n_prompt_tiles=npt),
        grid=(nt,),
        in_specs=[smem_tile(), smem_tile(), prompt_tile(), sample_tile(),
                  pl.BlockSpec((tm, ROUTE_ROWS), lambda i: (i, 0)),
                  pl.BlockSpec(memory_space=pl.ANY),
                  pl.BlockSpec((1, d), lambda i: (0, 0))],
        out_specs=[prompt_tile(), sample_tile()],
        out_shape=[jax.ShapeDtypeStruct((n_p, d), F32), jax.ShapeDtypeStruct((n_s, d), F32)],
        scratch_shapes=[pltpu.VMEM((tm, 1, d), F32), pltpu.VMEM((tm, 1, d), F32), pltpu.SemaphoreType.DMA(())],
        compiler_params=pltpu.CompilerParams(dimension_semantics=("arbitrary",), vmem_limit_bytes=VMEM_LIMIT),
        name="moe_combine",
    )(d0, d1, x2_prompt, x2_sample, route, ys, g_final)


TOKEN_TILE = 512
FOX_Q_TILE = 256
FOX_CACHE_TILE = 1024


def kernel(x_prompt, x_sample, cache_fox_k, cache_fox_v, cache_fox_logf, cache_band_k, cache_band_v, cache_mem_k, cache_mem_v, mem_prompt, g_mix, w_in, b_forget, g_out_fox, g_out_band, rel_table, w_out, g_cross, g_mem, w_cq, w_ck, w_cv, w_co, g_ffn, w_router1, b_router1, w_router2, b_router2, w_exp_gate, w_exp_up, w_exp_down, g_final):
    assert g_mix.shape[0] == 1, "single-layer model"
    bsz, seq, d = x_prompt.shape
    sb, st, _ = x_sample.shape
    n_s = sb * st
    past = cache_fox_k.shape[2]
    n_mem = mem_prompt.shape[1]
    row = lambda g: g.reshape(1, -1)

    w_pad, bf_pad, g_mix_r = _prep_proj(w_in[0], b_forget[0], g_mix[0])
    g_of, g_ob = row(g_out_fox[0]), row(g_out_band[0])
    bias_t, bias_s = _band_bias(_prep_band_bias_row(rel_table[0]))

    qx, kx, vat, qxb, kxb, vbt, kaf, vaf, kbf, vbf, logf = _proj(
        x_prompt, g_mix_r, w_pad, bf_pad, tm=TOKEN_TILE, prompt=True)
    a_p = _fox_prompt(qx, kx, vat, g_of, tq=FOX_Q_TILE)
    b_p = _band_prompt(qxb, kxb, vbt, bias_t, g_ob)

    s_out = _proj(x_sample.reshape(1, n_s, d), g_mix_r, w_pad, bf_pad, tm=n_s, prompt=False)
    sqa, ska, sva, sqb, skb, svb, skaf, svaf, skbf, svbf = (t.reshape(sb, st, W_GROUP) for t in s_out[:10])
    slogf = s_out[10].reshape(sb, st, N_HEADS)
    slft = s_out[11].reshape(N_HEADS, sb, st).transpose(1, 0, 2)
    a_s = _fox_sample(sqa, ska, sva, slft,
                      cache_fox_k[0].reshape(sb, past, W_GROUP), cache_fox_v[0].reshape(sb, past, W_GROUP),
                      cache_fox_logf[0].transpose(0, 2, 1), g_of, pt=FOX_CACHE_TILE)
    bp = cache_band_k.shape[2]
    b_s, nbk, nbv = _band_sample(sqb, skb, svb, skbf, svbf,
                                 cache_band_k[0].reshape(sb, bp, W_GROUP), cache_band_v[0].reshape(sb, bp, W_GROUP),
                                 bias_s, g_ob)

    w_ckv = jnp.concatenate([w_ck[0], w_cv[0]], axis=1).astype(BF16)
    mkf, mvf, mk, mv = _mem_kv(mem_prompt, row(g_mem[0]), w_ckv)
    post_w = _prep_post(w_out[0], g_cross[0], w_cq[0], w_co[0], g_ffn[0],
                        w_router1[0], b_router1[0], w_router2[0], b_router2[0])
    cnt0 = jnp.zeros((1, LANES), F32)
    x2_p, h3_p, route_p, routet_p, cnt1 = _post_block(x_prompt, a_p, b_p, mk, mv, post_w, cnt0, tm=TOKEN_TILE)
    cmk = cache_mem_k[0].reshape(sb, n_mem, W_MEM).astype(BF16)
    cmv = cache_mem_v[0].reshape(sb, n_mem, W_MEM).astype(BF16)
    x2_s, h3_s, route_s, routet_s, cnt2 = _post_block(x_sample, a_s, b_s, cmk, cmv, post_w, cnt1, tm=TOKEN_TILE)
    route = jnp.concatenate([route_p, route_s], axis=0)
    route_t = jnp.concatenate([routet_p, routet_s], axis=1)

    dest, offs, tile_expert = _plan(route_t, cnt2)
    xs = _dispatch(h3_p, h3_s, dest, offs, tm=TOKEN_TILE)
    ys = _experts(xs, tile_expert, offs, w_exp_gate[0], w_exp_up[0], w_exp_down[0])
    y_p, y_s = _combine(x2_p, x2_s, route, dest, ys, row(g_final), tm=TOKEN_TILE)

    heads = lambda t, n: t.reshape(1, n, -1, N_HEADS, HEAD_DIM)
    mem_heads = lambda t: t.reshape(1, bsz, n_mem, N_HEADS_MEM, HEAD_DIM_MEM)
    return (y_p.reshape(bsz, seq, d), y_s.reshape(sb, st, d),
            heads(kaf, bsz), heads(vaf, bsz), logf.reshape(1, bsz, seq, N_HEADS),
            heads(kbf, bsz), heads(vbf, bsz), mem_heads(mkf), mem_heads(mvf),
            heads(skaf, sb), heads(svaf, sb), slogf.reshape(1, sb, st, N_HEADS),
            heads(nbk, sb), heads(nbv, sb))
```

```python
import functools

import jax
import jax.numpy as jnp
from jax import lax
from jax.experimental import pallas as pl
from jax.experimental.pallas import tpu as pltpu

F32 = jnp.float32
BF16 = jnp.bfloat16

D_MODEL = 1024
HEAD_DIM = 64
N_HEADS = 8
W_GROUP = N_HEADS * HEAD_DIM
N_PAIRS = N_HEADS // 2
CHUNK = 64
LEFT_CHUNKS = 8
LEFT = LEFT_CHUNKS * CHUNK
REL_CLIP = 128
EPS = 1e-6
NEG_INF = -1e30
ATTN_SCALE = HEAD_DIM ** -0.5
LANES = 128
PROJ_PAD = 3 * W_GROUP * 2 + LANES
VMEM_LIMIT = 56 * 1024 * 1024


def _rms(x, g):
    ms = jnp.mean(x * x, axis=-1, keepdims=True)
    return x * lax.rsqrt(ms + EPS) * g


def _log_sigmoid(x):
    return -(jnp.maximum(-x, 0.0) + jnp.log1p(jnp.exp(-jnp.abs(x))))


def _lane_cumsum(x):
    n = x.shape[-1]
    lane = lax.broadcasted_iota(jnp.int32, x.shape, 1)
    k = 1
    while k < n:
        x = x + jnp.where(lane >= k, pltpu.roll(x, k, axis=1), 0.0)
        k *= 2
    return x


LOG2E = 1.4426950408889634
SCALE_BASE2 = ATTN_SCALE * LOG2E


def _split3(x):
    hi = x.astype(BF16).astype(F32)
    mid = (x - hi).astype(BF16).astype(F32)
    lo = x - hi - mid
    return hi, mid, lo


def _extra_lane(parity):
    return HEAD_DIM if parity == 0 else 0


def _fox_extras(c3t, hp, tm):
    row = lax.broadcasted_iota(jnp.int32, (8, tm), 0)

    def group(h, q_side):
        hi, mid, lo = (p[h:h + 1, :] for p in c3t)
        if q_side:
            return jnp.where(row < 3, 1.0, jnp.where(row == 3, hi, jnp.where(row == 4, mid, jnp.where(row == 5, lo, 0.0))))
        return jnp.where(row == 0, -hi, jnp.where(row == 1, -mid, jnp.where(row == 2, -lo, jnp.where(row < 6, 1.0, 0.0))))

    gap = jnp.zeros((HEAD_DIM - 8, tm), F32)
    sides = []
    for q_side in (True, False):
        t = jnp.concatenate([group(2 * hp + 1, q_side), gap, group(2 * hp, q_side), gap], axis=0)
        sides.append(t.T)
    return sides


def _head_blocks(x128, extras, lane):
    return (jnp.where(lane < HEAD_DIM, x128, extras).astype(BF16),
            jnp.where(lane >= HEAD_DIM, x128, extras).astype(BF16))


def _proj_common(x_ref, g_ref, w_ref, bf_ref, kaf_ref, vaf_ref, kbf_ref, vbf_ref, logf_ref, keep_tiles):
    s = pl.program_id(1)
    ns = pl.num_programs(1)
    h = _rms(x_ref[...], g_ref[...]).astype(BF16)
    z = jnp.dot(h, w_ref[...], preferred_element_type=F32)
    w = W_GROUP
    kaf_ref[...] = z[:, w:2 * w]
    vaf_ref[...] = z[:, 2 * w:3 * w]

    @pl.when(s >= ns - keep_tiles)
    def _():
        kbf_ref[...] = z[:, 4 * w:5 * w]
        vbf_ref[...] = z[:, 5 * w:6 * w]

    logf = _log_sigmoid(z[:, 6 * w:6 * w + LANES] + bf_ref[...])
    logf_ref[...] = logf[:, :N_HEADS]
    return z, logf


def _proj_prompt_body(x_ref, g_ref, w_ref, bf_ref, qx_ref, kx_ref, vat_ref, qxb_ref, kxb_ref, vbt_ref,
                      kaf_ref, vaf_ref, kbf_ref, vbf_ref, logf_ref, carry_ref, *, tm, keep_tiles):
    z, logf = _proj_common(x_ref, g_ref, w_ref, bf_ref, kaf_ref, vaf_ref, kbf_ref, vbf_ref, logf_ref, keep_tiles)
    w = W_GROUP
    vat_ref[...] = z[:, 2 * w:3 * w].T.astype(BF16)
    vbt_ref[...] = z[:, 5 * w:6 * w].T.astype(BF16)

    @pl.when(pl.program_id(1) == 0)
    def _():
        carry_ref[...] = jnp.zeros_like(carry_ref)

    ct = _lane_cumsum(logf.T[:N_HEADS, :]) + carry_ref[:, 0:1]
    carry_ref[...] = jnp.broadcast_to(ct[:, tm - 1:tm], carry_ref.shape)
    c3t = _split3(ct * LOG2E)
    lane = lax.broadcasted_iota(jnp.int32, (tm, LANES), 1)
    band_q_extras = jnp.where((lane == _extra_lane(0)) | (lane == _extra_lane(1)), 1.0, 0.0)
    band_k_extras = jnp.zeros((tm, LANES), F32)
    for hp in range(N_PAIRS):
        blocks = slice(2 * hp * LANES, 2 * (hp + 1) * LANES)
        blk = lambda base, hp=hp: z[:, base * w + hp * LANES:base * w + (hp + 1) * LANES]
        q_extras, k_extras = _fox_extras(c3t, hp, tm)
        qx_ref[:, blocks] = jnp.concatenate(_head_blocks(blk(0) * SCALE_BASE2, q_extras, lane), axis=1)
        kx_ref[:, blocks] = jnp.concatenate(_head_blocks(blk(1), k_extras, lane), axis=1)
        qxb_ref[:, blocks] = jnp.concatenate(_head_blocks(blk(3) * SCALE_BASE2, band_q_extras, lane), axis=1)
        kxb_ref[:, blocks] = jnp.concatenate(_head_blocks(blk(4), band_k_extras, lane), axis=1)


def _proj_sample_body(x_ref, g_ref, w_ref, bf_ref, qa_ref, ka_ref, va_ref, qb_ref, kb_ref, vb_ref,
                      kaf_ref, vaf_ref, kbf_ref, vbf_ref, logf_ref, lt_ref, *, keep_tiles):
    z, logf = _proj_common(x_ref, g_ref, w_ref, bf_ref, kaf_ref, vaf_ref, kbf_ref, vbf_ref, logf_ref, keep_tiles)
    w = W_GROUP
    qa_ref[...] = (z[:, 0:w] * ATTN_SCALE).astype(BF16)
    ka_ref[...] = z[:, w:2 * w].astype(BF16)
    va_ref[...] = z[:, 2 * w:3 * w].astype(BF16)
    qb_ref[...] = (z[:, 3 * w:4 * w] * ATTN_SCALE).astype(BF16)
    kb_ref[...] = z[:, 4 * w:5 * w].astype(BF16)
    vb_ref[...] = z[:, 5 * w:6 * w].astype(BF16)
    lt_ref[...] = logf.T[:N_HEADS, :]


def _proj(x, g_mix, w_pad, bf_pad, *, tm, prompt):
    b, s, d = x.shape
    ns = s // tm
    keep = min(LEFT, s)
    assert s % tm == 0 and keep % tm == 0
    keep_tiles = keep // tm
    row = pl.BlockSpec((None, tm, W_GROUP), lambda i, j: (i, j, 0))
    wide = pl.BlockSpec((None, tm, N_HEADS * LANES), lambda i, j: (i, j, 0))
    col = pl.BlockSpec((None, W_GROUP, tm), lambda i, j: (i, 0, j))
    keep_spec = pl.BlockSpec((None, tm, W_GROUP), lambda i, j: (i, jnp.maximum(j - (ns - keep_tiles), 0), 0))
    heads_row = pl.BlockSpec((None, tm, N_HEADS), lambda i, j: (i, j, 0))
    heads_col = pl.BlockSpec((None, N_HEADS, tm), lambda i, j: (i, 0, j))
    const = lambda shape: pl.BlockSpec(shape, lambda i, j: (0,) * len(shape))
    rows_bf = jax.ShapeDtypeStruct((b, s, W_GROUP), BF16)
    wide_bf = jax.ShapeDtypeStruct((b, s, N_HEADS * LANES), BF16)
    cols_bf = jax.ShapeDtypeStruct((b, W_GROUP, s), BF16)
    f32_tail = [jax.ShapeDtypeStruct((b, s, W_GROUP), F32)] * 2
    f32_tail += [jax.ShapeDtypeStruct((b, keep, W_GROUP), F32)] * 2
    f32_tail += [jax.ShapeDtypeStruct((b, s, N_HEADS), F32)]
    tail_specs = [row, row, keep_spec, keep_spec, heads_row]
    if prompt:
        body = functools.partial(_proj_prompt_body, tm=tm, keep_tiles=keep_tiles)
        out_shape = [wide_bf, wide_bf, cols_bf, wide_bf, wide_bf, cols_bf] + f32_tail
        out_specs = [wide, wide, col, wide, wide, col] + tail_specs
        scratch = [pltpu.VMEM((N_HEADS, LANES), F32)]
    else:
        body = functools.partial(_proj_sample_body, keep_tiles=keep_tiles)
        out_shape = [rows_bf] * 6 + f32_tail + [jax.ShapeDtypeStruct((b, N_HEADS, s), F32)]
        out_specs = [row] * 6 + tail_specs + [heads_col]
        scratch = []
    return pl.pallas_call(
        body,
        grid=(b, ns),
        in_specs=[pl.BlockSpec((None, tm, d), lambda i, j: (i, j, 0)),
                  const((1, d)), const(w_pad.shape), const((1, LANES))],
        out_specs=out_specs,
        out_shape=out_shape,
        scratch_shapes=scratch,
        compiler_params=pltpu.CompilerParams(
            dimension_semantics=("parallel", "arbitrary"), vmem_limit_bytes=VMEM_LIMIT),
        name="proj",
    )(x, g_mix, w_pad, bf_pad)


def _prep_proj(w_in, b_forget, g_mix):
    cols = w_in.shape[-1]
    w_pad = jnp.pad(w_in, ((0, 0), (0, PROJ_PAD - cols))).astype(BF16)
    bf_pad = jnp.pad(b_forget.reshape(1, -1), ((0, 0), (0, LANES - N_HEADS))).astype(F32)
    return w_pad, bf_pad, g_mix.reshape(1, -1)


def _pair_masks():
    lane = lax.broadcasted_iota(jnp.int32, (1, LANES), 1)
    return lane < HEAD_DIM


def _head_q(q128, even_lanes, parity):
    keep = even_lanes if parity == 0 else jnp.logical_not(even_lanes)
    return jnp.where(keep, q128, jnp.zeros_like(q128))


def _head_v(v128, even_lanes, parity):
    keep = even_lanes if parity == 0 else jnp.logical_not(even_lanes)
    return jnp.where(keep, v128, jnp.ones_like(v128))


def _head_out(acc_even, acc_odd, even_lanes):
    inv_e = 1.0 / acc_even[:, HEAD_DIM:HEAD_DIM + 1]
    inv_o = 1.0 / acc_odd[:, 0:1]
    return jnp.where(even_lanes, acc_even * inv_e, acc_odd * inv_o)


_NT = (((1,), (1,)), ((), ()))


def _pair_rows():
    row = lax.broadcasted_iota(jnp.int32, (LANES, 1), 0)
    return row < HEAD_DIM


def _head_vt(vt128, even_rows, parity):
    keep = even_rows if parity == 0 else jnp.logical_not(even_rows)
    return jnp.where(keep, vt128, jnp.ones_like(vt128))


def _head_out_t(acc_even, acc_odd, even_rows):
    inv_e = 1.0 / acc_even[HEAD_DIM:HEAD_DIM + 1, :]
    inv_o = 1.0 / acc_odd[0:1, :]
    return jnp.where(even_rows, acc_even * inv_e, acc_odd * inv_o)


def _fox_body(qx_ref, kx_ref, vt_ref, g_ref, o_ref, s_scr, p_scr, m_scr, alpha_scr, acc_scr, ot_scr, *, tq):
    qi = pl.program_id(1)
    even_rows = _pair_rows()
    m_scr[...] = jnp.full(m_scr.shape, NEG_INF, F32)
    acc_scr[...] = jnp.zeros(acc_scr.shape, F32)
    key = lax.broadcasted_iota(jnp.int32, (tq, tq), 0)
    qry = lax.broadcasted_iota(jnp.int32, (tq, tq), 1)
    causal = key <= qry

    def tile(j, masked):
        start = pl.multiple_of(j * tq, tq)
        for h in range(N_HEADS):
            head = slice(h * LANES, (h + 1) * LANES)
            s_scr[h] = lax.dot_general(kx_ref[pl.ds(start, tq), head], qx_ref[:, head], _NT,
                                       preferred_element_type=F32)
        for h in range(N_HEADS):
            st = s_scr[h]
            if masked:
                st = jnp.where(causal, st, NEG_INF)
            m_old = m_scr[h:h + 1, :]
            m_new = jnp.maximum(m_old, jnp.max(st, axis=0, keepdims=True))
            p_scr[h] = jnp.exp2(st - m_new).astype(BF16)
            alpha_scr[h:h + 1, :] = jnp.exp2(m_old - m_new)
            m_scr[h:h + 1, :] = m_new
        for h in range(N_HEADS):
            pair = slice((h // 2) * LANES, (h // 2 + 1) * LANES)
            vt = _head_vt(vt_ref[pair, pl.ds(start, tq)], even_rows, h % 2)
            acc_scr[h] = acc_scr[h] * alpha_scr[h:h + 1, :] + jnp.dot(vt, p_scr[h], preferred_element_type=F32)

    def body(j, carry):
        tile(j, False)
        return carry

    lax.fori_loop(0, qi, body, None)
    tile(qi, True)
    for hp in range(N_PAIRS):
        ot_scr[hp * LANES:(hp + 1) * LANES, :] = _head_out_t(acc_scr[2 * hp], acc_scr[2 * hp + 1], even_rows)
    o_ref[...] = _rms(ot_scr[...].T, g_ref[...]).astype(BF16)


def _fox_prompt(qx, kx, vat, g_out, *, tq):
    b, s, wx = qx.shape
    w = vat.shape[1]
    return pl.pallas_call(
        functools.partial(_fox_body, tq=tq),
        grid=(b, s // tq),
        in_specs=[pl.BlockSpec((None, tq, wx), lambda i, j: (i, j, 0)),
                  pl.BlockSpec((None, s, wx), lambda i, j: (i, 0, 0)),
                  pl.BlockSpec((None, w, s), lambda i, j: (i, 0, 0)),
                  pl.BlockSpec((1, w), lambda i, j: (0, 0))],
        out_specs=pl.BlockSpec((None, tq, w), lambda i, j: (i, j, 0)),
        out_shape=jax.ShapeDtypeStruct((b, s, w), BF16),
        scratch_shapes=[pltpu.VMEM((N_HEADS, tq, tq), F32), pltpu.VMEM((N_HEADS, tq, tq), BF16),
                        pltpu.VMEM((N_HEADS, tq), F32), pltpu.VMEM((N_HEADS, tq), F32),
                        pltpu.VMEM((N_HEADS, LANES, tq), F32), pltpu.VMEM((w, tq), F32)],
        compiler_params=pltpu.CompilerParams(
            dimension_semantics=("parallel", "arbitrary"), vmem_limit_bytes=VMEM_LIMIT),
        name="fox_prompt",
    )(qx, kx, vat, g_out)


BAND_CHUNKS = 4
BAND_Q = BAND_CHUNKS * CHUNK
BAND_K = (LEFT_CHUNKS + BAND_CHUNKS) * CHUNK
BIAS_ROW = BAND_K + BAND_Q
BAND_K_SAMPLE = (LEFT + CHUNK + LANES - 1) // LANES * LANES


def _prep_band_bias_row(rel_table):
    pivot = LEFT + BAND_Q
    n_hi = pivot - REL_CLIP + 1
    n_mid = min(2 * REL_CLIP, BIAS_ROW - n_hi)
    n_lo = BIAS_ROW - n_hi - n_mid
    parts = [jnp.broadcast_to(rel_table[2 * REL_CLIP:], (n_hi, N_HEADS)),
             rel_table[2 * REL_CLIP - 1::-1][:n_mid],
             jnp.broadcast_to(rel_table[:1], (n_lo, N_HEADS))]
    return jnp.concatenate(parts, axis=0).T.reshape(N_HEADS, 1, BIAS_ROW)


def _band_bias_body(row_ref, bt_ref, bs_ref):
    rows = jnp.broadcast_to(row_ref[...], (BAND_Q, BIAS_ROW))
    skew = pltpu.roll(rows, 0, axis=1, stride=1, stride_axis=0)
    bias = skew[:, BAND_Q:]
    bs_ref[...] = bias[:CHUNK, :BAND_K_SAMPLE]
    qc = lax.broadcasted_iota(jnp.int32, (BAND_Q, BAND_K), 0) // CHUNK
    kc = lax.broadcasted_iota(jnp.int32, (BAND_Q, BAND_K), 1) // CHUNK
    bt_ref[...] = jnp.where((kc >= qc) & (kc <= qc + LEFT_CHUNKS), bias * LOG2E, NEG_INF).T


def _band_bias(bias_row):
    return pl.pallas_call(
        _band_bias_body,
        grid=(N_HEADS,),
        in_specs=[pl.BlockSpec((None, 1, BIAS_ROW), lambda h: (h, 0, 0))],
        out_specs=[pl.BlockSpec((None, BAND_K, BAND_Q), lambda h: (h, 0, 0)),
                   pl.BlockSpec((None, CHUNK, BAND_K_SAMPLE), lambda h: (h, 0, 0))],
        out_shape=[jax.ShapeDtypeStruct((N_HEADS, BAND_K, BAND_Q), F32),
                   jax.ShapeDtypeStruct((N_HEADS, CHUNK, BAND_K_SAMPLE), F32)],
        name="band_bias",
    )(bias_row)


def _band_body(qx_ref, kx_ref, vt_ref, bias_ref, g_ref, o_ref, kpad, vtpad, s_scr, p_scr, ot_scr, *, s_len):
    step = pl.program_id(1)

    @pl.when(step == 0)
    def _():
        lane = lax.broadcasted_iota(jnp.int32, (LEFT, 2 * LANES), 1)
        flags = (lane == _extra_lane(0)) | (lane == LANES + _extra_lane(1))
        pad_pair = jnp.where(flags, NEG_INF, 0.0).astype(BF16)
        for hp in range(N_PAIRS):
            kpad[0:LEFT, 2 * hp * LANES:2 * (hp + 1) * LANES] = pad_pair
        vtpad[:, 0:LEFT] = jnp.zeros((W_GROUP, LEFT), BF16)
        kpad[LEFT:LEFT + s_len, :] = kx_ref[...]
        vtpad[:, LEFT:LEFT + s_len] = vt_ref[...]

    even_rows = _pair_rows()
    start = pl.multiple_of(step * BAND_Q, BAND_Q)
    for h in range(N_HEADS):
        head = slice(h * LANES, (h + 1) * LANES)
        s_scr[h] = lax.dot_general(kpad[pl.ds(start, BAND_K), head], qx_ref[:, head], _NT,
                                   preferred_element_type=F32)
    for h in range(N_HEADS):
        st = s_scr[h] + bias_ref[h]
        p_scr[h] = jnp.exp2(st - jnp.max(st, axis=0, keepdims=True)).astype(BF16)
    for hp in range(N_PAIRS):
        pair = slice(hp * LANES, (hp + 1) * LANES)
        vtwin = vtpad[pair, pl.ds(start, BAND_K)]
        accs = [jnp.dot(_head_vt(vtwin, even_rows, parity), p_scr[2 * hp + parity], preferred_element_type=F32)
                for parity in range(2)]
        ot_scr[pair, :] = _head_out_t(accs[0], accs[1], even_rows)
    o_ref[...] = _rms(ot_scr[...].T, g_ref[...]).astype(BF16)


def _band_prompt(qxb, kxb, vbt, bias_t, g_out):
    b, s, wx = qxb.shape
    w = vbt.shape[1]
    return pl.pallas_call(
        functools.partial(_band_body, s_len=s),
        grid=(b, s // BAND_Q),
        in_specs=[pl.BlockSpec((None, BAND_Q, wx), lambda i, j: (i, j, 0)),
                  pl.BlockSpec((None, s, wx), lambda i, j: (i, 0, 0)),
                  pl.BlockSpec((None, w, s), lambda i, j: (i, 0, 0)),
                  pl.BlockSpec(bias_t.shape, lambda i, j: (0, 0, 0)),
                  pl.BlockSpec((1, w), lambda i, j: (0, 0))],
        out_specs=pl.BlockSpec((None, BAND_Q, w), lambda i, j: (i, j, 0)),
        out_shape=jax.ShapeDtypeStruct((b, s, w), BF16),
        scratch_shapes=[pltpu.VMEM((LEFT + s, wx), BF16), pltpu.VMEM((w, LEFT + s), BF16),
                        pltpu.VMEM((N_HEADS, BAND_K, BAND_Q), F32), pltpu.VMEM((N_HEADS, BAND_K, BAND_Q), BF16),
                        pltpu.VMEM((w, BAND_Q), F32)],
        compiler_params=pltpu.CompilerParams(
            dimension_semantics=("parallel", "arbitrary"), vmem_limit_bytes=VMEM_LIMIT),
        name="band_prompt",
    )(qxb, kxb, vbt, bias_t, g_out)


def _row_to_col(row):
    n = row.shape[-1]
    r = lax.broadcasted_iota(jnp.int32, (n, n), 0)
    c = lax.broadcasted_iota(jnp.int32, (n, n), 1)
    return jnp.sum(jnp.where(r == c, jnp.broadcast_to(row, (n, n)), 0.0), axis=-1, keepdims=True)


def _fox_sample_body(q_ref, kn_ref, vn_ref, lft_ref, kc_ref, vc_ref, clft_ref, g_ref, o_ref,
                     cct_scr, cn_scr, m_scr, acc_scr, o_scr, *, t_new, pt):
    p_idx = pl.program_id(1)
    n_p = pl.num_programs(1)
    even = _pair_masks()

    @pl.when(p_idx == 0)
    def _():
        cct = _lane_cumsum(clft_ref[...])
        cct_scr[...] = cct
        cn_scr[...] = _lane_cumsum(lft_ref[...]) + cct[:, cct.shape[1] - 1:]
        m_scr[...] = jnp.full(m_scr.shape, NEG_INF, F32)
        acc_scr[...] = jnp.zeros(acc_scr.shape, F32)

    def update(h, s, v):
        m = m_scr[h]
        m_new = jnp.maximum(m, jnp.max(s, axis=-1, keepdims=True))
        p = jnp.exp(s - m_new).astype(BF16)
        acc_scr[h] = acc_scr[h] * jnp.exp(m - m_new) + jnp.dot(p, v, preferred_element_type=F32)
        m_scr[h] = m_new

    start = pl.multiple_of(p_idx * pt, pt)
    for hp in range(N_PAIRS):
        lanes = slice(hp * LANES, (hp + 1) * LANES)
        q128 = q_ref[:, lanes]
        kc = kc_ref[:, lanes].astype(BF16)
        vc = vc_ref[:, lanes].astype(BF16)
        for parity in range(2):
            h = 2 * hp + parity
            cq = _row_to_col(cn_scr[h:h + 1, :])
            s = lax.dot_general(_head_q(q128, even, parity), kc, _NT, preferred_element_type=F32)
            update(h, s + cq - cct_scr[h:h + 1, pl.ds(start, pt)], _head_v(vc, even, parity))

    @pl.when(p_idx == n_p - 1)
    def _():
        row = lax.broadcasted_iota(jnp.int32, (t_new, t_new), 0)
        col = lax.broadcasted_iota(jnp.int32, (t_new, t_new), 1)
        for hp in range(N_PAIRS):
            lanes = slice(hp * LANES, (hp + 1) * LANES)
            q128 = q_ref[:, lanes]
            kn = kn_ref[:, lanes]
            vn = vn_ref[:, lanes]
            for parity in range(2):
                h = 2 * hp + parity
                cn_row = cn_scr[h:h + 1, :]
                s = lax.dot_general(_head_q(q128, even, parity), kn, _NT, preferred_element_type=F32)
                s = jnp.where(col <= row, s + _row_to_col(cn_row) - cn_row, NEG_INF)
                update(h, s, _head_v(vn, even, parity))
            o_scr[:, lanes] = _head_out(acc_scr[2 * hp], acc_scr[2 * hp + 1], even)
        o_ref[...] = _rms(o_scr[...], g_ref[...]).astype(BF16)


def _fox_sample(q, kn, vn, lft, kc, vc, clft, g_out, *, pt):
    b, t, w = q.shape
    p_len = kc.shape[1]
    new = lambda: pl.BlockSpec((None, t, w), lambda i, j: (i, 0, 0))
    cache = lambda: pl.BlockSpec((None, pt, w), lambda i, j: (i, j, 0))
    return pl.pallas_call(
        functools.partial(_fox_sample_body, t_new=t, pt=pt),
        grid=(b, p_len // pt),
        in_specs=[new(), new(), new(),
                  pl.BlockSpec((None, N_HEADS, t), lambda i, j: (i, 0, 0)),
                  cache(), cache(),
                  pl.BlockSpec((None, N_HEADS, p_len), lambda i, j: (i, 0, 0)),
                  pl.BlockSpec((1, w), lambda i, j: (0, 0))],
        out_specs=new(),
        out_shape=jax.ShapeDtypeStruct((b, t, w), BF16),
        scratch_shapes=[pltpu.VMEM((N_HEADS, p_len), F32), pltpu.VMEM((N_HEADS, t), F32),
                        pltpu.VMEM((N_HEADS, t, 1), F32), pltpu.VMEM((N_HEADS, t, LANES), F32),
                        pltpu.VMEM((t, w), F32)],
        compiler_params=pltpu.CompilerParams(
            dimension_semantics=("parallel", "arbitrary"), vmem_limit_bytes=VMEM_LIMIT),
        name="fox_sample",
    )(q, kn, vn, lft, kc, vc, clft, g_out)


def _band_sample_body(q_ref, kn_ref, vn_ref, knf_ref, vnf_ref, kc_ref, vc_ref, bias_ref, g_ref,
                      o_ref, nk_ref, nv_ref, kcat, vcat, o_scr, *, t_new, bp):
    kcat[0:bp, :] = kc_ref[...].astype(BF16)
    vcat[0:bp, :] = vc_ref[...].astype(BF16)
    kcat[bp:bp + t_new, :] = kn_ref[...]
    vcat[bp:bp + t_new, :] = vn_ref[...]
    nk_ref[0:bp - t_new, :] = kc_ref[t_new:bp, :]
    nv_ref[0:bp - t_new, :] = vc_ref[t_new:bp, :]
    nk_ref[bp - t_new:bp, :] = knf_ref[...]
    nv_ref[bp - t_new:bp, :] = vnf_ref[...]
    even = _pair_masks()
    for hp in range(N_PAIRS):
        lanes = slice(hp * LANES, (hp + 1) * LANES)
        q128 = q_ref[:, lanes]
        k = kcat[:, lanes]
        v = vcat[:, lanes]
        accs = []
        for parity in range(2):
            h = 2 * hp + parity
            s = lax.dot_general(_head_q(q128, even, parity), k, _NT, preferred_element_type=F32)
            s = s + bias_ref[h, 0:t_new, 0:bp + t_new]
            p = jnp.exp(s - jnp.max(s, axis=-1, keepdims=True)).astype(BF16)
            accs.append(jnp.dot(p, _head_v(v, even, parity), preferred_element_type=F32))
        o_scr[:, lanes] = _head_out(accs[0], accs[1], even)
    o_ref[...] = _rms(o_scr[...], g_ref[...]).astype(BF16)


def _band_sample(q, kn, vn, knf, vnf, kc, vc, bias, g_out):
    b, t, w = q.shape
    bp = kc.shape[1]
    assert t == CHUNK and bp == LEFT
    new = lambda: pl.BlockSpec((None, t, w), lambda i: (i, 0, 0))
    buf = lambda: pl.BlockSpec((None, bp, w), lambda i: (i, 0, 0))
    return pl.pallas_call(
        functools.partial(_band_sample_body, t_new=t, bp=bp),
        grid=(b,),
        in_specs=[new(), new(), new(), new(), new(), buf(), buf(),
                  pl.BlockSpec(bias.shape, lambda i: (0, 0, 0)),
                  pl.BlockSpec((1, w), lambda i: (0, 0))],
        out_specs=[new(), buf(), buf()],
        out_shape=[jax.ShapeDtypeStruct((b, t, w), BF16), jax.ShapeDtypeStruct((b, bp, w), F32),
                   jax.ShapeDtypeStruct((b, bp, w), F32)],
        scratch_shapes=[pltpu.VMEM((bp + t, w), BF16), pltpu.VMEM((bp + t, w), BF16), pltpu.VMEM((t, w), F32)],
        compiler_params=pltpu.CompilerParams(dimension_semantics=("parallel",), vmem_limit_bytes=VMEM_LIMIT),
        name="band_sample",
    )(q, kn, vn, knf, vnf, kc, vc, bias, g_out)


N_HEADS_MEM = 4
HEAD_DIM_MEM = 128
W_MEM = N_HEADS_MEM * HEAD_DIM_MEM
MEM_SCALE = HEAD_DIM_MEM ** -0.5


def _memkv_body(m_ref, g_ref, w_ref, kf_ref, vf_ref, k_ref, v_ref):
    h = _rms(m_ref[...], g_ref[...]).astype(BF16)
    z = jnp.dot(h, w_ref[...], preferred_element_type=F32)
    kf_ref[...] = z[:, :W_MEM]
    vf_ref[...] = z[:, W_MEM:]
    k_ref[...] = z[:, :W_MEM].astype(BF16)
    v_ref[...] = z[:, W_MEM:].astype(BF16)


def _mem_kv(mem, g_mem, w_ckv):
    b, n, d = mem.shape
    blk = lambda: pl.BlockSpec((None, n, W_MEM), lambda i: (i, 0, 0))
    return pl.pallas_call(
        _memkv_body,
        grid=(b,),
        in_specs=[pl.BlockSpec((None, n, d), lambda i: (i, 0, 0)),
                  pl.BlockSpec((1, d), lambda i: (0, 0)),
                  pl.BlockSpec(w_ckv.shape, lambda i: (0, 0))],
        out_specs=[blk(), blk(), blk(), blk()],
        out_shape=[jax.ShapeDtypeStruct((b, n, W_MEM), F32)] * 2 + [jax.ShapeDtypeStruct((b, n, W_MEM), BF16)] * 2,
        compiler_params=pltpu.CompilerParams(dimension_semantics=("parallel",), vmem_limit_bytes=VMEM_LIMIT),
        name="mem_kv",
    )(mem, g_mem, w_ckv)


N_GROUPS = 4
EXPERTS_PER_GROUP = 8
N_EXPERTS = N_GROUPS * EXPERTS_PER_GROUP
ROUTE_L2 = N_GROUPS
ROUTE_ROWS = 8
R_EID0, R_EID1, R_RANK0, R_RANK1, R_GATE0, R_GATE1 = range(6)


def _lane_max(x, mask):
    return jnp.max(jnp.where(mask, x, -jnp.inf), axis=-1, keepdims=True)


def _first_lane(mask, lane):
    return jnp.min(jnp.where(mask, lane, LANES), axis=-1, keepdims=True)


def _route(logits, lane):
    is_l1 = lane < N_GROUPS
    m1 = _lane_max(logits, is_l1)
    grp = _first_lane(is_l1 & (logits == m1), lane)
    wg = 1.0 / jnp.sum(jnp.where(is_l1, jnp.exp(logits - m1), 0.0), axis=-1, keepdims=True)
    lo = ROUTE_L2 + grp * EXPERTS_PER_GROUP
    in_grp = (lane >= lo) & (lane < lo + EXPERTS_PER_GROUP)
    v0 = _lane_max(logits, in_grp)
    i0 = _first_lane(in_grp & (logits == v0), lane)
    rest = in_grp & (lane != i0)
    v1 = _lane_max(logits, rest)
    i1 = _first_lane(rest & (logits == v1), lane)
    e1 = jnp.exp(v1 - v0)
    den = 1.0 / (1.0 + e1)
    return i0, i1, wg * den, wg * e1 * den


def _post_body(x_ref, a_ref, b_ref, mk_ref, mv_ref, woa_ref, wob_ref, gc_ref, wcq_ref, wco_ref,
               gf_ref, wrt_ref, brt_ref,
               x2_ref, h3_ref, route_ref, routet_ref, cnt_ref, o_scr, *, tm, nsub):
    x1 = (x_ref[...] + jnp.dot(a_ref[...], woa_ref[...], preferred_element_type=F32)
          + jnp.dot(b_ref[...], wob_ref[...], preferred_element_type=F32))
    h2 = _rms(x1, gc_ref[...]).astype(BF16)
    qc = (jnp.dot(h2, wcq_ref[...], preferred_element_type=F32) * MEM_SCALE).astype(BF16)
    rows = tm // nsub
    for sub in range(nsub):
        rs = slice(sub * rows, (sub + 1) * rows)
        for hm in range(N_HEADS_MEM):
            lanes = slice(hm * HEAD_DIM_MEM, (hm + 1) * HEAD_DIM_MEM)
            s = lax.dot_general(qc[rs, lanes], mk_ref[sub, :, lanes], _NT, preferred_element_type=F32)
            p = jnp.exp(s - jnp.max(s, axis=-1, keepdims=True))
            inv = 1.0 / jnp.sum(p, axis=-1, keepdims=True)
            o_scr[rs, lanes] = jnp.dot(p.astype(BF16), mv_ref[sub, :, lanes], preferred_element_type=F32) * inv
    x2 = x1 + jnp.dot(o_scr[...].astype(BF16), wco_ref[...], preferred_element_type=F32)
    x2_ref[...] = x2
    h3 = _rms(x2, gf_ref[...]).astype(BF16)
    h3_ref[...] = h3

    logits = jnp.dot(h3, wrt_ref[...], preferred_element_type=F32) + brt_ref[...]
    lane = lax.broadcasted_iota(jnp.int32, (tm, LANES), 1)
    i0, i1, g0, g1 = _route(logits, lane)
    e0 = i0 - ROUTE_L2
    e1 = i1 - ROUTE_L2
    hit0 = lane == e0
    hit1 = lane == e1
    onehot = jnp.where(hit0 | hit1, 1.0, 0.0)
    row = lax.broadcasted_iota(jnp.int32, (tm, tm), 0)
    col = lax.broadcasted_iota(jnp.int32, (tm, tm), 1)
    before = jnp.where(col < row, 1.0, 0.0).astype(BF16)
    seen = jnp.dot(before, onehot.astype(BF16), preferred_element_type=F32)
    rank0 = jnp.sum(jnp.where(hit0, seen, 0.0), axis=-1, keepdims=True)
    rank1 = jnp.sum(jnp.where(hit1, seen, 0.0), axis=-1, keepdims=True)
    cnt_ref[...] = jnp.sum(onehot, axis=0, keepdims=True)

    rec = jnp.zeros((tm, LANES), F32)
    for idx, val in ((R_EID0, e0.astype(F32)), (R_EID1, e1.astype(F32)), (R_RANK0, rank0),
                     (R_RANK1, rank1), (R_GATE0, g0), (R_GATE1, g1)):
        rec = jnp.where(lane == idx, val, rec)
    route_ref[...] = rec[:, :ROUTE_ROWS]
    routet_ref[...] = rec.T[:ROUTE_ROWS, :]


def _post_block(x, a_n, b_n, mk, mv, weights, *, tm):
    b, s, d = x.shape
    if s >= tm:
        nsub, grid = 1, (b, s // tm)
        tok = lambda i, j: (i, j, 0)
        flat = lambda i, j: i * (s // tm) + j
    else:
        nsub = tm // s
        assert b % nsub == 0
        x, a_n, b_n = (t.reshape(b // nsub, tm, t.shape[-1]) for t in (x, a_n, b_n))
        grid = (b // nsub, 1)
        tok = lambda i, j: (i, 0, 0)
        flat = lambda i, j: i
    mem = lambda i, j: (i, 0, 0)
    const = lambda arr: pl.BlockSpec(arr.shape, lambda i, j: (0,) * arr.ndim)
    in_specs = [pl.BlockSpec((None, tm, d), tok),
                pl.BlockSpec((None, tm, W_GROUP), tok), pl.BlockSpec((None, tm, W_GROUP), tok),
                pl.BlockSpec((nsub, mk.shape[1], W_MEM), mem), pl.BlockSpec((nsub, mv.shape[1], W_MEM), mem)]
    in_specs += [const(w) for w in weights]
    n = b * s
    out_shape = [jax.ShapeDtypeStruct((n, d), F32), jax.ShapeDtypeStruct((n, d), BF16),
                 jax.ShapeDtypeStruct((n, ROUTE_ROWS), F32), jax.ShapeDtypeStruct((ROUTE_ROWS, n), F32),
                 jax.ShapeDtypeStruct((n // tm, 1, LANES), F32)]
    out_specs = [pl.BlockSpec((tm, d), lambda i, j: (flat(i, j), 0)),
                 pl.BlockSpec((tm, d), lambda i, j: (flat(i, j), 0)),
                 pl.BlockSpec((tm, ROUTE_ROWS), lambda i, j: (flat(i, j), 0)),
                 pl.BlockSpec((ROUTE_ROWS, tm), lambda i, j: (0, flat(i, j))),
                 pl.BlockSpec((None, 1, LANES), lambda i, j: (flat(i, j), 0, 0))]
    return pl.pallas_call(
        functools.partial(_post_body, tm=tm, nsub=nsub),
        grid=grid,
        in_specs=in_specs,
        out_specs=out_specs,
        out_shape=out_shape,
        scratch_shapes=[pltpu.VMEM((tm, W_MEM), F32)],
        compiler_params=pltpu.CompilerParams(
            dimension_semantics=("parallel", "parallel"), vmem_limit_bytes=VMEM_LIMIT),
        name="post_block",
    )(x, a_n, b_n, mk, mv, *weights)


def _prep_post(w_out, g_cross, w_cq, w_co, g_ffn, w_r1, b_r1, w_r2, b_r2):
    pad = LANES - N_GROUPS - N_EXPERTS
    w_rt = jnp.pad(jnp.concatenate([w_r1, w_r2], axis=1), ((0, 0), (0, pad))).astype(BF16)
    b_rt = jnp.pad(jnp.concatenate([b_r1, b_r2]).reshape(1, -1), ((0, 0), (0, pad))).astype(F32)
    return [w_out[:W_GROUP].astype(BF16), w_out[W_GROUP:].astype(BF16), g_cross.reshape(1, -1),
            w_cq.astype(BF16), w_co.astype(BF16), g_ffn.reshape(1, -1), w_rt, b_rt]


D_EXPERT = 512
TOP_K = 2
ROW_TILE = 256
RUN_ALIGN = 16
PLAN_TILES = LANES
TILE_TABLE = 2 * LANES


def _run_sizes(tm):
    sizes, s = [], RUN_ALIGN
    while s <= tm:
        sizes.append(s)
        s *= 2
    return tuple(reversed(sizes))


def _local_rows(tm):
    return -(-(TOP_K * tm + N_EXPERTS * (RUN_ALIGN - 1)) // ROW_TILE) * ROW_TILE


def _n_row_tiles(n_tokens, tm):
    rows = n_tokens * TOP_K + (n_tokens // tm) * N_EXPERTS * (RUN_ALIGN - 1) + N_EXPERTS * (ROW_TILE - 1)
    return rows // ROW_TILE


def _plan_body(cnt_ref, lstart_ref, gstart_ref, n16_ref, offs_ref, te_ref):
    cnt = cnt_ref[...].astype(jnp.int32)
    n16 = (cnt + (RUN_ALIGN - 1)) & ~(RUN_ALIGN - 1)
    n16f = n16.astype(F32)
    lstart = _lane_cumsum(n16f) - n16f
    earlier = (_lane_cumsum(n16f.T) - n16f.T).T
    total = jnp.sum(n16f, axis=0, keepdims=True).astype(jnp.int32)
    seg = jnp.broadcast_to((total + (ROW_TILE - 1)) & ~(ROW_TILE - 1), (8, LANES)).astype(F32)
    ends = _lane_cumsum(seg)
    offs = ends - seg
    n16_ref[...] = n16
    lstart_ref[...] = lstart.astype(jnp.int32)
    gstart_ref[...] = (earlier + offs[0:1, :]).astype(jnp.int32)
    offs_ref[...] = offs[0:1, :].astype(jnp.int32)
    lane = lax.broadcasted_iota(jnp.int32, (1, LANES), 1)
    tile_start = (lax.broadcasted_iota(jnp.int32, te_ref.shape, 1) * ROW_TILE).astype(F32)
    te = jnp.zeros(te_ref.shape, jnp.int32)
    for e in range(N_EXPERTS):
        end_e = jnp.sum(jnp.where(lane == e, ends[0:1, :], 0.0), axis=-1, keepdims=True)
        te = te + jnp.where(end_e <= tile_start, 1, 0)
    te_ref[...] = jnp.minimum(te, N_EXPERTS - 1)


def _plan(counts):
    nt = counts.shape[0]
    assert nt <= PLAN_TILES
    cnt = jnp.pad(counts.reshape(nt, LANES), ((0, PLAN_TILES - nt), (0, 0)))
    grid_i32 = jax.ShapeDtypeStruct((PLAN_TILES, LANES), jnp.int32)
    lstart, gstart, n16, offs, te = pl.pallas_call(
        _plan_body,
        out_shape=[grid_i32, grid_i32, grid_i32, jax.ShapeDtypeStruct((1, LANES), jnp.int32),
                   jax.ShapeDtypeStruct((1, TILE_TABLE), jnp.int32)],
        name="moe_plan",
    )(cnt)
    per_tile = lambda t: t[:nt].reshape(nt, 1, LANES)
    return per_tile(n16), per_tile(lstart), per_tile(gstart), offs.reshape(LANES), te.reshape(TILE_TABLE)


def _for_each_run_chunk(n16_ref, lstart_ref, gstart_ref, sizes, fn):
    def per_expert(e, carry):
        n = n16_ref[0, e]
        lo = lstart_ref[0, e]
        go = gstart_ref[0, e]
        for size in sizes:
            @pl.when((n & size) != 0)
            def _():
                fn(pl.multiple_of(lo, RUN_ALIGN), pl.multiple_of(go, RUN_ALIGN), size)
            lo = lo + (n & size)
            go = go + (n & size)
        return carry

    lax.fori_loop(0, N_EXPERTS, per_expert, None)


def _local_positions_row(rt_ref, lstart_ref):
    pos = []
    for r_eid, r_rank in ((R_EID0, R_RANK0), (R_EID1, R_RANK1)):
        eid = rt_ref[r_eid:r_eid + 1, :].astype(jnp.int32)
        p = rt_ref[r_rank:r_rank + 1, :].astype(jnp.int32)
        for e in range(N_EXPERTS):
            p = p + jnp.where(eid == e, lstart_ref[0, e], 0)
        pos.append(p)
    return pos


def _dispatch_body(offs_ref, n16_ref, lstart_ref, gstart_ref, hp_ref, hs_ref, rt_ref, xs_ref,
                   loc, zeros, sem, zsem, *, tm, n_prompt_tiles):
    i = pl.program_id(0)
    n_tiles = xs_ref.shape[0] // ROW_TILE

    @pl.when(i == 0)
    def _():
        zeros[...] = jnp.zeros(zeros.shape, zeros.dtype)
        zero_tile = lambda row: pltpu.make_async_copy(
            zeros, xs_ref.at[pl.ds(pl.multiple_of(row, ROW_TILE), ROW_TILE)], zsem)
        n_used = offs_ref[N_EXPERTS] // ROW_TILE

        def tail(j, carry, op):
            op(zero_tile(j * ROW_TILE))
            return carry

        for op in (lambda c: c.start(), lambda c: c.wait()):
            for e in range(N_EXPERTS):
                @pl.when(offs_ref[e + 1] > offs_ref[e])
                def _():
                    op(zero_tile(offs_ref[e + 1] - ROW_TILE))
            lax.fori_loop(n_used, n_tiles, functools.partial(tail, op=op), None)

    pos0, pos1 = _local_positions_row(rt_ref, lstart_ref)
    slot = lax.broadcasted_iota(jnp.int32, (loc.shape[0], tm), 0)
    perm = jnp.where(slot == pos0, 1.0, jnp.where(slot == pos1, 1.0, 0.0)).astype(BF16)

    @pl.when(i < n_prompt_tiles)
    def _():
        loc[...] = jnp.dot(perm, hp_ref[...], preferred_element_type=F32).astype(BF16)

    @pl.when(i >= n_prompt_tiles)
    def _():
        loc[...] = jnp.dot(perm, hs_ref[...], preferred_element_type=F32).astype(BF16)

    chunk = lambda lo, go, size: pltpu.make_async_copy(loc.at[pl.ds(lo, size)], xs_ref.at[pl.ds(go, size)], sem)
    sizes = _run_sizes(tm)
    _for_each_run_chunk(n16_ref, lstart_ref, gstart_ref, sizes, lambda lo, go, size: chunk(lo, go, size).start())
    _for_each_run_chunk(n16_ref, lstart_ref, gstart_ref, sizes, lambda lo, go, size: chunk(lo, go, size).wait())


def _dispatch(h3_prompt, h3_sample, route_t, plan, *, tm):
    n16, lstart, gstart, offs, _ = plan
    n_p, n_s, d = h3_prompt.shape[0], h3_sample.shape[0], h3_prompt.shape[-1]
    assert n_p % tm == 0 and n_s % tm == 0
    n = n_p + n_s
    nt = n // tm
    npt = n_p // tm
    n_rows = _n_row_tiles(n, tm) * ROW_TILE
    smem_tile = lambda: pl.BlockSpec((None, 1, LANES), lambda i, offs: (i, 0, 0), memory_space=pltpu.SMEM)
    return pl.pallas_call(
        functools.partial(_dispatch_body, tm=tm, n_prompt_tiles=npt),
        grid_spec=pltpu.PrefetchScalarGridSpec(
            num_scalar_prefetch=1,
            grid=(nt,),
            in_specs=[smem_tile(), smem_tile(), smem_tile(),
                      pl.BlockSpec((tm, d), lambda i, offs: (jnp.minimum(i, npt - 1), 0)),
                      pl.BlockSpec((tm, d), lambda i, offs: (jnp.maximum(i - npt, 0), 0)),
                      pl.BlockSpec((ROUTE_ROWS, tm), lambda i, offs: (0, i))],
            out_specs=pl.BlockSpec(memory_space=pl.ANY),
            scratch_shapes=[pltpu.VMEM((_local_rows(tm), d), BF16), pltpu.VMEM((ROW_TILE, d), BF16),
                            pltpu.SemaphoreType.DMA(()), pltpu.SemaphoreType.DMA(())]),
        out_shape=jax.ShapeDtypeStruct((n_rows, d), BF16),
        compiler_params=pltpu.CompilerParams(dimension_semantics=("arbitrary",), vmem_limit_bytes=VMEM_LIMIT),
        name="moe_dispatch",
    )(offs, n16, lstart, gstart, h3_prompt, h3_sample, route_t)


def _experts_body(te_ref, offs_ref, xs_ref, wg_ref, wu_ref, wd_ref, ys_ref, wg_bf, wu_bf, wd_bf):
    i = pl.program_id(0)
    n_used = offs_ref[N_EXPERTS] // ROW_TILE

    @pl.when(i < n_used)
    def _():
        @pl.when((i == 0) | (te_ref[i] != te_ref[jnp.maximum(i - 1, 0)]))
        def _():
            wg_bf[...] = wg_ref[...].astype(BF16)
            wu_bf[...] = wu_ref[...].astype(BF16)
            wd_bf[...] = wd_ref[...].astype(BF16)

        x = xs_ref[...]
        dot = functools.partial(jnp.dot, preferred_element_type=F32)
        gate = dot(x, wg_bf[...])
        up = dot(x, wu_bf[...])
        act = (gate * jax.nn.sigmoid(gate) * up).astype(BF16)
        ys_ref[...] = dot(act, wd_bf[...]).astype(BF16)

    @pl.when(i >= n_used)
    def _():
        ys_ref[...] = jnp.zeros(ys_ref.shape, ys_ref.dtype)


def _experts(xs, te, offs, w_gate, w_up, w_down):
    n_rows, d = xs.shape
    last = lambda i, te, offs: jnp.minimum(i, offs[N_EXPERTS] // ROW_TILE - 1)
    wspec = lambda shape: pl.BlockSpec((None,) + shape, lambda i, te, offs: (te[last(i, te, offs)], 0, 0))
    return pl.pallas_call(
        _experts_body,
        grid_spec=pltpu.PrefetchScalarGridSpec(
            num_scalar_prefetch=2,
            grid=(n_rows // ROW_TILE,),
            in_specs=[pl.BlockSpec((ROW_TILE, d), lambda i, te, offs: (last(i, te, offs), 0)),
                      wspec((d, D_EXPERT)), wspec((d, D_EXPERT)), wspec((D_EXPERT, d))],
            out_specs=pl.BlockSpec((ROW_TILE, d), lambda i, te, offs: (i, 0)),
            scratch_shapes=[pltpu.VMEM((d, D_EXPERT), BF16), pltpu.VMEM((d, D_EXPERT), BF16),
                            pltpu.VMEM((D_EXPERT, d), BF16)]),
        out_shape=jax.ShapeDtypeStruct((n_rows, d), BF16),
        compiler_params=pltpu.CompilerParams(dimension_semantics=("arbitrary",), vmem_limit_bytes=VMEM_LIMIT),
        name="moe_experts",
    )(te, offs, xs, w_gate, w_up, w_down)


def _combine_body(n16_ref, lstart_ref, gstart_ref, x2p_ref, x2s_ref, route_ref, lsv_ref, ys_ref, g_ref,
                  yp_ref, ysm_ref, loc, sem, *, tm, n_prompt_tiles):
    i = pl.program_id(0)

    @pl.when(i == 0)
    def _():
        loc[...] = jnp.zeros(loc.shape, loc.dtype)

    chunk = lambda lo, go, size: pltpu.make_async_copy(ys_ref.at[pl.ds(go, size)], loc.at[pl.ds(lo, size)], sem)
    sizes = _run_sizes(tm)
    _for_each_run_chunk(n16_ref, lstart_ref, gstart_ref, sizes, lambda lo, go, size: chunk(lo, go, size).start())
    _for_each_run_chunk(n16_ref, lstart_ref, gstart_ref, sizes, lambda lo, go, size: chunk(lo, go, size).wait())

    lane = lax.broadcasted_iota(jnp.int32, (tm, LANES), 1)
    slot = lax.broadcasted_iota(jnp.int32, (tm, loc.shape[0]), 1)
    weights = jnp.zeros(slot.shape, F32)
    for r_eid, r_rank, r_gate in ((R_EID0, R_RANK0, R_GATE0), (R_EID1, R_RANK1, R_GATE1)):
        eid = route_ref[:, r_eid:r_eid + 1].astype(jnp.int32)
        start = jnp.sum(jnp.where(lane == eid, lsv_ref[...], 0), axis=-1, keepdims=True)
        pos = route_ref[:, r_rank:r_rank + 1].astype(jnp.int32) + start
        weights = jnp.where(slot == pos, route_ref[:, r_gate:r_gate + 1], weights)
    moe = jnp.dot(weights.astype(BF16), loc[...], preferred_element_type=F32)

    @pl.when(i < n_prompt_tiles)
    def _():
        yp_ref[...] = _rms(x2p_ref[...] + moe, g_ref[...])

    @pl.when(i >= n_prompt_tiles)
    def _():
        ysm_ref[...] = _rms(x2s_ref[...] + moe, g_ref[...])


def _combine(x2_prompt, x2_sample, route, plan, ys, g_final, *, tm):
    n16, lstart, gstart, _, _ = plan
    (n_p, d), n_s = x2_prompt.shape, x2_sample.shape[0]
    assert n_p % tm == 0 and n_s % tm == 0
    npt = n_p // tm
    nt = npt + n_s // tm
    smem_tile = lambda: pl.BlockSpec((None, 1, LANES), lambda i: (i, 0, 0), memory_space=pltpu.SMEM)
    prompt_tile = lambda: pl.BlockSpec((tm, d), lambda i: (jnp.minimum(i, npt - 1), 0))
    sample_tile = lambda: pl.BlockSpec((tm, d), lambda i: (jnp.maximum(i - npt, 0), 0))
    return pl.pallas_call(
        functools.partial(_combine_body, tm=tm, n_prompt_tiles=npt),
        grid=(nt,),
        in_specs=[smem_tile(), smem_tile(), smem_tile(), prompt_tile(), sample_tile(),
                  pl.BlockSpec((tm, ROUTE_ROWS), lambda i: (i, 0)),
                  pl.BlockSpec((None, 1, LANES), lambda i: (i, 0, 0)),
                  pl.BlockSpec(memory_space=pl.ANY),
                  pl.BlockSpec((1, d), lambda i: (0, 0))],
        out_specs=[prompt_tile(), sample_tile()],
        out_shape=[jax.ShapeDtypeStruct((n_p, d), F32), jax.ShapeDtypeStruct((n_s, d), F32)],
        scratch_shapes=[pltpu.VMEM((_local_rows(tm), d), BF16), pltpu.SemaphoreType.DMA(())],
        compiler_params=pltpu.CompilerParams(dimension_semantics=("arbitrary",), vmem_limit_bytes=VMEM_LIMIT),
        name="moe_combine",
    )(n16, lstart, gstart, x2_prompt, x2_sample, route, lstart, ys, g_final)


TOKEN_TILE = 512
FOX_Q_TILE = 256
FOX_CACHE_TILE = 1024


def kernel(x_prompt, x_sample, cache_fox_k, cache_fox_v, cache_fox_logf, cache_band_k, cache_band_v, cache_mem_k, cache_mem_v, mem_prompt, g_mix, w_in, b_forget, g_out_fox, g_out_band, rel_table, w_out, g_cross, g_mem, w_cq, w_ck, w_cv, w_co, g_ffn, w_router1, b_router1, w_router2, b_router2, w_exp_gate, w_exp_up, w_exp_down, g_final):
    assert g_mix.shape[0] == 1, "single-layer model"
    bsz, seq, d = x_prompt.shape
    sb, st, _ = x_sample.shape
    n_s = sb * st
    past = cache_fox_k.shape[2]
    n_mem = mem_prompt.shape[1]
    row = lambda g: g.reshape(1, -1)

    w_pad, bf_pad, g_mix_r = _prep_proj(w_in[0], b_forget[0], g_mix[0])
    g_of, g_ob = row(g_out_fox[0]), row(g_out_band[0])
    bias_t, bias_s = _band_bias(_prep_band_bias_row(rel_table[0]))

    qx, kx, vat, qxb, kxb, vbt, kaf, vaf, kbf, vbf, logf = _proj(
        x_prompt, g_mix_r, w_pad, bf_pad, tm=TOKEN_TILE, prompt=True)
    a_p = _fox_prompt(qx, kx, vat, g_of, tq=FOX_Q_TILE)
    b_p = _band_prompt(qxb, kxb, vbt, bias_t, g_ob)

    s_out = _proj(x_sample.reshape(1, n_s, d), g_mix_r, w_pad, bf_pad, tm=n_s, prompt=False)
    sqa, ska, sva, sqb, skb, svb, skaf, svaf, skbf, svbf = (t.reshape(sb, st, W_GROUP) for t in s_out[:10])
    slogf = s_out[10].reshape(sb, st, N_HEADS)
    slft = s_out[11].reshape(N_HEADS, sb, st).transpose(1, 0, 2)
    a_s = _fox_sample(sqa, ska, sva, slft,
                      cache_fox_k[0].reshape(sb, past, W_GROUP), cache_fox_v[0].reshape(sb, past, W_GROUP),
                      cache_fox_logf[0].transpose(0, 2, 1), g_of, pt=FOX_CACHE_TILE)
    bp = cache_band_k.shape[2]
    b_s, nbk, nbv = _band_sample(sqb, skb, svb, skbf, svbf,
                                 cache_band_k[0].reshape(sb, bp, W_GROUP), cache_band_v[0].reshape(sb, bp, W_GROUP),
                                 bias_s, g_ob)

    w_ckv = jnp.concatenate([w_ck[0], w_cv[0]], axis=1).astype(BF16)
    mkf, mvf, mk, mv = _mem_kv(mem_prompt, row(g_mem[0]), w_ckv)
    post_w = _prep_post(w_out[0], g_cross[0], w_cq[0], w_co[0], g_ffn[0],
                        w_router1[0], b_router1[0], w_router2[0], b_router2[0])
    x2_p, h3_p, route_p, routet_p, cnt_p = _post_block(x_prompt, a_p, b_p, mk, mv, post_w, tm=TOKEN_TILE)
    cmk = cache_mem_k[0].reshape(sb, n_mem, W_MEM).astype(BF16)
    cmv = cache_mem_v[0].reshape(sb, n_mem, W_MEM).astype(BF16)
    x2_s, h3_s, route_s, routet_s, cnt_s = _post_block(x_sample, a_s, b_s, cmk, cmv, post_w, tm=TOKEN_TILE)
    route = jnp.concatenate([route_p, route_s], axis=0)
    route_t = jnp.concatenate([routet_p, routet_s], axis=1)

    plan = _plan(jnp.concatenate([cnt_p, cnt_s], axis=0))
    xs = _dispatch(h3_p, h3_s, route_t, plan, tm=TOKEN_TILE)
    ys = _experts(xs, plan[4], plan[3], w_exp_gate[0], w_exp_up[0], w_exp_down[0])
    y_p, y_s = _combine(x2_p, x2_s, route, plan, ys, row(g_final), tm=TOKEN_TILE)

    heads = lambda t, n: t.reshape(1, n, -1, N_HEADS, HEAD_DIM)
    mem_heads = lambda t: t.reshape(1, bsz, n_mem, N_HEADS_MEM, HEAD_DIM_MEM)
    return (y_p.reshape(bsz, seq, d), y_s.reshape(sb, st, d),
            heads(kaf, bsz), heads(vaf, bsz), logf.reshape(1, bsz, seq, N_HEADS),
            heads(kbf, bsz), heads(vbf, bsz), mem_heads(mkf), mem_heads(mvf),
            heads(skaf, sb), heads(svaf, sb), slogf.reshape(1, sb, st, N_HEADS),
            heads(nbk, sb), heads(nbv, sb))
```

```python
import functools

import jax
import jax.numpy as jnp
from jax import lax
from jax.experimental import pallas as pl
from jax.experimental.pallas import tpu as pltpu

F32 = jnp.float32
BF16 = jnp.bfloat16

D_MODEL = 1024
HEAD_DIM = 64
N_HEADS = 8
W_GROUP = N_HEADS * HEAD_DIM
N_PAIRS = N_HEADS // 2
CHUNK = 64
LEFT_CHUNKS = 8
LEFT = LEFT_CHUNKS * CHUNK
REL_CLIP = 128
EPS = 1e-6
NEG_INF = -1e30
ATTN_SCALE = HEAD_DIM ** -0.5
LANES = 128
PROJ_PAD = 3 * W_GROUP * 2 + LANES
VMEM_LIMIT = 56 * 1024 * 1024


def _rms(x, g):
    ms = jnp.mean(x * x, axis=-1, keepdims=True)
    return x * lax.rsqrt(ms + EPS) * g


def _log_sigmoid(x):
    return -(jnp.maximum(-x, 0.0) + jnp.log1p(jnp.exp(-jnp.abs(x))))


def _lane_cumsum(x):
    n = x.shape[-1]
    lane = lax.broadcasted_iota(jnp.int32, x.shape, 1)
    k = 1
    while k < n:
        x = x + jnp.where(lane >= k, pltpu.roll(x, k, axis=1), 0.0)
        k *= 2
    return x


LOG2E = 1.4426950408889634
SCALE_BASE2 = ATTN_SCALE * LOG2E


def _split3(x):
    hi = x.astype(BF16).astype(F32)
    mid = (x - hi).astype(BF16).astype(F32)
    lo = x - hi - mid
    return hi, mid, lo


def _extra_lane(parity):
    return HEAD_DIM if parity == 0 else 0


def _fox_extras(c3t, hp, tm):
    row = lax.broadcasted_iota(jnp.int32, (8, tm), 0)

    def group(h, q_side):
        hi, mid, lo = (p[h:h + 1, :] for p in c3t)
        if q_side:
            return jnp.where(row < 3, 1.0, jnp.where(row == 3, hi, jnp.where(row == 4, mid, jnp.where(row == 5, lo, 0.0))))
        return jnp.where(row == 0, -hi, jnp.where(row == 1, -mid, jnp.where(row == 2, -lo, jnp.where(row < 6, 1.0, 0.0))))

    gap = jnp.zeros((HEAD_DIM - 8, tm), F32)
    sides = []
    for q_side in (True, False):
        t = jnp.concatenate([group(2 * hp + 1, q_side), gap, group(2 * hp, q_side), gap], axis=0)
        sides.append(t.T)
    return sides


def _head_blocks(x128, extras, lane):
    return (jnp.where(lane < HEAD_DIM, x128, extras).astype(BF16),
            jnp.where(lane >= HEAD_DIM, x128, extras).astype(BF16))


def _proj_common(x_ref, g_ref, w_ref, bf_ref, kaf_ref, vaf_ref, kbf_ref, vbf_ref, logf_ref, keep_tiles):
    s = pl.program_id(1)
    ns = pl.num_programs(1)
    h = _rms(x_ref[...], g_ref[...]).astype(BF16)
    z = jnp.dot(h, w_ref[...], preferred_element_type=F32)
    w = W_GROUP
    kaf_ref[...] = z[:, w:2 * w]
    vaf_ref[...] = z[:, 2 * w:3 * w]

    @pl.when(s >= ns - keep_tiles)
    def _():
        kbf_ref[...] = z[:, 4 * w:5 * w]
        vbf_ref[...] = z[:, 5 * w:6 * w]

    logf = _log_sigmoid(z[:, 6 * w:6 * w + LANES] + bf_ref[...])
    logf_ref[...] = logf[:, :N_HEADS]
    return z, logf


def _proj_prompt_body(x_ref, g_ref, w_ref, bf_ref, qx_ref, kx_ref, vat_ref, qxb_ref, kxb_ref, vbt_ref,
                      kaf_ref, vaf_ref, kbf_ref, vbf_ref, logf_ref, carry_ref, *, tm, keep_tiles):
    z, logf = _proj_common(x_ref, g_ref, w_ref, bf_ref, kaf_ref, vaf_ref, kbf_ref, vbf_ref, logf_ref, keep_tiles)
    w = W_GROUP
    vat_ref[...] = z[:, 2 * w:3 * w].T.astype(BF16)
    vbt_ref[...] = z[:, 5 * w:6 * w].T.astype(BF16)

    @pl.when(pl.program_id(1) == 0)
    def _():
        carry_ref[...] = jnp.zeros_like(carry_ref)

    ct = _lane_cumsum(logf.T[:N_HEADS, :]) + carry_ref[:, 0:1]
    carry_ref[...] = jnp.broadcast_to(ct[:, tm - 1:tm], carry_ref.shape)
    c3t = _split3(ct * LOG2E)
    lane = lax.broadcasted_iota(jnp.int32, (tm, LANES), 1)
    band_q_extras = jnp.where((lane == _extra_lane(0)) | (lane == _extra_lane(1)), 1.0, 0.0)
    band_k_extras = jnp.zeros((tm, LANES), F32)
    for hp in range(N_PAIRS):
        blocks = slice(2 * hp * LANES, 2 * (hp + 1) * LANES)
        blk = lambda base, hp=hp: z[:, base * w + hp * LANES:base * w + (hp + 1) * LANES]
        q_extras, k_extras = _fox_extras(c3t, hp, tm)
        qx_ref[:, blocks] = jnp.concatenate(_head_blocks(blk(0) * SCALE_BASE2, q_extras, lane), axis=1)
        kx_ref[:, blocks] = jnp.concatenate(_head_blocks(blk(1), k_extras, lane), axis=1)
        qxb_ref[:, blocks] = jnp.concatenate(_head_blocks(blk(3) * SCALE_BASE2, band_q_extras, lane), axis=1)
        kxb_ref[:, blocks] = jnp.concatenate(_head_blocks(blk(4), band_k_extras, lane), axis=1)


def _proj_sample_body(x_ref, g_ref, w_ref, bf_ref, qa_ref, ka_ref, va_ref, qb_ref, kb_ref, vb_ref,
                      kaf_ref, vaf_ref, kbf_ref, vbf_ref, logf_ref, lt_ref, *, keep_tiles):
    z, logf = _proj_common(x_ref, g_ref, w_ref, bf_ref, kaf_ref, vaf_ref, kbf_ref, vbf_ref, logf_ref, keep_tiles)
    w = W_GROUP
    qa_ref[...] = (z[:, 0:w] * ATTN_SCALE).astype(BF16)
    ka_ref[...] = z[:, w:2 * w].astype(BF16)
    va_ref[...] = z[:, 2 * w:3 * w].astype(BF16)
    qb_ref[...] = (z[:, 3 * w:4 * w] * ATTN_SCALE).astype(BF16)
    kb_ref[...] = z[:, 4 * w:5 * w].astype(BF16)
    vb_ref[...] = z[:, 5 * w:6 * w].astype(BF16)
    lt_ref[...] = logf.T[:N_HEADS, :]


def _proj(x, g_mix, w_pad, bf_pad, *, tm, prompt):
    b, s, d = x.shape
    ns = s // tm
    keep = min(LEFT, s)
    assert s % tm == 0 and keep % tm == 0
    keep_tiles = keep // tm
    row = pl.BlockSpec((None, tm, W_GROUP), lambda i, j: (i, j, 0))
    wide = pl.BlockSpec((None, tm, N_HEADS * LANES), lambda i, j: (i, j, 0))
    col = pl.BlockSpec((None, W_GROUP, tm), lambda i, j: (i, 0, j))
    keep_spec = pl.BlockSpec((None, tm, W_GROUP), lambda i, j: (i, jnp.maximum(j - (ns - keep_tiles), 0), 0))
    heads_row = pl.BlockSpec((None, tm, N_HEADS), lambda i, j: (i, j, 0))
    heads_col = pl.BlockSpec((None, N_HEADS, tm), lambda i, j: (i, 0, j))
    const = lambda shape: pl.BlockSpec(shape, lambda i, j: (0,) * len(shape))
    rows_bf = jax.ShapeDtypeStruct((b, s, W_GROUP), BF16)
    wide_bf = jax.ShapeDtypeStruct((b, s, N_HEADS * LANES), BF16)
    cols_bf = jax.ShapeDtypeStruct((b, W_GROUP, s), BF16)
    f32_tail = [jax.ShapeDtypeStruct((b, s, W_GROUP), F32)] * 2
    f32_tail += [jax.ShapeDtypeStruct((b, keep, W_GROUP), F32)] * 2
    f32_tail += [jax.ShapeDtypeStruct((b, s, N_HEADS), F32)]
    tail_specs = [row, row, keep_spec, keep_spec, heads_row]
    if prompt:
        body = functools.partial(_proj_prompt_body, tm=tm, keep_tiles=keep_tiles)
        out_shape = [wide_bf, wide_bf, cols_bf, wide_bf, wide_bf, cols_bf] + f32_tail
        out_specs = [wide, wide, col, wide, wide, col] + tail_specs
        scratch = [pltpu.VMEM((N_HEADS, LANES), F32)]
    else:
        body = functools.partial(_proj_sample_body, keep_tiles=keep_tiles)
        out_shape = [rows_bf] * 6 + f32_tail + [jax.ShapeDtypeStruct((b, N_HEADS, s), F32)]
        out_specs = [row] * 6 + tail_specs + [heads_col]
        scratch = []
    return pl.pallas_call(
        body,
        grid=(b, ns),
        in_specs=[pl.BlockSpec((None, tm, d), lambda i, j: (i, j, 0)),
                  const((1, d)), const(w_pad.shape), const((1, LANES))],
        out_specs=out_specs,
        out_shape=out_shape,
        scratch_shapes=scratch,
        compiler_params=pltpu.CompilerParams(
            dimension_semantics=("parallel", "arbitrary"), vmem_limit_bytes=VMEM_LIMIT),
        name="proj",
    )(x, g_mix, w_pad, bf_pad)


def _prep_proj(w_in, b_forget, g_mix):
    cols = w_in.shape[-1]
    w_pad = jnp.pad(w_in, ((0, 0), (0, PROJ_PAD - cols))).astype(BF16)
    bf_pad = jnp.pad(b_forget.reshape(1, -1), ((0, 0), (0, LANES - N_HEADS))).astype(F32)
    return w_pad, bf_pad, g_mix.reshape(1, -1)


def _pair_masks():
    lane = lax.broadcasted_iota(jnp.int32, (1, LANES), 1)
    return lane < HEAD_DIM


def _head_q(q128, even_lanes, parity):
    keep = even_lanes if parity == 0 else jnp.logical_not(even_lanes)
    return jnp.where(keep, q128, jnp.zeros_like(q128))


def _head_v(v128, even_lanes, parity):
    keep = even_lanes if parity == 0 else jnp.logical_not(even_lanes)
    return jnp.where(keep, v128, jnp.ones_like(v128))


def _head_out(acc_even, acc_odd, even_lanes):
    inv_e = 1.0 / acc_even[:, HEAD_DIM:HEAD_DIM + 1]
    inv_o = 1.0 / acc_odd[:, 0:1]
    return jnp.where(even_lanes, acc_even * inv_e, acc_odd * inv_o)


_NT = (((1,), (1,)), ((), ()))


def _pair_rows():
    row = lax.broadcasted_iota(jnp.int32, (LANES, 1), 0)
    return row < HEAD_DIM


def _head_vt(vt128, even_rows, parity):
    keep = even_rows if parity == 0 else jnp.logical_not(even_rows)
    return jnp.where(keep, vt128, jnp.ones_like(vt128))


def _head_out_t(acc_even, acc_odd, even_rows):
    inv_e = 1.0 / acc_even[HEAD_DIM:HEAD_DIM + 1, :]
    inv_o = 1.0 / acc_odd[0:1, :]
    return jnp.where(even_rows, acc_even * inv_e, acc_odd * inv_o)


def _fox_body(qx_ref, kx_ref, vt_ref, g_ref, o_ref, s_scr, p_scr, m_scr, alpha_scr, acc_scr, ot_scr, *, tq):
    qi = pl.program_id(1)
    even_rows = _pair_rows()
    m_scr[...] = jnp.full(m_scr.shape, NEG_INF, F32)
    acc_scr[...] = jnp.zeros(acc_scr.shape, F32)
    key = lax.broadcasted_iota(jnp.int32, (tq, tq), 0)
    qry = lax.broadcasted_iota(jnp.int32, (tq, tq), 1)
    causal = key <= qry

    def tile(j, masked):
        start = pl.multiple_of(j * tq, tq)
        for h in range(N_HEADS):
            head = slice(h * LANES, (h + 1) * LANES)
            s_scr[h] = lax.dot_general(kx_ref[pl.ds(start, tq), head], qx_ref[:, head], _NT,
                                       preferred_element_type=F32)
        for h in range(N_HEADS):
            st = s_scr[h]
            if masked:
                st = jnp.where(causal, st, NEG_INF)
            m_old = m_scr[h:h + 1, :]
            m_new = jnp.maximum(m_old, jnp.max(st, axis=0, keepdims=True))
            p_scr[h] = jnp.exp2(st - m_new).astype(BF16)
            alpha_scr[h:h + 1, :] = jnp.exp2(m_old - m_new)
            m_scr[h:h + 1, :] = m_new
        for h in range(N_HEADS):
            pair = slice((h // 2) * LANES, (h // 2 + 1) * LANES)
            vt = _head_vt(vt_ref[pair, pl.ds(start, tq)], even_rows, h % 2)
            acc_scr[h] = acc_scr[h] * alpha_scr[h:h + 1, :] + jnp.dot(vt, p_scr[h], preferred_element_type=F32)

    def body(j, carry):
        tile(j, False)
        return carry

    lax.fori_loop(0, qi, body, None)
    tile(qi, True)
    for hp in range(N_PAIRS):
        ot_scr[hp * LANES:(hp + 1) * LANES, :] = _head_out_t(acc_scr[2 * hp], acc_scr[2 * hp + 1], even_rows)
    o_ref[...] = _rms(ot_scr[...].T, g_ref[...]).astype(BF16)


def _fox_prompt(qx, kx, vat, g_out, *, tq):
    b, s, wx = qx.shape
    w = vat.shape[1]
    return pl.pallas_call(
        functools.partial(_fox_body, tq=tq),
        grid=(b, s // tq),
        in_specs=[pl.BlockSpec((None, tq, wx), lambda i, j: (i, j, 0)),
                  pl.BlockSpec((None, s, wx), lambda i, j: (i, 0, 0)),
                  pl.BlockSpec((None, w, s), lambda i, j: (i, 0, 0)),
                  pl.BlockSpec((1, w), lambda i, j: (0, 0))],
        out_specs=pl.BlockSpec((None, tq, w), lambda i, j: (i, j, 0)),
        out_shape=jax.ShapeDtypeStruct((b, s, w), BF16),
        scratch_shapes=[pltpu.VMEM((N_HEADS, tq, tq), F32), pltpu.VMEM((N_HEADS, tq, tq), BF16),
                        pltpu.VMEM((N_HEADS, tq), F32), pltpu.VMEM((N_HEADS, tq), F32),
                        pltpu.VMEM((N_HEADS, LANES, tq), F32), pltpu.VMEM((w, tq), F32)],
        compiler_params=pltpu.CompilerParams(
            dimension_semantics=("parallel", "arbitrary"), vmem_limit_bytes=VMEM_LIMIT),
        name="fox_prompt",
    )(qx, kx, vat, g_out)


BAND_CHUNKS = 4
BAND_Q = BAND_CHUNKS * CHUNK
BAND_K = (LEFT_CHUNKS + BAND_CHUNKS) * CHUNK
BIAS_ROW = BAND_K + BAND_Q
BAND_K_SAMPLE = (LEFT + CHUNK + LANES - 1) // LANES * LANES


def _prep_band_bias_row(rel_table):
    pivot = LEFT + BAND_Q
    n_hi = pivot - REL_CLIP + 1
    n_mid = min(2 * REL_CLIP, BIAS_ROW - n_hi)
    n_lo = BIAS_ROW - n_hi - n_mid
    parts = [jnp.broadcast_to(rel_table[2 * REL_CLIP:], (n_hi, N_HEADS)),
             rel_table[2 * REL_CLIP - 1::-1][:n_mid],
             jnp.broadcast_to(rel_table[:1], (n_lo, N_HEADS))]
    return jnp.concatenate(parts, axis=0).T.reshape(N_HEADS, 1, BIAS_ROW)


def _band_bias_body(row_ref, bt_ref, bs_ref):
    rows = jnp.broadcast_to(row_ref[...], (BAND_Q, BIAS_ROW))
    skew = pltpu.roll(rows, 0, axis=1, stride=1, stride_axis=0)
    bias = skew[:, BAND_Q:]
    bs_ref[...] = bias[:CHUNK, :BAND_K_SAMPLE]
    qc = lax.broadcasted_iota(jnp.int32, (BAND_Q, BAND_K), 0) // CHUNK
    kc = lax.broadcasted_iota(jnp.int32, (BAND_Q, BAND_K), 1) // CHUNK
    bt_ref[...] = jnp.where((kc >= qc) & (kc <= qc + LEFT_CHUNKS), bias * LOG2E, NEG_INF).T


def _band_bias(bias_row):
    return pl.pallas_call(
        _band_bias_body,
        grid=(N_HEADS,),
        in_specs=[pl.BlockSpec((None, 1, BIAS_ROW), lambda h: (h, 0, 0))],
        out_specs=[pl.BlockSpec((None, BAND_K, BAND_Q), lambda h: (h, 0, 0)),
                   pl.BlockSpec((None, CHUNK, BAND_K_SAMPLE), lambda h: (h, 0, 0))],
        out_shape=[jax.ShapeDtypeStruct((N_HEADS, BAND_K, BAND_Q), F32),
                   jax.ShapeDtypeStruct((N_HEADS, CHUNK, BAND_K_SAMPLE), F32)],
        name="band_bias",
    )(bias_row)


def _band_body(qx_ref, kx_ref, vt_ref, bias_ref, g_ref, o_ref, kpad, vtpad, s_scr, p_scr, ot_scr, *, s_len):
    step = pl.program_id(1)

    @pl.when(step == 0)
    def _():
        lane = lax.broadcasted_iota(jnp.int32, (LEFT, 2 * LANES), 1)
        flags = (lane == _extra_lane(0)) | (lane == LANES + _extra_lane(1))
        pad_pair = jnp.where(flags, NEG_INF, 0.0).astype(BF16)
        for hp in range(N_PAIRS):
            kpad[0:LEFT, 2 * hp * LANES:2 * (hp + 1) * LANES] = pad_pair
        vtpad[:, 0:LEFT] = jnp.zeros((W_GROUP, LEFT), BF16)
        kpad[LEFT:LEFT + s_len, :] = kx_ref[...]
        vtpad[:, LEFT:LEFT + s_len] = vt_ref[...]

    even_rows = _pair_rows()
    start = pl.multiple_of(step * BAND_Q, BAND_Q)
    for h in range(N_HEADS):
        head = slice(h * LANES, (h + 1) * LANES)
        s_scr[h] = lax.dot_general(kpad[pl.ds(start, BAND_K), head], qx_ref[:, head], _NT,
                                   preferred_element_type=F32)
    for h in range(N_HEADS):
        st = s_scr[h] + bias_ref[h]
        p_scr[h] = jnp.exp2(st - jnp.max(st, axis=0, keepdims=True)).astype(BF16)
    for hp in range(N_PAIRS):
        pair = slice(hp * LANES, (hp + 1) * LANES)
        vtwin = vtpad[pair, pl.ds(start, BAND_K)]
        accs = [jnp.dot(_head_vt(vtwin, even_rows, parity), p_scr[2 * hp + parity], preferred_element_type=F32)
                for parity in range(2)]
        ot_scr[pair, :] = _head_out_t(accs[0], accs[1], even_rows)
    o_ref[...] = _rms(ot_scr[...].T, g_ref[...]).astype(BF16)


def _band_prompt(qxb, kxb, vbt, bias_t, g_out):
    b, s, wx = qxb.shape
    w = vbt.shape[1]
    return pl.pallas_call(
        functools.partial(_band_body, s_len=s),
        grid=(b, s // BAND_Q),
        in_specs=[pl.BlockSpec((None, BAND_Q, wx), lambda i, j: (i, j, 0)),
                  pl.BlockSpec((None, s, wx), lambda i, j: (i, 0, 0)),
                  pl.BlockSpec((None, w, s), lambda i, j: (i, 0, 0)),
                  pl.BlockSpec(bias_t.shape, lambda i, j: (0, 0, 0)),
                  pl.BlockSpec((1, w), lambda i, j: (0, 0))],
        out_specs=pl.BlockSpec((None, BAND_Q, w), lambda i, j: (i, j, 0)),
        out_shape=jax.ShapeDtypeStruct((b, s, w), BF16),
        scratch_shapes=[pltpu.VMEM((LEFT + s, wx), BF16), pltpu.VMEM((w, LEFT + s), BF16),
                        pltpu.VMEM((N_HEADS, BAND_K, BAND_Q), F32), pltpu.VMEM((N_HEADS, BAND_K, BAND_Q), BF16),
                        pltpu.VMEM((w, BAND_Q), F32)],
        compiler_params=pltpu.CompilerParams(
            dimension_semantics=("parallel", "arbitrary"), vmem_limit_bytes=VMEM_LIMIT),
        name="band_prompt",
    )(qxb, kxb, vbt, bias_t, g_out)


def _row_to_col(row):
    n = row.shape[-1]
    r = lax.broadcasted_iota(jnp.int32, (n, n), 0)
    c = lax.broadcasted_iota(jnp.int32, (n, n), 1)
    return jnp.sum(jnp.where(r == c, jnp.broadcast_to(row, (n, n)), 0.0), axis=-1, keepdims=True)


def _fox_sample_body(q_ref, kn_ref, vn_ref, lft_ref, kc_ref, vc_ref, clft_ref, g_ref, o_ref,
                     cct_scr, cn_scr, m_scr, acc_scr, o_scr, *, t_new, pt):
    p_idx = pl.program_id(1)
    n_p = pl.num_programs(1)
    even = _pair_masks()

    @pl.when(p_idx == 0)
    def _():
        cct = _lane_cumsum(clft_ref[...])
        cct_scr[...] = cct
        cn_scr[...] = _lane_cumsum(lft_ref[...]) + cct[:, cct.shape[1] - 1:]
        m_scr[...] = jnp.full(m_scr.shape, NEG_INF, F32)
        acc_scr[...] = jnp.zeros(acc_scr.shape, F32)

    def update(h, s, v):
        m = m_scr[h]
        m_new = jnp.maximum(m, jnp.max(s, axis=-1, keepdims=True))
        p = jnp.exp(s - m_new).astype(BF16)
        acc_scr[h] = acc_scr[h] * jnp.exp(m - m_new) + jnp.dot(p, v, preferred_element_type=F32)
        m_scr[h] = m_new

    start = pl.multiple_of(p_idx * pt, pt)
    for hp in range(N_PAIRS):
        lanes = slice(hp * LANES, (hp + 1) * LANES)
        q128 = q_ref[:, lanes]
        kc = kc_ref[:, lanes].astype(BF16)
        vc = vc_ref[:, lanes].astype(BF16)
        for parity in range(2):
            h = 2 * hp + parity
            cq = _row_to_col(cn_scr[h:h + 1, :])
            s = lax.dot_general(_head_q(q128, even, parity), kc, _NT, preferred_element_type=F32)
            update(h, s + cq - cct_scr[h:h + 1, pl.ds(start, pt)], _head_v(vc, even, parity))

    @pl.when(p_idx == n_p - 1)
    def _():
        row = lax.broadcasted_iota(jnp.int32, (t_new, t_new), 0)
        col = lax.broadcasted_iota(jnp.int32, (t_new, t_new), 1)
        for hp in range(N_PAIRS):
            lanes = slice(hp * LANES, (hp + 1) * LANES)
            q128 = q_ref[:, lanes]
            kn = kn_ref[:, lanes]
            vn = vn_ref[:, lanes]
            for parity in range(2):
                h = 2 * hp + parity
                cn_row = cn_scr[h:h + 1, :]
                s = lax.dot_general(_head_q(q128, even, parity), kn, _NT, preferred_element_type=F32)
                s = jnp.where(col <= row, s + _row_to_col(cn_row) - cn_row, NEG_INF)
                update(h, s, _head_v(vn, even, parity))
            o_scr[:, lanes] = _head_out(acc_scr[2 * hp], acc_scr[2 * hp + 1], even)
        o_ref[...] = _rms(o_scr[...], g_ref[...]).astype(BF16)


def _fox_sample(q, kn, vn, lft, kc, vc, clft, g_out, *, pt):
    b, t, w = q.shape
    p_len = kc.shape[1]
    new = lambda: pl.BlockSpec((None, t, w), lambda i, j: (i, 0, 0))
    cache = lambda: pl.BlockSpec((None, pt, w), lambda i, j: (i, j, 0))
    return pl.pallas_call(
        functools.partial(_fox_sample_body, t_new=t, pt=pt),
        grid=(b, p_len // pt),
        in_specs=[new(), new(), new(),
                  pl.BlockSpec((None, N_HEADS, t), lambda i, j: (i, 0, 0)),
                  cache(), cache(),
                  pl.BlockSpec((None, N_HEADS, p_len), lambda i, j: (i, 0, 0)),
                  pl.BlockSpec((1, w), lambda i, j: (0, 0))],
        out_specs=new(),
        out_shape=jax.ShapeDtypeStruct((b, t, w), BF16),
        scratch_shapes=[pltpu.VMEM((N_HEADS, p_len), F32), pltpu.VMEM((N_HEADS, t), F32),
                        pltpu.VMEM((N_HEADS, t, 1), F32), pltpu.VMEM((N_HEADS, t, LANES), F32),
                        pltpu.VMEM((t, w), F32)],
        compiler_params=pltpu.CompilerParams(
            dimension_semantics=("parallel", "arbitrary"), vmem_limit_bytes=VMEM_LIMIT),
        name="fox_sample",
    )(q, kn, vn, lft, kc, vc, clft, g_out)


def _band_sample_body(q_ref, kn_ref, vn_ref, knf_ref, vnf_ref, kc_ref, vc_ref, bias_ref, g_ref,
                      o_ref, nk_ref, nv_ref, kcat, vcat, o_scr, *, t_new, bp):
    kcat[0:bp, :] = kc_ref[...].astype(BF16)
    vcat[0:bp, :] = vc_ref[...].astype(BF16)
    kcat[bp:bp + t_new, :] = kn_ref[...]
    vcat[bp:bp + t_new, :] = vn_ref[...]
    nk_ref[0:bp - t_new, :] = kc_ref[t_new:bp, :]
    nv_ref[0:bp - t_new, :] = vc_ref[t_new:bp, :]
    nk_ref[bp - t_new:bp, :] = knf_ref[...]
    nv_ref[bp - t_new:bp, :] = vnf_ref[...]
    even = _pair_masks()
    for hp in range(N_PAIRS):
        lanes = slice(hp * LANES, (hp + 1) * LANES)
        q128 = q_ref[:, lanes]
        k = kcat[:, lanes]
        v = vcat[:, lanes]
        accs = []
        for parity in range(2):
            h = 2 * hp + parity
            s = lax.dot_general(_head_q(q128, even, parity), k, _NT, preferred_element_type=F32)
            s = s + bias_ref[h, 0:t_new, 0:bp + t_new]
            p = jnp.exp(s - jnp.max(s, axis=-1, keepdims=True)).astype(BF16)
            accs.append(jnp.dot(p, _head_v(v, even, parity), preferred_element_type=F32))
        o_scr[:, lanes] = _head_out(accs[0], accs[1], even)
    o_ref[...] = _rms(o_scr[...], g_ref[...]).astype(BF16)


def _band_sample(q, kn, vn, knf, vnf, kc, vc, bias, g_out):
    b, t, w = q.shape
    bp = kc.shape[1]
    assert t == CHUNK and bp == LEFT
    new = lambda: pl.BlockSpec((None, t, w), lambda i: (i, 0, 0))
    buf = lambda: pl.BlockSpec((None, bp, w), lambda i: (i, 0, 0))
    return pl.pallas_call(
        functools.partial(_band_sample_body, t_new=t, bp=bp),
        grid=(b,),
        in_specs=[new(), new(), new(), new(), new(), buf(), buf(),
                  pl.BlockSpec(bias.shape, lambda i: (0, 0, 0)),
                  pl.BlockSpec((1, w), lambda i: (0, 0))],
        out_specs=[new(), buf(), buf()],
        out_shape=[jax.ShapeDtypeStruct((b, t, w), BF16), jax.ShapeDtypeStruct((b, bp, w), F32),
                   jax.ShapeDtypeStruct((b, bp, w), F32)],
        scratch_shapes=[pltpu.VMEM((bp + t, w), BF16), pltpu.VMEM((bp + t, w), BF16), pltpu.VMEM((t, w), F32)],
        compiler_params=pltpu.CompilerParams(dimension_semantics=("parallel",), vmem_limit_bytes=VMEM_LIMIT),
        name="band_sample",
    )(q, kn, vn, knf, vnf, kc, vc, bias, g_out)


N_HEADS_MEM = 4
HEAD_DIM_MEM = 128
W_MEM = N_HEADS_MEM * HEAD_DIM_MEM
MEM_SCALE = HEAD_DIM_MEM ** -0.5


def _memkv_body(m_ref, g_ref, w_ref, kf_ref, vf_ref, k_ref, v_ref):
    h = _rms(m_ref[...], g_ref[...]).astype(BF16)
    z = jnp.dot(h, w_ref[...], preferred_element_type=F32)
    kf_ref[...] = z[:, :W_MEM]
    vf_ref[...] = z[:, W_MEM:]
    k_ref[...] = z[:, :W_MEM].astype(BF16)
    v_ref[...] = z[:, W_MEM:].astype(BF16)


def _mem_kv(mem, g_mem, w_ckv):
    b, n, d = mem.shape
    blk = lambda: pl.BlockSpec((None, n, W_MEM), lambda i: (i, 0, 0))
    return pl.pallas_call(
        _memkv_body,
        grid=(b,),
        in_specs=[pl.BlockSpec((None, n, d), lambda i: (i, 0, 0)),
                  pl.BlockSpec((1, d), lambda i: (0, 0)),
                  pl.BlockSpec(w_ckv.shape, lambda i: (0, 0))],
        out_specs=[blk(), blk(), blk(), blk()],
        out_shape=[jax.ShapeDtypeStruct((b, n, W_MEM), F32)] * 2 + [jax.ShapeDtypeStruct((b, n, W_MEM), BF16)] * 2,
        compiler_params=pltpu.CompilerParams(dimension_semantics=("parallel",), vmem_limit_bytes=VMEM_LIMIT),
        name="mem_kv",
    )(mem, g_mem, w_ckv)


N_GROUPS = 4
EXPERTS_PER_GROUP = 8
N_EXPERTS = N_GROUPS * EXPERTS_PER_GROUP
ROUTE_L2 = N_GROUPS
ROUTE_ROWS = 8
R_EID0, R_EID1, R_RANK0, R_RANK1, R_GATE0, R_GATE1 = range(6)


def _lane_max(x, mask):
    return jnp.max(jnp.where(mask, x, -jnp.inf), axis=-1, keepdims=True)


def _first_lane(mask, lane):
    return jnp.min(jnp.where(mask, lane, LANES), axis=-1, keepdims=True)


def _route(logits, lane):
    is_l1 = lane < N_GROUPS
    m1 = _lane_max(logits, is_l1)
    grp = _first_lane(is_l1 & (logits == m1), lane)
    wg = 1.0 / jnp.sum(jnp.where(is_l1, jnp.exp(logits - m1), 0.0), axis=-1, keepdims=True)
    lo = ROUTE_L2 + grp * EXPERTS_PER_GROUP
    in_grp = (lane >= lo) & (lane < lo + EXPERTS_PER_GROUP)
    v0 = _lane_max(logits, in_grp)
    i0 = _first_lane(in_grp & (logits == v0), lane)
    rest = in_grp & (lane != i0)
    v1 = _lane_max(logits, rest)
    i1 = _first_lane(rest & (logits == v1), lane)
    e1 = jnp.exp(v1 - v0)
    den = 1.0 / (1.0 + e1)
    return i0, i1, wg * den, wg * e1 * den


def _post_body(x_ref, a_ref, b_ref, mk_ref, mv_ref, woa_ref, wob_ref, gc_ref, wcq_ref, wco_ref,
               gf_ref, wrt_ref, brt_ref,
               x2_ref, h3_ref, route_ref, routet_ref, cnt_ref, o_scr, *, tm, nsub):
    x1 = (x_ref[...] + jnp.dot(a_ref[...], woa_ref[...], preferred_element_type=F32)
          + jnp.dot(b_ref[...], wob_ref[...], preferred_element_type=F32))
    h2 = _rms(x1, gc_ref[...]).astype(BF16)
    qc = (jnp.dot(h2, wcq_ref[...], preferred_element_type=F32) * MEM_SCALE).astype(BF16)
    rows = tm // nsub
    for sub in range(nsub):
        rs = slice(sub * rows, (sub + 1) * rows)
        for hm in range(N_HEADS_MEM):
            lanes = slice(hm * HEAD_DIM_MEM, (hm + 1) * HEAD_DIM_MEM)
            s = lax.dot_general(qc[rs, lanes], mk_ref[sub, :, lanes], _NT, preferred_element_type=F32)
            p = jnp.exp(s - jnp.max(s, axis=-1, keepdims=True))
            inv = 1.0 / jnp.sum(p, axis=-1, keepdims=True)
            o_scr[rs, lanes] = jnp.dot(p.astype(BF16), mv_ref[sub, :, lanes], preferred_element_type=F32) * inv
    x2 = x1 + jnp.dot(o_scr[...].astype(BF16), wco_ref[...], preferred_element_type=F32)
    x2_ref[...] = x2
    h3 = _rms(x2, gf_ref[...]).astype(BF16)
    h3_ref[...] = h3

    logits = jnp.dot(h3, wrt_ref[...], preferred_element_type=F32) + brt_ref[...]
    lane = lax.broadcasted_iota(jnp.int32, (tm, LANES), 1)
    i0, i1, g0, g1 = _route(logits, lane)
    e0 = i0 - ROUTE_L2
    e1 = i1 - ROUTE_L2
    hit0 = lane == e0
    hit1 = lane == e1
    onehot = jnp.where(hit0 | hit1, 1.0, 0.0)
    row = lax.broadcasted_iota(jnp.int32, (tm, tm), 0)
    col = lax.broadcasted_iota(jnp.int32, (tm, tm), 1)
    before = jnp.where(col < row, 1.0, 0.0).astype(BF16)
    seen = jnp.dot(before, onehot.astype(BF16), preferred_element_type=F32)
    rank0 = jnp.sum(jnp.where(hit0, seen, 0.0), axis=-1, keepdims=True)
    rank1 = jnp.sum(jnp.where(hit1, seen, 0.0), axis=-1, keepdims=True)
    cnt_ref[...] = jnp.sum(onehot, axis=0, keepdims=True)

    rec = jnp.zeros((tm, LANES), F32)
    for idx, val in ((R_EID0, e0.astype(F32)), (R_EID1, e1.astype(F32)), (R_RANK0, rank0),
                     (R_RANK1, rank1), (R_GATE0, g0), (R_GATE1, g1)):
        rec = jnp.where(lane == idx, val, rec)
    route_ref[...] = rec[:, :ROUTE_ROWS]
    routet_ref[...] = rec.T[:ROUTE_ROWS, :]


def _post_block(x, a_n, b_n, mk, mv, weights, *, tm):
    b, s, d = x.shape
    if s >= tm:
        nsub, grid = 1, (b, s // tm)
        tok = lambda i, j: (i, j, 0)
        flat = lambda i, j: i * (s // tm) + j
    else:
        nsub = tm // s
        assert b % nsub == 0
        x, a_n, b_n = (t.reshape(b // nsub, tm, t.shape[-1]) for t in (x, a_n, b_n))
        grid = (b // nsub, 1)
        tok = lambda i, j: (i, 0, 0)
        flat = lambda i, j: i
    mem = lambda i, j: (i, 0, 0)
    const = lambda arr: pl.BlockSpec(arr.shape, lambda i, j: (0,) * arr.ndim)
    in_specs = [pl.BlockSpec((None, tm, d), tok),
                pl.BlockSpec((None, tm, W_GROUP), tok), pl.BlockSpec((None, tm, W_GROUP), tok),
                pl.BlockSpec((nsub, mk.shape[1], W_MEM), mem), pl.BlockSpec((nsub, mv.shape[1], W_MEM), mem)]
    in_specs += [const(w) for w in weights]
    n = b * s
    out_shape = [jax.ShapeDtypeStruct((n, d), F32), jax.ShapeDtypeStruct((n, d), BF16),
                 jax.ShapeDtypeStruct((n, ROUTE_ROWS), F32), jax.ShapeDtypeStruct((ROUTE_ROWS, n), F32),
                 jax.ShapeDtypeStruct((n // tm, 1, LANES), F32)]
    out_specs = [pl.BlockSpec((tm, d), lambda i, j: (flat(i, j), 0)),
                 pl.BlockSpec((tm, d), lambda i, j: (flat(i, j), 0)),
                 pl.BlockSpec((tm, ROUTE_ROWS), lambda i, j: (flat(i, j), 0)),
                 pl.BlockSpec((ROUTE_ROWS, tm), lambda i, j: (0, flat(i, j))),
                 pl.BlockSpec((None, 1, LANES), lambda i, j: (flat(i, j), 0, 0))]
    return pl.pallas_call(
        functools.partial(_post_body, tm=tm, nsub=nsub),
        grid=grid,
        in_specs=in_specs,
        out_specs=out_specs,
        out_shape=out_shape,
        scratch_shapes=[pltpu.VMEM((tm, W_MEM), F32)],
        compiler_params=pltpu.CompilerParams(
            dimension_semantics=("parallel", "parallel"), vmem_limit_bytes=VMEM_LIMIT),
        name="post_block",
    )(x, a_n, b_n, mk, mv, *weights)


def _prep_post(w_out, g_cross, w_cq, w_co, g_ffn, w_r1, b_r1, w_r2, b_r2):
    pad = LANES - N_GROUPS - N_EXPERTS
    w_rt = jnp.pad(jnp.concatenate([w_r1, w_r2], axis=1), ((0, 0), (0, pad))).astype(BF16)
    b_rt = jnp.pad(jnp.concatenate([b_r1, b_r2]).reshape(1, -1), ((0, 0), (0, pad))).astype(F32)
    return [w_out[:W_GROUP].astype(BF16), w_out[W_GROUP:].astype(BF16), g_cross.reshape(1, -1),
            w_cq.astype(BF16), w_co.astype(BF16), g_ffn.reshape(1, -1), w_rt, b_rt]


D_EXPERT = 512
TOP_K = 2
ROW_TILE = 512
MXU_DIM = 256
RUN_ALIGN = 16
PLAN_TILES = LANES
TILE_TABLE = 2 * LANES


def _run_sizes(tm):
    sizes, s = [], RUN_ALIGN
    while s <= tm:
        sizes.append(s)
        s *= 2
    return tuple(reversed(sizes))


def _local_rows(tm):
    return -(-(TOP_K * tm + N_EXPERTS * (RUN_ALIGN - 1)) // MXU_DIM) * MXU_DIM


def _n_row_tiles(n_tokens, tm):
    rows = n_tokens * TOP_K + (n_tokens // tm) * N_EXPERTS * (RUN_ALIGN - 1) + N_EXPERTS * (ROW_TILE - 1)
    return rows // ROW_TILE


def _plan_body(cnt_ref, lstart_ref, gstart_ref, n16_ref, offs_ref, te_ref):
    cnt = cnt_ref[...].astype(jnp.int32)
    n16 = (cnt + (RUN_ALIGN - 1)) & ~(RUN_ALIGN - 1)
    n16f = n16.astype(F32)
    lstart = _lane_cumsum(n16f) - n16f
    earlier = (_lane_cumsum(n16f.T) - n16f.T).T
    total = jnp.sum(n16f, axis=0, keepdims=True).astype(jnp.int32)
    seg = jnp.broadcast_to((total + (ROW_TILE - 1)) & ~(ROW_TILE - 1), (8, LANES)).astype(F32)
    ends = _lane_cumsum(seg)
    offs = ends - seg
    n16_ref[...] = n16
    lstart_ref[...] = lstart.astype(jnp.int32)
    gstart_ref[...] = (earlier + offs[0:1, :]).astype(jnp.int32)
    offs_ref[...] = offs[0:1, :].astype(jnp.int32)
    lane = lax.broadcasted_iota(jnp.int32, (1, LANES), 1)
    tile_start = (lax.broadcasted_iota(jnp.int32, te_ref.shape, 1) * ROW_TILE).astype(F32)
    te = jnp.zeros(te_ref.shape, jnp.int32)
    for e in range(N_EXPERTS):
        end_e = jnp.sum(jnp.where(lane == e, ends[0:1, :], 0.0), axis=-1, keepdims=True)
        te = te + jnp.where(end_e <= tile_start, 1, 0)
    te_ref[...] = jnp.minimum(te, N_EXPERTS - 1)


def _plan(counts):
    nt = counts.shape[0]
    assert nt <= PLAN_TILES
    cnt = jnp.pad(counts.reshape(nt, LANES), ((0, PLAN_TILES - nt), (0, 0)))
    grid_i32 = jax.ShapeDtypeStruct((PLAN_TILES, LANES), jnp.int32)
    lstart, gstart, n16, offs, te = pl.pallas_call(
        _plan_body,
        out_shape=[grid_i32, grid_i32, grid_i32, jax.ShapeDtypeStruct((1, LANES), jnp.int32),
                   jax.ShapeDtypeStruct((1, TILE_TABLE), jnp.int32)],
        name="moe_plan",
    )(cnt)
    per_tile = lambda t: t[:nt].reshape(nt, 1, LANES)
    return per_tile(n16), per_tile(lstart), per_tile(gstart), offs.reshape(LANES), te.reshape(TILE_TABLE)


def _for_each_run_chunk(n16_ref, lstart_ref, gstart_ref, sizes, fn):
    def per_expert(e, carry):
        n = n16_ref[0, e]
        lo = lstart_ref[0, e]
        go = gstart_ref[0, e]
        for size in sizes:
            @pl.when((n & size) != 0)
            def _():
                fn(pl.multiple_of(lo, RUN_ALIGN), pl.multiple_of(go, RUN_ALIGN), size)
            lo = lo + (n & size)
            go = go + (n & size)
        return carry

    lax.fori_loop(0, N_EXPERTS, per_expert, None)


def _local_positions_row(rt_ref, lstart_ref):
    pos = []
    for r_eid, r_rank in ((R_EID0, R_RANK0), (R_EID1, R_RANK1)):
        eid = rt_ref[r_eid:r_eid + 1, :].astype(jnp.int32)
        p = rt_ref[r_rank:r_rank + 1, :].astype(jnp.int32)
        for e in range(N_EXPERTS):
            p = p + jnp.where(eid == e, lstart_ref[0, e], 0)
        pos.append(p)
    return pos


def _dispatch_body(offs_ref, n16_ref, lstart_ref, gstart_ref, pn16_ref, plstart_ref, pgstart_ref,
                   hp_ref, hs_ref, rt_ref, xs_ref, loc, zeros, sems, zsem, *, tm, n_prompt_tiles):
    i = pl.program_id(0)
    n_tiles = xs_ref.shape[0] // ROW_TILE
    half = lax.rem(i, 2)

    @pl.when(i == 0)
    def _():
        zeros[...] = jnp.zeros(zeros.shape, zeros.dtype)
        zero_tile = lambda row: pltpu.make_async_copy(
            zeros, xs_ref.at[pl.ds(pl.multiple_of(row, ROW_TILE), ROW_TILE)], zsem)
        n_used = offs_ref[N_EXPERTS] // ROW_TILE

        def tail(j, carry, op):
            op(zero_tile(j * ROW_TILE))
            return carry

        for op in (lambda c: c.start(), lambda c: c.wait()):
            for e in range(N_EXPERTS):
                @pl.when(offs_ref[e + 1] > offs_ref[e])
                def _():
                    op(zero_tile(offs_ref[e + 1] - ROW_TILE))
            lax.fori_loop(n_used, n_tiles, functools.partial(tail, op=op), None)

    pos0, pos1 = _local_positions_row(rt_ref, lstart_ref)
    slot = lax.broadcasted_iota(jnp.int32, (loc.shape[1], tm), 0)
    perm = jnp.where(slot == pos0, 1.0, jnp.where(slot == pos1, 1.0, 0.0)).astype(BF16)

    @pl.when(i < n_prompt_tiles)
    def _():
        loc[half] = jnp.dot(perm, hp_ref[...], preferred_element_type=F32).astype(BF16)

    @pl.when(i >= n_prompt_tiles)
    def _():
        loc[half] = jnp.dot(perm, hs_ref[...], preferred_element_type=F32).astype(BF16)

    def chunk(buf, lo, go, size):
        return pltpu.make_async_copy(loc.at[buf, pl.ds(lo, size)], xs_ref.at[pl.ds(go, size)], sems.at[buf])

    sizes = _run_sizes(tm)

    @pl.when(i > 0)
    def _():
        _for_each_run_chunk(pn16_ref, plstart_ref, pgstart_ref, sizes,
                            lambda lo, go, size: chunk(1 - half, lo, go, size).wait())

    _for_each_run_chunk(n16_ref, lstart_ref, gstart_ref, sizes, lambda lo, go, size: chunk(half, lo, go, size).start())

    @pl.when(i == pl.num_programs(0) - 1)
    def _():
        _for_each_run_chunk(n16_ref, lstart_ref, gstart_ref, sizes,
                            lambda lo, go, size: chunk(half, lo, go, size).wait())


def _dispatch(h3_prompt, h3_sample, route_t, plan, *, tm):
    n16, lstart, gstart, offs, _ = plan
    n_p, n_s, d = h3_prompt.shape[0], h3_sample.shape[0], h3_prompt.shape[-1]
    assert n_p % tm == 0 and n_s % tm == 0
    n = n_p + n_s
    nt = n // tm
    npt = n_p // tm
    n_rows = _n_row_tiles(n, tm) * ROW_TILE
    smem_tile = lambda: pl.BlockSpec((None, 1, LANES), lambda i, offs: (i, 0, 0), memory_space=pltpu.SMEM)
    smem_prev = lambda: pl.BlockSpec((None, 1, LANES), lambda i, offs: (jnp.maximum(i - 1, 0), 0, 0),
                                     memory_space=pltpu.SMEM)
    return pl.pallas_call(
        functools.partial(_dispatch_body, tm=tm, n_prompt_tiles=npt),
        grid_spec=pltpu.PrefetchScalarGridSpec(
            num_scalar_prefetch=1,
            grid=(nt,),
            in_specs=[smem_tile(), smem_tile(), smem_tile(), smem_prev(), smem_prev(), smem_prev(),
                      pl.BlockSpec((tm, d), lambda i, offs: (jnp.minimum(i, npt - 1), 0)),
                      pl.BlockSpec((tm, d), lambda i, offs: (jnp.maximum(i - npt, 0), 0)),
                      pl.BlockSpec((ROUTE_ROWS, tm), lambda i, offs: (0, i))],
            out_specs=pl.BlockSpec(memory_space=pl.ANY),
            scratch_shapes=[pltpu.VMEM((2, _local_rows(tm), d), BF16), pltpu.VMEM((ROW_TILE, d), BF16),
                            pltpu.SemaphoreType.DMA((2,)), pltpu.SemaphoreType.DMA(())]),
        out_shape=jax.ShapeDtypeStruct((n_rows, d), BF16),
        compiler_params=pltpu.CompilerParams(dimension_semantics=("arbitrary",), vmem_limit_bytes=VMEM_LIMIT),
        name="moe_dispatch",
    )(offs, n16, lstart, gstart, n16, lstart, gstart, h3_prompt, h3_sample, route_t)


def _experts_body(te_ref, offs_ref, xs_ref, wg_ref, wu_ref, wd_ref, ys_ref, wg_bf, wu_bf, wd_bf):
    i = pl.program_id(0)
    n_used = offs_ref[N_EXPERTS] // ROW_TILE

    @pl.when(i < n_used)
    def _():
        @pl.when((i == 0) | (te_ref[i] != te_ref[jnp.maximum(i - 1, 0)]))
        def _():
            wg_bf[...] = wg_ref[...].astype(BF16)
            wu_bf[...] = wu_ref[...].astype(BF16)
            wd_bf[...] = wd_ref[...].astype(BF16)

        x = xs_ref[...]
        dot = functools.partial(jnp.dot, preferred_element_type=F32)
        gate = dot(x, wg_bf[...])
        up = dot(x, wu_bf[...])
        act = (gate * jax.nn.sigmoid(gate) * up).astype(BF16)
        ys_ref[...] = dot(act, wd_bf[...]).astype(BF16)

    @pl.when(i >= n_used)
    def _():
        ys_ref[...] = jnp.zeros(ys_ref.shape, ys_ref.dtype)


def _experts(xs, te, offs, w_gate, w_up, w_down):
    n_rows, d = xs.shape
    last = lambda i, te, offs: jnp.minimum(i, offs[N_EXPERTS] // ROW_TILE - 1)
    wspec = lambda shape: pl.BlockSpec((None,) + shape, lambda i, te, offs: (te[last(i, te, offs)], 0, 0))
    return pl.pallas_call(
        _experts_body,
        grid_spec=pltpu.PrefetchScalarGridSpec(
            num_scalar_prefetch=2,
            grid=(n_rows // ROW_TILE,),
            in_specs=[pl.BlockSpec((ROW_TILE, d), lambda i, te, offs: (last(i, te, offs), 0)),
                      wspec((d, D_EXPERT)), wspec((d, D_EXPERT)), wspec((D_EXPERT, d))],
            out_specs=pl.BlockSpec((ROW_TILE, d), lambda i, te, offs: (i, 0)),
            scratch_shapes=[pltpu.VMEM((d, D_EXPERT), BF16), pltpu.VMEM((d, D_EXPERT), BF16),
                            pltpu.VMEM((D_EXPERT, d), BF16)]),
        out_shape=jax.ShapeDtypeStruct((n_rows, d), BF16),
        compiler_params=pltpu.CompilerParams(dimension_semantics=("arbitrary",), vmem_limit_bytes=VMEM_LIMIT),
        name="moe_experts",
    )(te, offs, xs, w_gate, w_up, w_down)


def _combine_body(n16_ref, lstart_ref, gstart_ref, nn16_ref, nlstart_ref, ngstart_ref,
                  x2p_ref, x2s_ref, route_ref, lsv_ref, ys_ref, g_ref,
                  yp_ref, ysm_ref, loc, sems, *, tm, n_prompt_tiles):
    i = pl.program_id(0)
    half = lax.rem(i, 2)
    sizes = _run_sizes(tm)

    def chunk(buf, lo, go, size):
        return pltpu.make_async_copy(ys_ref.at[pl.ds(go, size)], loc.at[buf, pl.ds(lo, size)], sems.at[buf])

    @pl.when(i == 0)
    def _():
        loc[...] = jnp.zeros(loc.shape, loc.dtype)
        _for_each_run_chunk(n16_ref, lstart_ref, gstart_ref, sizes, lambda lo, go, size: chunk(0, lo, go, size).start())

    @pl.when(i + 1 < pl.num_programs(0))
    def _():
        _for_each_run_chunk(nn16_ref, nlstart_ref, ngstart_ref, sizes,
                            lambda lo, go, size: chunk(1 - half, lo, go, size).start())

    _for_each_run_chunk(n16_ref, lstart_ref, gstart_ref, sizes, lambda lo, go, size: chunk(half, lo, go, size).wait())

    lane = lax.broadcasted_iota(jnp.int32, (tm, LANES), 1)
    slot = lax.broadcasted_iota(jnp.int32, (tm, loc.shape[1]), 1)
    weights = jnp.zeros(slot.shape, F32)
    for r_eid, r_rank, r_gate in ((R_EID0, R_RANK0, R_GATE0), (R_EID1, R_RANK1, R_GATE1)):
        eid = route_ref[:, r_eid:r_eid + 1].astype(jnp.int32)
        start = jnp.sum(jnp.where(lane == eid, lsv_ref[...], 0), axis=-1, keepdims=True)
        pos = route_ref[:, r_rank:r_rank + 1].astype(jnp.int32) + start
        weights = jnp.where(slot == pos, route_ref[:, r_gate:r_gate + 1], weights)
    moe = jnp.dot(weights.astype(BF16), loc[half], preferred_element_type=F32)

    @pl.when(i < n_prompt_tiles)
    def _():
        yp_ref[...] = _rms(x2p_ref[...] + moe, g_ref[...])

    @pl.when(i >= n_prompt_tiles)
    def _():
        ysm_ref[...] = _rms(x2s_ref[...] + moe, g_ref[...])


def _combine(x2_prompt, x2_sample, route, plan, ys, g_final, *, tm):
    n16, lstart, gstart, _, _ = plan
    (n_p, d), n_s = x2_prompt.shape, x2_sample.shape[0]
    assert n_p % tm == 0 and n_s % tm == 0
    npt = n_p // tm
    nt = npt + n_s // tm
    smem_tile = lambda: pl.BlockSpec((None, 1, LANES), lambda i: (i, 0, 0), memory_space=pltpu.SMEM)
    smem_next = lambda: pl.BlockSpec((None, 1, LANES), lambda i: (jnp.minimum(i + 1, nt - 1), 0, 0),
                                     memory_space=pltpu.SMEM)
    prompt_tile = lambda: pl.BlockSpec((tm, d), lambda i: (jnp.minimum(i, npt - 1), 0))
    sample_tile = lambda: pl.BlockSpec((tm, d), lambda i: (jnp.maximum(i - npt, 0), 0))
    return pl.pallas_call(
        functools.partial(_combine_body, tm=tm, n_prompt_tiles=npt),
        grid=(nt,),
        in_specs=[smem_tile(), smem_tile(), smem_tile(), smem_next(), smem_next(), smem_next(),
                  prompt_tile(), sample_tile(),
                  pl.BlockSpec((tm, ROUTE_ROWS), lambda i: (i, 0)),
                  pl.BlockSpec((None, 1, LANES), lambda i: (i, 0, 0)),
                  pl.BlockSpec(memory_space=pl.ANY),
                  pl.BlockSpec((1, d), lambda i: (0, 0))],
        out_specs=[prompt_tile(), sample_tile()],
        out_shape=[jax.ShapeDtypeStruct((n_p, d), F32), jax.ShapeDtypeStruct((n_s, d), F32)],
        scratch_shapes=[pltpu.VMEM((2, _local_rows(tm), d), BF16), pltpu.SemaphoreType.DMA((2,))],
        compiler_params=pltpu.CompilerParams(dimension_semantics=("arbitrary",), vmem_limit_bytes=VMEM_LIMIT),
        name="moe_combine",
    )(n16, lstart, gstart, n16, lstart, gstart, x2_prompt, x2_sample, route, lstart, ys, g_final)


TOKEN_TILE = 512
FOX_Q_TILE = 256
FOX_CACHE_TILE = 1024


def kernel(x_prompt, x_sample, cache_fox_k, cache_fox_v, cache_fox_logf, cache_band_k, cache_band_v, cache_mem_k, cache_mem_v, mem_prompt, g_mix, w_in, b_forget, g_out_fox, g_out_band, rel_table, w_out, g_cross, g_mem, w_cq, w_ck, w_cv, w_co, g_ffn, w_router1, b_router1, w_router2, b_router2, w_exp_gate, w_exp_up, w_exp_down, g_final):
    assert g_mix.shape[0] == 1, "single-layer model"
    bsz, seq, d = x_prompt.shape
    sb, st, _ = x_sample.shape
    n_s = sb * st
    past = cache_fox_k.shape[2]
    n_mem = mem_prompt.shape[1]
    row = lambda g: g.reshape(1, -1)

    w_pad, bf_pad, g_mix_r = _prep_proj(w_in[0], b_forget[0], g_mix[0])
    g_of, g_ob = row(g_out_fox[0]), row(g_out_band[0])
    bias_t, bias_s = _band_bias(_prep_band_bias_row(rel_table[0]))

    qx, kx, vat, qxb, kxb, vbt, kaf, vaf, kbf, vbf, logf = _proj(
        x_prompt, g_mix_r, w_pad, bf_pad, tm=TOKEN_TILE, prompt=True)
    a_p = _fox_prompt(qx, kx, vat, g_of, tq=FOX_Q_TILE)
    b_p = _band_prompt(qxb, kxb, vbt, bias_t, g_ob)

    s_out = _proj(x_sample.reshape(1, n_s, d), g_mix_r, w_pad, bf_pad, tm=n_s, prompt=False)
    sqa, ska, sva, sqb, skb, svb, skaf, svaf, skbf, svbf = (t.reshape(sb, st, W_GROUP) for t in s_out[:10])
    slogf = s_out[10].reshape(sb, st, N_HEADS)
    slft = s_out[11].reshape(N_HEADS, sb, st).transpose(1, 0, 2)
    a_s = _fox_sample(sqa, ska, sva, slft,
                      cache_fox_k[0].reshape(sb, past, W_GROUP), cache_fox_v[0].reshape(sb, past, W_GROUP),
                      cache_fox_logf[0].transpose(0, 2, 1), g_of, pt=FOX_CACHE_TILE)
    bp = cache_band_k.shape[2]
    b_s, nbk, nbv = _band_sample(sqb, skb, svb, skbf, svbf,
                                 cache_band_k[0].reshape(sb, bp, W_GROUP), cache_band_v[0].reshape(sb, bp, W_GROUP),
                                 bias_s, g_ob)

    w_ckv = jnp.concatenate([w_ck[0], w_cv[0]], axis=1).astype(BF16)
    mkf, mvf, mk, mv = _mem_kv(mem_prompt, row(g_mem[0]), w_ckv)
    post_w = _prep_post(w_out[0], g_cross[0], w_cq[0], w_co[0], g_ffn[0],
                        w_router1[0], b_router1[0], w_router2[0], b_router2[0])
    x2_p, h3_p, route_p, routet_p, cnt_p = _post_block(x_prompt, a_p, b_p, mk, mv, post_w, tm=TOKEN_TILE)
    cmk = cache_mem_k[0].reshape(sb, n_mem, W_MEM).astype(BF16)
    cmv = cache_mem_v[0].reshape(sb, n_mem, W_MEM).astype(BF16)
    x2_s, h3_s, route_s, routet_s, cnt_s = _post_block(x_sample, a_s, b_s, cmk, cmv, post_w, tm=TOKEN_TILE)
    route = jnp.concatenate([route_p, route_s], axis=0)
    route_t = jnp.concatenate([routet_p, routet_s], axis=1)

    plan = _plan(jnp.concatenate([cnt_p, cnt_s], axis=0))
    xs = _dispatch(h3_p, h3_s, route_t, plan, tm=TOKEN_TILE)
    ys = _experts(xs, plan[4], plan[3], w_exp_gate[0], w_exp_up[0], w_exp_down[0])
    y_p, y_s = _combine(x2_p, x2_s, route, plan, ys, row(g_final), tm=TOKEN_TILE)

    heads = lambda t, n: t.reshape(1, n, -1, N_HEADS, HEAD_DIM)
    mem_heads = lambda t: t.reshape(1, bsz, n_mem, N_HEADS_MEM, HEAD_DIM_MEM)
    return (y_p.reshape(bsz, seq, d), y_s.reshape(sb, st, d),
            heads(kaf, bsz), heads(vaf, bsz), logf.reshape(1, bsz, seq, N_HEADS),
            heads(kbf, bsz), heads(vbf, bsz), mem_heads(mkf), mem_heads(mvf),
            heads(skaf, sb), heads(svaf, sb), slogf.reshape(1, sb, st, N_HEADS),
            heads(nbk, sb), heads(nbv, sb))
```

```python
import functools

import jax
import jax.numpy as jnp
from jax import lax
from jax.experimental import pallas as pl
from jax.experimental.pallas import tpu as pltpu

F32 = jnp.float32
BF16 = jnp.bfloat16

D_MODEL = 1024
HEAD_DIM = 64
N_HEADS = 8
W_GROUP = N_HEADS * HEAD_DIM
N_PAIRS = N_HEADS // 2
CHUNK = 64
LEFT_CHUNKS = 8
LEFT = LEFT_CHUNKS * CHUNK
REL_CLIP = 128
EPS = 1e-6
NEG_INF = -1e30
ATTN_SCALE = HEAD_DIM ** -0.5
LANES = 128
PROJ_PAD = 3 * W_GROUP * 2 + LANES
VMEM_LIMIT = 56 * 1024 * 1024


def _rms(x, g):
    ms = jnp.mean(x * x, axis=-1, keepdims=True)
    return x * lax.rsqrt(ms + EPS) * g


def _log_sigmoid(x):
    return -(jnp.maximum(-x, 0.0) + jnp.log1p(jnp.exp(-jnp.abs(x))))


def _lane_cumsum(x):
    n = x.shape[-1]
    lane = lax.broadcasted_iota(jnp.int32, x.shape, 1)
    k = 1
    while k < n:
        x = x + jnp.where(lane >= k, pltpu.roll(x, k, axis=1), 0.0)
        k *= 2
    return x


LOG2E = 1.4426950408889634
SCALE_BASE2 = ATTN_SCALE * LOG2E


def _split3(x):
    hi = x.astype(BF16).astype(F32)
    mid = (x - hi).astype(BF16).astype(F32)
    lo = x - hi - mid
    return hi, mid, lo


def _extra_lane(parity):
    return HEAD_DIM if parity == 0 else 0


def _fox_extras(c3t, hp, tm):
    row = lax.broadcasted_iota(jnp.int32, (8, tm), 0)

    def group(h, q_side):
        hi, mid, lo = (p[h:h + 1, :] for p in c3t)
        if q_side:
            return jnp.where(row < 3, 1.0, jnp.where(row == 3, hi, jnp.where(row == 4, mid, jnp.where(row == 5, lo, 0.0))))
        return jnp.where(row == 0, -hi, jnp.where(row == 1, -mid, jnp.where(row == 2, -lo, jnp.where(row < 6, 1.0, 0.0))))

    gap = jnp.zeros((HEAD_DIM - 8, tm), F32)
    sides = []
    for q_side in (True, False):
        t = jnp.concatenate([group(2 * hp + 1, q_side), gap, group(2 * hp, q_side), gap], axis=0)
        sides.append(t.T)
    return sides


def _head_blocks(x128, extras, lane):
    return (jnp.where(lane < HEAD_DIM, x128, extras).astype(BF16),
            jnp.where(lane >= HEAD_DIM, x128, extras).astype(BF16))


Q_A, K_A, V_A, Q_B, K_B, V_B = range(6)


PROJ_CHAIN = 256


def _chains(tm, chain=PROJ_CHAIN):
    n = max(tm // chain, 1)
    return [pl.ds(i * (tm // n), tm // n) for i in range(n)]


def _proj_common(rows, x_ref, g_ref, w_ref, bf_ref, kaf_ref, vaf_ref, kbf_ref, vbf_ref, logf_ref, keep_tiles):
    s = pl.program_id(1)
    ns = pl.num_programs(1)
    h = _rms(x_ref[rows, :], g_ref[...]).astype(BF16)
    w = W_GROUP
    zf = jnp.dot(h, w_ref[:, 6 * w:6 * w + LANES], preferred_element_type=F32)
    z = [jnp.dot(h, w_ref[:, g * w:(g + 1) * w], preferred_element_type=F32) for g in range(6)]
    kaf_ref[rows, :] = z[K_A]
    vaf_ref[rows, :] = z[V_A]

    @pl.when(s >= ns - keep_tiles)
    def _():
        kbf_ref[rows, :] = z[K_B]
        vbf_ref[rows, :] = z[V_B]

    logf = _log_sigmoid(zf + bf_ref[...])
    logf_ref[rows, :] = logf[:, :N_HEADS]
    return z, logf


def _proj_prompt_body(x_ref, g_ref, w_ref, bf_ref, qx_ref, kx_ref, vat_ref, qxb_ref, kxb_ref, vbt_ref,
                      kaf_ref, vaf_ref, kbf_ref, vbf_ref, logf_ref, carry_ref, *, tm, keep_tiles):
    @pl.when(pl.program_id(1) == 0)
    def _():
        carry_ref[...] = jnp.zeros_like(carry_ref)

    for rows in _chains(tm):
        n = rows.size
        z, logf = _proj_common(rows, x_ref, g_ref, w_ref, bf_ref, kaf_ref, vaf_ref, kbf_ref, vbf_ref, logf_ref,
                               keep_tiles)
        vat_ref[:, rows] = z[V_A].T.astype(BF16)
        vbt_ref[:, rows] = z[V_B].T.astype(BF16)
        ct = _lane_cumsum(logf.T[:N_HEADS, :]) + carry_ref[:, 0:1]
        carry_ref[...] = jnp.broadcast_to(ct[:, n - 1:n], carry_ref.shape)
        c3t = _split3(ct * LOG2E)
        lane = lax.broadcasted_iota(jnp.int32, (n, LANES), 1)
        band_q_extras = jnp.where((lane == _extra_lane(0)) | (lane == _extra_lane(1)), 1.0, 0.0)
        band_k_extras = jnp.zeros((n, LANES), F32)
        for hp in range(N_PAIRS):
            blocks = slice(2 * hp * LANES, 2 * (hp + 1) * LANES)
            blk = lambda group, hp=hp, z=z: z[group][:, hp * LANES:(hp + 1) * LANES]
            q_extras, k_extras = _fox_extras(c3t, hp, n)
            qx_ref[rows, blocks] = jnp.concatenate(_head_blocks(blk(Q_A) * SCALE_BASE2, q_extras, lane), axis=1)
            kx_ref[rows, blocks] = jnp.concatenate(_head_blocks(blk(K_A), k_extras, lane), axis=1)
            qxb_ref[rows, blocks] = jnp.concatenate(_head_blocks(blk(Q_B) * SCALE_BASE2, band_q_extras, lane), axis=1)
            kxb_ref[rows, blocks] = jnp.concatenate(_head_blocks(blk(K_B), band_k_extras, lane), axis=1)


def _proj_sample_body(x_ref, g_ref, w_ref, bf_ref, qa_ref, ka_ref, va_ref, qb_ref, kb_ref, vb_ref,
                      kaf_ref, vaf_ref, kbf_ref, vbf_ref, logf_ref, lt_ref, *, tm, keep_tiles):
    for rows in _chains(tm):
        z, logf = _proj_common(rows, x_ref, g_ref, w_ref, bf_ref, kaf_ref, vaf_ref, kbf_ref, vbf_ref, logf_ref,
                               keep_tiles)
        qa_ref[rows, :] = (z[Q_A] * ATTN_SCALE).astype(BF16)
        ka_ref[rows, :] = z[K_A].astype(BF16)
        va_ref[rows, :] = z[V_A].astype(BF16)
        qb_ref[rows, :] = (z[Q_B] * ATTN_SCALE).astype(BF16)
        kb_ref[rows, :] = z[K_B].astype(BF16)
        vb_ref[rows, :] = z[V_B].astype(BF16)
        lt_ref[:, rows] = logf.T[:N_HEADS, :]


def _proj(x, g_mix, w_pad, bf_pad, *, tm, prompt):
    b, s, d = x.shape
    ns = s // tm
    keep = min(LEFT, s)
    assert s % tm == 0 and keep % tm == 0
    keep_tiles = keep // tm
    row = pl.BlockSpec((None, tm, W_GROUP), lambda i, j: (i, j, 0))
    wide = pl.BlockSpec((None, tm, N_HEADS * LANES), lambda i, j: (i, j, 0))
    col = pl.BlockSpec((None, W_GROUP, tm), lambda i, j: (i, 0, j))
    keep_spec = pl.BlockSpec((None, tm, W_GROUP), lambda i, j: (i, jnp.maximum(j - (ns - keep_tiles), 0), 0))
    heads_row = pl.BlockSpec((None, tm, N_HEADS), lambda i, j: (i, j, 0))
    heads_col = pl.BlockSpec((None, N_HEADS, tm), lambda i, j: (i, 0, j))
    const = lambda shape: pl.BlockSpec(shape, lambda i, j: (0,) * len(shape))
    rows_bf = jax.ShapeDtypeStruct((b, s, W_GROUP), BF16)
    wide_bf = jax.ShapeDtypeStruct((b, s, N_HEADS * LANES), BF16)
    cols_bf = jax.ShapeDtypeStruct((b, W_GROUP, s), BF16)
    f32_tail = [jax.ShapeDtypeStruct((b, s, W_GROUP), F32)] * 2
    f32_tail += [jax.ShapeDtypeStruct((b, keep, W_GROUP), F32)] * 2
    f32_tail += [jax.ShapeDtypeStruct((b, s, N_HEADS), F32)]
    tail_specs = [row, row, keep_spec, keep_spec, heads_row]
    if prompt:
        body = functools.partial(_proj_prompt_body, tm=tm, keep_tiles=keep_tiles)
        out_shape = [wide_bf, wide_bf, cols_bf, wide_bf, wide_bf, cols_bf] + f32_tail
        out_specs = [wide, wide, col, wide, wide, col] + tail_specs
        scratch = [pltpu.VMEM((N_HEADS, LANES), F32)]
    else:
        body = functools.partial(_proj_sample_body, tm=tm, keep_tiles=keep_tiles)
        out_shape = [rows_bf] * 6 + f32_tail + [jax.ShapeDtypeStruct((b, N_HEADS, s), F32)]
        out_specs = [row] * 6 + tail_specs + [heads_col]
        scratch = []
    return pl.pallas_call(
        body,
        grid=(b, ns),
        in_specs=[pl.BlockSpec((None, tm, d), lambda i, j: (i, j, 0)),
                  const((1, d)), const(w_pad.shape), const((1, LANES))],
        out_specs=out_specs,
        out_shape=out_shape,
        scratch_shapes=scratch,
        compiler_params=pltpu.CompilerParams(
            dimension_semantics=("parallel", "arbitrary"), vmem_limit_bytes=VMEM_LIMIT),
        name="proj",
    )(x, g_mix, w_pad, bf_pad)


def _prep_proj(w_in, b_forget, g_mix):
    cols = w_in.shape[-1]
    w_pad = jnp.pad(w_in, ((0, 0), (0, PROJ_PAD - cols))).astype(BF16)
    bf_pad = jnp.pad(b_forget.reshape(1, -1), ((0, 0), (0, LANES - N_HEADS))).astype(F32)
    return w_pad, bf_pad, g_mix.reshape(1, -1)


def _pair_masks():
    lane = lax.broadcasted_iota(jnp.int32, (1, LANES), 1)
    return lane < HEAD_DIM


def _head_q(q128, even_lanes, parity):
    keep = even_lanes if parity == 0 else jnp.logical_not(even_lanes)
    return jnp.where(keep, q128, jnp.zeros_like(q128))


def _head_v(v128, even_lanes, parity):
    keep = even_lanes if parity == 0 else jnp.logical_not(even_lanes)
    return jnp.where(keep, v128, jnp.ones_like(v128))


def _head_out(acc_even, acc_odd, even_lanes):
    inv_e = 1.0 / acc_even[:, HEAD_DIM:HEAD_DIM + 1]
    inv_o = 1.0 / acc_odd[:, 0:1]
    return jnp.where(even_lanes, acc_even * inv_e, acc_odd * inv_o)


_NT = (((1,), (1,)), ((), ()))


def _pair_rows():
    row = lax.broadcasted_iota(jnp.int32, (LANES, 1), 0)
    return row < HEAD_DIM


def _head_vt(vt128, even_rows, parity):
    keep = even_rows if parity == 0 else jnp.logical_not(even_rows)
    return jnp.where(keep, vt128, jnp.ones_like(vt128))


def _head_out_t(acc_even, acc_odd, even_rows):
    inv_e = 1.0 / acc_even[HEAD_DIM:HEAD_DIM + 1, :]
    inv_o = 1.0 / acc_odd[0:1, :]
    return jnp.where(even_rows, acc_even * inv_e, acc_odd * inv_o)


def _fox_body(qx_ref, kx_ref, vt_ref, g_ref, o_ref, s_scr, p_scr, m_scr, alpha_scr, acc_scr, ot_scr, *, tq):
    qi = pl.program_id(1)
    even_rows = _pair_rows()
    m_scr[...] = jnp.full(m_scr.shape, NEG_INF, F32)
    acc_scr[...] = jnp.zeros(acc_scr.shape, F32)
    key = lax.broadcasted_iota(jnp.int32, (tq, tq), 0)
    qry = lax.broadcasted_iota(jnp.int32, (tq, tq), 1)
    causal = key <= qry

    def logits(j, half):
        start = pl.multiple_of(j * tq, tq)
        for h in range(N_HEADS):
            head = slice(h * LANES, (h + 1) * LANES)
            s_scr[half, h] = lax.dot_general(kx_ref[pl.ds(start, tq), head], qx_ref[:, head], _NT,
                                             preferred_element_type=F32)

    def weigh(j, half, masked):
        start = pl.multiple_of(j * tq, tq)
        for h in range(N_HEADS):
            st = s_scr[half, h]
            if masked:
                st = jnp.where(causal, st, NEG_INF)
            m_old = m_scr[h:h + 1, :]
            m_new = jnp.maximum(m_old, jnp.max(st, axis=0, keepdims=True))
            p_scr[h] = jnp.exp2(st - m_new).astype(BF16)
            alpha_scr[h:h + 1, :] = jnp.exp2(m_old - m_new)
            m_scr[h:h + 1, :] = m_new
        for h in range(N_HEADS):
            pair = slice((h // 2) * LANES, (h // 2 + 1) * LANES)
            vt = _head_vt(vt_ref[pair, pl.ds(start, tq)], even_rows, h % 2)
            acc_scr[h] = acc_scr[h] * alpha_scr[h:h + 1, :] + jnp.dot(vt, p_scr[h], preferred_element_type=F32)

    def step(j, cur, nxt):
        logits(j + 1, nxt)
        weigh(j, cur, False)

    def body(jj, carry):
        step(2 * jj, 0, 1)
        step(2 * jj + 1, 1, 0)
        return carry

    logits(0, 0)
    lax.fori_loop(0, qi // 2, body, None)
    odd = lax.rem(qi, 2) == 1

    @pl.when(odd)
    def _():
        step(qi - 1, 0, 1)
        weigh(qi, 1, True)

    @pl.when(jnp.logical_not(odd))
    def _():
        weigh(qi, 0, True)
    for hp in range(N_PAIRS):
        ot_scr[hp * LANES:(hp + 1) * LANES, :] = _head_out_t(acc_scr[2 * hp], acc_scr[2 * hp + 1], even_rows)
    o_ref[...] = _rms(ot_scr[...].T, g_ref[...]).astype(BF16)


def _fox_prompt(qx, kx, vat, g_out, *, tq):
    b, s, wx = qx.shape
    w = vat.shape[1]
    return pl.pallas_call(
        functools.partial(_fox_body, tq=tq),
        grid=(b, s // tq),
        in_specs=[pl.BlockSpec((None, tq, wx), lambda i, j: (i, j, 0)),
                  pl.BlockSpec((None, s, wx), lambda i, j: (i, 0, 0)),
                  pl.BlockSpec((None, w, s), lambda i, j: (i, 0, 0)),
                  pl.BlockSpec((1, w), lambda i, j: (0, 0))],
        out_specs=pl.BlockSpec((None, tq, w), lambda i, j: (i, j, 0)),
        out_shape=jax.ShapeDtypeStruct((b, s, w), BF16),
        scratch_shapes=[pltpu.VMEM((2, N_HEADS, tq, tq), F32), pltpu.VMEM((N_HEADS, tq, tq), BF16),
                        pltpu.VMEM((N_HEADS, tq), F32), pltpu.VMEM((N_HEADS, tq), F32),
                        pltpu.VMEM((N_HEADS, LANES, tq), F32), pltpu.VMEM((w, tq), F32)],
        compiler_params=pltpu.CompilerParams(
            dimension_semantics=("parallel", "arbitrary"), vmem_limit_bytes=VMEM_LIMIT),
        name="fox_prompt",
    )(qx, kx, vat, g_out)


BAND_CHUNKS = 4
BAND_Q = BAND_CHUNKS * CHUNK
BAND_K = (LEFT_CHUNKS + BAND_CHUNKS) * CHUNK
BIAS_ROW = BAND_K + BAND_Q
BAND_K_SAMPLE = (LEFT + CHUNK + LANES - 1) // LANES * LANES


def _prep_band_bias_row(rel_table):
    pivot = LEFT + BAND_Q
    n_hi = pivot - REL_CLIP + 1
    n_mid = min(2 * REL_CLIP, BIAS_ROW - n_hi)
    n_lo = BIAS_ROW - n_hi - n_mid
    parts = [jnp.broadcast_to(rel_table[2 * REL_CLIP:], (n_hi, N_HEADS)),
             rel_table[2 * REL_CLIP - 1::-1][:n_mid],
             jnp.broadcast_to(rel_table[:1], (n_lo, N_HEADS))]
    return jnp.concatenate(parts, axis=0).T.reshape(N_HEADS, 1, BIAS_ROW)


def _band_bias_body(row_ref, bt_ref, bs_ref):
    rows = jnp.broadcast_to(row_ref[...], (BAND_Q, BIAS_ROW))
    skew = pltpu.roll(rows, 0, axis=1, stride=1, stride_axis=0)
    bias = skew[:, BAND_Q:]
    bs_ref[...] = bias[:CHUNK, :BAND_K_SAMPLE]
    qc = lax.broadcasted_iota(jnp.int32, (BAND_Q, BAND_K), 0) // CHUNK
    kc = lax.broadcasted_iota(jnp.int32, (BAND_Q, BAND_K), 1) // CHUNK
    bt_ref[...] = jnp.where((kc >= qc) & (kc <= qc + LEFT_CHUNKS), bias * LOG2E, NEG_INF).T


def _band_bias(bias_row):
    return pl.pallas_call(
        _band_bias_body,
        grid=(N_HEADS,),
        in_specs=[pl.BlockSpec((None, 1, BIAS_ROW), lambda h: (h, 0, 0))],
        out_specs=[pl.BlockSpec((None, BAND_K, BAND_Q), lambda h: (h, 0, 0)),
                   pl.BlockSpec((None, CHUNK, BAND_K_SAMPLE), lambda h: (h, 0, 0))],
        out_shape=[jax.ShapeDtypeStruct((N_HEADS, BAND_K, BAND_Q), F32),
                   jax.ShapeDtypeStruct((N_HEADS, CHUNK, BAND_K_SAMPLE), F32)],
        name="band_bias",
    )(bias_row)


def _band_body(qx_ref, kx_ref, vt_ref, bias_ref, g_ref, o_ref, kpad, vtpad, s_scr, p_scr, ot_scr, *, s_len):
    step = pl.program_id(1)

    @pl.when(step == 0)
    def _():
        lane = lax.broadcasted_iota(jnp.int32, (LEFT, 2 * LANES), 1)
        flags = (lane == _extra_lane(0)) | (lane == LANES + _extra_lane(1))
        pad_pair = jnp.where(flags, NEG_INF, 0.0).astype(BF16)
        for hp in range(N_PAIRS):
            kpad[0:LEFT, 2 * hp * LANES:2 * (hp + 1) * LANES] = pad_pair
        vtpad[:, 0:LEFT] = jnp.zeros((W_GROUP, LEFT), BF16)
        kpad[LEFT:LEFT + s_len, :] = kx_ref[...]
        vtpad[:, LEFT:LEFT + s_len] = vt_ref[...]

    even_rows = _pair_rows()
    start = pl.multiple_of(step * BAND_Q, BAND_Q)
    for h in range(N_HEADS):
        head = slice(h * LANES, (h + 1) * LANES)
        s_scr[h] = lax.dot_general(kpad[pl.ds(start, BAND_K), head], qx_ref[:, head], _NT,
                                   preferred_element_type=F32)
    for h in range(N_HEADS):
        st = s_scr[h] + bias_ref[h]
        p_scr[h] = jnp.exp2(st - jnp.max(st, axis=0, keepdims=True)).astype(BF16)
    for hp in range(N_PAIRS):
        pair = slice(hp * LANES, (hp + 1) * LANES)
        vtwin = vtpad[pair, pl.ds(start, BAND_K)]
        accs = [jnp.dot(_head_vt(vtwin, even_rows, parity), p_scr[2 * hp + parity], preferred_element_type=F32)
                for parity in range(2)]
        ot_scr[pair, :] = _head_out_t(accs[0], accs[1], even_rows)
    o_ref[...] = _rms(ot_scr[...].T, g_ref[...]).astype(BF16)


def _band_prompt(qxb, kxb, vbt, bias_t, g_out):
    b, s, wx = qxb.shape
    w = vbt.shape[1]
    return pl.pallas_call(
        functools.partial(_band_body, s_len=s),
        grid=(b, s // BAND_Q),
        in_specs=[pl.BlockSpec((None, BAND_Q, wx), lambda i, j: (i, j, 0)),
                  pl.BlockSpec((None, s, wx), lambda i, j: (i, 0, 0)),
                  pl.BlockSpec((None, w, s), lambda i, j: (i, 0, 0)),
                  pl.BlockSpec(bias_t.shape, lambda i, j: (0, 0, 0)),
                  pl.BlockSpec((1, w), lambda i, j: (0, 0))],
        out_specs=pl.BlockSpec((None, BAND_Q, w), lambda i, j: (i, j, 0)),
        out_shape=jax.ShapeDtypeStruct((b, s, w), BF16),
        scratch_shapes=[pltpu.VMEM((LEFT + s, wx), BF16), pltpu.VMEM((w, LEFT + s), BF16),
                        pltpu.VMEM((N_HEADS, BAND_K, BAND_Q), F32), pltpu.VMEM((N_HEADS, BAND_K, BAND_Q), BF16),
                        pltpu.VMEM((w, BAND_Q), F32)],
        compiler_params=pltpu.CompilerParams(
            dimension_semantics=("parallel", "arbitrary"), vmem_limit_bytes=VMEM_LIMIT),
        name="band_prompt",
    )(qxb, kxb, vbt, bias_t, g_out)


def _row_to_col(row):
    n = row.shape[-1]
    r = lax.broadcasted_iota(jnp.int32, (n, n), 0)
    c = lax.broadcasted_iota(jnp.int32, (n, n), 1)
    return jnp.sum(jnp.where(r == c, jnp.broadcast_to(row, (n, n)), 0.0), axis=-1, keepdims=True)


def _fox_sample_body(q_ref, kn_ref, vn_ref, lft_ref, kc_ref, vc_ref, clft_ref, g_ref, o_ref,
                     cct_scr, cn_scr, m_scr, acc_scr, o_scr, *, t_new, pt):
    p_idx = pl.program_id(1)
    n_p = pl.num_programs(1)
    even = _pair_masks()

    @pl.when(p_idx == 0)
    def _():
        cct = _lane_cumsum(clft_ref[...])
        cct_scr[...] = cct
        cn_scr[...] = _lane_cumsum(lft_ref[...]) + cct[:, cct.shape[1] - 1:]
        m_scr[...] = jnp.full(m_scr.shape, NEG_INF, F32)
        acc_scr[...] = jnp.zeros(acc_scr.shape, F32)

    def update(h, s, v):
        m = m_scr[h]
        m_new = jnp.maximum(m, jnp.max(s, axis=-1, keepdims=True))
        p = jnp.exp(s - m_new).astype(BF16)
        acc_scr[h] = acc_scr[h] * jnp.exp(m - m_new) + jnp.dot(p, v, preferred_element_type=F32)
        m_scr[h] = m_new

    start = pl.multiple_of(p_idx * pt, pt)
    for hp in range(N_PAIRS):
        lanes = slice(hp * LANES, (hp + 1) * LANES)
        q128 = q_ref[:, lanes]
        kc = kc_ref[:, lanes].astype(BF16)
        vc = vc_ref[:, lanes].astype(BF16)
        for parity in range(2):
            h = 2 * hp + parity
            cq = _row_to_col(cn_scr[h:h + 1, :])
            s = lax.dot_general(_head_q(q128, even, parity), kc, _NT, preferred_element_type=F32)
            update(h, s + cq - cct_scr[h:h + 1, pl.ds(start, pt)], _head_v(vc, even, parity))

    @pl.when(p_idx == n_p - 1)
    def _():
        row = lax.broadcasted_iota(jnp.int32, (t_new, t_new), 0)
        col = lax.broadcasted_iota(jnp.int32, (t_new, t_new), 1)
        for hp in range(N_PAIRS):
            lanes = slice(hp * LANES, (hp + 1) * LANES)
            q128 = q_ref[:, lanes]
            kn = kn_ref[:, lanes]
            vn = vn_ref[:, lanes]
            for parity in range(2):
                h = 2 * hp + parity
                cn_row = cn_scr[h:h + 1, :]
                s = lax.dot_general(_head_q(q128, even, parity), kn, _NT, preferred_element_type=F32)
                s = jnp.where(col <= row, s + _row_to_col(cn_row) - cn_row, NEG_INF)
                update(h, s, _head_v(vn, even, parity))
            o_scr[:, lanes] = _head_out(acc_scr[2 * hp], acc_scr[2 * hp + 1], even)
        o_ref[...] = _rms(o_scr[...], g_ref[...]).astype(BF16)


def _fox_sample(q, kn, vn, lft, kc, vc, clft, g_out, *, pt):
    b, t, w = q.shape
    p_len = kc.shape[1]
    new = lambda: pl.BlockSpec((None, t, w), lambda i, j: (i, 0, 0))
    cache = lambda: pl.BlockSpec((None, pt, w), lambda i, j: (i, j, 0))
    return pl.pallas_call(
        functools.partial(_fox_sample_body, t_new=t, pt=pt),
        grid=(b, p_len // pt),
        in_specs=[new(), new(), new(),
                  pl.BlockSpec((None, N_HEADS, t), lambda i, j: (i, 0, 0)),
                  cache(), cache(),
                  pl.BlockSpec((None, N_HEADS, p_len), lambda i, j: (i, 0, 0)),
                  pl.BlockSpec((1, w), lambda i, j: (0, 0))],
        out_specs=new(),
        out_shape=jax.ShapeDtypeStruct((b, t, w), BF16),
        scratch_shapes=[pltpu.VMEM((N_HEADS, p_len), F32), pltpu.VMEM((N_HEADS, t), F32),
                        pltpu.VMEM((N_HEADS, t, 1), F32), pltpu.VMEM((N_HEADS, t, LANES), F32),
                        pltpu.VMEM((t, w), F32)],
        compiler_params=pltpu.CompilerParams(
            dimension_semantics=("parallel", "arbitrary"), vmem_limit_bytes=VMEM_LIMIT),
        name="fox_sample",
    )(q, kn, vn, lft, kc, vc, clft, g_out)


def _band_sample_body(q_ref, kn_ref, vn_ref, knf_ref, vnf_ref, kc_ref, vc_ref, bias_ref, g_ref,
                      o_ref, nk_ref, nv_ref, kcat, vcat, o_scr, *, t_new, bp):
    kcat[0:bp, :] = kc_ref[...].astype(BF16)
    vcat[0:bp, :] = vc_ref[...].astype(BF16)
    kcat[bp:bp + t_new, :] = kn_ref[...]
    vcat[bp:bp + t_new, :] = vn_ref[...]
    nk_ref[0:bp - t_new, :] = kc_ref[t_new:bp, :]
    nv_ref[0:bp - t_new, :] = vc_ref[t_new:bp, :]
    nk_ref[bp - t_new:bp, :] = knf_ref[...]
    nv_ref[bp - t_new:bp, :] = vnf_ref[...]
    even = _pair_masks()
    for hp in range(N_PAIRS):
        lanes = slice(hp * LANES, (hp + 1) * LANES)
        q128 = q_ref[:, lanes]
        k = kcat[:, lanes]
        v = vcat[:, lanes]
        accs = []
        for parity in range(2):
            h = 2 * hp + parity
            s = lax.dot_general(_head_q(q128, even, parity), k, _NT, preferred_element_type=F32)
            s = s + bias_ref[h, 0:t_new, 0:bp + t_new]
            p = jnp.exp(s - jnp.max(s, axis=-1, keepdims=True)).astype(BF16)
            accs.append(jnp.dot(p, _head_v(v, even, parity), preferred_element_type=F32))
        o_scr[:, lanes] = _head_out(accs[0], accs[1], even)
    o_ref[...] = _rms(o_scr[...], g_ref[...]).astype(BF16)


def _band_sample(q, kn, vn, knf, vnf, kc, vc, bias, g_out):
    b, t, w = q.shape
    bp = kc.shape[1]
    assert t == CHUNK and bp == LEFT
    new = lambda: pl.BlockSpec((None, t, w), lambda i: (i, 0, 0))
    buf = lambda: pl.BlockSpec((None, bp, w), lambda i: (i, 0, 0))
    return pl.pallas_call(
        functools.partial(_band_sample_body, t_new=t, bp=bp),
        grid=(b,),
        in_specs=[new(), new(), new(), new(), new(), buf(), buf(),
                  pl.BlockSpec(bias.shape, lambda i: (0, 0, 0)),
                  pl.BlockSpec((1, w), lambda i: (0, 0))],
        out_specs=[new(), buf(), buf()],
        out_shape=[jax.ShapeDtypeStruct((b, t, w), BF16), jax.ShapeDtypeStruct((b, bp, w), F32),
                   jax.ShapeDtypeStruct((b, bp, w), F32)],
        scratch_shapes=[pltpu.VMEM((bp + t, w), BF16), pltpu.VMEM((bp + t, w), BF16), pltpu.VMEM((t, w), F32)],
        compiler_params=pltpu.CompilerParams(dimension_semantics=("parallel",), vmem_limit_bytes=VMEM_LIMIT),
        name="band_sample",
    )(q, kn, vn, knf, vnf, kc, vc, bias, g_out)


N_HEADS_MEM = 4
HEAD_DIM_MEM = 128
W_MEM = N_HEADS_MEM * HEAD_DIM_MEM
MEM_SCALE = HEAD_DIM_MEM ** -0.5


def _memkv_body(m_ref, g_ref, w_ref, kf_ref, vf_ref, k_ref, v_ref):
    h = _rms(m_ref[...], g_ref[...]).astype(BF16)
    z = jnp.dot(h, w_ref[...], preferred_element_type=F32)
    kf_ref[...] = z[:, :W_MEM]
    vf_ref[...] = z[:, W_MEM:]
    k_ref[...] = z[:, :W_MEM].astype(BF16)
    v_ref[...] = z[:, W_MEM:].astype(BF16)


def _mem_kv(mem, g_mem, w_ckv):
    b, n, d = mem.shape
    blk = lambda: pl.BlockSpec((None, n, W_MEM), lambda i: (i, 0, 0))
    return pl.pallas_call(
        _memkv_body,
        grid=(b,),
        in_specs=[pl.BlockSpec((None, n, d), lambda i: (i, 0, 0)),
                  pl.BlockSpec((1, d), lambda i: (0, 0)),
                  pl.BlockSpec(w_ckv.shape, lambda i: (0, 0))],
        out_specs=[blk(), blk(), blk(), blk()],
        out_shape=[jax.ShapeDtypeStruct((b, n, W_MEM), F32)] * 2 + [jax.ShapeDtypeStruct((b, n, W_MEM), BF16)] * 2,
        compiler_params=pltpu.CompilerParams(dimension_semantics=("parallel",), vmem_limit_bytes=VMEM_LIMIT),
        name="mem_kv",
    )(mem, g_mem, w_ckv)


N_GROUPS = 4
EXPERTS_PER_GROUP = 8
N_EXPERTS = N_GROUPS * EXPERTS_PER_GROUP
ROUTE_L2 = N_GROUPS
ROUTE_ROWS = 8
POST_CHAIN = 512
R_EID0, R_EID1, R_RANK0, R_RANK1, R_GATE0, R_GATE1 = range(6)


def _lane_max(x, mask):
    return jnp.max(jnp.where(mask, x, -jnp.inf), axis=-1, keepdims=True)


def _first_lane(mask, lane):
    return jnp.min(jnp.where(mask, lane, LANES), axis=-1, keepdims=True)


def _route(logits, lane):
    is_l1 = lane < N_GROUPS
    m1 = _lane_max(logits, is_l1)
    grp = _first_lane(is_l1 & (logits == m1), lane)
    wg = 1.0 / jnp.sum(jnp.where(is_l1, jnp.exp(logits - m1), 0.0), axis=-1, keepdims=True)
    lo = ROUTE_L2 + grp * EXPERTS_PER_GROUP
    in_grp = (lane >= lo) & (lane < lo + EXPERTS_PER_GROUP)
    v0 = _lane_max(logits, in_grp)
    i0 = _first_lane(in_grp & (logits == v0), lane)
    rest = in_grp & (lane != i0)
    v1 = _lane_max(logits, rest)
    i1 = _first_lane(rest & (logits == v1), lane)
    e1 = jnp.exp(v1 - v0)
    den = 1.0 / (1.0 + e1)
    return i0, i1, wg * den, wg * e1 * den


def _post_body(x_ref, a_ref, b_ref, mk_ref, mv_ref, woa_ref, wob_ref, gc_ref, wcq_ref, wco_ref,
               gf_ref, wrt_ref, brt_ref,
               x2_ref, h3_ref, route_ref, routet_ref, cnt_ref, o_scr, *, tm, nsub):
    seq = tm // nsub
    count = jnp.zeros((1, LANES), F32)
    for rows in _chains(tm, POST_CHAIN):
        n = rows.size
        x1 = (x_ref[rows, :] + jnp.dot(a_ref[rows, :], woa_ref[...], preferred_element_type=F32)
              + jnp.dot(b_ref[rows, :], wob_ref[...], preferred_element_type=F32))
        h2 = _rms(x1, gc_ref[...]).astype(BF16)
        qc = (jnp.dot(h2, wcq_ref[...], preferred_element_type=F32) * MEM_SCALE).astype(BF16)
        span = min(seq, n)
        for part in range(n // span):
            sub = (rows.start + part * span) // seq
            rs = slice(part * span, (part + 1) * span)
            orow = pl.ds(rows.start + part * span, span)
            for hm in range(N_HEADS_MEM):
                lanes = slice(hm * HEAD_DIM_MEM, (hm + 1) * HEAD_DIM_MEM)
                s = lax.dot_general(qc[rs, lanes], mk_ref[sub, :, lanes], _NT, preferred_element_type=F32)
                p = jnp.exp(s - jnp.max(s, axis=-1, keepdims=True))
                inv = 1.0 / jnp.sum(p, axis=-1, keepdims=True)
                o_scr[orow, lanes] = jnp.dot(p.astype(BF16), mv_ref[sub, :, lanes], preferred_element_type=F32) * inv
        x2 = x1 + jnp.dot(o_scr[rows, :].astype(BF16), wco_ref[...], preferred_element_type=F32)
        x2_ref[rows, :] = x2
        h3 = _rms(x2, gf_ref[...]).astype(BF16)
        h3_ref[rows, :] = h3

        logits = jnp.dot(h3, wrt_ref[...], preferred_element_type=F32) + brt_ref[...]
        lane = lax.broadcasted_iota(jnp.int32, (n, LANES), 1)
        i0, i1, g0, g1 = _route(logits, lane)
        e0 = i0 - ROUTE_L2
        e1 = i1 - ROUTE_L2
        hit0 = lane == e0
        hit1 = lane == e1
        onehot = jnp.where(hit0 | hit1, 1.0, 0.0)
        row = lax.broadcasted_iota(jnp.int32, (n, n), 0)
        col = lax.broadcasted_iota(jnp.int32, (n, n), 1)
        before = jnp.where(col < row, 1.0, 0.0).astype(BF16)
        seen = jnp.dot(before, onehot.astype(BF16), preferred_element_type=F32) + count
        rank0 = jnp.sum(jnp.where(hit0, seen, 0.0), axis=-1, keepdims=True)
        rank1 = jnp.sum(jnp.where(hit1, seen, 0.0), axis=-1, keepdims=True)
        count = count + jnp.sum(onehot, axis=0, keepdims=True)

        rec = jnp.zeros((n, LANES), F32)
        for idx, val in ((R_EID0, e0.astype(F32)), (R_EID1, e1.astype(F32)), (R_RANK0, rank0),
                         (R_RANK1, rank1), (R_GATE0, g0), (R_GATE1, g1)):
            rec = jnp.where(lane == idx, val, rec)
        route_ref[rows, :] = rec[:, :ROUTE_ROWS]
        routet_ref[:, rows] = rec.T[:ROUTE_ROWS, :]
    cnt_ref[...] = count


def _post_block(x, a_n, b_n, mk, mv, weights, *, tm):
    b, s, d = x.shape
    if s >= tm:
        nsub, grid = 1, (b, s // tm)
        tok = lambda i, j: (i, j, 0)
        flat = lambda i, j: i * (s // tm) + j
    else:
        nsub = tm // s
        assert b % nsub == 0
        x, a_n, b_n = (t.reshape(b // nsub, tm, t.shape[-1]) for t in (x, a_n, b_n))
        grid = (b // nsub, 1)
        tok = lambda i, j: (i, 0, 0)
        flat = lambda i, j: i
    mem = lambda i, j: (i, 0, 0)
    const = lambda arr: pl.BlockSpec(arr.shape, lambda i, j: (0,) * arr.ndim)
    in_specs = [pl.BlockSpec((None, tm, d), tok),
                pl.BlockSpec((None, tm, W_GROUP), tok), pl.BlockSpec((None, tm, W_GROUP), tok),
                pl.BlockSpec((nsub, mk.shape[1], W_MEM), mem), pl.BlockSpec((nsub, mv.shape[1], W_MEM), mem)]
    in_specs += [const(w) for w in weights]
    n = b * s
    out_shape = [jax.ShapeDtypeStruct((n, d), F32), jax.ShapeDtypeStruct((n, d), BF16),
                 jax.ShapeDtypeStruct((n, ROUTE_ROWS), F32), jax.ShapeDtypeStruct((ROUTE_ROWS, n), F32),
                 jax.ShapeDtypeStruct((n // tm, 1, LANES), F32)]
    out_specs = [pl.BlockSpec((tm, d), lambda i, j: (flat(i, j), 0)),
                 pl.BlockSpec((tm, d), lambda i, j: (flat(i, j), 0)),
                 pl.BlockSpec((tm, ROUTE_ROWS), lambda i, j: (flat(i, j), 0)),
                 pl.BlockSpec((ROUTE_ROWS, tm), lambda i, j: (0, flat(i, j))),
                 pl.BlockSpec((None, 1, LANES), lambda i, j: (flat(i, j), 0, 0))]
    return pl.pallas_call(
        functools.partial(_post_body, tm=tm, nsub=nsub),
        grid=grid,
        in_specs=in_specs,
        out_specs=out_specs,
        out_shape=out_shape,
        scratch_shapes=[pltpu.VMEM((tm, W_MEM), F32)],
        compiler_params=pltpu.CompilerParams(
            dimension_semantics=("parallel", "parallel"), vmem_limit_bytes=VMEM_LIMIT),
        name="post_block",
    )(x, a_n, b_n, mk, mv, *weights)


def _prep_post(w_out, g_cross, w_cq, w_co, g_ffn, w_r1, b_r1, w_r2, b_r2):
    pad = LANES - N_GROUPS - N_EXPERTS
    w_rt = jnp.pad(jnp.concatenate([w_r1, w_r2], axis=1), ((0, 0), (0, pad))).astype(BF16)
    b_rt = jnp.pad(jnp.concatenate([b_r1, b_r2]).reshape(1, -1), ((0, 0), (0, pad))).astype(F32)
    return [w_out[:W_GROUP].astype(BF16), w_out[W_GROUP:].astype(BF16), g_cross.reshape(1, -1),
            w_cq.astype(BF16), w_co.astype(BF16), g_ffn.reshape(1, -1), w_rt, b_rt]


D_EXPERT = 512
TOP_K = 2
ROW_TILE = 512
MXU_DIM = 256
RUN_ALIGN = 16
PLAN_TILES = LANES
TILE_TABLE = 2 * LANES


def _run_sizes(tm):
    sizes, s = [], RUN_ALIGN
    while s <= tm:
        sizes.append(s)
        s *= 2
    return tuple(reversed(sizes))


def _local_rows(tm):
    return -(-(TOP_K * tm + N_EXPERTS * (RUN_ALIGN - 1)) // MXU_DIM) * MXU_DIM


def _n_row_tiles(n_tokens, tm):
    rows = n_tokens * TOP_K + (n_tokens // tm) * N_EXPERTS * (RUN_ALIGN - 1) + N_EXPERTS * (ROW_TILE - 1)
    return rows // ROW_TILE


def _plan_body(cnt_ref, lstart_ref, gstart_ref, n16_ref, offs_ref, te_ref):
    cnt = cnt_ref[...].astype(jnp.int32)
    n16 = (cnt + (RUN_ALIGN - 1)) & ~(RUN_ALIGN - 1)
    n16f = n16.astype(F32)
    lstart = _lane_cumsum(n16f) - n16f
    earlier = (_lane_cumsum(n16f.T) - n16f.T).T
    total = jnp.sum(n16f, axis=0, keepdims=True).astype(jnp.int32)
    seg = jnp.broadcast_to((total + (ROW_TILE - 1)) & ~(ROW_TILE - 1), (8, LANES)).astype(F32)
    ends = _lane_cumsum(seg)
    offs = ends - seg
    n16_ref[...] = n16
    lstart_ref[...] = lstart.astype(jnp.int32)
    gstart_ref[...] = (earlier + offs[0:1, :]).astype(jnp.int32)
    offs_ref[...] = offs[0:1, :].astype(jnp.int32)
    lane = lax.broadcasted_iota(jnp.int32, (1, LANES), 1)
    tile_start = (lax.broadcasted_iota(jnp.int32, te_ref.shape, 1) * ROW_TILE).astype(F32)
    te = jnp.zeros(te_ref.shape, jnp.int32)
    for e in range(N_EXPERTS):
        end_e = jnp.sum(jnp.where(lane == e, ends[0:1, :], 0.0), axis=-1, keepdims=True)
        te = te + jnp.where(end_e <= tile_start, 1, 0)
    te_ref[...] = jnp.minimum(te, N_EXPERTS - 1)


def _plan(counts):
    nt = counts.shape[0]
    assert nt <= PLAN_TILES
    cnt = jnp.pad(counts.reshape(nt, LANES), ((0, PLAN_TILES - nt), (0, 0)))
    grid_i32 = jax.ShapeDtypeStruct((PLAN_TILES, LANES), jnp.int32)
    lstart, gstart, n16, offs, te = pl.pallas_call(
        _plan_body,
        out_shape=[grid_i32, grid_i32, grid_i32, jax.ShapeDtypeStruct((1, LANES), jnp.int32),
                   jax.ShapeDtypeStruct((1, TILE_TABLE), jnp.int32)],
        name="moe_plan",
    )(cnt)
    per_tile = lambda t: t[:nt].reshape(nt, 1, LANES)
    return per_tile(n16), per_tile(lstart), per_tile(gstart), offs.reshape(LANES), te.reshape(TILE_TABLE)


def _for_each_run_chunk(n16_ref, lstart_ref, gstart_ref, sizes, fn):
    def per_expert(e, carry):
        n = n16_ref[0, e]
        lo = lstart_ref[0, e]
        go = gstart_ref[0, e]
        for size in sizes:
            @pl.when((n & size) != 0)
            def _():
                fn(pl.multiple_of(lo, RUN_ALIGN), pl.multiple_of(go, RUN_ALIGN), size)
            lo = lo + (n & size)
            go = go + (n & size)
        return carry

    lax.fori_loop(0, N_EXPERTS, per_expert, None)


def _local_positions_row(rt_ref, lstart_ref):
    pos = []
    for r_eid, r_rank in ((R_EID0, R_RANK0), (R_EID1, R_RANK1)):
        eid = rt_ref[r_eid:r_eid + 1, :].astype(jnp.int32)
        p = rt_ref[r_rank:r_rank + 1, :].astype(jnp.int32)
        for e in range(N_EXPERTS):
            p = p + jnp.where(eid == e, lstart_ref[0, e], 0)
        pos.append(p)
    return pos


def _dispatch_body(offs_ref, n16_ref, lstart_ref, gstart_ref, pn16_ref, plstart_ref, pgstart_ref,
                   hp_ref, hs_ref, rt_ref, xs_ref, loc, zeros, sems, zsem, *, tm, n_prompt_tiles):
    i = pl.program_id(0)
    n_tiles = xs_ref.shape[0] // ROW_TILE
    half = lax.rem(i, 2)

    @pl.when(i == 0)
    def _():
        zeros[...] = jnp.zeros(zeros.shape, zeros.dtype)
        zero_tile = lambda row: pltpu.make_async_copy(
            zeros, xs_ref.at[pl.ds(pl.multiple_of(row, ROW_TILE), ROW_TILE)], zsem)
        n_used = offs_ref[N_EXPERTS] // ROW_TILE

        def tail(j, carry, op):
            op(zero_tile(j * ROW_TILE))
            return carry

        for op in (lambda c: c.start(), lambda c: c.wait()):
            for e in range(N_EXPERTS):
                @pl.when(offs_ref[e + 1] > offs_ref[e])
                def _():
                    op(zero_tile(offs_ref[e + 1] - ROW_TILE))
            lax.fori_loop(n_used, n_tiles, functools.partial(tail, op=op), None)

    pos0, pos1 = _local_positions_row(rt_ref, lstart_ref)
    slot = lax.broadcasted_iota(jnp.int32, (loc.shape[1], tm), 0)
    perm = jnp.where(slot == pos0, 1.0, jnp.where(slot == pos1, 1.0, 0.0)).astype(BF16)

    @pl.when(i < n_prompt_tiles)
    def _():
        loc[half] = jnp.dot(perm, hp_ref[...], preferred_element_type=F32).astype(BF16)

    @pl.when(i >= n_prompt_tiles)
    def _():
        loc[half] = jnp.dot(perm, hs_ref[...], preferred_element_type=F32).astype(BF16)

    def chunk(buf, lo, go, size):
        return pltpu.make_async_copy(loc.at[buf, pl.ds(lo, size)], xs_ref.at[pl.ds(go, size)], sems.at[buf])

    sizes = _run_sizes(tm)

    @pl.when(i > 0)
    def _():
        _for_each_run_chunk(pn16_ref, plstart_ref, pgstart_ref, sizes,
                            lambda lo, go, size: chunk(1 - half, lo, go, size).wait())

    _for_each_run_chunk(n16_ref, lstart_ref, gstart_ref, sizes, lambda lo, go, size: chunk(half, lo, go, size).start())

    @pl.when(i == pl.num_programs(0) - 1)
    def _():
        _for_each_run_chunk(n16_ref, lstart_ref, gstart_ref, sizes,
                            lambda lo, go, size: chunk(half, lo, go, size).wait())


def _dispatch(h3_prompt, h3_sample, route_t, plan, *, tm):
    n16, lstart, gstart, offs, _ = plan
    n_p, n_s, d = h3_prompt.shape[0], h3_sample.shape[0], h3_prompt.shape[-1]
    assert n_p % tm == 0 and n_s % tm == 0
    n = n_p + n_s
    nt = n // tm
    npt = n_p // tm
    n_rows = _n_row_tiles(n, tm) * ROW_TILE
    smem_tile = lambda: pl.BlockSpec((None, 1, LANES), lambda i, offs: (i, 0, 0), memory_space=pltpu.SMEM)
    smem_prev = lambda: pl.BlockSpec((None, 1, LANES), lambda i, offs: (jnp.maximum(i - 1, 0), 0, 0),
                                     memory_space=pltpu.SMEM)
    return pl.pallas_call(
        functools.partial(_dispatch_body, tm=tm, n_prompt_tiles=npt),
        grid_spec=pltpu.PrefetchScalarGridSpec(
            num_scalar_prefetch=1,
            grid=(nt,),
            in_specs=[smem_tile(), smem_tile(), smem_tile(), smem_prev(), smem_prev(), smem_prev(),
                      pl.BlockSpec((tm, d), lambda i, offs: (jnp.minimum(i, npt - 1), 0)),
                      pl.BlockSpec((tm, d), lambda i, offs: (jnp.maximum(i - npt, 0), 0)),
                      pl.BlockSpec((ROUTE_ROWS, tm), lambda i, offs: (0, i))],
            out_specs=pl.BlockSpec(memory_space=pl.ANY),
            scratch_shapes=[pltpu.VMEM((2, _local_rows(tm), d), BF16), pltpu.VMEM((ROW_TILE, d), BF16),
                            pltpu.SemaphoreType.DMA((2,)), pltpu.SemaphoreType.DMA(())]),
        out_shape=jax.ShapeDtypeStruct((n_rows, d), BF16),
        compiler_params=pltpu.CompilerParams(dimension_semantics=("arbitrary",), vmem_limit_bytes=VMEM_LIMIT),
        name="moe_dispatch",
    )(offs, n16, lstart, gstart, n16, lstart, gstart, h3_prompt, h3_sample, route_t)


def _experts_body(te_ref, offs_ref, xs_ref, wg_ref, wu_ref, wd_ref, ys_ref, wg_bf, wu_bf, wd_bf):
    i = pl.program_id(0)
    n_used = offs_ref[N_EXPERTS] // ROW_TILE

    @pl.when(i < n_used)
    def _():
        @pl.when((i == 0) | (te_ref[i] != te_ref[jnp.maximum(i - 1, 0)]))
        def _():
            wg_bf[...] = wg_ref[...].astype(BF16)
            wu_bf[...] = wu_ref[...].astype(BF16)
            wd_bf[...] = wd_ref[...].astype(BF16)

        dot = functools.partial(jnp.dot, preferred_element_type=F32)
        for rows in _chains(ROW_TILE):
            x = xs_ref[rows, :]
            gate = dot(x, wg_bf[...])
            up = dot(x, wu_bf[...])
            act = (gate * jax.nn.sigmoid(gate) * up).astype(BF16)
            ys_ref[rows, :] = dot(act, wd_bf[...]).astype(BF16)

    @pl.when(i >= n_used)
    def _():
        ys_ref[...] = jnp.zeros(ys_ref.shape, ys_ref.dtype)


def _experts(xs, te, offs, w_gate, w_up, w_down):
    n_rows, d = xs.shape
    last = lambda i, te, offs: jnp.minimum(i, offs[N_EXPERTS] // ROW_TILE - 1)
    wspec = lambda shape: pl.BlockSpec((None,) + shape, lambda i, te, offs: (te[last(i, te, offs)], 0, 0))
    return pl.pallas_call(
        _experts_body,
        grid_spec=pltpu.PrefetchScalarGridSpec(
            num_scalar_prefetch=2,
            grid=(n_rows // ROW_TILE,),
            in_specs=[pl.BlockSpec((ROW_TILE, d), lambda i, te, offs: (last(i, te, offs), 0)),
                      wspec((d, D_EXPERT)), wspec((d, D_EXPERT)), wspec((D_EXPERT, d))],
            out_specs=pl.BlockSpec((ROW_TILE, d), lambda i, te, offs: (i, 0)),
            scratch_shapes=[pltpu.VMEM((d, D_EXPERT), BF16), pltpu.VMEM((d, D_EXPERT), BF16),
                            pltpu.VMEM((D_EXPERT, d), BF16)]),
        out_shape=jax.ShapeDtypeStruct((n_rows, d), BF16),
        compiler_params=pltpu.CompilerParams(dimension_semantics=("arbitrary",), vmem_limit_bytes=VMEM_LIMIT),
        name="moe_experts",
    )(te, offs, xs, w_gate, w_up, w_down)


def _combine_body(n16_ref, lstart_ref, gstart_ref, nn16_ref, nlstart_ref, ngstart_ref,
                  x2p_ref, x2s_ref, route_ref, lsv_ref, ys_ref, g_ref,
                  yp_ref, ysm_ref, loc, sems, *, tm, n_prompt_tiles):
    i = pl.program_id(0)
    half = lax.rem(i, 2)
    sizes = _run_sizes(tm)

    def chunk(buf, lo, go, size):
        return pltpu.make_async_copy(ys_ref.at[pl.ds(go, size)], loc.at[buf, pl.ds(lo, size)], sems.at[buf])

    @pl.when(i == 0)
    def _():
        loc[...] = jnp.zeros(loc.shape, loc.dtype)
        _for_each_run_chunk(n16_ref, lstart_ref, gstart_ref, sizes, lambda lo, go, size: chunk(0, lo, go, size).start())

    @pl.when(i + 1 < pl.num_programs(0))
    def _():
        _for_each_run_chunk(nn16_ref, nlstart_ref, ngstart_ref, sizes,
                            lambda lo, go, size: chunk(1 - half, lo, go, size).start())

    _for_each_run_chunk(n16_ref, lstart_ref, gstart_ref, sizes, lambda lo, go, size: chunk(half, lo, go, size).wait())

    lane = lax.broadcasted_iota(jnp.int32, (tm, LANES), 1)
    slot = lax.broadcasted_iota(jnp.int32, (tm, loc.shape[1]), 1)
    weights = jnp.zeros(slot.shape, F32)
    for r_eid, r_rank, r_gate in ((R_EID0, R_RANK0, R_GATE0), (R_EID1, R_RANK1, R_GATE1)):
        eid = route_ref[:, r_eid:r_eid + 1].astype(jnp.int32)
        start = jnp.sum(jnp.where(lane == eid, lsv_ref[...], 0), axis=-1, keepdims=True)
        pos = route_ref[:, r_rank:r_rank + 1].astype(jnp.int32) + start
        weights = jnp.where(slot == pos, route_ref[:, r_gate:r_gate + 1], weights)
    moe = jnp.dot(weights.astype(BF16), loc[half], preferred_element_type=F32)

    @pl.when(i < n_prompt_tiles)
    def _():
        yp_ref[...] = _rms(x2p_ref[...] + moe, g_ref[...])

    @pl.when(i >= n_prompt_tiles)
    def _():
        ysm_ref[...] = _rms(x2s_ref[...] + moe, g_ref[...])


def _combine(x2_prompt, x2_sample, route, plan, ys, g_final, *, tm):
    n16, lstart, gstart, _, _ = plan
    (n_p, d), n_s = x2_prompt.shape, x2_sample.shape[0]
    assert n_p % tm == 0 and n_s % tm == 0
    npt = n_p // tm
    nt = npt + n_s // tm
    smem_tile = lambda: pl.BlockSpec((None, 1, LANES), lambda i: (i, 0, 0), memory_space=pltpu.SMEM)
    smem_next = lambda: pl.BlockSpec((None, 1, LANES), lambda i: (jnp.minimum(i + 1, nt - 1), 0, 0),
                                     memory_space=pltpu.SMEM)
    prompt_tile = lambda: pl.BlockSpec((tm, d), lambda i: (jnp.minimum(i, npt - 1), 0))
    sample_tile = lambda: pl.BlockSpec((tm, d), lambda i: (jnp.maximum(i - npt, 0), 0))
    return pl.pallas_call(
        functools.partial(_combine_body, tm=tm, n_prompt_tiles=npt),
        grid=(nt,),
        in_specs=[smem_tile(), smem_tile(), smem_tile(), smem_next(), smem_next(), smem_next(),
                  prompt_tile(), sample_tile(),
                  pl.BlockSpec((tm, ROUTE_ROWS), lambda i: (i, 0)),
                  pl.BlockSpec((None, 1, LANES), lambda i: (i, 0, 0)),
                  pl.BlockSpec(memory_space=pl.ANY),
                  pl.BlockSpec((1, d), lambda i: (0, 0))],
        out_specs=[prompt_tile(), sample_tile()],
        out_shape=[jax.ShapeDtypeStruct((n_p, d), F32), jax.ShapeDtypeStruct((n_s, d), F32)],
        scratch_shapes=[pltpu.VMEM((2, _local_rows(tm), d), BF16), pltpu.SemaphoreType.DMA((2,))],
        compiler_params=pltpu.CompilerParams(dimension_semantics=("arbitrary",), vmem_limit_bytes=VMEM_LIMIT),
        name="moe_combine",
    )(n16, lstart, gstart, n16, lstart, gstart, x2_prompt, x2_sample, route, lstart, ys, g_final)


TOKEN_TILE = 512
FOX_Q_TILE = 256
FOX_CACHE_TILE = 1024


def kernel(x_prompt, x_sample, cache_fox_k, cache_fox_v, cache_fox_logf, cache_band_k, cache_band_v, cache_mem_k, cache_mem_v, mem_prompt, g_mix, w_in, b_forget, g_out_fox, g_out_band, rel_table, w_out, g_cross, g_mem, w_cq, w_ck, w_cv, w_co, g_ffn, w_router1, b_router1, w_router2, b_router2, w_exp_gate, w_exp_up, w_exp_down, g_final):
    assert g_mix.shape[0] == 1, "single-layer model"
    bsz, seq, d = x_prompt.shape
    sb, st, _ = x_sample.shape
    n_s = sb * st
    past = cache_fox_k.shape[2]
    n_mem = mem_prompt.shape[1]
    row = lambda g: g.reshape(1, -1)

    w_pad, bf_pad, g_mix_r = _prep_proj(w_in[0], b_forget[0], g_mix[0])
    g_of, g_ob = row(g_out_fox[0]), row(g_out_band[0])
    bias_t, bias_s = _band_bias(_prep_band_bias_row(rel_table[0]))

    qx, kx, vat, qxb, kxb, vbt, kaf, vaf, kbf, vbf, logf = _proj(
        x_prompt, g_mix_r, w_pad, bf_pad, tm=TOKEN_TILE, prompt=True)
    a_p = _fox_prompt(qx, kx, vat, g_of, tq=FOX_Q_TILE)
    b_p = _band_prompt(qxb, kxb, vbt, bias_t, g_ob)

    s_out = _proj(x_sample.reshape(1, n_s, d), g_mix_r, w_pad, bf_pad, tm=n_s, prompt=False)
    sqa, ska, sva, sqb, skb, svb, skaf, svaf, skbf, svbf = (t.reshape(sb, st, W_GROUP) for t in s_out[:10])
    slogf = s_out[10].reshape(sb, st, N_HEADS)
    slft = s_out[11].reshape(N_HEADS, sb, st).transpose(1, 0, 2)
    a_s = _fox_sample(sqa, ska, sva, slft,
                      cache_fox_k[0].reshape(sb, past, W_GROUP), cache_fox_v[0].reshape(sb, past, W_GROUP),
                      cache_fox_logf[0].transpose(0, 2, 1), g_of, pt=FOX_CACHE_TILE)
    bp = cache_band_k.shape[2]
    b_s, nbk, nbv = _band_sample(sqb, skb, svb, skbf, svbf,
                                 cache_band_k[0].reshape(sb, bp, W_GROUP), cache_band_v[0].reshape(sb, bp, W_GROUP),
                                 bias_s, g_ob)

    w_ckv = jnp.concatenate([w_ck[0], w_cv[0]], axis=1).astype(BF16)
    mkf, mvf, mk, mv = _mem_kv(mem_prompt, row(g_mem[0]), w_ckv)
    post_w = _prep_post(w_out[0], g_cross[0], w_cq[0], w_co[0], g_ffn[0],
                        w_router1[0], b_router1[0], w_router2[0], b_router2[0])
    x2_p, h3_p, route_p, routet_p, cnt_p = _post_block(x_prompt, a_p, b_p, mk, mv, post_w, tm=TOKEN_TILE)
    cmk = cache_mem_k[0].reshape(sb, n_mem, W_MEM).astype(BF16)
    cmv = cache_mem_v[0].reshape(sb, n_mem, W_MEM).astype(BF16)
    x2_s, h3_s, route_s, routet_s, cnt_s = _post_block(x_sample, a_s, b_s, cmk, cmv, post_w, tm=TOKEN_TILE)
    route = jnp.concatenate([route_p, route_s], axis=0)
    route_t = jnp.concatenate([routet_p, routet_s], axis=1)

    plan = _plan(jnp.concatenate([cnt_p, cnt_s], axis=0))
    xs = _dispatch(h3_p, h3_s, route_t, plan, tm=TOKEN_TILE)
    ys = _experts(xs, plan[4], plan[3], w_exp_gate[0], w_exp_up[0], w_exp_down[0])
    y_p, y_s = _combine(x2_p, x2_s, route, plan, ys, row(g_final), tm=TOKEN_TILE)

    heads = lambda t, n: t.reshape(1, n, -1, N_HEADS, HEAD_DIM)
    mem_heads = lambda t: t.reshape(1, bsz, n_mem, N_HEADS_MEM, HEAD_DIM_MEM)
    return (y_p.reshape(bsz, seq, d), y_s.reshape(sb, st, d),
            heads(kaf, bsz), heads(vaf, bsz), logf.reshape(1, bsz, seq, N_HEADS),
            heads(kbf, bsz), heads(vbf, bsz), mem_heads(mkf), mem_heads(mvf),
            heads(skaf, sb), heads(svaf, sb), slogf.reshape(1, sb, st, N_HEADS),
            heads(nbk, sb), heads(nbv, sb))
```

```python
import functools

import jax
import jax.numpy as jnp
from jax import lax
from jax.experimental import pallas as pl
from jax.experimental.pallas import tpu as pltpu

F32 = jnp.float32
BF16 = jnp.bfloat16

D_MODEL = 1024
HEAD_DIM = 64
N_HEADS = 8
W_GROUP = N_HEADS * HEAD_DIM
N_PAIRS = N_HEADS // 2
CHUNK = 64
LEFT_CHUNKS = 8
LEFT = LEFT_CHUNKS * CHUNK
REL_CLIP = 128
EPS = 1e-6
NEG_INF = -1e30
ATTN_SCALE = HEAD_DIM ** -0.5
LANES = 128
PROJ_PAD = 3 * W_GROUP * 2 + LANES
VMEM_LIMIT = 56 * 1024 * 1024


def _rms(x, g):
    ms = jnp.mean(x * x, axis=-1, keepdims=True)
    return x * lax.rsqrt(ms + EPS) * g


def _log_sigmoid(x):
    return -(jnp.maximum(-x, 0.0) + jnp.log1p(jnp.exp(-jnp.abs(x))))


def _lane_cumsum(x):
    n = x.shape[-1]
    lane = lax.broadcasted_iota(jnp.int32, x.shape, 1)
    k = 1
    while k < n:
        x = x + jnp.where(lane >= k, pltpu.roll(x, k, axis=1), 0.0)
        k *= 2
    return x


LOG2E = 1.4426950408889634
SCALE_BASE2 = ATTN_SCALE * LOG2E


def _split3(x):
    hi = x.astype(BF16).astype(F32)
    mid = (x - hi).astype(BF16).astype(F32)
    lo = x - hi - mid
    return hi, mid, lo


def _extra_lane(parity):
    return HEAD_DIM if parity == 0 else 0


def _fox_extras(c3t, hp, tm):
    row = lax.broadcasted_iota(jnp.int32, (8, tm), 0)

    def group(h, q_side):
        hi, mid, lo = (p[h:h + 1, :] for p in c3t)
        if q_side:
            return jnp.where(row < 3, 1.0, jnp.where(row == 3, hi, jnp.where(row == 4, mid, jnp.where(row == 5, lo, 0.0))))
        return jnp.where(row == 0, -hi, jnp.where(row == 1, -mid, jnp.where(row == 2, -lo, jnp.where(row < 6, 1.0, 0.0))))

    gap = jnp.zeros((HEAD_DIM - 8, tm), F32)
    sides = []
    for q_side in (True, False):
        t = jnp.concatenate([group(2 * hp + 1, q_side), gap, group(2 * hp, q_side), gap], axis=0)
        sides.append(t.T)
    return sides


def _head_blocks(x128, extras, lane):
    return (jnp.where(lane < HEAD_DIM, x128, extras).astype(BF16),
            jnp.where(lane >= HEAD_DIM, x128, extras).astype(BF16))


Q_A, K_A, V_A, Q_B, K_B, V_B = range(6)


PROJ_CHAIN = 256


def _chains(tm, chain=PROJ_CHAIN):
    n = max(tm // chain, 1)
    return [pl.ds(i * (tm // n), tm // n) for i in range(n)]


def _proj_common(rows, x_ref, g_ref, w_ref, bf_ref, kaf_ref, vaf_ref, kbf_ref, vbf_ref, logf_ref, keep_tiles):
    s = pl.program_id(1)
    ns = pl.num_programs(1)
    h = _rms(x_ref[rows, :], g_ref[...]).astype(BF16)
    w = W_GROUP
    zf = jnp.dot(h, w_ref[:, 6 * w:6 * w + LANES], preferred_element_type=F32)
    z = [jnp.dot(h, w_ref[:, g * w:(g + 1) * w], preferred_element_type=F32) for g in range(6)]
    kaf_ref[rows, :] = z[K_A]
    vaf_ref[rows, :] = z[V_A]

    @pl.when(s >= ns - keep_tiles)
    def _():
        kbf_ref[rows, :] = z[K_B]
        vbf_ref[rows, :] = z[V_B]

    logf = _log_sigmoid(zf + bf_ref[...])
    logf_ref[rows, :] = logf[:, :N_HEADS]
    return z, logf


def _proj_prompt_body(x_ref, g_ref, w_ref, bf_ref, qx_ref, kx_ref, vat_ref, qxb_ref, kxb_ref, vbt_ref,
                      kaf_ref, vaf_ref, kbf_ref, vbf_ref, logf_ref, carry_ref, *, tm, keep_tiles):
    @pl.when(pl.program_id(1) == 0)
    def _():
        carry_ref[...] = jnp.zeros_like(carry_ref)

    for rows in _chains(tm):
        n = rows.size
        z, logf = _proj_common(rows, x_ref, g_ref, w_ref, bf_ref, kaf_ref, vaf_ref, kbf_ref, vbf_ref, logf_ref,
                               keep_tiles)
        vat_ref[:, rows] = z[V_A].T.astype(BF16)
        vbt_ref[:, rows] = z[V_B].T.astype(BF16)
        ct = _lane_cumsum(logf.T[:N_HEADS, :]) + carry_ref[:, 0:1]
        carry_ref[...] = jnp.broadcast_to(ct[:, n - 1:n], carry_ref.shape)
        c3t = _split3(ct * LOG2E)
        lane = lax.broadcasted_iota(jnp.int32, (n, LANES), 1)
        band_q_extras = jnp.where((lane == _extra_lane(0)) | (lane == _extra_lane(1)), 1.0, 0.0)
        band_k_extras = jnp.zeros((n, LANES), F32)
        for hp in range(N_PAIRS):
            blocks = slice(2 * hp * LANES, 2 * (hp + 1) * LANES)
            blk = lambda group, hp=hp, z=z: z[group][:, hp * LANES:(hp + 1) * LANES]
            q_extras, k_extras = _fox_extras(c3t, hp, n)
            qx_ref[rows, blocks] = jnp.concatenate(_head_blocks(blk(Q_A) * SCALE_BASE2, q_extras, lane), axis=1)
            kx_ref[rows, blocks] = jnp.concatenate(_head_blocks(blk(K_A), k_extras, lane), axis=1)
            qxb_ref[rows, blocks] = jnp.concatenate(_head_blocks(blk(Q_B) * SCALE_BASE2, band_q_extras, lane), axis=1)
            kxb_ref[rows, blocks] = jnp.concatenate(_head_blocks(blk(K_B), band_k_extras, lane), axis=1)


def _proj_sample_body(x_ref, g_ref, w_ref, bf_ref, qa_ref, ka_ref, va_ref, qb_ref, kb_ref, vb_ref,
                      kaf_ref, vaf_ref, kbf_ref, vbf_ref, logf_ref, lt_ref, *, tm, keep_tiles):
    for rows in _chains(tm):
        z, logf = _proj_common(rows, x_ref, g_ref, w_ref, bf_ref, kaf_ref, vaf_ref, kbf_ref, vbf_ref, logf_ref,
                               keep_tiles)
        qa_ref[rows, :] = (z[Q_A] * ATTN_SCALE).astype(BF16)
        ka_ref[rows, :] = z[K_A].astype(BF16)
        va_ref[rows, :] = z[V_A].astype(BF16)
        qb_ref[rows, :] = (z[Q_B] * ATTN_SCALE).astype(BF16)
        kb_ref[rows, :] = z[K_B].astype(BF16)
        vb_ref[rows, :] = z[V_B].astype(BF16)
        lt_ref[:, rows] = logf.T[:N_HEADS, :]


def _proj(x, g_mix, w_pad, bf_pad, *, tm, prompt):
    b, s, d = x.shape
    ns = s // tm
    keep = min(LEFT, s)
    assert s % tm == 0 and keep % tm == 0
    keep_tiles = keep // tm
    row = pl.BlockSpec((None, tm, W_GROUP), lambda i, j: (i, j, 0))
    wide = pl.BlockSpec((None, tm, N_HEADS * LANES), lambda i, j: (i, j, 0))
    col = pl.BlockSpec((None, W_GROUP, tm), lambda i, j: (i, 0, j))
    keep_spec = pl.BlockSpec((None, tm, W_GROUP), lambda i, j: (i, jnp.maximum(j - (ns - keep_tiles), 0), 0))
    heads_row = pl.BlockSpec((None, tm, N_HEADS), lambda i, j: (i, j, 0))
    heads_col = pl.BlockSpec((None, N_HEADS, tm), lambda i, j: (i, 0, j))
    const = lambda shape: pl.BlockSpec(shape, lambda i, j: (0,) * len(shape))
    rows_bf = jax.ShapeDtypeStruct((b, s, W_GROUP), BF16)
    wide_bf = jax.ShapeDtypeStruct((b, s, N_HEADS * LANES), BF16)
    cols_bf = jax.ShapeDtypeStruct((b, W_GROUP, s), BF16)
    f32_tail = [jax.ShapeDtypeStruct((b, s, W_GROUP), F32)] * 2
    f32_tail += [jax.ShapeDtypeStruct((b, keep, W_GROUP), F32)] * 2
    f32_tail += [jax.ShapeDtypeStruct((b, s, N_HEADS), F32)]
    tail_specs = [row, row, keep_spec, keep_spec, heads_row]
    if prompt:
        body = functools.partial(_proj_prompt_body, tm=tm, keep_tiles=keep_tiles)
        out_shape = [wide_bf, wide_bf, cols_bf, wide_bf, wide_bf, cols_bf] + f32_tail
        out_specs = [wide, wide, col, wide, wide, col] + tail_specs
        scratch = [pltpu.VMEM((N_HEADS, LANES), F32)]
    else:
        body = functools.partial(_proj_sample_body, tm=tm, keep_tiles=keep_tiles)
        out_shape = [rows_bf] * 6 + f32_tail + [jax.ShapeDtypeStruct((b, N_HEADS, s), F32)]
        out_specs = [row] * 6 + tail_specs + [heads_col]
        scratch = []
    return pl.pallas_call(
        body,
        grid=(b, ns),
        in_specs=[pl.BlockSpec((None, tm, d), lambda i, j: (i, j, 0)),
                  const((1, d)), const(w_pad.shape), const((1, LANES))],
        out_specs=out_specs,
        out_shape=out_shape,
        scratch_shapes=scratch,
        compiler_params=pltpu.CompilerParams(
            dimension_semantics=("parallel", "arbitrary"), vmem_limit_bytes=VMEM_LIMIT),
        name="proj",
    )(x, g_mix, w_pad, bf_pad)


def _prep_proj(w_in, b_forget, g_mix):
    cols = w_in.shape[-1]
    w_pad = jnp.pad(w_in, ((0, 0), (0, PROJ_PAD - cols))).astype(BF16)
    bf_pad = jnp.pad(b_forget.reshape(1, -1), ((0, 0), (0, LANES - N_HEADS))).astype(F32)
    return w_pad, bf_pad, g_mix.reshape(1, -1)


def _pair_masks():
    lane = lax.broadcasted_iota(jnp.int32, (1, LANES), 1)
    return lane < HEAD_DIM


def _head_q(q128, even_lanes, parity):
    keep = even_lanes if parity == 0 else jnp.logical_not(even_lanes)
    return jnp.where(keep, q128, jnp.zeros_like(q128))


def _head_v(v128, even_lanes, parity):
    keep = even_lanes if parity == 0 else jnp.logical_not(even_lanes)
    return jnp.where(keep, v128, jnp.ones_like(v128))


def _head_out(acc_even, acc_odd, even_lanes):
    inv_e = 1.0 / acc_even[:, HEAD_DIM:HEAD_DIM + 1]
    inv_o = 1.0 / acc_odd[:, 0:1]
    return jnp.where(even_lanes, acc_even * inv_e, acc_odd * inv_o)


_NT = (((1,), (1,)), ((), ()))


def _pair_rows():
    row = lax.broadcasted_iota(jnp.int32, (LANES, 1), 0)
    return row < HEAD_DIM


def _head_vt(vt128, even_rows, parity):
    keep = even_rows if parity == 0 else jnp.logical_not(even_rows)
    return jnp.where(keep, vt128, jnp.ones_like(vt128))


def _head_out_t(acc_even, acc_odd, even_rows):
    inv_e = 1.0 / acc_even[HEAD_DIM:HEAD_DIM + 1, :]
    inv_o = 1.0 / acc_odd[0:1, :]
    return jnp.where(even_rows, acc_even * inv_e, acc_odd * inv_o)


def _fox_body(qx_ref, kx_ref, vt_ref, g_ref, o_ref, s_scr, p_scr, m_scr, alpha_scr, acc_scr, ot_scr, *, tq):
    qi = pl.program_id(1)
    even_rows = _pair_rows()
    m_scr[...] = jnp.full(m_scr.shape, NEG_INF, F32)
    acc_scr[...] = jnp.zeros(acc_scr.shape, F32)
    key = lax.broadcasted_iota(jnp.int32, (tq, tq), 0)
    qry = lax.broadcasted_iota(jnp.int32, (tq, tq), 1)
    causal = key <= qry

    def logits(j, half):
        start = pl.multiple_of(j * tq, tq)
        for h in range(N_HEADS):
            head = slice(h * LANES, (h + 1) * LANES)
            s_scr[half, h] = lax.dot_general(kx_ref[pl.ds(start, tq), head], qx_ref[:, head], _NT,
                                             preferred_element_type=F32)

    def weigh(j, half, masked):
        start = pl.multiple_of(j * tq, tq)
        for h in range(N_HEADS):
            st = s_scr[half, h]
            if masked:
                st = jnp.where(causal, st, NEG_INF)
            m_old = m_scr[h:h + 1, :]
            m_new = jnp.maximum(m_old, jnp.max(st, axis=0, keepdims=True))
            p_scr[h] = jnp.exp2(st - m_new).astype(BF16)
            alpha_scr[h:h + 1, :] = jnp.exp2(m_old - m_new)
            m_scr[h:h + 1, :] = m_new
        for h in range(N_HEADS):
            pair = slice((h // 2) * LANES, (h // 2 + 1) * LANES)
            vt = _head_vt(vt_ref[pair, pl.ds(start, tq)], even_rows, h % 2)
            acc_scr[h] = acc_scr[h] * alpha_scr[h:h + 1, :] + jnp.dot(vt, p_scr[h], preferred_element_type=F32)

    def step(j, cur, nxt):
        logits(j + 1, nxt)
        weigh(j, cur, False)

    def body(jj, carry):
        step(2 * jj, 0, 1)
        step(2 * jj + 1, 1, 0)
        return carry

    logits(0, 0)
    lax.fori_loop(0, qi // 2, body, None)
    odd = lax.rem(qi, 2) == 1

    @pl.when(odd)
    def _():
        step(qi - 1, 0, 1)
        weigh(qi, 1, True)

    @pl.when(jnp.logical_not(odd))
    def _():
        weigh(qi, 0, True)
    for hp in range(N_PAIRS):
        ot_scr[hp * LANES:(hp + 1) * LANES, :] = _head_out_t(acc_scr[2 * hp], acc_scr[2 * hp + 1], even_rows)
    o_ref[...] = _rms(ot_scr[...].T, g_ref[...]).astype(BF16)


def _fox_prompt(qx, kx, vat, g_out, *, tq):
    b, s, wx = qx.shape
    w = vat.shape[1]
    return pl.pallas_call(
        functools.partial(_fox_body, tq=tq),
        grid=(b, s // tq),
        in_specs=[pl.BlockSpec((None, tq, wx), lambda i, j: (i, j, 0)),
                  pl.BlockSpec((None, s, wx), lambda i, j: (i, 0, 0)),
                  pl.BlockSpec((None, w, s), lambda i, j: (i, 0, 0)),
                  pl.BlockSpec((1, w), lambda i, j: (0, 0))],
        out_specs=pl.BlockSpec((None, tq, w), lambda i, j: (i, j, 0)),
        out_shape=jax.ShapeDtypeStruct((b, s, w), BF16),
        scratch_shapes=[pltpu.VMEM((2, N_HEADS, tq, tq), F32), pltpu.VMEM((N_HEADS, tq, tq), BF16),
                        pltpu.VMEM((N_HEADS, tq), F32), pltpu.VMEM((N_HEADS, tq), F32),
                        pltpu.VMEM((N_HEADS, LANES, tq), F32), pltpu.VMEM((w, tq), F32)],
        compiler_params=pltpu.CompilerParams(
            dimension_semantics=("parallel", "arbitrary"), vmem_limit_bytes=VMEM_LIMIT),
        name="fox_prompt",
    )(qx, kx, vat, g_out)


BAND_CHUNKS = 4
BAND_Q = BAND_CHUNKS * CHUNK
BAND_K = (LEFT_CHUNKS + BAND_CHUNKS) * CHUNK
BIAS_ROW = BAND_K + BAND_Q
BAND_K_SAMPLE = (LEFT + CHUNK + LANES - 1) // LANES * LANES


def _prep_band_bias_row(rel_table):
    pivot = LEFT + BAND_Q
    n_hi = pivot - REL_CLIP + 1
    n_mid = min(2 * REL_CLIP, BIAS_ROW - n_hi)
    n_lo = BIAS_ROW - n_hi - n_mid
    parts = [jnp.broadcast_to(rel_table[2 * REL_CLIP:], (n_hi, N_HEADS)),
             rel_table[2 * REL_CLIP - 1::-1][:n_mid],
             jnp.broadcast_to(rel_table[:1], (n_lo, N_HEADS))]
    return jnp.concatenate(parts, axis=0).T.reshape(N_HEADS, 1, BIAS_ROW)


def _band_bias_body(row_ref, bt_ref, bs_ref):
    rows = jnp.broadcast_to(row_ref[...], (BAND_Q, BIAS_ROW))
    skew = pltpu.roll(rows, 0, axis=1, stride=1, stride_axis=0)
    bias = skew[:, BAND_Q:]
    bs_ref[...] = bias[:CHUNK, :BAND_K_SAMPLE]
    qc = lax.broadcasted_iota(jnp.int32, (BAND_Q, BAND_K), 0) // CHUNK
    kc = lax.broadcasted_iota(jnp.int32, (BAND_Q, BAND_K), 1) // CHUNK
    bt_ref[...] = jnp.where((kc >= qc) & (kc <= qc + LEFT_CHUNKS), bias * LOG2E, NEG_INF).T


def _band_bias(bias_row):
    return pl.pallas_call(
        _band_bias_body,
        grid=(N_HEADS,),
        in_specs=[pl.BlockSpec((None, 1, BIAS_ROW), lambda h: (h, 0, 0))],
        out_specs=[pl.BlockSpec((None, BAND_K, BAND_Q), lambda h: (h, 0, 0)),
                   pl.BlockSpec((None, CHUNK, BAND_K_SAMPLE), lambda h: (h, 0, 0))],
        out_shape=[jax.ShapeDtypeStruct((N_HEADS, BAND_K, BAND_Q), F32),
                   jax.ShapeDtypeStruct((N_HEADS, CHUNK, BAND_K_SAMPLE), F32)],
        name="band_bias",
    )(bias_row)


def _band_body(qx_ref, kx_ref, vt_ref, bias_ref, g_ref, o_ref, kpad, vtpad, s_scr, p_scr, ot_scr, *, s_len):
    step = pl.program_id(1)

    @pl.when(step == 0)
    def _():
        lane = lax.broadcasted_iota(jnp.int32, (LEFT, 2 * LANES), 1)
        flags = (lane == _extra_lane(0)) | (lane == LANES + _extra_lane(1))
        pad_pair = jnp.where(flags, NEG_INF, 0.0).astype(BF16)
        for hp in range(N_PAIRS):
            kpad[0:LEFT, 2 * hp * LANES:2 * (hp + 1) * LANES] = pad_pair
        vtpad[:, 0:LEFT] = jnp.zeros((W_GROUP, LEFT), BF16)
        kpad[LEFT:LEFT + s_len, :] = kx_ref[...]
        vtpad[:, LEFT:LEFT + s_len] = vt_ref[...]

    even_rows = _pair_rows()
    start = pl.multiple_of(step * BAND_Q, BAND_Q)
    for h in range(N_HEADS):
        head = slice(h * LANES, (h + 1) * LANES)
        s_scr[h] = lax.dot_general(kpad[pl.ds(start, BAND_K), head], qx_ref[:, head], _NT,
                                   preferred_element_type=F32)
    for h in range(N_HEADS):
        st = s_scr[h] + bias_ref[h]
        p_scr[h] = jnp.exp2(st - jnp.max(st, axis=0, keepdims=True)).astype(BF16)
    for hp in range(N_PAIRS):
        pair = slice(hp * LANES, (hp + 1) * LANES)
        vtwin = vtpad[pair, pl.ds(start, BAND_K)]
        accs = [jnp.dot(_head_vt(vtwin, even_rows, parity), p_scr[2 * hp + parity], preferred_element_type=F32)
                for parity in range(2)]
        ot_scr[pair, :] = _head_out_t(accs[0], accs[1], even_rows)
    o_ref[...] = _rms(ot_scr[...].T, g_ref[...]).astype(BF16)


def _band_prompt(qxb, kxb, vbt, bias_t, g_out):
    b, s, wx = qxb.shape
    w = vbt.shape[1]
    return pl.pallas_call(
        functools.partial(_band_body, s_len=s),
        grid=(b, s // BAND_Q),
        in_specs=[pl.BlockSpec((None, BAND_Q, wx), lambda i, j: (i, j, 0)),
                  pl.BlockSpec((None, s, wx), lambda i, j: (i, 0, 0)),
                  pl.BlockSpec((None, w, s), lambda i, j: (i, 0, 0)),
                  pl.BlockSpec(bias_t.shape, lambda i, j: (0, 0, 0)),
                  pl.BlockSpec((1, w), lambda i, j: (0, 0))],
        out_specs=pl.BlockSpec((None, BAND_Q, w), lambda i, j: (i, j, 0)),
        out_shape=jax.ShapeDtypeStruct((b, s, w), BF16),
        scratch_shapes=[pltpu.VMEM((LEFT + s, wx), BF16), pltpu.VMEM((w, LEFT + s), BF16),
                        pltpu.VMEM((N_HEADS, BAND_K, BAND_Q), F32), pltpu.VMEM((N_HEADS, BAND_K, BAND_Q), BF16),
                        pltpu.VMEM((w, BAND_Q), F32)],
        compiler_params=pltpu.CompilerParams(
            dimension_semantics=("parallel", "arbitrary"), vmem_limit_bytes=VMEM_LIMIT),
        name="band_prompt",
    )(qxb, kxb, vbt, bias_t, g_out)


def _row_to_col(row):
    n = row.shape[-1]
    r = lax.broadcasted_iota(jnp.int32, (n, n), 0)
    c = lax.broadcasted_iota(jnp.int32, (n, n), 1)
    return jnp.sum(jnp.where(r == c, jnp.broadcast_to(row, (n, n)), 0.0), axis=-1, keepdims=True)


def _fox_sample_body(q_ref, kn_ref, vn_ref, lft_ref, kc_ref, vc_ref, clft_ref, g_ref, o_ref,
                     cct_scr, cn_scr, m_scr, acc_scr, o_scr, *, t_new, pt):
    p_idx = pl.program_id(1)
    n_p = pl.num_programs(1)
    even = _pair_masks()

    @pl.when(p_idx == 0)
    def _():
        cct = _lane_cumsum(clft_ref[...])
        cct_scr[...] = cct
        cn_scr[...] = _lane_cumsum(lft_ref[...]) + cct[:, cct.shape[1] - 1:]
        m_scr[...] = jnp.full(m_scr.shape, NEG_INF, F32)
        acc_scr[...] = jnp.zeros(acc_scr.shape, F32)

    def update(h, s, v):
        m = m_scr[h]
        m_new = jnp.maximum(m, jnp.max(s, axis=-1, keepdims=True))
        p = jnp.exp(s - m_new).astype(BF16)
        acc_scr[h] = acc_scr[h] * jnp.exp(m - m_new) + jnp.dot(p, v, preferred_element_type=F32)
        m_scr[h] = m_new

    start = pl.multiple_of(p_idx * pt, pt)
    for hp in range(N_PAIRS):
        lanes = slice(hp * LANES, (hp + 1) * LANES)
        q128 = q_ref[:, lanes]
        kc = kc_ref[:, lanes]
        vc = vc_ref[:, lanes]
        for parity in range(2):
            h = 2 * hp + parity
            cq = _row_to_col(cn_scr[h:h + 1, :])
            s = lax.dot_general(_head_q(q128, even, parity), kc, _NT, preferred_element_type=F32)
            update(h, s + cq - cct_scr[h:h + 1, pl.ds(start, pt)], _head_v(vc, even, parity))

    @pl.when(p_idx == n_p - 1)
    def _():
        row = lax.broadcasted_iota(jnp.int32, (t_new, t_new), 0)
        col = lax.broadcasted_iota(jnp.int32, (t_new, t_new), 1)
        for hp in range(N_PAIRS):
            lanes = slice(hp * LANES, (hp + 1) * LANES)
            q128 = q_ref[:, lanes]
            kn = kn_ref[:, lanes]
            vn = vn_ref[:, lanes]
            for parity in range(2):
                h = 2 * hp + parity
                cn_row = cn_scr[h:h + 1, :]
                s = lax.dot_general(_head_q(q128, even, parity), kn, _NT, preferred_element_type=F32)
                s = jnp.where(col <= row, s + _row_to_col(cn_row) - cn_row, NEG_INF)
                update(h, s, _head_v(vn, even, parity))
            o_scr[:, lanes] = _head_out(acc_scr[2 * hp], acc_scr[2 * hp + 1], even)
        o_ref[...] = _rms(o_scr[...], g_ref[...]).astype(BF16)


def _fox_sample(q, kn, vn, lft, kc, vc, clft, g_out, *, pt):
    b, t, w = q.shape
    p_len = kc.shape[1]
    new = lambda: pl.BlockSpec((None, t, w), lambda i, j: (i, 0, 0))
    cache = lambda: pl.BlockSpec((None, pt, w), lambda i, j: (i, j, 0))
    return pl.pallas_call(
        functools.partial(_fox_sample_body, t_new=t, pt=pt),
        grid=(b, p_len // pt),
        in_specs=[new(), new(), new(),
                  pl.BlockSpec((None, N_HEADS, t), lambda i, j: (i, 0, 0)),
                  cache(), cache(),
                  pl.BlockSpec((None, N_HEADS, p_len), lambda i, j: (i, 0, 0)),
                  pl.BlockSpec((1, w), lambda i, j: (0, 0))],
        out_specs=new(),
        out_shape=jax.ShapeDtypeStruct((b, t, w), BF16),
        scratch_shapes=[pltpu.VMEM((N_HEADS, p_len), F32), pltpu.VMEM((N_HEADS, t), F32),
                        pltpu.VMEM((N_HEADS, t, 1), F32), pltpu.VMEM((N_HEADS, t, LANES), F32),
                        pltpu.VMEM((t, w), F32)],
        compiler_params=pltpu.CompilerParams(
            dimension_semantics=("parallel", "arbitrary"), vmem_limit_bytes=VMEM_LIMIT),
        name="fox_sample",
    )(q, kn, vn, lft, kc, vc, clft, g_out)


def _band_sample_body(q_ref, kn_ref, vn_ref, knf_ref, vnf_ref, kc_ref, vc_ref, bias_ref, g_ref,
                      o_ref, nk_ref, nv_ref, kcat, vcat, o_scr, *, t_new, bp):
    kcat[0:bp, :] = kc_ref[...].astype(BF16)
    vcat[0:bp, :] = vc_ref[...].astype(BF16)
    kcat[bp:bp + t_new, :] = kn_ref[...]
    vcat[bp:bp + t_new, :] = vn_ref[...]
    nk_ref[0:bp - t_new, :] = kc_ref[t_new:bp, :]
    nv_ref[0:bp - t_new, :] = vc_ref[t_new:bp, :]
    nk_ref[bp - t_new:bp, :] = knf_ref[...]
    nv_ref[bp - t_new:bp, :] = vnf_ref[...]
    even = _pair_masks()
    for hp in range(N_PAIRS):
        lanes = slice(hp * LANES, (hp + 1) * LANES)
        q128 = q_ref[:, lanes]
        k = kcat[:, lanes]
        v = vcat[:, lanes]
        accs = []
        for parity in range(2):
            h = 2 * hp + parity
            s = lax.dot_general(_head_q(q128, even, parity), k, _NT, preferred_element_type=F32)
            s = s + bias_ref[h, 0:t_new, 0:bp + t_new]
            p = jnp.exp(s - jnp.max(s, axis=-1, keepdims=True)).astype(BF16)
            accs.append(jnp.dot(p, _head_v(v, even, parity), preferred_element_type=F32))
        o_scr[:, lanes] = _head_out(accs[0], accs[1], even)
    o_ref[...] = _rms(o_scr[...], g_ref[...]).astype(BF16)


def _band_sample(q, kn, vn, knf, vnf, kc, vc, bias, g_out):
    b, t, w = q.shape
    bp = kc.shape[1]
    assert t == CHUNK and bp == LEFT
    new = lambda: pl.BlockSpec((None, t, w), lambda i: (i, 0, 0))
    buf = lambda: pl.BlockSpec((None, bp, w), lambda i: (i, 0, 0))
    return pl.pallas_call(
        functools.partial(_band_sample_body, t_new=t, bp=bp),
        grid=(b,),
        in_specs=[new(), new(), new(), new(), new(), buf(), buf(),
                  pl.BlockSpec(bias.shape, lambda i: (0, 0, 0)),
                  pl.BlockSpec((1, w), lambda i: (0, 0))],
        out_specs=[new(), buf(), buf()],
        out_shape=[jax.ShapeDtypeStruct((b, t, w), BF16), jax.ShapeDtypeStruct((b, bp, w), F32),
                   jax.ShapeDtypeStruct((b, bp, w), F32)],
        scratch_shapes=[pltpu.VMEM((bp + t, w), BF16), pltpu.VMEM((bp + t, w), BF16), pltpu.VMEM((t, w), F32)],
        compiler_params=pltpu.CompilerParams(dimension_semantics=("parallel",), vmem_limit_bytes=VMEM_LIMIT),
        name="band_sample",
    )(q, kn, vn, knf, vnf, kc, vc, bias, g_out)


N_HEADS_MEM = 4
HEAD_DIM_MEM = 128
W_MEM = N_HEADS_MEM * HEAD_DIM_MEM
MEM_SCALE = HEAD_DIM_MEM ** -0.5


def _memkv_body(m_ref, g_ref, w_ref, kf_ref, vf_ref, k_ref, v_ref):
    h = _rms(m_ref[...], g_ref[...]).astype(BF16)
    z = jnp.dot(h, w_ref[...], preferred_element_type=F32)
    kf_ref[...] = z[:, :W_MEM]
    vf_ref[...] = z[:, W_MEM:]
    k_ref[...] = z[:, :W_MEM].astype(BF16)
    v_ref[...] = z[:, W_MEM:].astype(BF16)


def _mem_kv(mem, g_mem, w_ckv):
    b, n, d = mem.shape
    blk = lambda: pl.BlockSpec((None, n, W_MEM), lambda i: (i, 0, 0))
    return pl.pallas_call(
        _memkv_body,
        grid=(b,),
        in_specs=[pl.BlockSpec((None, n, d), lambda i: (i, 0, 0)),
                  pl.BlockSpec((1, d), lambda i: (0, 0)),
                  pl.BlockSpec(w_ckv.shape, lambda i: (0, 0))],
        out_specs=[blk(), blk(), blk(), blk()],
        out_shape=[jax.ShapeDtypeStruct((b, n, W_MEM), F32)] * 2 + [jax.ShapeDtypeStruct((b, n, W_MEM), BF16)] * 2,
        compiler_params=pltpu.CompilerParams(dimension_semantics=("parallel",), vmem_limit_bytes=VMEM_LIMIT),
        name="mem_kv",
    )(mem, g_mem, w_ckv)


N_GROUPS = 4
EXPERTS_PER_GROUP = 8
N_EXPERTS = N_GROUPS * EXPERTS_PER_GROUP
ROUTE_L2 = N_GROUPS
ROUTE_ROWS = 8
POST_CHAIN = 512
R_EID0, R_EID1, R_RANK0, R_RANK1, R_GATE0, R_GATE1 = range(6)


def _lane_max(x, mask):
    return jnp.max(jnp.where(mask, x, -jnp.inf), axis=-1, keepdims=True)


def _first_lane(mask, lane):
    return jnp.min(jnp.where(mask, lane, LANES), axis=-1, keepdims=True)


def _route(logits, lane):
    is_l1 = lane < N_GROUPS
    m1 = _lane_max(logits, is_l1)
    grp = _first_lane(is_l1 & (logits == m1), lane)
    wg = 1.0 / jnp.sum(jnp.where(is_l1, jnp.exp(logits - m1), 0.0), axis=-1, keepdims=True)
    lo = ROUTE_L2 + grp * EXPERTS_PER_GROUP
    in_grp = (lane >= lo) & (lane < lo + EXPERTS_PER_GROUP)
    v0 = _lane_max(logits, in_grp)
    i0 = _first_lane(in_grp & (logits == v0), lane)
    rest = in_grp & (lane != i0)
    v1 = _lane_max(logits, rest)
    i1 = _first_lane(rest & (logits == v1), lane)
    e1 = jnp.exp(v1 - v0)
    den = 1.0 / (1.0 + e1)
    return i0, i1, wg * den, wg * e1 * den


def _post_body(x_ref, a_ref, b_ref, mk_ref, mv_ref, woa_ref, wob_ref, gc_ref, wcq_ref, wco_ref,
               gf_ref, wrt_ref, brt_ref,
               x2_ref, h3_ref, route_ref, routet_ref, cnt_ref, o_scr, *, tm, nsub):
    seq = tm // nsub
    count = jnp.zeros((1, LANES), F32)
    for rows in _chains(tm, POST_CHAIN):
        n = rows.size
        x1 = (x_ref[rows, :] + jnp.dot(a_ref[rows, :], woa_ref[...], preferred_element_type=F32)
              + jnp.dot(b_ref[rows, :], wob_ref[...], preferred_element_type=F32))
        h2 = _rms(x1, gc_ref[...]).astype(BF16)
        qc = (jnp.dot(h2, wcq_ref[...], preferred_element_type=F32) * MEM_SCALE).astype(BF16)
        span = min(seq, n)
        for part in range(n // span):
            sub = (rows.start + part * span) // seq
            rs = slice(part * span, (part + 1) * span)
            orow = pl.ds(rows.start + part * span, span)
            for hm in range(N_HEADS_MEM):
                lanes = slice(hm * HEAD_DIM_MEM, (hm + 1) * HEAD_DIM_MEM)
                s = lax.dot_general(qc[rs, lanes], mk_ref[sub, :, lanes], _NT, preferred_element_type=F32)
                p = jnp.exp(s - jnp.max(s, axis=-1, keepdims=True))
                inv = 1.0 / jnp.sum(p, axis=-1, keepdims=True)
                o_scr[orow, lanes] = jnp.dot(p.astype(BF16), mv_ref[sub, :, lanes], preferred_element_type=F32) * inv
        x2 = x1 + jnp.dot(o_scr[rows, :].astype(BF16), wco_ref[...], preferred_element_type=F32)
        x2_ref[rows, :] = x2
        h3 = _rms(x2, gf_ref[...]).astype(BF16)
        h3_ref[rows, :] = h3

        logits = jnp.dot(h3, wrt_ref[...], preferred_element_type=F32) + brt_ref[...]
        lane = lax.broadcasted_iota(jnp.int32, (n, LANES), 1)
        i0, i1, g0, g1 = _route(logits, lane)
        e0 = i0 - ROUTE_L2
        e1 = i1 - ROUTE_L2
        hit0 = lane == e0
        hit1 = lane == e1
        onehot = jnp.where(hit0 | hit1, 1.0, 0.0)
        row = lax.broadcasted_iota(jnp.int32, (n, n), 0)
        col = lax.broadcasted_iota(jnp.int32, (n, n), 1)
        before = jnp.where(col < row, 1.0, 0.0).astype(BF16)
        seen = jnp.dot(before, onehot.astype(BF16), preferred_element_type=F32) + count
        rank0 = jnp.sum(jnp.where(hit0, seen, 0.0), axis=-1, keepdims=True)
        rank1 = jnp.sum(jnp.where(hit1, seen, 0.0), axis=-1, keepdims=True)
        count = count + jnp.sum(onehot, axis=0, keepdims=True)

        rec = jnp.zeros((n, LANES), F32)
        for idx, val in ((R_EID0, e0.astype(F32)), (R_EID1, e1.astype(F32)), (R_RANK0, rank0),
                         (R_RANK1, rank1), (R_GATE0, g0), (R_GATE1, g1)):
            rec = jnp.where(lane == idx, val, rec)
        route_ref[rows, :] = rec[:, :ROUTE_ROWS]
        routet_ref[:, rows] = rec.T[:ROUTE_ROWS, :]
    cnt_ref[...] = count


def _post_block(x, a_n, b_n, mk, mv, weights, *, tm):
    b, s, d = x.shape
    if s >= tm:
        nsub, grid = 1, (b, s // tm)
        tok = lambda i, j: (i, j, 0)
        flat = lambda i, j: i * (s // tm) + j
    else:
        nsub = tm // s
        assert b % nsub == 0
        x, a_n, b_n = (t.reshape(b // nsub, tm, t.shape[-1]) for t in (x, a_n, b_n))
        grid = (b // nsub, 1)
        tok = lambda i, j: (i, 0, 0)
        flat = lambda i, j: i
    mem = lambda i, j: (i, 0, 0)
    const = lambda arr: pl.BlockSpec(arr.shape, lambda i, j: (0,) * arr.ndim)
    in_specs = [pl.BlockSpec((None, tm, d), tok),
                pl.BlockSpec((None, tm, W_GROUP), tok), pl.BlockSpec((None, tm, W_GROUP), tok),
                pl.BlockSpec((nsub, mk.shape[1], W_MEM), mem), pl.BlockSpec((nsub, mv.shape[1], W_MEM), mem)]
    in_specs += [const(w) for w in weights]
    n = b * s
    out_shape = [jax.ShapeDtypeStruct((n, d), F32), jax.ShapeDtypeStruct((n, d), BF16),
                 jax.ShapeDtypeStruct((n, ROUTE_ROWS), F32), jax.ShapeDtypeStruct((ROUTE_ROWS, n), F32),
                 jax.ShapeDtypeStruct((n // tm, 1, LANES), F32)]
    out_specs = [pl.BlockSpec((tm, d), lambda i, j: (flat(i, j), 0)),
                 pl.BlockSpec((tm, d), lambda i, j: (flat(i, j), 0)),
                 pl.BlockSpec((tm, ROUTE_ROWS), lambda i, j: (flat(i, j), 0)),
                 pl.BlockSpec((ROUTE_ROWS, tm), lambda i, j: (0, flat(i, j))),
                 pl.BlockSpec((None, 1, LANES), lambda i, j: (flat(i, j), 0, 0))]
    return pl.pallas_call(
        functools.partial(_post_body, tm=tm, nsub=nsub),
        grid=grid,
        in_specs=in_specs,
        out_specs=out_specs,
        out_shape=out_shape,
        scratch_shapes=[pltpu.VMEM((tm, W_MEM), F32)],
        compiler_params=pltpu.CompilerParams(
            dimension_semantics=("parallel", "parallel"), vmem_limit_bytes=VMEM_LIMIT),
        name="post_block",
    )(x, a_n, b_n, mk, mv, *weights)


def _prep_post(w_out, g_cross, w_cq, w_co, g_ffn, w_r1, b_r1, w_r2, b_r2):
    pad = LANES - N_GROUPS - N_EXPERTS
    w_rt = jnp.pad(jnp.concatenate([w_r1, w_r2], axis=1), ((0, 0), (0, pad))).astype(BF16)
    b_rt = jnp.pad(jnp.concatenate([b_r1, b_r2]).reshape(1, -1), ((0, 0), (0, pad))).astype(F32)
    return [w_out[:W_GROUP].astype(BF16), w_out[W_GROUP:].astype(BF16), g_cross.reshape(1, -1),
            w_cq.astype(BF16), w_co.astype(BF16), g_ffn.reshape(1, -1), w_rt, b_rt]


D_EXPERT = 512
TOP_K = 2
ROW_TILE = 512
MXU_DIM = 256
RUN_ALIGN = 16
PLAN_TILES = LANES
TILE_TABLE = 2 * LANES


def _local_rows(tm):
    return -(-(TOP_K * tm + N_EXPERTS * (RUN_ALIGN - 1)) // MXU_DIM) * MXU_DIM


def _n_row_tiles(n_tokens, tm):
    rows = n_tokens * TOP_K + (n_tokens // tm) * N_EXPERTS * (RUN_ALIGN - 1) + N_EXPERTS * (ROW_TILE - 1)
    return rows // ROW_TILE


N_CHUNK_LANE = LANES - 1


def _plan_body(cnt_ref, lstart_ref, chunk_ref, offs_ref, te_ref):
    cnt = cnt_ref[...].astype(jnp.int32)
    n16 = ((cnt + (RUN_ALIGN - 1)) & ~(RUN_ALIGN - 1)).astype(F32)
    lend = _lane_cumsum(n16)
    lstart = lend - n16
    earlier = (_lane_cumsum(n16.T) - n16.T).T
    total = jnp.sum(n16, axis=0, keepdims=True).astype(jnp.int32)
    seg = jnp.broadcast_to((total + (ROW_TILE - 1)) & ~(ROW_TILE - 1), (8, LANES)).astype(F32)
    ends = _lane_cumsum(seg)
    offs = ends - seg
    shift = earlier + offs[0:1, :] - lstart
    lstart_ref[...] = lstart.astype(jnp.int32)
    offs_ref[...] = offs[0:1, :].astype(jnp.int32)

    lane = lax.broadcasted_iota(jnp.int32, (PLAN_TILES, LANES), 1)
    local_row = (lane * RUN_ALIGN).astype(F32)
    owner = jnp.zeros((PLAN_TILES, LANES), jnp.int32)
    for e in range(N_EXPERTS):
        owner = owner + jnp.where(lend[:, e:e + 1] <= local_row, 1, 0)
    glob = local_row
    for e in range(N_EXPERTS):
        glob = glob + jnp.where(owner == e, shift[:, e:e + 1], 0.0)
    n_chunks = lend[:, N_EXPERTS - 1:N_EXPERTS] * (1.0 / RUN_ALIGN)
    chunk_ref[...] = jnp.where(lane == N_CHUNK_LANE, n_chunks, glob).astype(jnp.int32)

    tile_start = (lax.broadcasted_iota(jnp.int32, te_ref.shape, 1) * ROW_TILE).astype(F32)
    te = jnp.zeros(te_ref.shape, jnp.int32)
    for e in range(N_EXPERTS):
        end_e = jnp.sum(jnp.where(lane[0:1, :] == e, ends[0:1, :], 0.0), axis=-1, keepdims=True)
        te = te + jnp.where(end_e <= tile_start, 1, 0)
    te_ref[...] = jnp.minimum(te, N_EXPERTS - 1)


def _plan(counts, tm):
    nt = counts.shape[0]
    assert nt <= PLAN_TILES and _local_rows(tm) // RUN_ALIGN <= N_CHUNK_LANE
    cnt = jnp.pad(counts.reshape(nt, LANES), ((0, PLAN_TILES - nt), (0, 0)))
    grid_i32 = jax.ShapeDtypeStruct((PLAN_TILES, LANES), jnp.int32)
    lstart, chunks, offs, te = pl.pallas_call(
        _plan_body,
        out_shape=[grid_i32, grid_i32, jax.ShapeDtypeStruct((1, LANES), jnp.int32),
                   jax.ShapeDtypeStruct((1, TILE_TABLE), jnp.int32)],
        name="moe_plan",
    )(cnt)
    per_tile = lambda t: t[:nt].reshape(nt, 1, LANES)
    return per_tile(lstart), per_tile(chunks), offs.reshape(LANES), te.reshape(TILE_TABLE)


def _for_each_chunk(chunk_ref, fn):
    def body(c, carry):
        fn(pl.multiple_of(c * RUN_ALIGN, RUN_ALIGN), pl.multiple_of(chunk_ref[0, c], RUN_ALIGN))
        return carry

    lax.fori_loop(0, chunk_ref[0, N_CHUNK_LANE], body, None)


def _local_positions_row(rt_ref, lstart_ref):
    pos = []
    for r_eid, r_rank in ((R_EID0, R_RANK0), (R_EID1, R_RANK1)):
        eid = rt_ref[r_eid:r_eid + 1, :].astype(jnp.int32)
        p = rt_ref[r_rank:r_rank + 1, :].astype(jnp.int32)
        for e in range(N_EXPERTS):
            p = p + jnp.where(eid == e, lstart_ref[0, e], 0)
        pos.append(p)
    return pos


def _dispatch_body(offs_ref, lstart_ref, chunk_ref, pchunk_ref,
                   hp_ref, hs_ref, rt_ref, xs_ref, loc, zeros, sems, zsem, *, tm, n_prompt_tiles):
    i = pl.program_id(0)
    n_tiles = xs_ref.shape[0] // ROW_TILE
    half = lax.rem(i, 2)

    @pl.when(i == 0)
    def _():
        zeros[...] = jnp.zeros(zeros.shape, zeros.dtype)
        zero_tile = lambda row: pltpu.make_async_copy(
            zeros, xs_ref.at[pl.ds(pl.multiple_of(row, ROW_TILE), ROW_TILE)], zsem)
        n_used = offs_ref[N_EXPERTS] // ROW_TILE

        def tail(j, carry, op):
            op(zero_tile(j * ROW_TILE))
            return carry

        for op in (lambda c: c.start(), lambda c: c.wait()):
            for e in range(N_EXPERTS):
                @pl.when(offs_ref[e + 1] > offs_ref[e])
                def _():
                    op(zero_tile(offs_ref[e + 1] - ROW_TILE))
            lax.fori_loop(n_used, n_tiles, functools.partial(tail, op=op), None)

    pos0, pos1 = _local_positions_row(rt_ref, lstart_ref)
    slot = lax.broadcasted_iota(jnp.int32, (loc.shape[1], tm), 0)
    perm = jnp.where(slot == pos0, 1.0, jnp.where(slot == pos1, 1.0, 0.0)).astype(BF16)

    @pl.when(i < n_prompt_tiles)
    def _():
        loc[half] = jnp.dot(perm, hp_ref[...], preferred_element_type=F32).astype(BF16)

    @pl.when(i >= n_prompt_tiles)
    def _():
        loc[half] = jnp.dot(perm, hs_ref[...], preferred_element_type=F32).astype(BF16)

    def chunk(buf, lo, go):
        return pltpu.make_async_copy(loc.at[buf, pl.ds(lo, RUN_ALIGN)], xs_ref.at[pl.ds(go, RUN_ALIGN)],
                                     sems.at[buf])

    @pl.when(i > 0)
    def _():
        _for_each_chunk(pchunk_ref, lambda lo, go: chunk(1 - half, lo, go).wait())

    _for_each_chunk(chunk_ref, lambda lo, go: chunk(half, lo, go).start())

    @pl.when(i == pl.num_programs(0) - 1)
    def _():
        _for_each_chunk(chunk_ref, lambda lo, go: chunk(half, lo, go).wait())


def _dispatch(h3_prompt, h3_sample, route_t, plan, *, tm):
    lstart, chunks, offs, _ = plan
    n_p, n_s, d = h3_prompt.shape[0], h3_sample.shape[0], h3_prompt.shape[-1]
    assert n_p % tm == 0 and n_s % tm == 0
    n = n_p + n_s
    nt = n // tm
    npt = n_p // tm
    n_rows = _n_row_tiles(n, tm) * ROW_TILE
    smem_tile = lambda: pl.BlockSpec((None, 1, LANES), lambda i, offs: (i, 0, 0), memory_space=pltpu.SMEM)
    smem_prev = lambda: pl.BlockSpec((None, 1, LANES), lambda i, offs: (jnp.maximum(i - 1, 0), 0, 0),
                                     memory_space=pltpu.SMEM)
    return pl.pallas_call(
        functools.partial(_dispatch_body, tm=tm, n_prompt_tiles=npt),
        grid_spec=pltpu.PrefetchScalarGridSpec(
            num_scalar_prefetch=1,
            grid=(nt,),
            in_specs=[smem_tile(), smem_tile(), smem_prev(),
                      pl.BlockSpec((tm, d), lambda i, offs: (jnp.minimum(i, npt - 1), 0)),
                      pl.BlockSpec((tm, d), lambda i, offs: (jnp.maximum(i - npt, 0), 0)),
                      pl.BlockSpec((ROUTE_ROWS, tm), lambda i, offs: (0, i))],
            out_specs=pl.BlockSpec(memory_space=pl.ANY),
            scratch_shapes=[pltpu.VMEM((2, _local_rows(tm), d), BF16), pltpu.VMEM((ROW_TILE, d), BF16),
                            pltpu.SemaphoreType.DMA((2,)), pltpu.SemaphoreType.DMA(())]),
        out_shape=jax.ShapeDtypeStruct((n_rows, d), BF16),
        compiler_params=pltpu.CompilerParams(dimension_semantics=("arbitrary",), vmem_limit_bytes=VMEM_LIMIT),
        name="moe_dispatch",
    )(offs, lstart, chunks, chunks, h3_prompt, h3_sample, route_t)


def _experts_body(te_ref, offs_ref, xs_ref, wg_ref, wu_ref, wd_ref, ys_ref, wg_bf, wu_bf, wd_bf):
    i = pl.program_id(0)
    n_used = offs_ref[N_EXPERTS] // ROW_TILE

    @pl.when(i < n_used)
    def _():
        @pl.when((i == 0) | (te_ref[i] != te_ref[jnp.maximum(i - 1, 0)]))
        def _():
            wg_bf[...] = wg_ref[...].astype(BF16)
            wu_bf[...] = wu_ref[...].astype(BF16)
            wd_bf[...] = wd_ref[...].astype(BF16)

        dot = functools.partial(jnp.dot, preferred_element_type=F32)
        for rows in _chains(ROW_TILE):
            x = xs_ref[rows, :]
            gate = dot(x, wg_bf[...])
            up = dot(x, wu_bf[...])
            act = (gate * jax.nn.sigmoid(gate) * up).astype(BF16)
            ys_ref[rows, :] = dot(act, wd_bf[...]).astype(BF16)

    @pl.when(i >= n_used)
    def _():
        ys_ref[...] = jnp.zeros(ys_ref.shape, ys_ref.dtype)


def _experts(xs, te, offs, w_gate, w_up, w_down):
    n_rows, d = xs.shape
    last = lambda i, te, offs: jnp.minimum(i, offs[N_EXPERTS] // ROW_TILE - 1)
    wspec = lambda shape: pl.BlockSpec((None,) + shape, lambda i, te, offs: (te[last(i, te, offs)], 0, 0))
    return pl.pallas_call(
        _experts_body,
        grid_spec=pltpu.PrefetchScalarGridSpec(
            num_scalar_prefetch=2,
            grid=(n_rows // ROW_TILE,),
            in_specs=[pl.BlockSpec((ROW_TILE, d), lambda i, te, offs: (last(i, te, offs), 0)),
                      wspec((d, D_EXPERT)), wspec((d, D_EXPERT)), wspec((D_EXPERT, d))],
            out_specs=pl.BlockSpec((ROW_TILE, d), lambda i, te, offs: (i, 0)),
            scratch_shapes=[pltpu.VMEM((d, D_EXPERT), BF16), pltpu.VMEM((d, D_EXPERT), BF16),
                            pltpu.VMEM((D_EXPERT, d), BF16)]),
        out_shape=jax.ShapeDtypeStruct((n_rows, d), BF16),
        compiler_params=pltpu.CompilerParams(dimension_semantics=("arbitrary",), vmem_limit_bytes=VMEM_LIMIT),
        name="moe_experts",
    )(te, offs, xs, w_gate, w_up, w_down)


def _combine_body(chunk_ref, nchunk_ref, x2p_ref, x2s_ref, route_ref, lsv_ref, ys_ref, g_ref,
                  yp_ref, ysm_ref, loc, sems, *, tm, n_prompt_tiles):
    i = pl.program_id(0)
    half = lax.rem(i, 2)

    def chunk(buf, lo, go):
        return pltpu.make_async_copy(ys_ref.at[pl.ds(go, RUN_ALIGN)], loc.at[buf, pl.ds(lo, RUN_ALIGN)],
                                     sems.at[buf])

    @pl.when(i == 0)
    def _():
        loc[...] = jnp.zeros(loc.shape, loc.dtype)
        _for_each_chunk(chunk_ref, lambda lo, go: chunk(0, lo, go).start())

    @pl.when(i + 1 < pl.num_programs(0))
    def _():
        _for_each_chunk(nchunk_ref, lambda lo, go: chunk(1 - half, lo, go).start())

    _for_each_chunk(chunk_ref, lambda lo, go: chunk(half, lo, go).wait())

    lane = lax.broadcasted_iota(jnp.int32, (tm, LANES), 1)
    slot = lax.broadcasted_iota(jnp.int32, (tm, loc.shape[1]), 1)
    weights = jnp.zeros(slot.shape, F32)
    for r_eid, r_rank, r_gate in ((R_EID0, R_RANK0, R_GATE0), (R_EID1, R_RANK1, R_GATE1)):
        eid = route_ref[:, r_eid:r_eid + 1].astype(jnp.int32)
        start = jnp.sum(jnp.where(lane == eid, lsv_ref[...], 0), axis=-1, keepdims=True)
        pos = route_ref[:, r_rank:r_rank + 1].astype(jnp.int32) + start
        weights = jnp.where(slot == pos, route_ref[:, r_gate:r_gate + 1], weights)
    moe = jnp.dot(weights.astype(BF16), loc[half], preferred_element_type=F32)

    @pl.when(i < n_prompt_tiles)
    def _():
        yp_ref[...] = _rms(x2p_ref[...] + moe, g_ref[...])

    @pl.when(i >= n_prompt_tiles)
    def _():
        ysm_ref[...] = _rms(x2s_ref[...] + moe, g_ref[...])


def _combine(x2_prompt, x2_sample, route, plan, ys, g_final, *, tm):
    lstart, chunks, _, _ = plan
    (n_p, d), n_s = x2_prompt.shape, x2_sample.shape[0]
    assert n_p % tm == 0 and n_s % tm == 0
    npt = n_p // tm
    nt = npt + n_s // tm
    smem_tile = lambda: pl.BlockSpec((None, 1, LANES), lambda i: (i, 0, 0), memory_space=pltpu.SMEM)
    smem_next = lambda: pl.BlockSpec((None, 1, LANES), lambda i: (jnp.minimum(i + 1, nt - 1), 0, 0),
                                     memory_space=pltpu.SMEM)
    prompt_tile = lambda: pl.BlockSpec((tm, d), lambda i: (jnp.minimum(i, npt - 1), 0))
    sample_tile = lambda: pl.BlockSpec((tm, d), lambda i: (jnp.maximum(i - npt, 0), 0))
    return pl.pallas_call(
        functools.partial(_combine_body, tm=tm, n_prompt_tiles=npt),
        grid=(nt,),
        in_specs=[smem_tile(), smem_next(), prompt_tile(), sample_tile(),
                  pl.BlockSpec((tm, ROUTE_ROWS), lambda i: (i, 0)),
                  pl.BlockSpec((None, 1, LANES), lambda i: (i, 0, 0)),
                  pl.BlockSpec(memory_space=pl.ANY),
                  pl.BlockSpec((1, d), lambda i: (0, 0))],
        out_specs=[prompt_tile(), sample_tile()],
        out_shape=[jax.ShapeDtypeStruct((n_p, d), F32), jax.ShapeDtypeStruct((n_s, d), F32)],
        scratch_shapes=[pltpu.VMEM((2, _local_rows(tm), d), BF16), pltpu.SemaphoreType.DMA((2,))],
        compiler_params=pltpu.CompilerParams(dimension_semantics=("arbitrary",), vmem_limit_bytes=VMEM_LIMIT),
        name="moe_combine",
    )(chunks, chunks, x2_prompt, x2_sample, route, lstart, ys, g_final)


TOKEN_TILE = 512
FOX_Q_TILE = 256
FOX_CACHE_TILE = 1024


def kernel(x_prompt, x_sample, cache_fox_k, cache_fox_v, cache_fox_logf, cache_band_k, cache_band_v, cache_mem_k, cache_mem_v, mem_prompt, g_mix, w_in, b_forget, g_out_fox, g_out_band, rel_table, w_out, g_cross, g_mem, w_cq, w_ck, w_cv, w_co, g_ffn, w_router1, b_router1, w_router2, b_router2, w_exp_gate, w_exp_up, w_exp_down, g_final):
    assert g_mix.shape[0] == 1, "single-layer model"
    bsz, seq, d = x_prompt.shape
    sb, st, _ = x_sample.shape
    n_s = sb * st
    n_mem = mem_prompt.shape[1]
    row = lambda g: g.reshape(1, -1)

    w_pad, bf_pad, g_mix_r = _prep_proj(w_in[0], b_forget[0], g_mix[0])
    g_of, g_ob = row(g_out_fox[0]), row(g_out_band[0])
    bias_t, bias_s = _band_bias(_prep_band_bias_row(rel_table[0]))

    qx, kx, vat, qxb, kxb, vbt, kaf, vaf, kbf, vbf, logf = _proj(
        x_prompt, g_mix_r, w_pad, bf_pad, tm=TOKEN_TILE, prompt=True)
    a_p = _fox_prompt(qx, kx, vat, g_of, tq=FOX_Q_TILE)
    b_p = _band_prompt(qxb, kxb, vbt, bias_t, g_ob)

    s_out = _proj(x_sample.reshape(1, n_s, d), g_mix_r, w_pad, bf_pad, tm=n_s, prompt=False)
    sqa, ska, sva, sqb, skb, svb, skaf, svaf, skbf, svbf = (t.reshape(sb, st, W_GROUP) for t in s_out[:10])
    slogf = s_out[10].reshape(sb, st, N_HEADS)
    slft = s_out[11].reshape(N_HEADS, sb, st).transpose(1, 0, 2)
    past = cache_fox_k.shape[2]
    a_s = _fox_sample(sqa, ska, sva, slft,
                      cache_fox_k[0].reshape(sb, past, W_GROUP).astype(BF16),
                      cache_fox_v[0].reshape(sb, past, W_GROUP).astype(BF16),
                      cache_fox_logf[0].transpose(0, 2, 1), g_of, pt=FOX_CACHE_TILE)
    bp = cache_band_k.shape[2]
    b_s, nbk, nbv = _band_sample(sqb, skb, svb, skbf, svbf,
                                 cache_band_k[0].reshape(sb, bp, W_GROUP), cache_band_v[0].reshape(sb, bp, W_GROUP),
                                 bias_s, g_ob)

    w_ckv = jnp.concatenate([w_ck[0], w_cv[0]], axis=1).astype(BF16)
    mkf, mvf, mk, mv = _mem_kv(mem_prompt, row(g_mem[0]), w_ckv)
    post_w = _prep_post(w_out[0], g_cross[0], w_cq[0], w_co[0], g_ffn[0],
                        w_router1[0], b_router1[0], w_router2[0], b_router2[0])
    x2_p, h3_p, route_p, routet_p, cnt_p = _post_block(x_prompt, a_p, b_p, mk, mv, post_w, tm=TOKEN_TILE)
    cmk = cache_mem_k[0].reshape(sb, n_mem, W_MEM).astype(BF16)
    cmv = cache_mem_v[0].reshape(sb, n_mem, W_MEM).astype(BF16)
    x2_s, h3_s, route_s, routet_s, cnt_s = _post_block(x_sample, a_s, b_s, cmk, cmv, post_w, tm=TOKEN_TILE)
    route = jnp.concatenate([route_p, route_s], axis=0)
    route_t = jnp.concatenate([routet_p, routet_s], axis=1)

    plan = _plan(jnp.concatenate([cnt_p, cnt_s], axis=0), TOKEN_TILE)
    xs = _dispatch(h3_p, h3_s, route_t, plan, tm=TOKEN_TILE)
    ys = _experts(xs, plan[3], plan[2], w_exp_gate[0], w_exp_up[0], w_exp_down[0])
    y_p, y_s = _combine(x2_p, x2_s, route, plan, ys, row(g_final), tm=TOKEN_TILE)

    heads = lambda t, n: t.reshape(1, n, -1, N_HEADS, HEAD_DIM)
    mem_heads = lambda t: t.reshape(1, bsz, n_mem, N_HEADS_MEM, HEAD_DIM_MEM)
    return (y_p.reshape(bsz, seq, d), y_s.reshape(sb, st, d),
            heads(kaf, bsz), heads(vaf, bsz), logf.reshape(1, bsz, seq, N_HEADS),
            heads(kbf, bsz), heads(vbf, bsz), mem_heads(mkf), mem_heads(mvf),
            heads(skaf, sb), heads(svaf, sb), slogf.reshape(1, sb, st, N_HEADS),
            heads(nbk, sb), heads(nbv, sb))
```

```python
import functools

import jax
import jax.numpy as jnp
from jax import lax
from jax.experimental import pallas as pl
from jax.experimental.pallas import tpu as pltpu

F32 = jnp.float32
BF16 = jnp.bfloat16

D_MODEL = 1024
HEAD_DIM = 64
N_HEADS = 8
W_GROUP = N_HEADS * HEAD_DIM
N_PAIRS = N_HEADS // 2
CHUNK = 64
LEFT_CHUNKS = 8
LEFT = LEFT_CHUNKS * CHUNK
REL_CLIP = 128
EPS = 1e-6
NEG_INF = -1e30
ATTN_SCALE = HEAD_DIM ** -0.5
LANES = 128
PROJ_PAD = 3 * W_GROUP * 2 + LANES
VMEM_LIMIT = 56 * 1024 * 1024


def _rms(x, g):
    ms = jnp.mean(x * x, axis=-1, keepdims=True)
    return x * lax.rsqrt(ms + EPS) * g


def _log_sigmoid(x):
    return -(jnp.maximum(-x, 0.0) + jnp.log1p(jnp.exp(-jnp.abs(x))))


def _lane_cumsum(x):
    n = x.shape[-1]
    lane = lax.broadcasted_iota(jnp.int32, x.shape, 1)
    k = 1
    while k < n:
        x = x + jnp.where(lane >= k, pltpu.roll(x, k, axis=1), 0.0)
        k *= 2
    return x


LOG2E = 1.4426950408889634
SCALE_BASE2 = ATTN_SCALE * LOG2E


def _split3(x):
    hi = x.astype(BF16).astype(F32)
    mid = (x - hi).astype(BF16).astype(F32)
    lo = x - hi - mid
    return hi, mid, lo


def _extra_lane(parity):
    return HEAD_DIM if parity == 0 else 0


def _fox_extras(c3t, hp, tm):
    row = lax.broadcasted_iota(jnp.int32, (8, tm), 0)

    def group(h, q_side):
        hi, mid, lo = (p[h:h + 1, :] for p in c3t)
        if q_side:
            return jnp.where(row < 3, 1.0, jnp.where(row == 3, hi, jnp.where(row == 4, mid, jnp.where(row == 5, lo, 0.0))))
        return jnp.where(row == 0, -hi, jnp.where(row == 1, -mid, jnp.where(row == 2, -lo, jnp.where(row < 6, 1.0, 0.0))))

    gap = jnp.zeros((HEAD_DIM - 8, tm), F32)
    sides = []
    for q_side in (True, False):
        t = jnp.concatenate([group(2 * hp + 1, q_side), gap, group(2 * hp, q_side), gap], axis=0)
        sides.append(t.T)
    return sides


def _head_blocks(x128, extras, lane):
    return (jnp.where(lane < HEAD_DIM, x128, extras).astype(BF16),
            jnp.where(lane >= HEAD_DIM, x128, extras).astype(BF16))


Q_A, K_A, V_A, Q_B, K_B, V_B = range(6)


PROJ_CHAIN = 256


def _chains(tm, chain=PROJ_CHAIN):
    n = max(tm // chain, 1)
    return [pl.ds(i * (tm // n), tm // n) for i in range(n)]


def _proj_common(rows, x_ref, g_ref, w_ref, bf_ref, kaf_ref, vaf_ref, kbf_ref, vbf_ref, logf_ref, keep_tiles):
    s = pl.program_id(1)
    ns = pl.num_programs(1)
    h = _rms(x_ref[rows, :], g_ref[...]).astype(BF16)
    w = W_GROUP
    zf = jnp.dot(h, w_ref[:, 6 * w:6 * w + LANES], preferred_element_type=F32)
    z = [jnp.dot(h, w_ref[:, g * w:(g + 1) * w], preferred_element_type=F32) for g in range(6)]
    kaf_ref[rows, :] = z[K_A]
    vaf_ref[rows, :] = z[V_A]

    @pl.when(s >= ns - keep_tiles)
    def _():
        kbf_ref[rows, :] = z[K_B]
        vbf_ref[rows, :] = z[V_B]

    logf = _log_sigmoid(zf + bf_ref[...])
    logf_ref[rows, :] = logf[:, :N_HEADS]
    return z, logf


def _proj_prompt_body(x_ref, g_ref, w_ref, bf_ref, qx_ref, kx_ref, vat_ref, qxb_ref, kxb_ref, vbt_ref,
                      kaf_ref, vaf_ref, kbf_ref, vbf_ref, logf_ref, carry_ref, *, tm, keep_tiles):
    @pl.when(pl.program_id(1) == 0)
    def _():
        carry_ref[...] = jnp.zeros_like(carry_ref)

    for rows in _chains(tm):
        n = rows.size
        z, logf = _proj_common(rows, x_ref, g_ref, w_ref, bf_ref, kaf_ref, vaf_ref, kbf_ref, vbf_ref, logf_ref,
                               keep_tiles)
        vat_ref[:, rows] = z[V_A].T.astype(BF16)
        vbt_ref[:, rows] = z[V_B].T.astype(BF16)
        ct = _lane_cumsum(logf.T[:N_HEADS, :]) + carry_ref[:, 0:1]
        carry_ref[...] = jnp.broadcast_to(ct[:, n - 1:n], carry_ref.shape)
        c3t = _split3(ct * LOG2E)
        lane = lax.broadcasted_iota(jnp.int32, (n, LANES), 1)
        band_q_extras = jnp.where((lane == _extra_lane(0)) | (lane == _extra_lane(1)), 1.0, 0.0)
        band_k_extras = jnp.zeros((n, LANES), F32)
        for hp in range(N_PAIRS):
            blocks = slice(2 * hp * LANES, 2 * (hp + 1) * LANES)
            blk = lambda group, hp=hp, z=z: z[group][:, hp * LANES:(hp + 1) * LANES]
            q_extras, k_extras = _fox_extras(c3t, hp, n)
            qx_ref[rows, blocks] = jnp.concatenate(_head_blocks(blk(Q_A) * SCALE_BASE2, q_extras, lane), axis=1)
            kx_ref[rows, blocks] = jnp.concatenate(_head_blocks(blk(K_A), k_extras, lane), axis=1)
            qxb_ref[rows, blocks] = jnp.concatenate(_head_blocks(blk(Q_B) * SCALE_BASE2, band_q_extras, lane), axis=1)
            kxb_ref[rows, blocks] = jnp.concatenate(_head_blocks(blk(K_B), band_k_extras, lane), axis=1)


def _proj_sample_body(x_ref, g_ref, w_ref, bf_ref, qa_ref, ka_ref, va_ref, qb_ref, kb_ref, vb_ref,
                      kaf_ref, vaf_ref, kbf_ref, vbf_ref, logf_ref, lt_ref, *, tm, keep_tiles):
    for rows in _chains(tm):
        z, logf = _proj_common(rows, x_ref, g_ref, w_ref, bf_ref, kaf_ref, vaf_ref, kbf_ref, vbf_ref, logf_ref,
                               keep_tiles)
        qa_ref[rows, :] = (z[Q_A] * ATTN_SCALE).astype(BF16)
        ka_ref[rows, :] = z[K_A].astype(BF16)
        va_ref[rows, :] = z[V_A].astype(BF16)
        qb_ref[rows, :] = (z[Q_B] * ATTN_SCALE).astype(BF16)
        kb_ref[rows, :] = z[K_B].astype(BF16)
        vb_ref[rows, :] = z[V_B].astype(BF16)
        lt_ref[:, rows] = logf.T[:N_HEADS, :]


def _proj(x, g_mix, w_pad, bf_pad, *, tm, prompt):
    b, s, d = x.shape
    ns = s // tm
    keep = min(LEFT, s)
    assert s % tm == 0 and keep % tm == 0
    keep_tiles = keep // tm
    row = pl.BlockSpec((None, tm, W_GROUP), lambda i, j: (i, j, 0))
    wide = pl.BlockSpec((None, tm, N_HEADS * LANES), lambda i, j: (i, j, 0))
    col = pl.BlockSpec((None, W_GROUP, tm), lambda i, j: (i, 0, j))
    keep_spec = pl.BlockSpec((None, tm, W_GROUP), lambda i, j: (i, jnp.maximum(j - (ns - keep_tiles), 0), 0))
    heads_row = pl.BlockSpec((None, tm, N_HEADS), lambda i, j: (i, j, 0))
    heads_col = pl.BlockSpec((None, N_HEADS, tm), lambda i, j: (i, 0, j))
    const = lambda shape: pl.BlockSpec(shape, lambda i, j: (0,) * len(shape))
    rows_bf = jax.ShapeDtypeStruct((b, s, W_GROUP), BF16)
    wide_bf = jax.ShapeDtypeStruct((b, s, N_HEADS * LANES), BF16)
    cols_bf = jax.ShapeDtypeStruct((b, W_GROUP, s), BF16)
    f32_tail = [jax.ShapeDtypeStruct((b, s, W_GROUP), F32)] * 2
    f32_tail += [jax.ShapeDtypeStruct((b, keep, W_GROUP), F32)] * 2
    f32_tail += [jax.ShapeDtypeStruct((b, s, N_HEADS), F32)]
    tail_specs = [row, row, keep_spec, keep_spec, heads_row]
    if prompt:
        body = functools.partial(_proj_prompt_body, tm=tm, keep_tiles=keep_tiles)
        out_shape = [wide_bf, wide_bf, cols_bf, wide_bf, wide_bf, cols_bf] + f32_tail
        out_specs = [wide, wide, col, wide, wide, col] + tail_specs
        scratch = [pltpu.VMEM((N_HEADS, LANES), F32)]
    else:
        body = functools.partial(_proj_sample_body, tm=tm, keep_tiles=keep_tiles)
        out_shape = [rows_bf] * 6 + f32_tail + [jax.ShapeDtypeStruct((b, N_HEADS, s), F32)]
        out_specs = [row] * 6 + tail_specs + [heads_col]
        scratch = []
    return pl.pallas_call(
        body,
        grid=(b, ns),
        in_specs=[pl.BlockSpec((None, tm, d), lambda i, j: (i, j, 0)),
                  const((1, d)), const(w_pad.shape), const((1, LANES))],
        out_specs=out_specs,
        out_shape=out_shape,
        scratch_shapes=scratch,
        compiler_params=pltpu.CompilerParams(
            dimension_semantics=("parallel", "arbitrary"), vmem_limit_bytes=VMEM_LIMIT),
        name="proj",
    )(x, g_mix, w_pad, bf_pad)


def _prep_proj(w_in, b_forget, g_mix):
    cols = w_in.shape[-1]
    w_pad = jnp.pad(w_in, ((0, 0), (0, PROJ_PAD - cols))).astype(BF16)
    bf_pad = jnp.pad(b_forget.reshape(1, -1), ((0, 0), (0, LANES - N_HEADS))).astype(F32)
    return w_pad, bf_pad, g_mix.reshape(1, -1)


def _pair_masks():
    lane = lax.broadcasted_iota(jnp.int32, (1, LANES), 1)
    return lane < HEAD_DIM


def _head_q(q128, even_lanes, parity):
    keep = even_lanes if parity == 0 else jnp.logical_not(even_lanes)
    return jnp.where(keep, q128, jnp.zeros_like(q128))


def _head_v(v128, even_lanes, parity):
    keep = even_lanes if parity == 0 else jnp.logical_not(even_lanes)
    return jnp.where(keep, v128, jnp.ones_like(v128))


def _head_out(acc_even, acc_odd, even_lanes):
    inv_e = 1.0 / acc_even[:, HEAD_DIM:HEAD_DIM + 1]
    inv_o = 1.0 / acc_odd[:, 0:1]
    return jnp.where(even_lanes, acc_even * inv_e, acc_odd * inv_o)


_NT = (((1,), (1,)), ((), ()))


def _pair_rows():
    row = lax.broadcasted_iota(jnp.int32, (LANES, 1), 0)
    return row < HEAD_DIM


def _head_vt(vt128, even_rows, parity):
    keep = even_rows if parity == 0 else jnp.logical_not(even_rows)
    return jnp.where(keep, vt128, jnp.ones_like(vt128))


def _head_out_t(acc_even, acc_odd, even_rows):
    inv_e = 1.0 / acc_even[HEAD_DIM:HEAD_DIM + 1, :]
    inv_o = 1.0 / acc_odd[0:1, :]
    return jnp.where(even_rows, acc_even * inv_e, acc_odd * inv_o)


def _fox_body(qx_ref, kx_ref, vt_ref, g_ref, o_ref, s_scr, p_scr, m_scr, alpha_scr, acc_scr, ot_scr, *, tq):
    qi = pl.program_id(1)
    even_rows = _pair_rows()
    m_scr[...] = jnp.full(m_scr.shape, NEG_INF, F32)
    acc_scr[...] = jnp.zeros(acc_scr.shape, F32)
    key = lax.broadcasted_iota(jnp.int32, (tq, tq), 0)
    qry = lax.broadcasted_iota(jnp.int32, (tq, tq), 1)
    causal = key <= qry

    def logits(j, half):
        start = pl.multiple_of(j * tq, tq)
        for h in range(N_HEADS):
            head = slice(h * LANES, (h + 1) * LANES)
            s_scr[half, h] = lax.dot_general(kx_ref[pl.ds(start, tq), head], qx_ref[:, head], _NT,
                                             preferred_element_type=F32)

    def weigh(j, half, masked):
        start = pl.multiple_of(j * tq, tq)
        for h in range(N_HEADS):
            st = s_scr[half, h]
            if masked:
                st = jnp.where(causal, st, NEG_INF)
            m_old = m_scr[h:h + 1, :]
            m_new = jnp.maximum(m_old, jnp.max(st, axis=0, keepdims=True))
            p_scr[h] = jnp.exp2(st - m_new).astype(BF16)
            alpha_scr[h:h + 1, :] = jnp.exp2(m_old - m_new)
            m_scr[h:h + 1, :] = m_new
        for h in range(N_HEADS):
            pair = slice((h // 2) * LANES, (h // 2 + 1) * LANES)
            vt = _head_vt(vt_ref[pair, pl.ds(start, tq)], even_rows, h % 2)
            acc_scr[h] = acc_scr[h] * alpha_scr[h:h + 1, :] + jnp.dot(vt, p_scr[h], preferred_element_type=F32)

    def step(j, cur, nxt):
        logits(j + 1, nxt)
        weigh(j, cur, False)

    def body(jj, carry):
        step(2 * jj, 0, 1)
        step(2 * jj + 1, 1, 0)
        return carry

    logits(0, 0)
    lax.fori_loop(0, qi // 2, body, None)
    odd = lax.rem(qi, 2) == 1

    @pl.when(odd)
    def _():
        step(qi - 1, 0, 1)
        weigh(qi, 1, True)

    @pl.when(jnp.logical_not(odd))
    def _():
        weigh(qi, 0, True)
    for hp in range(N_PAIRS):
        ot_scr[hp * LANES:(hp + 1) * LANES, :] = _head_out_t(acc_scr[2 * hp], acc_scr[2 * hp + 1], even_rows)
    o_ref[...] = _rms(ot_scr[...].T, g_ref[...]).astype(BF16)


def _fox_prompt(qx, kx, vat, g_out, *, tq):
    b, s, wx = qx.shape
    w = vat.shape[1]
    return pl.pallas_call(
        functools.partial(_fox_body, tq=tq),
        grid=(b, s // tq),
        in_specs=[pl.BlockSpec((None, tq, wx), lambda i, j: (i, j, 0)),
                  pl.BlockSpec((None, s, wx), lambda i, j: (i, 0, 0)),
                  pl.BlockSpec((None, w, s), lambda i, j: (i, 0, 0)),
                  pl.BlockSpec((1, w), lambda i, j: (0, 0))],
        out_specs=pl.BlockSpec((None, tq, w), lambda i, j: (i, j, 0)),
        out_shape=jax.ShapeDtypeStruct((b, s, w), BF16),
        scratch_shapes=[pltpu.VMEM((2, N_HEADS, tq, tq), F32), pltpu.VMEM((N_HEADS, tq, tq), BF16),
                        pltpu.VMEM((N_HEADS, tq), F32), pltpu.VMEM((N_HEADS, tq), F32),
                        pltpu.VMEM((N_HEADS, LANES, tq), F32), pltpu.VMEM((w, tq), F32)],
        compiler_params=pltpu.CompilerParams(
            dimension_semantics=("parallel", "arbitrary"), vmem_limit_bytes=VMEM_LIMIT),
        name="fox_prompt",
    )(qx, kx, vat, g_out)


BAND_CHUNKS = 4
BAND_Q = BAND_CHUNKS * CHUNK
BAND_K = (LEFT_CHUNKS + BAND_CHUNKS) * CHUNK
BIAS_ROW = BAND_K + BAND_Q
BAND_K_SAMPLE = (LEFT + CHUNK + LANES - 1) // LANES * LANES


def _prep_band_bias_row(rel_table):
    pivot = LEFT + BAND_Q
    n_hi = pivot - REL_CLIP + 1
    n_mid = min(2 * REL_CLIP, BIAS_ROW - n_hi)
    n_lo = BIAS_ROW - n_hi - n_mid
    parts = [jnp.broadcast_to(rel_table[2 * REL_CLIP:], (n_hi, N_HEADS)),
             rel_table[2 * REL_CLIP - 1::-1][:n_mid],
             jnp.broadcast_to(rel_table[:1], (n_lo, N_HEADS))]
    return jnp.concatenate(parts, axis=0).T.reshape(N_HEADS, 1, BIAS_ROW)


def _band_bias_body(row_ref, bt_ref, bs_ref):
    rows = jnp.broadcast_to(row_ref[...], (BAND_Q, BIAS_ROW))
    skew = pltpu.roll(rows, 0, axis=1, stride=1, stride_axis=0)
    bias = skew[:, BAND_Q:]
    bs_ref[...] = bias[:CHUNK, :BAND_K_SAMPLE]
    qc = lax.broadcasted_iota(jnp.int32, (BAND_Q, BAND_K), 0) // CHUNK
    kc = lax.broadcasted_iota(jnp.int32, (BAND_Q, BAND_K), 1) // CHUNK
    bt_ref[...] = jnp.where((kc >= qc) & (kc <= qc + LEFT_CHUNKS), bias * LOG2E, NEG_INF).T


def _band_bias(bias_row):
    return pl.pallas_call(
        _band_bias_body,
        grid=(N_HEADS,),
        in_specs=[pl.BlockSpec((None, 1, BIAS_ROW), lambda h: (h, 0, 0))],
        out_specs=[pl.BlockSpec((None, BAND_K, BAND_Q), lambda h: (h, 0, 0)),
                   pl.BlockSpec((None, CHUNK, BAND_K_SAMPLE), lambda h: (h, 0, 0))],
        out_shape=[jax.ShapeDtypeStruct((N_HEADS, BAND_K, BAND_Q), F32),
                   jax.ShapeDtypeStruct((N_HEADS, CHUNK, BAND_K_SAMPLE), F32)],
        name="band_bias",
    )(bias_row)


def _band_body(qx_ref, kx_ref, vt_ref, bias_ref, g_ref, o_ref, kpad, vtpad, s_scr, p_scr, ot_scr, *, s_len):
    step = pl.program_id(1)

    @pl.when(step == 0)
    def _():
        lane = lax.broadcasted_iota(jnp.int32, (LEFT, 2 * LANES), 1)
        flags = (lane == _extra_lane(0)) | (lane == LANES + _extra_lane(1))
        pad_pair = jnp.where(flags, NEG_INF, 0.0).astype(BF16)
        for hp in range(N_PAIRS):
            kpad[0:LEFT, 2 * hp * LANES:2 * (hp + 1) * LANES] = pad_pair
        vtpad[:, 0:LEFT] = jnp.zeros((W_GROUP, LEFT), BF16)
        kpad[LEFT:LEFT + s_len, :] = kx_ref[...]
        vtpad[:, LEFT:LEFT + s_len] = vt_ref[...]

    even_rows = _pair_rows()
    start = pl.multiple_of(step * BAND_Q, BAND_Q)
    for h in range(N_HEADS):
        head = slice(h * LANES, (h + 1) * LANES)
        s_scr[h] = lax.dot_general(kpad[pl.ds(start, BAND_K), head], qx_ref[:, head], _NT,
                                   preferred_element_type=F32)
    for h in range(N_HEADS):
        st = s_scr[h] + bias_ref[h]
        p_scr[h] = jnp.exp2(st - jnp.max(st, axis=0, keepdims=True)).astype(BF16)
    for hp in range(N_PAIRS):
        pair = slice(hp * LANES, (hp + 1) * LANES)
        vtwin = vtpad[pair, pl.ds(start, BAND_K)]
        accs = [jnp.dot(_head_vt(vtwin, even_rows, parity), p_scr[2 * hp + parity], preferred_element_type=F32)
                for parity in range(2)]
        ot_scr[pair, :] = _head_out_t(accs[0], accs[1], even_rows)
    o_ref[...] = _rms(ot_scr[...].T, g_ref[...]).astype(BF16)


def _band_prompt(qxb, kxb, vbt, bias_t, g_out):
    b, s, wx = qxb.shape
    w = vbt.shape[1]
    return pl.pallas_call(
        functools.partial(_band_body, s_len=s),
        grid=(b, s // BAND_Q),
        in_specs=[pl.BlockSpec((None, BAND_Q, wx), lambda i, j: (i, j, 0)),
                  pl.BlockSpec((None, s, wx), lambda i, j: (i, 0, 0)),
                  pl.BlockSpec((None, w, s), lambda i, j: (i, 0, 0)),
                  pl.BlockSpec(bias_t.shape, lambda i, j: (0, 0, 0)),
                  pl.BlockSpec((1, w), lambda i, j: (0, 0))],
        out_specs=pl.BlockSpec((None, BAND_Q, w), lambda i, j: (i, j, 0)),
        out_shape=jax.ShapeDtypeStruct((b, s, w), BF16),
        scratch_shapes=[pltpu.VMEM((LEFT + s, wx), BF16), pltpu.VMEM((w, LEFT + s), BF16),
                        pltpu.VMEM((N_HEADS, BAND_K, BAND_Q), F32), pltpu.VMEM((N_HEADS, BAND_K, BAND_Q), BF16),
                        pltpu.VMEM((w, BAND_Q), F32)],
        compiler_params=pltpu.CompilerParams(
            dimension_semantics=("parallel", "arbitrary"), vmem_limit_bytes=VMEM_LIMIT),
        name="band_prompt",
    )(qxb, kxb, vbt, bias_t, g_out)


def _row_to_col(row):
    n = row.shape[-1]
    r = lax.broadcasted_iota(jnp.int32, (n, n), 0)
    c = lax.broadcasted_iota(jnp.int32, (n, n), 1)
    return jnp.sum(jnp.where(r == c, jnp.broadcast_to(row, (n, n)), 0.0), axis=-1, keepdims=True)


def _fox_sample_body(q_ref, kn_ref, vn_ref, lft_ref, kc_ref, vc_ref, clft_ref, g_ref, o_ref,
                     cct_scr, cn_scr, m_scr, l_scr, acc_scr, o_scr, s_scr, p_scr, alpha_scr, *, t_new, pt):
    p_idx = pl.program_id(1)
    n_p = pl.num_programs(1)
    head = lambda h: slice(h * HEAD_DIM, (h + 1) * HEAD_DIM)

    @pl.when(p_idx == 0)
    def _():
        cct = _lane_cumsum(clft_ref[...])
        cct_scr[...] = cct
        cn_scr[...] = _lane_cumsum(lft_ref[...]) + cct[:, cct.shape[1] - 1:]
        m_scr[...] = jnp.full(m_scr.shape, NEG_INF, F32)
        l_scr[...] = jnp.zeros(l_scr.shape, F32)
        acc_scr[...] = jnp.zeros(acc_scr.shape, F32)

    start = pl.multiple_of(p_idx * pt, pt)
    of_head = lambda ref, h: ref[pl.ds(h, pt, stride=N_HEADS), :].astype(BF16)
    for h in range(N_HEADS):
        s_scr[h] = lax.dot_general(q_ref[:, head(h)], of_head(kc_ref, h), _NT, preferred_element_type=F32)
    for h in range(N_HEADS):
        cq = _row_to_col(cn_scr[h:h + 1, :])
        s = s_scr[h] + cq - cct_scr[h:h + 1, pl.ds(start, pt)]
        m = m_scr[h]
        m_new = jnp.maximum(m, jnp.max(s, axis=-1, keepdims=True))
        p = jnp.exp(s - m_new)
        alpha = jnp.exp(m - m_new)
        p_scr[h] = p.astype(BF16)
        alpha_scr[h] = alpha
        l_scr[h] = l_scr[h] * alpha + jnp.sum(p, axis=-1, keepdims=True)
        m_scr[h] = m_new
    for h in range(N_HEADS):
        acc_scr[h] = acc_scr[h] * alpha_scr[h] + jnp.dot(p_scr[h], of_head(vc_ref, h), preferred_element_type=F32)

    @pl.when(p_idx == n_p - 1)
    def _():
        row = lax.broadcasted_iota(jnp.int32, (t_new, t_new), 0)
        col = lax.broadcasted_iota(jnp.int32, (t_new, t_new), 1)
        for h in range(N_HEADS):
            cn_row = cn_scr[h:h + 1, :]
            s = lax.dot_general(q_ref[:, head(h)], kn_ref[:, head(h)], _NT, preferred_element_type=F32)
            s = jnp.where(col <= row, s + _row_to_col(cn_row) - cn_row, NEG_INF)
            m = m_scr[h]
            m_new = jnp.maximum(m, jnp.max(s, axis=-1, keepdims=True))
            p = jnp.exp(s - m_new)
            alpha = jnp.exp(m - m_new)
            l = l_scr[h] * alpha + jnp.sum(p, axis=-1, keepdims=True)
            acc = acc_scr[h] * alpha + jnp.dot(p.astype(BF16), vn_ref[:, head(h)], preferred_element_type=F32)
            o_scr[:, head(h)] = acc * (1.0 / l)
        o_ref[...] = _rms(o_scr[...], g_ref[...]).astype(BF16)


def _fox_sample(q, kn, vn, lft, kc, vc, clft, g_out, *, pt):
    b, t, w = q.shape
    p_len = kc.shape[1] // N_HEADS
    new = lambda: pl.BlockSpec((None, t, w), lambda i, j: (i, 0, 0))
    cache = lambda: pl.BlockSpec((None, pt * N_HEADS, HEAD_DIM), lambda i, j: (i, j, 0))
    return pl.pallas_call(
        functools.partial(_fox_sample_body, t_new=t, pt=pt),
        grid=(b, p_len // pt),
        in_specs=[new(), new(), new(),
                  pl.BlockSpec((None, N_HEADS, t), lambda i, j: (i, 0, 0)),
                  cache(), cache(),
                  pl.BlockSpec((None, N_HEADS, p_len), lambda i, j: (i, 0, 0)),
                  pl.BlockSpec((1, w), lambda i, j: (0, 0))],
        out_specs=new(),
        out_shape=jax.ShapeDtypeStruct((b, t, w), BF16),
        scratch_shapes=[pltpu.VMEM((N_HEADS, p_len), F32), pltpu.VMEM((N_HEADS, t), F32),
                        pltpu.VMEM((N_HEADS, t, 1), F32), pltpu.VMEM((N_HEADS, t, 1), F32),
                        pltpu.VMEM((N_HEADS, t, HEAD_DIM), F32),
                        pltpu.VMEM((t, w), F32), pltpu.VMEM((N_HEADS, t, pt), F32),
                        pltpu.VMEM((N_HEADS, t, pt), BF16), pltpu.VMEM((N_HEADS, t, 1), F32)],
        compiler_params=pltpu.CompilerParams(
            dimension_semantics=("parallel", "arbitrary"), vmem_limit_bytes=VMEM_LIMIT),
        name="fox_sample",
    )(q, kn, vn, lft, kc, vc, clft, g_out)


def _band_sample_body(q_ref, kn_ref, vn_ref, knf_ref, vnf_ref, kc_ref, vc_ref, bias_ref, g_ref,
                      o_ref, nk_ref, nv_ref, kcat, vcat, o_scr, *, t_new, bp):
    kcat[0:bp, :] = kc_ref[...].astype(BF16)
    vcat[0:bp, :] = vc_ref[...].astype(BF16)
    kcat[bp:bp + t_new, :] = kn_ref[...]
    vcat[bp:bp + t_new, :] = vn_ref[...]
    nk_ref[0:bp - t_new, :] = kc_ref[t_new:bp, :]
    nv_ref[0:bp - t_new, :] = vc_ref[t_new:bp, :]
    nk_ref[bp - t_new:bp, :] = knf_ref[...]
    nv_ref[bp - t_new:bp, :] = vnf_ref[...]
    even = _pair_masks()
    for hp in range(N_PAIRS):
        lanes = slice(hp * LANES, (hp + 1) * LANES)
        q128 = q_ref[:, lanes]
        k = kcat[:, lanes]
        v = vcat[:, lanes]
        accs = []
        for parity in range(2):
            h = 2 * hp + parity
            s = lax.dot_general(_head_q(q128, even, parity), k, _NT, preferred_element_type=F32)
            s = s + bias_ref[h, 0:t_new, 0:bp + t_new]
            p = jnp.exp(s - jnp.max(s, axis=-1, keepdims=True)).astype(BF16)
            accs.append(jnp.dot(p, _head_v(v, even, parity), preferred_element_type=F32))
        o_scr[:, lanes] = _head_out(accs[0], accs[1], even)
    o_ref[...] = _rms(o_scr[...], g_ref[...]).astype(BF16)


def _band_sample(q, kn, vn, knf, vnf, kc, vc, bias, g_out):
    b, t, w = q.shape
    bp = kc.shape[1]
    assert t == CHUNK and bp == LEFT
    new = lambda: pl.BlockSpec((None, t, w), lambda i: (i, 0, 0))
    buf = lambda: pl.BlockSpec((None, bp, w), lambda i: (i, 0, 0))
    return pl.pallas_call(
        functools.partial(_band_sample_body, t_new=t, bp=bp),
        grid=(b,),
        in_specs=[new(), new(), new(), new(), new(), buf(), buf(),
                  pl.BlockSpec(bias.shape, lambda i: (0, 0, 0)),
                  pl.BlockSpec((1, w), lambda i: (0, 0))],
        out_specs=[new(), buf(), buf()],
        out_shape=[jax.ShapeDtypeStruct((b, t, w), BF16), jax.ShapeDtypeStruct((b, bp, w), F32),
                   jax.ShapeDtypeStruct((b, bp, w), F32)],
        scratch_shapes=[pltpu.VMEM((bp + t, w), BF16), pltpu.VMEM((bp + t, w), BF16), pltpu.VMEM((t, w), F32)],
        compiler_params=pltpu.CompilerParams(dimension_semantics=("parallel",), vmem_limit_bytes=VMEM_LIMIT),
        name="band_sample",
    )(q, kn, vn, knf, vnf, kc, vc, bias, g_out)


N_HEADS_MEM = 4
HEAD_DIM_MEM = 128
W_MEM = N_HEADS_MEM * HEAD_DIM_MEM
MEM_SCALE = HEAD_DIM_MEM ** -0.5


def _memkv_body(m_ref, g_ref, w_ref, kf_ref, vf_ref, k_ref, v_ref):
    h = _rms(m_ref[...], g_ref[...]).astype(BF16)
    z = jnp.dot(h, w_ref[...], preferred_element_type=F32)
    kf_ref[...] = z[:, :W_MEM]
    vf_ref[...] = z[:, W_MEM:]
    k_ref[...] = z[:, :W_MEM].astype(BF16)
    v_ref[...] = z[:, W_MEM:].astype(BF16)


def _mem_kv(mem, g_mem, w_ckv):
    b, n, d = mem.shape
    blk = lambda: pl.BlockSpec((None, n, W_MEM), lambda i: (i, 0, 0))
    return pl.pallas_call(
        _memkv_body,
        grid=(b,),
        in_specs=[pl.BlockSpec((None, n, d), lambda i: (i, 0, 0)),
                  pl.BlockSpec((1, d), lambda i: (0, 0)),
                  pl.BlockSpec(w_ckv.shape, lambda i: (0, 0))],
        out_specs=[blk(), blk(), blk(), blk()],
        out_shape=[jax.ShapeDtypeStruct((b, n, W_MEM), F32)] * 2 + [jax.ShapeDtypeStruct((b, n, W_MEM), BF16)] * 2,
        compiler_params=pltpu.CompilerParams(dimension_semantics=("parallel",), vmem_limit_bytes=VMEM_LIMIT),
        name="mem_kv",
    )(mem, g_mem, w_ckv)


N_GROUPS = 4
EXPERTS_PER_GROUP = 8
N_EXPERTS = N_GROUPS * EXPERTS_PER_GROUP
ROUTE_L2 = N_GROUPS
ROUTE_ROWS = 8
POST_CHAIN = 512
R_EID0, R_EID1, R_RANK0, R_RANK1, R_GATE0, R_GATE1 = range(6)


def _lane_max(x, mask):
    return jnp.max(jnp.where(mask, x, -jnp.inf), axis=-1, keepdims=True)


def _first_lane(mask, lane):
    return jnp.min(jnp.where(mask, lane, LANES), axis=-1, keepdims=True)


def _route(logits, lane):
    is_l1 = lane < N_GROUPS
    m1 = _lane_max(logits, is_l1)
    grp = _first_lane(is_l1 & (logits == m1), lane)
    wg = 1.0 / jnp.sum(jnp.where(is_l1, jnp.exp(logits - m1), 0.0), axis=-1, keepdims=True)
    lo = ROUTE_L2 + grp * EXPERTS_PER_GROUP
    in_grp = (lane >= lo) & (lane < lo + EXPERTS_PER_GROUP)
    v0 = _lane_max(logits, in_grp)
    i0 = _first_lane(in_grp & (logits == v0), lane)
    rest = in_grp & (lane != i0)
    v1 = _lane_max(logits, rest)
    i1 = _first_lane(rest & (logits == v1), lane)
    e1 = jnp.exp(v1 - v0)
    den = 1.0 / (1.0 + e1)
    return i0, i1, wg * den, wg * e1 * den


def _post_body(x_ref, a_ref, b_ref, mk_ref, mv_ref, woa_ref, wob_ref, gc_ref, wcq_ref, wco_ref,
               gf_ref, wrt_ref, brt_ref,
               x2_ref, h3_ref, route_ref, routet_ref, cnt_ref, o_scr, *, tm, nsub):
    seq = tm // nsub
    count = jnp.zeros((1, LANES), F32)
    for rows in _chains(tm, POST_CHAIN):
        n = rows.size
        x1 = (x_ref[rows, :] + jnp.dot(a_ref[rows, :], woa_ref[...], preferred_element_type=F32)
              + jnp.dot(b_ref[rows, :], wob_ref[...], preferred_element_type=F32))
        h2 = _rms(x1, gc_ref[...]).astype(BF16)
        qc = (jnp.dot(h2, wcq_ref[...], preferred_element_type=F32) * MEM_SCALE).astype(BF16)
        span = min(seq, n)
        for part in range(n // span):
            sub = (rows.start + part * span) // seq
            rs = slice(part * span, (part + 1) * span)
            orow = pl.ds(rows.start + part * span, span)
            for hm in range(N_HEADS_MEM):
                lanes = slice(hm * HEAD_DIM_MEM, (hm + 1) * HEAD_DIM_MEM)
                s = lax.dot_general(qc[rs, lanes], mk_ref[sub, :, lanes], _NT, preferred_element_type=F32)
                p = jnp.exp(s - jnp.max(s, axis=-1, keepdims=True))
                inv = 1.0 / jnp.sum(p, axis=-1, keepdims=True)
                o_scr[orow, lanes] = jnp.dot(p.astype(BF16), mv_ref[sub, :, lanes], preferred_element_type=F32) * inv
        x2 = x1 + jnp.dot(o_scr[rows, :].astype(BF16), wco_ref[...], preferred_element_type=F32)
        x2_ref[rows, :] = x2
        h3 = _rms(x2, gf_ref[...]).astype(BF16)
        h3_ref[rows, :] = h3

        logits = jnp.dot(h3, wrt_ref[...], preferred_element_type=F32) + brt_ref[...]
        lane = lax.broadcasted_iota(jnp.int32, (n, LANES), 1)
        i0, i1, g0, g1 = _route(logits, lane)
        e0 = i0 - ROUTE_L2
        e1 = i1 - ROUTE_L2
        hit0 = lane == e0
        hit1 = lane == e1
        onehot = jnp.where(hit0 | hit1, 1.0, 0.0)
        row = lax.broadcasted_iota(jnp.int32, (n, n), 0)
        col = lax.broadcasted_iota(jnp.int32, (n, n), 1)
        before = jnp.where(col < row, 1.0, 0.0).astype(BF16)
        seen = jnp.dot(before, onehot.astype(BF16), preferred_element_type=F32) + count
        rank0 = jnp.sum(jnp.where(hit0, seen, 0.0), axis=-1, keepdims=True)
        rank1 = jnp.sum(jnp.where(hit1, seen, 0.0), axis=-1, keepdims=True)
        count = count + jnp.sum(onehot, axis=0, keepdims=True)

        rec = jnp.zeros((n, LANES), F32)
        for idx, val in ((R_EID0, e0.astype(F32)), (R_EID1, e1.astype(F32)), (R_RANK0, rank0),
                         (R_RANK1, rank1), (R_GATE0, g0), (R_GATE1, g1)):
            rec = jnp.where(lane == idx, val, rec)
        route_ref[rows, :] = rec[:, :ROUTE_ROWS]
        routet_ref[:, rows] = rec.T[:ROUTE_ROWS, :]
    cnt_ref[...] = count


def _post_block(x, a_n, b_n, mk, mv, weights, *, tm):
    b, s, d = x.shape
    if s >= tm:
        nsub, grid = 1, (b, s // tm)
        tok = lambda i, j: (i, j, 0)
        flat = lambda i, j: i * (s // tm) + j
    else:
        nsub = tm // s
        assert b % nsub == 0
        x, a_n, b_n = (t.reshape(b // nsub, tm, t.shape[-1]) for t in (x, a_n, b_n))
        grid = (b // nsub, 1)
        tok = lambda i, j: (i, 0, 0)
        flat = lambda i, j: i
    mem = lambda i, j: (i, 0, 0)
    const = lambda arr: pl.BlockSpec(arr.shape, lambda i, j: (0,) * arr.ndim)
    in_specs = [pl.BlockSpec((None, tm, d), tok),
                pl.BlockSpec((None, tm, W_GROUP), tok), pl.BlockSpec((None, tm, W_GROUP), tok),
                pl.BlockSpec((nsub, mk.shape[1], W_MEM), mem), pl.BlockSpec((nsub, mv.shape[1], W_MEM), mem)]
    in_specs += [const(w) for w in weights]
    n = b * s
    out_shape = [jax.ShapeDtypeStruct((n, d), F32), jax.ShapeDtypeStruct((n, d), BF16),
                 jax.ShapeDtypeStruct((n, ROUTE_ROWS), F32), jax.ShapeDtypeStruct((ROUTE_ROWS, n), F32),
                 jax.ShapeDtypeStruct((n // tm, 1, LANES), F32)]
    out_specs = [pl.BlockSpec((tm, d), lambda i, j: (flat(i, j), 0)),
                 pl.BlockSpec((tm, d), lambda i, j: (flat(i, j), 0)),
                 pl.BlockSpec((tm, ROUTE_ROWS), lambda i, j: (flat(i, j), 0)),
                 pl.BlockSpec((ROUTE_ROWS, tm), lambda i, j: (0, flat(i, j))),
                 pl.BlockSpec((None, 1, LANES), lambda i, j: (flat(i, j), 0, 0))]
    return pl.pallas_call(
        functools.partial(_post_body, tm=tm, nsub=nsub),
        grid=grid,
        in_specs=in_specs,
        out_specs=out_specs,
        out_shape=out_shape,
        scratch_shapes=[pltpu.VMEM((tm, W_MEM), F32)],
        compiler_params=pltpu.CompilerParams(
            dimension_semantics=("parallel", "parallel"), vmem_limit_bytes=VMEM_LIMIT),
        name="post_block",
    )(x, a_n, b_n, mk, mv, *weights)


def _prep_post(w_out, g_cross, w_cq, w_co, g_ffn, w_r1, b_r1, w_r2, b_r2):
    pad = LANES - N_GROUPS - N_EXPERTS
    w_rt = jnp.pad(jnp.concatenate([w_r1, w_r2], axis=1), ((0, 0), (0, pad))).astype(BF16)
    b_rt = jnp.pad(jnp.concatenate([b_r1, b_r2]).reshape(1, -1), ((0, 0), (0, pad))).astype(F32)
    return [w_out[:W_GROUP].astype(BF16), w_out[W_GROUP:].astype(BF16), g_cross.reshape(1, -1),
            w_cq.astype(BF16), w_co.astype(BF16), g_ffn.reshape(1, -1), w_rt, b_rt]


D_EXPERT = 512
TOP_K = 2
ROW_TILE = 512
MXU_DIM = 256
RUN_ALIGN = 16
PLAN_TILES = LANES
TILE_TABLE = 2 * LANES


def _local_rows(tm):
    return -(-(TOP_K * tm + N_EXPERTS * (RUN_ALIGN - 1)) // MXU_DIM) * MXU_DIM


def _n_row_tiles(n_tokens, tm):
    rows = n_tokens * TOP_K + (n_tokens // tm) * N_EXPERTS * (RUN_ALIGN - 1) + N_EXPERTS * (ROW_TILE - 1)
    return rows // ROW_TILE


N_CHUNK_LANE = LANES - 1


def _plan_body(cnt_ref, lstart_ref, chunk_ref, offs_ref, te_ref):
    cnt = cnt_ref[...].astype(jnp.int32)
    n16 = ((cnt + (RUN_ALIGN - 1)) & ~(RUN_ALIGN - 1)).astype(F32)
    lend = _lane_cumsum(n16)
    lstart = lend - n16
    earlier = (_lane_cumsum(n16.T) - n16.T).T
    total = jnp.sum(n16, axis=0, keepdims=True).astype(jnp.int32)
    seg = jnp.broadcast_to((total + (ROW_TILE - 1)) & ~(ROW_TILE - 1), (8, LANES)).astype(F32)
    ends = _lane_cumsum(seg)
    offs = ends - seg
    shift = earlier + offs[0:1, :] - lstart
    lstart_ref[...] = lstart.astype(jnp.int32)
    offs_ref[...] = offs[0:1, :].astype(jnp.int32)

    lane = lax.broadcasted_iota(jnp.int32, (PLAN_TILES, LANES), 1)
    local_row = (lane * RUN_ALIGN).astype(F32)
    owner = jnp.zeros((PLAN_TILES, LANES), jnp.int32)
    for e in range(N_EXPERTS):
        owner = owner + jnp.where(lend[:, e:e + 1] <= local_row, 1, 0)
    glob = local_row
    for e in range(N_EXPERTS):
        glob = glob + jnp.where(owner == e, shift[:, e:e + 1], 0.0)
    n_chunks = lend[:, N_EXPERTS - 1:N_EXPERTS] * (1.0 / RUN_ALIGN)
    chunk_ref[...] = jnp.where(lane == N_CHUNK_LANE, n_chunks, glob).astype(jnp.int32)

    tile_start = (lax.broadcasted_iota(jnp.int32, te_ref.shape, 1) * ROW_TILE).astype(F32)
    te = jnp.zeros(te_ref.shape, jnp.int32)
    for e in range(N_EXPERTS):
        end_e = jnp.sum(jnp.where(lane[0:1, :] == e, ends[0:1, :], 0.0), axis=-1, keepdims=True)
        te = te + jnp.where(end_e <= tile_start, 1, 0)
    te_ref[...] = jnp.minimum(te, N_EXPERTS - 1)


def _plan(counts, tm):
    nt = counts.shape[0]
    assert nt <= PLAN_TILES and _local_rows(tm) // RUN_ALIGN <= N_CHUNK_LANE
    cnt = jnp.pad(counts.reshape(nt, LANES), ((0, PLAN_TILES - nt), (0, 0)))
    grid_i32 = jax.ShapeDtypeStruct((PLAN_TILES, LANES), jnp.int32)
    lstart, chunks, offs, te = pl.pallas_call(
        _plan_body,
        out_shape=[grid_i32, grid_i32, jax.ShapeDtypeStruct((1, LANES), jnp.int32),
                   jax.ShapeDtypeStruct((1, TILE_TABLE), jnp.int32)],
        name="moe_plan",
    )(cnt)
    per_tile = lambda t: t[:nt].reshape(nt, 1, LANES)
    return per_tile(lstart), per_tile(chunks), offs.reshape(LANES), te.reshape(TILE_TABLE)


def _for_each_chunk(chunk_ref, fn):
    def body(c, carry):
        fn(pl.multiple_of(c * RUN_ALIGN, RUN_ALIGN), pl.multiple_of(chunk_ref[0, c], RUN_ALIGN))
        return carry

    lax.fori_loop(0, chunk_ref[0, N_CHUNK_LANE], body, None)


def _local_positions_row(rt_ref, lstart_ref):
    pos = []
    for r_eid, r_rank in ((R_EID0, R_RANK0), (R_EID1, R_RANK1)):
        eid = rt_ref[r_eid:r_eid + 1, :].astype(jnp.int32)
        p = rt_ref[r_rank:r_rank + 1, :].astype(jnp.int32)
        for e in range(N_EXPERTS):
            p = p + jnp.where(eid == e, lstart_ref[0, e], 0)
        pos.append(p)
    return pos


def _dispatch_body(offs_ref, lstart_ref, chunk_ref, pchunk_ref,
                   hp_ref, hs_ref, rt_ref, xs_ref, loc, zeros, sems, zsem, *, tm, n_prompt_tiles):
    i = pl.program_id(0)
    n_tiles = xs_ref.shape[0] // ROW_TILE
    half = lax.rem(i, 2)

    @pl.when(i == 0)
    def _():
        zeros[...] = jnp.zeros(zeros.shape, zeros.dtype)
        zero_tile = lambda row: pltpu.make_async_copy(
            zeros, xs_ref.at[pl.ds(pl.multiple_of(row, ROW_TILE), ROW_TILE)], zsem)
        n_used = offs_ref[N_EXPERTS] // ROW_TILE

        def tail(j, carry, op):
            op(zero_tile(j * ROW_TILE))
            return carry

        for op in (lambda c: c.start(), lambda c: c.wait()):
            for e in range(N_EXPERTS):
                @pl.when(offs_ref[e + 1] > offs_ref[e])
                def _():
                    op(zero_tile(offs_ref[e + 1] - ROW_TILE))
            lax.fori_loop(n_used, n_tiles, functools.partial(tail, op=op), None)

    pos0, pos1 = _local_positions_row(rt_ref, lstart_ref)
    slot = lax.broadcasted_iota(jnp.int32, (loc.shape[1], tm), 0)
    perm = jnp.where(slot == pos0, 1.0, jnp.where(slot == pos1, 1.0, 0.0)).astype(BF16)

    @pl.when(i < n_prompt_tiles)
    def _():
        loc[half] = jnp.dot(perm, hp_ref[...], preferred_element_type=F32).astype(BF16)

    @pl.when(i >= n_prompt_tiles)
    def _():
        loc[half] = jnp.dot(perm, hs_ref[...], preferred_element_type=F32).astype(BF16)

    def chunk(buf, lo, go):
        return pltpu.make_async_copy(loc.at[buf, pl.ds(lo, RUN_ALIGN)], xs_ref.at[pl.ds(go, RUN_ALIGN)],
                                     sems.at[buf])

    @pl.when(i > 0)
    def _():
        _for_each_chunk(pchunk_ref, lambda lo, go: chunk(1 - half, lo, go).wait())

    _for_each_chunk(chunk_ref, lambda lo, go: chunk(half, lo, go).start())

    @pl.when(i == pl.num_programs(0) - 1)
    def _():
        _for_each_chunk(chunk_ref, lambda lo, go: chunk(half, lo, go).wait())


def _dispatch(h3_prompt, h3_sample, route_t, plan, *, tm):
    lstart, chunks, offs, _ = plan
    n_p, n_s, d = h3_prompt.shape[0], h3_sample.shape[0], h3_prompt.shape[-1]
    assert n_p % tm == 0 and n_s % tm == 0
    n = n_p + n_s
    nt = n // tm
    npt = n_p // tm
    n_rows = _n_row_tiles(n, tm) * ROW_TILE
    smem_tile = lambda: pl.BlockSpec((None, 1, LANES), lambda i, offs: (i, 0, 0), memory_space=pltpu.SMEM)
    smem_prev = lambda: pl.BlockSpec((None, 1, LANES), lambda i, offs: (jnp.maximum(i - 1, 0), 0, 0),
                                     memory_space=pltpu.SMEM)
    return pl.pallas_call(
        functools.partial(_dispatch_body, tm=tm, n_prompt_tiles=npt),
        grid_spec=pltpu.PrefetchScalarGridSpec(
            num_scalar_prefetch=1,
            grid=(nt,),
            in_specs=[smem_tile(), smem_tile(), smem_prev(),
                      pl.BlockSpec((tm, d), lambda i, offs: (jnp.minimum(i, npt - 1), 0)),
                      pl.BlockSpec((tm, d), lambda i, offs: (jnp.maximum(i - npt, 0), 0)),
                      pl.BlockSpec((ROUTE_ROWS, tm), lambda i, offs: (0, i))],
            out_specs=pl.BlockSpec(memory_space=pl.ANY),
            scratch_shapes=[pltpu.VMEM((2, _local_rows(tm), d), BF16), pltpu.VMEM((ROW_TILE, d), BF16),
                            pltpu.SemaphoreType.DMA((2,)), pltpu.SemaphoreType.DMA(())]),
        out_shape=jax.ShapeDtypeStruct((n_rows, d), BF16),
        compiler_params=pltpu.CompilerParams(dimension_semantics=("arbitrary",), vmem_limit_bytes=VMEM_LIMIT),
        name="moe_dispatch",
    )(offs, lstart, chunks, chunks, h3_prompt, h3_sample, route_t)


def _experts_body(te_ref, offs_ref, xs_ref, wg_ref, wu_ref, wd_ref, ys_ref, wg_bf, wu_bf, wd_bf):
    i = pl.program_id(0)
    n_used = offs_ref[N_EXPERTS] // ROW_TILE

    @pl.when(i < n_used)
    def _():
        @pl.when((i == 0) | (te_ref[i] != te_ref[jnp.maximum(i - 1, 0)]))
        def _():
            wg_bf[...] = wg_ref[...].astype(BF16)
            wu_bf[...] = wu_ref[...].astype(BF16)
            wd_bf[...] = wd_ref[...].astype(BF16)

        dot = functools.partial(jnp.dot, preferred_element_type=F32)
        for rows in _chains(ROW_TILE):
            x = xs_ref[rows, :]
            gate = dot(x, wg_bf[...])
            up = dot(x, wu_bf[...])
            act = (gate * jax.nn.sigmoid(gate) * up).astype(BF16)
            ys_ref[rows, :] = dot(act, wd_bf[...]).astype(BF16)

    @pl.when(i >= n_used)
    def _():
        ys_ref[...] = jnp.zeros(ys_ref.shape, ys_ref.dtype)


def _experts(xs, te, offs, w_gate, w_up, w_down):
    n_rows, d = xs.shape
    last = lambda i, te, offs: jnp.minimum(i, offs[N_EXPERTS] // ROW_TILE - 1)
    wspec = lambda shape: pl.BlockSpec((None,) + shape, lambda i, te, offs: (te[last(i, te, offs)], 0, 0))
    return pl.pallas_call(
        _experts_body,
        grid_spec=pltpu.PrefetchScalarGridSpec(
            num_scalar_prefetch=2,
            grid=(n_rows // ROW_TILE,),
            in_specs=[pl.BlockSpec((ROW_TILE, d), lambda i, te, offs: (last(i, te, offs), 0)),
                      wspec((d, D_EXPERT)), wspec((d, D_EXPERT)), wspec((D_EXPERT, d))],
            out_specs=pl.BlockSpec((ROW_TILE, d), lambda i, te, offs: (i, 0)),
            scratch_shapes=[pltpu.VMEM((d, D_EXPERT), BF16), pltpu.VMEM((d, D_EXPERT), BF16),
                            pltpu.VMEM((D_EXPERT, d), BF16)]),
        out_shape=jax.ShapeDtypeStruct((n_rows, d), BF16),
        compiler_params=pltpu.CompilerParams(dimension_semantics=("arbitrary",), vmem_limit_bytes=VMEM_LIMIT),
        name="moe_experts",
    )(te, offs, xs, w_gate, w_up, w_down)


def _combine_body(chunk_ref, nchunk_ref, x2p_ref, x2s_ref, route_ref, lsv_ref, ys_ref, g_ref,
                  yp_ref, ysm_ref, loc, sems, *, tm, n_prompt_tiles):
    i = pl.program_id(0)
    half = lax.rem(i, 2)

    def chunk(buf, lo, go):
        return pltpu.make_async_copy(ys_ref.at[pl.ds(go, RUN_ALIGN)], loc.at[buf, pl.ds(lo, RUN_ALIGN)],
                                     sems.at[buf])

    @pl.when(i == 0)
    def _():
        loc[...] = jnp.zeros(loc.shape, loc.dtype)
        _for_each_chunk(chunk_ref, lambda lo, go: chunk(0, lo, go).start())

    @pl.when(i + 1 < pl.num_programs(0))
    def _():
        _for_each_chunk(nchunk_ref, lambda lo, go: chunk(1 - half, lo, go).start())

    _for_each_chunk(chunk_ref, lambda lo, go: chunk(half, lo, go).wait())

    lane = lax.broadcasted_iota(jnp.int32, (tm, LANES), 1)
    slot = lax.broadcasted_iota(jnp.int32, (tm, loc.shape[1]), 1)
    weights = jnp.zeros(slot.shape, F32)
    for r_eid, r_rank, r_gate in ((R_EID0, R_RANK0, R_GATE0), (R_EID1, R_RANK1, R_GATE1)):
        eid = route_ref[:, r_eid:r_eid + 1].astype(jnp.int32)
        start = jnp.sum(jnp.where(lane == eid, lsv_ref[...], 0), axis=-1, keepdims=True)
        pos = route_ref[:, r_rank:r_rank + 1].astype(jnp.int32) + start
        weights = jnp.where(slot == pos, route_ref[:, r_gate:r_gate + 1], weights)
    moe = jnp.dot(weights.astype(BF16), loc[half], preferred_element_type=F32)

    @pl.when(i < n_prompt_tiles)
    def _():
        yp_ref[...] = _rms(x2p_ref[...] + moe, g_ref[...])

    @pl.when(i >= n_prompt_tiles)
    def _():
        ysm_ref[...] = _rms(x2s_ref[...] + moe, g_ref[...])


def _combine(x2_prompt, x2_sample, route, plan, ys, g_final, *, tm):
    lstart, chunks, _, _ = plan
    (n_p, d), n_s = x2_prompt.shape, x2_sample.shape[0]
    assert n_p % tm == 0 and n_s % tm == 0
    npt = n_p // tm
    nt = npt + n_s // tm
    smem_tile = lambda: pl.BlockSpec((None, 1, LANES), lambda i: (i, 0, 0), memory_space=pltpu.SMEM)
    smem_next = lambda: pl.BlockSpec((None, 1, LANES), lambda i: (jnp.minimum(i + 1, nt - 1), 0, 0),
                                     memory_space=pltpu.SMEM)
    prompt_tile = lambda: pl.BlockSpec((tm, d), lambda i: (jnp.minimum(i, npt - 1), 0))
    sample_tile = lambda: pl.BlockSpec((tm, d), lambda i: (jnp.maximum(i - npt, 0), 0))
    return pl.pallas_call(
        functools.partial(_combine_body, tm=tm, n_prompt_tiles=npt),
        grid=(nt,),
        in_specs=[smem_tile(), smem_next(), prompt_tile(), sample_tile(),
                  pl.BlockSpec((tm, ROUTE_ROWS), lambda i: (i, 0)),
                  pl.BlockSpec((None, 1, LANES), lambda i: (i, 0, 0)),
                  pl.BlockSpec(memory_space=pl.ANY),
                  pl.BlockSpec((1, d), lambda i: (0, 0))],
        out_specs=[prompt_tile(), sample_tile()],
        out_shape=[jax.ShapeDtypeStruct((n_p, d), F32), jax.ShapeDtypeStruct((n_s, d), F32)],
        scratch_shapes=[pltpu.VMEM((2, _local_rows(tm), d), BF16), pltpu.SemaphoreType.DMA((2,))],
        compiler_params=pltpu.CompilerParams(dimension_semantics=("arbitrary",), vmem_limit_bytes=VMEM_LIMIT),
        name="moe_combine",
    )(chunks, chunks, x2_prompt, x2_sample, route, lstart, ys, g_final)


TOKEN_TILE = 512
FOX_Q_TILE = 256
FOX_CACHE_TILE = 1024


def kernel(x_prompt, x_sample, cache_fox_k, cache_fox_v, cache_fox_logf, cache_band_k, cache_band_v, cache_mem_k, cache_mem_v, mem_prompt, g_mix, w_in, b_forget, g_out_fox, g_out_band, rel_table, w_out, g_cross, g_mem, w_cq, w_ck, w_cv, w_co, g_ffn, w_router1, b_router1, w_router2, b_router2, w_exp_gate, w_exp_up, w_exp_down, g_final):
    assert g_mix.shape[0] == 1, "single-layer model"
    bsz, seq, d = x_prompt.shape
    sb, st, _ = x_sample.shape
    n_s = sb * st
    n_mem = mem_prompt.shape[1]
    row = lambda g: g.reshape(1, -1)

    w_pad, bf_pad, g_mix_r = _prep_proj(w_in[0], b_forget[0], g_mix[0])
    g_of, g_ob = row(g_out_fox[0]), row(g_out_band[0])
    bias_t, bias_s = _band_bias(_prep_band_bias_row(rel_table[0]))

    qx, kx, vat, qxb, kxb, vbt, kaf, vaf, kbf, vbf, logf = _proj(
        x_prompt, g_mix_r, w_pad, bf_pad, tm=TOKEN_TILE, prompt=True)
    a_p = _fox_prompt(qx, kx, vat, g_of, tq=FOX_Q_TILE)
    b_p = _band_prompt(qxb, kxb, vbt, bias_t, g_ob)

    s_out = _proj(x_sample.reshape(1, n_s, d), g_mix_r, w_pad, bf_pad, tm=n_s, prompt=False)
    sqa, ska, sva, sqb, skb, svb, skaf, svaf, skbf, svbf = (t.reshape(sb, st, W_GROUP) for t in s_out[:10])
    slogf = s_out[10].reshape(sb, st, N_HEADS)
    slft = s_out[11].reshape(N_HEADS, sb, st).transpose(1, 0, 2)
    past = cache_fox_k.shape[2]
    a_s = _fox_sample(sqa, ska, sva, slft,
                      cache_fox_k[0].reshape(sb, past * N_HEADS, HEAD_DIM),
                      cache_fox_v[0].reshape(sb, past * N_HEADS, HEAD_DIM),
                      cache_fox_logf[0].transpose(0, 2, 1), g_of, pt=FOX_CACHE_TILE)
    bp = cache_band_k.shape[2]
    b_s, nbk, nbv = _band_sample(sqb, skb, svb, skbf, svbf,
                                 cache_band_k[0].reshape(sb, bp, W_GROUP), cache_band_v[0].reshape(sb, bp, W_GROUP),
                                 bias_s, g_ob)

    w_ckv = jnp.concatenate([w_ck[0], w_cv[0]], axis=1).astype(BF16)
    mkf, mvf, mk, mv = _mem_kv(mem_prompt, row(g_mem[0]), w_ckv)
    post_w = _prep_post(w_out[0], g_cross[0], w_cq[0], w_co[0], g_ffn[0],
                        w_router1[0], b_router1[0], w_router2[0], b_router2[0])
    x2_p, h3_p, route_p, routet_p, cnt_p = _post_block(x_prompt, a_p, b_p, mk, mv, post_w, tm=TOKEN_TILE)
    cmk = cache_mem_k[0].reshape(sb, n_mem, W_MEM).astype(BF16)
    cmv = cache_mem_v[0].reshape(sb, n_mem, W_MEM).astype(BF16)
    x2_s, h3_s, route_s, routet_s, cnt_s = _post_block(x_sample, a_s, b_s, cmk, cmv, post_w, tm=TOKEN_TILE)
    route = jnp.concatenate([route_p, route_s], axis=0)
    route_t = jnp.concatenate([routet_p, routet_s], axis=1)

    plan = _plan(jnp.concatenate([cnt_p, cnt_s], axis=0), TOKEN_TILE)
    xs = _dispatch(h3_p, h3_s, route_t, plan, tm=TOKEN_TILE)
    ys = _experts(xs, plan[3], plan[2], w_exp_gate[0], w_exp_up[0], w_exp_down[0])
    y_p, y_s = _combine(x2_p, x2_s, route, plan, ys, row(g_final), tm=TOKEN_TILE)

    heads = lambda t, n: t.reshape(1, n, -1, N_HEADS, HEAD_DIM)
    mem_heads = lambda t: t.reshape(1, bsz, n_mem, N_HEADS_MEM, HEAD_DIM_MEM)
    return (y_p.reshape(bsz, seq, d), y_s.reshape(sb, st, d),
            heads(kaf, bsz), heads(vaf, bsz), logf.reshape(1, bsz, seq, N_HEADS),
            heads(kbf, bsz), heads(vbf, bsz), mem_heads(mkf), mem_heads(mvf),
            heads(skaf, sb), heads(svaf, sb), slogf.reshape(1, sb, st, N_HEADS),
            heads(nbk, sb), heads(nbv, sb))
```

```python
import functools

import jax
import jax.numpy as jnp
from jax import lax
from jax.experimental import pallas as pl
from jax.experimental.pallas import tpu as pltpu

F32 = jnp.float32
BF16 = jnp.bfloat16

D_MODEL = 1024
HEAD_DIM = 64
N_HEADS = 8
W_GROUP = N_HEADS * HEAD_DIM
N_PAIRS = N_HEADS // 2
CHUNK = 64
LEFT_CHUNKS = 8
LEFT = LEFT_CHUNKS * CHUNK
REL_CLIP = 128
EPS = 1e-6
NEG_INF = -1e30
ATTN_SCALE = HEAD_DIM ** -0.5
LANES = 128
PROJ_PAD = 3 * W_GROUP * 2 + LANES
VMEM_LIMIT = 56 * 1024 * 1024


def _rms(x, g):
    ms = jnp.mean(x * x, axis=-1, keepdims=True)
    return x * lax.rsqrt(ms + EPS) * g


def _log_sigmoid(x):
    return -(jnp.maximum(-x, 0.0) + jnp.log1p(jnp.exp(-jnp.abs(x))))


def _lane_cumsum(x):
    n = x.shape[-1]
    lane = lax.broadcasted_iota(jnp.int32, x.shape, 1)
    k = 1
    while k < n:
        x = x + jnp.where(lane >= k, pltpu.roll(x, k, axis=1), 0.0)
        k *= 2
    return x


LOG2E = 1.4426950408889634
SCALE_BASE2 = ATTN_SCALE * LOG2E


def _split3(x):
    hi = x.astype(BF16).astype(F32)
    mid = (x - hi).astype(BF16).astype(F32)
    lo = x - hi - mid
    return hi, mid, lo


def _extra_lane(parity):
    return HEAD_DIM if parity == 0 else 0


def _fox_extras(c3t, hp, tm):
    row = lax.broadcasted_iota(jnp.int32, (8, tm), 0)

    def group(h, q_side):
        hi, mid, lo = (p[h:h + 1, :] for p in c3t)
        if q_side:
            return jnp.where(row < 3, 1.0, jnp.where(row == 3, hi, jnp.where(row == 4, mid, jnp.where(row == 5, lo, 0.0))))
        return jnp.where(row == 0, -hi, jnp.where(row == 1, -mid, jnp.where(row == 2, -lo, jnp.where(row < 6, 1.0, 0.0))))

    gap = jnp.zeros((HEAD_DIM - 8, tm), F32)
    sides = []
    for q_side in (True, False):
        t = jnp.concatenate([group(2 * hp + 1, q_side), gap, group(2 * hp, q_side), gap], axis=0)
        sides.append(t.T)
    return sides


def _head_blocks(x128, extras, lane):
    return (jnp.where(lane < HEAD_DIM, x128, extras).astype(BF16),
            jnp.where(lane >= HEAD_DIM, x128, extras).astype(BF16))


Q_A, K_A, V_A, Q_B, K_B, V_B = range(6)


PROJ_CHAIN = 256


def _chains(tm, chain=PROJ_CHAIN):
    n = max(tm // chain, 1)
    return [pl.ds(i * (tm // n), tm // n) for i in range(n)]


def _proj_common(rows, x_ref, g_ref, w_ref, bf_ref, kaf_ref, vaf_ref, kbf_ref, vbf_ref, logf_ref, keep_tiles):
    s = pl.program_id(1)
    ns = pl.num_programs(1)
    h = _rms(x_ref[rows, :], g_ref[...]).astype(BF16)
    w = W_GROUP
    zf = jnp.dot(h, w_ref[:, 6 * w:6 * w + LANES], preferred_element_type=F32)
    z = [jnp.dot(h, w_ref[:, g * w:(g + 1) * w], preferred_element_type=F32) for g in range(6)]
    kaf_ref[rows, :] = z[K_A]
    vaf_ref[rows, :] = z[V_A]

    @pl.when(s >= ns - keep_tiles)
    def _():
        kbf_ref[rows, :] = z[K_B]
        vbf_ref[rows, :] = z[V_B]

    logf = _log_sigmoid(zf + bf_ref[...])
    logf_ref[rows, :] = logf[:, :N_HEADS]
    return z, logf


def _proj_prompt_body(x_ref, g_ref, w_ref, bf_ref, qx_ref, kx_ref, vat_ref, qxb_ref, kxb_ref, vbt_ref,
                      kaf_ref, vaf_ref, kbf_ref, vbf_ref, logf_ref, carry_ref, *, tm, keep_tiles):
    @pl.when(pl.program_id(1) == 0)
    def _():
        carry_ref[...] = jnp.zeros_like(carry_ref)

    for rows in _chains(tm):
        n = rows.size
        z, logf = _proj_common(rows, x_ref, g_ref, w_ref, bf_ref, kaf_ref, vaf_ref, kbf_ref, vbf_ref, logf_ref,
                               keep_tiles)
        vat_ref[:, rows] = z[V_A].T.astype(BF16)
        vbt_ref[:, rows] = z[V_B].T.astype(BF16)
        ct = _lane_cumsum(logf.T[:N_HEADS, :]) + carry_ref[:, 0:1]
        carry_ref[...] = jnp.broadcast_to(ct[:, n - 1:n], carry_ref.shape)
        c3t = _split3(ct * LOG2E)
        lane = lax.broadcasted_iota(jnp.int32, (n, LANES), 1)
        band_q_extras = jnp.where((lane == _extra_lane(0)) | (lane == _extra_lane(1)), 1.0, 0.0)
        band_k_extras = jnp.zeros((n, LANES), F32)
        for hp in range(N_PAIRS):
            blocks = slice(2 * hp * LANES, 2 * (hp + 1) * LANES)
            blk = lambda group, hp=hp, z=z: z[group][:, hp * LANES:(hp + 1) * LANES]
            q_extras, k_extras = _fox_extras(c3t, hp, n)
            qx_ref[rows, blocks] = jnp.concatenate(_head_blocks(blk(Q_A) * SCALE_BASE2, q_extras, lane), axis=1)
            kx_ref[rows, blocks] = jnp.concatenate(_head_blocks(blk(K_A), k_extras, lane), axis=1)
            qxb_ref[rows, blocks] = jnp.concatenate(_head_blocks(blk(Q_B) * SCALE_BASE2, band_q_extras, lane), axis=1)
            kxb_ref[rows, blocks] = jnp.concatenate(_head_blocks(blk(K_B), band_k_extras, lane), axis=1)


def _proj_sample_body(x_ref, g_ref, w_ref, bf_ref, qa_ref, ka_ref, va_ref, qb_ref, kb_ref, vb_ref,
                      kaf_ref, vaf_ref, kbf_ref, vbf_ref, logf_ref, lt_ref, *, tm, keep_tiles):
    for rows in _chains(tm):
        z, logf = _proj_common(rows, x_ref, g_ref, w_ref, bf_ref, kaf_ref, vaf_ref, kbf_ref, vbf_ref, logf_ref,
                               keep_tiles)
        qa_ref[rows, :] = (z[Q_A] * ATTN_SCALE).astype(BF16)
        ka_ref[rows, :] = z[K_A].astype(BF16)
        va_ref[rows, :] = z[V_A].astype(BF16)
        qb_ref[rows, :] = (z[Q_B] * ATTN_SCALE).astype(BF16)
        kb_ref[rows, :] = z[K_B].astype(BF16)
        vb_ref[rows, :] = z[V_B].astype(BF16)
        lt_ref[:, rows] = logf.T[:N_HEADS, :]


def _proj(x, g_mix, w_pad, bf_pad, *, tm, prompt):
    b, s, d = x.shape
    ns = s // tm
    keep = min(LEFT, s)
    assert s % tm == 0 and keep % tm == 0
    keep_tiles = keep // tm
    row = pl.BlockSpec((None, tm, W_GROUP), lambda i, j: (i, j, 0))
    wide = pl.BlockSpec((None, tm, N_HEADS * LANES), lambda i, j: (i, j, 0))
    col = pl.BlockSpec((None, W_GROUP, tm), lambda i, j: (i, 0, j))
    keep_spec = pl.BlockSpec((None, tm, W_GROUP), lambda i, j: (i, jnp.maximum(j - (ns - keep_tiles), 0), 0))
    heads_row = pl.BlockSpec((None, tm, N_HEADS), lambda i, j: (i, j, 0))
    heads_col = pl.BlockSpec((None, N_HEADS, tm), lambda i, j: (i, 0, j))
    const = lambda shape: pl.BlockSpec(shape, lambda i, j: (0,) * len(shape))
    rows_bf = jax.ShapeDtypeStruct((b, s, W_GROUP), BF16)
    wide_bf = jax.ShapeDtypeStruct((b, s, N_HEADS * LANES), BF16)
    cols_bf = jax.ShapeDtypeStruct((b, W_GROUP, s), BF16)
    f32_tail = [jax.ShapeDtypeStruct((b, s, W_GROUP), F32)] * 2
    f32_tail += [jax.ShapeDtypeStruct((b, keep, W_GROUP), F32)] * 2
    f32_tail += [jax.ShapeDtypeStruct((b, s, N_HEADS), F32)]
    tail_specs = [row, row, keep_spec, keep_spec, heads_row]
    if prompt:
        body = functools.partial(_proj_prompt_body, tm=tm, keep_tiles=keep_tiles)
        out_shape = [wide_bf, wide_bf, cols_bf, wide_bf, wide_bf, cols_bf] + f32_tail
        out_specs = [wide, wide, col, wide, wide, col] + tail_specs
        scratch = [pltpu.VMEM((N_HEADS, LANES), F32)]
    else:
        body = functools.partial(_proj_sample_body, tm=tm, keep_tiles=keep_tiles)
        out_shape = [rows_bf] * 6 + f32_tail + [jax.ShapeDtypeStruct((b, N_HEADS, s), F32)]
        out_specs = [row] * 6 + tail_specs + [heads_col]
        scratch = []
    return pl.pallas_call(
        body,
        grid=(b, ns),
        in_specs=[pl.BlockSpec((None, tm, d), lambda i, j: (i, j, 0)),
                  const((1, d)), const(w_pad.shape), const((1, LANES))],
        out_specs=out_specs,
        out_shape=out_shape,
        scratch_shapes=scratch,
        compiler_params=pltpu.CompilerParams(
            dimension_semantics=("parallel", "arbitrary"), vmem_limit_bytes=VMEM_LIMIT),
        name="proj",
    )(x, g_mix, w_pad, bf_pad)


def _prep_proj(w_in, b_forget, g_mix):
    cols = w_in.shape[-1]
    w_pad = jnp.pad(w_in, ((0, 0), (0, PROJ_PAD - cols))).astype(BF16)
    bf_pad = jnp.pad(b_forget.reshape(1, -1), ((0, 0), (0, LANES - N_HEADS))).astype(F32)
    return w_pad, bf_pad, g_mix.reshape(1, -1)


def _pair_masks():
    lane = lax.broadcasted_iota(jnp.int32, (1, LANES), 1)
    return lane < HEAD_DIM


def _head_q(q128, even_lanes, parity):
    keep = even_lanes if parity == 0 else jnp.logical_not(even_lanes)
    return jnp.where(keep, q128, jnp.zeros_like(q128))


def _head_v(v128, even_lanes, parity):
    keep = even_lanes if parity == 0 else jnp.logical_not(even_lanes)
    return jnp.where(keep, v128, jnp.ones_like(v128))


def _head_out(acc_even, acc_odd, even_lanes):
    inv_e = 1.0 / acc_even[:, HEAD_DIM:HEAD_DIM + 1]
    inv_o = 1.0 / acc_odd[:, 0:1]
    return jnp.where(even_lanes, acc_even * inv_e, acc_odd * inv_o)


_NT = (((1,), (1,)), ((), ()))


def _pair_rows():
    row = lax.broadcasted_iota(jnp.int32, (LANES, 1), 0)
    return row < HEAD_DIM


def _head_vt(vt128, even_rows, parity):
    keep = even_rows if parity == 0 else jnp.logical_not(even_rows)
    return jnp.where(keep, vt128, jnp.ones_like(vt128))


def _head_out_t(acc_even, acc_odd, even_rows):
    inv_e = 1.0 / acc_even[HEAD_DIM:HEAD_DIM + 1, :]
    inv_o = 1.0 / acc_odd[0:1, :]
    return jnp.where(even_rows, acc_even * inv_e, acc_odd * inv_o)


def _fox_body(qx_ref, kx_ref, vt_ref, g_ref, o_ref, s_scr, p_scr, m_scr, alpha_scr, acc_scr, ot_scr, *, tq):
    qi = pl.program_id(1)
    even_rows = _pair_rows()
    m_scr[...] = jnp.full(m_scr.shape, NEG_INF, F32)
    acc_scr[...] = jnp.zeros(acc_scr.shape, F32)
    key = lax.broadcasted_iota(jnp.int32, (tq, tq), 0)
    qry = lax.broadcasted_iota(jnp.int32, (tq, tq), 1)
    causal = key <= qry

    def logits(j, half):
        start = pl.multiple_of(j * tq, tq)
        for h in range(N_HEADS):
            head = slice(h * LANES, (h + 1) * LANES)
            s_scr[half, h] = lax.dot_general(kx_ref[pl.ds(start, tq), head], qx_ref[:, head], _NT,
                                             preferred_element_type=F32)

    def weigh(j, half, masked):
        start = pl.multiple_of(j * tq, tq)
        for h in range(N_HEADS):
            st = s_scr[half, h]
            if masked:
                st = jnp.where(causal, st, NEG_INF)
            m_old = m_scr[h:h + 1, :]
            m_new = jnp.maximum(m_old, jnp.max(st, axis=0, keepdims=True))
            p_scr[h] = jnp.exp2(st - m_new).astype(BF16)
            alpha_scr[h:h + 1, :] = jnp.exp2(m_old - m_new)
            m_scr[h:h + 1, :] = m_new
        for h in range(N_HEADS):
            pair = slice((h // 2) * LANES, (h // 2 + 1) * LANES)
            vt = _head_vt(vt_ref[pair, pl.ds(start, tq)], even_rows, h % 2)
            acc_scr[h] = acc_scr[h] * alpha_scr[h:h + 1, :] + jnp.dot(vt, p_scr[h], preferred_element_type=F32)

    def step(j, cur, nxt):
        logits(j + 1, nxt)
        weigh(j, cur, False)

    def body(jj, carry):
        step(2 * jj, 0, 1)
        step(2 * jj + 1, 1, 0)
        return carry

    logits(0, 0)
    lax.fori_loop(0, qi // 2, body, None)
    odd = lax.rem(qi, 2) == 1

    @pl.when(odd)
    def _():
        step(qi - 1, 0, 1)
        weigh(qi, 1, True)

    @pl.when(jnp.logical_not(odd))
    def _():
        weigh(qi, 0, True)
    for hp in range(N_PAIRS):
        ot_scr[hp * LANES:(hp + 1) * LANES, :] = _head_out_t(acc_scr[2 * hp], acc_scr[2 * hp + 1], even_rows)
    o_ref[...] = _rms(ot_scr[...].T, g_ref[...]).astype(BF16)


def _fox_prompt(qx, kx, vat, g_out, *, tq):
    b, s, wx = qx.shape
    w = vat.shape[1]
    return pl.pallas_call(
        functools.partial(_fox_body, tq=tq),
        grid=(b, s // tq),
        in_specs=[pl.BlockSpec((None, tq, wx), lambda i, j: (i, j, 0)),
                  pl.BlockSpec((None, s, wx), lambda i, j: (i, 0, 0)),
                  pl.BlockSpec((None, w, s), lambda i, j: (i, 0, 0)),
                  pl.BlockSpec((1, w), lambda i, j: (0, 0))],
        out_specs=pl.BlockSpec((None, tq, w), lambda i, j: (i, j, 0)),
        out_shape=jax.ShapeDtypeStruct((b, s, w), BF16),
        scratch_shapes=[pltpu.VMEM((2, N_HEADS, tq, tq), F32), pltpu.VMEM((N_HEADS, tq, tq), BF16),
                        pltpu.VMEM((N_HEADS, tq), F32), pltpu.VMEM((N_HEADS, tq), F32),
                        pltpu.VMEM((N_HEADS, LANES, tq), F32), pltpu.VMEM((w, tq), F32)],
        compiler_params=pltpu.CompilerParams(
            dimension_semantics=("parallel", "arbitrary"), vmem_limit_bytes=VMEM_LIMIT),
        name="fox_prompt",
    )(qx, kx, vat, g_out)


BAND_CHUNKS = 4
BAND_Q = BAND_CHUNKS * CHUNK
BAND_K = (LEFT_CHUNKS + BAND_CHUNKS) * CHUNK
BIAS_ROW = BAND_K + BAND_Q
BAND_K_SAMPLE = (LEFT + CHUNK + LANES - 1) // LANES * LANES


def _prep_band_bias_row(rel_table):
    pivot = LEFT + BAND_Q
    n_hi = pivot - REL_CLIP + 1
    n_mid = min(2 * REL_CLIP, BIAS_ROW - n_hi)
    n_lo = BIAS_ROW - n_hi - n_mid
    parts = [jnp.broadcast_to(rel_table[2 * REL_CLIP:], (n_hi, N_HEADS)),
             rel_table[2 * REL_CLIP - 1::-1][:n_mid],
             jnp.broadcast_to(rel_table[:1], (n_lo, N_HEADS))]
    return jnp.concatenate(parts, axis=0).T.reshape(N_HEADS, 1, BIAS_ROW)


def _band_bias_body(row_ref, bt_ref, bs_ref):
    rows = jnp.broadcast_to(row_ref[...], (BAND_Q, BIAS_ROW))
    skew = pltpu.roll(rows, 0, axis=1, stride=1, stride_axis=0)
    bias = skew[:, BAND_Q:]
    bs_ref[...] = bias[:CHUNK, :BAND_K_SAMPLE]
    qc = lax.broadcasted_iota(jnp.int32, (BAND_Q, BAND_K), 0) // CHUNK
    kc = lax.broadcasted_iota(jnp.int32, (BAND_Q, BAND_K), 1) // CHUNK
    bt_ref[...] = jnp.where((kc >= qc) & (kc <= qc + LEFT_CHUNKS), bias * LOG2E, NEG_INF).T


def _band_bias(bias_row):
    return pl.pallas_call(
        _band_bias_body,
        grid=(N_HEADS,),
        in_specs=[pl.BlockSpec((None, 1, BIAS_ROW), lambda h: (h, 0, 0))],
        out_specs=[pl.BlockSpec((None, BAND_K, BAND_Q), lambda h: (h, 0, 0)),
                   pl.BlockSpec((None, CHUNK, BAND_K_SAMPLE), lambda h: (h, 0, 0))],
        out_shape=[jax.ShapeDtypeStruct((N_HEADS, BAND_K, BAND_Q), F32),
                   jax.ShapeDtypeStruct((N_HEADS, CHUNK, BAND_K_SAMPLE), F32)],
        name="band_bias",
    )(bias_row)


def _band_body(qx_ref, kx_ref, vt_ref, bias_ref, g_ref, o_ref, kpad, vtpad, s_scr, p_scr, ot_scr, *, s_len):
    step = pl.program_id(1)

    @pl.when(step == 0)
    def _():
        lane = lax.broadcasted_iota(jnp.int32, (LEFT, 2 * LANES), 1)
        flags = (lane == _extra_lane(0)) | (lane == LANES + _extra_lane(1))
        pad_pair = jnp.where(flags, NEG_INF, 0.0).astype(BF16)
        for hp in range(N_PAIRS):
            kpad[0:LEFT, 2 * hp * LANES:2 * (hp + 1) * LANES] = pad_pair
        vtpad[:, 0:LEFT] = jnp.zeros((W_GROUP, LEFT), BF16)
        kpad[LEFT:LEFT + s_len, :] = kx_ref[...]
        vtpad[:, LEFT:LEFT + s_len] = vt_ref[...]

    even_rows = _pair_rows()
    start = pl.multiple_of(step * BAND_Q, BAND_Q)
    for h in range(N_HEADS):
        head = slice(h * LANES, (h + 1) * LANES)
        s_scr[h] = lax.dot_general(kpad[pl.ds(start, BAND_K), head], qx_ref[:, head], _NT,
                                   preferred_element_type=F32)
    for h in range(N_HEADS):
        st = s_scr[h] + bias_ref[h]
        p_scr[h] = jnp.exp2(st - jnp.max(st, axis=0, keepdims=True)).astype(BF16)
    for hp in range(N_PAIRS):
        pair = slice(hp * LANES, (hp + 1) * LANES)
        vtwin = vtpad[pair, pl.ds(start, BAND_K)]
        accs = [jnp.dot(_head_vt(vtwin, even_rows, parity), p_scr[2 * hp + parity], preferred_element_type=F32)
                for parity in range(2)]
        ot_scr[pair, :] = _head_out_t(accs[0], accs[1], even_rows)
    o_ref[...] = _rms(ot_scr[...].T, g_ref[...]).astype(BF16)


def _band_prompt(qxb, kxb, vbt, bias_t, g_out):
    b, s, wx = qxb.shape
    w = vbt.shape[1]
    return pl.pallas_call(
        functools.partial(_band_body, s_len=s),
        grid=(b, s // BAND_Q),
        in_specs=[pl.BlockSpec((None, BAND_Q, wx), lambda i, j: (i, j, 0)),
                  pl.BlockSpec((None, s, wx), lambda i, j: (i, 0, 0)),
                  pl.BlockSpec((None, w, s), lambda i, j: (i, 0, 0)),
                  pl.BlockSpec(bias_t.shape, lambda i, j: (0, 0, 0)),
                  pl.BlockSpec((1, w), lambda i, j: (0, 0))],
        out_specs=pl.BlockSpec((None, BAND_Q, w), lambda i, j: (i, j, 0)),
        out_shape=jax.ShapeDtypeStruct((b, s, w), BF16),
        scratch_shapes=[pltpu.VMEM((LEFT + s, wx), BF16), pltpu.VMEM((w, LEFT + s), BF16),
                        pltpu.VMEM((N_HEADS, BAND_K, BAND_Q), F32), pltpu.VMEM((N_HEADS, BAND_K, BAND_Q), BF16),
                        pltpu.VMEM((w, BAND_Q), F32)],
        compiler_params=pltpu.CompilerParams(
            dimension_semantics=("parallel", "arbitrary"), vmem_limit_bytes=VMEM_LIMIT),
        name="band_prompt",
    )(qxb, kxb, vbt, bias_t, g_out)


def _row_to_col(row):
    n = row.shape[-1]
    r = lax.broadcasted_iota(jnp.int32, (n, n), 0)
    c = lax.broadcasted_iota(jnp.int32, (n, n), 1)
    return jnp.sum(jnp.where(r == c, jnp.broadcast_to(row, (n, n)), 0.0), axis=-1, keepdims=True)


def _fox_sample_body(q_ref, kn_ref, vn_ref, lft_ref, kc_ref, vc_ref, clft_ref, g_ref, o_ref,
                     cct_scr, cn_scr, m_scr, acc_scr, o_scr, s_scr, p_scr, alpha_scr, *, t_new, pt):
    p_idx = pl.program_id(1)
    n_p = pl.num_programs(1)
    even = _pair_masks()

    @pl.when(p_idx == 0)
    def _():
        cct = _lane_cumsum(clft_ref[...])
        cct_scr[...] = cct
        cn_scr[...] = _lane_cumsum(lft_ref[...]) + cct[:, cct.shape[1] - 1:]
        m_scr[...] = jnp.full(m_scr.shape, NEG_INF, F32)
        acc_scr[...] = jnp.zeros(acc_scr.shape, F32)

    def update(h, s, v):
        m = m_scr[h]
        m_new = jnp.maximum(m, jnp.max(s, axis=-1, keepdims=True))
        p = jnp.exp(s - m_new).astype(BF16)
        acc_scr[h] = acc_scr[h] * jnp.exp(m - m_new) + jnp.dot(p, v, preferred_element_type=F32)
        m_scr[h] = m_new

    start = pl.multiple_of(p_idx * pt, pt)
    pair = lambda h: slice((h // 2) * LANES, (h // 2 + 1) * LANES)
    for h in range(N_HEADS):
        s_scr[h] = lax.dot_general(_head_q(q_ref[:, pair(h)], even, h % 2), kc_ref[:, pair(h)].astype(BF16),
                                   _NT, preferred_element_type=F32)
    for h in range(N_HEADS):
        cq = _row_to_col(cn_scr[h:h + 1, :])
        s = s_scr[h] + cq - cct_scr[h:h + 1, pl.ds(start, pt)]
        m = m_scr[h]
        m_new = jnp.maximum(m, jnp.max(s, axis=-1, keepdims=True))
        p_scr[h] = jnp.exp(s - m_new).astype(BF16)
        alpha_scr[h] = jnp.exp(m - m_new)
        m_scr[h] = m_new
    for h in range(N_HEADS):
        v = _head_v(vc_ref[:, pair(h)].astype(BF16), even, h % 2)
        acc_scr[h] = acc_scr[h] * alpha_scr[h] + jnp.dot(p_scr[h], v, preferred_element_type=F32)

    @pl.when(p_idx == n_p - 1)
    def _():
        row = lax.broadcasted_iota(jnp.int32, (t_new, t_new), 0)
        col = lax.broadcasted_iota(jnp.int32, (t_new, t_new), 1)
        for hp in range(N_PAIRS):
            lanes = slice(hp * LANES, (hp + 1) * LANES)
            q128 = q_ref[:, lanes]
            kn = kn_ref[:, lanes]
            vn = vn_ref[:, lanes]
            for parity in range(2):
                h = 2 * hp + parity
                cn_row = cn_scr[h:h + 1, :]
                s = lax.dot_general(_head_q(q128, even, parity), kn, _NT, preferred_element_type=F32)
                s = jnp.where(col <= row, s + _row_to_col(cn_row) - cn_row, NEG_INF)
                update(h, s, _head_v(vn, even, parity))
            o_scr[:, lanes] = _head_out(acc_scr[2 * hp], acc_scr[2 * hp + 1], even)
        o_ref[...] = _rms(o_scr[...], g_ref[...]).astype(BF16)


def _fox_sample(q, kn, vn, lft, kc, vc, clft, g_out, *, pt):
    b, t, w = q.shape
    p_len = kc.shape[1]
    new = lambda: pl.BlockSpec((None, t, w), lambda i, j: (i, 0, 0))
    cache = lambda: pl.BlockSpec((None, pt, w), lambda i, j: (i, j, 0))
    return pl.pallas_call(
        functools.partial(_fox_sample_body, t_new=t, pt=pt),
        grid=(b, p_len // pt),
        in_specs=[new(), new(), new(),
                  pl.BlockSpec((None, N_HEADS, t), lambda i, j: (i, 0, 0)),
                  cache(), cache(),
                  pl.BlockSpec((None, N_HEADS, p_len), lambda i, j: (i, 0, 0)),
                  pl.BlockSpec((1, w), lambda i, j: (0, 0))],
        out_specs=new(),
        out_shape=jax.ShapeDtypeStruct((b, t, w), BF16),
        scratch_shapes=[pltpu.VMEM((N_HEADS, p_len), F32), pltpu.VMEM((N_HEADS, t), F32),
                        pltpu.VMEM((N_HEADS, t, 1), F32), pltpu.VMEM((N_HEADS, t, LANES), F32),
                        pltpu.VMEM((t, w), F32), pltpu.VMEM((N_HEADS, t, pt), F32),
                        pltpu.VMEM((N_HEADS, t, pt), BF16), pltpu.VMEM((N_HEADS, t, 1), F32)],
        compiler_params=pltpu.CompilerParams(
            dimension_semantics=("parallel", "arbitrary"), vmem_limit_bytes=VMEM_LIMIT),
        name="fox_sample",
    )(q, kn, vn, lft, kc, vc, clft, g_out)


def _band_sample_body(q_ref, kn_ref, vn_ref, knf_ref, vnf_ref, kc_ref, vc_ref, bias_ref, g_ref,
                      o_ref, nk_ref, nv_ref, kcat, vcat, o_scr, *, t_new, bp):
    kcat[0:bp, :] = kc_ref[...].astype(BF16)
    vcat[0:bp, :] = vc_ref[...].astype(BF16)
    kcat[bp:bp + t_new, :] = kn_ref[...]
    vcat[bp:bp + t_new, :] = vn_ref[...]
    nk_ref[0:bp - t_new, :] = kc_ref[t_new:bp, :]
    nv_ref[0:bp - t_new, :] = vc_ref[t_new:bp, :]
    nk_ref[bp - t_new:bp, :] = knf_ref[...]
    nv_ref[bp - t_new:bp, :] = vnf_ref[...]
    even = _pair_masks()
    for hp in range(N_PAIRS):
        lanes = slice(hp * LANES, (hp + 1) * LANES)
        q128 = q_ref[:, lanes]
        k = kcat[:, lanes]
        v = vcat[:, lanes]
        accs = []
        for parity in range(2):
            h = 2 * hp + parity
            s = lax.dot_general(_head_q(q128, even, parity), k, _NT, preferred_element_type=F32)
            s = s + bias_ref[h, 0:t_new, 0:bp + t_new]
            p = jnp.exp(s - jnp.max(s, axis=-1, keepdims=True)).astype(BF16)
            accs.append(jnp.dot(p, _head_v(v, even, parity), preferred_element_type=F32))
        o_scr[:, lanes] = _head_out(accs[0], accs[1], even)
    o_ref[...] = _rms(o_scr[...], g_ref[...]).astype(BF16)


def _band_sample(q, kn, vn, knf, vnf, kc, vc, bias, g_out):
    b, t, w = q.shape
    bp = kc.shape[1]
    assert t == CHUNK and bp == LEFT
    new = lambda: pl.BlockSpec((None, t, w), lambda i: (i, 0, 0))
    buf = lambda: pl.BlockSpec((None, bp, w), lambda i: (i, 0, 0))
    return pl.pallas_call(
        functools.partial(_band_sample_body, t_new=t, bp=bp),
        grid=(b,),
        in_specs=[new(), new(), new(), new(), new(), buf(), buf(),
                  pl.BlockSpec(bias.shape, lambda i: (0, 0, 0)),
                  pl.BlockSpec((1, w), lambda i: (0, 0))],
        out_specs=[new(), buf(), buf()],
        out_shape=[jax.ShapeDtypeStruct((b, t, w), BF16), jax.ShapeDtypeStruct((b, bp, w), F32),
                   jax.ShapeDtypeStruct((b, bp, w), F32)],
        scratch_shapes=[pltpu.VMEM((bp + t, w), BF16), pltpu.VMEM((bp + t, w), BF16), pltpu.VMEM((t, w), F32)],
        compiler_params=pltpu.CompilerParams(dimension_semantics=("parallel",), vmem_limit_bytes=VMEM_LIMIT),
        name="band_sample",
    )(q, kn, vn, knf, vnf, kc, vc, bias, g_out)


N_HEADS_MEM = 4
HEAD_DIM_MEM = 128
W_MEM = N_HEADS_MEM * HEAD_DIM_MEM
MEM_SCALE = HEAD_DIM_MEM ** -0.5


def _memkv_body(m_ref, g_ref, w_ref, kf_ref, vf_ref, k_ref, v_ref):
    h = _rms(m_ref[...], g_ref[...]).astype(BF16)
    z = jnp.dot(h, w_ref[...], preferred_element_type=F32)
    kf_ref[...] = z[:, :W_MEM]
    vf_ref[...] = z[:, W_MEM:]
    k_ref[...] = z[:, :W_MEM].astype(BF16)
    v_ref[...] = z[:, W_MEM:].astype(BF16)


def _mem_kv(mem, g_mem, w_ckv):
    b, n, d = mem.shape
    blk = lambda: pl.BlockSpec((None, n, W_MEM), lambda i: (i, 0, 0))
    return pl.pallas_call(
        _memkv_body,
        grid=(b,),
        in_specs=[pl.BlockSpec((None, n, d), lambda i: (i, 0, 0)),
                  pl.BlockSpec((1, d), lambda i: (0, 0)),
                  pl.BlockSpec(w_ckv.shape, lambda i: (0, 0))],
        out_specs=[blk(), blk(), blk(), blk()],
        out_shape=[jax.ShapeDtypeStruct((b, n, W_MEM), F32)] * 2 + [jax.ShapeDtypeStruct((b, n, W_MEM), BF16)] * 2,
        compiler_params=pltpu.CompilerParams(dimension_semantics=("parallel",), vmem_limit_bytes=VMEM_LIMIT),
        name="mem_kv",
    )(mem, g_mem, w_ckv)


N_GROUPS = 4
EXPERTS_PER_GROUP = 8
N_EXPERTS = N_GROUPS * EXPERTS_PER_GROUP
ROUTE_L2 = N_GROUPS
ROUTE_ROWS = 8
POST_CHAIN = 512
R_EID0, R_EID1, R_RANK0, R_RANK1, R_GATE0, R_GATE1 = range(6)


def _lane_max(x, mask):
    return jnp.max(jnp.where(mask, x, -jnp.inf), axis=-1, keepdims=True)


def _first_lane(mask, lane):
    return jnp.min(jnp.where(mask, lane, LANES), axis=-1, keepdims=True)


def _route(logits, lane):
    is_l1 = lane < N_GROUPS
    m1 = _lane_max(logits, is_l1)
    grp = _first_lane(is_l1 & (logits == m1), lane)
    wg = 1.0 / jnp.sum(jnp.where(is_l1, jnp.exp(logits - m1), 0.0), axis=-1, keepdims=True)
    lo = ROUTE_L2 + grp * EXPERTS_PER_GROUP
    in_grp = (lane >= lo) & (lane < lo + EXPERTS_PER_GROUP)
    v0 = _lane_max(logits, in_grp)
    i0 = _first_lane(in_grp & (logits == v0), lane)
    rest = in_grp & (lane != i0)
    v1 = _lane_max(logits, rest)
    i1 = _first_lane(rest & (logits == v1), lane)
    e1 = jnp.exp(v1 - v0)
    den = 1.0 / (1.0 + e1)
    return i0, i1, wg * den, wg * e1 * den


def _post_body(x_ref, a_ref, b_ref, mk_ref, mv_ref, woa_ref, wob_ref, gc_ref, wcq_ref, wco_ref,
               gf_ref, wrt_ref, brt_ref,
               x2_ref, h3_ref, route_ref, routet_ref, cnt_ref, o_scr, *, tm, nsub):
    seq = tm // nsub
    count = jnp.zeros((1, LANES), F32)
    for rows in _chains(tm, POST_CHAIN):
        n = rows.size
        x1 = (x_ref[rows, :] + jnp.dot(a_ref[rows, :], woa_ref[...], preferred_element_type=F32)
              + jnp.dot(b_ref[rows, :], wob_ref[...], preferred_element_type=F32))
        h2 = _rms(x1, gc_ref[...]).astype(BF16)
        qc = (jnp.dot(h2, wcq_ref[...], preferred_element_type=F32) * MEM_SCALE).astype(BF16)
        span = min(seq, n)
        for part in range(n // span):
            sub = (rows.start + part * span) // seq
            rs = slice(part * span, (part + 1) * span)
            orow = pl.ds(rows.start + part * span, span)
            for hm in range(N_HEADS_MEM):
                lanes = slice(hm * HEAD_DIM_MEM, (hm + 1) * HEAD_DIM_MEM)
                s = lax.dot_general(qc[rs, lanes], mk_ref[sub, :, lanes], _NT, preferred_element_type=F32)
                p = jnp.exp(s - jnp.max(s, axis=-1, keepdims=True))
                inv = 1.0 / jnp.sum(p, axis=-1, keepdims=True)
                o_scr[orow, lanes] = jnp.dot(p.astype(BF16), mv_ref[sub, :, lanes], preferred_element_type=F32) * inv
        x2 = x1 + jnp.dot(o_scr[rows, :].astype(BF16), wco_ref[...], preferred_element_type=F32)
        x2_ref[rows, :] = x2
        h3 = _rms(x2, gf_ref[...]).astype(BF16)
        h3_ref[rows, :] = h3

        logits = jnp.dot(h3, wrt_ref[...], preferred_element_type=F32) + brt_ref[...]
        lane = lax.broadcasted_iota(jnp.int32, (n, LANES), 1)
        i0, i1, g0, g1 = _route(logits, lane)
        e0 = i0 - ROUTE_L2
        e1 = i1 - ROUTE_L2
        hit0 = lane == e0
        hit1 = lane == e1
        onehot = jnp.where(hit0 | hit1, 1.0, 0.0)
        row = lax.broadcasted_iota(jnp.int32, (n, n), 0)
        col = lax.broadcasted_iota(jnp.int32, (n, n), 1)
        before = jnp.where(col < row, 1.0, 0.0).astype(BF16)
        seen = jnp.dot(before, onehot.astype(BF16), preferred_element_type=F32) + count
        rank0 = jnp.sum(jnp.where(hit0, seen, 0.0), axis=-1, keepdims=True)
        rank1 = jnp.sum(jnp.where(hit1, seen, 0.0), axis=-1, keepdims=True)
        count = count + jnp.sum(onehot, axis=0, keepdims=True)

        rec = jnp.zeros((n, LANES), F32)
        for idx, val in ((R_EID0, e0.astype(F32)), (R_EID1, e1.astype(F32)), (R_RANK0, rank0),
                         (R_RANK1, rank1), (R_GATE0, g0), (R_GATE1, g1)):
            rec = jnp.where(lane == idx, val, rec)
        route_ref[rows, :] = rec[:, :ROUTE_ROWS]
        routet_ref[:, rows] = rec.T[:ROUTE_ROWS, :]
    cnt_ref[...] = count


def _post_block(x, a_n, b_n, mk, mv, weights, *, tm):
    b, s, d = x.shape
    if s >= tm:
        nsub, grid = 1, (b, s // tm)
        tok = lambda i, j: (i, j, 0)
        flat = lambda i, j: i * (s // tm) + j
    else:
        nsub = tm // s
        assert b % nsub == 0
        x, a_n, b_n = (t.reshape(b // nsub, tm, t.shape[-1]) for t in (x, a_n, b_n))
        grid = (b // nsub, 1)
        tok = lambda i, j: (i, 0, 0)
        flat = lambda i, j: i
    mem = lambda i, j: (i, 0, 0)
    const = lambda arr: pl.BlockSpec(arr.shape, lambda i, j: (0,) * arr.ndim)
    in_specs = [pl.BlockSpec((None, tm, d), tok),
                pl.BlockSpec((None, tm, W_GROUP), tok), pl.BlockSpec((None, tm, W_GROUP), tok),
                pl.BlockSpec((nsub, mk.shape[1], W_MEM), mem), pl.BlockSpec((nsub, mv.shape[1], W_MEM), mem)]
    in_specs += [const(w) for w in weights]
    n = b * s
    out_shape = [jax.ShapeDtypeStruct((n, d), F32), jax.ShapeDtypeStruct((n, d), BF16),
                 jax.ShapeDtypeStruct((n, ROUTE_ROWS), F32), jax.ShapeDtypeStruct((ROUTE_ROWS, n), F32),
                 jax.ShapeDtypeStruct((n // tm, 1, LANES), F32)]
    out_specs = [pl.BlockSpec((tm, d), lambda i, j: (flat(i, j), 0)),
                 pl.BlockSpec((tm, d), lambda i, j: (flat(i, j), 0)),
                 pl.BlockSpec((tm, ROUTE_ROWS), lambda i, j: (flat(i, j), 0)),
                 pl.BlockSpec((ROUTE_ROWS, tm), lambda i, j: (0, flat(i, j))),
                 pl.BlockSpec((None, 1, LANES), lambda i, j: (flat(i, j), 0, 0))]
    return pl.pallas_call(
        functools.partial(_post_body, tm=tm, nsub=nsub),
        grid=grid,
        in_specs=in_specs,
        out_specs=out_specs,
        out_shape=out_shape,
        scratch_shapes=[pltpu.VMEM((tm, W_MEM), F32)],
        compiler_params=pltpu.CompilerParams(
            dimension_semantics=("parallel", "parallel"), vmem_limit_bytes=VMEM_LIMIT),
        name="post_block",
    )(x, a_n, b_n, mk, mv, *weights)


def _prep_post(w_out, g_cross, w_cq, w_co, g_ffn, w_r1, b_r1, w_r2, b_r2):
    pad = LANES - N_GROUPS - N_EXPERTS
    w_rt = jnp.pad(jnp.concatenate([w_r1, w_r2], axis=1), ((0, 0), (0, pad))).astype(BF16)
    b_rt = jnp.pad(jnp.concatenate([b_r1, b_r2]).reshape(1, -1), ((0, 0), (0, pad))).astype(F32)
    return [w_out[:W_GROUP].astype(BF16), w_out[W_GROUP:].astype(BF16), g_cross.reshape(1, -1),
            w_cq.astype(BF16), w_co.astype(BF16), g_ffn.reshape(1, -1), w_rt, b_rt]


D_EXPERT = 512
TOP_K = 2
ROW_TILE = 512
MXU_DIM = 256
RUN_ALIGN = 16
PLAN_TILES = LANES
TILE_TABLE = 2 * LANES


def _local_rows(tm):
    return -(-(TOP_K * tm + N_EXPERTS * (RUN_ALIGN - 1)) // MXU_DIM) * MXU_DIM


def _n_row_tiles(n_tokens, tm):
    rows = n_tokens * TOP_K + (n_tokens // tm) * N_EXPERTS * (RUN_ALIGN - 1) + N_EXPERTS * (ROW_TILE - 1)
    return rows // ROW_TILE


N_CHUNK_LANE = LANES - 1


def _plan_body(cnt_ref, lstart_ref, chunk_ref, offs_ref, te_ref):
    cnt = cnt_ref[...].astype(jnp.int32)
    n16 = ((cnt + (RUN_ALIGN - 1)) & ~(RUN_ALIGN - 1)).astype(F32)
    lend = _lane_cumsum(n16)
    lstart = lend - n16
    earlier = (_lane_cumsum(n16.T) - n16.T).T
    total = jnp.sum(n16, axis=0, keepdims=True).astype(jnp.int32)
    seg = jnp.broadcast_to((total + (ROW_TILE - 1)) & ~(ROW_TILE - 1), (8, LANES)).astype(F32)
    ends = _lane_cumsum(seg)
    offs = ends - seg
    shift = earlier + offs[0:1, :] - lstart
    lstart_ref[...] = lstart.astype(jnp.int32)
    offs_ref[...] = offs[0:1, :].astype(jnp.int32)

    lane = lax.broadcasted_iota(jnp.int32, (PLAN_TILES, LANES), 1)
    local_row = (lane * RUN_ALIGN).astype(F32)
    owner = jnp.zeros((PLAN_TILES, LANES), jnp.int32)
    for e in range(N_EXPERTS):
        owner = owner + jnp.where(lend[:, e:e + 1] <= local_row, 1, 0)
    glob = local_row
    for e in range(N_EXPERTS):
        glob = glob + jnp.where(owner == e, shift[:, e:e + 1], 0.0)
    n_chunks = lend[:, N_EXPERTS - 1:N_EXPERTS] * (1.0 / RUN_ALIGN)
    chunk_ref[...] = jnp.where(lane == N_CHUNK_LANE, n_chunks, glob).astype(jnp.int32)

    tile_start = (lax.broadcasted_iota(jnp.int32, te_ref.shape, 1) * ROW_TILE).astype(F32)
    te = jnp.zeros(te_ref.shape, jnp.int32)
    for e in range(N_EXPERTS):
        end_e = jnp.sum(jnp.where(lane[0:1, :] == e, ends[0:1, :], 0.0), axis=-1, keepdims=True)
        te = te + jnp.where(end_e <= tile_start, 1, 0)
    te_ref[...] = jnp.minimum(te, N_EXPERTS - 1)


def _plan(counts, tm):
    nt = counts.shape[0]
    assert nt <= PLAN_TILES and _local_rows(tm) // RUN_ALIGN <= N_CHUNK_LANE
    cnt = jnp.pad(counts.reshape(nt, LANES), ((0, PLAN_TILES - nt), (0, 0)))
    grid_i32 = jax.ShapeDtypeStruct((PLAN_TILES, LANES), jnp.int32)
    lstart, chunks, offs, te = pl.pallas_call(
        _plan_body,
        out_shape=[grid_i32, grid_i32, jax.ShapeDtypeStruct((1, LANES), jnp.int32),
                   jax.ShapeDtypeStruct((1, TILE_TABLE), jnp.int32)],
        name="moe_plan",
    )(cnt)
    per_tile = lambda t: t[:nt].reshape(nt, 1, LANES)
    return per_tile(lstart), per_tile(chunks), offs.reshape(LANES), te.reshape(TILE_TABLE)


def _for_each_chunk(chunk_ref, fn):
    def body(c, carry):
        fn(pl.multiple_of(c * RUN_ALIGN, RUN_ALIGN), pl.multiple_of(chunk_ref[0, c], RUN_ALIGN))
        return carry

    lax.fori_loop(0, chunk_ref[0, N_CHUNK_LANE], body, None)


def _local_positions_row(rt_ref, lstart_ref):
    pos = []
    for r_eid, r_rank in ((R_EID0, R_RANK0), (R_EID1, R_RANK1)):
        eid = rt_ref[r_eid:r_eid + 1, :].astype(jnp.int32)
        p = rt_ref[r_rank:r_rank + 1, :].astype(jnp.int32)
        for e in range(N_EXPERTS):
            p = p + jnp.where(eid == e, lstart_ref[0, e], 0)
        pos.append(p)
    return pos


def _dispatch_body(offs_ref, lstart_ref, chunk_ref, pchunk_ref,
                   hp_ref, hs_ref, rt_ref, xs_ref, loc, zeros, sems, zsem, *, tm, n_prompt_tiles):
    i = pl.program_id(0)
    n_tiles = xs_ref.shape[0] // ROW_TILE
    half = lax.rem(i, 2)

    @pl.when(i == 0)
    def _():
        zeros[...] = jnp.zeros(zeros.shape, zeros.dtype)
        zero_tile = lambda row: pltpu.make_async_copy(
            zeros, xs_ref.at[pl.ds(pl.multiple_of(row, ROW_TILE), ROW_TILE)], zsem)
        n_used = offs_ref[N_EXPERTS] // ROW_TILE

        def tail(j, carry, op):
            op(zero_tile(j * ROW_TILE))
            return carry

        for op in (lambda c: c.start(), lambda c: c.wait()):
            for e in range(N_EXPERTS):
                @pl.when(offs_ref[e + 1] > offs_ref[e])
                def _():
                    op(zero_tile(offs_ref[e + 1] - ROW_TILE))
            lax.fori_loop(n_used, n_tiles, functools.partial(tail, op=op), None)

    pos0, pos1 = _local_positions_row(rt_ref, lstart_ref)
    used = chunk_ref[0, N_CHUNK_LANE] * RUN_ALIGN
    body_rows = loc.shape[1] - MXU_DIM

    def sort_rows(h_ref, lo, n):
        slot = lo + lax.broadcasted_iota(jnp.int32, (n, tm), 0)
        perm = jnp.where(slot == pos0, 1.0, jnp.where(slot == pos1, 1.0, 0.0)).astype(BF16)
        loc[half, lo:lo + n, :] = jnp.dot(perm, h_ref[...], preferred_element_type=F32).astype(BF16)

    for h_ref, mine in ((hp_ref, i < n_prompt_tiles), (hs_ref, i >= n_prompt_tiles)):
        @pl.when(mine)
        def _():
            sort_rows(h_ref, 0, body_rows)

        @pl.when(mine & (used > body_rows))
        def _():
            sort_rows(h_ref, body_rows, MXU_DIM)

    def chunk(buf, lo, go):
        return pltpu.make_async_copy(loc.at[buf, pl.ds(lo, RUN_ALIGN)], xs_ref.at[pl.ds(go, RUN_ALIGN)],
                                     sems.at[buf])

    @pl.when(i > 0)
    def _():
        _for_each_chunk(pchunk_ref, lambda lo, go: chunk(1 - half, lo, go).wait())

    _for_each_chunk(chunk_ref, lambda lo, go: chunk(half, lo, go).start())

    @pl.when(i == pl.num_programs(0) - 1)
    def _():
        _for_each_chunk(chunk_ref, lambda lo, go: chunk(half, lo, go).wait())


def _dispatch(h3_prompt, h3_sample, route_t, plan, *, tm):
    lstart, chunks, offs, _ = plan
    n_p, n_s, d = h3_prompt.shape[0], h3_sample.shape[0], h3_prompt.shape[-1]
    assert n_p % tm == 0 and n_s % tm == 0
    n = n_p + n_s
    nt = n // tm
    npt = n_p // tm
    n_rows = _n_row_tiles(n, tm) * ROW_TILE
    smem_tile = lambda: pl.BlockSpec((None, 1, LANES), lambda i, offs: (i, 0, 0), memory_space=pltpu.SMEM)
    smem_prev = lambda: pl.BlockSpec((None, 1, LANES), lambda i, offs: (jnp.maximum(i - 1, 0), 0, 0),
                                     memory_space=pltpu.SMEM)
    return pl.pallas_call(
        functools.partial(_dispatch_body, tm=tm, n_prompt_tiles=npt),
        grid_spec=pltpu.PrefetchScalarGridSpec(
            num_scalar_prefetch=1,
            grid=(nt,),
            in_specs=[smem_tile(), smem_tile(), smem_prev(),
                      pl.BlockSpec((tm, d), lambda i, offs: (jnp.minimum(i, npt - 1), 0)),
                      pl.BlockSpec((tm, d), lambda i, offs: (jnp.maximum(i - npt, 0), 0)),
                      pl.BlockSpec((ROUTE_ROWS, tm), lambda i, offs: (0, i))],
            out_specs=pl.BlockSpec(memory_space=pl.ANY),
            scratch_shapes=[pltpu.VMEM((2, _local_rows(tm), d), BF16), pltpu.VMEM((ROW_TILE, d), BF16),
                            pltpu.SemaphoreType.DMA((2,)), pltpu.SemaphoreType.DMA(())]),
        out_shape=jax.ShapeDtypeStruct((n_rows, d), BF16),
        compiler_params=pltpu.CompilerParams(dimension_semantics=("arbitrary",), vmem_limit_bytes=VMEM_LIMIT),
        name="moe_dispatch",
    )(offs, lstart, chunks, chunks, h3_prompt, h3_sample, route_t)


def _experts_body(te_ref, offs_ref, xs_ref, wg_ref, wu_ref, wd_ref, ys_ref,
                  wg_buf, wu_buf, wd_buf, wg_bf, wu_bf, wd_bf, turn_ref, sems):
    i = pl.program_id(0)
    n_used = offs_ref[N_EXPERTS] // ROW_TILE

    def fetch(expert, half):
        return [pltpu.make_async_copy(src.at[expert], dst.at[half], sems.at[half, k])
                for k, (src, dst) in enumerate(((wg_ref, wg_buf), (wu_ref, wu_buf), (wd_ref, wd_buf)))]

    @pl.when(i == 0)
    def _():
        turn_ref[0] = 0
        for copy in fetch(te_ref[0], 0):
            copy.start()

    @pl.when(i < n_used)
    def _():
        expert = te_ref[i]

        @pl.when((i == 0) | (expert != te_ref[jnp.maximum(i - 1, 0)]))
        def _():
            half = lax.rem(turn_ref[0], 2)
            turn_ref[0] = turn_ref[0] + 1
            for copy in fetch(expert, half):
                copy.wait()
            wg_bf[...] = wg_buf[half].astype(BF16)
            wu_bf[...] = wu_buf[half].astype(BF16)
            wd_bf[...] = wd_buf[half].astype(BF16)
            following = offs_ref[expert + 1] // ROW_TILE

            @pl.when(following < n_used)
            def _():
                for copy in fetch(te_ref[following], 1 - half):
                    copy.start()

        dot = functools.partial(jnp.dot, preferred_element_type=F32)
        for rows in _chains(ROW_TILE):
            x = xs_ref[rows, :]
            gate = dot(x, wg_bf[...])
            up = dot(x, wu_bf[...])
            act = (gate * jax.nn.sigmoid(gate) * up).astype(BF16)
            ys_ref[rows, :] = dot(act, wd_bf[...]).astype(BF16)

    @pl.when(i >= n_used)
    def _():
        ys_ref[...] = jnp.zeros(ys_ref.shape, ys_ref.dtype)


def _experts(xs, te, offs, w_gate, w_up, w_down):
    n_rows, d = xs.shape
    last = lambda i, te, offs: jnp.minimum(i, offs[N_EXPERTS] // ROW_TILE - 1)
    hbm = pl.BlockSpec(memory_space=pl.ANY)
    return pl.pallas_call(
        _experts_body,
        grid_spec=pltpu.PrefetchScalarGridSpec(
            num_scalar_prefetch=2,
            grid=(n_rows // ROW_TILE,),
            in_specs=[pl.BlockSpec((ROW_TILE, d), lambda i, te, offs: (last(i, te, offs), 0)), hbm, hbm, hbm],
            out_specs=pl.BlockSpec((ROW_TILE, d), lambda i, te, offs: (i, 0)),
            scratch_shapes=[pltpu.VMEM((2, d, D_EXPERT), F32), pltpu.VMEM((2, d, D_EXPERT), F32),
                            pltpu.VMEM((2, D_EXPERT, d), F32),
                            pltpu.VMEM((d, D_EXPERT), BF16), pltpu.VMEM((d, D_EXPERT), BF16),
                            pltpu.VMEM((D_EXPERT, d), BF16),
                            pltpu.SMEM((1,), jnp.int32), pltpu.SemaphoreType.DMA((2, 3))]),
        out_shape=jax.ShapeDtypeStruct((n_rows, d), BF16),
        compiler_params=pltpu.CompilerParams(dimension_semantics=("arbitrary",), vmem_limit_bytes=VMEM_LIMIT),
        name="moe_experts",
    )(te, offs, xs, w_gate, w_up, w_down)


def _combine_body(chunk_ref, nchunk_ref, x2p_ref, x2s_ref, route_ref, lsv_ref, ys_ref, g_ref,
                  yp_ref, ysm_ref, loc, sems, *, tm, n_prompt_tiles):
    i = pl.program_id(0)
    half = lax.rem(i, 2)

    def chunk(buf, lo, go):
        return pltpu.make_async_copy(ys_ref.at[pl.ds(go, RUN_ALIGN)], loc.at[buf, pl.ds(lo, RUN_ALIGN)],
                                     sems.at[buf])

    @pl.when(i == 0)
    def _():
        loc[...] = jnp.zeros(loc.shape, loc.dtype)
        _for_each_chunk(chunk_ref, lambda lo, go: chunk(0, lo, go).start())

    @pl.when(i + 1 < pl.num_programs(0))
    def _():
        _for_each_chunk(nchunk_ref, lambda lo, go: chunk(1 - half, lo, go).start())

    _for_each_chunk(chunk_ref, lambda lo, go: chunk(half, lo, go).wait())

    lane = lax.broadcasted_iota(jnp.int32, (tm, LANES), 1)
    picks = []
    for r_eid, r_rank, r_gate in ((R_EID0, R_RANK0, R_GATE0), (R_EID1, R_RANK1, R_GATE1)):
        eid = route_ref[:, r_eid:r_eid + 1].astype(jnp.int32)
        start = jnp.sum(jnp.where(lane == eid, lsv_ref[...], 0), axis=-1, keepdims=True)
        picks.append((route_ref[:, r_rank:r_rank + 1].astype(jnp.int32) + start, route_ref[:, r_gate:r_gate + 1]))

    def gather_rows(lo, n):
        slot = lo + lax.broadcasted_iota(jnp.int32, (tm, n), 1)
        weights = jnp.zeros(slot.shape, F32)
        for pos, gate in picks:
            weights = jnp.where(slot == pos, gate, weights)
        return jnp.dot(weights.astype(BF16), loc[half, lo:lo + n, :], preferred_element_type=F32)

    used = chunk_ref[0, N_CHUNK_LANE] * RUN_ALIGN
    body_rows = loc.shape[1] - MXU_DIM
    moe = gather_rows(0, body_rows) + lax.cond(
        used > body_rows, lambda: gather_rows(body_rows, MXU_DIM), lambda: jnp.zeros((tm, loc.shape[2]), F32))

    @pl.when(i < n_prompt_tiles)
    def _():
        yp_ref[...] = _rms(x2p_ref[...] + moe, g_ref[...])

    @pl.when(i >= n_prompt_tiles)
    def _():
        ysm_ref[...] = _rms(x2s_ref[...] + moe, g_ref[...])


def _combine(x2_prompt, x2_sample, route, plan, ys, g_final, *, tm):
    lstart, chunks, _, _ = plan
    (n_p, d), n_s = x2_prompt.shape, x2_sample.shape[0]
    assert n_p % tm == 0 and n_s % tm == 0
    npt = n_p // tm
    nt = npt + n_s // tm
    smem_tile = lambda: pl.BlockSpec((None, 1, LANES), lambda i: (i, 0, 0), memory_space=pltpu.SMEM)
    smem_next = lambda: pl.BlockSpec((None, 1, LANES), lambda i: (jnp.minimum(i + 1, nt - 1), 0, 0),
                                     memory_space=pltpu.SMEM)
    prompt_tile = lambda: pl.BlockSpec((tm, d), lambda i: (jnp.minimum(i, npt - 1), 0))
    sample_tile = lambda: pl.BlockSpec((tm, d), lambda i: (jnp.maximum(i - npt, 0), 0))
    return pl.pallas_call(
        functools.partial(_combine_body, tm=tm, n_prompt_tiles=npt),
        grid=(nt,),
        in_specs=[smem_tile(), smem_next(), prompt_tile(), sample_tile(),
                  pl.BlockSpec((tm, ROUTE_ROWS), lambda i: (i, 0)),
                  pl.BlockSpec((None, 1, LANES), lambda i: (i, 0, 0)),
                  pl.BlockSpec(memory_space=pl.ANY),
                  pl.BlockSpec((1, d), lambda i: (0, 0))],
        out_specs=[prompt_tile(), sample_tile()],
        out_shape=[jax.ShapeDtypeStruct((n_p, d), F32), jax.ShapeDtypeStruct((n_s, d), F32)],
        scratch_shapes=[pltpu.VMEM((2, _local_rows(tm), d), BF16), pltpu.SemaphoreType.DMA((2,))],
        compiler_params=pltpu.CompilerParams(dimension_semantics=("arbitrary",), vmem_limit_bytes=VMEM_LIMIT),
        name="moe_combine",
    )(chunks, chunks, x2_prompt, x2_sample, route, lstart, ys, g_final)


TOKEN_TILE = 512
FOX_Q_TILE = 256
FOX_CACHE_TILE = 1024


def kernel(x_prompt, x_sample, cache_fox_k, cache_fox_v, cache_fox_logf, cache_band_k, cache_band_v, cache_mem_k, cache_mem_v, mem_prompt, g_mix, w_in, b_forget, g_out_fox, g_out_band, rel_table, w_out, g_cross, g_mem, w_cq, w_ck, w_cv, w_co, g_ffn, w_router1, b_router1, w_router2, b_router2, w_exp_gate, w_exp_up, w_exp_down, g_final):
    assert g_mix.shape[0] == 1, "single-layer model"
    bsz, seq, d = x_prompt.shape
    sb, st, _ = x_sample.shape
    n_s = sb * st
    n_mem = mem_prompt.shape[1]
    row = lambda g: g.reshape(1, -1)

    w_pad, bf_pad, g_mix_r = _prep_proj(w_in[0], b_forget[0], g_mix[0])
    g_of, g_ob = row(g_out_fox[0]), row(g_out_band[0])
    bias_t, bias_s = _band_bias(_prep_band_bias_row(rel_table[0]))

    qx, kx, vat, qxb, kxb, vbt, kaf, vaf, kbf, vbf, logf = _proj(
        x_prompt, g_mix_r, w_pad, bf_pad, tm=TOKEN_TILE, prompt=True)
    a_p = _fox_prompt(qx, kx, vat, g_of, tq=FOX_Q_TILE)
    b_p = _band_prompt(qxb, kxb, vbt, bias_t, g_ob)

    s_out = _proj(x_sample.reshape(1, n_s, d), g_mix_r, w_pad, bf_pad, tm=n_s, prompt=False)
    sqa, ska, sva, sqb, skb, svb, skaf, svaf, skbf, svbf = (t.reshape(sb, st, W_GROUP) for t in s_out[:10])
    slogf = s_out[10].reshape(sb, st, N_HEADS)
    slft = s_out[11].reshape(N_HEADS, sb, st).transpose(1, 0, 2)
    past = cache_fox_k.shape[2]
    a_s = _fox_sample(sqa, ska, sva, slft,
                      cache_fox_k[0].reshape(sb, past, W_GROUP), cache_fox_v[0].reshape(sb, past, W_GROUP),
                      cache_fox_logf[0].transpose(0, 2, 1), g_of, pt=FOX_CACHE_TILE)
    bp = cache_band_k.shape[2]
    b_s, nbk, nbv = _band_sample(sqb, skb, svb, skbf, svbf,
                                 cache_band_k[0].reshape(sb, bp, W_GROUP), cache_band_v[0].reshape(sb, bp, W_GROUP),
                                 bias_s, g_ob)

    w_ckv = jnp.concatenate([w_ck[0], w_cv[0]], axis=1).astype(BF16)
    mkf, mvf, mk, mv = _mem_kv(mem_prompt, row(g_mem[0]), w_ckv)
    post_w = _prep_post(w_out[0], g_cross[0], w_cq[0], w_co[0], g_ffn[0],
                        w_router1[0], b_router1[0], w_router2[0], b_router2[0])
    x2_p, h3_p, route_p, routet_p, cnt_p = _post_block(x_prompt, a_p, b_p, mk, mv, post_w, tm=TOKEN_TILE)
    cmk = cache_mem_k[0].reshape(sb, n_mem, W_MEM).astype(BF16)
    cmv = cache_mem_v[0].reshape(sb, n_mem, W_MEM).astype(BF16)
    x2_s, h3_s, route_s, routet_s, cnt_s = _post_block(x_sample, a_s, b_s, cmk, cmv, post_w, tm=TOKEN_TILE)
    route = jnp.concatenate([route_p, route_s], axis=0)
    route_t = jnp.concatenate([routet_p, routet_s], axis=1)

    plan = _plan(jnp.concatenate([cnt_p, cnt_s], axis=0), TOKEN_TILE)
    xs = _dispatch(h3_p, h3_s, route_t, plan, tm=TOKEN_TILE)
    ys = _experts(xs, plan[3], plan[2], w_exp_gate[0], w_exp_up[0], w_exp_down[0])
    y_p, y_s = _combine(x2_p, x2_s, route, plan, ys, row(g_final), tm=TOKEN_TILE)

    heads = lambda t, n: t.reshape(1, n, -1, N_HEADS, HEAD_DIM)
    mem_heads = lambda t: t.reshape(1, bsz, n_mem, N_HEADS_MEM, HEAD_DIM_MEM)
    return (y_p.reshape(bsz, seq, d), y_s.reshape(sb, st, d),
            heads(kaf, bsz), heads(vaf, bsz), logf.reshape(1, bsz, seq, N_HEADS),
            heads(kbf, bsz), heads(vbf, bsz), mem_heads(mkf), mem_heads(mvf),
            heads(skaf, sb), heads(svaf, sb), slogf.reshape(1, sb, st, N_HEADS),
            heads(nbk, sb), heads(nbv, sb))
```

```python
import functools

import jax
import jax.numpy as jnp
from jax import lax
from jax.experimental import pallas as pl
from jax.experimental.pallas import tpu as pltpu

F32 = jnp.float32
BF16 = jnp.bfloat16

D_MODEL = 1024
HEAD_DIM = 64
N_HEADS = 8
W_GROUP = N_HEADS * HEAD_DIM
N_PAIRS = N_HEADS // 2
CHUNK = 64
LEFT_CHUNKS = 8
LEFT = LEFT_CHUNKS * CHUNK
REL_CLIP = 128
EPS = 1e-6
NEG_INF = -1e30
ATTN_SCALE = HEAD_DIM ** -0.5
LANES = 128
PROJ_PAD = 3 * W_GROUP * 2 + LANES
VMEM_LIMIT = 56 * 1024 * 1024


def _rms(x, g):
    ms = jnp.mean(x * x, axis=-1, keepdims=True)
    return x * lax.rsqrt(ms + EPS) * g


def _log_sigmoid(x):
    return -(jnp.maximum(-x, 0.0) + jnp.log1p(jnp.exp(-jnp.abs(x))))


def _lane_cumsum(x):
    n = x.shape[-1]
    lane = lax.broadcasted_iota(jnp.int32, x.shape, 1)
    k = 1
    while k < n:
        x = x + jnp.where(lane >= k, pltpu.roll(x, k, axis=1), 0.0)
        k *= 2
    return x


LOG2E = 1.4426950408889634
SCALE_BASE2 = ATTN_SCALE * LOG2E


def _split3(x):
    hi = x.astype(BF16).astype(F32)
    mid = (x - hi).astype(BF16).astype(F32)
    lo = x - hi - mid
    return hi, mid, lo


def _extra_lane(parity):
    return HEAD_DIM if parity == 0 else 0


def _fox_extras(c3t, hp, tm):
    row = lax.broadcasted_iota(jnp.int32, (8, tm), 0)

    def group(h, q_side):
        hi, mid, lo = (p[h:h + 1, :] for p in c3t)
        if q_side:
            return jnp.where(row < 3, 1.0, jnp.where(row == 3, hi, jnp.where(row == 4, mid, jnp.where(row == 5, lo, 0.0))))
        return jnp.where(row == 0, -hi, jnp.where(row == 1, -mid, jnp.where(row == 2, -lo, jnp.where(row < 6, 1.0, 0.0))))

    gap = jnp.zeros((HEAD_DIM - 8, tm), F32)
    sides = []
    for q_side in (True, False):
        t = jnp.concatenate([group(2 * hp + 1, q_side), gap, group(2 * hp, q_side), gap], axis=0)
        sides.append(t.T)
    return sides


def _head_blocks(x128, extras, lane):
    return (jnp.where(lane < HEAD_DIM, x128, extras).astype(BF16),
            jnp.where(lane >= HEAD_DIM, x128, extras).astype(BF16))


Q_A, K_A, V_A, Q_B, K_B, V_B = range(6)


PROJ_CHAIN = 256


def _chains(tm, chain=PROJ_CHAIN):
    n = max(tm // chain, 1)
    return [pl.ds(i * (tm // n), tm // n) for i in range(n)]


def _proj_common(rows, x_ref, g_ref, w_ref, bf_ref, kaf_ref, vaf_ref, kbf_ref, vbf_ref, logf_ref, keep_tiles):
    s = pl.program_id(1)
    ns = pl.num_programs(1)
    h = _rms(x_ref[rows, :], g_ref[...]).astype(BF16)
    w = W_GROUP
    zf = jnp.dot(h, w_ref[:, 6 * w:6 * w + LANES], preferred_element_type=F32)
    z = [jnp.dot(h, w_ref[:, g * w:(g + 1) * w], preferred_element_type=F32) for g in range(6)]
    kaf_ref[rows, :] = z[K_A]
    vaf_ref[rows, :] = z[V_A]

    @pl.when(s >= ns - keep_tiles)
    def _():
        kbf_ref[rows, :] = z[K_B]
        vbf_ref[rows, :] = z[V_B]

    logf = _log_sigmoid(zf + bf_ref[...])
    logf_ref[rows, :] = logf[:, :N_HEADS]
    return z, logf


def _proj_prompt_body(x_ref, g_ref, w_ref, bf_ref, qx_ref, kx_ref, vat_ref, qxb_ref, kxb_ref, vbt_ref,
                      kaf_ref, vaf_ref, kbf_ref, vbf_ref, logf_ref, carry_ref, *, tm, keep_tiles):
    @pl.when(pl.program_id(1) == 0)
    def _():
        carry_ref[...] = jnp.zeros_like(carry_ref)

    for rows in _chains(tm):
        n = rows.size
        z, logf = _proj_common(rows, x_ref, g_ref, w_ref, bf_ref, kaf_ref, vaf_ref, kbf_ref, vbf_ref, logf_ref,
                               keep_tiles)
        vat_ref[:, rows] = z[V_A].T.astype(BF16)
        vbt_ref[:, rows] = z[V_B].T.astype(BF16)
        ct = _lane_cumsum(logf.T[:N_HEADS, :]) + carry_ref[:, 0:1]
        carry_ref[...] = jnp.broadcast_to(ct[:, n - 1:n], carry_ref.shape)
        c3t = _split3(ct * LOG2E)
        lane = lax.broadcasted_iota(jnp.int32, (n, LANES), 1)
        band_q_extras = jnp.where((lane == _extra_lane(0)) | (lane == _extra_lane(1)), 1.0, 0.0)
        band_k_extras = jnp.zeros((n, LANES), F32)
        for hp in range(N_PAIRS):
            blocks = slice(2 * hp * LANES, 2 * (hp + 1) * LANES)
            blk = lambda group, hp=hp, z=z: z[group][:, hp * LANES:(hp + 1) * LANES]
            q_extras, k_extras = _fox_extras(c3t, hp, n)
            qx_ref[rows, blocks] = jnp.concatenate(_head_blocks(blk(Q_A) * SCALE_BASE2, q_extras, lane), axis=1)
            kx_ref[rows, blocks] = jnp.concatenate(_head_blocks(blk(K_A), k_extras, lane), axis=1)
            qxb_ref[rows, blocks] = jnp.concatenate(_head_blocks(blk(Q_B) * SCALE_BASE2, band_q_extras, lane), axis=1)
            kxb_ref[rows, blocks] = jnp.concatenate(_head_blocks(blk(K_B), band_k_extras, lane), axis=1)


def _proj_sample_body(x_ref, g_ref, w_ref, bf_ref, qa_ref, ka_ref, va_ref, qb_ref, kb_ref, vb_ref,
                      kaf_ref, vaf_ref, kbf_ref, vbf_ref, logf_ref, lt_ref, *, tm, keep_tiles):
    for rows in _chains(tm):
        z, logf = _proj_common(rows, x_ref, g_ref, w_ref, bf_ref, kaf_ref, vaf_ref, kbf_ref, vbf_ref, logf_ref,
                               keep_tiles)
        qa_ref[rows, :] = (z[Q_A] * ATTN_SCALE).astype(BF16)
        ka_ref[rows, :] = z[K_A].astype(BF16)
        va_ref[rows, :] = z[V_A].astype(BF16)
        qb_ref[rows, :] = (z[Q_B] * ATTN_SCALE).astype(BF16)
        kb_ref[rows, :] = z[K_B].astype(BF16)
        vb_ref[rows, :] = z[V_B].astype(BF16)
        lt_ref[:, rows] = logf.T[:N_HEADS, :]


def _proj(x, g_mix, w_pad, bf_pad, *, tm, prompt):
    b, s, d = x.shape
    ns = s // tm
    keep = min(LEFT, s)
    assert s % tm == 0 and keep % tm == 0
    keep_tiles = keep // tm
    row = pl.BlockSpec((None, tm, W_GROUP), lambda i, j: (i, j, 0))
    wide = pl.BlockSpec((None, tm, N_HEADS * LANES), lambda i, j: (i, j, 0))
    col = pl.BlockSpec((None, W_GROUP, tm), lambda i, j: (i, 0, j))
    keep_spec = pl.BlockSpec((None, tm, W_GROUP), lambda i, j: (i, jnp.maximum(j - (ns - keep_tiles), 0), 0))
    heads_row = pl.BlockSpec((None, tm, N_HEADS), lambda i, j: (i, j, 0))
    heads_col = pl.BlockSpec((None, N_HEADS, tm), lambda i, j: (i, 0, j))
    const = lambda shape: pl.BlockSpec(shape, lambda i, j: (0,) * len(shape))
    rows_bf = jax.ShapeDtypeStruct((b, s, W_GROUP), BF16)
    wide_bf = jax.ShapeDtypeStruct((b, s, N_HEADS * LANES), BF16)
    cols_bf = jax.ShapeDtypeStruct((b, W_GROUP, s), BF16)
    f32_tail = [jax.ShapeDtypeStruct((b, s, W_GROUP), F32)] * 2
    f32_tail += [jax.ShapeDtypeStruct((b, keep, W_GROUP), F32)] * 2
    f32_tail += [jax.ShapeDtypeStruct((b, s, N_HEADS), F32)]
    tail_specs = [row, row, keep_spec, keep_spec, heads_row]
    if prompt:
        body = functools.partial(_proj_prompt_body, tm=tm, keep_tiles=keep_tiles)
        out_shape = [wide_bf, wide_bf, cols_bf, wide_bf, wide_bf, cols_bf] + f32_tail
        out_specs = [wide, wide, col, wide, wide, col] + tail_specs
        scratch = [pltpu.VMEM((N_HEADS, LANES), F32)]
    else:
        body = functools.partial(_proj_sample_body, tm=tm, keep_tiles=keep_tiles)
        out_shape = [rows_bf] * 6 + f32_tail + [jax.ShapeDtypeStruct((b, N_HEADS, s), F32)]
        out_specs = [row] * 6 + tail_specs + [heads_col]
        scratch = []
    return pl.pallas_call(
        body,
        grid=(b, ns),
        in_specs=[pl.BlockSpec((None, tm, d), lambda i, j: (i, j, 0)),
                  const((1, d)), const(w_pad.shape), const((1, LANES))],
        out_specs=out_specs,
        out_shape=out_shape,
        scratch_shapes=scratch,
        compiler_params=pltpu.CompilerParams(
            dimension_semantics=("parallel", "arbitrary"), vmem_limit_bytes=VMEM_LIMIT),
        name="proj",
    )(x, g_mix, w_pad, bf_pad)


def _prep_proj(w_in, b_forget, g_mix):
    cols = w_in.shape[-1]
    w_pad = jnp.pad(w_in, ((0, 0), (0, PROJ_PAD - cols))).astype(BF16)
    bf_pad = jnp.pad(b_forget.reshape(1, -1), ((0, 0), (0, LANES - N_HEADS))).astype(F32)
    return w_pad, bf_pad, g_mix.reshape(1, -1)


def _pair_masks():
    lane = lax.broadcasted_iota(jnp.int32, (1, LANES), 1)
    return lane < HEAD_DIM


def _head_q(q128, even_lanes, parity):
    keep = even_lanes if parity == 0 else jnp.logical_not(even_lanes)
    return jnp.where(keep, q128, jnp.zeros_like(q128))


def _head_v(v128, even_lanes, parity):
    keep = even_lanes if parity == 0 else jnp.logical_not(even_lanes)
    return jnp.where(keep, v128, jnp.ones_like(v128))


def _head_out(acc_even, acc_odd, even_lanes):
    inv_e = 1.0 / acc_even[:, HEAD_DIM:HEAD_DIM + 1]
    inv_o = 1.0 / acc_odd[:, 0:1]
    return jnp.where(even_lanes, acc_even * inv_e, acc_odd * inv_o)


_NT = (((1,), (1,)), ((), ()))


def _pair_rows():
    row = lax.broadcasted_iota(jnp.int32, (LANES, 1), 0)
    return row < HEAD_DIM


def _head_vt(vt128, even_rows, parity):
    keep = even_rows if parity == 0 else jnp.logical_not(even_rows)
    return jnp.where(keep, vt128, jnp.ones_like(vt128))


def _head_out_t(acc_even, acc_odd, even_rows):
    inv_e = 1.0 / acc_even[HEAD_DIM:HEAD_DIM + 1, :]
    inv_o = 1.0 / acc_odd[0:1, :]
    return jnp.where(even_rows, acc_even * inv_e, acc_odd * inv_o)


def _fox_body(qx_ref, kx_ref, vt_ref, g_ref, o_ref, s_scr, p_scr, m_scr, alpha_scr, acc_scr, ot_scr, *, tq):
    qi = pl.program_id(1)
    even_rows = _pair_rows()
    m_scr[...] = jnp.full(m_scr.shape, NEG_INF, F32)
    acc_scr[...] = jnp.zeros(acc_scr.shape, F32)
    key = lax.broadcasted_iota(jnp.int32, (tq, tq), 0)
    qry = lax.broadcasted_iota(jnp.int32, (tq, tq), 1)
    causal = key <= qry

    def logits(j, half):
        start = pl.multiple_of(j * tq, tq)
        for h in range(N_HEADS):
            head = slice(h * LANES, (h + 1) * LANES)
            s_scr[half, h] = lax.dot_general(kx_ref[pl.ds(start, tq), head], qx_ref[:, head], _NT,
                                             preferred_element_type=F32)

    def weigh(j, half, masked):
        start = pl.multiple_of(j * tq, tq)
        for h in range(N_HEADS):
            st = s_scr[half, h]
            if masked:
                st = jnp.where(causal, st, NEG_INF)
            m_old = m_scr[h:h + 1, :]
            m_new = jnp.maximum(m_old, jnp.max(st, axis=0, keepdims=True))
            p_scr[h] = jnp.exp2(st - m_new).astype(BF16)
            alpha_scr[h:h + 1, :] = jnp.exp2(m_old - m_new)
            m_scr[h:h + 1, :] = m_new
        for h in range(N_HEADS):
            pair = slice((h // 2) * LANES, (h // 2 + 1) * LANES)
            vt = _head_vt(vt_ref[pair, pl.ds(start, tq)], even_rows, h % 2)
            acc_scr[h] = acc_scr[h] * alpha_scr[h:h + 1, :] + jnp.dot(vt, p_scr[h], preferred_element_type=F32)

    def step(j, cur, nxt):
        logits(j + 1, nxt)
        weigh(j, cur, False)

    def body(jj, carry):
        step(2 * jj, 0, 1)
        step(2 * jj + 1, 1, 0)
        return carry

    logits(0, 0)
    lax.fori_loop(0, qi // 2, body, None)
    odd = lax.rem(qi, 2) == 1

    @pl.when(odd)
    def _():
        step(qi - 1, 0, 1)
        weigh(qi, 1, True)

    @pl.when(jnp.logical_not(odd))
    def _():
        weigh(qi, 0, True)
    for hp in range(N_PAIRS):
        ot_scr[hp * LANES:(hp + 1) * LANES, :] = _head_out_t(acc_scr[2 * hp], acc_scr[2 * hp + 1], even_rows)
    o_ref[...] = _rms(ot_scr[...].T, g_ref[...]).astype(BF16)


def _fox_prompt(qx, kx, vat, g_out, *, tq):
    b, s, wx = qx.shape
    w = vat.shape[1]
    return pl.pallas_call(
        functools.partial(_fox_body, tq=tq),
        grid=(b, s // tq),
        in_specs=[pl.BlockSpec((None, tq, wx), lambda i, j: (i, j, 0)),
                  pl.BlockSpec((None, s, wx), lambda i, j: (i, 0, 0)),
                  pl.BlockSpec((None, w, s), lambda i, j: (i, 0, 0)),
                  pl.BlockSpec((1, w), lambda i, j: (0, 0))],
        out_specs=pl.BlockSpec((None, tq, w), lambda i, j: (i, j, 0)),
        out_shape=jax.ShapeDtypeStruct((b, s, w), BF16),
        scratch_shapes=[pltpu.VMEM((2, N_HEADS, tq, tq), F32), pltpu.VMEM((N_HEADS, tq, tq), BF16),
                        pltpu.VMEM((N_HEADS, tq), F32), pltpu.VMEM((N_HEADS, tq), F32),
                        pltpu.VMEM((N_HEADS, LANES, tq), F32), pltpu.VMEM((w, tq), F32)],
        compiler_params=pltpu.CompilerParams(
            dimension_semantics=("parallel", "arbitrary"), vmem_limit_bytes=VMEM_LIMIT),
        name="fox_prompt",
    )(qx, kx, vat, g_out)


BAND_CHUNKS = 4
BAND_Q = BAND_CHUNKS * CHUNK
BAND_K = (LEFT_CHUNKS + BAND_CHUNKS) * CHUNK
BIAS_ROW = BAND_K + BAND_Q
BAND_K_SAMPLE = (LEFT + CHUNK + LANES - 1) // LANES * LANES


def _prep_band_bias_row(rel_table):
    pivot = LEFT + BAND_Q
    n_hi = pivot - REL_CLIP + 1
    n_mid = min(2 * REL_CLIP, BIAS_ROW - n_hi)
    n_lo = BIAS_ROW - n_hi - n_mid
    parts = [jnp.broadcast_to(rel_table[2 * REL_CLIP:], (n_hi, N_HEADS)),
             rel_table[2 * REL_CLIP - 1::-1][:n_mid],
             jnp.broadcast_to(rel_table[:1], (n_lo, N_HEADS))]
    return jnp.concatenate(parts, axis=0).T.reshape(N_HEADS, 1, BIAS_ROW)


def _band_bias_body(row_ref, bt_ref, bs_ref):
    rows = jnp.broadcast_to(row_ref[...], (BAND_Q, BIAS_ROW))
    skew = pltpu.roll(rows, 0, axis=1, stride=1, stride_axis=0)
    bias = skew[:, BAND_Q:]
    bs_ref[...] = bias[:CHUNK, :BAND_K_SAMPLE]
    qc = lax.broadcasted_iota(jnp.int32, (BAND_Q, BAND_K), 0) // CHUNK
    kc = lax.broadcasted_iota(jnp.int32, (BAND_Q, BAND_K), 1) // CHUNK
    bt_ref[...] = jnp.where((kc >= qc) & (kc <= qc + LEFT_CHUNKS), bias * LOG2E, NEG_INF).T


def _band_bias(bias_row):
    return pl.pallas_call(
        _band_bias_body,
        grid=(N_HEADS,),
        in_specs=[pl.BlockSpec((None, 1, BIAS_ROW), lambda h: (h, 0, 0))],
        out_specs=[pl.BlockSpec((None, BAND_K, BAND_Q), lambda h: (h, 0, 0)),
                   pl.BlockSpec((None, CHUNK, BAND_K_SAMPLE), lambda h: (h, 0, 0))],
        out_shape=[jax.ShapeDtypeStruct((N_HEADS, BAND_K, BAND_Q), F32),
                   jax.ShapeDtypeStruct((N_HEADS, CHUNK, BAND_K_SAMPLE), F32)],
        name="band_bias",
    )(bias_row)


def _band_body(qx_ref, kx_ref, vt_ref, bias_ref, g_ref, o_ref, kpad, vtpad, s_scr, p_scr, ot_scr, *, s_len):
    step = pl.program_id(1)

    @pl.when(step == 0)
    def _():
        lane = lax.broadcasted_iota(jnp.int32, (LEFT, 2 * LANES), 1)
        flags = (lane == _extra_lane(0)) | (lane == LANES + _extra_lane(1))
        pad_pair = jnp.where(flags, NEG_INF, 0.0).astype(BF16)
        for hp in range(N_PAIRS):
            kpad[0:LEFT, 2 * hp * LANES:2 * (hp + 1) * LANES] = pad_pair
        vtpad[:, 0:LEFT] = jnp.zeros((W_GROUP, LEFT), BF16)
        kpad[LEFT:LEFT + s_len, :] = kx_ref[...]
        vtpad[:, LEFT:LEFT + s_len] = vt_ref[...]

    even_rows = _pair_rows()
    start = pl.multiple_of(step * BAND_Q, BAND_Q)
    for h in range(N_HEADS):
        head = slice(h * LANES, (h + 1) * LANES)
        s_scr[h] = lax.dot_general(kpad[pl.ds(start, BAND_K), head], qx_ref[:, head], _NT,
                                   preferred_element_type=F32)
    for h in range(N_HEADS):
        st = s_scr[h] + bias_ref[h]
        p_scr[h] = jnp.exp2(st - jnp.max(st, axis=0, keepdims=True)).astype(BF16)
    for hp in range(N_PAIRS):
        pair = slice(hp * LANES, (hp + 1) * LANES)
        vtwin = vtpad[pair, pl.ds(start, BAND_K)]
        accs = [jnp.dot(_head_vt(vtwin, even_rows, parity), p_scr[2 * hp + parity], preferred_element_type=F32)
                for parity in range(2)]
        ot_scr[pair, :] = _head_out_t(accs[0], accs[1], even_rows)
    o_ref[...] = _rms(ot_scr[...].T, g_ref[...]).astype(BF16)


def _band_prompt(qxb, kxb, vbt, bias_t, g_out):
    b, s, wx = qxb.shape
    w = vbt.shape[1]
    return pl.pallas_call(
        functools.partial(_band_body, s_len=s),
        grid=(b, s // BAND_Q),
        in_specs=[pl.BlockSpec((None, BAND_Q, wx), lambda i, j: (i, j, 0)),
                  pl.BlockSpec((None, s, wx), lambda i, j: (i, 0, 0)),
                  pl.BlockSpec((None, w, s), lambda i, j: (i, 0, 0)),
                  pl.BlockSpec(bias_t.shape, lambda i, j: (0, 0, 0)),
                  pl.BlockSpec((1, w), lambda i, j: (0, 0))],
        out_specs=pl.BlockSpec((None, BAND_Q, w), lambda i, j: (i, j, 0)),
        out_shape=jax.ShapeDtypeStruct((b, s, w), BF16),
        scratch_shapes=[pltpu.VMEM((LEFT + s, wx), BF16), pltpu.VMEM((w, LEFT + s), BF16),
                        pltpu.VMEM((N_HEADS, BAND_K, BAND_Q), F32), pltpu.VMEM((N_HEADS, BAND_K, BAND_Q), BF16),
                        pltpu.VMEM((w, BAND_Q), F32)],
        compiler_params=pltpu.CompilerParams(
            dimension_semantics=("parallel", "arbitrary"), vmem_limit_bytes=VMEM_LIMIT),
        name="band_prompt",
    )(qxb, kxb, vbt, bias_t, g_out)


def _row_to_col(row):
    n = row.shape[-1]
    r = lax.broadcasted_iota(jnp.int32, (n, n), 0)
    c = lax.broadcasted_iota(jnp.int32, (n, n), 1)
    return jnp.sum(jnp.where(r == c, jnp.broadcast_to(row, (n, n)), 0.0), axis=-1, keepdims=True)


def _fox_sample_body(q_ref, kn_ref, vn_ref, lft_ref, kc_ref, vc_ref, clft_ref, g_ref, o_ref,
                     cct_scr, cn_scr, m_scr, l_scr, acc_scr, o_scr, s_scr, p_scr, alpha_scr, *, t_new, pt):
    p_idx = pl.program_id(1)
    n_p = pl.num_programs(1)
    head = lambda h: slice(h * HEAD_DIM, (h + 1) * HEAD_DIM)

    @pl.when(p_idx == 0)
    def _():
        cct = _lane_cumsum(clft_ref[...])
        cct_scr[...] = cct
        cn_scr[...] = _lane_cumsum(lft_ref[...]) + cct[:, cct.shape[1] - 1:]
        m_scr[...] = jnp.full(m_scr.shape, NEG_INF, F32)
        l_scr[...] = jnp.zeros(l_scr.shape, F32)
        acc_scr[...] = jnp.zeros(acc_scr.shape, F32)

    start = pl.multiple_of(p_idx * pt, pt)
    k_heads = pltpu.einshape("phd->hpd", kc_ref[...])
    v_heads = pltpu.einshape("phd->hpd", vc_ref[...])
    for h in range(N_HEADS):
        s_scr[h] = lax.dot_general(q_ref[:, head(h)], k_heads[h].astype(BF16), _NT, preferred_element_type=F32)
    for h in range(N_HEADS):
        cq = _row_to_col(cn_scr[h:h + 1, :])
        s = s_scr[h] + cq - cct_scr[h:h + 1, pl.ds(start, pt)]
        m = m_scr[h]
        m_new = jnp.maximum(m, jnp.max(s, axis=-1, keepdims=True))
        p = jnp.exp(s - m_new)
        alpha = jnp.exp(m - m_new)
        p_scr[h] = p.astype(BF16)
        alpha_scr[h] = alpha
        l_scr[h] = l_scr[h] * alpha + jnp.sum(p, axis=-1, keepdims=True)
        m_scr[h] = m_new
    for h in range(N_HEADS):
        acc_scr[h] = acc_scr[h] * alpha_scr[h] + jnp.dot(p_scr[h], v_heads[h].astype(BF16),
                                                         preferred_element_type=F32)

    @pl.when(p_idx == n_p - 1)
    def _():
        row = lax.broadcasted_iota(jnp.int32, (t_new, t_new), 0)
        col = lax.broadcasted_iota(jnp.int32, (t_new, t_new), 1)
        for h in range(N_HEADS):
            cn_row = cn_scr[h:h + 1, :]
            s = lax.dot_general(q_ref[:, head(h)], kn_ref[:, head(h)], _NT, preferred_element_type=F32)
            s = jnp.where(col <= row, s + _row_to_col(cn_row) - cn_row, NEG_INF)
            m = m_scr[h]
            m_new = jnp.maximum(m, jnp.max(s, axis=-1, keepdims=True))
            p = jnp.exp(s - m_new)
            alpha = jnp.exp(m - m_new)
            l = l_scr[h] * alpha + jnp.sum(p, axis=-1, keepdims=True)
            acc = acc_scr[h] * alpha + jnp.dot(p.astype(BF16), vn_ref[:, head(h)], preferred_element_type=F32)
            o_scr[:, head(h)] = acc * (1.0 / l)
        o_ref[...] = _rms(o_scr[...], g_ref[...]).astype(BF16)


def _fox_sample(q, kn, vn, lft, kc, vc, clft, g_out, *, pt):
    b, t, w = q.shape
    p_len = kc.shape[1]
    new = lambda: pl.BlockSpec((None, t, w), lambda i, j: (i, 0, 0))
    cache = lambda: pl.BlockSpec((None, pt, N_HEADS, HEAD_DIM), lambda i, j: (i, j, 0, 0))
    return pl.pallas_call(
        functools.partial(_fox_sample_body, t_new=t, pt=pt),
        grid=(b, p_len // pt),
        in_specs=[new(), new(), new(),
                  pl.BlockSpec((None, N_HEADS, t), lambda i, j: (i, 0, 0)),
                  cache(), cache(),
                  pl.BlockSpec((None, N_HEADS, p_len), lambda i, j: (i, 0, 0)),
                  pl.BlockSpec((1, w), lambda i, j: (0, 0))],
        out_specs=new(),
        out_shape=jax.ShapeDtypeStruct((b, t, w), BF16),
        scratch_shapes=[pltpu.VMEM((N_HEADS, p_len), F32), pltpu.VMEM((N_HEADS, t), F32),
                        pltpu.VMEM((N_HEADS, t, 1), F32), pltpu.VMEM((N_HEADS, t, 1), F32),
                        pltpu.VMEM((N_HEADS, t, HEAD_DIM), F32),
                        pltpu.VMEM((t, w), F32), pltpu.VMEM((N_HEADS, t, pt), F32),
                        pltpu.VMEM((N_HEADS, t, pt), BF16), pltpu.VMEM((N_HEADS, t, 1), F32)],
        compiler_params=pltpu.CompilerParams(
            dimension_semantics=("parallel", "arbitrary"), vmem_limit_bytes=VMEM_LIMIT),
        name="fox_sample",
    )(q, kn, vn, lft, kc, vc, clft, g_out)


def _band_sample_body(q_ref, kn_ref, vn_ref, knf_ref, vnf_ref, kc_ref, vc_ref, bias_ref, g_ref,
                      o_ref, nk_ref, nv_ref, kcat, vcat, o_scr, *, t_new, bp):
    kcat[0:bp, :] = kc_ref[...].astype(BF16)
    vcat[0:bp, :] = vc_ref[...].astype(BF16)
    kcat[bp:bp + t_new, :] = kn_ref[...]
    vcat[bp:bp + t_new, :] = vn_ref[...]
    nk_ref[0:bp - t_new, :] = kc_ref[t_new:bp, :]
    nv_ref[0:bp - t_new, :] = vc_ref[t_new:bp, :]
    nk_ref[bp - t_new:bp, :] = knf_ref[...]
    nv_ref[bp - t_new:bp, :] = vnf_ref[...]
    even = _pair_masks()
    for hp in range(N_PAIRS):
        lanes = slice(hp * LANES, (hp + 1) * LANES)
        q128 = q_ref[:, lanes]
        k = kcat[:, lanes]
        v = vcat[:, lanes]
        accs = []
        for parity in range(2):
            h = 2 * hp + parity
            s = lax.dot_general(_head_q(q128, even, parity), k, _NT, preferred_element_type=F32)
            s = s + bias_ref[h, 0:t_new, 0:bp + t_new]
            p = jnp.exp(s - jnp.max(s, axis=-1, keepdims=True)).astype(BF16)
            accs.append(jnp.dot(p, _head_v(v, even, parity), preferred_element_type=F32))
        o_scr[:, lanes] = _head_out(accs[0], accs[1], even)
    o_ref[...] = _rms(o_scr[...], g_ref[...]).astype(BF16)


def _band_sample(q, kn, vn, knf, vnf, kc, vc, bias, g_out):
    b, t, w = q.shape
    bp = kc.shape[1]
    assert t == CHUNK and bp == LEFT
    new = lambda: pl.BlockSpec((None, t, w), lambda i: (i, 0, 0))
    buf = lambda: pl.BlockSpec((None, bp, w), lambda i: (i, 0, 0))
    return pl.pallas_call(
        functools.partial(_band_sample_body, t_new=t, bp=bp),
        grid=(b,),
        in_specs=[new(), new(), new(), new(), new(), buf(), buf(),
                  pl.BlockSpec(bias.shape, lambda i: (0, 0, 0)),
                  pl.BlockSpec((1, w), lambda i: (0, 0))],
        out_specs=[new(), buf(), buf()],
        out_shape=[jax.ShapeDtypeStruct((b, t, w), BF16), jax.ShapeDtypeStruct((b, bp, w), F32),
                   jax.ShapeDtypeStruct((b, bp, w), F32)],
        scratch_shapes=[pltpu.VMEM((bp + t, w), BF16), pltpu.VMEM((bp + t, w), BF16), pltpu.VMEM((t, w), F32)],
        compiler_params=pltpu.CompilerParams(dimension_semantics=("parallel",), vmem_limit_bytes=VMEM_LIMIT),
        name="band_sample",
    )(q, kn, vn, knf, vnf, kc, vc, bias, g_out)


N_HEADS_MEM = 4
HEAD_DIM_MEM = 128
W_MEM = N_HEADS_MEM * HEAD_DIM_MEM
MEM_SCALE = HEAD_DIM_MEM ** -0.5


def _memkv_body(m_ref, g_ref, w_ref, kf_ref, vf_ref, k_ref, v_ref):
    h = _rms(m_ref[...], g_ref[...]).astype(BF16)
    z = jnp.dot(h, w_ref[...], preferred_element_type=F32)
    kf_ref[...] = z[:, :W_MEM]
    vf_ref[...] = z[:, W_MEM:]
    k_ref[...] = z[:, :W_MEM].astype(BF16)
    v_ref[...] = z[:, W_MEM:].astype(BF16)


def _mem_kv(mem, g_mem, w_ckv):
    b, n, d = mem.shape
    blk = lambda: pl.BlockSpec((None, n, W_MEM), lambda i: (i, 0, 0))
    return pl.pallas_call(
        _memkv_body,
        grid=(b,),
        in_specs=[pl.BlockSpec((None, n, d), lambda i: (i, 0, 0)),
                  pl.BlockSpec((1, d), lambda i: (0, 0)),
                  pl.BlockSpec(w_ckv.shape, lambda i: (0, 0))],
        out_specs=[blk(), blk(), blk(), blk()],
        out_shape=[jax.ShapeDtypeStruct((b, n, W_MEM), F32)] * 2 + [jax.ShapeDtypeStruct((b, n, W_MEM), BF16)] * 2,
        compiler_params=pltpu.CompilerParams(dimension_semantics=("parallel",), vmem_limit_bytes=VMEM_LIMIT),
        name="mem_kv",
    )(mem, g_mem, w_ckv)


N_GROUPS = 4
EXPERTS_PER_GROUP = 8
N_EXPERTS = N_GROUPS * EXPERTS_PER_GROUP
ROUTE_L2 = N_GROUPS
ROUTE_ROWS = 8
POST_CHAIN = 512
R_EID0, R_EID1, R_RANK0, R_RANK1, R_GATE0, R_GATE1 = range(6)


def _lane_max(x, mask):
    return jnp.max(jnp.where(mask, x, -jnp.inf), axis=-1, keepdims=True)


def _first_lane(mask, lane):
    return jnp.min(jnp.where(mask, lane, LANES), axis=-1, keepdims=True)


def _route(logits, lane):
    is_l1 = lane < N_GROUPS
    m1 = _lane_max(logits, is_l1)
    grp = _first_lane(is_l1 & (logits == m1), lane)
    wg = 1.0 / jnp.sum(jnp.where(is_l1, jnp.exp(logits - m1), 0.0), axis=-1, keepdims=True)
    lo = ROUTE_L2 + grp * EXPERTS_PER_GROUP
    in_grp = (lane >= lo) & (lane < lo + EXPERTS_PER_GROUP)
    v0 = _lane_max(logits, in_grp)
    i0 = _first_lane(in_grp & (logits == v0), lane)
    rest = in_grp & (lane != i0)
    v1 = _lane_max(logits, rest)
    i1 = _first_lane(rest & (logits == v1), lane)
    e1 = jnp.exp(v1 - v0)
    den = 1.0 / (1.0 + e1)
    return i0, i1, wg * den, wg * e1 * den


def _post_body(x_ref, a_ref, b_ref, mk_ref, mv_ref, woa_ref, wob_ref, gc_ref, wcq_ref, wco_ref,
               gf_ref, wrt_ref, brt_ref,
               x2_ref, h3_ref, route_ref, routet_ref, cnt_ref, o_scr, *, tm, nsub):
    seq = tm // nsub
    count = jnp.zeros((1, LANES), F32)
    for rows in _chains(tm, POST_CHAIN):
        n = rows.size
        x1 = (x_ref[rows, :] + jnp.dot(a_ref[rows, :], woa_ref[...], preferred_element_type=F32)
              + jnp.dot(b_ref[rows, :], wob_ref[...], preferred_element_type=F32))
        h2 = _rms(x1, gc_ref[...]).astype(BF16)
        qc = (jnp.dot(h2, wcq_ref[...], preferred_element_type=F32) * MEM_SCALE).astype(BF16)
        span = min(seq, n)
        for part in range(n // span):
            sub = (rows.start + part * span) // seq
            rs = slice(part * span, (part + 1) * span)
            orow = pl.ds(rows.start + part * span, span)
            for hm in range(N_HEADS_MEM):
                lanes = slice(hm * HEAD_DIM_MEM, (hm + 1) * HEAD_DIM_MEM)
                s = lax.dot_general(qc[rs, lanes], mk_ref[sub, :, lanes], _NT, preferred_element_type=F32)
                p = jnp.exp(s - jnp.max(s, axis=-1, keepdims=True))
                inv = 1.0 / jnp.sum(p, axis=-1, keepdims=True)
                o_scr[orow, lanes] = jnp.dot(p.astype(BF16), mv_ref[sub, :, lanes], preferred_element_type=F32) * inv
        x2 = x1 + jnp.dot(o_scr[rows, :].astype(BF16), wco_ref[...], preferred_element_type=F32)
        x2_ref[rows, :] = x2
        h3 = _rms(x2, gf_ref[...]).astype(BF16)
        h3_ref[rows, :] = h3

        logits = jnp.dot(h3, wrt_ref[...], preferred_element_type=F32) + brt_ref[...]
        lane = lax.broadcasted_iota(jnp.int32, (n, LANES), 1)
        i0, i1, g0, g1 = _route(logits, lane)
        e0 = i0 - ROUTE_L2
        e1 = i1 - ROUTE_L2
        hit0 = lane == e0
        hit1 = lane == e1
        onehot = jnp.where(hit0 | hit1, 1.0, 0.0)
        row = lax.broadcasted_iota(jnp.int32, (n, n), 0)
        col = lax.broadcasted_iota(jnp.int32, (n, n), 1)
        before = jnp.where(col < row, 1.0, 0.0).astype(BF16)
        seen = jnp.dot(before, onehot.astype(BF16), preferred_element_type=F32) + count
        rank0 = jnp.sum(jnp.where(hit0, seen, 0.0), axis=-1, keepdims=True)
        rank1 = jnp.sum(jnp.where(hit1, seen, 0.0), axis=-1, keepdims=True)
        count = count + jnp.sum(onehot, axis=0, keepdims=True)

        rec = jnp.zeros((n, LANES), F32)
        for idx, val in ((R_EID0, e0.astype(F32)), (R_EID1, e1.astype(F32)), (R_RANK0, rank0),
                         (R_RANK1, rank1), (R_GATE0, g0), (R_GATE1, g1)):
            rec = jnp.where(lane == idx, val, rec)
        route_ref[rows, :] = rec[:, :ROUTE_ROWS]
        routet_ref[:, rows] = rec.T[:ROUTE_ROWS, :]
    cnt_ref[...] = count


def _post_block(x, a_n, b_n, mk, mv, weights, *, tm):
    b, s, d = x.shape
    if s >= tm:
        nsub, grid = 1, (b, s // tm)
        tok = lambda i, j: (i, j, 0)
        flat = lambda i, j: i * (s // tm) + j
    else:
        nsub = tm // s
        assert b % nsub == 0
        x, a_n, b_n = (t.reshape(b // nsub, tm, t.shape[-1]) for t in (x, a_n, b_n))
        grid = (b // nsub, 1)
        tok = lambda i, j: (i, 0, 0)
        flat = lambda i, j: i
    mem = lambda i, j: (i, 0, 0)
    const = lambda arr: pl.BlockSpec(arr.shape, lambda i, j: (0,) * arr.ndim)
    in_specs = [pl.BlockSpec((None, tm, d), tok),
                pl.BlockSpec((None, tm, W_GROUP), tok), pl.BlockSpec((None, tm, W_GROUP), tok),
                pl.BlockSpec((nsub, mk.shape[1], W_MEM), mem), pl.BlockSpec((nsub, mv.shape[1], W_MEM), mem)]
    in_specs += [const(w) for w in weights]
    n = b * s
    out_shape = [jax.ShapeDtypeStruct((n, d), F32), jax.ShapeDtypeStruct((n, d), BF16),
                 jax.ShapeDtypeStruct((n, ROUTE_ROWS), F32), jax.ShapeDtypeStruct((ROUTE_ROWS, n), F32),
                 jax.ShapeDtypeStruct((n // tm, 1, LANES), F32)]
    out_specs = [pl.BlockSpec((tm, d), lambda i, j: (flat(i, j), 0)),
                 pl.BlockSpec((tm, d), lambda i, j: (flat(i, j), 0)),
                 pl.BlockSpec((tm, ROUTE_ROWS), lambda i, j: (flat(i, j), 0)),
                 pl.BlockSpec((ROUTE_ROWS, tm), lambda i, j: (0, flat(i, j))),
                 pl.BlockSpec((None, 1, LANES), lambda i, j: (flat(i, j), 0, 0))]
    return pl.pallas_call(
        functools.partial(_post_body, tm=tm, nsub=nsub),
        grid=grid,
        in_specs=in_specs,
        out_specs=out_specs,
        out_shape=out_shape,
        scratch_shapes=[pltpu.VMEM((tm, W_MEM), F32)],
        compiler_params=pltpu.CompilerParams(
            dimension_semantics=("parallel", "parallel"), vmem_limit_bytes=VMEM_LIMIT),
        name="post_block",
    )(x, a_n, b_n, mk, mv, *weights)


def _prep_post(w_out, g_cross, w_cq, w_co, g_ffn, w_r1, b_r1, w_r2, b_r2):
    pad = LANES - N_GROUPS - N_EXPERTS
    w_rt = jnp.pad(jnp.concatenate([w_r1, w_r2], axis=1), ((0, 0), (0, pad))).astype(BF16)
    b_rt = jnp.pad(jnp.concatenate([b_r1, b_r2]).reshape(1, -1), ((0, 0), (0, pad))).astype(F32)
    return [w_out[:W_GROUP].astype(BF16), w_out[W_GROUP:].astype(BF16), g_cross.reshape(1, -1),
            w_cq.astype(BF16), w_co.astype(BF16), g_ffn.reshape(1, -1), w_rt, b_rt]


D_EXPERT = 512
TOP_K = 2
ROW_TILE = 512
MXU_DIM = 256
RUN_ALIGN = 16
PLAN_TILES = LANES
TILE_TABLE = 2 * LANES


def _local_rows(tm):
    return -(-(TOP_K * tm + N_EXPERTS * (RUN_ALIGN - 1)) // MXU_DIM) * MXU_DIM


def _n_row_tiles(n_tokens, tm):
    rows = n_tokens * TOP_K + (n_tokens // tm) * N_EXPERTS * (RUN_ALIGN - 1) + N_EXPERTS * (ROW_TILE - 1)
    return rows // ROW_TILE


N_CHUNK_LANE = LANES - 1


def _plan_body(cnt_ref, lstart_ref, chunk_ref, offs_ref, te_ref):
    cnt = cnt_ref[...].astype(jnp.int32)
    n16 = ((cnt + (RUN_ALIGN - 1)) & ~(RUN_ALIGN - 1)).astype(F32)
    lend = _lane_cumsum(n16)
    lstart = lend - n16
    earlier = (_lane_cumsum(n16.T) - n16.T).T
    total = jnp.sum(n16, axis=0, keepdims=True).astype(jnp.int32)
    seg = jnp.broadcast_to((total + (ROW_TILE - 1)) & ~(ROW_TILE - 1), (8, LANES)).astype(F32)
    ends = _lane_cumsum(seg)
    offs = ends - seg
    shift = earlier + offs[0:1, :] - lstart
    lstart_ref[...] = lstart.astype(jnp.int32)
    offs_ref[...] = offs[0:1, :].astype(jnp.int32)

    lane = lax.broadcasted_iota(jnp.int32, (PLAN_TILES, LANES), 1)
    local_row = (lane * RUN_ALIGN).astype(F32)
    owner = jnp.zeros((PLAN_TILES, LANES), jnp.int32)
    for e in range(N_EXPERTS):
        owner = owner + jnp.where(lend[:, e:e + 1] <= local_row, 1, 0)
    glob = local_row
    for e in range(N_EXPERTS):
        glob = glob + jnp.where(owner == e, shift[:, e:e + 1], 0.0)
    n_chunks = lend[:, N_EXPERTS - 1:N_EXPERTS] * (1.0 / RUN_ALIGN)
    chunk_ref[...] = jnp.where(lane == N_CHUNK_LANE, n_chunks, glob).astype(jnp.int32)

    tile_start = (lax.broadcasted_iota(jnp.int32, te_ref.shape, 1) * ROW_TILE).astype(F32)
    te = jnp.zeros(te_ref.shape, jnp.int32)
    for e in range(N_EXPERTS):
        end_e = jnp.sum(jnp.where(lane[0:1, :] == e, ends[0:1, :], 0.0), axis=-1, keepdims=True)
        te = te + jnp.where(end_e <= tile_start, 1, 0)
    te_ref[...] = jnp.minimum(te, N_EXPERTS - 1)


def _plan(counts, tm):
    nt = counts.shape[0]
    assert nt <= PLAN_TILES and _local_rows(tm) // RUN_ALIGN <= N_CHUNK_LANE
    cnt = jnp.pad(counts.reshape(nt, LANES), ((0, PLAN_TILES - nt), (0, 0)))
    grid_i32 = jax.ShapeDtypeStruct((PLAN_TILES, LANES), jnp.int32)
    lstart, chunks, offs, te = pl.pallas_call(
        _plan_body,
        out_shape=[grid_i32, grid_i32, jax.ShapeDtypeStruct((1, LANES), jnp.int32),
                   jax.ShapeDtypeStruct((1, TILE_TABLE), jnp.int32)],
        name="moe_plan",
    )(cnt)
    per_tile = lambda t: t[:nt].reshape(nt, 1, LANES)
    return per_tile(lstart), per_tile(chunks), offs.reshape(LANES), te.reshape(TILE_TABLE)


def _for_each_chunk(chunk_ref, fn):
    def body(c, carry):
        fn(pl.multiple_of(c * RUN_ALIGN, RUN_ALIGN), pl.multiple_of(chunk_ref[0, c], RUN_ALIGN))
        return carry

    lax.fori_loop(0, chunk_ref[0, N_CHUNK_LANE], body, None)


def _local_positions_row(rt_ref, lstart_ref):
    pos = []
    for r_eid, r_rank in ((R_EID0, R_RANK0), (R_EID1, R_RANK1)):
        eid = rt_ref[r_eid:r_eid + 1, :].astype(jnp.int32)
        p = rt_ref[r_rank:r_rank + 1, :].astype(jnp.int32)
        for e in range(N_EXPERTS):
            p = p + jnp.where(eid == e, lstart_ref[0, e], 0)
        pos.append(p)
    return pos


def _dispatch_body(offs_ref, lstart_ref, chunk_ref, pchunk_ref,
                   hp_ref, hs_ref, rt_ref, xs_ref, loc, zeros, sems, zsem, *, tm, n_prompt_tiles):
    i = pl.program_id(0)
    n_tiles = xs_ref.shape[0] // ROW_TILE
    half = lax.rem(i, 2)

    @pl.when(i == 0)
    def _():
        zeros[...] = jnp.zeros(zeros.shape, zeros.dtype)
        zero_tile = lambda row: pltpu.make_async_copy(
            zeros, xs_ref.at[pl.ds(pl.multiple_of(row, ROW_TILE), ROW_TILE)], zsem)
        n_used = offs_ref[N_EXPERTS] // ROW_TILE

        def tail(j, carry, op):
            op(zero_tile(j * ROW_TILE))
            return carry

        for op in (lambda c: c.start(), lambda c: c.wait()):
            for e in range(N_EXPERTS):
                @pl.when(offs_ref[e + 1] > offs_ref[e])
                def _():
                    op(zero_tile(offs_ref[e + 1] - ROW_TILE))
            lax.fori_loop(n_used, n_tiles, functools.partial(tail, op=op), None)

    pos0, pos1 = _local_positions_row(rt_ref, lstart_ref)
    used = chunk_ref[0, N_CHUNK_LANE] * RUN_ALIGN
    body_rows = loc.shape[1] - MXU_DIM

    def sort_rows(h_ref, lo, n):
        slot = lo + lax.broadcasted_iota(jnp.int32, (n, tm), 0)
        perm = jnp.where(slot == pos0, 1.0, jnp.where(slot == pos1, 1.0, 0.0)).astype(BF16)
        loc[half, lo:lo + n, :] = jnp.dot(perm, h_ref[...], preferred_element_type=F32).astype(BF16)

    for h_ref, mine in ((hp_ref, i < n_prompt_tiles), (hs_ref, i >= n_prompt_tiles)):
        @pl.when(mine)
        def _():
            sort_rows(h_ref, 0, body_rows)

        @pl.when(mine & (used > body_rows))
        def _():
            sort_rows(h_ref, body_rows, MXU_DIM)

    def chunk(buf, lo, go):
        return pltpu.make_async_copy(loc.at[buf, pl.ds(lo, RUN_ALIGN)], xs_ref.at[pl.ds(go, RUN_ALIGN)],
                                     sems.at[buf])

    @pl.when(i > 0)
    def _():
        _for_each_chunk(pchunk_ref, lambda lo, go: chunk(1 - half, lo, go).wait())

    _for_each_chunk(chunk_ref, lambda lo, go: chunk(half, lo, go).start())

    @pl.when(i == pl.num_programs(0) - 1)
    def _():
        _for_each_chunk(chunk_ref, lambda lo, go: chunk(half, lo, go).wait())


def _dispatch(h3_prompt, h3_sample, route_t, plan, *, tm):
    lstart, chunks, offs, _ = plan
    n_p, n_s, d = h3_prompt.shape[0], h3_sample.shape[0], h3_prompt.shape[-1]
    assert n_p % tm == 0 and n_s % tm == 0
    n = n_p + n_s
    nt = n // tm
    npt = n_p // tm
    n_rows = _n_row_tiles(n, tm) * ROW_TILE
    smem_tile = lambda: pl.BlockSpec((None, 1, LANES), lambda i, offs: (i, 0, 0), memory_space=pltpu.SMEM)
    smem_prev = lambda: pl.BlockSpec((None, 1, LANES), lambda i, offs: (jnp.maximum(i - 1, 0), 0, 0),
                                     memory_space=pltpu.SMEM)
    return pl.pallas_call(
        functools.partial(_dispatch_body, tm=tm, n_prompt_tiles=npt),
        grid_spec=pltpu.PrefetchScalarGridSpec(
            num_scalar_prefetch=1,
            grid=(nt,),
            in_specs=[smem_tile(), smem_tile(), smem_prev(),
                      pl.BlockSpec((tm, d), lambda i, offs: (jnp.minimum(i, npt - 1), 0)),
                      pl.BlockSpec((tm, d), lambda i, offs: (jnp.maximum(i - npt, 0), 0)),
                      pl.BlockSpec((ROUTE_ROWS, tm), lambda i, offs: (0, i))],
            out_specs=pl.BlockSpec(memory_space=pl.ANY),
            scratch_shapes=[pltpu.VMEM((2, _local_rows(tm), d), BF16), pltpu.VMEM((ROW_TILE, d), BF16),
                            pltpu.SemaphoreType.DMA((2,)), pltpu.SemaphoreType.DMA(())]),
        out_shape=jax.ShapeDtypeStruct((n_rows, d), BF16),
        compiler_params=pltpu.CompilerParams(dimension_semantics=("arbitrary",), vmem_limit_bytes=VMEM_LIMIT),
        name="moe_dispatch",
    )(offs, lstart, chunks, chunks, h3_prompt, h3_sample, route_t)


def _experts_body(te_ref, offs_ref, xs_ref, wg_ref, wu_ref, wd_ref, ys_ref,
                  wg_buf, wu_buf, wd_buf, wg_bf, wu_bf, wd_bf, turn_ref, sems):
    i = pl.program_id(0)
    n_used = offs_ref[N_EXPERTS] // ROW_TILE

    def fetch(expert, half):
        return [pltpu.make_async_copy(src.at[expert], dst.at[half], sems.at[half, k])
                for k, (src, dst) in enumerate(((wg_ref, wg_buf), (wu_ref, wu_buf), (wd_ref, wd_buf)))]

    @pl.when(i == 0)
    def _():
        turn_ref[0] = 0
        for copy in fetch(te_ref[0], 0):
            copy.start()

    @pl.when(i < n_used)
    def _():
        expert = te_ref[i]

        @pl.when((i == 0) | (expert != te_ref[jnp.maximum(i - 1, 0)]))
        def _():
            half = lax.rem(turn_ref[0], 2)
            turn_ref[0] = turn_ref[0] + 1
            for copy in fetch(expert, half):
                copy.wait()
            wg_bf[...] = wg_buf[half].astype(BF16)
            wu_bf[...] = wu_buf[half].astype(BF16)
            wd_bf[...] = wd_buf[half].astype(BF16)
            following = offs_ref[expert + 1] // ROW_TILE

            @pl.when(following < n_used)
            def _():
                for copy in fetch(te_ref[following], 1 - half):
                    copy.start()

        dot = functools.partial(jnp.dot, preferred_element_type=F32)
        for rows in _chains(ROW_TILE):
            x = xs_ref[rows, :]
            gate = dot(x, wg_bf[...])
            up = dot(x, wu_bf[...])
            act = (gate * jax.nn.sigmoid(gate) * up).astype(BF16)
            ys_ref[rows, :] = dot(act, wd_bf[...]).astype(BF16)

    @pl.when(i >= n_used)
    def _():
        ys_ref[...] = jnp.zeros(ys_ref.shape, ys_ref.dtype)


def _experts(xs, te, offs, w_gate, w_up, w_down):
    n_rows, d = xs.shape
    last = lambda i, te, offs: jnp.minimum(i, offs[N_EXPERTS] // ROW_TILE - 1)
    hbm = pl.BlockSpec(memory_space=pl.ANY)
    return pl.pallas_call(
        _experts_body,
        grid_spec=pltpu.PrefetchScalarGridSpec(
            num_scalar_prefetch=2,
            grid=(n_rows // ROW_TILE,),
            in_specs=[pl.BlockSpec((ROW_TILE, d), lambda i, te, offs: (last(i, te, offs), 0)), hbm, hbm, hbm],
            out_specs=pl.BlockSpec((ROW_TILE, d), lambda i, te, offs: (i, 0)),
            scratch_shapes=[pltpu.VMEM((2, d, D_EXPERT), F32), pltpu.VMEM((2, d, D_EXPERT), F32),
                            pltpu.VMEM((2, D_EXPERT, d), F32),
                            pltpu.VMEM((d, D_EXPERT), BF16), pltpu.VMEM((d, D_EXPERT), BF16),
                            pltpu.VMEM((D_EXPERT, d), BF16),
                            pltpu.SMEM((1,), jnp.int32), pltpu.SemaphoreType.DMA((2, 3))]),
        out_shape=jax.ShapeDtypeStruct((n_rows, d), BF16),
        compiler_params=pltpu.CompilerParams(dimension_semantics=("arbitrary",), vmem_limit_bytes=VMEM_LIMIT),
        name="moe_experts",
    )(te, offs, xs, w_gate, w_up, w_down)


def _combine_body(chunk_ref, nchunk_ref, x2p_ref, x2s_ref, route_ref, lsv_ref, ys_ref, g_ref,
                  yp_ref, ysm_ref, loc, sems, *, tm, n_prompt_tiles):
    i = pl.program_id(0)
    half = lax.rem(i, 2)

    def chunk(buf, lo, go):
        return pltpu.make_async_copy(ys_ref.at[pl.ds(go, RUN_ALIGN)], loc.at[buf, pl.ds(lo, RUN_ALIGN)],
                                     sems.at[buf])

    @pl.when(i == 0)
    def _():
        loc[...] = jnp.zeros(loc.shape, loc.dtype)
        _for_each_chunk(chunk_ref, lambda lo, go: chunk(0, lo, go).start())

    @pl.when(i + 1 < pl.num_programs(0))
    def _():
        _for_each_chunk(nchunk_ref, lambda lo, go: chunk(1 - half, lo, go).start())

    _for_each_chunk(chunk_ref, lambda lo, go: chunk(half, lo, go).wait())

    lane = lax.broadcasted_iota(jnp.int32, (tm, LANES), 1)
    picks = []
    for r_eid, r_rank, r_gate in ((R_EID0, R_RANK0, R_GATE0), (R_EID1, R_RANK1, R_GATE1)):
        eid = route_ref[:, r_eid:r_eid + 1].astype(jnp.int32)
        start = jnp.sum(jnp.where(lane == eid, lsv_ref[...], 0), axis=-1, keepdims=True)
        picks.append((route_ref[:, r_rank:r_rank + 1].astype(jnp.int32) + start, route_ref[:, r_gate:r_gate + 1]))

    slot = lax.broadcasted_iota(jnp.int32, (tm, loc.shape[1]), 1)
    weights = jnp.zeros(slot.shape, F32)
    for pos, gate in picks:
        weights = jnp.where(slot == pos, gate, weights)
    moe = jnp.dot(weights.astype(BF16), loc[half], preferred_element_type=F32)

    @pl.when(i < n_prompt_tiles)
    def _():
        yp_ref[...] = _rms(x2p_ref[...] + moe, g_ref[...])

    @pl.when(i >= n_prompt_tiles)
    def _():
        ysm_ref[...] = _rms(x2s_ref[...] + moe, g_ref[...])


def _combine(x2_prompt, x2_sample, route, plan, ys, g_final, *, tm):
    lstart, chunks, _, _ = plan
    (n_p, d), n_s = x2_prompt.shape, x2_sample.shape[0]
    assert n_p % tm == 0 and n_s % tm == 0
    npt = n_p // tm
    nt = npt + n_s // tm
    smem_tile = lambda: pl.BlockSpec((None, 1, LANES), lambda i: (i, 0, 0), memory_space=pltpu.SMEM)
    smem_next = lambda: pl.BlockSpec((None, 1, LANES), lambda i: (jnp.minimum(i + 1, nt - 1), 0, 0),
                                     memory_space=pltpu.SMEM)
    prompt_tile = lambda: pl.BlockSpec((tm, d), lambda i: (jnp.minimum(i, npt - 1), 0))
    sample_tile = lambda: pl.BlockSpec((tm, d), lambda i: (jnp.maximum(i - npt, 0), 0))
    return pl.pallas_call(
        functools.partial(_combine_body, tm=tm, n_prompt_tiles=npt),
        grid=(nt,),
        in_specs=[smem_tile(), smem_next(), prompt_tile(), sample_tile(),
                  pl.BlockSpec((tm, ROUTE_ROWS), lambda i: (i, 0)),
                  pl.BlockSpec((None, 1, LANES), lambda i: (i, 0, 0)),
                  pl.BlockSpec(memory_space=pl.ANY),
                  pl.BlockSpec((1, d), lambda i: (0, 0))],
        out_specs=[prompt_tile(), sample_tile()],
        out_shape=[jax.ShapeDtypeStruct((n_p, d), F32), jax.ShapeDtypeStruct((n_s, d), F32)],
        scratch_shapes=[pltpu.VMEM((2, _local_rows(tm), d), BF16), pltpu.SemaphoreType.DMA((2,))],
        compiler_params=pltpu.CompilerParams(dimension_semantics=("arbitrary",), vmem_limit_bytes=VMEM_LIMIT),
        name="moe_combine",
    )(chunks, chunks, x2_prompt, x2_sample, route, lstart, ys, g_final)


TOKEN_TILE = 512
FOX_Q_TILE = 256
FOX_CACHE_TILE = 1024


def kernel(x_prompt, x_sample, cache_fox_k, cache_fox_v, cache_fox_logf, cache_band_k, cache_band_v, cache_mem_k, cache_mem_v, mem_prompt, g_mix, w_in, b_forget, g_out_fox, g_out_band, rel_table, w_out, g_cross, g_mem, w_cq, w_ck, w_cv, w_co, g_ffn, w_router1, b_router1, w_router2, b_router2, w_exp_gate, w_exp_up, w_exp_down, g_final):
    assert g_mix.shape[0] == 1, "single-layer model"
    bsz, seq, d = x_prompt.shape
    sb, st, _ = x_sample.shape
    n_s = sb * st
    n_mem = mem_prompt.shape[1]
    row = lambda g: g.reshape(1, -1)

    w_pad, bf_pad, g_mix_r = _prep_proj(w_in[0], b_forget[0], g_mix[0])
    g_of, g_ob = row(g_out_fox[0]), row(g_out_band[0])
    bias_t, bias_s = _band_bias(_prep_band_bias_row(rel_table[0]))

    qx, kx, vat, qxb, kxb, vbt, kaf, vaf, kbf, vbf, logf = _proj(
        x_prompt, g_mix_r, w_pad, bf_pad, tm=TOKEN_TILE, prompt=True)
    a_p = _fox_prompt(qx, kx, vat, g_of, tq=FOX_Q_TILE)
    b_p = _band_prompt(qxb, kxb, vbt, bias_t, g_ob)

    s_out = _proj(x_sample.reshape(1, n_s, d), g_mix_r, w_pad, bf_pad, tm=n_s, prompt=False)
    sqa, ska, sva, sqb, skb, svb, skaf, svaf, skbf, svbf = (t.reshape(sb, st, W_GROUP) for t in s_out[:10])
    slogf = s_out[10].reshape(sb, st, N_HEADS)
    slft = s_out[11].reshape(N_HEADS, sb, st).transpose(1, 0, 2)
    a_s = _fox_sample(sqa, ska, sva, slft, cache_fox_k[0], cache_fox_v[0],
                      cache_fox_logf[0].transpose(0, 2, 1), g_of, pt=FOX_CACHE_TILE)
    bp = cache_band_k.shape[2]
    b_s, nbk, nbv = _band_sample(sqb, skb, svb, skbf, svbf,
                                 cache_band_k[0].reshape(sb, bp, W_GROUP), cache_band_v[0].reshape(sb, bp, W_GROUP),
                                 bias_s, g_ob)

    w_ckv = jnp.concatenate([w_ck[0], w_cv[0]], axis=1).astype(BF16)
    mkf, mvf, mk, mv = _mem_kv(mem_prompt, row(g_mem[0]), w_ckv)
    post_w = _prep_post(w_out[0], g_cross[0], w_cq[0], w_co[0], g_ffn[0],
                        w_router1[0], b_router1[0], w_router2[0], b_router2[0])
    x2_p, h3_p, route_p, routet_p, cnt_p = _post_block(x_prompt, a_p, b_p, mk, mv, post_w, tm=TOKEN_TILE)
    cmk = cache_mem_k[0].reshape(sb, n_mem, W_MEM).astype(BF16)
    cmv = cache_mem_v[0].reshape(sb, n_mem, W_MEM).astype(BF16)
    x2_s, h3_s, route_s, routet_s, cnt_s = _post_block(x_sample, a_s, b_s, cmk, cmv, post_w, tm=TOKEN_TILE)
    route = jnp.concatenate([route_p, route_s], axis=0)
    route_t = jnp.concatenate([routet_p, routet_s], axis=1)

    plan = _plan(jnp.concatenate([cnt_p, cnt_s], axis=0), TOKEN_TILE)
    xs = _dispatch(h3_p, h3_s, route_t, plan, tm=TOKEN_TILE)
    ys = _experts(xs, plan[3], plan[2], w_exp_gate[0], w_exp_up[0], w_exp_down[0])
    y_p, y_s = _combine(x2_p, x2_s, route, plan, ys, row(g_final), tm=TOKEN_TILE)

    heads = lambda t, n: t.reshape(1, n, -1, N_HEADS, HEAD_DIM)
    mem_heads = lambda t: t.reshape(1, bsz, n_mem, N_HEADS_MEM, HEAD_DIM_MEM)
    return (y_p.reshape(bsz, seq, d), y_s.reshape(sb, st, d),
            heads(kaf, bsz), heads(vaf, bsz), logf.reshape(1, bsz, seq, N_HEADS),
            heads(kbf, bsz), heads(vbf, bsz), mem_heads(mkf), mem_heads(mvf),
            heads(skaf, sb), heads(svaf, sb), slogf.reshape(1, sb, st, N_HEADS),
            heads(nbk, sb), heads(nbv, sb))
```

```python
import functools

import jax
import jax.numpy as jnp
from jax import lax
from jax.experimental import pallas as pl
from jax.experimental.pallas import tpu as pltpu

F32 = jnp.float32
BF16 = jnp.bfloat16

D_MODEL = 1024
HEAD_DIM = 64
N_HEADS = 8
W_GROUP = N_HEADS * HEAD_DIM
N_PAIRS = N_HEADS // 2
CHUNK = 64
LEFT_CHUNKS = 8
LEFT = LEFT_CHUNKS * CHUNK
REL_CLIP = 128
EPS = 1e-6
NEG_INF = -1e30
ATTN_SCALE = HEAD_DIM ** -0.5
LANES = 128
PROJ_PAD = 3 * W_GROUP * 2 + LANES
VMEM_LIMIT = 56 * 1024 * 1024


def _rms(x, g):
    ms = jnp.mean(x * x, axis=-1, keepdims=True)
    return x * lax.rsqrt(ms + EPS) * g


def _log_sigmoid(x):
    return -(jnp.maximum(-x, 0.0) + jnp.log1p(jnp.exp(-jnp.abs(x))))


def _lane_cumsum(x):
    n = x.shape[-1]
    lane = lax.broadcasted_iota(jnp.int32, x.shape, 1)
    k = 1
    while k < n:
        x = x + jnp.where(lane >= k, pltpu.roll(x, k, axis=1), 0.0)
        k *= 2
    return x


LOG2E = 1.4426950408889634
SCALE_BASE2 = ATTN_SCALE * LOG2E


def _split3(x):
    hi = x.astype(BF16).astype(F32)
    mid = (x - hi).astype(BF16).astype(F32)
    lo = x - hi - mid
    return hi, mid, lo


def _extra_lane(parity):
    return HEAD_DIM if parity == 0 else 0


def _fox_extras(c3t, hp, tm):
    row = lax.broadcasted_iota(jnp.int32, (8, tm), 0)

    def group(h, q_side):
        hi, mid, lo = (p[h:h + 1, :] for p in c3t)
        if q_side:
            return jnp.where(row < 3, 1.0, jnp.where(row == 3, hi, jnp.where(row == 4, mid, jnp.where(row == 5, lo, 0.0))))
        return jnp.where(row == 0, -hi, jnp.where(row == 1, -mid, jnp.where(row == 2, -lo, jnp.where(row < 6, 1.0, 0.0))))

    gap = jnp.zeros((HEAD_DIM - 8, tm), F32)
    sides = []
    for q_side in (True, False):
        t = jnp.concatenate([group(2 * hp + 1, q_side), gap, group(2 * hp, q_side), gap], axis=0)
        sides.append(t.T)
    return sides


def _head_blocks(x128, extras, lane):
    return (jnp.where(lane < HEAD_DIM, x128, extras).astype(BF16),
            jnp.where(lane >= HEAD_DIM, x128, extras).astype(BF16))


Q_A, K_A, V_A, Q_B, K_B, V_B = range(6)


PROJ_CHAIN = 256


def _chains(tm, chain=PROJ_CHAIN):
    n = max(tm // chain, 1)
    return [pl.ds(i * (tm // n), tm // n) for i in range(n)]


def _proj_common(rows, x_ref, g_ref, w_ref, bf_ref, kaf_ref, vaf_ref, kbf_ref, vbf_ref, logf_ref, keep_tiles):
    s = pl.program_id(1)
    ns = pl.num_programs(1)
    h = _rms(x_ref[rows, :], g_ref[...]).astype(BF16)
    w = W_GROUP
    zf = jnp.dot(h, w_ref[:, 6 * w:6 * w + LANES], preferred_element_type=F32)
    z = [jnp.dot(h, w_ref[:, g * w:(g + 1) * w], preferred_element_type=F32) for g in range(6)]
    kaf_ref[rows, :] = z[K_A]
    vaf_ref[rows, :] = z[V_A]

    @pl.when(s >= ns - keep_tiles)
    def _():
        kbf_ref[rows, :] = z[K_B]
        vbf_ref[rows, :] = z[V_B]

    logf = _log_sigmoid(zf + bf_ref[...])
    logf_ref[rows, :] = logf[:, :N_HEADS]
    return z, logf


def _proj_prompt_body(x_ref, g_ref, w_ref, bf_ref, qx_ref, kx_ref, vat_ref, qxb_ref, kxb_ref, vbt_ref,
                      kaf_ref, vaf_ref, kbf_ref, vbf_ref, logf_ref, carry_ref, *, tm, keep_tiles):
    @pl.when(pl.program_id(1) == 0)
    def _():
        carry_ref[...] = jnp.zeros_like(carry_ref)

    for rows in _chains(tm):
        n = rows.size
        z, logf = _proj_common(rows, x_ref, g_ref, w_ref, bf_ref, kaf_ref, vaf_ref, kbf_ref, vbf_ref, logf_ref,
                               keep_tiles)
        vat_ref[:, rows] = z[V_A].T.astype(BF16)
        vbt_ref[:, rows] = z[V_B].T.astype(BF16)
        ct = _lane_cumsum(logf.T[:N_HEADS, :]) + carry_ref[:, 0:1]
        carry_ref[...] = jnp.broadcast_to(ct[:, n - 1:n], carry_ref.shape)
        c3t = _split3(ct * LOG2E)
        lane = lax.broadcasted_iota(jnp.int32, (n, LANES), 1)
        band_q_extras = jnp.where((lane == _extra_lane(0)) | (lane == _extra_lane(1)), 1.0, 0.0)
        band_k_extras = jnp.zeros((n, LANES), F32)
        for hp in range(N_PAIRS):
            blocks = slice(2 * hp * LANES, 2 * (hp + 1) * LANES)
            blk = lambda group, hp=hp, z=z: z[group][:, hp * LANES:(hp + 1) * LANES]
            q_extras, k_extras = _fox_extras(c3t, hp, n)
            qx_ref[rows, blocks] = jnp.concatenate(_head_blocks(blk(Q_A) * SCALE_BASE2, q_extras, lane), axis=1)
            kx_ref[rows, blocks] = jnp.concatenate(_head_blocks(blk(K_A), k_extras, lane), axis=1)
            qxb_ref[rows, blocks] = jnp.concatenate(_head_blocks(blk(Q_B) * SCALE_BASE2, band_q_extras, lane), axis=1)
            kxb_ref[rows, blocks] = jnp.concatenate(_head_blocks(blk(K_B), band_k_extras, lane), axis=1)


def _proj_sample_body(x_ref, g_ref, w_ref, bf_ref, qa_ref, ka_ref, va_ref, qb_ref, kb_ref, vb_ref,
                      kaf_ref, vaf_ref, kbf_ref, vbf_ref, logf_ref, lt_ref, *, tm, keep_tiles):
    for rows in _chains(tm):
        z, logf = _proj_common(rows, x_ref, g_ref, w_ref, bf_ref, kaf_ref, vaf_ref, kbf_ref, vbf_ref, logf_ref,
                               keep_tiles)
        qa_ref[rows, :] = (z[Q_A] * ATTN_SCALE).astype(BF16)
        ka_ref[rows, :] = z[K_A].astype(BF16)
        va_ref[rows, :] = z[V_A].astype(BF16)
        qb_ref[rows, :] = (z[Q_B] * ATTN_SCALE).astype(BF16)
        kb_ref[rows, :] = z[K_B].astype(BF16)
        vb_ref[rows, :] = z[V_B].astype(BF16)
        lt_ref[:, rows] = logf.T[:N_HEADS, :]


def _proj(x, g_mix, w_pad, bf_pad, *, tm, prompt):
    b, s, d = x.shape
    ns = s // tm
    keep = min(LEFT, s)
    assert s % tm == 0 and keep % tm == 0
    keep_tiles = keep // tm
    row = pl.BlockSpec((None, tm, W_GROUP), lambda i, j: (i, j, 0))
    wide = pl.BlockSpec((None, tm, N_HEADS * LANES), lambda i, j: (i, j, 0))
    col = pl.BlockSpec((None, W_GROUP, tm), lambda i, j: (i, 0, j))
    keep_spec = pl.BlockSpec((None, tm, W_GROUP), lambda i, j: (i, jnp.maximum(j - (ns - keep_tiles), 0), 0))
    heads_row = pl.BlockSpec((None, tm, N_HEADS), lambda i, j: (i, j, 0))
    heads_col = pl.BlockSpec((None, N_HEADS, tm), lambda i, j: (i, 0, j))
    const = lambda shape: pl.BlockSpec(shape, lambda i, j: (0,) * len(shape))
    rows_bf = jax.ShapeDtypeStruct((b, s, W_GROUP), BF16)
    wide_bf = jax.ShapeDtypeStruct((b, s, N_HEADS * LANES), BF16)
    cols_bf = jax.ShapeDtypeStruct((b, W_GROUP, s), BF16)
    f32_tail = [jax.ShapeDtypeStruct((b, s, W_GROUP), F32)] * 2
    f32_tail += [jax.ShapeDtypeStruct((b, keep, W_GROUP), F32)] * 2
    f32_tail += [jax.ShapeDtypeStruct((b, s, N_HEADS), F32)]
    tail_specs = [row, row, keep_spec, keep_spec, heads_row]
    if prompt:
        body = functools.partial(_proj_prompt_body, tm=tm, keep_tiles=keep_tiles)
        out_shape = [wide_bf, wide_bf, cols_bf, wide_bf, wide_bf, cols_bf] + f32_tail
        out_specs = [wide, wide, col, wide, wide, col] + tail_specs
        scratch = [pltpu.VMEM((N_HEADS, LANES), F32)]
    else:
        body = functools.partial(_proj_sample_body, tm=tm, keep_tiles=keep_tiles)
        out_shape = [rows_bf] * 6 + f32_tail + [jax.ShapeDtypeStruct((b, N_HEADS, s), F32)]
        out_specs = [row] * 6 + tail_specs + [heads_col]
        scratch = []
    return pl.pallas_call(
        body,
        grid=(b, ns),
        in_specs=[pl.BlockSpec((None, tm, d), lambda i, j: (i, j, 0)),
                  const((1, d)), const(w_pad.shape), const((1, LANES))],
        out_specs=out_specs,
        out_shape=out_shape,
        scratch_shapes=scratch,
        compiler_params=pltpu.CompilerParams(
            dimension_semantics=("parallel", "arbitrary"), vmem_limit_bytes=VMEM_LIMIT),
        name="proj",
    )(x, g_mix, w_pad, bf_pad)


def _prep_proj(w_in, b_forget, g_mix):
    cols = w_in.shape[-1]
    w_pad = jnp.pad(w_in, ((0, 0), (0, PROJ_PAD - cols))).astype(BF16)
    bf_pad = jnp.pad(b_forget.reshape(1, -1), ((0, 0), (0, LANES - N_HEADS))).astype(F32)
    return w_pad, bf_pad, g_mix.reshape(1, -1)


def _pair_masks():
    lane = lax.broadcasted_iota(jnp.int32, (1, LANES), 1)
    return lane < HEAD_DIM


def _head_q(q128, even_lanes, parity):
    keep = even_lanes if parity == 0 else jnp.logical_not(even_lanes)
    return jnp.where(keep, q128, jnp.zeros_like(q128))


def _head_v(v128, even_lanes, parity):
    keep = even_lanes if parity == 0 else jnp.logical_not(even_lanes)
    return jnp.where(keep, v128, jnp.ones_like(v128))


def _head_out(acc_even, acc_odd, even_lanes):
    inv_e = 1.0 / acc_even[:, HEAD_DIM:HEAD_DIM + 1]
    inv_o = 1.0 / acc_odd[:, 0:1]
    return jnp.where(even_lanes, acc_even * inv_e, acc_odd * inv_o)


_NT = (((1,), (1,)), ((), ()))


def _pair_rows():
    row = lax.broadcasted_iota(jnp.int32, (LANES, 1), 0)
    return row < HEAD_DIM


def _head_vt(vt128, even_rows, parity):
    keep = even_rows if parity == 0 else jnp.logical_not(even_rows)
    return jnp.where(keep, vt128, jnp.ones_like(vt128))


def _head_out_t(acc_even, acc_odd, even_rows):
    inv_e = 1.0 / acc_even[HEAD_DIM:HEAD_DIM + 1, :]
    inv_o = 1.0 / acc_odd[0:1, :]
    return jnp.where(even_rows, acc_even * inv_e, acc_odd * inv_o)


def _fox_body(qx_ref, kx_ref, vt_ref, g_ref, o_ref, s_scr, p_scr, m_scr, alpha_scr, acc_scr, ot_scr, *, tq):
    qi = pl.program_id(1)
    even_rows = _pair_rows()
    m_scr[...] = jnp.full(m_scr.shape, NEG_INF, F32)
    acc_scr[...] = jnp.zeros(acc_scr.shape, F32)
    key = lax.broadcasted_iota(jnp.int32, (tq, tq), 0)
    qry = lax.broadcasted_iota(jnp.int32, (tq, tq), 1)
    causal = key <= qry

    def logits(j, half):
        start = pl.multiple_of(j * tq, tq)
        for h in range(N_HEADS):
            head = slice(h * LANES, (h + 1) * LANES)
            s_scr[half, h] = lax.dot_general(kx_ref[pl.ds(start, tq), head], qx_ref[:, head], _NT,
                                             preferred_element_type=F32)

    def weigh(j, half, masked):
        start = pl.multiple_of(j * tq, tq)
        for h in range(N_HEADS):
            st = s_scr[half, h]
            if masked:
                st = jnp.where(causal, st, NEG_INF)
            m_old = m_scr[h:h + 1, :]
            m_new = jnp.maximum(m_old, jnp.max(st, axis=0, keepdims=True))
            p_scr[h] = jnp.exp2(st - m_new).astype(BF16)
            alpha_scr[h:h + 1, :] = jnp.exp2(m_old - m_new)
            m_scr[h:h + 1, :] = m_new
        for h in range(N_HEADS):
            pair = slice((h // 2) * LANES, (h // 2 + 1) * LANES)
            vt = _head_vt(vt_ref[pair, pl.ds(start, tq)], even_rows, h % 2)
            acc_scr[h] = acc_scr[h] * alpha_scr[h:h + 1, :] + jnp.dot(vt, p_scr[h], preferred_element_type=F32)

    def step(j, cur, nxt):
        logits(j + 1, nxt)
        weigh(j, cur, False)

    def body(jj, carry):
        step(2 * jj, 0, 1)
        step(2 * jj + 1, 1, 0)
        return carry

    logits(0, 0)
    lax.fori_loop(0, qi // 2, body, None)
    odd = lax.rem(qi, 2) == 1

    @pl.when(odd)
    def _():
        step(qi - 1, 0, 1)
        weigh(qi, 1, True)

    @pl.when(jnp.logical_not(odd))
    def _():
        weigh(qi, 0, True)
    for hp in range(N_PAIRS):
        ot_scr[hp * LANES:(hp + 1) * LANES, :] = _head_out_t(acc_scr[2 * hp], acc_scr[2 * hp + 1], even_rows)
    o_ref[...] = _rms(ot_scr[...].T, g_ref[...]).astype(BF16)


def _fox_prompt(qx, kx, vat, g_out, *, tq):
    b, s, wx = qx.shape
    w = vat.shape[1]
    return pl.pallas_call(
        functools.partial(_fox_body, tq=tq),
        grid=(b, s // tq),
        in_specs=[pl.BlockSpec((None, tq, wx), lambda i, j: (i, j, 0)),
                  pl.BlockSpec((None, s, wx), lambda i, j: (i, 0, 0)),
                  pl.BlockSpec((None, w, s), lambda i, j: (i, 0, 0)),
                  pl.BlockSpec((1, w), lambda i, j: (0, 0))],
        out_specs=pl.BlockSpec((None, tq, w), lambda i, j: (i, j, 0)),
        out_shape=jax.ShapeDtypeStruct((b, s, w), BF16),
        scratch_shapes=[pltpu.VMEM((2, N_HEADS, tq, tq), F32), pltpu.VMEM((N_HEADS, tq, tq), BF16),
                        pltpu.VMEM((N_HEADS, tq), F32), pltpu.VMEM((N_HEADS, tq), F32),
                        pltpu.VMEM((N_HEADS, LANES, tq), F32), pltpu.VMEM((w, tq), F32)],
        compiler_params=pltpu.CompilerParams(
            dimension_semantics=("parallel", "arbitrary"), vmem_limit_bytes=VMEM_LIMIT),
        name="fox_prompt",
    )(qx, kx, vat, g_out)


BAND_CHUNKS = 4
BAND_Q = BAND_CHUNKS * CHUNK
BAND_K = (LEFT_CHUNKS + BAND_CHUNKS) * CHUNK
BIAS_ROW = BAND_K + BAND_Q
BAND_K_SAMPLE = (LEFT + CHUNK + LANES - 1) // LANES * LANES


def _prep_band_bias_row(rel_table):
    pivot = LEFT + BAND_Q
    n_hi = pivot - REL_CLIP + 1
    n_mid = min(2 * REL_CLIP, BIAS_ROW - n_hi)
    n_lo = BIAS_ROW - n_hi - n_mid
    parts = [jnp.broadcast_to(rel_table[2 * REL_CLIP:], (n_hi, N_HEADS)),
             rel_table[2 * REL_CLIP - 1::-1][:n_mid],
             jnp.broadcast_to(rel_table[:1], (n_lo, N_HEADS))]
    return jnp.concatenate(parts, axis=0).T.reshape(N_HEADS, 1, BIAS_ROW)


def _band_bias_body(row_ref, bt_ref, bs_ref):
    rows = jnp.broadcast_to(row_ref[...], (BAND_Q, BIAS_ROW))
    skew = pltpu.roll(rows, 0, axis=1, stride=1, stride_axis=0)
    bias = skew[:, BAND_Q:]
    bs_ref[...] = bias[:CHUNK, :BAND_K_SAMPLE]
    qc = lax.broadcasted_iota(jnp.int32, (BAND_Q, BAND_K), 0) // CHUNK
    kc = lax.broadcasted_iota(jnp.int32, (BAND_Q, BAND_K), 1) // CHUNK
    bt_ref[...] = jnp.where((kc >= qc) & (kc <= qc + LEFT_CHUNKS), bias * LOG2E, NEG_INF).T


def _band_bias(bias_row):
    return pl.pallas_call(
        _band_bias_body,
        grid=(N_HEADS,),
        in_specs=[pl.BlockSpec((None, 1, BIAS_ROW), lambda h: (h, 0, 0))],
        out_specs=[pl.BlockSpec((None, BAND_K, BAND_Q), lambda h: (h, 0, 0)),
                   pl.BlockSpec((None, CHUNK, BAND_K_SAMPLE), lambda h: (h, 0, 0))],
        out_shape=[jax.ShapeDtypeStruct((N_HEADS, BAND_K, BAND_Q), F32),
                   jax.ShapeDtypeStruct((N_HEADS, CHUNK, BAND_K_SAMPLE), F32)],
        name="band_bias",
    )(bias_row)


def _band_body(qx_ref, kx_ref, vt_ref, bias_ref, g_ref, o_ref, kpad, vtpad, s_scr, p_scr, ot_scr, *, s_len):
    step = pl.program_id(1)

    @pl.when(step == 0)
    def _():
        lane = lax.broadcasted_iota(jnp.int32, (LEFT, 2 * LANES), 1)
        flags = (lane == _extra_lane(0)) | (lane == LANES + _extra_lane(1))
        pad_pair = jnp.where(flags, NEG_INF, 0.0).astype(BF16)
        for hp in range(N_PAIRS):
            kpad[0:LEFT, 2 * hp * LANES:2 * (hp + 1) * LANES] = pad_pair
        vtpad[:, 0:LEFT] = jnp.zeros((W_GROUP, LEFT), BF16)
        kpad[LEFT:LEFT + s_len, :] = kx_ref[...]
        vtpad[:, LEFT:LEFT + s_len] = vt_ref[...]

    even_rows = _pair_rows()
    start = pl.multiple_of(step * BAND_Q, BAND_Q)
    for h in range(N_HEADS):
        head = slice(h * LANES, (h + 1) * LANES)
        s_scr[h] = lax.dot_general(kpad[pl.ds(start, BAND_K), head], qx_ref[:, head], _NT,
                                   preferred_element_type=F32)
    for h in range(N_HEADS):
        st = s_scr[h] + bias_ref[h]
        p_scr[h] = jnp.exp2(st - jnp.max(st, axis=0, keepdims=True)).astype(BF16)
    for hp in range(N_PAIRS):
        pair = slice(hp * LANES, (hp + 1) * LANES)
        vtwin = vtpad[pair, pl.ds(start, BAND_K)]
        accs = [jnp.dot(_head_vt(vtwin, even_rows, parity), p_scr[2 * hp + parity], preferred_element_type=F32)
                for parity in range(2)]
        ot_scr[pair, :] = _head_out_t(accs[0], accs[1], even_rows)
    o_ref[...] = _rms(ot_scr[...].T, g_ref[...]).astype(BF16)


def _band_prompt(qxb, kxb, vbt, bias_t, g_out):
    b, s, wx = qxb.shape
    w = vbt.shape[1]
    return pl.pallas_call(
        functools.partial(_band_body, s_len=s),
        grid=(b, s // BAND_Q),
        in_specs=[pl.BlockSpec((None, BAND_Q, wx), lambda i, j: (i, j, 0)),
                  pl.BlockSpec((None, s, wx), lambda i, j: (i, 0, 0)),
                  pl.BlockSpec((None, w, s), lambda i, j: (i, 0, 0)),
                  pl.BlockSpec(bias_t.shape, lambda i, j: (0, 0, 0)),
                  pl.BlockSpec((1, w), lambda i, j: (0, 0))],
        out_specs=pl.BlockSpec((None, BAND_Q, w), lambda i, j: (i, j, 0)),
        out_shape=jax.ShapeDtypeStruct((b, s, w), BF16),
        scratch_shapes=[pltpu.VMEM((LEFT + s, wx), BF16), pltpu.VMEM((w, LEFT + s), BF16),
                        pltpu.VMEM((N_HEADS, BAND_K, BAND_Q), F32), pltpu.VMEM((N_HEADS, BAND_K, BAND_Q), BF16),
                        pltpu.VMEM((w, BAND_Q), F32)],
        compiler_params=pltpu.CompilerParams(
            dimension_semantics=("parallel", "arbitrary"), vmem_limit_bytes=VMEM_LIMIT),
        name="band_prompt",
    )(qxb, kxb, vbt, bias_t, g_out)


def _row_to_col(row):
    n = row.shape[-1]
    r = lax.broadcasted_iota(jnp.int32, (n, n), 0)
    c = lax.broadcasted_iota(jnp.int32, (n, n), 1)
    return jnp.sum(jnp.where(r == c, jnp.broadcast_to(row, (n, n)), 0.0), axis=-1, keepdims=True)


def _fox_sample_body(q_ref, kn_ref, vn_ref, lft_ref, kc_ref, vc_ref, clft_ref, g_ref, o_ref,
                     cct_scr, cn_scr, m_scr, acc_scr, o_scr, s_scr, p_scr, alpha_scr, *, t_new, pt):
    p_idx = pl.program_id(1)
    n_p = pl.num_programs(1)
    even = _pair_masks()

    @pl.when(p_idx == 0)
    def _():
        cct = _lane_cumsum(clft_ref[...])
        cct_scr[...] = cct
        cn_scr[...] = _lane_cumsum(lft_ref[...]) + cct[:, cct.shape[1] - 1:]
        m_scr[...] = jnp.full(m_scr.shape, NEG_INF, F32)
        acc_scr[...] = jnp.zeros(acc_scr.shape, F32)

    def update(h, s, v):
        m = m_scr[h]
        m_new = jnp.maximum(m, jnp.max(s, axis=-1, keepdims=True))
        p = jnp.exp(s - m_new).astype(BF16)
        acc_scr[h] = acc_scr[h] * jnp.exp(m - m_new) + jnp.dot(p, v, preferred_element_type=F32)
        m_scr[h] = m_new

    start = pl.multiple_of(p_idx * pt, pt)
    pair = lambda h: slice((h // 2) * LANES, (h // 2 + 1) * LANES)
    for h in range(N_HEADS):
        s_scr[h] = lax.dot_general(_head_q(q_ref[:, pair(h)], even, h % 2), kc_ref[:, pair(h)].astype(BF16),
                                   _NT, preferred_element_type=F32)
    for h in range(N_HEADS):
        cq = _row_to_col(cn_scr[h:h + 1, :])
        s = s_scr[h] + cq - cct_scr[h:h + 1, pl.ds(start, pt)]
        m = m_scr[h]
        m_new = jnp.maximum(m, jnp.max(s, axis=-1, keepdims=True))
        p_scr[h] = jnp.exp(s - m_new).astype(BF16)
        alpha_scr[h] = jnp.exp(m - m_new)
        m_scr[h] = m_new
    for h in range(N_HEADS):
        v = _head_v(vc_ref[:, pair(h)].astype(BF16), even, h % 2)
        acc_scr[h] = acc_scr[h] * alpha_scr[h] + jnp.dot(p_scr[h], v, preferred_element_type=F32)

    @pl.when(p_idx == n_p - 1)
    def _():
        row = lax.broadcasted_iota(jnp.int32, (t_new, t_new), 0)
        col = lax.broadcasted_iota(jnp.int32, (t_new, t_new), 1)
        for hp in range(N_PAIRS):
            lanes = slice(hp * LANES, (hp + 1) * LANES)
            q128 = q_ref[:, lanes]
            kn = kn_ref[:, lanes]
            vn = vn_ref[:, lanes]
            for parity in range(2):
                h = 2 * hp + parity
                cn_row = cn_scr[h:h + 1, :]
                s = lax.dot_general(_head_q(q128, even, parity), kn, _NT, preferred_element_type=F32)
                s = jnp.where(col <= row, s + _row_to_col(cn_row) - cn_row, NEG_INF)
                update(h, s, _head_v(vn, even, parity))
            o_scr[:, lanes] = _head_out(acc_scr[2 * hp], acc_scr[2 * hp + 1], even)
        o_ref[...] = _rms(o_scr[...], g_ref[...]).astype(BF16)


def _fox_sample(q, kn, vn, lft, kc, vc, clft, g_out, *, pt):
    b, t, w = q.shape
    p_len = kc.shape[1]
    new = lambda: pl.BlockSpec((None, t, w), lambda i, j: (i, 0, 0))
    cache = lambda: pl.BlockSpec((None, pt, w), lambda i, j: (i, j, 0))
    return pl.pallas_call(
        functools.partial(_fox_sample_body, t_new=t, pt=pt),
        grid=(b, p_len // pt),
        in_specs=[new(), new(), new(),
                  pl.BlockSpec((None, N_HEADS, t), lambda i, j: (i, 0, 0)),
                  cache(), cache(),
                  pl.BlockSpec((None, N_HEADS, p_len), lambda i, j: (i, 0, 0)),
                  pl.BlockSpec((1, w), lambda i, j: (0, 0))],
        out_specs=new(),
        out_shape=jax.ShapeDtypeStruct((b, t, w), BF16),
        scratch_shapes=[pltpu.VMEM((N_HEADS, p_len), F32), pltpu.VMEM((N_HEADS, t), F32),
                        pltpu.VMEM((N_HEADS, t, 1), F32), pltpu.VMEM((N_HEADS, t, LANES), F32),
                        pltpu.VMEM((t, w), F32), pltpu.VMEM((N_HEADS, t, pt), F32),
                        pltpu.VMEM((N_HEADS, t, pt), BF16), pltpu.VMEM((N_HEADS, t, 1), F32)],
        compiler_params=pltpu.CompilerParams(
            dimension_semantics=("parallel", "arbitrary"), vmem_limit_bytes=VMEM_LIMIT),
        name="fox_sample",
    )(q, kn, vn, lft, kc, vc, clft, g_out)


def _band_sample_body(q_ref, kn_ref, vn_ref, knf_ref, vnf_ref, kc_ref, vc_ref, bias_ref, g_ref,
                      o_ref, nk_ref, nv_ref, kcat, vcat, o_scr, *, t_new, bp):
    kcat[0:bp, :] = kc_ref[...].astype(BF16)
    vcat[0:bp, :] = vc_ref[...].astype(BF16)
    kcat[bp:bp + t_new, :] = kn_ref[...]
    vcat[bp:bp + t_new, :] = vn_ref[...]
    nk_ref[0:bp - t_new, :] = kc_ref[t_new:bp, :]
    nv_ref[0:bp - t_new, :] = vc_ref[t_new:bp, :]
    nk_ref[bp - t_new:bp, :] = knf_ref[...]
    nv_ref[bp - t_new:bp, :] = vnf_ref[...]
    even = _pair_masks()
    for hp in range(N_PAIRS):
        lanes = slice(hp * LANES, (hp + 1) * LANES)
        q128 = q_ref[:, lanes]
        k = kcat[:, lanes]
        v = vcat[:, lanes]
        accs = []
        for parity in range(2):
            h = 2 * hp + parity
            s = lax.dot_general(_head_q(q128, even, parity), k, _NT, preferred_element_type=F32)
            s = s + bias_ref[h, 0:t_new, 0:bp + t_new]
            p = jnp.exp(s - jnp.max(s, axis=-1, keepdims=True)).astype(BF16)
            accs.append(jnp.dot(p, _head_v(v, even, parity), preferred_element_type=F32))
        o_scr[:, lanes] = _head_out(accs[0], accs[1], even)
    o_ref[...] = _rms(o_scr[...], g_ref[...]).astype(BF16)


def _band_sample(q, kn, vn, knf, vnf, kc, vc, bias, g_out):
    b, t, w = q.shape
    bp = kc.shape[1]
    assert t == CHUNK and bp == LEFT
    new = lambda: pl.BlockSpec((None, t, w), lambda i: (i, 0, 0))
    buf = lambda: pl.BlockSpec((None, bp, w), lambda i: (i, 0, 0))
    return pl.pallas_call(
        functools.partial(_band_sample_body, t_new=t, bp=bp),
        grid=(b,),
        in_specs=[new(), new(), new(), new(), new(), buf(), buf(),
                  pl.BlockSpec(bias.shape, lambda i: (0, 0, 0)),
                  pl.BlockSpec((1, w), lambda i: (0, 0))],
        out_specs=[new(), buf(), buf()],
        out_shape=[jax.ShapeDtypeStruct((b, t, w), BF16), jax.ShapeDtypeStruct((b, bp, w), F32),
                   jax.ShapeDtypeStruct((b, bp, w), F32)],
        scratch_shapes=[pltpu.VMEM((bp + t, w), BF16), pltpu.VMEM((bp + t, w), BF16), pltpu.VMEM((t, w), F32)],
        compiler_params=pltpu.CompilerParams(dimension_semantics=("parallel",), vmem_limit_bytes=VMEM_LIMIT),
        name="band_sample",
    )(q, kn, vn, knf, vnf, kc, vc, bias, g_out)


N_HEADS_MEM = 4
HEAD_DIM_MEM = 128
W_MEM = N_HEADS_MEM * HEAD_DIM_MEM
MEM_SCALE = HEAD_DIM_MEM ** -0.5


def _memkv_body(m_ref, g_ref, w_ref, kf_ref, vf_ref, k_ref, v_ref):
    h = _rms(m_ref[...], g_ref[...]).astype(BF16)
    z = jnp.dot(h, w_ref[...], preferred_element_type=F32)
    kf_ref[...] = z[:, :W_MEM]
    vf_ref[...] = z[:, W_MEM:]
    k_ref[...] = z[:, :W_MEM].astype(BF16)
    v_ref[...] = z[:, W_MEM:].astype(BF16)


def _mem_kv(mem, g_mem, w_ckv):
    b, n, d = mem.shape
    blk = lambda: pl.BlockSpec((None, n, W_MEM), lambda i: (i, 0, 0))
    return pl.pallas_call(
        _memkv_body,
        grid=(b,),
        in_specs=[pl.BlockSpec((None, n, d), lambda i: (i, 0, 0)),
                  pl.BlockSpec((1, d), lambda i: (0, 0)),
                  pl.BlockSpec(w_ckv.shape, lambda i: (0, 0))],
        out_specs=[blk(), blk(), blk(), blk()],
        out_shape=[jax.ShapeDtypeStruct((b, n, W_MEM), F32)] * 2 + [jax.ShapeDtypeStruct((b, n, W_MEM), BF16)] * 2,
        compiler_params=pltpu.CompilerParams(dimension_semantics=("parallel",), vmem_limit_bytes=VMEM_LIMIT),
        name="mem_kv",
    )(mem, g_mem, w_ckv)


N_GROUPS = 4
EXPERTS_PER_GROUP = 8
N_EXPERTS = N_GROUPS * EXPERTS_PER_GROUP
ROUTE_L2 = N_GROUPS
ROUTE_ROWS = 8
POST_CHAIN = 512
R_EID0, R_EID1, R_RANK0, R_RANK1, R_GATE0, R_GATE1 = range(6)


ROUTE_LOGIT_ROWS = 40


def _row_max(x, mask):
    return jnp.max(jnp.where(mask, x, -jnp.inf), axis=0, keepdims=True)


def _first_row(mask, row):
    return jnp.min(jnp.where(mask, row, LANES), axis=0, keepdims=True)


def _route(logits, row):
    is_l1 = row < N_GROUPS
    m1 = _row_max(logits, is_l1)
    grp = _first_row(is_l1 & (logits == m1), row)
    wg = 1.0 / jnp.sum(jnp.where(is_l1, jnp.exp(logits - m1), 0.0), axis=0, keepdims=True)
    lo = ROUTE_L2 + grp * EXPERTS_PER_GROUP
    in_grp = (row >= lo) & (row < lo + EXPERTS_PER_GROUP)
    v0 = _row_max(logits, in_grp)
    i0 = _first_row(in_grp & (logits == v0), row)
    rest = in_grp & (row != i0)
    v1 = _row_max(logits, rest)
    i1 = _first_row(rest & (logits == v1), row)
    e1 = jnp.exp(v1 - v0)
    den = 1.0 / (1.0 + e1)
    return i0, i1, wg * den, wg * e1 * den


def _post_body(x_ref, a_ref, b_ref, mk_ref, mv_ref, woa_ref, wob_ref, gc_ref, wcq_ref, wco_ref,
               gf_ref, wrt_ref, brt_ref,
               x2_ref, h3_ref, route_ref, routet_ref, cnt_ref, o_scr, *, tm, nsub):
    seq = tm // nsub
    count = jnp.zeros((1, LANES), F32)
    for rows in _chains(tm, POST_CHAIN):
        n = rows.size
        x1 = (x_ref[rows, :] + jnp.dot(a_ref[rows, :], woa_ref[...], preferred_element_type=F32)
              + jnp.dot(b_ref[rows, :], wob_ref[...], preferred_element_type=F32))
        h2 = _rms(x1, gc_ref[...]).astype(BF16)
        qc = (jnp.dot(h2, wcq_ref[...], preferred_element_type=F32) * MEM_SCALE).astype(BF16)
        span = min(seq, n)
        for part in range(n // span):
            sub = (rows.start + part * span) // seq
            rs = slice(part * span, (part + 1) * span)
            orow = pl.ds(rows.start + part * span, span)
            for hm in range(N_HEADS_MEM):
                lanes = slice(hm * HEAD_DIM_MEM, (hm + 1) * HEAD_DIM_MEM)
                s = lax.dot_general(qc[rs, lanes], mk_ref[sub, :, lanes], _NT, preferred_element_type=F32)
                p = jnp.exp(s - jnp.max(s, axis=-1, keepdims=True))
                inv = 1.0 / jnp.sum(p, axis=-1, keepdims=True)
                o_scr[orow, lanes] = jnp.dot(p.astype(BF16), mv_ref[sub, :, lanes], preferred_element_type=F32) * inv
        x2 = x1 + jnp.dot(o_scr[rows, :].astype(BF16), wco_ref[...], preferred_element_type=F32)
        x2_ref[rows, :] = x2
        h3 = _rms(x2, gf_ref[...]).astype(BF16)
        h3_ref[rows, :] = h3

        logits = lax.dot_general(wrt_ref[...], h3, _NT, preferred_element_type=F32) + brt_ref[...]
        row = lax.broadcasted_iota(jnp.int32, (ROUTE_LOGIT_ROWS, n), 0)
        i0, i1, g0, g1 = _route(logits[:ROUTE_LOGIT_ROWS, :], row)
        e0 = i0 - ROUTE_L2
        e1 = i1 - ROUTE_L2
        expert = lax.broadcasted_iota(jnp.int32, (LANES, n), 0)
        hit0 = expert == e0
        hit1 = expert == e1
        onehot = jnp.where(hit0, 1.0, jnp.where(hit1, 1.0, 0.0)).astype(BF16)
        earlier = lax.broadcasted_iota(jnp.int32, (n, n), 0) < lax.broadcasted_iota(jnp.int32, (n, n), 1)
        before = jnp.where(earlier, 1.0, 0.0).astype(BF16)
        seen = jnp.dot(onehot, before, preferred_element_type=F32) + _row_to_col(count)
        rank0 = jnp.sum(jnp.where(hit0, seen, 0.0), axis=0, keepdims=True)
        rank1 = jnp.sum(jnp.where(hit1, seen, 0.0), axis=0, keepdims=True)
        count = count + lax.dot_general(jnp.ones((8, n), BF16), onehot, _NT, preferred_element_type=F32)[0:1, :]

        rec = jnp.concatenate([e0.astype(F32), e1.astype(F32), rank0, rank1, g0, g1, jnp.zeros((2, n), F32)], axis=0)
        routet_ref[:, rows] = rec
        route_ref[rows, :] = jnp.concatenate([rec, jnp.zeros((LANES - ROUTE_ROWS, n), F32)], axis=0).T[:, :ROUTE_ROWS]
    cnt_ref[...] = count


def _post_block(x, a_n, b_n, mk, mv, weights, *, tm):
    b, s, d = x.shape
    if s >= tm:
        nsub, grid = 1, (b, s // tm)
        tok = lambda i, j: (i, j, 0)
        flat = lambda i, j: i * (s // tm) + j
    else:
        nsub = tm // s
        assert b % nsub == 0
        x, a_n, b_n = (t.reshape(b // nsub, tm, t.shape[-1]) for t in (x, a_n, b_n))
        grid = (b // nsub, 1)
        tok = lambda i, j: (i, 0, 0)
        flat = lambda i, j: i
    mem = lambda i, j: (i, 0, 0)
    const = lambda arr: pl.BlockSpec(arr.shape, lambda i, j: (0,) * arr.ndim)
    in_specs = [pl.BlockSpec((None, tm, d), tok),
                pl.BlockSpec((None, tm, W_GROUP), tok), pl.BlockSpec((None, tm, W_GROUP), tok),
                pl.BlockSpec((nsub, mk.shape[1], W_MEM), mem), pl.BlockSpec((nsub, mv.shape[1], W_MEM), mem)]
    in_specs += [const(w) for w in weights]
    n = b * s
    out_shape = [jax.ShapeDtypeStruct((n, d), F32), jax.ShapeDtypeStruct((n, d), BF16),
                 jax.ShapeDtypeStruct((n, ROUTE_ROWS), F32), jax.ShapeDtypeStruct((ROUTE_ROWS, n), F32),
                 jax.ShapeDtypeStruct((n // tm, 1, LANES), F32)]
    out_specs = [pl.BlockSpec((tm, d), lambda i, j: (flat(i, j), 0)),
                 pl.BlockSpec((tm, d), lambda i, j: (flat(i, j), 0)),
                 pl.BlockSpec((tm, ROUTE_ROWS), lambda i, j: (flat(i, j), 0)),
                 pl.BlockSpec((ROUTE_ROWS, tm), lambda i, j: (0, flat(i, j))),
                 pl.BlockSpec((None, 1, LANES), lambda i, j: (flat(i, j), 0, 0))]
    return pl.pallas_call(
        functools.partial(_post_body, tm=tm, nsub=nsub),
        grid=grid,
        in_specs=in_specs,
        out_specs=out_specs,
        out_shape=out_shape,
        scratch_shapes=[pltpu.VMEM((tm, W_MEM), F32)],
        compiler_params=pltpu.CompilerParams(
            dimension_semantics=("parallel", "parallel"), vmem_limit_bytes=VMEM_LIMIT),
        name="post_block",
    )(x, a_n, b_n, mk, mv, *weights)


def _prep_post(w_out, g_cross, w_cq, w_co, g_ffn, w_r1, b_r1, w_r2, b_r2):
    pad = LANES - N_GROUPS - N_EXPERTS
    w_rt = jnp.pad(jnp.concatenate([w_r1, w_r2], axis=1), ((0, 0), (0, pad))).astype(BF16).T
    b_rt = jnp.pad(jnp.concatenate([b_r1, b_r2]).reshape(-1, 1), ((0, pad), (0, 0))).astype(F32)
    return [w_out[:W_GROUP].astype(BF16), w_out[W_GROUP:].astype(BF16), g_cross.reshape(1, -1),
            w_cq.astype(BF16), w_co.astype(BF16), g_ffn.reshape(1, -1), w_rt, b_rt]


D_EXPERT = 512
TOP_K = 2
ROW_TILE = 512
MXU_DIM = 256
RUN_ALIGN = 16
PLAN_TILES = LANES
TILE_TABLE = 2 * LANES


def _local_rows(tm):
    return -(-(TOP_K * tm + N_EXPERTS * (RUN_ALIGN - 1)) // MXU_DIM) * MXU_DIM


def _n_row_tiles(n_tokens, tm):
    rows = n_tokens * TOP_K + (n_tokens // tm) * N_EXPERTS * (RUN_ALIGN - 1) + N_EXPERTS * (ROW_TILE - 1)
    return rows // ROW_TILE


N_CHUNK_LANE = LANES - 1


def _plan_body(cnt_ref, lstart_ref, chunk_ref, offs_ref, te_ref):
    cnt = cnt_ref[...].astype(jnp.int32)
    n16 = ((cnt + (RUN_ALIGN - 1)) & ~(RUN_ALIGN - 1)).astype(F32)
    lend = _lane_cumsum(n16)
    lstart = lend - n16
    earlier = (_lane_cumsum(n16.T) - n16.T).T
    total = jnp.sum(n16, axis=0, keepdims=True).astype(jnp.int32)
    seg = jnp.broadcast_to((total + (ROW_TILE - 1)) & ~(ROW_TILE - 1), (8, LANES)).astype(F32)
    ends = _lane_cumsum(seg)
    offs = ends - seg
    shift = earlier + offs[0:1, :] - lstart
    lstart_ref[...] = lstart.astype(jnp.int32)
    offs_ref[...] = offs[0:1, :].astype(jnp.int32)

    lane = lax.broadcasted_iota(jnp.int32, (PLAN_TILES, LANES), 1)
    local_row = (lane * RUN_ALIGN).astype(F32)
    owner = jnp.zeros((PLAN_TILES, LANES), jnp.int32)
    for e in range(N_EXPERTS):
        owner = owner + jnp.where(lend[:, e:e + 1] <= local_row, 1, 0)
    glob = local_row
    for e in range(N_EXPERTS):
        glob = glob + jnp.where(owner == e, shift[:, e:e + 1], 0.0)
    n_chunks = lend[:, N_EXPERTS - 1:N_EXPERTS] * (1.0 / RUN_ALIGN)
    chunk_ref[...] = jnp.where(lane == N_CHUNK_LANE, n_chunks, glob).astype(jnp.int32)

    tile_start = (lax.broadcasted_iota(jnp.int32, te_ref.shape, 1) * ROW_TILE).astype(F32)
    te = jnp.zeros(te_ref.shape, jnp.int32)
    for e in range(N_EXPERTS):
        end_e = jnp.sum(jnp.where(lane[0:1, :] == e, ends[0:1, :], 0.0), axis=-1, keepdims=True)
        te = te + jnp.where(end_e <= tile_start, 1, 0)
    te_ref[...] = jnp.minimum(te, N_EXPERTS - 1)


def _plan(counts, tm):
    nt = counts.shape[0]
    assert nt <= PLAN_TILES and _local_rows(tm) // RUN_ALIGN <= N_CHUNK_LANE
    cnt = jnp.pad(counts.reshape(nt, LANES), ((0, PLAN_TILES - nt), (0, 0)))
    grid_i32 = jax.ShapeDtypeStruct((PLAN_TILES, LANES), jnp.int32)
    lstart, chunks, offs, te = pl.pallas_call(
        _plan_body,
        out_shape=[grid_i32, grid_i32, jax.ShapeDtypeStruct((1, LANES), jnp.int32),
                   jax.ShapeDtypeStruct((1, TILE_TABLE), jnp.int32)],
        name="moe_plan",
    )(cnt)
    per_tile = lambda t: t[:nt].reshape(nt, 1, LANES)
    return per_tile(lstart), per_tile(chunks), offs.reshape(LANES), te.reshape(TILE_TABLE)


def _for_each_chunk(chunk_ref, fn):
    def body(c, carry):
        fn(pl.multiple_of(c * RUN_ALIGN, RUN_ALIGN), pl.multiple_of(chunk_ref[0, c], RUN_ALIGN))
        return carry

    lax.fori_loop(0, chunk_ref[0, N_CHUNK_LANE], body, None)


def _local_positions_row(rt_ref, lstart_ref):
    pos = []
    for r_eid, r_rank in ((R_EID0, R_RANK0), (R_EID1, R_RANK1)):
        eid = rt_ref[r_eid:r_eid + 1, :].astype(jnp.int32)
        p = rt_ref[r_rank:r_rank + 1, :].astype(jnp.int32)
        for e in range(N_EXPERTS):
            p = p + jnp.where(eid == e, lstart_ref[0, e], 0)
        pos.append(p)
    return pos


def _dispatch_body(offs_ref, lstart_ref, chunk_ref, pchunk_ref,
                   hp_ref, hs_ref, rt_ref, xs_ref, loc, zeros, sems, zsem, *, tm, n_prompt_tiles):
    i = pl.program_id(0)
    n_tiles = xs_ref.shape[0] // ROW_TILE
    half = lax.rem(i, 2)

    @pl.when(i == 0)
    def _():
        zeros[...] = jnp.zeros(zeros.shape, zeros.dtype)
        zero_tile = lambda row: pltpu.make_async_copy(
            zeros, xs_ref.at[pl.ds(pl.multiple_of(row, ROW_TILE), ROW_TILE)], zsem)
        n_used = offs_ref[N_EXPERTS] // ROW_TILE

        def tail(j, carry, op):
            op(zero_tile(j * ROW_TILE))
            return carry

        for op in (lambda c: c.start(), lambda c: c.wait()):
            for e in range(N_EXPERTS):
                @pl.when(offs_ref[e + 1] > offs_ref[e])
                def _():
                    op(zero_tile(offs_ref[e + 1] - ROW_TILE))
            lax.fori_loop(n_used, n_tiles, functools.partial(tail, op=op), None)

    pos0, pos1 = _local_positions_row(rt_ref, lstart_ref)
    used = chunk_ref[0, N_CHUNK_LANE] * RUN_ALIGN
    body_rows = loc.shape[1] - MXU_DIM

    def sort_rows(h_ref, lo, n):
        slot = lo + lax.broadcasted_iota(jnp.int32, (n, tm), 0)
        perm = jnp.where(slot == pos0, 1.0, jnp.where(slot == pos1, 1.0, 0.0)).astype(BF16)
        loc[half, lo:lo + n, :] = jnp.dot(perm, h_ref[...], preferred_element_type=F32).astype(BF16)

    for h_ref, mine in ((hp_ref, i < n_prompt_tiles), (hs_ref, i >= n_prompt_tiles)):
        @pl.when(mine)
        def _():
            sort_rows(h_ref, 0, body_rows)

        @pl.when(mine & (used > body_rows))
        def _():
            sort_rows(h_ref, body_rows, MXU_DIM)

    def chunk(buf, lo, go):
        return pltpu.make_async_copy(loc.at[buf, pl.ds(lo, RUN_ALIGN)], xs_ref.at[pl.ds(go, RUN_ALIGN)],
                                     sems.at[buf])

    @pl.when(i > 0)
    def _():
        _for_each_chunk(pchunk_ref, lambda lo, go: chunk(1 - half, lo, go).wait())

    _for_each_chunk(chunk_ref, lambda lo, go: chunk(half, lo, go).start())

    @pl.when(i == pl.num_programs(0) - 1)
    def _():
        _for_each_chunk(chunk_ref, lambda lo, go: chunk(half, lo, go).wait())


def _dispatch(h3_prompt, h3_sample, route_t, plan, *, tm):
    lstart, chunks, offs, _ = plan
    n_p, n_s, d = h3_prompt.shape[0], h3_sample.shape[0], h3_prompt.shape[-1]
    assert n_p % tm == 0 and n_s % tm == 0
    n = n_p + n_s
    nt = n // tm
    npt = n_p // tm
    n_rows = _n_row_tiles(n, tm) * ROW_TILE
    smem_tile = lambda: pl.BlockSpec((None, 1, LANES), lambda i, offs: (i, 0, 0), memory_space=pltpu.SMEM)
    smem_prev = lambda: pl.BlockSpec((None, 1, LANES), lambda i, offs: (jnp.maximum(i - 1, 0), 0, 0),
                                     memory_space=pltpu.SMEM)
    return pl.pallas_call(
        functools.partial(_dispatch_body, tm=tm, n_prompt_tiles=npt),
        grid_spec=pltpu.PrefetchScalarGridSpec(
            num_scalar_prefetch=1,
            grid=(nt,),
            in_specs=[smem_tile(), smem_tile(), smem_prev(),
                      pl.BlockSpec((tm, d), lambda i, offs: (jnp.minimum(i, npt - 1), 0)),
                      pl.BlockSpec((tm, d), lambda i, offs: (jnp.maximum(i - npt, 0), 0)),
                      pl.BlockSpec((ROUTE_ROWS, tm), lambda i, offs: (0, i))],
            out_specs=pl.BlockSpec(memory_space=pl.ANY),
            scratch_shapes=[pltpu.VMEM((2, _local_rows(tm), d), BF16), pltpu.VMEM((ROW_TILE, d), BF16),
                            pltpu.SemaphoreType.DMA((2,)), pltpu.SemaphoreType.DMA(())]),
        out_shape=jax.ShapeDtypeStruct((n_rows, d), BF16),
        compiler_params=pltpu.CompilerParams(dimension_semantics=("arbitrary",), vmem_limit_bytes=VMEM_LIMIT),
        name="moe_dispatch",
    )(offs, lstart, chunks, chunks, h3_prompt, h3_sample, route_t)


def _experts_body(te_ref, offs_ref, xs_ref, wg_ref, wu_ref, wd_ref, ys_ref,
                  wg_buf, wu_buf, wd_buf, wg_bf, wu_bf, wd_bf, turn_ref, sems):
    i = pl.program_id(0)
    n_used = offs_ref[N_EXPERTS] // ROW_TILE

    def fetch(expert, half):
        return [pltpu.make_async_copy(src.at[expert], dst.at[half], sems.at[half, k])
                for k, (src, dst) in enumerate(((wg_ref, wg_buf), (wu_ref, wu_buf), (wd_ref, wd_buf)))]

    @pl.when(i == 0)
    def _():
        turn_ref[0] = 0
        for copy in fetch(te_ref[0], 0):
            copy.start()

    @pl.when(i < n_used)
    def _():
        expert = te_ref[i]

        @pl.when((i == 0) | (expert != te_ref[jnp.maximum(i - 1, 0)]))
        def _():
            half = lax.rem(turn_ref[0], 2)
            turn_ref[0] = turn_ref[0] + 1
            for copy in fetch(expert, half):
                copy.wait()
            wg_bf[...] = wg_buf[half].astype(BF16)
            wu_bf[...] = wu_buf[half].astype(BF16)
            wd_bf[...] = wd_buf[half].astype(BF16)
            following = offs_ref[expert + 1] // ROW_TILE

            @pl.when(following < n_used)
            def _():
                for copy in fetch(te_ref[following], 1 - half):
                    copy.start()

        dot = functools.partial(jnp.dot, preferred_element_type=F32)
        for rows in _chains(ROW_TILE):
            x = xs_ref[rows, :]
            gate = dot(x, wg_bf[...])
            up = dot(x, wu_bf[...])
            act = (gate * jax.nn.sigmoid(gate) * up).astype(BF16)
            ys_ref[rows, :] = dot(act, wd_bf[...]).astype(BF16)

    @pl.when(i >= n_used)
    def _():
        ys_ref[...] = jnp.zeros(ys_ref.shape, ys_ref.dtype)


def _experts(xs, te, offs, w_gate, w_up, w_down):
    n_rows, d = xs.shape
    last = lambda i, te, offs: jnp.minimum(i, offs[N_EXPERTS] // ROW_TILE - 1)
    hbm = pl.BlockSpec(memory_space=pl.ANY)
    return pl.pallas_call(
        _experts_body,
        grid_spec=pltpu.PrefetchScalarGridSpec(
            num_scalar_prefetch=2,
            grid=(n_rows // ROW_TILE,),
            in_specs=[pl.BlockSpec((ROW_TILE, d), lambda i, te, offs: (last(i, te, offs), 0)), hbm, hbm, hbm],
            out_specs=pl.BlockSpec((ROW_TILE, d), lambda i, te, offs: (i, 0)),
            scratch_shapes=[pltpu.VMEM((2, d, D_EXPERT), F32), pltpu.VMEM((2, d, D_EXPERT), F32),
                            pltpu.VMEM((2, D_EXPERT, d), F32),
                            pltpu.VMEM((d, D_EXPERT), BF16), pltpu.VMEM((d, D_EXPERT), BF16),
                            pltpu.VMEM((D_EXPERT, d), BF16),
                            pltpu.SMEM((1,), jnp.int32), pltpu.SemaphoreType.DMA((2, 3))]),
        out_shape=jax.ShapeDtypeStruct((n_rows, d), BF16),
        compiler_params=pltpu.CompilerParams(dimension_semantics=("arbitrary",), vmem_limit_bytes=VMEM_LIMIT),
        name="moe_experts",
    )(te, offs, xs, w_gate, w_up, w_down)


def _combine_body(chunk_ref, nchunk_ref, x2p_ref, x2s_ref, route_ref, lsv_ref, ys_ref, g_ref,
                  yp_ref, ysm_ref, loc, sems, *, tm, n_prompt_tiles):
    i = pl.program_id(0)
    half = lax.rem(i, 2)

    def chunk(buf, lo, go):
        return pltpu.make_async_copy(ys_ref.at[pl.ds(go, RUN_ALIGN)], loc.at[buf, pl.ds(lo, RUN_ALIGN)],
                                     sems.at[buf])

    @pl.when(i == 0)
    def _():
        loc[...] = jnp.zeros(loc.shape, loc.dtype)
        _for_each_chunk(chunk_ref, lambda lo, go: chunk(0, lo, go).start())

    @pl.when(i + 1 < pl.num_programs(0))
    def _():
        _for_each_chunk(nchunk_ref, lambda lo, go: chunk(1 - half, lo, go).start())

    _for_each_chunk(chunk_ref, lambda lo, go: chunk(half, lo, go).wait())

    lane = lax.broadcasted_iota(jnp.int32, (tm, LANES), 1)
    picks = []
    for r_eid, r_rank, r_gate in ((R_EID0, R_RANK0, R_GATE0), (R_EID1, R_RANK1, R_GATE1)):
        eid = route_ref[:, r_eid:r_eid + 1].astype(jnp.int32)
        start = jnp.sum(jnp.where(lane == eid, lsv_ref[...], 0), axis=-1, keepdims=True)
        picks.append((route_ref[:, r_rank:r_rank + 1].astype(jnp.int32) + start, route_ref[:, r_gate:r_gate + 1]))

    slot = lax.broadcasted_iota(jnp.int32, (tm, loc.shape[1]), 1)
    weights = jnp.zeros(slot.shape, F32)
    for pos, gate in picks:
        weights = jnp.where(slot == pos, gate, weights)
    moe = jnp.dot(weights.astype(BF16), loc[half], preferred_element_type=F32)

    @pl.when(i < n_prompt_tiles)
    def _():
        yp_ref[...] = _rms(x2p_ref[...] + moe, g_ref[...])

    @pl.when(i >= n_prompt_tiles)
    def _():
        ysm_ref[...] = _rms(x2s_ref[...] + moe, g_ref[...])


def _combine(x2_prompt, x2_sample, route, plan, ys, g_final, *, tm):
    lstart, chunks, _, _ = plan
    (n_p, d), n_s = x2_prompt.shape, x2_sample.shape[0]
    assert n_p % tm == 0 and n_s % tm == 0
    npt = n_p // tm
    nt = npt + n_s // tm
    smem_tile = lambda: pl.BlockSpec((None, 1, LANES), lambda i: (i, 0, 0), memory_space=pltpu.SMEM)
    smem_next = lambda: pl.BlockSpec((None, 1, LANES), lambda i: (jnp.minimum(i + 1, nt - 1), 0, 0),
                                     memory_space=pltpu.SMEM)
    prompt_tile = lambda: pl.BlockSpec((tm, d), lambda i: (jnp.minimum(i, npt - 1), 0))
    sample_tile = lambda: pl.BlockSpec((tm, d), lambda i: (jnp.maximum(i - npt, 0), 0))
    return pl.pallas_call(
        functools.partial(_combine_body, tm=tm, n_prompt_tiles=npt),
        grid=(nt,),
        in_specs=[smem_tile(), smem_next(), prompt_tile(), sample_tile(),
                  pl.BlockSpec((tm, ROUTE_ROWS), lambda i: (i, 0)),
                  pl.BlockSpec((None, 1, LANES), lambda i: (i, 0, 0)),
                  pl.BlockSpec(memory_space=pl.ANY),
                  pl.BlockSpec((1, d), lambda i: (0, 0))],
        out_specs=[prompt_tile(), sample_tile()],
        out_shape=[jax.ShapeDtypeStruct((n_p, d), F32), jax.ShapeDtypeStruct((n_s, d), F32)],
        scratch_shapes=[pltpu.VMEM((2, _local_rows(tm), d), BF16), pltpu.SemaphoreType.DMA((2,))],
        compiler_params=pltpu.CompilerParams(dimension_semantics=("arbitrary",), vmem_limit_bytes=VMEM_LIMIT),
        name="moe_combine",
    )(chunks, chunks, x2_prompt, x2_sample, route, lstart, ys, g_final)


TOKEN_TILE = 512
FOX_Q_TILE = 256
FOX_CACHE_TILE = 1024


def kernel(x_prompt, x_sample, cache_fox_k, cache_fox_v, cache_fox_logf, cache_band_k, cache_band_v, cache_mem_k, cache_mem_v, mem_prompt, g_mix, w_in, b_forget, g_out_fox, g_out_band, rel_table, w_out, g_cross, g_mem, w_cq, w_ck, w_cv, w_co, g_ffn, w_router1, b_router1, w_router2, b_router2, w_exp_gate, w_exp_up, w_exp_down, g_final):
    assert g_mix.shape[0] == 1, "single-layer model"
    bsz, seq, d = x_prompt.shape
    sb, st, _ = x_sample.shape
    n_s = sb * st
    n_mem = mem_prompt.shape[1]
    row = lambda g: g.reshape(1, -1)

    w_pad, bf_pad, g_mix_r = _prep_proj(w_in[0], b_forget[0], g_mix[0])
    g_of, g_ob = row(g_out_fox[0]), row(g_out_band[0])
    bias_t, bias_s = _band_bias(_prep_band_bias_row(rel_table[0]))

    qx, kx, vat, qxb, kxb, vbt, kaf, vaf, kbf, vbf, logf = _proj(
        x_prompt, g_mix_r, w_pad, bf_pad, tm=TOKEN_TILE, prompt=True)
    a_p = _fox_prompt(qx, kx, vat, g_of, tq=FOX_Q_TILE)
    b_p = _band_prompt(qxb, kxb, vbt, bias_t, g_ob)

    s_out = _proj(x_sample.reshape(1, n_s, d), g_mix_r, w_pad, bf_pad, tm=n_s, prompt=False)
    sqa, ska, sva, sqb, skb, svb, skaf, svaf, skbf, svbf = (t.reshape(sb, st, W_GROUP) for t in s_out[:10])
    slogf = s_out[10].reshape(sb, st, N_HEADS)
    slft = s_out[11].reshape(N_HEADS, sb, st).transpose(1, 0, 2)
    past = cache_fox_k.shape[2]
    a_s = _fox_sample(sqa, ska, sva, slft,
                      cache_fox_k[0].reshape(sb, past, W_GROUP), cache_fox_v[0].reshape(sb, past, W_GROUP),
                      cache_fox_logf[0].transpose(0, 2, 1), g_of, pt=FOX_CACHE_TILE)
    bp = cache_band_k.shape[2]
    b_s, nbk, nbv = _band_sample(sqb, skb, svb, skbf, svbf,
                                 cache_band_k[0].reshape(sb, bp, W_GROUP), cache_band_v[0].reshape(sb, bp, W_GROUP),
                                 bias_s, g_ob)

    w_ckv = jnp.concatenate([w_ck[0], w_cv[0]], axis=1).astype(BF16)
    mkf, mvf, mk, mv = _mem_kv(mem_prompt, row(g_mem[0]), w_ckv)
    post_w = _prep_post(w_out[0], g_cross[0], w_cq[0], w_co[0], g_ffn[0],
                        w_router1[0], b_router1[0], w_router2[0], b_router2[0])
    x2_p, h3_p, route_p, routet_p, cnt_p = _post_block(x_prompt, a_p, b_p, mk, mv, post_w, tm=TOKEN_TILE)
    cmk = cache_mem_k[0].reshape(sb, n_mem, W_MEM).astype(BF16)
    cmv = cache_mem_v[0].reshape(sb, n_mem, W_MEM).astype(BF16)
    x2_s, h3_s, route_s, routet_s, cnt_s = _post_block(x_sample, a_s, b_s, cmk, cmv, post_w, tm=TOKEN_TILE)
    route = jnp.concatenate([route_p, route_s], axis=0)
    route_t = jnp.concatenate([routet_p, routet_s], axis=1)

    plan = _plan(jnp.concatenate([cnt_p, cnt_s], axis=0), TOKEN_TILE)
    xs = _dispatch(h3_p, h3_s, route_t, plan, tm=TOKEN_TILE)
    ys = _experts(xs, plan[3], plan[2], w_exp_gate[0], w_exp_up[0], w_exp_down[0])
    y_p, y_s = _combine(x2_p, x2_s, route, plan, ys, row(g_final), tm=TOKEN_TILE)

    heads = lambda t, n: t.reshape(1, n, -1, N_HEADS, HEAD_DIM)
    mem_heads = lambda t: t.reshape(1, bsz, n_mem, N_HEADS_MEM, HEAD_DIM_MEM)
    return (y_p.reshape(bsz, seq, d), y_s.reshape(sb, st, d),
            heads(kaf, bsz), heads(vaf, bsz), logf.reshape(1, bsz, seq, N_HEADS),
            heads(kbf, bsz), heads(vbf, bsz), mem_heads(mkf), mem_heads(mvf),
            heads(skaf, sb), heads(svaf, sb), slogf.reshape(1, sb, st, N_HEADS),
            heads(nbk, sb), heads(nbv, sb))
```

```python
import functools

import jax
import jax.numpy as jnp
from jax import lax
from jax.experimental import pallas as pl
from jax.experimental.pallas import tpu as pltpu

F32 = jnp.float32
BF16 = jnp.bfloat16

D_MODEL = 1024
HEAD_DIM = 64
N_HEADS = 8
W_GROUP = N_HEADS * HEAD_DIM
N_PAIRS = N_HEADS // 2
CHUNK = 64
LEFT_CHUNKS = 8
LEFT = LEFT_CHUNKS * CHUNK
REL_CLIP = 128
EPS = 1e-6
NEG_INF = -1e30
ATTN_SCALE = HEAD_DIM ** -0.5
LANES = 128
PROJ_PAD = 3 * W_GROUP * 2 + LANES
VMEM_LIMIT = 56 * 1024 * 1024


def _rms(x, g):
    ms = jnp.mean(x * x, axis=-1, keepdims=True)
    return x * lax.rsqrt(ms + EPS) * g


def _log_sigmoid(x):
    return -(jnp.maximum(-x, 0.0) + jnp.log1p(jnp.exp(-jnp.abs(x))))


def _lane_cumsum(x):
    n = x.shape[-1]
    lane = lax.broadcasted_iota(jnp.int32, x.shape, 1)
    k = 1
    while k < n:
        x = x + jnp.where(lane >= k, pltpu.roll(x, k, axis=1), 0.0)
        k *= 2
    return x


LOG2E = 1.4426950408889634
SCALE_BASE2 = ATTN_SCALE * LOG2E


def _split3(x):
    hi = x.astype(BF16).astype(F32)
    mid = (x - hi).astype(BF16).astype(F32)
    lo = x - hi - mid
    return hi, mid, lo


def _extra_lane(parity):
    return HEAD_DIM if parity == 0 else 0


def _fox_extras(c3t, hp, tm):
    row = lax.broadcasted_iota(jnp.int32, (8, tm), 0)

    def group(h, q_side):
        hi, mid, lo = (p[h:h + 1, :] for p in c3t)
        if q_side:
            return jnp.where(row < 3, 1.0, jnp.where(row == 3, hi, jnp.where(row == 4, mid, jnp.where(row == 5, lo, 0.0))))
        return jnp.where(row == 0, -hi, jnp.where(row == 1, -mid, jnp.where(row == 2, -lo, jnp.where(row < 6, 1.0, 0.0))))

    gap = jnp.zeros((HEAD_DIM - 8, tm), F32)
    sides = []
    for q_side in (True, False):
        t = jnp.concatenate([group(2 * hp + 1, q_side), gap, group(2 * hp, q_side), gap], axis=0)
        sides.append(t.T)
    return sides


def _head_blocks(x128, extras, lane):
    return (jnp.where(lane < HEAD_DIM, x128, extras).astype(BF16),
            jnp.where(lane >= HEAD_DIM, x128, extras).astype(BF16))


Q_A, K_A, V_A, Q_B, K_B, V_B = range(6)


PROJ_CHAIN = 256


def _chains(tm, chain=PROJ_CHAIN):
    n = max(tm // chain, 1)
    return [pl.ds(i * (tm // n), tm // n) for i in range(n)]


def _proj_common(rows, x_ref, g_ref, w_ref, bf_ref, kaf_ref, vaf_ref, kbf_ref, vbf_ref, logf_ref, keep_tiles):
    s = pl.program_id(1)
    ns = pl.num_programs(1)
    h = _rms(x_ref[rows, :], g_ref[...]).astype(BF16)
    w = W_GROUP
    zf = jnp.dot(h, w_ref[:, 6 * w:6 * w + LANES], preferred_element_type=F32)
    z = [jnp.dot(h, w_ref[:, g * w:(g + 1) * w], preferred_element_type=F32) for g in range(6)]
    kaf_ref[rows, :] = z[K_A]
    vaf_ref[rows, :] = z[V_A]

    @pl.when(s >= ns - keep_tiles)
    def _():
        kbf_ref[rows, :] = z[K_B]
        vbf_ref[rows, :] = z[V_B]

    logf = _log_sigmoid(zf + bf_ref[...])
    logf_ref[rows, :] = logf[:, :N_HEADS]
    return z, logf


def _proj_prompt_body(x_ref, g_ref, w_ref, bf_ref, qx_ref, kx_ref, vat_ref, qxb_ref, kxb_ref, vbt_ref,
                      kaf_ref, vaf_ref, kbf_ref, vbf_ref, logf_ref, carry_ref, *, tm, keep_tiles):
    @pl.when(pl.program_id(1) == 0)
    def _():
        carry_ref[...] = jnp.zeros_like(carry_ref)

    for rows in _chains(tm):
        n = rows.size
        z, logf = _proj_common(rows, x_ref, g_ref, w_ref, bf_ref, kaf_ref, vaf_ref, kbf_ref, vbf_ref, logf_ref,
                               keep_tiles)
        vat_ref[:, rows] = z[V_A].T.astype(BF16)
        vbt_ref[:, rows] = z[V_B].T.astype(BF16)
        ct = _lane_cumsum(logf.T[:N_HEADS, :]) + carry_ref[:, 0:1]
        carry_ref[...] = jnp.broadcast_to(ct[:, n - 1:n], carry_ref.shape)
        c3t = _split3(ct * LOG2E)
        lane = lax.broadcasted_iota(jnp.int32, (n, LANES), 1)
        band_q_extras = jnp.where((lane == _extra_lane(0)) | (lane == _extra_lane(1)), 1.0, 0.0)
        band_k_extras = jnp.zeros((n, LANES), F32)
        for hp in range(N_PAIRS):
            blocks = slice(2 * hp * LANES, 2 * (hp + 1) * LANES)
            blk = lambda group, hp=hp, z=z: z[group][:, hp * LANES:(hp + 1) * LANES]
            q_extras, k_extras = _fox_extras(c3t, hp, n)
            qx_ref[rows, blocks] = jnp.concatenate(_head_blocks(blk(Q_A) * SCALE_BASE2, q_extras, lane), axis=1)
            kx_ref[rows, blocks] = jnp.concatenate(_head_blocks(blk(K_A), k_extras, lane), axis=1)
            qxb_ref[rows, blocks] = jnp.concatenate(_head_blocks(blk(Q_B) * SCALE_BASE2, band_q_extras, lane), axis=1)
            kxb_ref[rows, blocks] = jnp.concatenate(_head_blocks(blk(K_B), band_k_extras, lane), axis=1)


def _proj_sample_body(x_ref, g_ref, w_ref, bf_ref, qa_ref, ka_ref, va_ref, qb_ref, kb_ref, vb_ref,
                      kaf_ref, vaf_ref, kbf_ref, vbf_ref, logf_ref, lt_ref, *, tm, keep_tiles):
    for rows in _chains(tm):
        z, logf = _proj_common(rows, x_ref, g_ref, w_ref, bf_ref, kaf_ref, vaf_ref, kbf_ref, vbf_ref, logf_ref,
                               keep_tiles)
        qa_ref[rows, :] = (z[Q_A] * ATTN_SCALE).astype(BF16)
        ka_ref[rows, :] = z[K_A].astype(BF16)
        va_ref[rows, :] = z[V_A].astype(BF16)
        qb_ref[rows, :] = (z[Q_B] * ATTN_SCALE).astype(BF16)
        kb_ref[rows, :] = z[K_B].astype(BF16)
        vb_ref[rows, :] = z[V_B].astype(BF16)
        lt_ref[:, rows] = logf.T[:N_HEADS, :]


def _proj(x, g_mix, w_pad, bf_pad, *, tm, prompt):
    b, s, d = x.shape
    ns = s // tm
    keep = min(LEFT, s)
    assert s % tm == 0 and keep % tm == 0
    keep_tiles = keep // tm
    row = pl.BlockSpec((None, tm, W_GROUP), lambda i, j: (i, j, 0))
    wide = pl.BlockSpec((None, tm, N_HEADS * LANES), lambda i, j: (i, j, 0))
    col = pl.BlockSpec((None, W_GROUP, tm), lambda i, j: (i, 0, j))
    keep_spec = pl.BlockSpec((None, tm, W_GROUP), lambda i, j: (i, jnp.maximum(j - (ns - keep_tiles), 0), 0))
    heads_row = pl.BlockSpec((None, tm, N_HEADS), lambda i, j: (i, j, 0))
    heads_col = pl.BlockSpec((None, N_HEADS, tm), lambda i, j: (i, 0, j))
    const = lambda shape: pl.BlockSpec(shape, lambda i, j: (0,) * len(shape))
    rows_bf = jax.ShapeDtypeStruct((b, s, W_GROUP), BF16)
    wide_bf = jax.ShapeDtypeStruct((b, s, N_HEADS * LANES), BF16)
    cols_bf = jax.ShapeDtypeStruct((b, W_GROUP, s), BF16)
    f32_tail = [jax.ShapeDtypeStruct((b, s, W_GROUP), F32)] * 2
    f32_tail += [jax.ShapeDtypeStruct((b, keep, W_GROUP), F32)] * 2
    f32_tail += [jax.ShapeDtypeStruct((b, s, N_HEADS), F32)]
    tail_specs = [row, row, keep_spec, keep_spec, heads_row]
    if prompt:
        body = functools.partial(_proj_prompt_body, tm=tm, keep_tiles=keep_tiles)
        out_shape = [wide_bf, wide_bf, cols_bf, wide_bf, wide_bf, cols_bf] + f32_tail
        out_specs = [wide, wide, col, wide, wide, col] + tail_specs
        scratch = [pltpu.VMEM((N_HEADS, LANES), F32)]
    else:
        body = functools.partial(_proj_sample_body, tm=tm, keep_tiles=keep_tiles)
        out_shape = [rows_bf] * 6 + f32_tail + [jax.ShapeDtypeStruct((b, N_HEADS, s), F32)]
        out_specs = [row] * 6 + tail_specs + [heads_col]
        scratch = []
    return pl.pallas_call(
        body,
        grid=(b, ns),
        in_specs=[pl.BlockSpec((None, tm, d), lambda i, j: (i, j, 0)),
                  const((1, d)), const(w_pad.shape), const((1, LANES))],
        out_specs=out_specs,
        out_shape=out_shape,
        scratch_shapes=scratch,
        compiler_params=pltpu.CompilerParams(
            dimension_semantics=("parallel", "arbitrary"), vmem_limit_bytes=VMEM_LIMIT),
        name="proj",
    )(x, g_mix, w_pad, bf_pad)


def _prep_proj(w_in, b_forget, g_mix):
    cols = w_in.shape[-1]
    w_pad = jnp.pad(w_in, ((0, 0), (0, PROJ_PAD - cols))).astype(BF16)
    bf_pad = jnp.pad(b_forget.reshape(1, -1), ((0, 0), (0, LANES - N_HEADS))).astype(F32)
    return w_pad, bf_pad, g_mix.reshape(1, -1)


def _pair_masks():
    lane = lax.broadcasted_iota(jnp.int32, (1, LANES), 1)
    return lane < HEAD_DIM


def _head_q(q128, even_lanes, parity):
    keep = even_lanes if parity == 0 else jnp.logical_not(even_lanes)
    return jnp.where(keep, q128, jnp.zeros_like(q128))


def _head_v(v128, even_lanes, parity):
    keep = even_lanes if parity == 0 else jnp.logical_not(even_lanes)
    return jnp.where(keep, v128, jnp.ones_like(v128))


def _head_out(acc_even, acc_odd, even_lanes):
    inv_e = 1.0 / acc_even[:, HEAD_DIM:HEAD_DIM + 1]
    inv_o = 1.0 / acc_odd[:, 0:1]
    return jnp.where(even_lanes, acc_even * inv_e, acc_odd * inv_o)


_NT = (((1,), (1,)), ((), ()))


def _pair_rows():
    row = lax.broadcasted_iota(jnp.int32, (LANES, 1), 0)
    return row < HEAD_DIM


def _head_vt(vt128, even_rows, parity):
    keep = even_rows if parity == 0 else jnp.logical_not(even_rows)
    return jnp.where(keep, vt128, jnp.ones_like(vt128))


def _head_out_t(acc_even, acc_odd, even_rows):
    inv_e = 1.0 / acc_even[HEAD_DIM:HEAD_DIM + 1, :]
    inv_o = 1.0 / acc_odd[0:1, :]
    return jnp.where(even_rows, acc_even * inv_e, acc_odd * inv_o)


def _fox_body(qx_ref, kx_ref, vt_ref, g_ref, o_ref, s_scr, p_scr, m_scr, alpha_scr, acc_scr, ot_scr, *, tq):
    qi = pl.program_id(1)
    even_rows = _pair_rows()
    m_scr[...] = jnp.full(m_scr.shape, NEG_INF, F32)
    acc_scr[...] = jnp.zeros(acc_scr.shape, F32)
    key = lax.broadcasted_iota(jnp.int32, (tq, tq), 0)
    qry = lax.broadcasted_iota(jnp.int32, (tq, tq), 1)
    causal = key <= qry

    def logits(j, half):
        start = pl.multiple_of(j * tq, tq)
        for h in range(N_HEADS):
            head = slice(h * LANES, (h + 1) * LANES)
            s_scr[half, h] = lax.dot_general(kx_ref[pl.ds(start, tq), head], qx_ref[:, head], _NT,
                                             preferred_element_type=F32)

    def weigh(j, half, masked):
        start = pl.multiple_of(j * tq, tq)
        for h in range(N_HEADS):
            st = s_scr[half, h]
            if masked:
                st = jnp.where(causal, st, NEG_INF)
            m_old = m_scr[h:h + 1, :]
            m_new = jnp.maximum(m_old, jnp.max(st, axis=0, keepdims=True))
            p_scr[h] = jnp.exp2(st - m_new).astype(BF16)
            alpha_scr[h:h + 1, :] = jnp.exp2(m_old - m_new)
            m_scr[h:h + 1, :] = m_new
        for h in range(N_HEADS):
            pair = slice((h // 2) * LANES, (h // 2 + 1) * LANES)
            vt = _head_vt(vt_ref[pair, pl.ds(start, tq)], even_rows, h % 2)
            acc_scr[h] = acc_scr[h] * alpha_scr[h:h + 1, :] + jnp.dot(vt, p_scr[h], preferred_element_type=F32)

    def step(j, cur, nxt):
        logits(j + 1, nxt)
        weigh(j, cur, False)

    def body(jj, carry):
        step(2 * jj, 0, 1)
        step(2 * jj + 1, 1, 0)
        return carry

    logits(0, 0)
    lax.fori_loop(0, qi // 2, body, None)
    odd = lax.rem(qi, 2) == 1

    @pl.when(odd)
    def _():
        step(qi - 1, 0, 1)
        weigh(qi, 1, True)

    @pl.when(jnp.logical_not(odd))
    def _():
        weigh(qi, 0, True)
    for hp in range(N_PAIRS):
        ot_scr[hp * LANES:(hp + 1) * LANES, :] = _head_out_t(acc_scr[2 * hp], acc_scr[2 * hp + 1], even_rows)
    o_ref[...] = _rms(ot_scr[...].T, g_ref[...]).astype(BF16)


def _fox_prompt(qx, kx, vat, g_out, *, tq):
    b, s, wx = qx.shape
    w = vat.shape[1]
    return pl.pallas_call(
        functools.partial(_fox_body, tq=tq),
        grid=(b, s // tq),
        in_specs=[pl.BlockSpec((None, tq, wx), lambda i, j: (i, j, 0)),
                  pl.BlockSpec((None, s, wx), lambda i, j: (i, 0, 0)),
                  pl.BlockSpec((None, w, s), lambda i, j: (i, 0, 0)),
                  pl.BlockSpec((1, w), lambda i, j: (0, 0))],
        out_specs=pl.BlockSpec((None, tq, w), lambda i, j: (i, j, 0)),
        out_shape=jax.ShapeDtypeStruct((b, s, w), BF16),
        scratch_shapes=[pltpu.VMEM((2, N_HEADS, tq, tq), F32), pltpu.VMEM((N_HEADS, tq, tq), BF16),
                        pltpu.VMEM((N_HEADS, tq), F32), pltpu.VMEM((N_HEADS, tq), F32),
                        pltpu.VMEM((N_HEADS, LANES, tq), F32), pltpu.VMEM((w, tq), F32)],
        compiler_params=pltpu.CompilerParams(
            dimension_semantics=("parallel", "arbitrary"), vmem_limit_bytes=VMEM_LIMIT),
        name="fox_prompt",
    )(qx, kx, vat, g_out)


BAND_CHUNKS = 4
BAND_Q = BAND_CHUNKS * CHUNK
BAND_K = (LEFT_CHUNKS + BAND_CHUNKS) * CHUNK
BIAS_ROW = BAND_K + BAND_Q
BAND_K_SAMPLE = (LEFT + CHUNK + LANES - 1) // LANES * LANES


def _prep_band_bias_row(rel_table):
    pivot = LEFT + BAND_Q
    n_hi = pivot - REL_CLIP + 1
    n_mid = min(2 * REL_CLIP, BIAS_ROW - n_hi)
    n_lo = BIAS_ROW - n_hi - n_mid
    parts = [jnp.broadcast_to(rel_table[2 * REL_CLIP:], (n_hi, N_HEADS)),
             rel_table[2 * REL_CLIP - 1::-1][:n_mid],
             jnp.broadcast_to(rel_table[:1], (n_lo, N_HEADS))]
    return jnp.concatenate(parts, axis=0).T.reshape(N_HEADS, 1, BIAS_ROW)


def _band_bias_body(row_ref, bt_ref, bs_ref):
    rows = jnp.broadcast_to(row_ref[...], (BAND_Q, BIAS_ROW))
    skew = pltpu.roll(rows, 0, axis=1, stride=1, stride_axis=0)
    bias = skew[:, BAND_Q:]
    bs_ref[...] = bias[:CHUNK, :BAND_K_SAMPLE]
    qc = lax.broadcasted_iota(jnp.int32, (BAND_Q, BAND_K), 0) // CHUNK
    kc = lax.broadcasted_iota(jnp.int32, (BAND_Q, BAND_K), 1) // CHUNK
    bt_ref[...] = jnp.where((kc >= qc) & (kc <= qc + LEFT_CHUNKS), bias * LOG2E, NEG_INF).T


def _band_bias(bias_row):
    return pl.pallas_call(
        _band_bias_body,
        grid=(N_HEADS,),
        in_specs=[pl.BlockSpec((None, 1, BIAS_ROW), lambda h: (h, 0, 0))],
        out_specs=[pl.BlockSpec((None, BAND_K, BAND_Q), lambda h: (h, 0, 0)),
                   pl.BlockSpec((None, CHUNK, BAND_K_SAMPLE), lambda h: (h, 0, 0))],
        out_shape=[jax.ShapeDtypeStruct((N_HEADS, BAND_K, BAND_Q), F32),
                   jax.ShapeDtypeStruct((N_HEADS, CHUNK, BAND_K_SAMPLE), F32)],
        name="band_bias",
    )(bias_row)


def _band_body(qx_ref, kx_ref, vt_ref, bias_ref, g_ref, o_ref, kpad, vtpad, s_scr, p_scr, ot_scr, *, s_len):
    step = pl.program_id(1)

    @pl.when(step == 0)
    def _():
        lane = lax.broadcasted_iota(jnp.int32, (LEFT, 2 * LANES), 1)
        flags = (lane == _extra_lane(0)) | (lane == LANES + _extra_lane(1))
        pad_pair = jnp.where(flags, NEG_INF, 0.0).astype(BF16)
        for hp in range(N_PAIRS):
            kpad[0:LEFT, 2 * hp * LANES:2 * (hp + 1) * LANES] = pad_pair
        vtpad[:, 0:LEFT] = jnp.zeros((W_GROUP, LEFT), BF16)
        kpad[LEFT:LEFT + s_len, :] = kx_ref[...]
        vtpad[:, LEFT:LEFT + s_len] = vt_ref[...]

    even_rows = _pair_rows()
    start = pl.multiple_of(step * BAND_Q, BAND_Q)
    def logits(h):
        head = slice(h * LANES, (h + 1) * LANES)
        s_scr[h] = bias_ref[h] + lax.dot_general(kpad[pl.ds(start, BAND_K), head], qx_ref[:, head], _NT,
                                                 preferred_element_type=F32)

    def weigh(h):
        st = s_scr[h]
        p_scr[h] = jnp.exp2(st - jnp.max(st, axis=0, keepdims=True)).astype(BF16)

    def values(hp):
        pair = slice(hp * LANES, (hp + 1) * LANES)
        vtwin = vtpad[pair, pl.ds(start, BAND_K)]
        accs = [jnp.dot(_head_vt(vtwin, even_rows, parity), p_scr[2 * hp + parity], preferred_element_type=F32)
                for parity in range(2)]
        ot_scr[pair, :] = _head_out_t(accs[0], accs[1], even_rows)

    lead = N_HEADS // 2
    for h in range(lead):
        logits(h)
    for h in range(N_HEADS):
        weigh(h)
        if h + lead < N_HEADS:
            logits(h + lead)
        if h % 2 == 1:
            values(h // 2)
    o_ref[...] = _rms(ot_scr[...].T, g_ref[...]).astype(BF16)


def _band_prompt(qxb, kxb, vbt, bias_t, g_out):
    b, s, wx = qxb.shape
    w = vbt.shape[1]
    return pl.pallas_call(
        functools.partial(_band_body, s_len=s),
        grid=(b, s // BAND_Q),
        in_specs=[pl.BlockSpec((None, BAND_Q, wx), lambda i, j: (i, j, 0)),
                  pl.BlockSpec((None, s, wx), lambda i, j: (i, 0, 0)),
                  pl.BlockSpec((None, w, s), lambda i, j: (i, 0, 0)),
                  pl.BlockSpec(bias_t.shape, lambda i, j: (0, 0, 0)),
                  pl.BlockSpec((1, w), lambda i, j: (0, 0))],
        out_specs=pl.BlockSpec((None, BAND_Q, w), lambda i, j: (i, j, 0)),
        out_shape=jax.ShapeDtypeStruct((b, s, w), BF16),
        scratch_shapes=[pltpu.VMEM((LEFT + s, wx), BF16), pltpu.VMEM((w, LEFT + s), BF16),
                        pltpu.VMEM((N_HEADS, BAND_K, BAND_Q), F32), pltpu.VMEM((N_HEADS, BAND_K, BAND_Q), BF16),
                        pltpu.VMEM((w, BAND_Q), F32)],
        compiler_params=pltpu.CompilerParams(
            dimension_semantics=("parallel", "arbitrary"), vmem_limit_bytes=VMEM_LIMIT),
        name="band_prompt",
    )(qxb, kxb, vbt, bias_t, g_out)


def _row_to_col(row):
    n = row.shape[-1]
    r = lax.broadcasted_iota(jnp.int32, (n, n), 0)
    c = lax.broadcasted_iota(jnp.int32, (n, n), 1)
    return jnp.sum(jnp.where(r == c, jnp.broadcast_to(row, (n, n)), 0.0), axis=-1, keepdims=True)


def _fox_sample_body(q_ref, kn_ref, vn_ref, lft_ref, kc_ref, vc_ref, clft_ref, g_ref, o_ref,
                     cct_scr, cn_scr, m_scr, acc_scr, o_scr, s_scr, p_scr, alpha_scr, *, t_new, pt):
    p_idx = pl.program_id(1)
    n_p = pl.num_programs(1)
    even = _pair_masks()

    @pl.when(p_idx == 0)
    def _():
        cct = _lane_cumsum(clft_ref[...])
        cct_scr[...] = cct
        cn_scr[...] = _lane_cumsum(lft_ref[...]) + cct[:, cct.shape[1] - 1:]
        m_scr[...] = jnp.full(m_scr.shape, NEG_INF, F32)
        acc_scr[...] = jnp.zeros(acc_scr.shape, F32)

    start = pl.multiple_of(p_idx * pt, pt)
    pair = lambda h: slice((h // 2) * LANES, (h // 2 + 1) * LANES)
    for h in range(N_HEADS):
        s_scr[h] = lax.dot_general(_head_q(q_ref[:, pair(h)], even, h % 2), kc_ref[:, pair(h)].astype(BF16),
                                   _NT, preferred_element_type=F32)
    for h in range(N_HEADS):
        cq = _row_to_col(cn_scr[h:h + 1, :])
        s = s_scr[h] + cq - cct_scr[h:h + 1, pl.ds(start, pt)]
        m = m_scr[h]
        m_new = jnp.maximum(m, jnp.max(s, axis=-1, keepdims=True))
        p_scr[h] = jnp.exp(s - m_new).astype(BF16)
        alpha_scr[h] = jnp.exp(m - m_new)
        m_scr[h] = m_new
    for h in range(N_HEADS):
        v = _head_v(vc_ref[:, pair(h)].astype(BF16), even, h % 2)
        acc_scr[h] = acc_scr[h] * alpha_scr[h] + jnp.dot(p_scr[h], v, preferred_element_type=F32)

    @pl.when(p_idx == n_p - 1)
    def _():
        row = lax.broadcasted_iota(jnp.int32, (t_new, t_new), 0)
        col = lax.broadcasted_iota(jnp.int32, (t_new, t_new), 1)
        logits = []
        for h in range(N_HEADS):
            cn_row = cn_scr[h:h + 1, :]
            s = lax.dot_general(_head_q(q_ref[:, pair(h)], even, h % 2), kn_ref[:, pair(h)], _NT,
                                preferred_element_type=F32)
            logits.append(jnp.where(col <= row, s + _row_to_col(cn_row) - cn_row, NEG_INF))
        weights, scales = [], []
        for h in range(N_HEADS):
            m = m_scr[h]
            m_new = jnp.maximum(m, jnp.max(logits[h], axis=-1, keepdims=True))
            weights.append(jnp.exp(logits[h] - m_new).astype(BF16))
            scales.append(jnp.exp(m - m_new))
        accs = [acc_scr[h] * scales[h] + jnp.dot(weights[h], _head_v(vn_ref[:, pair(h)], even, h % 2),
                                                 preferred_element_type=F32) for h in range(N_HEADS)]
        for hp in range(N_PAIRS):
            o_scr[:, hp * LANES:(hp + 1) * LANES] = _head_out(accs[2 * hp], accs[2 * hp + 1], even)
        o_ref[...] = _rms(o_scr[...], g_ref[...]).astype(BF16)


def _fox_sample(q, kn, vn, lft, kc, vc, clft, g_out, *, pt):
    b, t, w = q.shape
    p_len = kc.shape[1]
    new = lambda: pl.BlockSpec((None, t, w), lambda i, j: (i, 0, 0))
    cache = lambda: pl.BlockSpec((None, pt, w), lambda i, j: (i, j, 0))
    return pl.pallas_call(
        functools.partial(_fox_sample_body, t_new=t, pt=pt),
        grid=(b, p_len // pt),
        in_specs=[new(), new(), new(),
                  pl.BlockSpec((None, N_HEADS, t), lambda i, j: (i, 0, 0)),
                  cache(), cache(),
                  pl.BlockSpec((None, N_HEADS, p_len), lambda i, j: (i, 0, 0)),
                  pl.BlockSpec((1, w), lambda i, j: (0, 0))],
        out_specs=new(),
        out_shape=jax.ShapeDtypeStruct((b, t, w), BF16),
        scratch_shapes=[pltpu.VMEM((N_HEADS, p_len), F32), pltpu.VMEM((N_HEADS, t), F32),
                        pltpu.VMEM((N_HEADS, t, 1), F32), pltpu.VMEM((N_HEADS, t, LANES), F32),
                        pltpu.VMEM((t, w), F32), pltpu.VMEM((N_HEADS, t, pt), F32),
                        pltpu.VMEM((N_HEADS, t, pt), BF16), pltpu.VMEM((N_HEADS, t, 1), F32)],
        compiler_params=pltpu.CompilerParams(
            dimension_semantics=("parallel", "arbitrary"), vmem_limit_bytes=VMEM_LIMIT),
        name="fox_sample",
    )(q, kn, vn, lft, kc, vc, clft, g_out)


def _band_sample_body(q_ref, kn_ref, vn_ref, knf_ref, vnf_ref, kc_ref, vc_ref, bias_ref, g_ref,
                      o_ref, nk_ref, nv_ref, kcat, vcat, o_scr, *, t_new, bp):
    kcat[0:bp, :] = kc_ref[...].astype(BF16)
    vcat[0:bp, :] = vc_ref[...].astype(BF16)
    kcat[bp:bp + t_new, :] = kn_ref[...]
    vcat[bp:bp + t_new, :] = vn_ref[...]
    nk_ref[0:bp - t_new, :] = kc_ref[t_new:bp, :]
    nv_ref[0:bp - t_new, :] = vc_ref[t_new:bp, :]
    nk_ref[bp - t_new:bp, :] = knf_ref[...]
    nv_ref[bp - t_new:bp, :] = vnf_ref[...]
    even = _pair_masks()
    for hp in range(N_PAIRS):
        lanes = slice(hp * LANES, (hp + 1) * LANES)
        q128 = q_ref[:, lanes]
        k = kcat[:, lanes]
        v = vcat[:, lanes]
        accs = []
        for parity in range(2):
            h = 2 * hp + parity
            s = lax.dot_general(_head_q(q128, even, parity), k, _NT, preferred_element_type=F32)
            s = s + bias_ref[h, 0:t_new, 0:bp + t_new]
            p = jnp.exp(s - jnp.max(s, axis=-1, keepdims=True)).astype(BF16)
            accs.append(jnp.dot(p, _head_v(v, even, parity), preferred_element_type=F32))
        o_scr[:, lanes] = _head_out(accs[0], accs[1], even)
    o_ref[...] = _rms(o_scr[...], g_ref[...]).astype(BF16)


def _band_sample(q, kn, vn, knf, vnf, kc, vc, bias, g_out):
    b, t, w = q.shape
    bp = kc.shape[1]
    assert t == CHUNK and bp == LEFT
    new = lambda: pl.BlockSpec((None, t, w), lambda i: (i, 0, 0))
    buf = lambda: pl.BlockSpec((None, bp, w), lambda i: (i, 0, 0))
    return pl.pallas_call(
        functools.partial(_band_sample_body, t_new=t, bp=bp),
        grid=(b,),
        in_specs=[new(), new(), new(), new(), new(), buf(), buf(),
                  pl.BlockSpec(bias.shape, lambda i: (0, 0, 0)),
                  pl.BlockSpec((1, w), lambda i: (0, 0))],
        out_specs=[new(), buf(), buf()],
        out_shape=[jax.ShapeDtypeStruct((b, t, w), BF16), jax.ShapeDtypeStruct((b, bp, w), F32),
                   jax.ShapeDtypeStruct((b, bp, w), F32)],
        scratch_shapes=[pltpu.VMEM((bp + t, w), BF16), pltpu.VMEM((bp + t, w), BF16), pltpu.VMEM((t, w), F32)],
        compiler_params=pltpu.CompilerParams(dimension_semantics=("parallel",), vmem_limit_bytes=VMEM_LIMIT),
        name="band_sample",
    )(q, kn, vn, knf, vnf, kc, vc, bias, g_out)


N_HEADS_MEM = 4
HEAD_DIM_MEM = 128
W_MEM = N_HEADS_MEM * HEAD_DIM_MEM
MEM_SCALE = HEAD_DIM_MEM ** -0.5


def _memkv_body(m_ref, g_ref, w_ref, kf_ref, vf_ref, k_ref, v_ref):
    h = _rms(m_ref[...], g_ref[...]).astype(BF16)
    z = jnp.dot(h, w_ref[...], preferred_element_type=F32)
    kf_ref[...] = z[:, :W_MEM]
    vf_ref[...] = z[:, W_MEM:]
    k_ref[...] = z[:, :W_MEM].astype(BF16)
    v_ref[...] = z[:, W_MEM:].astype(BF16)


def _mem_kv(mem, g_mem, w_ckv):
    b, n, d = mem.shape
    blk = lambda: pl.BlockSpec((None, n, W_MEM), lambda i: (i, 0, 0))
    return pl.pallas_call(
        _memkv_body,
        grid=(b,),
        in_specs=[pl.BlockSpec((None, n, d), lambda i: (i, 0, 0)),
                  pl.BlockSpec((1, d), lambda i: (0, 0)),
                  pl.BlockSpec(w_ckv.shape, lambda i: (0, 0))],
        out_specs=[blk(), blk(), blk(), blk()],
        out_shape=[jax.ShapeDtypeStruct((b, n, W_MEM), F32)] * 2 + [jax.ShapeDtypeStruct((b, n, W_MEM), BF16)] * 2,
        compiler_params=pltpu.CompilerParams(dimension_semantics=("parallel",), vmem_limit_bytes=VMEM_LIMIT),
        name="mem_kv",
    )(mem, g_mem, w_ckv)


N_GROUPS = 4
EXPERTS_PER_GROUP = 8
N_EXPERTS = N_GROUPS * EXPERTS_PER_GROUP
ROUTE_L2 = N_GROUPS
ROUTE_ROWS = 8
POST_CHAIN = 512
R_EID0, R_EID1, R_RANK0, R_RANK1, R_GATE0, R_GATE1 = range(6)


ROUTE_LOGIT_ROWS = 40


def _row_max(x, mask):
    return jnp.max(jnp.where(mask, x, -jnp.inf), axis=0, keepdims=True)


def _first_row(mask, row):
    return jnp.min(jnp.where(mask, row, LANES), axis=0, keepdims=True)


def _route(logits, row):
    is_l1 = row < N_GROUPS
    m1 = _row_max(logits, is_l1)
    grp = _first_row(is_l1 & (logits == m1), row)
    wg = 1.0 / jnp.sum(jnp.where(is_l1, jnp.exp(logits - m1), 0.0), axis=0, keepdims=True)
    lo = ROUTE_L2 + grp * EXPERTS_PER_GROUP
    in_grp = (row >= lo) & (row < lo + EXPERTS_PER_GROUP)
    v0 = _row_max(logits, in_grp)
    i0 = _first_row(in_grp & (logits == v0), row)
    rest = in_grp & (row != i0)
    v1 = _row_max(logits, rest)
    i1 = _first_row(rest & (logits == v1), row)
    e1 = jnp.exp(v1 - v0)
    den = 1.0 / (1.0 + e1)
    return i0, i1, wg * den, wg * e1 * den


def _post_body(x_ref, a_ref, b_ref, mk_ref, mv_ref, woa_ref, wob_ref, gc_ref, wcq_ref, wco_ref,
               gf_ref, wrt_ref, brt_ref,
               x2_ref, h3_ref, route_ref, routet_ref, cnt_ref, o_scr, *, tm, nsub):
    seq = tm // nsub
    count = jnp.zeros((1, LANES), F32)
    for rows in _chains(tm, POST_CHAIN):
        n = rows.size
        x1 = (x_ref[rows, :] + jnp.dot(a_ref[rows, :], woa_ref[...], preferred_element_type=F32)
              + jnp.dot(b_ref[rows, :], wob_ref[...], preferred_element_type=F32))
        h2 = _rms(x1, gc_ref[...]).astype(BF16)
        qc = (jnp.dot(h2, wcq_ref[...], preferred_element_type=F32) * MEM_SCALE).astype(BF16)
        span = min(seq, n)
        for part in range(n // span):
            sub = (rows.start + part * span) // seq
            rs = slice(part * span, (part + 1) * span)
            orow = pl.ds(rows.start + part * span, span)
            for hm in range(N_HEADS_MEM):
                lanes = slice(hm * HEAD_DIM_MEM, (hm + 1) * HEAD_DIM_MEM)
                s = lax.dot_general(qc[rs, lanes], mk_ref[sub, :, lanes], _NT, preferred_element_type=F32)
                p = jnp.exp(s - jnp.max(s, axis=-1, keepdims=True))
                inv = 1.0 / jnp.sum(p, axis=-1, keepdims=True)
                o_scr[orow, lanes] = jnp.dot(p.astype(BF16), mv_ref[sub, :, lanes], preferred_element_type=F32) * inv
        x2 = x1 + jnp.dot(o_scr[rows, :].astype(BF16), wco_ref[...], preferred_element_type=F32)
        x2_ref[rows, :] = x2
        h3 = _rms(x2, gf_ref[...]).astype(BF16)
        h3_ref[rows, :] = h3

        logits = lax.dot_general(wrt_ref[...], h3, _NT, preferred_element_type=F32) + brt_ref[...]
        row = lax.broadcasted_iota(jnp.int32, (ROUTE_LOGIT_ROWS, n), 0)
        i0, i1, g0, g1 = _route(logits[:ROUTE_LOGIT_ROWS, :], row)
        e0 = i0 - ROUTE_L2
        e1 = i1 - ROUTE_L2
        expert = lax.broadcasted_iota(jnp.int32, (LANES, n), 0)
        hit0 = expert == e0
        hit1 = expert == e1
        onehot = jnp.where(hit0, 1.0, jnp.where(hit1, 1.0, 0.0)).astype(BF16)
        earlier = lax.broadcasted_iota(jnp.int32, (n, n), 0) < lax.broadcasted_iota(jnp.int32, (n, n), 1)
        before = jnp.where(earlier, 1.0, 0.0).astype(BF16)
        seen = jnp.dot(onehot, before, preferred_element_type=F32) + _row_to_col(count)
        rank0 = jnp.sum(jnp.where(hit0, seen, 0.0), axis=0, keepdims=True)
        rank1 = jnp.sum(jnp.where(hit1, seen, 0.0), axis=0, keepdims=True)
        count = count + lax.dot_general(jnp.ones((8, n), BF16), onehot, _NT, preferred_element_type=F32)[0:1, :]

        rec = jnp.concatenate([e0.astype(F32), e1.astype(F32), rank0, rank1, g0, g1, jnp.zeros((2, n), F32)], axis=0)
        routet_ref[:, rows] = rec
        route_ref[rows, :] = jnp.concatenate([rec, jnp.zeros((LANES - ROUTE_ROWS, n), F32)], axis=0).T[:, :ROUTE_ROWS]
    cnt_ref[...] = count


def _post_block(x, a_n, b_n, mk, mv, weights, *, tm):
    b, s, d = x.shape
    if s >= tm:
        nsub, grid = 1, (b, s // tm)
        tok = lambda i, j: (i, j, 0)
        flat = lambda i, j: i * (s // tm) + j
    else:
        nsub = tm // s
        assert b % nsub == 0
        x, a_n, b_n = (t.reshape(b // nsub, tm, t.shape[-1]) for t in (x, a_n, b_n))
        grid = (b // nsub, 1)
        tok = lambda i, j: (i, 0, 0)
        flat = lambda i, j: i
    mem = lambda i, j: (i, 0, 0)
    const = lambda arr: pl.BlockSpec(arr.shape, lambda i, j: (0,) * arr.ndim)
    in_specs = [pl.BlockSpec((None, tm, d), tok),
                pl.BlockSpec((None, tm, W_GROUP), tok), pl.BlockSpec((None, tm, W_GROUP), tok),
                pl.BlockSpec((nsub, mk.shape[1], W_MEM), mem), pl.BlockSpec((nsub, mv.shape[1], W_MEM), mem)]
    in_specs += [const(w) for w in weights]
    n = b * s
    out_shape = [jax.ShapeDtypeStruct((n, d), F32), jax.ShapeDtypeStruct((n, d), BF16),
                 jax.ShapeDtypeStruct((n, ROUTE_ROWS), F32), jax.ShapeDtypeStruct((ROUTE_ROWS, n), F32),
                 jax.ShapeDtypeStruct((n // tm, 1, LANES), F32)]
    out_specs = [pl.BlockSpec((tm, d), lambda i, j: (flat(i, j), 0)),
                 pl.BlockSpec((tm, d), lambda i, j: (flat(i, j), 0)),
                 pl.BlockSpec((tm, ROUTE_ROWS), lambda i, j: (flat(i, j), 0)),
                 pl.BlockSpec((ROUTE_ROWS, tm), lambda i, j: (0, flat(i, j))),
                 pl.BlockSpec((None, 1, LANES), lambda i, j: (flat(i, j), 0, 0))]
    return pl.pallas_call(
        functools.partial(_post_body, tm=tm, nsub=nsub),
        grid=grid,
        in_specs=in_specs,
        out_specs=out_specs,
        out_shape=out_shape,
        scratch_shapes=[pltpu.VMEM((tm, W_MEM), F32)],
        compiler_params=pltpu.CompilerParams(
            dimension_semantics=("parallel", "parallel"), vmem_limit_bytes=VMEM_LIMIT),
        name="post_block",
    )(x, a_n, b_n, mk, mv, *weights)


def _prep_post(w_out, g_cross, w_cq, w_co, g_ffn, w_r1, b_r1, w_r2, b_r2):
    pad = LANES - N_GROUPS - N_EXPERTS
    w_rt = jnp.pad(jnp.concatenate([w_r1, w_r2], axis=1), ((0, 0), (0, pad))).astype(BF16).T
    b_rt = jnp.pad(jnp.concatenate([b_r1, b_r2]).reshape(-1, 1), ((0, pad), (0, 0))).astype(F32)
    return [w_out[:W_GROUP].astype(BF16), w_out[W_GROUP:].astype(BF16), g_cross.reshape(1, -1),
            w_cq.astype(BF16), w_co.astype(BF16), g_ffn.reshape(1, -1), w_rt, b_rt]


D_EXPERT = 512
TOP_K = 2
ROW_TILE = 512
MXU_DIM = 256
RUN_ALIGN = 16
PLAN_TILES = LANES
TILE_TABLE = 2 * LANES


def _local_rows(tm):
    return -(-(TOP_K * tm + N_EXPERTS * (RUN_ALIGN - 1)) // MXU_DIM) * MXU_DIM


def _n_row_tiles(n_tokens, tm):
    rows = n_tokens * TOP_K + (n_tokens // tm) * N_EXPERTS * (RUN_ALIGN - 1) + N_EXPERTS * (ROW_TILE - 1)
    return rows // ROW_TILE


N_CHUNK_LANE = LANES - 1


def _plan_body(cnt_ref, lstart_ref, chunk_ref, offs_ref, te_ref):
    cnt = cnt_ref[...].astype(jnp.int32)
    n16 = ((cnt + (RUN_ALIGN - 1)) & ~(RUN_ALIGN - 1)).astype(F32)
    lend = _lane_cumsum(n16)
    lstart = lend - n16
    earlier = (_lane_cumsum(n16.T) - n16.T).T
    total = jnp.sum(n16, axis=0, keepdims=True).astype(jnp.int32)
    seg = jnp.broadcast_to((total + (ROW_TILE - 1)) & ~(ROW_TILE - 1), (8, LANES)).astype(F32)
    ends = _lane_cumsum(seg)
    offs = ends - seg
    shift = earlier + offs[0:1, :] - lstart
    lstart_ref[...] = lstart.astype(jnp.int32)
    offs_ref[...] = offs[0:1, :].astype(jnp.int32)

    lane = lax.broadcasted_iota(jnp.int32, (PLAN_TILES, LANES), 1)
    local_row = (lane * RUN_ALIGN).astype(F32)
    owner = jnp.zeros((PLAN_TILES, LANES), jnp.int32)
    for e in range(N_EXPERTS):
        owner = owner + jnp.where(lend[:, e:e + 1] <= local_row, 1, 0)
    glob = local_row
    for e in range(N_EXPERTS):
        glob = glob + jnp.where(owner == e, shift[:, e:e + 1], 0.0)
    n_chunks = lend[:, N_EXPERTS - 1:N_EXPERTS] * (1.0 / RUN_ALIGN)
    chunk_ref[...] = jnp.where(lane == N_CHUNK_LANE, n_chunks, glob).astype(jnp.int32)

    tile_start = (lax.broadcasted_iota(jnp.int32, te_ref.shape, 1) * ROW_TILE).astype(F32)
    te = jnp.zeros(te_ref.shape, jnp.int32)
    for e in range(N_EXPERTS):
        end_e = jnp.sum(jnp.where(lane[0:1, :] == e, ends[0:1, :], 0.0), axis=-1, keepdims=True)
        te = te + jnp.where(end_e <= tile_start, 1, 0)
    te_ref[...] = jnp.minimum(te, N_EXPERTS - 1)


def _plan(counts, tm):
    nt = counts.shape[0]
    assert nt <= PLAN_TILES and _local_rows(tm) // RUN_ALIGN <= N_CHUNK_LANE
    cnt = jnp.pad(counts.reshape(nt, LANES), ((0, PLAN_TILES - nt), (0, 0)))
    grid_i32 = jax.ShapeDtypeStruct((PLAN_TILES, LANES), jnp.int32)
    lstart, chunks, offs, te = pl.pallas_call(
        _plan_body,
        out_shape=[grid_i32, grid_i32, jax.ShapeDtypeStruct((1, LANES), jnp.int32),
                   jax.ShapeDtypeStruct((1, TILE_TABLE), jnp.int32)],
        name="moe_plan",
    )(cnt)
    per_tile = lambda t: t[:nt].reshape(nt, 1, LANES)
    return per_tile(lstart), per_tile(chunks), offs.reshape(LANES), te.reshape(TILE_TABLE)


def _for_each_chunk(chunk_ref, fn):
    def body(c, carry):
        fn(pl.multiple_of(c * RUN_ALIGN, RUN_ALIGN), pl.multiple_of(chunk_ref[0, c], RUN_ALIGN))
        return carry

    lax.fori_loop(0, chunk_ref[0, N_CHUNK_LANE], body, None)


def _local_positions_row(rt_ref, lstart_ref):
    pos = []
    for r_eid, r_rank in ((R_EID0, R_RANK0), (R_EID1, R_RANK1)):
        eid = rt_ref[r_eid:r_eid + 1, :].astype(jnp.int32)
        p = rt_ref[r_rank:r_rank + 1, :].astype(jnp.int32)
        for e in range(N_EXPERTS):
            p = p + jnp.where(eid == e, lstart_ref[0, e], 0)
        pos.append(p)
    return pos


def _dispatch_body(offs_ref, lstart_ref, chunk_ref, pchunk_ref,
                   hp_ref, hs_ref, rt_ref, xs_ref, loc, zeros, sems, zsem, *, tm, n_prompt_tiles):
    i = pl.program_id(0)
    n_tiles = xs_ref.shape[0] // ROW_TILE
    half = lax.rem(i, 2)

    @pl.when(i == 0)
    def _():
        zeros[...] = jnp.zeros(zeros.shape, zeros.dtype)
        zero_tile = lambda row: pltpu.make_async_copy(
            zeros, xs_ref.at[pl.ds(pl.multiple_of(row, ROW_TILE), ROW_TILE)], zsem)
        n_used = offs_ref[N_EXPERTS] // ROW_TILE

        def tail(j, carry, op):
            op(zero_tile(j * ROW_TILE))
            return carry

        for op in (lambda c: c.start(), lambda c: c.wait()):
            for e in range(N_EXPERTS):
                @pl.when(offs_ref[e + 1] > offs_ref[e])
                def _():
                    op(zero_tile(offs_ref[e + 1] - ROW_TILE))
            lax.fori_loop(n_used, n_tiles, functools.partial(tail, op=op), None)

    pos0, pos1 = _local_positions_row(rt_ref, lstart_ref)
    used = chunk_ref[0, N_CHUNK_LANE] * RUN_ALIGN
    body_rows = loc.shape[1] - MXU_DIM

    def sort_rows(h_ref, lo, n):
        slot = lo + lax.broadcasted_iota(jnp.int32, (n, tm), 0)
        perm = jnp.where(slot == pos0, 1.0, jnp.where(slot == pos1, 1.0, 0.0)).astype(BF16)
        loc[half, lo:lo + n, :] = jnp.dot(perm, h_ref[...], preferred_element_type=F32).astype(BF16)

    for h_ref, mine in ((hp_ref, i < n_prompt_tiles), (hs_ref, i >= n_prompt_tiles)):
        @pl.when(mine)
        def _():
            sort_rows(h_ref, 0, body_rows)

        @pl.when(mine & (used > body_rows))
        def _():
            sort_rows(h_ref, body_rows, MXU_DIM)

    def chunk(buf, lo, go):
        return pltpu.make_async_copy(loc.at[buf, pl.ds(lo, RUN_ALIGN)], xs_ref.at[pl.ds(go, RUN_ALIGN)],
                                     sems.at[buf])

    @pl.when(i > 0)
    def _():
        _for_each_chunk(pchunk_ref, lambda lo, go: chunk(1 - half, lo, go).wait())

    _for_each_chunk(chunk_ref, lambda lo, go: chunk(half, lo, go).start())

    @pl.when(i == pl.num_programs(0) - 1)
    def _():
        _for_each_chunk(chunk_ref, lambda lo, go: chunk(half, lo, go).wait())


def _dispatch(h3_prompt, h3_sample, route_t, plan, *, tm):
    lstart, chunks, offs, _ = plan
    n_p, n_s, d = h3_prompt.shape[0], h3_sample.shape[0], h3_prompt.shape[-1]
    assert n_p % tm == 0 and n_s % tm == 0
    n = n_p + n_s
    nt = n // tm
    npt = n_p // tm
    n_rows = _n_row_tiles(n, tm) * ROW_TILE
    smem_tile = lambda: pl.BlockSpec((None, 1, LANES), lambda i, offs: (i, 0, 0), memory_space=pltpu.SMEM)
    smem_prev = lambda: pl.BlockSpec((None, 1, LANES), lambda i, offs: (jnp.maximum(i - 1, 0), 0, 0),
                                     memory_space=pltpu.SMEM)
    return pl.pallas_call(
        functools.partial(_dispatch_body, tm=tm, n_prompt_tiles=npt),
        grid_spec=pltpu.PrefetchScalarGridSpec(
            num_scalar_prefetch=1,
            grid=(nt,),
            in_specs=[smem_tile(), smem_tile(), smem_prev(),
                      pl.BlockSpec((tm, d), lambda i, offs: (jnp.minimum(i, npt - 1), 0)),
                      pl.BlockSpec((tm, d), lambda i, offs: (jnp.maximum(i - npt, 0), 0)),
                      pl.BlockSpec((ROUTE_ROWS, tm), lambda i, offs: (0, i))],
            out_specs=pl.BlockSpec(memory_space=pl.ANY),
            scratch_shapes=[pltpu.VMEM((2, _local_rows(tm), d), BF16), pltpu.VMEM((ROW_TILE, d), BF16),
                            pltpu.SemaphoreType.DMA((2,)), pltpu.SemaphoreType.DMA(())]),
        out_shape=jax.ShapeDtypeStruct((n_rows, d), BF16),
        compiler_params=pltpu.CompilerParams(dimension_semantics=("arbitrary",), vmem_limit_bytes=VMEM_LIMIT),
        name="moe_dispatch",
    )(offs, lstart, chunks, chunks, h3_prompt, h3_sample, route_t)


def _experts_body(te_ref, offs_ref, xs_ref, wg_ref, wu_ref, wd_ref, ys_ref,
                  wg_buf, wu_buf, wd_buf, wgu_bf, wd_bf, turn_ref, sems):
    i = pl.program_id(0)
    n_used = offs_ref[N_EXPERTS] // ROW_TILE

    def fetch(expert, half):
        return [pltpu.make_async_copy(src.at[expert], dst.at[half], sems.at[half, k])
                for k, (src, dst) in enumerate(((wg_ref, wg_buf), (wu_ref, wu_buf), (wd_ref, wd_buf)))]

    @pl.when(i == 0)
    def _():
        turn_ref[0] = 0
        for copy in fetch(te_ref[0], 0):
            copy.start()

    @pl.when(i < n_used)
    def _():
        expert = te_ref[i]

        @pl.when((i == 0) | (expert != te_ref[jnp.maximum(i - 1, 0)]))
        def _():
            half = lax.rem(turn_ref[0], 2)
            turn_ref[0] = turn_ref[0] + 1
            for copy in fetch(expert, half):
                copy.wait()
            wgu_bf[:, 0:D_EXPERT] = wg_buf[half].astype(BF16)
            wgu_bf[:, D_EXPERT:] = wu_buf[half].astype(BF16)
            wd_bf[...] = wd_buf[half].astype(BF16)
            following = offs_ref[expert + 1] // ROW_TILE

            @pl.when(following < n_used)
            def _():
                for copy in fetch(te_ref[following], 1 - half):
                    copy.start()

        dot = functools.partial(jnp.dot, preferred_element_type=F32)
        for rows in _chains(ROW_TILE):
            gate_up = dot(xs_ref[rows, :], wgu_bf[...])
            gate, up = gate_up[:, :D_EXPERT], gate_up[:, D_EXPERT:]
            act = (gate * jax.nn.sigmoid(gate) * up).astype(BF16)
            ys_ref[rows, :] = dot(act, wd_bf[...]).astype(BF16)

    @pl.when(i >= n_used)
    def _():
        ys_ref[...] = jnp.zeros(ys_ref.shape, ys_ref.dtype)


def _experts(xs, te, offs, w_gate, w_up, w_down):
    n_rows, d = xs.shape
    last = lambda i, te, offs: jnp.minimum(i, offs[N_EXPERTS] // ROW_TILE - 1)
    hbm = pl.BlockSpec(memory_space=pl.ANY)
    return pl.pallas_call(
        _experts_body,
        grid_spec=pltpu.PrefetchScalarGridSpec(
            num_scalar_prefetch=2,
            grid=(n_rows // ROW_TILE,),
            in_specs=[pl.BlockSpec((ROW_TILE, d), lambda i, te, offs: (last(i, te, offs), 0)), hbm, hbm, hbm],
            out_specs=pl.BlockSpec((ROW_TILE, d), lambda i, te, offs: (i, 0)),
            scratch_shapes=[pltpu.VMEM((2, d, D_EXPERT), F32), pltpu.VMEM((2, d, D_EXPERT), F32),
                            pltpu.VMEM((2, D_EXPERT, d), F32),
                            pltpu.VMEM((d, 2 * D_EXPERT), BF16), pltpu.VMEM((D_EXPERT, d), BF16),
                            pltpu.SMEM((1,), jnp.int32), pltpu.SemaphoreType.DMA((2, 3))]),
        out_shape=jax.ShapeDtypeStruct((n_rows, d), BF16),
        compiler_params=pltpu.CompilerParams(dimension_semantics=("arbitrary",), vmem_limit_bytes=VMEM_LIMIT),
        name="moe_experts",
    )(te, offs, xs, w_gate, w_up, w_down)


def _combine_body(chunk_ref, nchunk_ref, x2p_ref, x2s_ref, route_ref, lsv_ref, ys_ref, g_ref,
                  yp_ref, ysm_ref, loc, sems, *, tm, n_prompt_tiles):
    i = pl.program_id(0)
    half = lax.rem(i, 2)

    def chunk(buf, lo, go):
        return pltpu.make_async_copy(ys_ref.at[pl.ds(go, RUN_ALIGN)], loc.at[buf, pl.ds(lo, RUN_ALIGN)],
                                     sems.at[buf])

    @pl.when(i == 0)
    def _():
        loc[...] = jnp.zeros(loc.shape, loc.dtype)
        _for_each_chunk(chunk_ref, lambda lo, go: chunk(0, lo, go).start())

    @pl.when(i + 1 < pl.num_programs(0))
    def _():
        _for_each_chunk(nchunk_ref, lambda lo, go: chunk(1 - half, lo, go).start())

    _for_each_chunk(chunk_ref, lambda lo, go: chunk(half, lo, go).wait())

    lane = lax.broadcasted_iota(jnp.int32, (tm, LANES), 1)
    picks = []
    for r_eid, r_rank, r_gate in ((R_EID0, R_RANK0, R_GATE0), (R_EID1, R_RANK1, R_GATE1)):
        eid = route_ref[:, r_eid:r_eid + 1].astype(jnp.int32)
        start = jnp.sum(jnp.where(lane == eid, lsv_ref[...], 0), axis=-1, keepdims=True)
        picks.append((route_ref[:, r_rank:r_rank + 1].astype(jnp.int32) + start, route_ref[:, r_gate:r_gate + 1]))

    slot = lax.broadcasted_iota(jnp.int32, (tm, loc.shape[1]), 1)
    weights = jnp.zeros(slot.shape, F32)
    for pos, gate in picks:
        weights = jnp.where(slot == pos, gate, weights)
    moe = jnp.dot(weights.astype(BF16), loc[half], preferred_element_type=F32)

    @pl.when(i < n_prompt_tiles)
    def _():
        yp_ref[...] = _rms(x2p_ref[...] + moe, g_ref[...])

    @pl.when(i >= n_prompt_tiles)
    def _():
        ysm_ref[...] = _rms(x2s_ref[...] + moe, g_ref[...])


def _combine(x2_prompt, x2_sample, route, plan, ys, g_final, *, tm):
    lstart, chunks, _, _ = plan
    (n_p, d), n_s = x2_prompt.shape, x2_sample.shape[0]
    assert n_p % tm == 0 and n_s % tm == 0
    npt = n_p // tm
    nt = npt + n_s // tm
    smem_tile = lambda: pl.BlockSpec((None, 1, LANES), lambda i: (i, 0, 0), memory_space=pltpu.SMEM)
    smem_next = lambda: pl.BlockSpec((None, 1, LANES), lambda i: (jnp.minimum(i + 1, nt - 1), 0, 0),
                                     memory_space=pltpu.SMEM)
    prompt_tile = lambda: pl.BlockSpec((tm, d), lambda i: (jnp.minimum(i, npt - 1), 0))
    sample_tile = lambda: pl.BlockSpec((tm, d), lambda i: (jnp.maximum(i - npt, 0), 0))
    return pl.pallas_call(
        functools.partial(_combine_body, tm=tm, n_prompt_tiles=npt),
        grid=(nt,),
        in_specs=[smem_tile(), smem_next(), prompt_tile(), sample_tile(),
                  pl.BlockSpec((tm, ROUTE_ROWS), lambda i: (i, 0)),
                  pl.BlockSpec((None, 1, LANES), lambda i: (i, 0, 0)),
                  pl.BlockSpec(memory_space=pl.ANY),
                  pl.BlockSpec((1, d), lambda i: (0, 0))],
        out_specs=[prompt_tile(), sample_tile()],
        out_shape=[jax.ShapeDtypeStruct((n_p, d), F32), jax.ShapeDtypeStruct((n_s, d), F32)],
        scratch_shapes=[pltpu.VMEM((2, _local_rows(tm), d), BF16), pltpu.SemaphoreType.DMA((2,))],
        compiler_params=pltpu.CompilerParams(dimension_semantics=("arbitrary",), vmem_limit_bytes=VMEM_LIMIT),
        name="moe_combine",
    )(chunks, chunks, x2_prompt, x2_sample, route, lstart, ys, g_final)


TOKEN_TILE = 512
FOX_Q_TILE = 256
FOX_CACHE_TILE = 1024


def kernel(x_prompt, x_sample, cache_fox_k, cache_fox_v, cache_fox_logf, cache_band_k, cache_band_v, cache_mem_k, cache_mem_v, mem_prompt, g_mix, w_in, b_forget, g_out_fox, g_out_band, rel_table, w_out, g_cross, g_mem, w_cq, w_ck, w_cv, w_co, g_ffn, w_router1, b_router1, w_router2, b_router2, w_exp_gate, w_exp_up, w_exp_down, g_final):
    assert g_mix.shape[0] == 1, "single-layer model"
    bsz, seq, d = x_prompt.shape
    sb, st, _ = x_sample.shape
    n_s = sb * st
    n_mem = mem_prompt.shape[1]
    row = lambda g: g.reshape(1, -1)

    w_pad, bf_pad, g_mix_r = _prep_proj(w_in[0], b_forget[0], g_mix[0])
    g_of, g_ob = row(g_out_fox[0]), row(g_out_band[0])
    bias_t, bias_s = _band_bias(_prep_band_bias_row(rel_table[0]))

    qx, kx, vat, qxb, kxb, vbt, kaf, vaf, kbf, vbf, logf = _proj(
        x_prompt, g_mix_r, w_pad, bf_pad, tm=TOKEN_TILE, prompt=True)
    a_p = _fox_prompt(qx, kx, vat, g_of, tq=FOX_Q_TILE)
    b_p = _band_prompt(qxb, kxb, vbt, bias_t, g_ob)

    s_out = _proj(x_sample.reshape(1, n_s, d), g_mix_r, w_pad, bf_pad, tm=n_s, prompt=False)
    sqa, ska, sva, sqb, skb, svb, skaf, svaf, skbf, svbf = (t.reshape(sb, st, W_GROUP) for t in s_out[:10])
    slogf = s_out[10].reshape(sb, st, N_HEADS)
    slft = s_out[11].reshape(N_HEADS, sb, st).transpose(1, 0, 2)
    past = cache_fox_k.shape[2]
    a_s = _fox_sample(sqa, ska, sva, slft,
                      cache_fox_k[0].reshape(sb, past, W_GROUP), cache_fox_v[0].reshape(sb, past, W_GROUP),
                      cache_fox_logf[0].transpose(0, 2, 1), g_of, pt=FOX_CACHE_TILE)
    bp = cache_band_k.shape[2]
    b_s, nbk, nbv = _band_sample(sqb, skb, svb, skbf, svbf,
                                 cache_band_k[0].reshape(sb, bp, W_GROUP), cache_band_v[0].reshape(sb, bp, W_GROUP),
                                 bias_s, g_ob)

    w_ckv = jnp.concatenate([w_ck[0], w_cv[0]], axis=1).astype(BF16)
    mkf, mvf, mk, mv = _mem_kv(mem_prompt, row(g_mem[0]), w_ckv)
    post_w = _prep_post(w_out[0], g_cross[0], w_cq[0], w_co[0], g_ffn[0],
                        w_router1[0], b_router1[0], w_router2[0], b_router2[0])
    x2_p, h3_p, route_p, routet_p, cnt_p = _post_block(x_prompt, a_p, b_p, mk, mv, post_w, tm=TOKEN_TILE)
    cmk = cache_mem_k[0].reshape(sb, n_mem, W_MEM).astype(BF16)
    cmv = cache_mem_v[0].reshape(sb, n_mem, W_MEM).astype(BF16)
    x2_s, h3_s, route_s, routet_s, cnt_s = _post_block(x_sample, a_s, b_s, cmk, cmv, post_w, tm=TOKEN_TILE)
    route = jnp.concatenate([route_p, route_s], axis=0)
    route_t = jnp.concatenate([routet_p, routet_s], axis=1)

    plan = _plan(jnp.concatenate([cnt_p, cnt_s], axis=0), TOKEN_TILE)
    xs = _dispatch(h3_p, h3_s, route_t, plan, tm=TOKEN_TILE)
    ys = _experts(xs, plan[3], plan[2], w_exp_gate[0], w_exp_up[0], w_exp_down[0])
    y_p, y_s = _combine(x2_p, x2_s, route, plan, ys, row(g_final), tm=TOKEN_TILE)

    heads = lambda t, n: t.reshape(1, n, -1, N_HEADS, HEAD_DIM)
    mem_heads = lambda t: t.reshape(1, bsz, n_mem, N_HEADS_MEM, HEAD_DIM_MEM)
    return (y_p.reshape(bsz, seq, d), y_s.reshape(sb, st, d),
            heads(kaf, bsz), heads(vaf, bsz), logf.reshape(1, bsz, seq, N_HEADS),
            heads(kbf, bsz), heads(vbf, bsz), mem_heads(mkf), mem_heads(mvf),
            heads(skaf, sb), heads(svaf, sb), slogf.reshape(1, sb, st, N_HEADS),
            heads(nbk, sb), heads(nbv, sb))
```

```python
import functools

import jax
import jax.numpy as jnp
from jax import lax
from jax.experimental import pallas as pl
from jax.experimental.pallas import tpu as pltpu

F32 = jnp.float32
BF16 = jnp.bfloat16

D_MODEL = 1024
HEAD_DIM = 64
N_HEADS = 8
W_GROUP = N_HEADS * HEAD_DIM
N_PAIRS = N_HEADS // 2
CHUNK = 64
LEFT_CHUNKS = 8
LEFT = LEFT_CHUNKS * CHUNK
REL_CLIP = 128
EPS = 1e-6
NEG_INF = -1e30
ATTN_SCALE = HEAD_DIM ** -0.5
LANES = 128
PROJ_PAD = 3 * W_GROUP * 2 + LANES
VMEM_LIMIT = 56 * 1024 * 1024


def _rms(x, g):
    ms = jnp.mean(x * x, axis=-1, keepdims=True)
    return x * lax.rsqrt(ms + EPS) * g


def _log_sigmoid(x):
    return -(jnp.maximum(-x, 0.0) + jnp.log1p(jnp.exp(-jnp.abs(x))))


def _lane_cumsum(x):
    n = x.shape[-1]
    lane = lax.broadcasted_iota(jnp.int32, x.shape, 1)
    k = 1
    while k < n:
        x = x + jnp.where(lane >= k, pltpu.roll(x, k, axis=1), 0.0)
        k *= 2
    return x


LOG2E = 1.4426950408889634
SCALE_BASE2 = ATTN_SCALE * LOG2E


def _split3(x):
    hi = x.astype(BF16).astype(F32)
    mid = (x - hi).astype(BF16).astype(F32)
    lo = x - hi - mid
    return hi, mid, lo


def _extra_lane(parity):
    return HEAD_DIM if parity == 0 else 0


def _fox_extras(c3t, hp, tm):
    row = lax.broadcasted_iota(jnp.int32, (8, tm), 0)

    def group(h, q_side):
        hi, mid, lo = (p[h:h + 1, :] for p in c3t)
        if q_side:
            return jnp.where(row < 3, 1.0, jnp.where(row == 3, hi, jnp.where(row == 4, mid, jnp.where(row == 5, lo, 0.0))))
        return jnp.where(row == 0, -hi, jnp.where(row == 1, -mid, jnp.where(row == 2, -lo, jnp.where(row < 6, 1.0, 0.0))))

    gap = jnp.zeros((HEAD_DIM - 8, tm), F32)
    sides = []
    for q_side in (True, False):
        t = jnp.concatenate([group(2 * hp + 1, q_side), gap, group(2 * hp, q_side), gap], axis=0)
        sides.append(t.T)
    return sides


def _head_blocks(x128, extras, lane):
    return (jnp.where(lane < HEAD_DIM, x128, extras).astype(BF16),
            jnp.where(lane >= HEAD_DIM, x128, extras).astype(BF16))


Q_A, K_A, V_A, Q_B, K_B, V_B = range(6)


PROJ_CHAIN = 256


def _chains(tm, chain=PROJ_CHAIN):
    n = max(tm // chain, 1)
    return [pl.ds(i * (tm // n), tm // n) for i in range(n)]


def _proj_common(rows, x_ref, g_ref, w_ref, bf_ref, kaf_ref, vaf_ref, kbf_ref, vbf_ref, logf_ref, keep_tiles):
    s = pl.program_id(1)
    ns = pl.num_programs(1)
    h = _rms(x_ref[rows, :], g_ref[...]).astype(BF16)
    w = W_GROUP
    zf = jnp.dot(h, w_ref[:, 6 * w:6 * w + LANES], preferred_element_type=F32)
    z = [jnp.dot(h, w_ref[:, g * w:(g + 1) * w], preferred_element_type=F32) for g in range(6)]
    kaf_ref[rows, :] = z[K_A]
    vaf_ref[rows, :] = z[V_A]

    @pl.when(s >= ns - keep_tiles)
    def _():
        kbf_ref[rows, :] = z[K_B]
        vbf_ref[rows, :] = z[V_B]

    logf = _log_sigmoid(zf + bf_ref[...])
    logf_ref[rows, :] = logf[:, :N_HEADS]
    return z, logf


def _proj_prompt_body(x_ref, g_ref, w_ref, bf_ref, qx_ref, kx_ref, vat_ref, qxb_ref, kxb_ref, vbt_ref,
                      kaf_ref, vaf_ref, kbf_ref, vbf_ref, logf_ref, carry_ref, *, tm, keep_tiles):
    @pl.when(pl.program_id(1) == 0)
    def _():
        carry_ref[...] = jnp.zeros_like(carry_ref)

    for rows in _chains(tm):
        n = rows.size
        z, logf = _proj_common(rows, x_ref, g_ref, w_ref, bf_ref, kaf_ref, vaf_ref, kbf_ref, vbf_ref, logf_ref,
                               keep_tiles)
        vat_ref[:, rows] = z[V_A].T.astype(BF16)
        vbt_ref[:, rows] = z[V_B].T.astype(BF16)
        ct = _lane_cumsum(logf.T[:N_HEADS, :]) + carry_ref[:, 0:1]
        carry_ref[...] = jnp.broadcast_to(ct[:, n - 1:n], carry_ref.shape)
        c3t = _split3(ct * LOG2E)
        lane = lax.broadcasted_iota(jnp.int32, (n, LANES), 1)
        band_q_extras = jnp.where((lane == _extra_lane(0)) | (lane == _extra_lane(1)), 1.0, 0.0)
        band_k_extras = jnp.zeros((n, LANES), F32)
        for hp in range(N_PAIRS):
            blocks = slice(2 * hp * LANES, 2 * (hp + 1) * LANES)
            blk = lambda group, hp=hp, z=z: z[group][:, hp * LANES:(hp + 1) * LANES]
            q_extras, k_extras = _fox_extras(c3t, hp, n)
            qx_ref[rows, blocks] = jnp.concatenate(_head_blocks(blk(Q_A) * SCALE_BASE2, q_extras, lane), axis=1)
            kx_ref[rows, blocks] = jnp.concatenate(_head_blocks(blk(K_A), k_extras, lane), axis=1)
            qxb_ref[rows, blocks] = jnp.concatenate(_head_blocks(blk(Q_B) * SCALE_BASE2, band_q_extras, lane), axis=1)
            kxb_ref[rows, blocks] = jnp.concatenate(_head_blocks(blk(K_B), band_k_extras, lane), axis=1)


def _proj_sample_body(x_ref, g_ref, w_ref, bf_ref, qa_ref, ka_ref, va_ref, qb_ref, kb_ref, vb_ref,
                      kaf_ref, vaf_ref, kbf_ref, vbf_ref, logf_ref, lt_ref, *, tm, keep_tiles):
    for rows in _chains(tm):
        z, logf = _proj_common(rows, x_ref, g_ref, w_ref, bf_ref, kaf_ref, vaf_ref, kbf_ref, vbf_ref, logf_ref,
                               keep_tiles)
        qa_ref[rows, :] = (z[Q_A] * ATTN_SCALE).astype(BF16)
        ka_ref[rows, :] = z[K_A].astype(BF16)
        va_ref[rows, :] = z[V_A].astype(BF16)
        qb_ref[rows, :] = (z[Q_B] * ATTN_SCALE).astype(BF16)
        kb_ref[rows, :] = z[K_B].astype(BF16)
        vb_ref[rows, :] = z[V_B].astype(BF16)
        lt_ref[:, rows] = logf.T[:N_HEADS, :]


def _proj(x, g_mix, w_pad, bf_pad, *, tm, prompt):
    b, s, d = x.shape
    ns = s // tm
    keep = min(LEFT, s)
    assert s % tm == 0 and keep % tm == 0
    keep_tiles = keep // tm
    row = pl.BlockSpec((None, tm, W_GROUP), lambda i, j: (i, j, 0))
    wide = pl.BlockSpec((None, tm, N_HEADS * LANES), lambda i, j: (i, j, 0))
    col = pl.BlockSpec((None, W_GROUP, tm), lambda i, j: (i, 0, j))
    keep_spec = pl.BlockSpec((None, tm, W_GROUP), lambda i, j: (i, jnp.maximum(j - (ns - keep_tiles), 0), 0))
    heads_row = pl.BlockSpec((None, tm, N_HEADS), lambda i, j: (i, j, 0))
    heads_col = pl.BlockSpec((None, N_HEADS, tm), lambda i, j: (i, 0, j))
    const = lambda shape: pl.BlockSpec(shape, lambda i, j: (0,) * len(shape))
    rows_bf = jax.ShapeDtypeStruct((b, s, W_GROUP), BF16)
    wide_bf = jax.ShapeDtypeStruct((b, s, N_HEADS * LANES), BF16)
    cols_bf = jax.ShapeDtypeStruct((b, W_GROUP, s), BF16)
    f32_tail = [jax.ShapeDtypeStruct((b, s, W_GROUP), F32)] * 2
    f32_tail += [jax.ShapeDtypeStruct((b, keep, W_GROUP), F32)] * 2
    f32_tail += [jax.ShapeDtypeStruct((b, s, N_HEADS), F32)]
    tail_specs = [row, row, keep_spec, keep_spec, heads_row]
    if prompt:
        body = functools.partial(_proj_prompt_body, tm=tm, keep_tiles=keep_tiles)
        out_shape = [wide_bf, wide_bf, cols_bf, wide_bf, wide_bf, cols_bf] + f32_tail
        out_specs = [wide, wide, col, wide, wide, col] + tail_specs
        scratch = [pltpu.VMEM((N_HEADS, LANES), F32)]
    else:
        body = functools.partial(_proj_sample_body, tm=tm, keep_tiles=keep_tiles)
        out_shape = [rows_bf] * 6 + f32_tail + [jax.ShapeDtypeStruct((b, N_HEADS, s), F32)]
        out_specs = [row] * 6 + tail_specs + [heads_col]
        scratch = []
    return pl.pallas_call(
        body,
        grid=(b, ns),
        in_specs=[pl.BlockSpec((None, tm, d), lambda i, j: (i, j, 0)),
                  const((1, d)), const(w_pad.shape), const((1, LANES))],
        out_specs=out_specs,
        out_shape=out_shape,
        scratch_shapes=scratch,
        compiler_params=pltpu.CompilerParams(
            dimension_semantics=("parallel", "arbitrary"), vmem_limit_bytes=VMEM_LIMIT),
        name="proj",
    )(x, g_mix, w_pad, bf_pad)


def _prep_proj(w_in, b_forget, g_mix):
    cols = w_in.shape[-1]
    w_pad = jnp.pad(w_in, ((0, 0), (0, PROJ_PAD - cols))).astype(BF16)
    bf_pad = jnp.pad(b_forget.reshape(1, -1), ((0, 0), (0, LANES - N_HEADS))).astype(F32)
    return w_pad, bf_pad, g_mix.reshape(1, -1)


def _pair_masks():
    lane = lax.broadcasted_iota(jnp.int32, (1, LANES), 1)
    return lane < HEAD_DIM


def _head_q(q128, even_lanes, parity):
    keep = even_lanes if parity == 0 else jnp.logical_not(even_lanes)
    return jnp.where(keep, q128, jnp.zeros_like(q128))


def _head_v(v128, even_lanes, parity):
    keep = even_lanes if parity == 0 else jnp.logical_not(even_lanes)
    return jnp.where(keep, v128, jnp.ones_like(v128))


def _head_out(acc_even, acc_odd, even_lanes):
    inv_e = 1.0 / acc_even[:, HEAD_DIM:HEAD_DIM + 1]
    inv_o = 1.0 / acc_odd[:, 0:1]
    return jnp.where(even_lanes, acc_even * inv_e, acc_odd * inv_o)


_NT = (((1,), (1,)), ((), ()))


def _pair_rows():
    row = lax.broadcasted_iota(jnp.int32, (LANES, 1), 0)
    return row < HEAD_DIM


def _head_vt(vt128, even_rows, parity):
    keep = even_rows if parity == 0 else jnp.logical_not(even_rows)
    return jnp.where(keep, vt128, jnp.ones_like(vt128))


def _head_out_t(acc_even, acc_odd, even_rows):
    inv_e = 1.0 / acc_even[HEAD_DIM:HEAD_DIM + 1, :]
    inv_o = 1.0 / acc_odd[0:1, :]
    return jnp.where(even_rows, acc_even * inv_e, acc_odd * inv_o)


def _fox_body(qx_ref, kx_ref, vt_ref, g_ref, o_ref, s_scr, top_scr, p_scr, m_scr, alpha_scr, acc_scr, ot_scr,
              *, tq):
    qi = pl.program_id(1)
    even_rows = _pair_rows()
    m_scr[...] = jnp.full(m_scr.shape, NEG_INF, F32)
    acc_scr[...] = jnp.zeros(acc_scr.shape, F32)
    key = lax.broadcasted_iota(jnp.int32, (tq, tq), 0)
    qry = lax.broadcasted_iota(jnp.int32, (tq, tq), 1)
    causal = key <= qry

    def logits(j, half):
        start = pl.multiple_of(j * tq, tq)
        for h in range(N_HEADS):
            head = slice(h * LANES, (h + 1) * LANES)
            st = lax.dot_general(kx_ref[pl.ds(start, tq), head], qx_ref[:, head], _NT,
                                 preferred_element_type=F32)
            s_scr[half, h] = st
            top_scr[half, h:h + 1, :] = jnp.max(st, axis=0, keepdims=True)

    def weigh(j, half, masked):
        start = pl.multiple_of(j * tq, tq)
        for h in range(N_HEADS):
            st = s_scr[half, h]
            if masked:
                st = jnp.where(causal, st, NEG_INF)
                top = jnp.max(st, axis=0, keepdims=True)
            else:
                top = top_scr[half, h:h + 1, :]
            m_old = m_scr[h:h + 1, :]
            m_new = jnp.maximum(m_old, top)
            p_scr[h] = jnp.exp2(st - m_new).astype(BF16)
            alpha_scr[h:h + 1, :] = jnp.exp2(m_old - m_new)
            m_scr[h:h + 1, :] = m_new
        for h in range(N_HEADS):
            pair = slice((h // 2) * LANES, (h // 2 + 1) * LANES)
            vt = _head_vt(vt_ref[pair, pl.ds(start, tq)], even_rows, h % 2)
            acc_scr[h] = acc_scr[h] * alpha_scr[h:h + 1, :] + jnp.dot(vt, p_scr[h], preferred_element_type=F32)

    def step(j, cur, nxt):
        logits(j + 1, nxt)
        weigh(j, cur, False)

    def body(jj, carry):
        step(2 * jj, 0, 1)
        step(2 * jj + 1, 1, 0)
        return carry

    logits(0, 0)
    lax.fori_loop(0, qi // 2, body, None)
    odd = lax.rem(qi, 2) == 1

    @pl.when(odd)
    def _():
        step(qi - 1, 0, 1)
        weigh(qi, 1, True)

    @pl.when(jnp.logical_not(odd))
    def _():
        weigh(qi, 0, True)
    for hp in range(N_PAIRS):
        ot_scr[hp * LANES:(hp + 1) * LANES, :] = _head_out_t(acc_scr[2 * hp], acc_scr[2 * hp + 1], even_rows)
    o_ref[...] = _rms(ot_scr[...].T, g_ref[...]).astype(BF16)


def _fox_prompt(qx, kx, vat, g_out, *, tq):
    b, s, wx = qx.shape
    w = vat.shape[1]
    return pl.pallas_call(
        functools.partial(_fox_body, tq=tq),
        grid=(b, s // tq),
        in_specs=[pl.BlockSpec((None, tq, wx), lambda i, j: (i, j, 0)),
                  pl.BlockSpec((None, s, wx), lambda i, j: (i, 0, 0)),
                  pl.BlockSpec((None, w, s), lambda i, j: (i, 0, 0)),
                  pl.BlockSpec((1, w), lambda i, j: (0, 0))],
        out_specs=pl.BlockSpec((None, tq, w), lambda i, j: (i, j, 0)),
        out_shape=jax.ShapeDtypeStruct((b, s, w), BF16),
        scratch_shapes=[pltpu.VMEM((2, N_HEADS, tq, tq), F32), pltpu.VMEM((2, N_HEADS, tq), F32),
                        pltpu.VMEM((N_HEADS, tq, tq), BF16),
                        pltpu.VMEM((N_HEADS, tq), F32), pltpu.VMEM((N_HEADS, tq), F32),
                        pltpu.VMEM((N_HEADS, LANES, tq), F32), pltpu.VMEM((w, tq), F32)],
        compiler_params=pltpu.CompilerParams(
            dimension_semantics=("parallel", "arbitrary"), vmem_limit_bytes=VMEM_LIMIT),
        name="fox_prompt",
    )(qx, kx, vat, g_out)


BAND_CHUNKS = 4
BAND_Q = BAND_CHUNKS * CHUNK
BAND_K = (LEFT_CHUNKS + BAND_CHUNKS) * CHUNK
BIAS_ROW = BAND_K + BAND_Q
BAND_K_SAMPLE = (LEFT + CHUNK + LANES - 1) // LANES * LANES


def _prep_band_bias_row(rel_table):
    pivot = LEFT + BAND_Q
    n_hi = pivot - REL_CLIP + 1
    n_mid = min(2 * REL_CLIP, BIAS_ROW - n_hi)
    n_lo = BIAS_ROW - n_hi - n_mid
    parts = [jnp.broadcast_to(rel_table[2 * REL_CLIP:], (n_hi, N_HEADS)),
             rel_table[2 * REL_CLIP - 1::-1][:n_mid],
             jnp.broadcast_to(rel_table[:1], (n_lo, N_HEADS))]
    return jnp.concatenate(parts, axis=0).T.reshape(N_HEADS, 1, BIAS_ROW)


def _band_bias_body(row_ref, bt_ref, bs_ref):
    rows = jnp.broadcast_to(row_ref[...], (BAND_Q, BIAS_ROW))
    skew = pltpu.roll(rows, 0, axis=1, stride=1, stride_axis=0)
    bias = skew[:, BAND_Q:]
    bs_ref[...] = bias[:CHUNK, :BAND_K_SAMPLE]
    qc = lax.broadcasted_iota(jnp.int32, (BAND_Q, BAND_K), 0) // CHUNK
    kc = lax.broadcasted_iota(jnp.int32, (BAND_Q, BAND_K), 1) // CHUNK
    bt_ref[...] = jnp.where((kc >= qc) & (kc <= qc + LEFT_CHUNKS), bias * LOG2E, NEG_INF).T


def _band_bias(bias_row):
    return pl.pallas_call(
        _band_bias_body,
        grid=(N_HEADS,),
        in_specs=[pl.BlockSpec((None, 1, BIAS_ROW), lambda h: (h, 0, 0))],
        out_specs=[pl.BlockSpec((None, BAND_K, BAND_Q), lambda h: (h, 0, 0)),
                   pl.BlockSpec((None, CHUNK, BAND_K_SAMPLE), lambda h: (h, 0, 0))],
        out_shape=[jax.ShapeDtypeStruct((N_HEADS, BAND_K, BAND_Q), F32),
                   jax.ShapeDtypeStruct((N_HEADS, CHUNK, BAND_K_SAMPLE), F32)],
        name="band_bias",
    )(bias_row)


def _band_body(qx_ref, kx_ref, vt_ref, bias_ref, g_ref, o_ref, kpad, vtpad, s_scr, p_scr, ot_scr, *, s_len):
    step = pl.program_id(1)

    @pl.when(step == 0)
    def _():
        lane = lax.broadcasted_iota(jnp.int32, (LEFT, 2 * LANES), 1)
        flags = (lane == _extra_lane(0)) | (lane == LANES + _extra_lane(1))
        pad_pair = jnp.where(flags, NEG_INF, 0.0).astype(BF16)
        for hp in range(N_PAIRS):
            kpad[0:LEFT, 2 * hp * LANES:2 * (hp + 1) * LANES] = pad_pair
        vtpad[:, 0:LEFT] = jnp.zeros((W_GROUP, LEFT), BF16)
        kpad[LEFT:LEFT + s_len, :] = kx_ref[...]
        vtpad[:, LEFT:LEFT + s_len] = vt_ref[...]

    even_rows = _pair_rows()
    start = pl.multiple_of(step * BAND_Q, BAND_Q)
    def logits(h):
        head = slice(h * LANES, (h + 1) * LANES)
        s_scr[h] = bias_ref[h] + lax.dot_general(kpad[pl.ds(start, BAND_K), head], qx_ref[:, head], _NT,
                                                 preferred_element_type=F32)

    def weigh(h):
        st = s_scr[h]
        p_scr[h] = jnp.exp2(st - jnp.max(st, axis=0, keepdims=True)).astype(BF16)

    def values(hp):
        pair = slice(hp * LANES, (hp + 1) * LANES)
        vtwin = vtpad[pair, pl.ds(start, BAND_K)]
        accs = [jnp.dot(_head_vt(vtwin, even_rows, parity), p_scr[2 * hp + parity], preferred_element_type=F32)
                for parity in range(2)]
        ot_scr[pair, :] = _head_out_t(accs[0], accs[1], even_rows)

    lead = N_HEADS // 2
    for h in range(lead):
        logits(h)
    for h in range(N_HEADS):
        weigh(h)
        if h + lead < N_HEADS:
            logits(h + lead)
        if h % 2 == 1:
            values(h // 2)
    o_ref[...] = _rms(ot_scr[...].T, g_ref[...]).astype(BF16)


def _band_prompt(qxb, kxb, vbt, bias_t, g_out):
    b, s, wx = qxb.shape
    w = vbt.shape[1]
    return pl.pallas_call(
        functools.partial(_band_body, s_len=s),
        grid=(b, s // BAND_Q),
        in_specs=[pl.BlockSpec((None, BAND_Q, wx), lambda i, j: (i, j, 0)),
                  pl.BlockSpec((None, s, wx), lambda i, j: (i, 0, 0)),
                  pl.BlockSpec((None, w, s), lambda i, j: (i, 0, 0)),
                  pl.BlockSpec(bias_t.shape, lambda i, j: (0, 0, 0)),
                  pl.BlockSpec((1, w), lambda i, j: (0, 0))],
        out_specs=pl.BlockSpec((None, BAND_Q, w), lambda i, j: (i, j, 0)),
        out_shape=jax.ShapeDtypeStruct((b, s, w), BF16),
        scratch_shapes=[pltpu.VMEM((LEFT + s, wx), BF16), pltpu.VMEM((w, LEFT + s), BF16),
                        pltpu.VMEM((N_HEADS, BAND_K, BAND_Q), F32), pltpu.VMEM((N_HEADS, BAND_K, BAND_Q), BF16),
                        pltpu.VMEM((w, BAND_Q), F32)],
        compiler_params=pltpu.CompilerParams(
            dimension_semantics=("parallel", "arbitrary"), vmem_limit_bytes=VMEM_LIMIT),
        name="band_prompt",
    )(qxb, kxb, vbt, bias_t, g_out)


def _row_to_col(row):
    n = row.shape[-1]
    r = lax.broadcasted_iota(jnp.int32, (n, n), 0)
    c = lax.broadcasted_iota(jnp.int32, (n, n), 1)
    return jnp.sum(jnp.where(r == c, jnp.broadcast_to(row, (n, n)), 0.0), axis=-1, keepdims=True)


def _fox_sample_body(q_ref, kn_ref, vn_ref, lft_ref, kc_ref, vc_ref, clft_ref, g_ref, o_ref,
                     cct_scr, cn_scr, m_scr, acc_scr, o_scr, s_scr, p_scr, alpha_scr, *, t_new, pt):
    p_idx = pl.program_id(1)
    n_p = pl.num_programs(1)
    even = _pair_masks()

    @pl.when(p_idx == 0)
    def _():
        cct = _lane_cumsum(clft_ref[...])
        cct_scr[...] = cct
        cn_scr[...] = _lane_cumsum(lft_ref[...]) + cct[:, cct.shape[1] - 1:]
        m_scr[...] = jnp.full(m_scr.shape, NEG_INF, F32)
        acc_scr[...] = jnp.zeros(acc_scr.shape, F32)

    start = pl.multiple_of(p_idx * pt, pt)
    pair = lambda h: slice((h // 2) * LANES, (h // 2 + 1) * LANES)
    for h in range(N_HEADS):
        s_scr[h] = lax.dot_general(_head_q(q_ref[:, pair(h)], even, h % 2), kc_ref[:, pair(h)].astype(BF16),
                                   _NT, preferred_element_type=F32)
    for h in range(N_HEADS):
        cq = _row_to_col(cn_scr[h:h + 1, :])
        s = s_scr[h] + cq - cct_scr[h:h + 1, pl.ds(start, pt)]
        m = m_scr[h]
        m_new = jnp.maximum(m, jnp.max(s, axis=-1, keepdims=True))
        p_scr[h] = jnp.exp(s - m_new).astype(BF16)
        alpha_scr[h] = jnp.exp(m - m_new)
        m_scr[h] = m_new
    for h in range(N_HEADS):
        v = _head_v(vc_ref[:, pair(h)].astype(BF16), even, h % 2)
        acc_scr[h] = acc_scr[h] * alpha_scr[h] + jnp.dot(p_scr[h], v, preferred_element_type=F32)

    @pl.when(p_idx == n_p - 1)
    def _():
        row = lax.broadcasted_iota(jnp.int32, (t_new, t_new), 0)
        col = lax.broadcasted_iota(jnp.int32, (t_new, t_new), 1)
        logits = []
        for h in range(N_HEADS):
            cn_row = cn_scr[h:h + 1, :]
            s = lax.dot_general(_head_q(q_ref[:, pair(h)], even, h % 2), kn_ref[:, pair(h)], _NT,
                                preferred_element_type=F32)
            logits.append(jnp.where(col <= row, s + _row_to_col(cn_row) - cn_row, NEG_INF))
        weights, scales = [], []
        for h in range(N_HEADS):
            m = m_scr[h]
            m_new = jnp.maximum(m, jnp.max(logits[h], axis=-1, keepdims=True))
            weights.append(jnp.exp(logits[h] - m_new).astype(BF16))
            scales.append(jnp.exp(m - m_new))
        accs = [acc_scr[h] * scales[h] + jnp.dot(weights[h], _head_v(vn_ref[:, pair(h)], even, h % 2),
                                                 preferred_element_type=F32) for h in range(N_HEADS)]
        for hp in range(N_PAIRS):
            o_scr[:, hp * LANES:(hp + 1) * LANES] = _head_out(accs[2 * hp], accs[2 * hp + 1], even)
        o_ref[...] = _rms(o_scr[...], g_ref[...]).astype(BF16)


def _fox_sample(q, kn, vn, lft, kc, vc, clft, g_out, *, pt):
    b, t, w = q.shape
    p_len = kc.shape[1]
    new = lambda: pl.BlockSpec((None, t, w), lambda i, j: (i, 0, 0))
    cache = lambda: pl.BlockSpec((None, pt, w), lambda i, j: (i, j, 0))
    return pl.pallas_call(
        functools.partial(_fox_sample_body, t_new=t, pt=pt),
        grid=(b, p_len // pt),
        in_specs=[new(), new(), new(),
                  pl.BlockSpec((None, N_HEADS, t), lambda i, j: (i, 0, 0)),
                  cache(), cache(),
                  pl.BlockSpec((None, N_HEADS, p_len), lambda i, j: (i, 0, 0)),
                  pl.BlockSpec((1, w), lambda i, j: (0, 0))],
        out_specs=new(),
        out_shape=jax.ShapeDtypeStruct((b, t, w), BF16),
        scratch_shapes=[pltpu.VMEM((N_HEADS, p_len), F32), pltpu.VMEM((N_HEADS, t), F32),
                        pltpu.VMEM((N_HEADS, t, 1), F32), pltpu.VMEM((N_HEADS, t, LANES), F32),
                        pltpu.VMEM((t, w), F32), pltpu.VMEM((N_HEADS, t, pt), F32),
                        pltpu.VMEM((N_HEADS, t, pt), BF16), pltpu.VMEM((N_HEADS, t, 1), F32)],
        compiler_params=pltpu.CompilerParams(
            dimension_semantics=("parallel", "arbitrary"), vmem_limit_bytes=VMEM_LIMIT),
        name="fox_sample",
    )(q, kn, vn, lft, kc, vc, clft, g_out)


def _band_sample_body(q_ref, kn_ref, vn_ref, knf_ref, vnf_ref, kc_ref, vc_ref, bias_ref, g_ref,
                      o_ref, nk_ref, nv_ref, kcat, vcat, o_scr, *, t_new, bp):
    kcat[0:bp, :] = kc_ref[...].astype(BF16)
    vcat[0:bp, :] = vc_ref[...].astype(BF16)
    kcat[bp:bp + t_new, :] = kn_ref[...]
    vcat[bp:bp + t_new, :] = vn_ref[...]
    nk_ref[0:bp - t_new, :] = kc_ref[t_new:bp, :]
    nv_ref[0:bp - t_new, :] = vc_ref[t_new:bp, :]
    nk_ref[bp - t_new:bp, :] = knf_ref[...]
    nv_ref[bp - t_new:bp, :] = vnf_ref[...]
    even = _pair_masks()
    for hp in range(N_PAIRS):
        lanes = slice(hp * LANES, (hp + 1) * LANES)
        q128 = q_ref[:, lanes]
        k = kcat[:, lanes]
        v = vcat[:, lanes]
        accs = []
        for parity in range(2):
            h = 2 * hp + parity
            s = lax.dot_general(_head_q(q128, even, parity), k, _NT, preferred_element_type=F32)
            s = s + bias_ref[h, 0:t_new, 0:bp + t_new]
            p = jnp.exp(s - jnp.max(s, axis=-1, keepdims=True)).astype(BF16)
            accs.append(jnp.dot(p, _head_v(v, even, parity), preferred_element_type=F32))
        o_scr[:, lanes] = _head_out(accs[0], accs[1], even)
    o_ref[...] = _rms(o_scr[...], g_ref[...]).astype(BF16)


def _band_sample(q, kn, vn, knf, vnf, kc, vc, bias, g_out):
    b, t, w = q.shape
    bp = kc.shape[1]
    assert t == CHUNK and bp == LEFT
    new = lambda: pl.BlockSpec((None, t, w), lambda i: (i, 0, 0))
    buf = lambda: pl.BlockSpec((None, bp, w), lambda i: (i, 0, 0))
    return pl.pallas_call(
        functools.partial(_band_sample_body, t_new=t, bp=bp),
        grid=(b,),
        in_specs=[new(), new(), new(), new(), new(), buf(), buf(),
                  pl.BlockSpec(bias.shape, lambda i: (0, 0, 0)),
                  pl.BlockSpec((1, w), lambda i: (0, 0))],
        out_specs=[new(), buf(), buf()],
        out_shape=[jax.ShapeDtypeStruct((b, t, w), BF16), jax.ShapeDtypeStruct((b, bp, w), F32),
                   jax.ShapeDtypeStruct((b, bp, w), F32)],
        scratch_shapes=[pltpu.VMEM((bp + t, w), BF16), pltpu.VMEM((bp + t, w), BF16), pltpu.VMEM((t, w), F32)],
        compiler_params=pltpu.CompilerParams(dimension_semantics=("parallel",), vmem_limit_bytes=VMEM_LIMIT),
        name="band_sample",
    )(q, kn, vn, knf, vnf, kc, vc, bias, g_out)


N_HEADS_MEM = 4
HEAD_DIM_MEM = 128
W_MEM = N_HEADS_MEM * HEAD_DIM_MEM
MEM_SCALE = HEAD_DIM_MEM ** -0.5


def _memkv_body(m_ref, g_ref, w_ref, kf_ref, vf_ref, k_ref, v_ref):
    h = _rms(m_ref[...], g_ref[...]).astype(BF16)
    z = jnp.dot(h, w_ref[...], preferred_element_type=F32)
    kf_ref[...] = z[:, :W_MEM]
    vf_ref[...] = z[:, W_MEM:]
    k_ref[...] = z[:, :W_MEM].astype(BF16)
    v_ref[...] = z[:, W_MEM:].astype(BF16)


def _mem_kv(mem, g_mem, w_ckv):
    b, n, d = mem.shape
    blk = lambda: pl.BlockSpec((None, n, W_MEM), lambda i: (i, 0, 0))
    return pl.pallas_call(
        _memkv_body,
        grid=(b,),
        in_specs=[pl.BlockSpec((None, n, d), lambda i: (i, 0, 0)),
                  pl.BlockSpec((1, d), lambda i: (0, 0)),
                  pl.BlockSpec(w_ckv.shape, lambda i: (0, 0))],
        out_specs=[blk(), blk(), blk(), blk()],
        out_shape=[jax.ShapeDtypeStruct((b, n, W_MEM), F32)] * 2 + [jax.ShapeDtypeStruct((b, n, W_MEM), BF16)] * 2,
        compiler_params=pltpu.CompilerParams(dimension_semantics=("parallel",), vmem_limit_bytes=VMEM_LIMIT),
        name="mem_kv",
    )(mem, g_mem, w_ckv)


N_GROUPS = 4
EXPERTS_PER_GROUP = 8
N_EXPERTS = N_GROUPS * EXPERTS_PER_GROUP
ROUTE_L2 = N_GROUPS
ROUTE_ROWS = 8
POST_CHAIN = 512
R_EID0, R_EID1, R_RANK0, R_RANK1, R_GATE0, R_GATE1 = range(6)


ROUTE_LOGIT_ROWS = 40


def _row_max(x, mask):
    return jnp.max(jnp.where(mask, x, -jnp.inf), axis=0, keepdims=True)


def _first_row(mask, row):
    return jnp.min(jnp.where(mask, row, LANES), axis=0, keepdims=True)


def _route(logits, row):
    is_l1 = row < N_GROUPS
    m1 = _row_max(logits, is_l1)
    grp = _first_row(is_l1 & (logits == m1), row)
    wg = 1.0 / jnp.sum(jnp.where(is_l1, jnp.exp(logits - m1), 0.0), axis=0, keepdims=True)
    lo = ROUTE_L2 + grp * EXPERTS_PER_GROUP
    in_grp = (row >= lo) & (row < lo + EXPERTS_PER_GROUP)
    v0 = _row_max(logits, in_grp)
    i0 = _first_row(in_grp & (logits == v0), row)
    rest = in_grp & (row != i0)
    v1 = _row_max(logits, rest)
    i1 = _first_row(rest & (logits == v1), row)
    e1 = jnp.exp(v1 - v0)
    den = 1.0 / (1.0 + e1)
    return i0, i1, wg * den, wg * e1 * den


def _post_body(x_ref, a_ref, b_ref, mk_ref, mv_ref, woa_ref, wob_ref, gc_ref, wcq_ref, wco_ref,
               gf_ref, wrt_ref, brt_ref,
               x2_ref, h3_ref, route_ref, routet_ref, cnt_ref, o_scr, *, tm, nsub):
    seq = tm // nsub
    count = jnp.zeros((1, LANES), F32)
    for rows in _chains(tm, POST_CHAIN):
        n = rows.size
        x1 = (x_ref[rows, :] + jnp.dot(a_ref[rows, :], woa_ref[...], preferred_element_type=F32)
              + jnp.dot(b_ref[rows, :], wob_ref[...], preferred_element_type=F32))
        h2 = _rms(x1, gc_ref[...]).astype(BF16)
        qc = (jnp.dot(h2, wcq_ref[...], preferred_element_type=F32) * MEM_SCALE).astype(BF16)
        span = min(seq, n)
        for part in range(n // span):
            sub = (rows.start + part * span) // seq
            rs = slice(part * span, (part + 1) * span)
            orow = pl.ds(rows.start + part * span, span)
            for hm in range(N_HEADS_MEM):
                lanes = slice(hm * HEAD_DIM_MEM, (hm + 1) * HEAD_DIM_MEM)
                s = lax.dot_general(qc[rs, lanes], mk_ref[sub, :, lanes], _NT, preferred_element_type=F32)
                p = jnp.exp(s - jnp.max(s, axis=-1, keepdims=True))
                inv = 1.0 / jnp.sum(p, axis=-1, keepdims=True)
                o_scr[orow, lanes] = jnp.dot(p.astype(BF16), mv_ref[sub, :, lanes], preferred_element_type=F32) * inv
        x2 = x1 + jnp.dot(o_scr[rows, :].astype(BF16), wco_ref[...], preferred_element_type=F32)
        x2_ref[rows, :] = x2
        h3 = _rms(x2, gf_ref[...]).astype(BF16)
        h3_ref[rows, :] = h3

        logits = lax.dot_general(wrt_ref[...], h3, _NT, preferred_element_type=F32) + brt_ref[...]
        row = lax.broadcasted_iota(jnp.int32, (ROUTE_LOGIT_ROWS, n), 0)
        i0, i1, g0, g1 = _route(logits[:ROUTE_LOGIT_ROWS, :], row)
        e0 = i0 - ROUTE_L2
        e1 = i1 - ROUTE_L2
        expert = lax.broadcasted_iota(jnp.int32, (LANES, n), 0)
        hit0 = expert == e0
        hit1 = expert == e1
        onehot = jnp.where(hit0, 1.0, jnp.where(hit1, 1.0, 0.0)).astype(BF16)
        earlier = lax.broadcasted_iota(jnp.int32, (n, n), 0) < lax.broadcasted_iota(jnp.int32, (n, n), 1)
        before = jnp.where(earlier, 1.0, 0.0).astype(BF16)
        seen = jnp.dot(onehot, before, preferred_element_type=F32) + _row_to_col(count)
        rank0 = jnp.sum(jnp.where(hit0, seen, 0.0), axis=0, keepdims=True)
        rank1 = jnp.sum(jnp.where(hit1, seen, 0.0), axis=0, keepdims=True)
        count = count + lax.dot_general(jnp.ones((8, n), BF16), onehot, _NT, preferred_element_type=F32)[0:1, :]

        rec = jnp.concatenate([e0.astype(F32), e1.astype(F32), rank0, rank1, g0, g1, jnp.zeros((2, n), F32)], axis=0)
        routet_ref[:, rows] = rec
        route_ref[rows, :] = jnp.concatenate([rec, jnp.zeros((LANES - ROUTE_ROWS, n), F32)], axis=0).T[:, :ROUTE_ROWS]
    cnt_ref[...] = count


def _post_block(x, a_n, b_n, mk, mv, weights, *, tm):
    b, s, d = x.shape
    if s >= tm:
        nsub, grid = 1, (b, s // tm)
        tok = lambda i, j: (i, j, 0)
        flat = lambda i, j: i * (s // tm) + j
    else:
        nsub = tm // s
        assert b % nsub == 0
        x, a_n, b_n = (t.reshape(b // nsub, tm, t.shape[-1]) for t in (x, a_n, b_n))
        grid = (b // nsub, 1)
        tok = lambda i, j: (i, 0, 0)
        flat = lambda i, j: i
    mem = lambda i, j: (i, 0, 0)
    const = lambda arr: pl.BlockSpec(arr.shape, lambda i, j: (0,) * arr.ndim)
    in_specs = [pl.BlockSpec((None, tm, d), tok),
                pl.BlockSpec((None, tm, W_GROUP), tok), pl.BlockSpec((None, tm, W_GROUP), tok),
                pl.BlockSpec((nsub, mk.shape[1], W_MEM), mem), pl.BlockSpec((nsub, mv.shape[1], W_MEM), mem)]
    in_specs += [const(w) for w in weights]
    n = b * s
    out_shape = [jax.ShapeDtypeStruct((n, d), F32), jax.ShapeDtypeStruct((n, d), BF16),
                 jax.ShapeDtypeStruct((n, ROUTE_ROWS), F32), jax.ShapeDtypeStruct((ROUTE_ROWS, n), F32),
                 jax.ShapeDtypeStruct((n // tm, 1, LANES), F32)]
    out_specs = [pl.BlockSpec((tm, d), lambda i, j: (flat(i, j), 0)),
                 pl.BlockSpec((tm, d), lambda i, j: (flat(i, j), 0)),
                 pl.BlockSpec((tm, ROUTE_ROWS), lambda i, j: (flat(i, j), 0)),
                 pl.BlockSpec((ROUTE_ROWS, tm), lambda i, j: (0, flat(i, j))),
                 pl.BlockSpec((None, 1, LANES), lambda i, j: (flat(i, j), 0, 0))]
    return pl.pallas_call(
        functools.partial(_post_body, tm=tm, nsub=nsub),
        grid=grid,
        in_specs=in_specs,
        out_specs=out_specs,
        out_shape=out_shape,
        scratch_shapes=[pltpu.VMEM((tm, W_MEM), F32)],
        compiler_params=pltpu.CompilerParams(
            dimension_semantics=("parallel", "parallel"), vmem_limit_bytes=VMEM_LIMIT),
        name="post_block",
    )(x, a_n, b_n, mk, mv, *weights)


def _prep_post(w_out, g_cross, w_cq, w_co, g_ffn, w_r1, b_r1, w_r2, b_r2):
    pad = LANES - N_GROUPS - N_EXPERTS
    w_rt = jnp.pad(jnp.concatenate([w_r1, w_r2], axis=1), ((0, 0), (0, pad))).astype(BF16).T
    b_rt = jnp.pad(jnp.concatenate([b_r1, b_r2]).reshape(-1, 1), ((0, pad), (0, 0))).astype(F32)
    return [w_out[:W_GROUP].astype(BF16), w_out[W_GROUP:].astype(BF16), g_cross.reshape(1, -1),
            w_cq.astype(BF16), w_co.astype(BF16), g_ffn.reshape(1, -1), w_rt, b_rt]


D_EXPERT = 512
TOP_K = 2
ROW_TILE = 512
MXU_DIM = 256
RUN_ALIGN = 16
PLAN_TILES = LANES
TILE_TABLE = 2 * LANES


def _local_rows(tm):
    return -(-(TOP_K * tm + N_EXPERTS * (RUN_ALIGN - 1)) // MXU_DIM) * MXU_DIM


def _n_row_tiles(n_tokens, tm):
    rows = n_tokens * TOP_K + (n_tokens // tm) * N_EXPERTS * (RUN_ALIGN - 1) + N_EXPERTS * (ROW_TILE - 1)
    return rows // ROW_TILE


N_CHUNK_LANE = LANES - 1


def _plan_body(cnt_ref, lstart_ref, chunk_ref, offs_ref, te_ref):
    cnt = cnt_ref[...].astype(jnp.int32)
    n16 = ((cnt + (RUN_ALIGN - 1)) & ~(RUN_ALIGN - 1)).astype(F32)
    lend = _lane_cumsum(n16)
    lstart = lend - n16
    earlier = (_lane_cumsum(n16.T) - n16.T).T
    total = jnp.sum(n16, axis=0, keepdims=True).astype(jnp.int32)
    seg = jnp.broadcast_to((total + (ROW_TILE - 1)) & ~(ROW_TILE - 1), (8, LANES)).astype(F32)
    ends = _lane_cumsum(seg)
    offs = ends - seg
    shift = earlier + offs[0:1, :] - lstart
    lstart_ref[...] = lstart.astype(jnp.int32)
    offs_ref[...] = offs[0:1, :].astype(jnp.int32)

    lane = lax.broadcasted_iota(jnp.int32, (PLAN_TILES, LANES), 1)
    local_row = (lane * RUN_ALIGN).astype(F32)
    owner = jnp.zeros((PLAN_TILES, LANES), jnp.int32)
    for e in range(N_EXPERTS):
        owner = owner + jnp.where(lend[:, e:e + 1] <= local_row, 1, 0)
    glob = local_row
    for e in range(N_EXPERTS):
        glob = glob + jnp.where(owner == e, shift[:, e:e + 1], 0.0)
    n_chunks = lend[:, N_EXPERTS - 1:N_EXPERTS] * (1.0 / RUN_ALIGN)
    chunk_ref[...] = jnp.where(lane == N_CHUNK_LANE, n_chunks, glob).astype(jnp.int32)

    tile_start = (lax.broadcasted_iota(jnp.int32, te_ref.shape, 1) * ROW_TILE).astype(F32)
    te = jnp.zeros(te_ref.shape, jnp.int32)
    for e in range(N_EXPERTS):
        end_e = jnp.sum(jnp.where(lane[0:1, :] == e, ends[0:1, :], 0.0), axis=-1, keepdims=True)
        te = te + jnp.where(end_e <= tile_start, 1, 0)
    te_ref[...] = jnp.minimum(te, N_EXPERTS - 1)


def _plan(counts, tm):
    nt = counts.shape[0]
    assert nt <= PLAN_TILES and _local_rows(tm) // RUN_ALIGN <= N_CHUNK_LANE
    cnt = jnp.pad(counts.reshape(nt, LANES), ((0, PLAN_TILES - nt), (0, 0)))
    grid_i32 = jax.ShapeDtypeStruct((PLAN_TILES, LANES), jnp.int32)
    lstart, chunks, offs, te = pl.pallas_call(
        _plan_body,
        out_shape=[grid_i32, grid_i32, jax.ShapeDtypeStruct((1, LANES), jnp.int32),
                   jax.ShapeDtypeStruct((1, TILE_TABLE), jnp.int32)],
        name="moe_plan",
    )(cnt)
    per_tile = lambda t: t[:nt].reshape(nt, 1, LANES)
    return per_tile(lstart), per_tile(chunks), offs.reshape(LANES), te.reshape(TILE_TABLE)


def _for_each_chunk(chunk_ref, fn):
    def body(c, carry):
        fn(pl.multiple_of(c * RUN_ALIGN, RUN_ALIGN), pl.multiple_of(chunk_ref[0, c], RUN_ALIGN))
        return carry

    lax.fori_loop(0, chunk_ref[0, N_CHUNK_LANE], body, None)


def _local_positions_row(rt_ref, lstart_ref):
    pos = []
    for r_eid, r_rank in ((R_EID0, R_RANK0), (R_EID1, R_RANK1)):
        eid = rt_ref[r_eid:r_eid + 1, :].astype(jnp.int32)
        p = rt_ref[r_rank:r_rank + 1, :].astype(jnp.int32)
        for e in range(N_EXPERTS):
            p = p + jnp.where(eid == e, lstart_ref[0, e], 0)
        pos.append(p)
    return pos


def _dispatch_body(offs_ref, lstart_ref, chunk_ref, pchunk_ref,
                   hp_ref, hs_ref, rt_ref, xs_ref, loc, zeros, sems, zsem, *, tm, n_prompt_tiles):
    i = pl.program_id(0)
    n_tiles = xs_ref.shape[0] // ROW_TILE
    half = lax.rem(i, 2)

    @pl.when(i == 0)
    def _():
        zeros[...] = jnp.zeros(zeros.shape, zeros.dtype)
        zero_tile = lambda row: pltpu.make_async_copy(
            zeros, xs_ref.at[pl.ds(pl.multiple_of(row, ROW_TILE), ROW_TILE)], zsem)
        n_used = offs_ref[N_EXPERTS] // ROW_TILE

        def tail(j, carry, op):
            op(zero_tile(j * ROW_TILE))
            return carry

        for op in (lambda c: c.start(), lambda c: c.wait()):
            for e in range(N_EXPERTS):
                @pl.when(offs_ref[e + 1] > offs_ref[e])
                def _():
                    op(zero_tile(offs_ref[e + 1] - ROW_TILE))
            lax.fori_loop(n_used, n_tiles, functools.partial(tail, op=op), None)

    pos0, pos1 = _local_positions_row(rt_ref, lstart_ref)
    used = chunk_ref[0, N_CHUNK_LANE] * RUN_ALIGN
    body_rows = loc.shape[1] - MXU_DIM

    def sort_rows(h_ref, lo, n):
        slot = lo + lax.broadcasted_iota(jnp.int32, (n, tm), 0)
        perm = jnp.where(slot == pos0, 1.0, jnp.where(slot == pos1, 1.0, 0.0)).astype(BF16)
        loc[half, lo:lo + n, :] = jnp.dot(perm, h_ref[...], preferred_element_type=F32).astype(BF16)

    for h_ref, mine in ((hp_ref, i < n_prompt_tiles), (hs_ref, i >= n_prompt_tiles)):
        @pl.when(mine)
        def _():
            sort_rows(h_ref, 0, body_rows)

        @pl.when(mine & (used > body_rows))
        def _():
            sort_rows(h_ref, body_rows, MXU_DIM)

    def chunk(buf, lo, go):
        return pltpu.make_async_copy(loc.at[buf, pl.ds(lo, RUN_ALIGN)], xs_ref.at[pl.ds(go, RUN_ALIGN)],
                                     sems.at[buf])

    @pl.when(i > 0)
    def _():
        _for_each_chunk(pchunk_ref, lambda lo, go: chunk(1 - half, lo, go).wait())

    _for_each_chunk(chunk_ref, lambda lo, go: chunk(half, lo, go).start())

    @pl.when(i == pl.num_programs(0) - 1)
    def _():
        _for_each_chunk(chunk_ref, lambda lo, go: chunk(half, lo, go).wait())


def _dispatch(h3_prompt, h3_sample, route_t, plan, *, tm):
    lstart, chunks, offs, _ = plan
    n_p, n_s, d = h3_prompt.shape[0], h3_sample.shape[0], h3_prompt.shape[-1]
    assert n_p % tm == 0 and n_s % tm == 0
    n = n_p + n_s
    nt = n // tm
    npt = n_p // tm
    n_rows = _n_row_tiles(n, tm) * ROW_TILE
    smem_tile = lambda: pl.BlockSpec((None, 1, LANES), lambda i, offs: (i, 0, 0), memory_space=pltpu.SMEM)
    smem_prev = lambda: pl.BlockSpec((None, 1, LANES), lambda i, offs: (jnp.maximum(i - 1, 0), 0, 0),
                                     memory_space=pltpu.SMEM)
    return pl.pallas_call(
        functools.partial(_dispatch_body, tm=tm, n_prompt_tiles=npt),
        grid_spec=pltpu.PrefetchScalarGridSpec(
            num_scalar_prefetch=1,
            grid=(nt,),
            in_specs=[smem_tile(), smem_tile(), smem_prev(),
                      pl.BlockSpec((tm, d), lambda i, offs: (jnp.minimum(i, npt - 1), 0)),
                      pl.BlockSpec((tm, d), lambda i, offs: (jnp.maximum(i - npt, 0), 0)),
                      pl.BlockSpec((ROUTE_ROWS, tm), lambda i, offs: (0, i))],
            out_specs=pl.BlockSpec(memory_space=pl.ANY),
            scratch_shapes=[pltpu.VMEM((2, _local_rows(tm), d), BF16), pltpu.VMEM((ROW_TILE, d), BF16),
                            pltpu.SemaphoreType.DMA((2,)), pltpu.SemaphoreType.DMA(())]),
        out_shape=jax.ShapeDtypeStruct((n_rows, d), BF16),
        compiler_params=pltpu.CompilerParams(dimension_semantics=("arbitrary",), vmem_limit_bytes=VMEM_LIMIT),
        name="moe_dispatch",
    )(offs, lstart, chunks, chunks, h3_prompt, h3_sample, route_t)


def _experts_body(te_ref, offs_ref, xs_ref, wg_ref, wu_ref, wd_ref, ys_ref,
                  wg_buf, wu_buf, wd_buf, wgu_bf, wd_bf, turn_ref, sems):
    i = pl.program_id(0)
    n_used = offs_ref[N_EXPERTS] // ROW_TILE

    def fetch(expert, half):
        return [pltpu.make_async_copy(src.at[expert], dst.at[half], sems.at[half, k])
                for k, (src, dst) in enumerate(((wg_ref, wg_buf), (wu_ref, wu_buf), (wd_ref, wd_buf)))]

    @pl.when(i == 0)
    def _():
        turn_ref[0] = 0
        for copy in fetch(te_ref[0], 0):
            copy.start()

    @pl.when(i < n_used)
    def _():
        expert = te_ref[i]

        @pl.when((i == 0) | (expert != te_ref[jnp.maximum(i - 1, 0)]))
        def _():
            half = lax.rem(turn_ref[0], 2)
            turn_ref[0] = turn_ref[0] + 1
            for copy in fetch(expert, half):
                copy.wait()
            wgu_bf[:, 0:D_EXPERT] = wg_buf[half].astype(BF16)
            wgu_bf[:, D_EXPERT:] = wu_buf[half].astype(BF16)
            wd_bf[...] = wd_buf[half].astype(BF16)
            following = offs_ref[expert + 1] // ROW_TILE

            @pl.when(following < n_used)
            def _():
                for copy in fetch(te_ref[following], 1 - half):
                    copy.start()

        dot = functools.partial(jnp.dot, preferred_element_type=F32)
        for rows in _chains(ROW_TILE):
            gate_up = dot(xs_ref[rows, :], wgu_bf[...])
            gate, up = gate_up[:, :D_EXPERT], gate_up[:, D_EXPERT:]
            act = (gate * jax.nn.sigmoid(gate) * up).astype(BF16)
            ys_ref[rows, :] = dot(act, wd_bf[...]).astype(BF16)

    @pl.when(i >= n_used)
    def _():
        ys_ref[...] = jnp.zeros(ys_ref.shape, ys_ref.dtype)


def _experts(xs, te, offs, w_gate, w_up, w_down):
    n_rows, d = xs.shape
    last = lambda i, te, offs: jnp.minimum(i, offs[N_EXPERTS] // ROW_TILE - 1)
    hbm = pl.BlockSpec(memory_space=pl.ANY)
    return pl.pallas_call(
        _experts_body,
        grid_spec=pltpu.PrefetchScalarGridSpec(
            num_scalar_prefetch=2,
            grid=(n_rows // ROW_TILE,),
            in_specs=[pl.BlockSpec((ROW_TILE, d), lambda i, te, offs: (last(i, te, offs), 0)), hbm, hbm, hbm],
            out_specs=pl.BlockSpec((ROW_TILE, d), lambda i, te, offs: (i, 0)),
            scratch_shapes=[pltpu.VMEM((2, d, D_EXPERT), F32), pltpu.VMEM((2, d, D_EXPERT), F32),
                            pltpu.VMEM((2, D_EXPERT, d), F32),
                            pltpu.VMEM((d, 2 * D_EXPERT), BF16), pltpu.VMEM((D_EXPERT, d), BF16),
                            pltpu.SMEM((1,), jnp.int32), pltpu.SemaphoreType.DMA((2, 3))]),
        out_shape=jax.ShapeDtypeStruct((n_rows, d), BF16),
        compiler_params=pltpu.CompilerParams(dimension_semantics=("arbitrary",), vmem_limit_bytes=VMEM_LIMIT),
        name="moe_experts",
    )(te, offs, xs, w_gate, w_up, w_down)


def _combine_body(chunk_ref, nchunk_ref, x2p_ref, x2s_ref, route_ref, lsv_ref, ys_ref, g_ref,
                  yp_ref, ysm_ref, loc, sems, *, tm, n_prompt_tiles):
    i = pl.program_id(0)
    half = lax.rem(i, 2)

    def chunk(buf, lo, go):
        return pltpu.make_async_copy(ys_ref.at[pl.ds(go, RUN_ALIGN)], loc.at[buf, pl.ds(lo, RUN_ALIGN)],
                                     sems.at[buf])

    @pl.when(i == 0)
    def _():
        loc[...] = jnp.zeros(loc.shape, loc.dtype)
        _for_each_chunk(chunk_ref, lambda lo, go: chunk(0, lo, go).start())

    @pl.when(i + 1 < pl.num_programs(0))
    def _():
        _for_each_chunk(nchunk_ref, lambda lo, go: chunk(1 - half, lo, go).start())

    _for_each_chunk(chunk_ref, lambda lo, go: chunk(half, lo, go).wait())

    lane = lax.broadcasted_iota(jnp.int32, (tm, LANES), 1)
    picks = []
    for r_eid, r_rank, r_gate in ((R_EID0, R_RANK0, R_GATE0), (R_EID1, R_RANK1, R_GATE1)):
        eid = route_ref[:, r_eid:r_eid + 1].astype(jnp.int32)
        start = jnp.sum(jnp.where(lane == eid, lsv_ref[...], 0), axis=-1, keepdims=True)
        picks.append((route_ref[:, r_rank:r_rank + 1].astype(jnp.int32) + start, route_ref[:, r_gate:r_gate + 1]))

    slot = lax.broadcasted_iota(jnp.int32, (tm, loc.shape[1]), 1)
    weights = jnp.zeros(slot.shape, F32)
    for pos, gate in picks:
        weights = jnp.where(slot == pos, gate, weights)
    moe = jnp.dot(weights.astype(BF16), loc[half], preferred_element_type=F32)

    @pl.when(i < n_prompt_tiles)
    def _():
        yp_ref[...] = _rms(x2p_ref[...] + moe, g_ref[...])

    @pl.when(i >= n_prompt_tiles)
    def _():
        ysm_ref[...] = _rms(x2s_ref[...] + moe, g_ref[...])


def _combine(x2_prompt, x2_sample, route, plan, ys, g_final, *, tm):
    lstart, chunks, _, _ = plan
    (n_p, d), n_s = x2_prompt.shape, x2_sample.shape[0]
    assert n_p % tm == 0 and n_s % tm == 0
    npt = n_p // tm
    nt = npt + n_s // tm
    smem_tile = lambda: pl.BlockSpec((None, 1, LANES), lambda i: (i, 0, 0), memory_space=pltpu.SMEM)
    smem_next = lambda: pl.BlockSpec((None, 1, LANES), lambda i: (jnp.minimum(i + 1, nt - 1), 0, 0),
                                     memory_space=pltpu.SMEM)
    prompt_tile = lambda: pl.BlockSpec((tm, d), lambda i: (jnp.minimum(i, npt - 1), 0))
    sample_tile = lambda: pl.BlockSpec((tm, d), lambda i: (jnp.maximum(i - npt, 0), 0))
    return pl.pallas_call(
        functools.partial(_combine_body, tm=tm, n_prompt_tiles=npt),
        grid=(nt,),
        in_specs=[smem_tile(), smem_next(), prompt_tile(), sample_tile(),
                  pl.BlockSpec((tm, ROUTE_ROWS), lambda i: (i, 0)),
                  pl.BlockSpec((None, 1, LANES), lambda i: (i, 0, 0)),
                  pl.BlockSpec(memory_space=pl.ANY),
                  pl.BlockSpec((1, d), lambda i: (0, 0))],
        out_specs=[prompt_tile(), sample_tile()],
        out_shape=[jax.ShapeDtypeStruct((n_p, d), F32), jax.ShapeDtypeStruct((n_s, d), F32)],
        scratch_shapes=[pltpu.VMEM((2, _local_rows(tm), d), BF16), pltpu.SemaphoreType.DMA((2,))],
        compiler_params=pltpu.CompilerParams(dimension_semantics=("arbitrary",), vmem_limit_bytes=VMEM_LIMIT),
        name="moe_combine",
    )(chunks, chunks, x2_prompt, x2_sample, route, lstart, ys, g_final)


TOKEN_TILE = 512
FOX_Q_TILE = 256
FOX_CACHE_TILE = 1024


def kernel(x_prompt, x_sample, cache_fox_k, cache_fox_v, cache_fox_logf, cache_band_k, cache_band_v, cache_mem_k, cache_mem_v, mem_prompt, g_mix, w_in, b_forget, g_out_fox, g_out_band, rel_table, w_out, g_cross, g_mem, w_cq, w_ck, w_cv, w_co, g_ffn, w_router1, b_router1, w_router2, b_router2, w_exp_gate, w_exp_up, w_exp_down, g_final):
    assert g_mix.shape[0] == 1, "single-layer model"
    bsz, seq, d = x_prompt.shape
    sb, st, _ = x_sample.shape
    n_s = sb * st
    n_mem = mem_prompt.shape[1]
    row = lambda g: g.reshape(1, -1)

    w_pad, bf_pad, g_mix_r = _prep_proj(w_in[0], b_forget[0], g_mix[0])
    g_of, g_ob = row(g_out_fox[0]), row(g_out_band[0])
    bias_t, bias_s = _band_bias(_prep_band_bias_row(rel_table[0]))

    qx, kx, vat, qxb, kxb, vbt, kaf, vaf, kbf, vbf, logf = _proj(
        x_prompt, g_mix_r, w_pad, bf_pad, tm=TOKEN_TILE, prompt=True)
    a_p = _fox_prompt(qx, kx, vat, g_of, tq=FOX_Q_TILE)
    b_p = _band_prompt(qxb, kxb, vbt, bias_t, g_ob)

    s_out = _proj(x_sample.reshape(1, n_s, d), g_mix_r, w_pad, bf_pad, tm=n_s, prompt=False)
    sqa, ska, sva, sqb, skb, svb, skaf, svaf, skbf, svbf = (t.reshape(sb, st, W_GROUP) for t in s_out[:10])
    slogf = s_out[10].reshape(sb, st, N_HEADS)
    slft = s_out[11].reshape(N_HEADS, sb, st).transpose(1, 0, 2)
    past = cache_fox_k.shape[2]
    a_s = _fox_sample(sqa, ska, sva, slft,
                      cache_fox_k[0].reshape(sb, past, W_GROUP), cache_fox_v[0].reshape(sb, past, W_GROUP),
                      cache_fox_logf[0].transpose(0, 2, 1), g_of, pt=FOX_CACHE_TILE)
    bp = cache_band_k.shape[2]
    b_s, nbk, nbv = _band_sample(sqb, skb, svb, skbf, svbf,
                                 cache_band_k[0].reshape(sb, bp, W_GROUP), cache_band_v[0].reshape(sb, bp, W_GROUP),
                                 bias_s, g_ob)

    w_ckv = jnp.concatenate([w_ck[0], w_cv[0]], axis=1).astype(BF16)
    mkf, mvf, mk, mv = _mem_kv(mem_prompt, row(g_mem[0]), w_ckv)
    post_w = _prep_post(w_out[0], g_cross[0], w_cq[0], w_co[0], g_ffn[0],
                        w_router1[0], b_router1[0], w_router2[0], b_router2[0])
    x2_p, h3_p, route_p, routet_p, cnt_p = _post_block(x_prompt, a_p, b_p, mk, mv, post_w, tm=TOKEN_TILE)
    cmk = cache_mem_k[0].reshape(sb, n_mem, W_MEM).astype(BF16)
    cmv = cache_mem_v[0].reshape(sb, n_mem, W_MEM).astype(BF16)
    x2_s, h3_s, route_s, routet_s, cnt_s = _post_block(x_sample, a_s, b_s, cmk, cmv, post_w, tm=TOKEN_TILE)
    route = jnp.concatenate([route_p, route_s], axis=0)
    route_t = jnp.concatenate([routet_p, routet_s], axis=1)

    plan = _plan(jnp.concatenate([cnt_p, cnt_s], axis=0), TOKEN_TILE)
    xs = _dispatch(h3_p, h3_s, route_t, plan, tm=TOKEN_TILE)
    ys = _experts(xs, plan[3], plan[2], w_exp_gate[0], w_exp_up[0], w_exp_down[0])
    y_p, y_s = _combine(x2_p, x2_s, route, plan, ys, row(g_final), tm=TOKEN_TILE)

    heads = lambda t, n: t.reshape(1, n, -1, N_HEADS, HEAD_DIM)
    mem_heads = lambda t: t.reshape(1, bsz, n_mem, N_HEADS_MEM, HEAD_DIM_MEM)
    return (y_p.reshape(bsz, seq, d), y_s.reshape(sb, st, d),
            heads(kaf, bsz), heads(vaf, bsz), logf.reshape(1, bsz, seq, N_HEADS),
            heads(kbf, bsz), heads(vbf, bsz), mem_heads(mkf), mem_heads(mvf),
            heads(skaf, sb), heads(svaf, sb), slogf.reshape(1, sb, st, N_HEADS),
            heads(nbk, sb), heads(nbv, sb))
```

```python
import functools

import jax
import jax.numpy as jnp
from jax import lax
from jax.experimental import pallas as pl
from jax.experimental.pallas import tpu as pltpu

F32 = jnp.float32
BF16 = jnp.bfloat16

D_MODEL = 1024
HEAD_DIM = 64
N_HEADS = 8
W_GROUP = N_HEADS * HEAD_DIM
N_PAIRS = N_HEADS // 2
CHUNK = 64
LEFT_CHUNKS = 8
LEFT = LEFT_CHUNKS * CHUNK
REL_CLIP = 128
EPS = 1e-6
NEG_INF = -1e30
ATTN_SCALE = HEAD_DIM ** -0.5
LANES = 128
PROJ_PAD = 3 * W_GROUP * 2 + LANES
VMEM_LIMIT = 56 * 1024 * 1024


def _rms(x, g):
    ms = jnp.mean(x * x, axis=-1, keepdims=True)
    return x * lax.rsqrt(ms + EPS) * g


def _log_sigmoid(x):
    return -(jnp.maximum(-x, 0.0) + jnp.log1p(jnp.exp(-jnp.abs(x))))


def _lane_cumsum(x):
    n = x.shape[-1]
    lane = lax.broadcasted_iota(jnp.int32, x.shape, 1)
    k = 1
    while k < n:
        x = x + jnp.where(lane >= k, pltpu.roll(x, k, axis=1), 0.0)
        k *= 2
    return x


LOG2E = 1.4426950408889634
SCALE_BASE2 = ATTN_SCALE * LOG2E


def _split3(x):
    hi = x.astype(BF16).astype(F32)
    mid = (x - hi).astype(BF16).astype(F32)
    lo = x - hi - mid
    return hi, mid, lo


def _extra_lane(parity):
    return HEAD_DIM if parity == 0 else 0


def _fox_extras(c3t, hp, tm):
    row = lax.broadcasted_iota(jnp.int32, (8, tm), 0)

    def group(h, q_side):
        hi, mid, lo = (p[h:h + 1, :] for p in c3t)
        if q_side:
            return jnp.where(row < 3, 1.0, jnp.where(row == 3, hi, jnp.where(row == 4, mid, jnp.where(row == 5, lo, 0.0))))
        return jnp.where(row == 0, -hi, jnp.where(row == 1, -mid, jnp.where(row == 2, -lo, jnp.where(row < 6, 1.0, 0.0))))

    gap = jnp.zeros((HEAD_DIM - 8, tm), F32)
    sides = []
    for q_side in (True, False):
        t = jnp.concatenate([group(2 * hp + 1, q_side), gap, group(2 * hp, q_side), gap], axis=0)
        sides.append(t.T)
    return sides


def _head_blocks(x128, extras, lane):
    return (jnp.where(lane < HEAD_DIM, x128, extras).astype(BF16),
            jnp.where(lane >= HEAD_DIM, x128, extras).astype(BF16))


Q_A, K_A, V_A, Q_B, K_B, V_B = range(6)


PROJ_CHAIN = 256


def _chains(tm, chain=PROJ_CHAIN):
    n = max(tm // chain, 1)
    return [pl.ds(i * (tm // n), tm // n) for i in range(n)]


def _proj_common(rows, x_ref, g_ref, w_ref, bf_ref, kaf_ref, vaf_ref, kbf_ref, vbf_ref, logf_ref, keep_tiles):
    s = pl.program_id(1)
    ns = pl.num_programs(1)
    h = _rms(x_ref[rows, :], g_ref[...]).astype(BF16)
    w = W_GROUP
    zf = jnp.dot(h, w_ref[:, 6 * w:6 * w + LANES], preferred_element_type=F32)
    z = [jnp.dot(h, w_ref[:, g * w:(g + 1) * w], preferred_element_type=F32) for g in range(6)]
    kaf_ref[rows, :] = z[K_A]
    vaf_ref[rows, :] = z[V_A]

    @pl.when(s >= ns - keep_tiles)
    def _():
        kbf_ref[rows, :] = z[K_B]
        vbf_ref[rows, :] = z[V_B]

    logf = _log_sigmoid(zf + bf_ref[...])
    logf_ref[rows, :] = logf[:, :N_HEADS]
    return z, logf


def _proj_prompt_body(x_ref, g_ref, w_ref, bf_ref, qx_ref, kx_ref, vat_ref, qxb_ref, kxb_ref, vbt_ref,
                      kaf_ref, vaf_ref, kbf_ref, vbf_ref, logf_ref, carry_ref, *, tm, keep_tiles):
    @pl.when(pl.program_id(1) == 0)
    def _():
        carry_ref[...] = jnp.zeros_like(carry_ref)

    for rows in _chains(tm):
        n = rows.size
        z, logf = _proj_common(rows, x_ref, g_ref, w_ref, bf_ref, kaf_ref, vaf_ref, kbf_ref, vbf_ref, logf_ref,
                               keep_tiles)
        vat_ref[:, rows] = z[V_A].T.astype(BF16)
        vbt_ref[:, rows] = z[V_B].T.astype(BF16)
        ct = _lane_cumsum(logf.T[:N_HEADS, :]) + carry_ref[:, 0:1]
        carry_ref[...] = jnp.broadcast_to(ct[:, n - 1:n], carry_ref.shape)
        c3t = _split3(ct * LOG2E)
        lane = lax.broadcasted_iota(jnp.int32, (n, LANES), 1)
        band_q_extras = jnp.where((lane == _extra_lane(0)) | (lane == _extra_lane(1)), 1.0, 0.0)
        band_k_extras = jnp.zeros((n, LANES), F32)
        for hp in range(N_PAIRS):
            blocks = slice(2 * hp * LANES, 2 * (hp + 1) * LANES)
            blk = lambda group, hp=hp, z=z: z[group][:, hp * LANES:(hp + 1) * LANES]
            q_extras, k_extras = _fox_extras(c3t, hp, n)
            qx_ref[rows, blocks] = jnp.concatenate(_head_blocks(blk(Q_A) * SCALE_BASE2, q_extras, lane), axis=1)
            kx_ref[rows, blocks] = jnp.concatenate(_head_blocks(blk(K_A), k_extras, lane), axis=1)
            qxb_ref[rows, blocks] = jnp.concatenate(_head_blocks(blk(Q_B) * SCALE_BASE2, band_q_extras, lane), axis=1)
            kxb_ref[rows, blocks] = jnp.concatenate(_head_blocks(blk(K_B), band_k_extras, lane), axis=1)


def _proj_sample_body(x_ref, g_ref, w_ref, bf_ref, qa_ref, ka_ref, va_ref, qb_ref, kb_ref, vb_ref,
                      kaf_ref, vaf_ref, kbf_ref, vbf_ref, logf_ref, lt_ref, *, tm, keep_tiles):
    for rows in _chains(tm):
        z, logf = _proj_common(rows, x_ref, g_ref, w_ref, bf_ref, kaf_ref, vaf_ref, kbf_ref, vbf_ref, logf_ref,
                               keep_tiles)
        qa_ref[rows, :] = (z[Q_A] * ATTN_SCALE).astype(BF16)
        ka_ref[rows, :] = z[K_A].astype(BF16)
        va_ref[rows, :] = z[V_A].astype(BF16)
        qb_ref[rows, :] = (z[Q_B] * ATTN_SCALE).astype(BF16)
        kb_ref[rows, :] = z[K_B].astype(BF16)
        vb_ref[rows, :] = z[V_B].astype(BF16)
        lt_ref[:, rows] = logf.T[:N_HEADS, :]


def _proj(x, g_mix, w_pad, bf_pad, *, tm, prompt):
    b, s, d = x.shape
    ns = s // tm
    keep = min(LEFT, s)
    assert s % tm == 0 and keep % tm == 0
    keep_tiles = keep // tm
    row = pl.BlockSpec((None, tm, W_GROUP), lambda i, j: (i, j, 0))
    wide = pl.BlockSpec((None, tm, N_HEADS * LANES), lambda i, j: (i, j, 0))
    col = pl.BlockSpec((None, W_GROUP, tm), lambda i, j: (i, 0, j))
    keep_spec = pl.BlockSpec((None, tm, W_GROUP), lambda i, j: (i, jnp.maximum(j - (ns - keep_tiles), 0), 0))
    heads_row = pl.BlockSpec((None, tm, N_HEADS), lambda i, j: (i, j, 0))
    heads_col = pl.BlockSpec((None, N_HEADS, tm), lambda i, j: (i, 0, j))
    const = lambda shape: pl.BlockSpec(shape, lambda i, j: (0,) * len(shape))
    rows_bf = jax.ShapeDtypeStruct((b, s, W_GROUP), BF16)
    wide_bf = jax.ShapeDtypeStruct((b, s, N_HEADS * LANES), BF16)
    cols_bf = jax.ShapeDtypeStruct((b, W_GROUP, s), BF16)
    f32_tail = [jax.ShapeDtypeStruct((b, s, W_GROUP), F32)] * 2
    f32_tail += [jax.ShapeDtypeStruct((b, keep, W_GROUP), F32)] * 2
    f32_tail += [jax.ShapeDtypeStruct((b, s, N_HEADS), F32)]
    tail_specs = [row, row, keep_spec, keep_spec, heads_row]
    if prompt:
        body = functools.partial(_proj_prompt_body, tm=tm, keep_tiles=keep_tiles)
        out_shape = [wide_bf, wide_bf, cols_bf, wide_bf, wide_bf, cols_bf] + f32_tail
        out_specs = [wide, wide, col, wide, wide, col] + tail_specs
        scratch = [pltpu.VMEM((N_HEADS, LANES), F32)]
    else:
        body = functools.partial(_proj_sample_body, tm=tm, keep_tiles=keep_tiles)
        out_shape = [rows_bf] * 6 + f32_tail + [jax.ShapeDtypeStruct((b, N_HEADS, s), F32)]
        out_specs = [row] * 6 + tail_specs + [heads_col]
        scratch = []
    return pl.pallas_call(
        body,
        grid=(b, ns),
        in_specs=[pl.BlockSpec((None, tm, d), lambda i, j: (i, j, 0)),
                  const((1, d)), const(w_pad.shape), const((1, LANES))],
        out_specs=out_specs,
        out_shape=out_shape,
        scratch_shapes=scratch,
        compiler_params=pltpu.CompilerParams(
            dimension_semantics=("parallel", "arbitrary"), vmem_limit_bytes=VMEM_LIMIT),
        name="proj",
    )(x, g_mix, w_pad, bf_pad)


def _prep_proj(w_in, b_forget, g_mix):
    cols = w_in.shape[-1]
    w_pad = jnp.pad(w_in, ((0, 0), (0, PROJ_PAD - cols))).astype(BF16)
    bf_pad = jnp.pad(b_forget.reshape(1, -1), ((0, 0), (0, LANES - N_HEADS))).astype(F32)
    return w_pad, bf_pad, g_mix.reshape(1, -1)


def _pair_masks():
    lane = lax.broadcasted_iota(jnp.int32, (1, LANES), 1)
    return lane < HEAD_DIM


def _head_q(q128, even_lanes, parity):
    keep = even_lanes if parity == 0 else jnp.logical_not(even_lanes)
    return jnp.where(keep, q128, jnp.zeros_like(q128))


def _head_v(v128, even_lanes, parity):
    keep = even_lanes if parity == 0 else jnp.logical_not(even_lanes)
    return jnp.where(keep, v128, jnp.ones_like(v128))


def _head_out(acc_even, acc_odd, even_lanes):
    inv_e = 1.0 / acc_even[:, HEAD_DIM:HEAD_DIM + 1]
    inv_o = 1.0 / acc_odd[:, 0:1]
    return jnp.where(even_lanes, acc_even * inv_e, acc_odd * inv_o)


_NT = (((1,), (1,)), ((), ()))


def _pair_rows():
    row = lax.broadcasted_iota(jnp.int32, (LANES, 1), 0)
    return row < HEAD_DIM


def _head_vt(vt128, even_rows, parity):
    keep = even_rows if parity == 0 else jnp.logical_not(even_rows)
    return jnp.where(keep, vt128, jnp.ones_like(vt128))


def _head_out_t(acc_even, acc_odd, even_rows):
    inv_e = 1.0 / acc_even[HEAD_DIM:HEAD_DIM + 1, :]
    inv_o = 1.0 / acc_odd[0:1, :]
    return jnp.where(even_rows, acc_even * inv_e, acc_odd * inv_o)


def _fox_body(qx_ref, kx_ref, vt_ref, g_ref, o_ref, s_scr, top_scr, p_scr, m_scr, alpha_scr, acc_scr, ot_scr,
              *, tq):
    qi = pl.program_id(1)
    even_rows = _pair_rows()
    m_scr[...] = jnp.full(m_scr.shape, NEG_INF, F32)
    acc_scr[...] = jnp.zeros(acc_scr.shape, F32)
    key = lax.broadcasted_iota(jnp.int32, (tq, tq), 0)
    qry = lax.broadcasted_iota(jnp.int32, (tq, tq), 1)
    causal = key <= qry

    def logits(j, half):
        start = pl.multiple_of(j * tq, tq)
        for h in range(N_HEADS):
            head = slice(h * LANES, (h + 1) * LANES)
            st = lax.dot_general(kx_ref[pl.ds(start, tq), head], qx_ref[:, head], _NT,
                                 preferred_element_type=F32)
            s_scr[half, h] = st
            top_scr[half, h:h + 1, :] = jnp.max(st, axis=0, keepdims=True)

    def weigh(j, half, masked):
        start = pl.multiple_of(j * tq, tq)
        for h in range(N_HEADS):
            st = s_scr[half, h]
            if masked:
                st = jnp.where(causal, st, NEG_INF)
                top = jnp.max(st, axis=0, keepdims=True)
            else:
                top = top_scr[half, h:h + 1, :]
            m_old = m_scr[h:h + 1, :]
            m_new = jnp.maximum(m_old, top)
            p_scr[h] = jnp.exp2(st - m_new).astype(BF16)
            alpha_scr[h:h + 1, :] = jnp.exp2(m_old - m_new)
            m_scr[h:h + 1, :] = m_new
        for h in range(N_HEADS):
            pair = slice((h // 2) * LANES, (h // 2 + 1) * LANES)
            vt = _head_vt(vt_ref[pair, pl.ds(start, tq)], even_rows, h % 2)
            acc_scr[h] = acc_scr[h] * alpha_scr[h:h + 1, :] + jnp.dot(vt, p_scr[h], preferred_element_type=F32)

    def step(j, cur, nxt):
        logits(j + 1, nxt)
        weigh(j, cur, False)

    def body(jj, carry):
        step(2 * jj, 0, 1)
        step(2 * jj + 1, 1, 0)
        return carry

    logits(0, 0)
    lax.fori_loop(0, qi // 2, body, None)
    odd = lax.rem(qi, 2) == 1

    @pl.when(odd)
    def _():
        step(qi - 1, 0, 1)
        weigh(qi, 1, True)

    @pl.when(jnp.logical_not(odd))
    def _():
        weigh(qi, 0, True)
    for hp in range(N_PAIRS):
        ot_scr[hp * LANES:(hp + 1) * LANES, :] = _head_out_t(acc_scr[2 * hp], acc_scr[2 * hp + 1], even_rows)
    o_ref[...] = _rms(ot_scr[...].T, g_ref[...]).astype(BF16)


def _fox_prompt(qx, kx, vat, g_out, *, tq):
    b, s, wx = qx.shape
    w = vat.shape[1]
    return pl.pallas_call(
        functools.partial(_fox_body, tq=tq),
        grid=(b, s // tq),
        in_specs=[pl.BlockSpec((None, tq, wx), lambda i, j: (i, j, 0)),
                  pl.BlockSpec((None, s, wx), lambda i, j: (i, 0, 0)),
                  pl.BlockSpec((None, w, s), lambda i, j: (i, 0, 0)),
                  pl.BlockSpec((1, w), lambda i, j: (0, 0))],
        out_specs=pl.BlockSpec((None, tq, w), lambda i, j: (i, j, 0)),
        out_shape=jax.ShapeDtypeStruct((b, s, w), BF16),
        scratch_shapes=[pltpu.VMEM((2, N_HEADS, tq, tq), F32), pltpu.VMEM((2, N_HEADS, tq), F32),
                        pltpu.VMEM((N_HEADS, tq, tq), BF16),
                        pltpu.VMEM((N_HEADS, tq), F32), pltpu.VMEM((N_HEADS, tq), F32),
                        pltpu.VMEM((N_HEADS, LANES, tq), F32), pltpu.VMEM((w, tq), F32)],
        compiler_params=pltpu.CompilerParams(
            dimension_semantics=("parallel", "arbitrary"), vmem_limit_bytes=VMEM_LIMIT),
        name="fox_prompt",
    )(qx, kx, vat, g_out)


BAND_CHUNKS = 4
BAND_Q = BAND_CHUNKS * CHUNK
BAND_K = (LEFT_CHUNKS + BAND_CHUNKS) * CHUNK
BIAS_ROW = BAND_K + BAND_Q
BAND_K_SAMPLE = (LEFT + CHUNK + LANES - 1) // LANES * LANES


def _prep_band_bias_row(rel_table):
    pivot = LEFT + BAND_Q
    n_hi = pivot - REL_CLIP + 1
    n_mid = min(2 * REL_CLIP, BIAS_ROW - n_hi)
    n_lo = BIAS_ROW - n_hi - n_mid
    parts = [jnp.broadcast_to(rel_table[2 * REL_CLIP:], (n_hi, N_HEADS)),
             rel_table[2 * REL_CLIP - 1::-1][:n_mid],
             jnp.broadcast_to(rel_table[:1], (n_lo, N_HEADS))]
    return jnp.concatenate(parts, axis=0).T.reshape(N_HEADS, 1, BIAS_ROW)


def _band_bias_body(row_ref, bt_ref, bs_ref):
    rows = jnp.broadcast_to(row_ref[...], (BAND_Q, BIAS_ROW))
    skew = pltpu.roll(rows, 0, axis=1, stride=1, stride_axis=0)
    bias = skew[:, BAND_Q:]
    bs_ref[...] = bias[:CHUNK, :BAND_K_SAMPLE]
    qc = lax.broadcasted_iota(jnp.int32, (BAND_Q, BAND_K), 0) // CHUNK
    kc = lax.broadcasted_iota(jnp.int32, (BAND_Q, BAND_K), 1) // CHUNK
    bt_ref[...] = jnp.where((kc >= qc) & (kc <= qc + LEFT_CHUNKS), bias * LOG2E, NEG_INF).T


def _band_bias(bias_row):
    return pl.pallas_call(
        _band_bias_body,
        grid=(N_HEADS,),
        in_specs=[pl.BlockSpec((None, 1, BIAS_ROW), lambda h: (h, 0, 0))],
        out_specs=[pl.BlockSpec((None, BAND_K, BAND_Q), lambda h: (h, 0, 0)),
                   pl.BlockSpec((None, CHUNK, BAND_K_SAMPLE), lambda h: (h, 0, 0))],
        out_shape=[jax.ShapeDtypeStruct((N_HEADS, BAND_K, BAND_Q), F32),
                   jax.ShapeDtypeStruct((N_HEADS, CHUNK, BAND_K_SAMPLE), F32)],
        name="band_bias",
    )(bias_row)


def _band_body(qx_ref, kx_ref, vt_ref, bias_ref, g_ref, o_ref, kpad, vtpad, s_scr, p_scr, ot_scr, *, s_len):
    step = pl.program_id(1)

    @pl.when(step == 0)
    def _():
        lane = lax.broadcasted_iota(jnp.int32, (LEFT, 2 * LANES), 1)
        flags = (lane == _extra_lane(0)) | (lane == LANES + _extra_lane(1))
        pad_pair = jnp.where(flags, NEG_INF, 0.0).astype(BF16)
        for hp in range(N_PAIRS):
            kpad[0:LEFT, 2 * hp * LANES:2 * (hp + 1) * LANES] = pad_pair
        vtpad[:, 0:LEFT] = jnp.zeros((W_GROUP, LEFT), BF16)
        kpad[LEFT:LEFT + s_len, :] = kx_ref[...]
        vtpad[:, LEFT:LEFT + s_len] = vt_ref[...]

    even_rows = _pair_rows()
    start = pl.multiple_of(step * BAND_Q, BAND_Q)
    def logits(h):
        head = slice(h * LANES, (h + 1) * LANES)
        s_scr[h] = bias_ref[h] + lax.dot_general(kpad[pl.ds(start, BAND_K), head], qx_ref[:, head], _NT,
                                                 preferred_element_type=F32)

    def weigh(h):
        st = s_scr[h]
        p_scr[h] = jnp.exp2(st - jnp.max(st, axis=0, keepdims=True)).astype(BF16)

    def values(hp):
        pair = slice(hp * LANES, (hp + 1) * LANES)
        vtwin = vtpad[pair, pl.ds(start, BAND_K)]
        accs = [jnp.dot(_head_vt(vtwin, even_rows, parity), p_scr[2 * hp + parity], preferred_element_type=F32)
                for parity in range(2)]
        ot_scr[pair, :] = _head_out_t(accs[0], accs[1], even_rows)

    lead = N_HEADS // 2
    for h in range(lead):
        logits(h)
    for h in range(N_HEADS):
        weigh(h)
        if h + lead < N_HEADS:
            logits(h + lead)
        if h % 2 == 1:
            values(h // 2)
    o_ref[...] = _rms(ot_scr[...].T, g_ref[...]).astype(BF16)


def _band_prompt(qxb, kxb, vbt, bias_t, g_out):
    b, s, wx = qxb.shape
    w = vbt.shape[1]
    return pl.pallas_call(
        functools.partial(_band_body, s_len=s),
        grid=(b, s // BAND_Q),
        in_specs=[pl.BlockSpec((None, BAND_Q, wx), lambda i, j: (i, j, 0)),
                  pl.BlockSpec((None, s, wx), lambda i, j: (i, 0, 0)),
                  pl.BlockSpec((None, w, s), lambda i, j: (i, 0, 0)),
                  pl.BlockSpec(bias_t.shape, lambda i, j: (0, 0, 0)),
                  pl.BlockSpec((1, w), lambda i, j: (0, 0))],
        out_specs=pl.BlockSpec((None, BAND_Q, w), lambda i, j: (i, j, 0)),
        out_shape=jax.ShapeDtypeStruct((b, s, w), BF16),
        scratch_shapes=[pltpu.VMEM((LEFT + s, wx), BF16), pltpu.VMEM((w, LEFT + s), BF16),
                        pltpu.VMEM((N_HEADS, BAND_K, BAND_Q), F32), pltpu.VMEM((N_HEADS, BAND_K, BAND_Q), BF16),
                        pltpu.VMEM((w, BAND_Q), F32)],
        compiler_params=pltpu.CompilerParams(
            dimension_semantics=("parallel", "arbitrary"), vmem_limit_bytes=VMEM_LIMIT),
        name="band_prompt",
    )(qxb, kxb, vbt, bias_t, g_out)


def _row_to_col(row):
    n = row.shape[-1]
    r = lax.broadcasted_iota(jnp.int32, (n, n), 0)
    c = lax.broadcasted_iota(jnp.int32, (n, n), 1)
    return jnp.sum(jnp.where(r == c, jnp.broadcast_to(row, (n, n)), 0.0), axis=-1, keepdims=True)


def _fox_sample_body(q_ref, kn_ref, vn_ref, lft_ref, kc_ref, vc_ref, clft_ref, g_ref, o_ref,
                     cct_scr, cn_scr, m_scr, acc_scr, o_scr, s_scr, p_scr, alpha_scr, *, t_new, pt):
    p_idx = pl.program_id(1)
    n_p = pl.num_programs(1)
    even = _pair_masks()

    @pl.when(p_idx == 0)
    def _():
        cct = _lane_cumsum(clft_ref[...])
        cct_scr[...] = cct
        cn_scr[...] = _lane_cumsum(lft_ref[...]) + cct[:, cct.shape[1] - 1:]
        m_scr[...] = jnp.full(m_scr.shape, NEG_INF, F32)
        acc_scr[...] = jnp.zeros(acc_scr.shape, F32)

    start = pl.multiple_of(p_idx * pt, pt)
    pair = lambda h: slice((h // 2) * LANES, (h // 2 + 1) * LANES)
    for h in range(N_HEADS):
        s_scr[h] = lax.dot_general(_head_q(q_ref[:, pair(h)], even, h % 2), kc_ref[:, pair(h)].astype(BF16),
                                   _NT, preferred_element_type=F32)
    for h in range(N_HEADS):
        cq = _row_to_col(cn_scr[h:h + 1, :])
        s = s_scr[h] + cq - cct_scr[h:h + 1, pl.ds(start, pt)]
        m = m_scr[h]
        m_new = jnp.maximum(m, jnp.max(s, axis=-1, keepdims=True))
        p_scr[h] = jnp.exp(s - m_new).astype(BF16)
        alpha_scr[h] = jnp.exp(m - m_new)
        m_scr[h] = m_new
    for h in range(N_HEADS):
        v = _head_v(vc_ref[:, pair(h)].astype(BF16), even, h % 2)
        acc_scr[h] = acc_scr[h] * alpha_scr[h] + jnp.dot(p_scr[h], v, preferred_element_type=F32)

    @pl.when(p_idx == n_p - 1)
    def _():
        row = lax.broadcasted_iota(jnp.int32, (t_new, t_new), 0)
        col = lax.broadcasted_iota(jnp.int32, (t_new, t_new), 1)
        logits = []
        for h in range(N_HEADS):
            cn_row = cn_scr[h:h + 1, :]
            s = lax.dot_general(_head_q(q_ref[:, pair(h)], even, h % 2), kn_ref[:, pair(h)], _NT,
                                preferred_element_type=F32)
            logits.append(jnp.where(col <= row, s + _row_to_col(cn_row) - cn_row, NEG_INF))
        weights, scales = [], []
        for h in range(N_HEADS):
            m = m_scr[h]
            m_new = jnp.maximum(m, jnp.max(logits[h], axis=-1, keepdims=True))
            weights.append(jnp.exp(logits[h] - m_new).astype(BF16))
            scales.append(jnp.exp(m - m_new))
        accs = [acc_scr[h] * scales[h] + jnp.dot(weights[h], _head_v(vn_ref[:, pair(h)], even, h % 2),
                                                 preferred_element_type=F32) for h in range(N_HEADS)]
        for hp in range(N_PAIRS):
            o_scr[:, hp * LANES:(hp + 1) * LANES] = _head_out(accs[2 * hp], accs[2 * hp + 1], even)
        o_ref[...] = _rms(o_scr[...], g_ref[...]).astype(BF16)


def _fox_sample(q, kn, vn, lft, kc, vc, clft, g_out, *, pt):
    b, t, w = q.shape
    p_len = kc.shape[1]
    new = lambda: pl.BlockSpec((None, t, w), lambda i, j: (i, 0, 0))
    cache = lambda: pl.BlockSpec((None, pt, w), lambda i, j: (i, j, 0))
    return pl.pallas_call(
        functools.partial(_fox_sample_body, t_new=t, pt=pt),
        grid=(b, p_len // pt),
        in_specs=[new(), new(), new(),
                  pl.BlockSpec((None, N_HEADS, t), lambda i, j: (i, 0, 0)),
                  cache(), cache(),
                  pl.BlockSpec((None, N_HEADS, p_len), lambda i, j: (i, 0, 0)),
                  pl.BlockSpec((1, w), lambda i, j: (0, 0))],
        out_specs=new(),
        out_shape=jax.ShapeDtypeStruct((b, t, w), BF16),
        scratch_shapes=[pltpu.VMEM((N_HEADS, p_len), F32), pltpu.VMEM((N_HEADS, t), F32),
                        pltpu.VMEM((N_HEADS, t, 1), F32), pltpu.VMEM((N_HEADS, t, LANES), F32),
                        pltpu.VMEM((t, w), F32), pltpu.VMEM((N_HEADS, t, pt), F32),
                        pltpu.VMEM((N_HEADS, t, pt), BF16), pltpu.VMEM((N_HEADS, t, 1), F32)],
        compiler_params=pltpu.CompilerParams(
            dimension_semantics=("parallel", "arbitrary"), vmem_limit_bytes=VMEM_LIMIT),
        name="fox_sample",
    )(q, kn, vn, lft, kc, vc, clft, g_out)


def _band_sample_body(q_ref, kn_ref, vn_ref, knf_ref, vnf_ref, kc_ref, vc_ref, bias_ref, g_ref,
                      o_ref, nk_ref, nv_ref, kcat, vcat, o_scr, *, t_new, bp):
    kcat[0:bp, :] = kc_ref[...].astype(BF16)
    vcat[0:bp, :] = vc_ref[...].astype(BF16)
    kcat[bp:bp + t_new, :] = kn_ref[...]
    vcat[bp:bp + t_new, :] = vn_ref[...]
    nk_ref[0:bp - t_new, :] = kc_ref[t_new:bp, :]
    nv_ref[0:bp - t_new, :] = vc_ref[t_new:bp, :]
    nk_ref[bp - t_new:bp, :] = knf_ref[...]
    nv_ref[bp - t_new:bp, :] = vnf_ref[...]
    even = _pair_masks()
    for hp in range(N_PAIRS):
        lanes = slice(hp * LANES, (hp + 1) * LANES)
        q128 = q_ref[:, lanes]
        k = kcat[:, lanes]
        v = vcat[:, lanes]
        accs = []
        for parity in range(2):
            h = 2 * hp + parity
            s = lax.dot_general(_head_q(q128, even, parity), k, _NT, preferred_element_type=F32)
            s = s + bias_ref[h, 0:t_new, 0:bp + t_new]
            p = jnp.exp(s - jnp.max(s, axis=-1, keepdims=True)).astype(BF16)
            accs.append(jnp.dot(p, _head_v(v, even, parity), preferred_element_type=F32))
        o_scr[:, lanes] = _head_out(accs[0], accs[1], even)
    o_ref[...] = _rms(o_scr[...], g_ref[...]).astype(BF16)


def _band_sample(q, kn, vn, knf, vnf, kc, vc, bias, g_out):
    b, t, w = q.shape
    bp = kc.shape[1]
    assert t == CHUNK and bp == LEFT
    new = lambda: pl.BlockSpec((None, t, w), lambda i: (i, 0, 0))
    buf = lambda: pl.BlockSpec((None, bp, w), lambda i: (i, 0, 0))
    return pl.pallas_call(
        functools.partial(_band_sample_body, t_new=t, bp=bp),
        grid=(b,),
        in_specs=[new(), new(), new(), new(), new(), buf(), buf(),
                  pl.BlockSpec(bias.shape, lambda i: (0, 0, 0)),
                  pl.BlockSpec((1, w), lambda i: (0, 0))],
        out_specs=[new(), buf(), buf()],
        out_shape=[jax.ShapeDtypeStruct((b, t, w), BF16), jax.ShapeDtypeStruct((b, bp, w), F32),
                   jax.ShapeDtypeStruct((b, bp, w), F32)],
        scratch_shapes=[pltpu.VMEM((bp + t, w), BF16), pltpu.VMEM((bp + t, w), BF16), pltpu.VMEM((t, w), F32)],
        compiler_params=pltpu.CompilerParams(dimension_semantics=("parallel",), vmem_limit_bytes=VMEM_LIMIT),
        name="band_sample",
    )(q, kn, vn, knf, vnf, kc, vc, bias, g_out)


N_HEADS_MEM = 4
HEAD_DIM_MEM = 128
W_MEM = N_HEADS_MEM * HEAD_DIM_MEM
MEM_SCALE = HEAD_DIM_MEM ** -0.5


def _memkv_body(m_ref, g_ref, w_ref, kf_ref, vf_ref, k_ref, v_ref):
    h = _rms(m_ref[...], g_ref[...]).astype(BF16)
    z = jnp.dot(h, w_ref[...], preferred_element_type=F32)
    kf_ref[...] = z[:, :W_MEM]
    vf_ref[...] = z[:, W_MEM:]
    k_ref[...] = z[:, :W_MEM].astype(BF16)
    v_ref[...] = z[:, W_MEM:].astype(BF16)


def _mem_kv(mem, g_mem, w_ckv):
    b, n, d = mem.shape
    blk = lambda: pl.BlockSpec((None, n, W_MEM), lambda i: (i, 0, 0))
    return pl.pallas_call(
        _memkv_body,
        grid=(b,),
        in_specs=[pl.BlockSpec((None, n, d), lambda i: (i, 0, 0)),
                  pl.BlockSpec((1, d), lambda i: (0, 0)),
                  pl.BlockSpec(w_ckv.shape, lambda i: (0, 0))],
        out_specs=[blk(), blk(), blk(), blk()],
        out_shape=[jax.ShapeDtypeStruct((b, n, W_MEM), F32)] * 2 + [jax.ShapeDtypeStruct((b, n, W_MEM), BF16)] * 2,
        compiler_params=pltpu.CompilerParams(dimension_semantics=("parallel",), vmem_limit_bytes=VMEM_LIMIT),
        name="mem_kv",
    )(mem, g_mem, w_ckv)


N_GROUPS = 4
EXPERTS_PER_GROUP = 8
N_EXPERTS = N_GROUPS * EXPERTS_PER_GROUP
ROUTE_L2 = N_GROUPS
ROUTE_ROWS = 8
POST_CHAIN = 512
R_EID0, R_EID1, R_RANK0, R_RANK1, R_GATE0, R_GATE1 = range(6)


ROUTE_LOGIT_ROWS = 40


def _row_max(x, mask):
    return jnp.max(jnp.where(mask, x, -jnp.inf), axis=0, keepdims=True)


def _first_row(mask, row):
    return jnp.min(jnp.where(mask, row, LANES), axis=0, keepdims=True)


def _route(logits, row):
    is_l1 = row < N_GROUPS
    m1 = _row_max(logits, is_l1)
    grp = _first_row(is_l1 & (logits == m1), row)
    wg = 1.0 / jnp.sum(jnp.where(is_l1, jnp.exp(logits - m1), 0.0), axis=0, keepdims=True)
    lo = ROUTE_L2 + grp * EXPERTS_PER_GROUP
    in_grp = (row >= lo) & (row < lo + EXPERTS_PER_GROUP)
    v0 = _row_max(logits, in_grp)
    i0 = _first_row(in_grp & (logits == v0), row)
    rest = in_grp & (row != i0)
    v1 = _row_max(logits, rest)
    i1 = _first_row(rest & (logits == v1), row)
    e1 = jnp.exp(v1 - v0)
    den = 1.0 / (1.0 + e1)
    return i0, i1, wg * den, wg * e1 * den


def _post_body(x_ref, a_ref, b_ref, mk_ref, mv_ref, woa_ref, wob_ref, gc_ref, wcq_ref, wco_ref,
               gf_ref, wrt_ref, brt_ref,
               x2_ref, h3_ref, route_ref, routet_ref, cnt_ref, o_scr, *, tm, nsub):
    seq = tm // nsub
    count = jnp.zeros((1, LANES), F32)
    for rows in _chains(tm, POST_CHAIN):
        n = rows.size
        x1 = (x_ref[rows, :] + jnp.dot(a_ref[rows, :], woa_ref[...], preferred_element_type=F32)
              + jnp.dot(b_ref[rows, :], wob_ref[...], preferred_element_type=F32))
        h2 = _rms(x1, gc_ref[...]).astype(BF16)
        qc = (jnp.dot(h2, wcq_ref[...], preferred_element_type=F32) * MEM_SCALE).astype(BF16)
        span = min(seq, n)
        for part in range(n // span):
            sub = (rows.start + part * span) // seq
            rs = slice(part * span, (part + 1) * span)
            orow = pl.ds(rows.start + part * span, span)
            for hm in range(N_HEADS_MEM):
                lanes = slice(hm * HEAD_DIM_MEM, (hm + 1) * HEAD_DIM_MEM)
                s = lax.dot_general(qc[rs, lanes], mk_ref[sub, :, lanes], _NT, preferred_element_type=F32)
                p = jnp.exp(s - jnp.max(s, axis=-1, keepdims=True))
                inv = 1.0 / jnp.sum(p, axis=-1, keepdims=True)
                o_scr[orow, lanes] = jnp.dot(p.astype(BF16), mv_ref[sub, :, lanes], preferred_element_type=F32) * inv
        x2 = x1 + jnp.dot(o_scr[rows, :].astype(BF16), wco_ref[...], preferred_element_type=F32)
        x2_ref[rows, :] = x2
        h3 = _rms(x2, gf_ref[...]).astype(BF16)
        h3_ref[rows, :] = h3

        logits = lax.dot_general(wrt_ref[...], h3, _NT, preferred_element_type=F32) + brt_ref[...]
        row = lax.broadcasted_iota(jnp.int32, (ROUTE_LOGIT_ROWS, n), 0)
        i0, i1, g0, g1 = _route(logits[:ROUTE_LOGIT_ROWS, :], row)
        e0 = i0 - ROUTE_L2
        e1 = i1 - ROUTE_L2
        expert = lax.broadcasted_iota(jnp.int32, (LANES, n), 0)
        hit0 = expert == e0
        hit1 = expert == e1
        onehot = jnp.where(hit0, 1.0, jnp.where(hit1, 1.0, 0.0)).astype(BF16)
        earlier = lax.broadcasted_iota(jnp.int32, (n, n), 0) < lax.broadcasted_iota(jnp.int32, (n, n), 1)
        before = jnp.where(earlier, 1.0, 0.0).astype(BF16)
        seen = jnp.dot(onehot, before, preferred_element_type=F32) + _row_to_col(count)
        rank0 = jnp.sum(jnp.where(hit0, seen, 0.0), axis=0, keepdims=True)
        rank1 = jnp.sum(jnp.where(hit1, seen, 0.0), axis=0, keepdims=True)
        count = count + lax.dot_general(jnp.ones((8, n), BF16), onehot, _NT, preferred_element_type=F32)[0:1, :]

        rec = jnp.concatenate([e0.astype(F32), e1.astype(F32), rank0, rank1, g0, g1, jnp.zeros((2, n), F32)], axis=0)
        routet_ref[:, rows] = rec
        route_ref[rows, :] = jnp.concatenate([rec, jnp.zeros((LANES - ROUTE_ROWS, n), F32)], axis=0).T[:, :ROUTE_ROWS]
    cnt_ref[...] = count


def _post_block(x, a_n, b_n, mk, mv, weights, *, tm):
    b, s, d = x.shape
    if s >= tm:
        nsub, grid = 1, (b, s // tm)
        tok = lambda i, j: (i, j, 0)
        flat = lambda i, j: i * (s // tm) + j
    else:
        nsub = tm // s
        assert b % nsub == 0
        x, a_n, b_n = (t.reshape(b // nsub, tm, t.shape[-1]) for t in (x, a_n, b_n))
        grid = (b // nsub, 1)
        tok = lambda i, j: (i, 0, 0)
        flat = lambda i, j: i
    mem = lambda i, j: (i, 0, 0)
    const = lambda arr: pl.BlockSpec(arr.shape, lambda i, j: (0,) * arr.ndim)
    in_specs = [pl.BlockSpec((None, tm, d), tok),
                pl.BlockSpec((None, tm, W_GROUP), tok), pl.BlockSpec((None, tm, W_GROUP), tok),
                pl.BlockSpec((nsub, mk.shape[1], W_MEM), mem), pl.BlockSpec((nsub, mv.shape[1], W_MEM), mem)]
    in_specs += [const(w) for w in weights]
    n = b * s
    out_shape = [jax.ShapeDtypeStruct((n, d), F32), jax.ShapeDtypeStruct((n, d), BF16),
                 jax.ShapeDtypeStruct((n, ROUTE_ROWS), F32), jax.ShapeDtypeStruct((ROUTE_ROWS, n), F32),
                 jax.ShapeDtypeStruct((n // tm, 1, LANES), F32)]
    out_specs = [pl.BlockSpec((tm, d), lambda i, j: (flat(i, j), 0)),
                 pl.BlockSpec((tm, d), lambda i, j: (flat(i, j), 0)),
                 pl.BlockSpec((tm, ROUTE_ROWS), lambda i, j: (flat(i, j), 0)),
                 pl.BlockSpec((ROUTE_ROWS, tm), lambda i, j: (0, flat(i, j))),
                 pl.BlockSpec((None, 1, LANES), lambda i, j: (flat(i, j), 0, 0))]
    return pl.pallas_call(
        functools.partial(_post_body, tm=tm, nsub=nsub),
        grid=grid,
        in_specs=in_specs,
        out_specs=out_specs,
        out_shape=out_shape,
        scratch_shapes=[pltpu.VMEM((tm, W_MEM), F32)],
        compiler_params=pltpu.CompilerParams(
            dimension_semantics=("parallel", "parallel"), vmem_limit_bytes=VMEM_LIMIT),
        name="post_block",
    )(x, a_n, b_n, mk, mv, *weights)


def _prep_post(w_out, g_cross, w_cq, w_co, g_ffn, w_r1, b_r1, w_r2, b_r2):
    pad = LANES - N_GROUPS - N_EXPERTS
    w_rt = jnp.pad(jnp.concatenate([w_r1, w_r2], axis=1), ((0, 0), (0, pad))).astype(BF16).T
    b_rt = jnp.pad(jnp.concatenate([b_r1, b_r2]).reshape(-1, 1), ((0, pad), (0, 0))).astype(F32)
    return [w_out[:W_GROUP].astype(BF16), w_out[W_GROUP:].astype(BF16), g_cross.reshape(1, -1),
            w_cq.astype(BF16), w_co.astype(BF16), g_ffn.reshape(1, -1), w_rt, b_rt]


D_EXPERT = 512
TOP_K = 2
ROW_TILE = 512
MXU_DIM = 256
RUN_ALIGN = 16
PLAN_TILES = LANES
TILE_TABLE = 2 * LANES


def _local_rows(tm):
    return -(-(TOP_K * tm + N_EXPERTS * (RUN_ALIGN - 1)) // MXU_DIM) * MXU_DIM


def _n_row_tiles(n_tokens, tm):
    rows = n_tokens * TOP_K + (n_tokens // tm) * N_EXPERTS * (RUN_ALIGN - 1) + N_EXPERTS * (ROW_TILE - 1)
    return rows // ROW_TILE


N_CHUNK_LANE = LANES - 1


def _plan_body(cnt_ref, lstart_ref, chunk_ref, offs_ref, te_ref):
    cnt = cnt_ref[...].astype(jnp.int32)
    n16 = ((cnt + (RUN_ALIGN - 1)) & ~(RUN_ALIGN - 1)).astype(F32)
    lend = _lane_cumsum(n16)
    lstart = lend - n16
    earlier = (_lane_cumsum(n16.T) - n16.T).T
    total = jnp.sum(n16, axis=0, keepdims=True).astype(jnp.int32)
    seg = jnp.broadcast_to((total + (ROW_TILE - 1)) & ~(ROW_TILE - 1), (8, LANES)).astype(F32)
    ends = _lane_cumsum(seg)
    offs = ends - seg
    shift = earlier + offs[0:1, :] - lstart
    lstart_ref[...] = lstart.astype(jnp.int32)
    offs_ref[...] = offs[0:1, :].astype(jnp.int32)

    lane = lax.broadcasted_iota(jnp.int32, (PLAN_TILES, LANES), 1)
    local_row = (lane * RUN_ALIGN).astype(F32)
    owner = jnp.zeros((PLAN_TILES, LANES), jnp.int32)
    for e in range(N_EXPERTS):
        owner = owner + jnp.where(lend[:, e:e + 1] <= local_row, 1, 0)
    glob = local_row
    for e in range(N_EXPERTS):
        glob = glob + jnp.where(owner == e, shift[:, e:e + 1], 0.0)
    n_chunks = lend[:, N_EXPERTS - 1:N_EXPERTS] * (1.0 / RUN_ALIGN)
    chunk_ref[...] = jnp.where(lane == N_CHUNK_LANE, n_chunks, glob).astype(jnp.int32)

    tile_start = (lax.broadcasted_iota(jnp.int32, te_ref.shape, 1) * ROW_TILE).astype(F32)
    te = jnp.zeros(te_ref.shape, jnp.int32)
    for e in range(N_EXPERTS):
        end_e = jnp.sum(jnp.where(lane[0:1, :] == e, ends[0:1, :], 0.0), axis=-1, keepdims=True)
        te = te + jnp.where(end_e <= tile_start, 1, 0)
    te_ref[...] = jnp.minimum(te, N_EXPERTS - 1)


def _plan(counts, tm):
    nt = counts.shape[0]
    assert nt <= PLAN_TILES and _local_rows(tm) // RUN_ALIGN <= N_CHUNK_LANE
    cnt = jnp.pad(counts.reshape(nt, LANES), ((0, PLAN_TILES - nt), (0, 0)))
    grid_i32 = jax.ShapeDtypeStruct((PLAN_TILES, LANES), jnp.int32)
    lstart, chunks, offs, te = pl.pallas_call(
        _plan_body,
        out_shape=[grid_i32, grid_i32, jax.ShapeDtypeStruct((1, LANES), jnp.int32),
                   jax.ShapeDtypeStruct((1, TILE_TABLE), jnp.int32)],
        name="moe_plan",
    )(cnt)
    per_tile = lambda t: t[:nt].reshape(nt, 1, LANES)
    return per_tile(lstart), per_tile(chunks), offs.reshape(LANES), te.reshape(TILE_TABLE)


def _for_each_chunk(chunk_ref, fn):
    def body(c, carry):
        fn(pl.multiple_of(c * RUN_ALIGN, RUN_ALIGN), pl.multiple_of(chunk_ref[0, c], RUN_ALIGN))
        return carry

    lax.fori_loop(0, chunk_ref[0, N_CHUNK_LANE], body, None)


def _local_positions_row(rt_ref, lstart_ref):
    pos = []
    for r_eid, r_rank in ((R_EID0, R_RANK0), (R_EID1, R_RANK1)):
        eid = rt_ref[r_eid:r_eid + 1, :].astype(jnp.int32)
        p = rt_ref[r_rank:r_rank + 1, :].astype(jnp.int32)
        for e in range(N_EXPERTS):
            p = p + jnp.where(eid == e, lstart_ref[0, e], 0)
        pos.append(p)
    return pos


def _dispatch_body(offs_ref, lstart_ref, chunk_ref, pchunk_ref,
                   hp_ref, hs_ref, rt_ref, xs_ref, loc, zeros, sems, zsem, *, tm, n_prompt_tiles):
    i = pl.program_id(0)
    n_tiles = xs_ref.shape[0] // ROW_TILE
    half = lax.rem(i, 2)

    @pl.when(i == 0)
    def _():
        zeros[...] = jnp.zeros(zeros.shape, zeros.dtype)
        zero_tile = lambda row: pltpu.make_async_copy(
            zeros, xs_ref.at[pl.ds(pl.multiple_of(row, ROW_TILE), ROW_TILE)], zsem)
        n_used = offs_ref[N_EXPERTS] // ROW_TILE

        def tail(j, carry, op):
            op(zero_tile(j * ROW_TILE))
            return carry

        for op in (lambda c: c.start(), lambda c: c.wait()):
            for e in range(N_EXPERTS):
                @pl.when(offs_ref[e + 1] > offs_ref[e])
                def _():
                    op(zero_tile(offs_ref[e + 1] - ROW_TILE))
            lax.fori_loop(n_used, n_tiles, functools.partial(tail, op=op), None)

    pos0, pos1 = _local_positions_row(rt_ref, lstart_ref)
    used = chunk_ref[0, N_CHUNK_LANE] * RUN_ALIGN
    body_rows = loc.shape[1] - MXU_DIM

    def sort_rows(h_ref, lo, n):
        slot = lo + lax.broadcasted_iota(jnp.int32, (n, tm), 0)
        perm = jnp.where(slot == pos0, 1.0, jnp.where(slot == pos1, 1.0, 0.0)).astype(BF16)
        loc[half, lo:lo + n, :] = jnp.dot(perm, h_ref[...], preferred_element_type=F32).astype(BF16)

    for h_ref, mine in ((hp_ref, i < n_prompt_tiles), (hs_ref, i >= n_prompt_tiles)):
        @pl.when(mine)
        def _():
            sort_rows(h_ref, 0, body_rows)

        @pl.when(mine & (used > body_rows))
        def _():
            sort_rows(h_ref, body_rows, MXU_DIM)

    def chunk(buf, lo, go):
        return pltpu.make_async_copy(loc.at[buf, pl.ds(lo, RUN_ALIGN)], xs_ref.at[pl.ds(go, RUN_ALIGN)],
                                     sems.at[buf])

    @pl.when(i > 0)
    def _():
        _for_each_chunk(pchunk_ref, lambda lo, go: chunk(1 - half, lo, go).wait())

    _for_each_chunk(chunk_ref, lambda lo, go: chunk(half, lo, go).start())

    @pl.when(i == pl.num_programs(0) - 1)
    def _():
        _for_each_chunk(chunk_ref, lambda lo, go: chunk(half, lo, go).wait())


def _dispatch(h3_prompt, h3_sample, route_t, plan, *, tm):
    lstart, chunks, offs, _ = plan
    n_p, n_s, d = h3_prompt.shape[0], h3_sample.shape[0], h3_prompt.shape[-1]
    assert n_p % tm == 0 and n_s % tm == 0
    n = n_p + n_s
    nt = n // tm
    npt = n_p // tm
    n_rows = _n_row_tiles(n, tm) * ROW_TILE
    smem_tile = lambda: pl.BlockSpec((None, 1, LANES), lambda i, offs: (i, 0, 0), memory_space=pltpu.SMEM)
    smem_prev = lambda: pl.BlockSpec((None, 1, LANES), lambda i, offs: (jnp.maximum(i - 1, 0), 0, 0),
                                     memory_space=pltpu.SMEM)
    return pl.pallas_call(
        functools.partial(_dispatch_body, tm=tm, n_prompt_tiles=npt),
        grid_spec=pltpu.PrefetchScalarGridSpec(
            num_scalar_prefetch=1,
            grid=(nt,),
            in_specs=[smem_tile(), smem_tile(), smem_prev(),
                      pl.BlockSpec((tm, d), lambda i, offs: (jnp.minimum(i, npt - 1), 0)),
                      pl.BlockSpec((tm, d), lambda i, offs: (jnp.maximum(i - npt, 0), 0)),
                      pl.BlockSpec((ROUTE_ROWS, tm), lambda i, offs: (0, i))],
            out_specs=pl.BlockSpec(memory_space=pl.ANY),
            scratch_shapes=[pltpu.VMEM((2, _local_rows(tm), d), BF16), pltpu.VMEM((ROW_TILE, d), BF16),
                            pltpu.SemaphoreType.DMA((2,)), pltpu.SemaphoreType.DMA(())]),
        out_shape=jax.ShapeDtypeStruct((n_rows, d), BF16),
        compiler_params=pltpu.CompilerParams(dimension_semantics=("arbitrary",), vmem_limit_bytes=VMEM_LIMIT),
        name="moe_dispatch",
    )(offs, lstart, chunks, chunks, h3_prompt, h3_sample, route_t)


def _experts_body(te_ref, offs_ref, xs_ref, wg_ref, wu_ref, wd_ref, ys_ref,
                  wg_buf, wu_buf, wd_buf, wgu_bf, wd_bf, turn_ref, sems):
    i = pl.program_id(0)
    n_used = offs_ref[N_EXPERTS] // ROW_TILE

    def fetch(expert, half):
        return [pltpu.make_async_copy(src.at[expert], dst.at[half], sems.at[half, k])
                for k, (src, dst) in enumerate(((wg_ref, wg_buf), (wu_ref, wu_buf), (wd_ref, wd_buf)))]

    @pl.when(i == 0)
    def _():
        turn_ref[0] = 0
        for copy in fetch(te_ref[0], 0):
            copy.start()

    @pl.when(i < n_used)
    def _():
        expert = te_ref[i]

        @pl.when((i == 0) | (expert != te_ref[jnp.maximum(i - 1, 0)]))
        def _():
            half = lax.rem(turn_ref[0], 2)
            turn_ref[0] = turn_ref[0] + 1
            for copy in fetch(expert, half):
                copy.wait()
            wgu_bf[:, 0:D_EXPERT] = wg_buf[half].astype(BF16)
            wgu_bf[:, D_EXPERT:] = wu_buf[half].astype(BF16)
            wd_bf[...] = wd_buf[half].astype(BF16)
            following = offs_ref[expert + 1] // ROW_TILE

            @pl.when(following < n_used)
            def _():
                for copy in fetch(te_ref[following], 1 - half):
                    copy.start()

        dot = functools.partial(jnp.dot, preferred_element_type=F32)
        for rows in _chains(ROW_TILE, ROW_TILE):
            gate_up = dot(xs_ref[rows, :], wgu_bf[...])
            gate, up = gate_up[:, :D_EXPERT], gate_up[:, D_EXPERT:]
            act = (gate * jax.nn.sigmoid(gate) * up).astype(BF16)
            ys_ref[rows, :] = dot(act, wd_bf[...]).astype(BF16)

    @pl.when(i >= n_used)
    def _():
        ys_ref[...] = jnp.zeros(ys_ref.shape, ys_ref.dtype)


def _experts(xs, te, offs, w_gate, w_up, w_down):
    n_rows, d = xs.shape
    last = lambda i, te, offs: jnp.minimum(i, offs[N_EXPERTS] // ROW_TILE - 1)
    hbm = pl.BlockSpec(memory_space=pl.ANY)
    return pl.pallas_call(
        _experts_body,
        grid_spec=pltpu.PrefetchScalarGridSpec(
            num_scalar_prefetch=2,
            grid=(n_rows // ROW_TILE,),
            in_specs=[pl.BlockSpec((ROW_TILE, d), lambda i, te, offs: (last(i, te, offs), 0)), hbm, hbm, hbm],
            out_specs=pl.BlockSpec((ROW_TILE, d), lambda i, te, offs: (i, 0)),
            scratch_shapes=[pltpu.VMEM((2, d, D_EXPERT), F32), pltpu.VMEM((2, d, D_EXPERT), F32),
                            pltpu.VMEM((2, D_EXPERT, d), F32),
                            pltpu.VMEM((d, 2 * D_EXPERT), BF16), pltpu.VMEM((D_EXPERT, d), BF16),
                            pltpu.SMEM((1,), jnp.int32), pltpu.SemaphoreType.DMA((2, 3))]),
        out_shape=jax.ShapeDtypeStruct((n_rows, d), BF16),
        compiler_params=pltpu.CompilerParams(dimension_semantics=("arbitrary",), vmem_limit_bytes=VMEM_LIMIT),
        name="moe_experts",
    )(te, offs, xs, w_gate, w_up, w_down)


def _combine_body(chunk_ref, nchunk_ref, x2p_ref, x2s_ref, route_ref, lsv_ref, ys_ref, g_ref,
                  yp_ref, ysm_ref, loc, sems, *, tm, n_prompt_tiles):
    i = pl.program_id(0)
    half = lax.rem(i, 2)

    def chunk(buf, lo, go):
        return pltpu.make_async_copy(ys_ref.at[pl.ds(go, RUN_ALIGN)], loc.at[buf, pl.ds(lo, RUN_ALIGN)],
                                     sems.at[buf])

    @pl.when(i == 0)
    def _():
        loc[...] = jnp.zeros(loc.shape, loc.dtype)
        _for_each_chunk(chunk_ref, lambda lo, go: chunk(0, lo, go).start())

    @pl.when(i + 1 < pl.num_programs(0))
    def _():
        _for_each_chunk(nchunk_ref, lambda lo, go: chunk(1 - half, lo, go).start())

    _for_each_chunk(chunk_ref, lambda lo, go: chunk(half, lo, go).wait())

    lane = lax.broadcasted_iota(jnp.int32, (tm, LANES), 1)
    picks = []
    for r_eid, r_rank, r_gate in ((R_EID0, R_RANK0, R_GATE0), (R_EID1, R_RANK1, R_GATE1)):
        eid = route_ref[:, r_eid:r_eid + 1].astype(jnp.int32)
        start = jnp.sum(jnp.where(lane == eid, lsv_ref[...], 0), axis=-1, keepdims=True)
        picks.append((route_ref[:, r_rank:r_rank + 1].astype(jnp.int32) + start, route_ref[:, r_gate:r_gate + 1]))

    slot = lax.broadcasted_iota(jnp.int32, (tm, loc.shape[1]), 1)
    weights = jnp.zeros(slot.shape, F32)
    for pos, gate in picks:
        weights = jnp.where(slot == pos, gate, weights)
    moe = jnp.dot(weights.astype(BF16), loc[half], preferred_element_type=F32)

    @pl.when(i < n_prompt_tiles)
    def _():
        yp_ref[...] = _rms(x2p_ref[...] + moe, g_ref[...])

    @pl.when(i >= n_prompt_tiles)
    def _():
        ysm_ref[...] = _rms(x2s_ref[...] + moe, g_ref[...])


def _combine(x2_prompt, x2_sample, route, plan, ys, g_final, *, tm):
    lstart, chunks, _, _ = plan
    (n_p, d), n_s = x2_prompt.shape, x2_sample.shape[0]
    assert n_p % tm == 0 and n_s % tm == 0
    npt = n_p // tm
    nt = npt + n_s // tm
    smem_tile = lambda: pl.BlockSpec((None, 1, LANES), lambda i: (i, 0, 0), memory_space=pltpu.SMEM)
    smem_next = lambda: pl.BlockSpec((None, 1, LANES), lambda i: (jnp.minimum(i + 1, nt - 1), 0, 0),
                                     memory_space=pltpu.SMEM)
    prompt_tile = lambda: pl.BlockSpec((tm, d), lambda i: (jnp.minimum(i, npt - 1), 0))
    sample_tile = lambda: pl.BlockSpec((tm, d), lambda i: (jnp.maximum(i - npt, 0), 0))
    return pl.pallas_call(
        functools.partial(_combine_body, tm=tm, n_prompt_tiles=npt),
        grid=(nt,),
        in_specs=[smem_tile(), smem_next(), prompt_tile(), sample_tile(),
                  pl.BlockSpec((tm, ROUTE_ROWS), lambda i: (i, 0)),
                  pl.BlockSpec((None, 1, LANES), lambda i: (i, 0, 0)),
                  pl.BlockSpec(memory_space=pl.ANY),
                  pl.BlockSpec((1, d), lambda i: (0, 0))],
        out_specs=[prompt_tile(), sample_tile()],
        out_shape=[jax.ShapeDtypeStruct((n_p, d), F32), jax.ShapeDtypeStruct((n_s, d), F32)],
        scratch_shapes=[pltpu.VMEM((2, _local_rows(tm), d), BF16), pltpu.SemaphoreType.DMA((2,))],
        compiler_params=pltpu.CompilerParams(dimension_semantics=("arbitrary",), vmem_limit_bytes=VMEM_LIMIT),
        name="moe_combine",
    )(chunks, chunks, x2_prompt, x2_sample, route, lstart, ys, g_final)


TOKEN_TILE = 512
FOX_Q_TILE = 256
FOX_CACHE_TILE = 2048


def kernel(x_prompt, x_sample, cache_fox_k, cache_fox_v, cache_fox_logf, cache_band_k, cache_band_v, cache_mem_k, cache_mem_v, mem_prompt, g_mix, w_in, b_forget, g_out_fox, g_out_band, rel_table, w_out, g_cross, g_mem, w_cq, w_ck, w_cv, w_co, g_ffn, w_router1, b_router1, w_router2, b_router2, w_exp_gate, w_exp_up, w_exp_down, g_final):
    assert g_mix.shape[0] == 1, "single-layer model"
    bsz, seq, d = x_prompt.shape
    sb, st, _ = x_sample.shape
    n_s = sb * st
    n_mem = mem_prompt.shape[1]
    row = lambda g: g.reshape(1, -1)

    w_pad, bf_pad, g_mix_r = _prep_proj(w_in[0], b_forget[0], g_mix[0])
    g_of, g_ob = row(g_out_fox[0]), row(g_out_band[0])
    bias_t, bias_s = _band_bias(_prep_band_bias_row(rel_table[0]))

    qx, kx, vat, qxb, kxb, vbt, kaf, vaf, kbf, vbf, logf = _proj(
        x_prompt, g_mix_r, w_pad, bf_pad, tm=TOKEN_TILE, prompt=True)
    a_p = _fox_prompt(qx, kx, vat, g_of, tq=FOX_Q_TILE)
    b_p = _band_prompt(qxb, kxb, vbt, bias_t, g_ob)

    s_out = _proj(x_sample.reshape(1, n_s, d), g_mix_r, w_pad, bf_pad, tm=n_s, prompt=False)
    sqa, ska, sva, sqb, skb, svb, skaf, svaf, skbf, svbf = (t.reshape(sb, st, W_GROUP) for t in s_out[:10])
    slogf = s_out[10].reshape(sb, st, N_HEADS)
    slft = s_out[11].reshape(N_HEADS, sb, st).transpose(1, 0, 2)
    past = cache_fox_k.shape[2]
    a_s = _fox_sample(sqa, ska, sva, slft,
                      cache_fox_k[0].reshape(sb, past, W_GROUP), cache_fox_v[0].reshape(sb, past, W_GROUP),
                      cache_fox_logf[0].transpose(0, 2, 1), g_of, pt=FOX_CACHE_TILE)
    bp = cache_band_k.shape[2]
    b_s, nbk, nbv = _band_sample(sqb, skb, svb, skbf, svbf,
                                 cache_band_k[0].reshape(sb, bp, W_GROUP), cache_band_v[0].reshape(sb, bp, W_GROUP),
                                 bias_s, g_ob)

    w_ckv = jnp.concatenate([w_ck[0], w_cv[0]], axis=1).astype(BF16)
    mkf, mvf, mk, mv = _mem_kv(mem_prompt, row(g_mem[0]), w_ckv)
    post_w = _prep_post(w_out[0], g_cross[0], w_cq[0], w_co[0], g_ffn[0],
                        w_router1[0], b_router1[0], w_router2[0], b_router2[0])
    x2_p, h3_p, route_p, routet_p, cnt_p = _post_block(x_prompt, a_p, b_p, mk, mv, post_w, tm=TOKEN_TILE)
    cmk = cache_mem_k[0].reshape(sb, n_mem, W_MEM).astype(BF16)
    cmv = cache_mem_v[0].reshape(sb, n_mem, W_MEM).astype(BF16)
    x2_s, h3_s, route_s, routet_s, cnt_s = _post_block(x_sample, a_s, b_s, cmk, cmv, post_w, tm=TOKEN_TILE)
    route = jnp.concatenate([route_p, route_s], axis=0)
    route_t = jnp.concatenate([routet_p, routet_s], axis=1)

    plan = _plan(jnp.concatenate([cnt_p, cnt_s], axis=0), TOKEN_TILE)
    xs = _dispatch(h3_p, h3_s, route_t, plan, tm=TOKEN_TILE)
    ys = _experts(xs, plan[3], plan[2], w_exp_gate[0], w_exp_up[0], w_exp_down[0])
    y_p, y_s = _combine(x2_p, x2_s, route, plan, ys, row(g_final), tm=TOKEN_TILE)

    heads = lambda t, n: t.reshape(1, n, -1, N_HEADS, HEAD_DIM)
    mem_heads = lambda t: t.reshape(1, bsz, n_mem, N_HEADS_MEM, HEAD_DIM_MEM)
    return (y_p.reshape(bsz, seq, d), y_s.reshape(sb, st, d),
            heads(kaf, bsz), heads(vaf, bsz), logf.reshape(1, bsz, seq, N_HEADS),
            heads(kbf, bsz), heads(vbf, bsz), mem_heads(mkf), mem_heads(mvf),
            heads(skaf, sb), heads(svaf, sb), slogf.reshape(1, sb, st, N_HEADS),
            heads(nbk, sb), heads(nbv, sb))
```

```python
import functools

import jax
import jax.numpy as jnp
from jax import lax
from jax.experimental import pallas as pl
from jax.experimental.pallas import tpu as pltpu

F32 = jnp.float32
BF16 = jnp.bfloat16

D_MODEL = 1024
HEAD_DIM = 64
N_HEADS = 8
W_GROUP = N_HEADS * HEAD_DIM
N_PAIRS = N_HEADS // 2
CHUNK = 64
LEFT_CHUNKS = 8
LEFT = LEFT_CHUNKS * CHUNK
REL_CLIP = 128
EPS = 1e-6
NEG_INF = -1e30
ATTN_SCALE = HEAD_DIM ** -0.5
LANES = 128
PROJ_PAD = 3 * W_GROUP * 2 + LANES
VMEM_LIMIT = 56 * 1024 * 1024


def _rms(x, g):
    ms = jnp.mean(x * x, axis=-1, keepdims=True)
    return x * lax.rsqrt(ms + EPS) * g


def _log_sigmoid(x):
    return -(jnp.maximum(-x, 0.0) + jnp.log1p(jnp.exp(-jnp.abs(x))))


def _lane_cumsum(x):
    n = x.shape[-1]
    lane = lax.broadcasted_iota(jnp.int32, x.shape, 1)
    k = 1
    while k < n:
        x = x + jnp.where(lane >= k, pltpu.roll(x, k, axis=1), 0.0)
        k *= 2
    return x


LOG2E = 1.4426950408889634
SCALE_BASE2 = ATTN_SCALE * LOG2E


def _split3(x):
    hi = x.astype(BF16).astype(F32)
    mid = (x - hi).astype(BF16).astype(F32)
    lo = x - hi - mid
    return hi, mid, lo


def _extra_lane(parity):
    return HEAD_DIM if parity == 0 else 0


def _fox_extras(c3t, hp, tm):
    row = lax.broadcasted_iota(jnp.int32, (8, tm), 0)

    def group(h, q_side):
        hi, mid, lo = (p[h:h + 1, :] for p in c3t)
        if q_side:
            return jnp.where(row < 3, 1.0, jnp.where(row == 3, hi, jnp.where(row == 4, mid, jnp.where(row == 5, lo, 0.0))))
        return jnp.where(row == 0, -hi, jnp.where(row == 1, -mid, jnp.where(row == 2, -lo, jnp.where(row < 6, 1.0, 0.0))))

    gap = jnp.zeros((HEAD_DIM - 8, tm), F32)
    sides = []
    for q_side in (True, False):
        t = jnp.concatenate([group(2 * hp + 1, q_side), gap, group(2 * hp, q_side), gap], axis=0)
        sides.append(t.T)
    return sides


def _head_blocks(x128, extras, lane):
    return (jnp.where(lane < HEAD_DIM, x128, extras).astype(BF16),
            jnp.where(lane >= HEAD_DIM, x128, extras).astype(BF16))


Q_A, K_A, V_A, Q_B, K_B, V_B = range(6)


PROJ_CHAIN = 256


def _chains(tm, chain=PROJ_CHAIN):
    n = max(tm // chain, 1)
    return [pl.ds(i * (tm // n), tm // n) for i in range(n)]


def _proj_common(rows, x_ref, g_ref, w_ref, bf_ref, kaf_ref, vaf_ref, kbf_ref, vbf_ref, logf_ref, keep_tiles):
    s = pl.program_id(1)
    ns = pl.num_programs(1)
    h = _rms(x_ref[rows, :], g_ref[...]).astype(BF16)
    w = W_GROUP
    zf = jnp.dot(h, w_ref[:, 6 * w:6 * w + LANES], preferred_element_type=F32)
    z = [jnp.dot(h, w_ref[:, g * w:(g + 1) * w], preferred_element_type=F32) for g in range(6)]
    kaf_ref[rows, :] = z[K_A]
    vaf_ref[rows, :] = z[V_A]

    @pl.when(s >= ns - keep_tiles)
    def _():
        kbf_ref[rows, :] = z[K_B]
        vbf_ref[rows, :] = z[V_B]

    logf = _log_sigmoid(zf + bf_ref[...])
    logf_ref[rows, :] = logf[:, :N_HEADS]
    return z, logf


def _proj_prompt_body(x_ref, g_ref, w_ref, bf_ref, qx_ref, kx_ref, vat_ref, qxb_ref, kxb_ref, vbt_ref,
                      kaf_ref, vaf_ref, kbf_ref, vbf_ref, logf_ref, carry_ref, *, tm, keep_tiles):
    @pl.when(pl.program_id(1) == 0)
    def _():
        carry_ref[...] = jnp.zeros_like(carry_ref)

    for rows in _chains(tm):
        n = rows.size
        z, logf = _proj_common(rows, x_ref, g_ref, w_ref, bf_ref, kaf_ref, vaf_ref, kbf_ref, vbf_ref, logf_ref,
                               keep_tiles)
        vat_ref[:, rows] = z[V_A].T.astype(BF16)
        vbt_ref[:, rows] = z[V_B].T.astype(BF16)
        ct = _lane_cumsum(logf.T[:N_HEADS, :]) + carry_ref[:, 0:1]
        carry_ref[...] = jnp.broadcast_to(ct[:, n - 1:n], carry_ref.shape)
        c3t = _split3(ct * LOG2E)
        lane = lax.broadcasted_iota(jnp.int32, (n, LANES), 1)
        band_q_extras = jnp.where((lane == _extra_lane(0)) | (lane == _extra_lane(1)), 1.0, 0.0)
        band_k_extras = jnp.zeros((n, LANES), F32)
        for hp in range(N_PAIRS):
            blocks = slice(2 * hp * LANES, 2 * (hp + 1) * LANES)
            blk = lambda group, hp=hp, z=z: z[group][:, hp * LANES:(hp + 1) * LANES]
            q_extras, k_extras = _fox_extras(c3t, hp, n)
            qx_ref[rows, blocks] = jnp.concatenate(_head_blocks(blk(Q_A) * SCALE_BASE2, q_extras, lane), axis=1)
            kx_ref[rows, blocks] = jnp.concatenate(_head_blocks(blk(K_A), k_extras, lane), axis=1)
            qxb_ref[rows, blocks] = jnp.concatenate(_head_blocks(blk(Q_B) * SCALE_BASE2, band_q_extras, lane), axis=1)
            kxb_ref[rows, blocks] = jnp.concatenate(_head_blocks(blk(K_B), band_k_extras, lane), axis=1)


def _proj_sample_body(x_ref, g_ref, w_ref, bf_ref, qa_ref, ka_ref, va_ref, qb_ref, kb_ref, vb_ref,
                      kaf_ref, vaf_ref, kbf_ref, vbf_ref, logf_ref, lt_ref, *, tm, keep_tiles):
    for rows in _chains(tm):
        z, logf = _proj_common(rows, x_ref, g_ref, w_ref, bf_ref, kaf_ref, vaf_ref, kbf_ref, vbf_ref, logf_ref,
                               keep_tiles)
        qa_ref[rows, :] = (z[Q_A] * ATTN_SCALE).astype(BF16)
        ka_ref[rows, :] = z[K_A].astype(BF16)
        va_ref[rows, :] = z[V_A].astype(BF16)
        qb_ref[rows, :] = (z[Q_B] * ATTN_SCALE).astype(BF16)
        kb_ref[rows, :] = z[K_B].astype(BF16)
        vb_ref[rows, :] = z[V_B].astype(BF16)
        lt_ref[:, rows] = logf.T[:N_HEADS, :]


def _proj(x, g_mix, w_pad, bf_pad, *, tm, prompt):
    b, s, d = x.shape
    ns = s // tm
    keep = min(LEFT, s)
    assert s % tm == 0 and keep % tm == 0
    keep_tiles = keep // tm
    row = pl.BlockSpec((None, tm, W_GROUP), lambda i, j: (i, j, 0))
    wide = pl.BlockSpec((None, tm, N_HEADS * LANES), lambda i, j: (i, j, 0))
    col = pl.BlockSpec((None, W_GROUP, tm), lambda i, j: (i, 0, j))
    keep_spec = pl.BlockSpec((None, tm, W_GROUP), lambda i, j: (i, jnp.maximum(j - (ns - keep_tiles), 0), 0))
    heads_row = pl.BlockSpec((None, tm, N_HEADS), lambda i, j: (i, j, 0))
    heads_col = pl.BlockSpec((None, N_HEADS, tm), lambda i, j: (i, 0, j))
    const = lambda shape: pl.BlockSpec(shape, lambda i, j: (0,) * len(shape))
    rows_bf = jax.ShapeDtypeStruct((b, s, W_GROUP), BF16)
    wide_bf = jax.ShapeDtypeStruct((b, s, N_HEADS * LANES), BF16)
    cols_bf = jax.ShapeDtypeStruct((b, W_GROUP, s), BF16)
    f32_tail = [jax.ShapeDtypeStruct((b, s, W_GROUP), F32)] * 2
    f32_tail += [jax.ShapeDtypeStruct((b, keep, W_GROUP), F32)] * 2
    f32_tail += [jax.ShapeDtypeStruct((b, s, N_HEADS), F32)]
    tail_specs = [row, row, keep_spec, keep_spec, heads_row]
    if prompt:
        body = functools.partial(_proj_prompt_body, tm=tm, keep_tiles=keep_tiles)
        out_shape = [wide_bf, wide_bf, cols_bf, wide_bf, wide_bf, cols_bf] + f32_tail
        out_specs = [wide, wide, col, wide, wide, col] + tail_specs
        scratch = [pltpu.VMEM((N_HEADS, LANES), F32)]
    else:
        body = functools.partial(_proj_sample_body, tm=tm, keep_tiles=keep_tiles)
        out_shape = [rows_bf] * 6 + f32_tail + [jax.ShapeDtypeStruct((b, N_HEADS, s), F32)]
        out_specs = [row] * 6 + tail_specs + [heads_col]
        scratch = []
    return pl.pallas_call(
        body,
        grid=(b, ns),
        in_specs=[pl.BlockSpec((None, tm, d), lambda i, j: (i, j, 0)),
                  const((1, d)), const(w_pad.shape), const((1, LANES))],
        out_specs=out_specs,
        out_shape=out_shape,
        scratch_shapes=scratch,
        compiler_params=pltpu.CompilerParams(
            dimension_semantics=("parallel", "arbitrary"), vmem_limit_bytes=VMEM_LIMIT),
        name="proj",
    )(x, g_mix, w_pad, bf_pad)


def _prep_proj(w_in, b_forget, g_mix):
    cols = w_in.shape[-1]
    w_pad = jnp.pad(w_in, ((0, 0), (0, PROJ_PAD - cols))).astype(BF16)
    bf_pad = jnp.pad(b_forget.reshape(1, -1), ((0, 0), (0, LANES - N_HEADS))).astype(F32)
    return w_pad, bf_pad, g_mix.reshape(1, -1)


def _pair_masks():
    lane = lax.broadcasted_iota(jnp.int32, (1, LANES), 1)
    return lane < HEAD_DIM


def _head_q(q128, even_lanes, parity):
    keep = even_lanes if parity == 0 else jnp.logical_not(even_lanes)
    return jnp.where(keep, q128, jnp.zeros_like(q128))


def _head_v(v128, even_lanes, parity):
    keep = even_lanes if parity == 0 else jnp.logical_not(even_lanes)
    return jnp.where(keep, v128, jnp.ones_like(v128))


def _head_out(acc_even, acc_odd, even_lanes):
    inv_e = 1.0 / acc_even[:, HEAD_DIM:HEAD_DIM + 1]
    inv_o = 1.0 / acc_odd[:, 0:1]
    return jnp.where(even_lanes, acc_even * inv_e, acc_odd * inv_o)


_NT = (((1,), (1,)), ((), ()))


def _pair_rows():
    row = lax.broadcasted_iota(jnp.int32, (LANES, 1), 0)
    return row < HEAD_DIM


def _head_vt(vt128, even_rows, parity):
    keep = even_rows if parity == 0 else jnp.logical_not(even_rows)
    return jnp.where(keep, vt128, jnp.ones_like(vt128))


def _head_out_t(acc_even, acc_odd, even_rows):
    inv_e = 1.0 / acc_even[HEAD_DIM:HEAD_DIM + 1, :]
    inv_o = 1.0 / acc_odd[0:1, :]
    return jnp.where(even_rows, acc_even * inv_e, acc_odd * inv_o)


def _fox_body(qx_ref, kx_ref, vt_ref, g_ref, o_ref, s_scr, top_scr, p_scr, m_scr, alpha_scr, acc_scr, ot_scr,
              *, tq):
    qi = pl.program_id(1)
    even_rows = _pair_rows()
    m_scr[...] = jnp.full(m_scr.shape, NEG_INF, F32)
    acc_scr[...] = jnp.zeros(acc_scr.shape, F32)
    key = lax.broadcasted_iota(jnp.int32, (tq, tq), 0)
    qry = lax.broadcasted_iota(jnp.int32, (tq, tq), 1)
    causal = key <= qry

    def logits(j, half):
        start = pl.multiple_of(j * tq, tq)
        for h in range(N_HEADS):
            head = slice(h * LANES, (h + 1) * LANES)
            st = lax.dot_general(kx_ref[pl.ds(start, tq), head], qx_ref[:, head], _NT,
                                 preferred_element_type=F32)
            s_scr[half, h] = st
            top_scr[half, h:h + 1, :] = jnp.max(st, axis=0, keepdims=True)

    def weigh(j, half, masked):
        start = pl.multiple_of(j * tq, tq)
        for h in range(N_HEADS):
            st = s_scr[half, h]
            if masked:
                st = jnp.where(causal, st, NEG_INF)
                top = jnp.max(st, axis=0, keepdims=True)
            else:
                top = top_scr[half, h:h + 1, :]
            m_old = m_scr[h:h + 1, :]
            m_new = jnp.maximum(m_old, top)
            p_scr[h] = jnp.exp2(st - m_new).astype(BF16)
            alpha_scr[h:h + 1, :] = jnp.exp2(m_old - m_new)
            m_scr[h:h + 1, :] = m_new
        for h in range(N_HEADS):
            pair = slice((h // 2) * LANES, (h // 2 + 1) * LANES)
            vt = _head_vt(vt_ref[pair, pl.ds(start, tq)], even_rows, h % 2)
            acc_scr[h] = acc_scr[h] * alpha_scr[h:h + 1, :] + jnp.dot(vt, p_scr[h], preferred_element_type=F32)

    def step(j, cur, nxt):
        logits(j + 1, nxt)
        weigh(j, cur, False)

    def body(jj, carry):
        step(2 * jj, 0, 1)
        step(2 * jj + 1, 1, 0)
        return carry

    logits(0, 0)
    lax.fori_loop(0, qi // 2, body, None)
    odd = lax.rem(qi, 2) == 1

    @pl.when(odd)
    def _():
        step(qi - 1, 0, 1)
        weigh(qi, 1, True)

    @pl.when(jnp.logical_not(odd))
    def _():
        weigh(qi, 0, True)
    for hp in range(N_PAIRS):
        ot_scr[hp * LANES:(hp + 1) * LANES, :] = _head_out_t(acc_scr[2 * hp], acc_scr[2 * hp + 1], even_rows)
    o_ref[...] = _rms(ot_scr[...].T, g_ref[...]).astype(BF16)


def _fox_prompt(qx, kx, vat, g_out, *, tq):
    b, s, wx = qx.shape
    w = vat.shape[1]
    return pl.pallas_call(
        functools.partial(_fox_body, tq=tq),
        grid=(b, s // tq),
        in_specs=[pl.BlockSpec((None, tq, wx), lambda i, j: (i, j, 0)),
                  pl.BlockSpec((None, s, wx), lambda i, j: (i, 0, 0)),
                  pl.BlockSpec((None, w, s), lambda i, j: (i, 0, 0)),
                  pl.BlockSpec((1, w), lambda i, j: (0, 0))],
        out_specs=pl.BlockSpec((None, tq, w), lambda i, j: (i, j, 0)),
        out_shape=jax.ShapeDtypeStruct((b, s, w), BF16),
        scratch_shapes=[pltpu.VMEM((2, N_HEADS, tq, tq), F32), pltpu.VMEM((2, N_HEADS, tq), F32),
                        pltpu.VMEM((N_HEADS, tq, tq), BF16),
                        pltpu.VMEM((N_HEADS, tq), F32), pltpu.VMEM((N_HEADS, tq), F32),
                        pltpu.VMEM((N_HEADS, LANES, tq), F32), pltpu.VMEM((w, tq), F32)],
        compiler_params=pltpu.CompilerParams(
            dimension_semantics=("parallel", "arbitrary"), vmem_limit_bytes=VMEM_LIMIT),
        name="fox_prompt",
    )(qx, kx, vat, g_out)


BAND_CHUNKS = 4
BAND_Q = BAND_CHUNKS * CHUNK
BAND_K = (LEFT_CHUNKS + BAND_CHUNKS) * CHUNK
BIAS_ROW = BAND_K + BAND_Q
BAND_K_SAMPLE = (LEFT + CHUNK + LANES - 1) // LANES * LANES


def _prep_band_bias_row(rel_table):
    pivot = LEFT + BAND_Q
    n_hi = pivot - REL_CLIP + 1
    n_mid = min(2 * REL_CLIP, BIAS_ROW - n_hi)
    n_lo = BIAS_ROW - n_hi - n_mid
    parts = [jnp.broadcast_to(rel_table[2 * REL_CLIP:], (n_hi, N_HEADS)),
             rel_table[2 * REL_CLIP - 1::-1][:n_mid],
             jnp.broadcast_to(rel_table[:1], (n_lo, N_HEADS))]
    return jnp.concatenate(parts, axis=0).T.reshape(N_HEADS, 1, BIAS_ROW)


def _band_bias_body(row_ref, bt_ref, bs_ref):
    rows = jnp.broadcast_to(row_ref[...], (BAND_Q, BIAS_ROW))
    skew = pltpu.roll(rows, 0, axis=1, stride=1, stride_axis=0)
    bias = skew[:, BAND_Q:]
    bs_ref[...] = bias[:CHUNK, :BAND_K_SAMPLE]
    qc = lax.broadcasted_iota(jnp.int32, (BAND_Q, BAND_K), 0) // CHUNK
    kc = lax.broadcasted_iota(jnp.int32, (BAND_Q, BAND_K), 1) // CHUNK
    bt_ref[...] = jnp.where((kc >= qc) & (kc <= qc + LEFT_CHUNKS), bias * LOG2E, NEG_INF).T


def _band_bias(bias_row):
    return pl.pallas_call(
        _band_bias_body,
        grid=(N_HEADS,),
        in_specs=[pl.BlockSpec((None, 1, BIAS_ROW), lambda h: (h, 0, 0))],
        out_specs=[pl.BlockSpec((None, BAND_K, BAND_Q), lambda h: (h, 0, 0)),
                   pl.BlockSpec((None, CHUNK, BAND_K_SAMPLE), lambda h: (h, 0, 0))],
        out_shape=[jax.ShapeDtypeStruct((N_HEADS, BAND_K, BAND_Q), F32),
                   jax.ShapeDtypeStruct((N_HEADS, CHUNK, BAND_K_SAMPLE), F32)],
        name="band_bias",
    )(bias_row)


def _band_body(qx_ref, kx_ref, vt_ref, bias_ref, g_ref, o_ref, kpad, vtpad, s_scr, p_scr, ot_scr, *, s_len):
    step = pl.program_id(1)

    @pl.when(step == 0)
    def _():
        lane = lax.broadcasted_iota(jnp.int32, (LEFT, 2 * LANES), 1)
        flags = (lane == _extra_lane(0)) | (lane == LANES + _extra_lane(1))
        pad_pair = jnp.where(flags, NEG_INF, 0.0).astype(BF16)
        for hp in range(N_PAIRS):
            kpad[0:LEFT, 2 * hp * LANES:2 * (hp + 1) * LANES] = pad_pair
        vtpad[:, 0:LEFT] = jnp.zeros((W_GROUP, LEFT), BF16)
        kpad[LEFT:LEFT + s_len, :] = kx_ref[...]
        vtpad[:, LEFT:LEFT + s_len] = vt_ref[...]

    even_rows = _pair_rows()
    start = pl.multiple_of(step * BAND_Q, BAND_Q)
    def logits(h):
        head = slice(h * LANES, (h + 1) * LANES)
        s_scr[h] = bias_ref[h] + lax.dot_general(kpad[pl.ds(start, BAND_K), head], qx_ref[:, head], _NT,
                                                 preferred_element_type=F32)

    def weigh(h):
        st = s_scr[h]
        p_scr[h] = jnp.exp2(st - jnp.max(st, axis=0, keepdims=True)).astype(BF16)

    def values(hp):
        pair = slice(hp * LANES, (hp + 1) * LANES)
        vtwin = vtpad[pair, pl.ds(start, BAND_K)]
        accs = [jnp.dot(_head_vt(vtwin, even_rows, parity), p_scr[2 * hp + parity], preferred_element_type=F32)
                for parity in range(2)]
        ot_scr[pair, :] = _head_out_t(accs[0], accs[1], even_rows)

    lead = N_HEADS // 2
    for h in range(lead):
        logits(h)
    for h in range(N_HEADS):
        weigh(h)
        if h + lead < N_HEADS:
            logits(h + lead)
        if h % 2 == 1:
            values(h // 2)
    o_ref[...] = _rms(ot_scr[...].T, g_ref[...]).astype(BF16)


def _band_prompt(qxb, kxb, vbt, bias_t, g_out):
    b, s, wx = qxb.shape
    w = vbt.shape[1]
    return pl.pallas_call(
        functools.partial(_band_body, s_len=s),
        grid=(b, s // BAND_Q),
        in_specs=[pl.BlockSpec((None, BAND_Q, wx), lambda i, j: (i, j, 0)),
                  pl.BlockSpec((None, s, wx), lambda i, j: (i, 0, 0)),
                  pl.BlockSpec((None, w, s), lambda i, j: (i, 0, 0)),
                  pl.BlockSpec(bias_t.shape, lambda i, j: (0, 0, 0)),
                  pl.BlockSpec((1, w), lambda i, j: (0, 0))],
        out_specs=pl.BlockSpec((None, BAND_Q, w), lambda i, j: (i, j, 0)),
        out_shape=jax.ShapeDtypeStruct((b, s, w), BF16),
        scratch_shapes=[pltpu.VMEM((LEFT + s, wx), BF16), pltpu.VMEM((w, LEFT + s), BF16),
                        pltpu.VMEM((N_HEADS, BAND_K, BAND_Q), F32), pltpu.VMEM((N_HEADS, BAND_K, BAND_Q), BF16),
                        pltpu.VMEM((w, BAND_Q), F32)],
        compiler_params=pltpu.CompilerParams(
            dimension_semantics=("parallel", "arbitrary"), vmem_limit_bytes=VMEM_LIMIT),
        name="band_prompt",
    )(qxb, kxb, vbt, bias_t, g_out)


def _row_to_col(row):
    n = row.shape[-1]
    r = lax.broadcasted_iota(jnp.int32, (n, n), 0)
    c = lax.broadcasted_iota(jnp.int32, (n, n), 1)
    return jnp.sum(jnp.where(r == c, jnp.broadcast_to(row, (n, n)), 0.0), axis=-1, keepdims=True)


def _fox_sample_body(q_ref, kn_ref, vn_ref, lft_ref, kc_ref, vc_ref, clft_ref, g_ref, o_ref,
                     cct_scr, cn_scr, m_scr, acc_scr, o_scr, s_scr, p_scr, alpha_scr, *, t_new, pt):
    p_idx = pl.program_id(1)
    n_p = pl.num_programs(1)
    even = _pair_masks()

    @pl.when(p_idx == 0)
    def _():
        cct = _lane_cumsum(clft_ref[...])
        cct_scr[...] = cct
        cn_scr[...] = _lane_cumsum(lft_ref[...]) + cct[:, cct.shape[1] - 1:]
        m_scr[...] = jnp.full(m_scr.shape, NEG_INF, F32)
        acc_scr[...] = jnp.zeros(acc_scr.shape, F32)

    start = pl.multiple_of(p_idx * pt, pt)
    pair = lambda h: slice((h // 2) * LANES, (h // 2 + 1) * LANES)
    for h in range(N_HEADS):
        s_scr[h] = lax.dot_general(_head_q(q_ref[:, pair(h)], even, h % 2), kc_ref[:, pair(h)].astype(BF16),
                                   _NT, preferred_element_type=F32)
    for h in range(N_HEADS):
        cq = _row_to_col(cn_scr[h:h + 1, :])
        s = s_scr[h] + cq - cct_scr[h:h + 1, pl.ds(start, pt)]
        m = m_scr[h]
        m_new = jnp.maximum(m, jnp.max(s, axis=-1, keepdims=True))
        p_scr[h] = jnp.exp(s - m_new).astype(BF16)
        alpha_scr[h] = jnp.exp(m - m_new)
        m_scr[h] = m_new
    for h in range(N_HEADS):
        v = _head_v(vc_ref[:, pair(h)].astype(BF16), even, h % 2)
        acc_scr[h] = acc_scr[h] * alpha_scr[h] + jnp.dot(p_scr[h], v, preferred_element_type=F32)

    @pl.when(p_idx == n_p - 1)
    def _():
        row = lax.broadcasted_iota(jnp.int32, (t_new, t_new), 0)
        col = lax.broadcasted_iota(jnp.int32, (t_new, t_new), 1)
        logits = []
        for h in range(N_HEADS):
            cn_row = cn_scr[h:h + 1, :]
            s = lax.dot_general(_head_q(q_ref[:, pair(h)], even, h % 2), kn_ref[:, pair(h)], _NT,
                                preferred_element_type=F32)
            logits.append(jnp.where(col <= row, s + _row_to_col(cn_row) - cn_row, NEG_INF))
        weights, scales = [], []
        for h in range(N_HEADS):
            m = m_scr[h]
            m_new = jnp.maximum(m, jnp.max(logits[h], axis=-1, keepdims=True))
            weights.append(jnp.exp(logits[h] - m_new).astype(BF16))
            scales.append(jnp.exp(m - m_new))
        accs = [acc_scr[h] * scales[h] + jnp.dot(weights[h], _head_v(vn_ref[:, pair(h)], even, h % 2),
                                                 preferred_element_type=F32) for h in range(N_HEADS)]
        for hp in range(N_PAIRS):
            o_scr[:, hp * LANES:(hp + 1) * LANES] = _head_out(accs[2 * hp], accs[2 * hp + 1], even)
        o_ref[...] = _rms(o_scr[...], g_ref[...]).astype(BF16)


def _fox_sample(q, kn, vn, lft, kc, vc, clft, g_out, *, pt):
    b, t, w = q.shape
    p_len = kc.shape[1]
    new = lambda: pl.BlockSpec((None, t, w), lambda i, j: (i, 0, 0))
    cache = lambda: pl.BlockSpec((None, pt, w), lambda i, j: (i, j, 0))
    return pl.pallas_call(
        functools.partial(_fox_sample_body, t_new=t, pt=pt),
        grid=(b, p_len // pt),
        in_specs=[new(), new(), new(),
                  pl.BlockSpec((None, N_HEADS, t), lambda i, j: (i, 0, 0)),
                  cache(), cache(),
                  pl.BlockSpec((None, N_HEADS, p_len), lambda i, j: (i, 0, 0)),
                  pl.BlockSpec((1, w), lambda i, j: (0, 0))],
        out_specs=new(),
        out_shape=jax.ShapeDtypeStruct((b, t, w), BF16),
        scratch_shapes=[pltpu.VMEM((N_HEADS, p_len), F32), pltpu.VMEM((N_HEADS, t), F32),
                        pltpu.VMEM((N_HEADS, t, 1), F32), pltpu.VMEM((N_HEADS, t, LANES), F32),
                        pltpu.VMEM((t, w), F32), pltpu.VMEM((N_HEADS, t, pt), F32),
                        pltpu.VMEM((N_HEADS, t, pt), BF16), pltpu.VMEM((N_HEADS, t, 1), F32)],
        compiler_params=pltpu.CompilerParams(
            dimension_semantics=("parallel", "arbitrary"), vmem_limit_bytes=VMEM_LIMIT),
        name="fox_sample",
    )(q, kn, vn, lft, kc, vc, clft, g_out)


def _band_sample_body(q_ref, kn_ref, vn_ref, knf_ref, vnf_ref, kc_ref, vc_ref, bias_ref, g_ref,
                      o_ref, nk_ref, nv_ref, kcat, vcat, o_scr, *, t_new, bp):
    kcat[0:bp, :] = kc_ref[...].astype(BF16)
    vcat[0:bp, :] = vc_ref[...].astype(BF16)
    kcat[bp:bp + t_new, :] = kn_ref[...]
    vcat[bp:bp + t_new, :] = vn_ref[...]
    nk_ref[0:bp - t_new, :] = kc_ref[t_new:bp, :]
    nv_ref[0:bp - t_new, :] = vc_ref[t_new:bp, :]
    nk_ref[bp - t_new:bp, :] = knf_ref[...]
    nv_ref[bp - t_new:bp, :] = vnf_ref[...]
    even = _pair_masks()
    for hp in range(N_PAIRS):
        lanes = slice(hp * LANES, (hp + 1) * LANES)
        q128 = q_ref[:, lanes]
        k = kcat[:, lanes]
        v = vcat[:, lanes]
        accs = []
        for parity in range(2):
            h = 2 * hp + parity
            s = lax.dot_general(_head_q(q128, even, parity), k, _NT, preferred_element_type=F32)
            s = s + bias_ref[h, 0:t_new, 0:bp + t_new]
            p = jnp.exp(s - jnp.max(s, axis=-1, keepdims=True)).astype(BF16)
            accs.append(jnp.dot(p, _head_v(v, even, parity), preferred_element_type=F32))
        o_scr[:, lanes] = _head_out(accs[0], accs[1], even)
    o_ref[...] = _rms(o_scr[...], g_ref[...]).astype(BF16)


def _band_sample(q, kn, vn, knf, vnf, kc, vc, bias, g_out):
    b, t, w = q.shape
    bp = kc.shape[1]
    assert t == CHUNK and bp == LEFT
    new = lambda: pl.BlockSpec((None, t, w), lambda i: (i, 0, 0))
    buf = lambda: pl.BlockSpec((None, bp, w), lambda i: (i, 0, 0))
    return pl.pallas_call(
        functools.partial(_band_sample_body, t_new=t, bp=bp),
        grid=(b,),
        in_specs=[new(), new(), new(), new(), new(), buf(), buf(),
                  pl.BlockSpec(bias.shape, lambda i: (0, 0, 0)),
                  pl.BlockSpec((1, w), lambda i: (0, 0))],
        out_specs=[new(), buf(), buf()],
        out_shape=[jax.ShapeDtypeStruct((b, t, w), BF16), jax.ShapeDtypeStruct((b, bp, w), F32),
                   jax.ShapeDtypeStruct((b, bp, w), F32)],
        scratch_shapes=[pltpu.VMEM((bp + t, w), BF16), pltpu.VMEM((bp + t, w), BF16), pltpu.VMEM((t, w), F32)],
        compiler_params=pltpu.CompilerParams(dimension_semantics=("parallel",), vmem_limit_bytes=VMEM_LIMIT),
        name="band_sample",
    )(q, kn, vn, knf, vnf, kc, vc, bias, g_out)


N_HEADS_MEM = 4
HEAD_DIM_MEM = 128
W_MEM = N_HEADS_MEM * HEAD_DIM_MEM
MEM_SCALE = HEAD_DIM_MEM ** -0.5


def _memkv_body(m_ref, g_ref, w_ref, kf_ref, vf_ref, k_ref, v_ref):
    h = _rms(m_ref[...], g_ref[...]).astype(BF16)
    z = jnp.dot(h, w_ref[...], preferred_element_type=F32)
    kf_ref[...] = z[:, :W_MEM]
    vf_ref[...] = z[:, W_MEM:]
    k_ref[...] = z[:, :W_MEM].astype(BF16)
    v_ref[...] = z[:, W_MEM:].astype(BF16)


def _mem_kv(mem, g_mem, w_ckv):
    b, n, d = mem.shape
    blk = lambda: pl.BlockSpec((None, n, W_MEM), lambda i: (i, 0, 0))
    return pl.pallas_call(
        _memkv_body,
        grid=(b,),
        in_specs=[pl.BlockSpec((None, n, d), lambda i: (i, 0, 0)),
                  pl.BlockSpec((1, d), lambda i: (0, 0)),
                  pl.BlockSpec(w_ckv.shape, lambda i: (0, 0))],
        out_specs=[blk(), blk(), blk(), blk()],
        out_shape=[jax.ShapeDtypeStruct((b, n, W_MEM), F32)] * 2 + [jax.ShapeDtypeStruct((b, n, W_MEM), BF16)] * 2,
        compiler_params=pltpu.CompilerParams(dimension_semantics=("parallel",), vmem_limit_bytes=VMEM_LIMIT),
        name="mem_kv",
    )(mem, g_mem, w_ckv)


N_GROUPS = 4
EXPERTS_PER_GROUP = 8
N_EXPERTS = N_GROUPS * EXPERTS_PER_GROUP
ROUTE_L2 = N_GROUPS
ROUTE_ROWS = 8
POST_CHAIN = 512
R_EID0, R_EID1, R_RANK0, R_RANK1, R_GATE0, R_GATE1 = range(6)


ROUTE_LOGIT_ROWS = 40


def _row_max(x, mask):
    return jnp.max(jnp.where(mask, x, -jnp.inf), axis=0, keepdims=True)


def _first_row(mask, row):
    return jnp.min(jnp.where(mask, row, LANES), axis=0, keepdims=True)


def _route(logits, row):
    is_l1 = row < N_GROUPS
    m1 = _row_max(logits, is_l1)
    grp = _first_row(is_l1 & (logits == m1), row)
    wg = 1.0 / jnp.sum(jnp.where(is_l1, jnp.exp(logits - m1), 0.0), axis=0, keepdims=True)
    lo = ROUTE_L2 + grp * EXPERTS_PER_GROUP
    in_grp = (row >= lo) & (row < lo + EXPERTS_PER_GROUP)
    v0 = _row_max(logits, in_grp)
    i0 = _first_row(in_grp & (logits == v0), row)
    rest = in_grp & (row != i0)
    v1 = _row_max(logits, rest)
    i1 = _first_row(rest & (logits == v1), row)
    e1 = jnp.exp(v1 - v0)
    den = 1.0 / (1.0 + e1)
    return i0, i1, wg * den, wg * e1 * den


def _post_body(x_ref, a_ref, b_ref, mk_ref, mv_ref, woa_ref, wob_ref, gc_ref, wcq_ref, wco_ref,
               gf_ref, wrt_ref, brt_ref,
               x2_ref, h3_ref, route_ref, routet_ref, cnt_ref, o_scr, *, tm, nsub):
    seq = tm // nsub
    count = jnp.zeros((1, LANES), F32)
    for rows in _chains(tm, POST_CHAIN):
        n = rows.size
        x1 = (x_ref[rows, :] + jnp.dot(a_ref[rows, :], woa_ref[...], preferred_element_type=F32)
              + jnp.dot(b_ref[rows, :], wob_ref[...], preferred_element_type=F32))
        h2 = _rms(x1, gc_ref[...]).astype(BF16)
        qc = (jnp.dot(h2, wcq_ref[...], preferred_element_type=F32) * MEM_SCALE).astype(BF16)
        span = min(seq, n)
        for part in range(n // span):
            sub = (rows.start + part * span) // seq
            rs = slice(part * span, (part + 1) * span)
            orow = pl.ds(rows.start + part * span, span)
            for hm in range(N_HEADS_MEM):
                lanes = slice(hm * HEAD_DIM_MEM, (hm + 1) * HEAD_DIM_MEM)
                s = lax.dot_general(qc[rs, lanes], mk_ref[sub, :, lanes], _NT, preferred_element_type=F32)
                p = jnp.exp(s - jnp.max(s, axis=-1, keepdims=True))
                inv = 1.0 / jnp.sum(p, axis=-1, keepdims=True)
                o_scr[orow, lanes] = jnp.dot(p.astype(BF16), mv_ref[sub, :, lanes], preferred_element_type=F32) * inv
        x2 = x1 + jnp.dot(o_scr[rows, :].astype(BF16), wco_ref[...], preferred_element_type=F32)
        x2_ref[rows, :] = x2
        h3 = _rms(x2, gf_ref[...]).astype(BF16)
        h3_ref[rows, :] = h3

        logits = lax.dot_general(wrt_ref[...], h3, _NT, preferred_element_type=F32) + brt_ref[...]
        row = lax.broadcasted_iota(jnp.int32, (ROUTE_LOGIT_ROWS, n), 0)
        i0, i1, g0, g1 = _route(logits[:ROUTE_LOGIT_ROWS, :], row)
        e0 = i0 - ROUTE_L2
        e1 = i1 - ROUTE_L2
        expert = lax.broadcasted_iota(jnp.int32, (LANES, n), 0)
        hit0 = expert == e0
        hit1 = expert == e1
        onehot = jnp.where(hit0, 1.0, jnp.where(hit1, 1.0, 0.0)).astype(BF16)
        earlier = lax.broadcasted_iota(jnp.int32, (n, n), 0) < lax.broadcasted_iota(jnp.int32, (n, n), 1)
        before = jnp.where(earlier, 1.0, 0.0).astype(BF16)
        seen = jnp.dot(onehot, before, preferred_element_type=F32) + _row_to_col(count)
        rank0 = jnp.sum(jnp.where(hit0, seen, 0.0), axis=0, keepdims=True)
        rank1 = jnp.sum(jnp.where(hit1, seen, 0.0), axis=0, keepdims=True)
        count = count + lax.dot_general(jnp.ones((8, n), BF16), onehot, _NT, preferred_element_type=F32)[0:1, :]

        rec = jnp.concatenate([e0.astype(F32), e1.astype(F32), rank0, rank1, g0, g1, jnp.zeros((2, n), F32)], axis=0)
        routet_ref[:, rows] = rec
        route_ref[rows, :] = jnp.concatenate([rec, jnp.zeros((LANES - ROUTE_ROWS, n), F32)], axis=0).T[:, :ROUTE_ROWS]
    cnt_ref[...] = count


def _post_block(x, a_n, b_n, mk, mv, weights, *, tm):
    b, s, d = x.shape
    if s >= tm:
        nsub, grid = 1, (b, s // tm)
        tok = lambda i, j: (i, j, 0)
        flat = lambda i, j: i * (s // tm) + j
    else:
        nsub = tm // s
        assert b % nsub == 0
        x, a_n, b_n = (t.reshape(b // nsub, tm, t.shape[-1]) for t in (x, a_n, b_n))
        grid = (b // nsub, 1)
        tok = lambda i, j: (i, 0, 0)
        flat = lambda i, j: i
    mem = lambda i, j: (i, 0, 0)
    const = lambda arr: pl.BlockSpec(arr.shape, lambda i, j: (0,) * arr.ndim)
    in_specs = [pl.BlockSpec((None, tm, d), tok),
                pl.BlockSpec((None, tm, W_GROUP), tok), pl.BlockSpec((None, tm, W_GROUP), tok),
                pl.BlockSpec((nsub, mk.shape[1], W_MEM), mem), pl.BlockSpec((nsub, mv.shape[1], W_MEM), mem)]
    in_specs += [const(w) for w in weights]
    n = b * s
    out_shape = [jax.ShapeDtypeStruct((n, d), F32), jax.ShapeDtypeStruct((n, d), BF16),
                 jax.ShapeDtypeStruct((n, ROUTE_ROWS), F32), jax.ShapeDtypeStruct((ROUTE_ROWS, n), F32),
                 jax.ShapeDtypeStruct((n // tm, 1, LANES), F32)]
    out_specs = [pl.BlockSpec((tm, d), lambda i, j: (flat(i, j), 0)),
                 pl.BlockSpec((tm, d), lambda i, j: (flat(i, j), 0)),
                 pl.BlockSpec((tm, ROUTE_ROWS), lambda i, j: (flat(i, j), 0)),
                 pl.BlockSpec((ROUTE_ROWS, tm), lambda i, j: (0, flat(i, j))),
                 pl.BlockSpec((None, 1, LANES), lambda i, j: (flat(i, j), 0, 0))]
    return pl.pallas_call(
        functools.partial(_post_body, tm=tm, nsub=nsub),
        grid=grid,
        in_specs=in_specs,
        out_specs=out_specs,
        out_shape=out_shape,
        scratch_shapes=[pltpu.VMEM((tm, W_MEM), F32)],
        compiler_params=pltpu.CompilerParams(
            dimension_semantics=("parallel", "parallel"), vmem_limit_bytes=VMEM_LIMIT),
        name="post_block",
    )(x, a_n, b_n, mk, mv, *weights)


def _prep_post(w_out, g_cross, w_cq, w_co, g_ffn, w_r1, b_r1, w_r2, b_r2):
    pad = LANES - N_GROUPS - N_EXPERTS
    w_rt = jnp.pad(jnp.concatenate([w_r1, w_r2], axis=1), ((0, 0), (0, pad))).astype(BF16).T
    b_rt = jnp.pad(jnp.concatenate([b_r1, b_r2]).reshape(-1, 1), ((0, pad), (0, 0))).astype(F32)
    return [w_out[:W_GROUP].astype(BF16), w_out[W_GROUP:].astype(BF16), g_cross.reshape(1, -1),
            w_cq.astype(BF16), w_co.astype(BF16), g_ffn.reshape(1, -1), w_rt, b_rt]


D_EXPERT = 512
TOP_K = 2
ROW_TILE = 512
MXU_DIM = 256
RUN_ALIGN = 16
PLAN_TILES = LANES
TILE_TABLE = 2 * LANES


def _local_rows(tm):
    return -(-(TOP_K * tm + N_EXPERTS * (RUN_ALIGN - 1)) // MXU_DIM) * MXU_DIM


def _n_row_tiles(n_tokens, tm):
    rows = n_tokens * TOP_K + (n_tokens // tm) * N_EXPERTS * (RUN_ALIGN - 1) + N_EXPERTS * (ROW_TILE - 1)
    return rows // ROW_TILE


N_CHUNK_LANE = LANES - 1


def _plan_body(cnt_ref, lstart_ref, chunk_ref, offs_ref, te_ref):
    cnt = cnt_ref[...].astype(jnp.int32)
    n16 = ((cnt + (RUN_ALIGN - 1)) & ~(RUN_ALIGN - 1)).astype(F32)
    lend = _lane_cumsum(n16)
    lstart = lend - n16
    earlier = (_lane_cumsum(n16.T) - n16.T).T
    total = jnp.sum(n16, axis=0, keepdims=True).astype(jnp.int32)
    seg = jnp.broadcast_to((total + (ROW_TILE - 1)) & ~(ROW_TILE - 1), (8, LANES)).astype(F32)
    ends = _lane_cumsum(seg)
    offs = ends - seg
    shift = earlier + offs[0:1, :] - lstart
    lstart_ref[...] = lstart.astype(jnp.int32)
    offs_ref[...] = offs[0:1, :].astype(jnp.int32)

    lane = lax.broadcasted_iota(jnp.int32, (PLAN_TILES, LANES), 1)
    local_row = (lane * RUN_ALIGN).astype(F32)
    owner = jnp.zeros((PLAN_TILES, LANES), jnp.int32)
    for e in range(N_EXPERTS):
        owner = owner + jnp.where(lend[:, e:e + 1] <= local_row, 1, 0)
    glob = local_row
    for e in range(N_EXPERTS):
        glob = glob + jnp.where(owner == e, shift[:, e:e + 1], 0.0)
    n_chunks = lend[:, N_EXPERTS - 1:N_EXPERTS] * (1.0 / RUN_ALIGN)
    chunk_ref[...] = jnp.where(lane == N_CHUNK_LANE, n_chunks, glob).astype(jnp.int32)

    tile_start = (lax.broadcasted_iota(jnp.int32, te_ref.shape, 1) * ROW_TILE).astype(F32)
    te = jnp.zeros(te_ref.shape, jnp.int32)
    for e in range(N_EXPERTS):
        end_e = jnp.sum(jnp.where(lane[0:1, :] == e, ends[0:1, :], 0.0), axis=-1, keepdims=True)
        te = te + jnp.where(end_e <= tile_start, 1, 0)
    te_ref[...] = jnp.minimum(te, N_EXPERTS - 1)


def _plan(counts, tm):
    nt = counts.shape[0]
    assert nt <= PLAN_TILES and _local_rows(tm) // RUN_ALIGN <= N_CHUNK_LANE
    cnt = jnp.pad(counts.reshape(nt, LANES), ((0, PLAN_TILES - nt), (0, 0)))
    grid_i32 = jax.ShapeDtypeStruct((PLAN_TILES, LANES), jnp.int32)
    lstart, chunks, offs, te = pl.pallas_call(
        _plan_body,
        out_shape=[grid_i32, grid_i32, jax.ShapeDtypeStruct((1, LANES), jnp.int32),
                   jax.ShapeDtypeStruct((1, TILE_TABLE), jnp.int32)],
        name="moe_plan",
    )(cnt)
    per_tile = lambda t: t[:nt].reshape(nt, 1, LANES)
    return per_tile(lstart), per_tile(chunks), offs.reshape(LANES), te.reshape(TILE_TABLE)


def _for_each_chunk(chunk_ref, fn):
    n = chunk_ref[0, N_CHUNK_LANE]

    def visit(c, priority):
        fn(pl.multiple_of(c * RUN_ALIGN, RUN_ALIGN), pl.multiple_of(chunk_ref[0, c], RUN_ALIGN), priority)

    def body(c, carry):
        visit(2 * c, 0)
        visit(2 * c + 1, 1)
        return carry

    lax.fori_loop(0, n // 2, body, None)

    @pl.when(lax.rem(n, 2) == 1)
    def _():
        visit(n - 1, 0)


def _local_positions_row(rt_ref, lstart_ref):
    pos = []
    for r_eid, r_rank in ((R_EID0, R_RANK0), (R_EID1, R_RANK1)):
        eid = rt_ref[r_eid:r_eid + 1, :].astype(jnp.int32)
        p = rt_ref[r_rank:r_rank + 1, :].astype(jnp.int32)
        for e in range(N_EXPERTS):
            p = p + jnp.where(eid == e, lstart_ref[0, e], 0)
        pos.append(p)
    return pos


def _dispatch_body(offs_ref, lstart_ref, chunk_ref, pchunk_ref,
                   hp_ref, hs_ref, rt_ref, xs_ref, loc, zeros, sems, zsem, *, tm, n_prompt_tiles):
    i = pl.program_id(0)
    n_tiles = xs_ref.shape[0] // ROW_TILE
    half = lax.rem(i, 2)

    @pl.when(i == 0)
    def _():
        zeros[...] = jnp.zeros(zeros.shape, zeros.dtype)
        zero_tile = lambda row: pltpu.make_async_copy(
            zeros, xs_ref.at[pl.ds(pl.multiple_of(row, ROW_TILE), ROW_TILE)], zsem)
        n_used = offs_ref[N_EXPERTS] // ROW_TILE

        def tail(j, carry, op):
            op(zero_tile(j * ROW_TILE))
            return carry

        for op in (lambda c: c.start(), lambda c: c.wait()):
            for e in range(N_EXPERTS):
                @pl.when(offs_ref[e + 1] > offs_ref[e])
                def _():
                    op(zero_tile(offs_ref[e + 1] - ROW_TILE))
            lax.fori_loop(n_used, n_tiles, functools.partial(tail, op=op), None)

    pos0, pos1 = _local_positions_row(rt_ref, lstart_ref)
    used = chunk_ref[0, N_CHUNK_LANE] * RUN_ALIGN
    body_rows = loc.shape[1] - MXU_DIM

    def sort_rows(h_ref, lo, n):
        slot = lo + lax.broadcasted_iota(jnp.int32, (n, tm), 0)
        perm = jnp.where(slot == pos0, 1.0, jnp.where(slot == pos1, 1.0, 0.0)).astype(BF16)
        loc[half, lo:lo + n, :] = jnp.dot(perm, h_ref[...], preferred_element_type=F32).astype(BF16)

    for h_ref, mine in ((hp_ref, i < n_prompt_tiles), (hs_ref, i >= n_prompt_tiles)):
        @pl.when(mine)
        def _():
            sort_rows(h_ref, 0, body_rows)

        @pl.when(mine & (used > body_rows))
        def _():
            sort_rows(h_ref, body_rows, MXU_DIM)

    def chunk(buf, lo, go):
        return pltpu.make_async_copy(loc.at[buf, pl.ds(lo, RUN_ALIGN)], xs_ref.at[pl.ds(go, RUN_ALIGN)],
                                     sems.at[buf])

    @pl.when(i > 0)
    def _():
        _for_each_chunk(pchunk_ref, lambda lo, go, pr: chunk(1 - half, lo, go).wait())

    _for_each_chunk(chunk_ref, lambda lo, go, pr: chunk(half, lo, go).start(priority=pr))

    @pl.when(i == pl.num_programs(0) - 1)
    def _():
        _for_each_chunk(chunk_ref, lambda lo, go, pr: chunk(half, lo, go).wait())


def _dispatch(h3_prompt, h3_sample, route_t, plan, *, tm):
    lstart, chunks, offs, _ = plan
    n_p, n_s, d = h3_prompt.shape[0], h3_sample.shape[0], h3_prompt.shape[-1]
    assert n_p % tm == 0 and n_s % tm == 0
    n = n_p + n_s
    nt = n // tm
    npt = n_p // tm
    n_rows = _n_row_tiles(n, tm) * ROW_TILE
    smem_tile = lambda: pl.BlockSpec((None, 1, LANES), lambda i, offs: (i, 0, 0), memory_space=pltpu.SMEM)
    smem_prev = lambda: pl.BlockSpec((None, 1, LANES), lambda i, offs: (jnp.maximum(i - 1, 0), 0, 0),
                                     memory_space=pltpu.SMEM)
    return pl.pallas_call(
        functools.partial(_dispatch_body, tm=tm, n_prompt_tiles=npt),
        grid_spec=pltpu.PrefetchScalarGridSpec(
            num_scalar_prefetch=1,
            grid=(nt,),
            in_specs=[smem_tile(), smem_tile(), smem_prev(),
                      pl.BlockSpec((tm, d), lambda i, offs: (jnp.minimum(i, npt - 1), 0)),
                      pl.BlockSpec((tm, d), lambda i, offs: (jnp.maximum(i - npt, 0), 0)),
                      pl.BlockSpec((ROUTE_ROWS, tm), lambda i, offs: (0, i))],
            out_specs=pl.BlockSpec(memory_space=pl.ANY),
            scratch_shapes=[pltpu.VMEM((2, _local_rows(tm), d), BF16), pltpu.VMEM((ROW_TILE, d), BF16),
                            pltpu.SemaphoreType.DMA((2,)), pltpu.SemaphoreType.DMA(())]),
        out_shape=jax.ShapeDtypeStruct((n_rows, d), BF16),
        compiler_params=pltpu.CompilerParams(dimension_semantics=("arbitrary",), vmem_limit_bytes=VMEM_LIMIT),
        name="moe_dispatch",
    )(offs, lstart, chunks, chunks, h3_prompt, h3_sample, route_t)


def _experts_body(te_ref, offs_ref, xs_ref, wg_ref, wu_ref, wd_ref, ys_ref,
                  wg_buf, wu_buf, wd_buf, wgu_bf, wd_bf, turn_ref, sems):
    i = pl.program_id(0)
    n_used = offs_ref[N_EXPERTS] // ROW_TILE

    def fetch(expert, half):
        return [pltpu.make_async_copy(src.at[expert], dst.at[half], sems.at[half, k])
                for k, (src, dst) in enumerate(((wg_ref, wg_buf), (wu_ref, wu_buf), (wd_ref, wd_buf)))]

    @pl.when(i == 0)
    def _():
        turn_ref[0] = 0
        for copy in fetch(te_ref[0], 0):
            copy.start()

    @pl.when(i < n_used)
    def _():
        expert = te_ref[i]

        @pl.when((i == 0) | (expert != te_ref[jnp.maximum(i - 1, 0)]))
        def _():
            half = lax.rem(turn_ref[0], 2)
            turn_ref[0] = turn_ref[0] + 1
            for copy in fetch(expert, half):
                copy.wait()
            wgu_bf[:, 0:D_EXPERT] = wg_buf[half].astype(BF16)
            wgu_bf[:, D_EXPERT:] = wu_buf[half].astype(BF16)
            wd_bf[...] = wd_buf[half].astype(BF16)
            following = offs_ref[expert + 1] // ROW_TILE

            @pl.when(following < n_used)
            def _():
                for copy in fetch(te_ref[following], 1 - half):
                    copy.start()

        dot = functools.partial(jnp.dot, preferred_element_type=F32)
        for rows in _chains(ROW_TILE, ROW_TILE):
            gate_up = dot(xs_ref[rows, :], wgu_bf[...])
            gate, up = gate_up[:, :D_EXPERT], gate_up[:, D_EXPERT:]
            act = (gate * jax.nn.sigmoid(gate) * up).astype(BF16)
            ys_ref[rows, :] = dot(act, wd_bf[...]).astype(BF16)

    @pl.when(i >= n_used)
    def _():
        ys_ref[...] = jnp.zeros(ys_ref.shape, ys_ref.dtype)


def _experts(xs, te, offs, w_gate, w_up, w_down):
    n_rows, d = xs.shape
    last = lambda i, te, offs: jnp.minimum(i, offs[N_EXPERTS] // ROW_TILE - 1)
    hbm = pl.BlockSpec(memory_space=pl.ANY)
    return pl.pallas_call(
        _experts_body,
        grid_spec=pltpu.PrefetchScalarGridSpec(
            num_scalar_prefetch=2,
            grid=(n_rows // ROW_TILE,),
            in_specs=[pl.BlockSpec((ROW_TILE, d), lambda i, te, offs: (last(i, te, offs), 0)), hbm, hbm, hbm],
            out_specs=pl.BlockSpec((ROW_TILE, d), lambda i, te, offs: (i, 0)),
            scratch_shapes=[pltpu.VMEM((2, d, D_EXPERT), F32), pltpu.VMEM((2, d, D_EXPERT), F32),
                            pltpu.VMEM((2, D_EXPERT, d), F32),
                            pltpu.VMEM((d, 2 * D_EXPERT), BF16), pltpu.VMEM((D_EXPERT, d), BF16),
                            pltpu.SMEM((1,), jnp.int32), pltpu.SemaphoreType.DMA((2, 3))]),
        out_shape=jax.ShapeDtypeStruct((n_rows, d), BF16),
        compiler_params=pltpu.CompilerParams(dimension_semantics=("arbitrary",), vmem_limit_bytes=VMEM_LIMIT),
        name="moe_experts",
    )(te, offs, xs, w_gate, w_up, w_down)


def _combine_body(chunk_ref, nchunk_ref, x2p_ref, x2s_ref, route_ref, lsv_ref, ys_ref, g_ref,
                  yp_ref, ysm_ref, loc, sems, *, tm, n_prompt_tiles):
    i = pl.program_id(0)
    half = lax.rem(i, 2)

    def chunk(buf, lo, go):
        return pltpu.make_async_copy(ys_ref.at[pl.ds(go, RUN_ALIGN)], loc.at[buf, pl.ds(lo, RUN_ALIGN)],
                                     sems.at[buf])

    @pl.when(i == 0)
    def _():
        loc[...] = jnp.zeros(loc.shape, loc.dtype)
        _for_each_chunk(chunk_ref, lambda lo, go, pr: chunk(0, lo, go).start(priority=pr))

    @pl.when(i + 1 < pl.num_programs(0))
    def _():
        _for_each_chunk(nchunk_ref, lambda lo, go, pr: chunk(1 - half, lo, go).start(priority=pr))

    _for_each_chunk(chunk_ref, lambda lo, go, pr: chunk(half, lo, go).wait())

    lane = lax.broadcasted_iota(jnp.int32, (tm, LANES), 1)
    picks = []
    for r_eid, r_rank, r_gate in ((R_EID0, R_RANK0, R_GATE0), (R_EID1, R_RANK1, R_GATE1)):
        eid = route_ref[:, r_eid:r_eid + 1].astype(jnp.int32)
        start = jnp.sum(jnp.where(lane == eid, lsv_ref[...], 0), axis=-1, keepdims=True)
        picks.append((route_ref[:, r_rank:r_rank + 1].astype(jnp.int32) + start, route_ref[:, r_gate:r_gate + 1]))

    slot = lax.broadcasted_iota(jnp.int32, (tm, loc.shape[1]), 1)
    weights = jnp.zeros(slot.shape, F32)
    for pos, gate in picks:
        weights = jnp.where(slot == pos, gate, weights)
    moe = jnp.dot(weights.astype(BF16), loc[half], preferred_element_type=F32)

    @pl.when(i < n_prompt_tiles)
    def _():
        yp_ref[...] = _rms(x2p_ref[...] + moe, g_ref[...])

    @pl.when(i >= n_prompt_tiles)
    def _():
        ysm_ref[...] = _rms(x2s_ref[...] + moe, g_ref[...])


def _combine(x2_prompt, x2_sample, route, plan, ys, g_final, *, tm):
    lstart, chunks, _, _ = plan
    (n_p, d), n_s = x2_prompt.shape, x2_sample.shape[0]
    assert n_p % tm == 0 and n_s % tm == 0
    npt = n_p // tm
    nt = npt + n_s // tm
    smem_tile = lambda: pl.BlockSpec((None, 1, LANES), lambda i: (i, 0, 0), memory_space=pltpu.SMEM)
    smem_next = lambda: pl.BlockSpec((None, 1, LANES), lambda i: (jnp.minimum(i + 1, nt - 1), 0, 0),
                                     memory_space=pltpu.SMEM)
    prompt_tile = lambda: pl.BlockSpec((tm, d), lambda i: (jnp.minimum(i, npt - 1), 0))
    sample_tile = lambda: pl.BlockSpec((tm, d), lambda i: (jnp.maximum(i - npt, 0), 0))
    return pl.pallas_call(
        functools.partial(_combine_body, tm=tm, n_prompt_tiles=npt),
        grid=(nt,),
        in_specs=[smem_tile(), smem_next(), prompt_tile(), sample_tile(),
                  pl.BlockSpec((tm, ROUTE_ROWS), lambda i: (i, 0)),
                  pl.BlockSpec((None, 1, LANES), lambda i: (i, 0, 0)),
                  pl.BlockSpec(memory_space=pl.ANY),
                  pl.BlockSpec((1, d), lambda i: (0, 0))],
        out_specs=[prompt_tile(), sample_tile()],
        out_shape=[jax.ShapeDtypeStruct((n_p, d), F32), jax.ShapeDtypeStruct((n_s, d), F32)],
        scratch_shapes=[pltpu.VMEM((2, _local_rows(tm), d), BF16), pltpu.SemaphoreType.DMA((2,))],
        compiler_params=pltpu.CompilerParams(dimension_semantics=("arbitrary",), vmem_limit_bytes=VMEM_LIMIT),
        name="moe_combine",
    )(chunks, chunks, x2_prompt, x2_sample, route, lstart, ys, g_final)


TOKEN_TILE = 512
FOX_Q_TILE = 256
FOX_CACHE_TILE = 2048


def kernel(x_prompt, x_sample, cache_fox_k, cache_fox_v, cache_fox_logf, cache_band_k, cache_band_v, cache_mem_k, cache_mem_v, mem_prompt, g_mix, w_in, b_forget, g_out_fox, g_out_band, rel_table, w_out, g_cross, g_mem, w_cq, w_ck, w_cv, w_co, g_ffn, w_router1, b_router1, w_router2, b_router2, w_exp_gate, w_exp_up, w_exp_down, g_final):
    assert g_mix.shape[0] == 1, "single-layer model"
    bsz, seq, d = x_prompt.shape
    sb, st, _ = x_sample.shape
    n_s = sb * st
    n_mem = mem_prompt.shape[1]
    row = lambda g: g.reshape(1, -1)

    w_pad, bf_pad, g_mix_r = _prep_proj(w_in[0], b_forget[0], g_mix[0])
    g_of, g_ob = row(g_out_fox[0]), row(g_out_band[0])
    bias_t, bias_s = _band_bias(_prep_band_bias_row(rel_table[0]))

    qx, kx, vat, qxb, kxb, vbt, kaf, vaf, kbf, vbf, logf = _proj(
        x_prompt, g_mix_r, w_pad, bf_pad, tm=TOKEN_TILE, prompt=True)
    a_p = _fox_prompt(qx, kx, vat, g_of, tq=FOX_Q_TILE)
    b_p = _band_prompt(qxb, kxb, vbt, bias_t, g_ob)

    s_out = _proj(x_sample.reshape(1, n_s, d), g_mix_r, w_pad, bf_pad, tm=n_s, prompt=False)
    sqa, ska, sva, sqb, skb, svb, skaf, svaf, skbf, svbf = (t.reshape(sb, st, W_GROUP) for t in s_out[:10])
    slogf = s_out[10].reshape(sb, st, N_HEADS)
    slft = s_out[11].reshape(N_HEADS, sb, st).transpose(1, 0, 2)
    past = cache_fox_k.shape[2]
    a_s = _fox_sample(sqa, ska, sva, slft,
                      cache_fox_k[0].reshape(sb, past, W_GROUP), cache_fox_v[0].reshape(sb, past, W_GROUP),
                      cache_fox_logf[0].transpose(0, 2, 1), g_of, pt=FOX_CACHE_TILE)
    bp = cache_band_k.shape[2]
    b_s, nbk, nbv = _band_sample(sqb, skb, svb, skbf, svbf,
                                 cache_band_k[0].reshape(sb, bp, W_GROUP), cache_band_v[0].reshape(sb, bp, W_GROUP),
                                 bias_s, g_ob)

    w_ckv = jnp.concatenate([w_ck[0], w_cv[0]], axis=1).astype(BF16)
    mkf, mvf, mk, mv = _mem_kv(mem_prompt, row(g_mem[0]), w_ckv)
    post_w = _prep_post(w_out[0], g_cross[0], w_cq[0], w_co[0], g_ffn[0],
                        w_router1[0], b_router1[0], w_router2[0], b_router2[0])
    x2_p, h3_p, route_p, routet_p, cnt_p = _post_block(x_prompt, a_p, b_p, mk, mv, post_w, tm=TOKEN_TILE)
    cmk = cache_mem_k[0].reshape(sb, n_mem, W_MEM).astype(BF16)
    cmv = cache_mem_v[0].reshape(sb, n_mem, W_MEM).astype(BF16)
    x2_s, h3_s, route_s, routet_s, cnt_s = _post_block(x_sample, a_s, b_s, cmk, cmv, post_w, tm=TOKEN_TILE)
    route = jnp.concatenate([route_p, route_s], axis=0)
    route_t = jnp.concatenate([routet_p, routet_s], axis=1)

    plan = _plan(jnp.concatenate([cnt_p, cnt_s], axis=0), TOKEN_TILE)
    xs = _dispatch(h3_p, h3_s, route_t, plan, tm=TOKEN_TILE)
    ys = _experts(xs, plan[3], plan[2], w_exp_gate[0], w_exp_up[0], w_exp_down[0])
    y_p, y_s = _combine(x2_p, x2_s, route, plan, ys, row(g_final), tm=TOKEN_TILE)

    heads = lambda t, n: t.reshape(1, n, -1, N_HEADS, HEAD_DIM)
    mem_heads = lambda t: t.reshape(1, bsz, n_mem, N_HEADS_MEM, HEAD_DIM_MEM)
    return (y_p.reshape(bsz, seq, d), y_s.reshape(sb, st, d),
            heads(kaf, bsz), heads(vaf, bsz), logf.reshape(1, bsz, seq, N_HEADS),
            heads(kbf, bsz), heads(vbf, bsz), mem_heads(mkf), mem_heads(mvf),
            heads(skaf, sb), heads(svaf, sb), slogf.reshape(1, sb, st, N_HEADS),
            heads(nbk, sb), heads(nbv, sb))
```

```python
import functools

import jax
import jax.numpy as jnp
from jax import lax
from jax.experimental import pallas as pl
from jax.experimental.pallas import tpu as pltpu

F32 = jnp.float32
BF16 = jnp.bfloat16

D_MODEL = 1024
HEAD_DIM = 64
N_HEADS = 8
W_GROUP = N_HEADS * HEAD_DIM
N_PAIRS = N_HEADS // 2
CHUNK = 64
LEFT_CHUNKS = 8
LEFT = LEFT_CHUNKS * CHUNK
REL_CLIP = 128
EPS = 1e-6
NEG_INF = -1e30
ATTN_SCALE = HEAD_DIM ** -0.5
LANES = 128
PROJ_PAD = 3 * W_GROUP * 2 + LANES
VMEM_LIMIT = 56 * 1024 * 1024


def _rms(x, g):
    ms = jnp.mean(x * x, axis=-1, keepdims=True)
    return x * lax.rsqrt(ms + EPS) * g


def _log_sigmoid(x):
    return -(jnp.maximum(-x, 0.0) + jnp.log1p(jnp.exp(-jnp.abs(x))))


def _lane_cumsum(x):
    n = x.shape[-1]
    lane = lax.broadcasted_iota(jnp.int32, x.shape, 1)
    k = 1
    while k < n:
        x = x + jnp.where(lane >= k, pltpu.roll(x, k, axis=1), 0.0)
        k *= 2
    return x


LOG2E = 1.4426950408889634
SCALE_BASE2 = ATTN_SCALE * LOG2E


def _split3(x):
    hi = x.astype(BF16).astype(F32)
    mid = (x - hi).astype(BF16).astype(F32)
    lo = x - hi - mid
    return hi, mid, lo


def _extra_lane(parity):
    return HEAD_DIM if parity == 0 else 0


def _fox_extras(c3t, hp, tm):
    row = lax.broadcasted_iota(jnp.int32, (8, tm), 0)

    def group(h, q_side):
        hi, mid, lo = (p[h:h + 1, :] for p in c3t)
        if q_side:
            return jnp.where(row < 3, 1.0, jnp.where(row == 3, hi, jnp.where(row == 4, mid, jnp.where(row == 5, lo, 0.0))))
        return jnp.where(row == 0, -hi, jnp.where(row == 1, -mid, jnp.where(row == 2, -lo, jnp.where(row < 6, 1.0, 0.0))))

    gap = jnp.zeros((HEAD_DIM - 8, tm), F32)
    sides = []
    for q_side in (True, False):
        t = jnp.concatenate([group(2 * hp + 1, q_side), gap, group(2 * hp, q_side), gap], axis=0)
        sides.append(t.T)
    return sides


def _head_blocks(x128, extras, lane):
    return (jnp.where(lane < HEAD_DIM, x128, extras).astype(BF16),
            jnp.where(lane >= HEAD_DIM, x128, extras).astype(BF16))


Q_A, K_A, V_A, Q_B, K_B, V_B = range(6)


PROJ_CHAIN = 256


def _chains(tm, chain=PROJ_CHAIN):
    n = max(tm // chain, 1)
    return [pl.ds(i * (tm // n), tm // n) for i in range(n)]


def _proj_common(rows, x_ref, g_ref, w_ref, bf_ref, kaf_ref, vaf_ref, kbf_ref, vbf_ref, logf_ref, keep_tiles):
    s = pl.program_id(1)
    ns = pl.num_programs(1)
    h = _rms(x_ref[rows, :], g_ref[...]).astype(BF16)
    w = W_GROUP
    zf = jnp.dot(h, w_ref[:, 6 * w:6 * w + LANES], preferred_element_type=F32)
    z = [jnp.dot(h, w_ref[:, g * w:(g + 1) * w], preferred_element_type=F32) for g in range(6)]
    kaf_ref[rows, :] = z[K_A]
    vaf_ref[rows, :] = z[V_A]

    @pl.when(s >= ns - keep_tiles)
    def _():
        kbf_ref[rows, :] = z[K_B]
        vbf_ref[rows, :] = z[V_B]

    logf = _log_sigmoid(zf + bf_ref[...])
    logf_ref[rows, :] = logf[:, :N_HEADS]
    return z, logf


def _proj_prompt_body(x_ref, g_ref, w_ref, bf_ref, qx_ref, kx_ref, vat_ref, qxb_ref, kxb_ref, vbt_ref,
                      kaf_ref, vaf_ref, kbf_ref, vbf_ref, logf_ref, carry_ref, *, tm, keep_tiles):
    @pl.when(pl.program_id(1) == 0)
    def _():
        carry_ref[...] = jnp.zeros_like(carry_ref)

    for rows in _chains(tm):
        n = rows.size
        z, logf = _proj_common(rows, x_ref, g_ref, w_ref, bf_ref, kaf_ref, vaf_ref, kbf_ref, vbf_ref, logf_ref,
                               keep_tiles)
        vat_ref[:, rows] = z[V_A].T.astype(BF16)
        vbt_ref[:, rows] = z[V_B].T.astype(BF16)
        ct = _lane_cumsum(logf.T[:N_HEADS, :]) + carry_ref[:, 0:1]
        carry_ref[...] = jnp.broadcast_to(ct[:, n - 1:n], carry_ref.shape)
        c3t = _split3(ct * LOG2E)
        lane = lax.broadcasted_iota(jnp.int32, (n, LANES), 1)
        band_q_extras = jnp.where((lane == _extra_lane(0)) | (lane == _extra_lane(1)), 1.0, 0.0)
        band_k_extras = jnp.zeros((n, LANES), F32)
        for hp in range(N_PAIRS):
            blocks = slice(2 * hp * LANES, 2 * (hp + 1) * LANES)
            blk = lambda group, hp=hp, z=z: z[group][:, hp * LANES:(hp + 1) * LANES]
            q_extras, k_extras = _fox_extras(c3t, hp, n)
            qx_ref[rows, blocks] = jnp.concatenate(_head_blocks(blk(Q_A) * SCALE_BASE2, q_extras, lane), axis=1)
            kx_ref[rows, blocks] = jnp.concatenate(_head_blocks(blk(K_A), k_extras, lane), axis=1)
            qxb_ref[rows, blocks] = jnp.concatenate(_head_blocks(blk(Q_B) * SCALE_BASE2, band_q_extras, lane), axis=1)
            kxb_ref[rows, blocks] = jnp.concatenate(_head_blocks(blk(K_B), band_k_extras, lane), axis=1)


def _proj_sample_body(x_ref, g_ref, w_ref, bf_ref, qa_ref, ka_ref, va_ref, qb_ref, kb_ref, vb_ref,
                      kaf_ref, vaf_ref, kbf_ref, vbf_ref, logf_ref, lt_ref, *, tm, keep_tiles):
    for rows in _chains(tm):
        z, logf = _proj_common(rows, x_ref, g_ref, w_ref, bf_ref, kaf_ref, vaf_ref, kbf_ref, vbf_ref, logf_ref,
                               keep_tiles)
        qa_ref[rows, :] = (z[Q_A] * ATTN_SCALE).astype(BF16)
        ka_ref[rows, :] = z[K_A].astype(BF16)
        va_ref[rows, :] = z[V_A].astype(BF16)
        qb_ref[rows, :] = (z[Q_B] * ATTN_SCALE).astype(BF16)
        kb_ref[rows, :] = z[K_B].astype(BF16)
        vb_ref[rows, :] = z[V_B].astype(BF16)
        lt_ref[:, rows] = logf.T[:N_HEADS, :]


def _proj(x, g_mix, w_pad, bf_pad, *, tm, prompt):
    b, s, d = x.shape
    ns = s // tm
    keep = min(LEFT, s)
    assert s % tm == 0 and keep % tm == 0
    keep_tiles = keep // tm
    row = pl.BlockSpec((None, tm, W_GROUP), lambda i, j: (i, j, 0))
    wide = pl.BlockSpec((None, tm, N_HEADS * LANES), lambda i, j: (i, j, 0))
    col = pl.BlockSpec((None, W_GROUP, tm), lambda i, j: (i, 0, j))
    keep_spec = pl.BlockSpec((None, tm, W_GROUP), lambda i, j: (i, jnp.maximum(j - (ns - keep_tiles), 0), 0))
    heads_row = pl.BlockSpec((None, tm, N_HEADS), lambda i, j: (i, j, 0))
    heads_col = pl.BlockSpec((None, N_HEADS, tm), lambda i, j: (i, 0, j))
    const = lambda shape: pl.BlockSpec(shape, lambda i, j: (0,) * len(shape))
    rows_bf = jax.ShapeDtypeStruct((b, s, W_GROUP), BF16)
    wide_bf = jax.ShapeDtypeStruct((b, s, N_HEADS * LANES), BF16)
    cols_bf = jax.ShapeDtypeStruct((b, W_GROUP, s), BF16)
    f32_tail = [jax.ShapeDtypeStruct((b, s, W_GROUP), F32)] * 2
    f32_tail += [jax.ShapeDtypeStruct((b, keep, W_GROUP), F32)] * 2
    f32_tail += [jax.ShapeDtypeStruct((b, s, N_HEADS), F32)]
    tail_specs = [row, row, keep_spec, keep_spec, heads_row]
    if prompt:
        body = functools.partial(_proj_prompt_body, tm=tm, keep_tiles=keep_tiles)
        out_shape = [wide_bf, wide_bf, cols_bf, wide_bf, wide_bf, cols_bf] + f32_tail
        out_specs = [wide, wide, col, wide, wide, col] + tail_specs
        scratch = [pltpu.VMEM((N_HEADS, LANES), F32)]
    else:
        body = functools.partial(_proj_sample_body, tm=tm, keep_tiles=keep_tiles)
        out_shape = [rows_bf] * 6 + f32_tail + [jax.ShapeDtypeStruct((b, N_HEADS, s), F32)]
        out_specs = [row] * 6 + tail_specs + [heads_col]
        scratch = []
    return pl.pallas_call(
        body,
        grid=(b, ns),
        in_specs=[pl.BlockSpec((None, tm, d), lambda i, j: (i, j, 0)),
                  const((1, d)), const(w_pad.shape), const((1, LANES))],
        out_specs=out_specs,
        out_shape=out_shape,
        scratch_shapes=scratch,
        compiler_params=pltpu.CompilerParams(
            dimension_semantics=("parallel", "arbitrary"), vmem_limit_bytes=VMEM_LIMIT),
        name="proj",
    )(x, g_mix, w_pad, bf_pad)


def _prep_proj(w_in, b_forget, g_mix):
    cols = w_in.shape[-1]
    w_pad = jnp.pad(w_in, ((0, 0), (0, PROJ_PAD - cols))).astype(BF16)
    bf_pad = jnp.pad(b_forget.reshape(1, -1), ((0, 0), (0, LANES - N_HEADS))).astype(F32)
    return w_pad, bf_pad, g_mix.reshape(1, -1)


def _pair_masks():
    lane = lax.broadcasted_iota(jnp.int32, (1, LANES), 1)
    return lane < HEAD_DIM


def _head_q(q128, even_lanes, parity):
    keep = even_lanes if parity == 0 else jnp.logical_not(even_lanes)
    return jnp.where(keep, q128, jnp.zeros_like(q128))


def _head_v(v128, even_lanes, parity):
    keep = even_lanes if parity == 0 else jnp.logical_not(even_lanes)
    return jnp.where(keep, v128, jnp.ones_like(v128))


def _head_out(acc_even, acc_odd, even_lanes):
    inv_e = 1.0 / acc_even[:, HEAD_DIM:HEAD_DIM + 1]
    inv_o = 1.0 / acc_odd[:, 0:1]
    return jnp.where(even_lanes, acc_even * inv_e, acc_odd * inv_o)


_NT = (((1,), (1,)), ((), ()))


def _pair_rows():
    row = lax.broadcasted_iota(jnp.int32, (LANES, 1), 0)
    return row < HEAD_DIM


def _head_vt(vt128, even_rows, parity):
    keep = even_rows if parity == 0 else jnp.logical_not(even_rows)
    return jnp.where(keep, vt128, jnp.ones_like(vt128))


def _head_out_t(acc_even, acc_odd, even_rows):
    inv_e = 1.0 / acc_even[HEAD_DIM:HEAD_DIM + 1, :]
    inv_o = 1.0 / acc_odd[0:1, :]
    return jnp.where(even_rows, acc_even * inv_e, acc_odd * inv_o)


def _fox_body(qx_ref, kx_ref, vt_ref, g_ref, o_ref, s_scr, top_scr, p_scr, m_scr, alpha_scr, acc_scr, ot_scr,
              *, tq):
    qi = pl.program_id(1)
    even_rows = _pair_rows()
    m_scr[...] = jnp.full(m_scr.shape, NEG_INF, F32)
    acc_scr[...] = jnp.zeros(acc_scr.shape, F32)
    key = lax.broadcasted_iota(jnp.int32, (tq, tq), 0)
    qry = lax.broadcasted_iota(jnp.int32, (tq, tq), 1)
    causal = key <= qry

    def logits(j, half):
        start = pl.multiple_of(j * tq, tq)
        for h in range(N_HEADS):
            head = slice(h * LANES, (h + 1) * LANES)
            st = lax.dot_general(kx_ref[pl.ds(start, tq), head], qx_ref[:, head], _NT,
                                 preferred_element_type=F32)
            s_scr[half, h] = st
            top_scr[half, h:h + 1, :] = jnp.max(st, axis=0, keepdims=True)

    def weigh(j, half, masked):
        start = pl.multiple_of(j * tq, tq)
        for h in range(N_HEADS):
            st = s_scr[half, h]
            if masked:
                st = jnp.where(causal, st, NEG_INF)
                top = jnp.max(st, axis=0, keepdims=True)
            else:
                top = top_scr[half, h:h + 1, :]
            m_old = m_scr[h:h + 1, :]
            m_new = jnp.maximum(m_old, top)
            p_scr[h] = jnp.exp2(st - m_new).astype(BF16)
            alpha_scr[h:h + 1, :] = jnp.exp2(m_old - m_new)
            m_scr[h:h + 1, :] = m_new
        for h in range(N_HEADS):
            pair = slice((h // 2) * LANES, (h // 2 + 1) * LANES)
            vt = _head_vt(vt_ref[pair, pl.ds(start, tq)], even_rows, h % 2)
            acc_scr[h] = acc_scr[h] * alpha_scr[h:h + 1, :] + jnp.dot(vt, p_scr[h], preferred_element_type=F32)

    def step(j, cur, nxt):
        logits(j + 1, nxt)
        weigh(j, cur, False)

    def body(jj, carry):
        step(2 * jj, 0, 1)
        step(2 * jj + 1, 1, 0)
        return carry

    logits(0, 0)
    lax.fori_loop(0, qi // 2, body, None)
    odd = lax.rem(qi, 2) == 1

    @pl.when(odd)
    def _():
        step(qi - 1, 0, 1)
        weigh(qi, 1, True)

    @pl.when(jnp.logical_not(odd))
    def _():
        weigh(qi, 0, True)
    for hp in range(N_PAIRS):
        ot_scr[hp * LANES:(hp + 1) * LANES, :] = _head_out_t(acc_scr[2 * hp], acc_scr[2 * hp + 1], even_rows)
    o_ref[...] = _rms(ot_scr[...].T, g_ref[...]).astype(BF16)


def _fox_prompt(qx, kx, vat, g_out, *, tq):
    b, s, wx = qx.shape
    w = vat.shape[1]
    return pl.pallas_call(
        functools.partial(_fox_body, tq=tq),
        grid=(b, s // tq),
        in_specs=[pl.BlockSpec((None, tq, wx), lambda i, j: (i, j, 0)),
                  pl.BlockSpec((None, s, wx), lambda i, j: (i, 0, 0)),
                  pl.BlockSpec((None, w, s), lambda i, j: (i, 0, 0)),
                  pl.BlockSpec((1, w), lambda i, j: (0, 0))],
        out_specs=pl.BlockSpec((None, tq, w), lambda i, j: (i, j, 0)),
        out_shape=jax.ShapeDtypeStruct((b, s, w), BF16),
        scratch_shapes=[pltpu.VMEM((2, N_HEADS, tq, tq), F32), pltpu.VMEM((2, N_HEADS, tq), F32),
                        pltpu.VMEM((N_HEADS, tq, tq), BF16),
                        pltpu.VMEM((N_HEADS, tq), F32), pltpu.VMEM((N_HEADS, tq), F32),
                        pltpu.VMEM((N_HEADS, LANES, tq), F32), pltpu.VMEM((w, tq), F32)],
        compiler_params=pltpu.CompilerParams(
            dimension_semantics=("parallel", "arbitrary"), vmem_limit_bytes=VMEM_LIMIT),
        name="fox_prompt",
    )(qx, kx, vat, g_out)


BAND_CHUNKS = 4
BAND_Q = BAND_CHUNKS * CHUNK
BAND_K = (LEFT_CHUNKS + BAND_CHUNKS) * CHUNK
BIAS_ROW = BAND_K + BAND_Q
BAND_K_SAMPLE = (LEFT + CHUNK + LANES - 1) // LANES * LANES


def _prep_band_bias_row(rel_table):
    pivot = LEFT + BAND_Q
    n_hi = pivot - REL_CLIP + 1
    n_mid = min(2 * REL_CLIP, BIAS_ROW - n_hi)
    n_lo = BIAS_ROW - n_hi - n_mid
    parts = [jnp.broadcast_to(rel_table[2 * REL_CLIP:], (n_hi, N_HEADS)),
             rel_table[2 * REL_CLIP - 1::-1][:n_mid],
             jnp.broadcast_to(rel_table[:1], (n_lo, N_HEADS))]
    return jnp.concatenate(parts, axis=0).T.reshape(N_HEADS, 1, BIAS_ROW)


def _band_bias_body(row_ref, bt_ref, bs_ref):
    rows = jnp.broadcast_to(row_ref[...], (BAND_Q, BIAS_ROW))
    skew = pltpu.roll(rows, 0, axis=1, stride=1, stride_axis=0)
    bias = skew[:, BAND_Q:]
    bs_ref[...] = bias[:CHUNK, :BAND_K_SAMPLE]
    qc = lax.broadcasted_iota(jnp.int32, (BAND_Q, BAND_K), 0) // CHUNK
    kc = lax.broadcasted_iota(jnp.int32, (BAND_Q, BAND_K), 1) // CHUNK
    bt_ref[...] = jnp.where((kc >= qc) & (kc <= qc + LEFT_CHUNKS), bias * LOG2E, NEG_INF).T


def _band_bias(bias_row):
    return pl.pallas_call(
        _band_bias_body,
        grid=(N_HEADS,),
        in_specs=[pl.BlockSpec((None, 1, BIAS_ROW), lambda h: (h, 0, 0))],
        out_specs=[pl.BlockSpec((None, BAND_K, BAND_Q), lambda h: (h, 0, 0)),
                   pl.BlockSpec((None, CHUNK, BAND_K_SAMPLE), lambda h: (h, 0, 0))],
        out_shape=[jax.ShapeDtypeStruct((N_HEADS, BAND_K, BAND_Q), F32),
                   jax.ShapeDtypeStruct((N_HEADS, CHUNK, BAND_K_SAMPLE), F32)],
        name="band_bias",
    )(bias_row)


def _band_body(qx_ref, kx_ref, vt_ref, bias_ref, g_ref, o_ref, kpad, vtpad, s_scr, p_scr, ot_scr, *, s_len):
    step = pl.program_id(1)

    @pl.when(step == 0)
    def _():
        lane = lax.broadcasted_iota(jnp.int32, (LEFT, 2 * LANES), 1)
        flags = (lane == _extra_lane(0)) | (lane == LANES + _extra_lane(1))
        pad_pair = jnp.where(flags, NEG_INF, 0.0).astype(BF16)
        for hp in range(N_PAIRS):
            kpad[0:LEFT, 2 * hp * LANES:2 * (hp + 1) * LANES] = pad_pair
        vtpad[:, 0:LEFT] = jnp.zeros((W_GROUP, LEFT), BF16)
        kpad[LEFT:LEFT + s_len, :] = kx_ref[...]
        vtpad[:, LEFT:LEFT + s_len] = vt_ref[...]

    even_rows = _pair_rows()
    start = pl.multiple_of(step * BAND_Q, BAND_Q)
    def logits(h):
        head = slice(h * LANES, (h + 1) * LANES)
        s_scr[h] = bias_ref[h] + lax.dot_general(kpad[pl.ds(start, BAND_K), head], qx_ref[:, head], _NT,
                                                 preferred_element_type=F32)

    def weigh(h):
        st = s_scr[h]
        p_scr[h] = jnp.exp2(st - jnp.max(st, axis=0, keepdims=True)).astype(BF16)

    def values(hp):
        pair = slice(hp * LANES, (hp + 1) * LANES)
        vtwin = vtpad[pair, pl.ds(start, BAND_K)]
        accs = [jnp.dot(_head_vt(vtwin, even_rows, parity), p_scr[2 * hp + parity], preferred_element_type=F32)
                for parity in range(2)]
        ot_scr[pair, :] = _head_out_t(accs[0], accs[1], even_rows)

    lead = N_HEADS // 2
    for h in range(lead):
        logits(h)
    for h in range(N_HEADS):
        weigh(h)
        if h + lead < N_HEADS:
            logits(h + lead)
        if h % 2 == 1:
            values(h // 2)
    o_ref[...] = _rms(ot_scr[...].T, g_ref[...]).astype(BF16)


def _band_prompt(qxb, kxb, vbt, bias_t, g_out):
    b, s, wx = qxb.shape
    w = vbt.shape[1]
    return pl.pallas_call(
        functools.partial(_band_body, s_len=s),
        grid=(b, s // BAND_Q),
        in_specs=[pl.BlockSpec((None, BAND_Q, wx), lambda i, j: (i, j, 0)),
                  pl.BlockSpec((None, s, wx), lambda i, j: (i, 0, 0)),
                  pl.BlockSpec((None, w, s), lambda i, j: (i, 0, 0)),
                  pl.BlockSpec(bias_t.shape, lambda i, j: (0, 0, 0)),
                  pl.BlockSpec((1, w), lambda i, j: (0, 0))],
        out_specs=pl.BlockSpec((None, BAND_Q, w), lambda i, j: (i, j, 0)),
        out_shape=jax.ShapeDtypeStruct((b, s, w), BF16),
        scratch_shapes=[pltpu.VMEM((LEFT + s, wx), BF16), pltpu.VMEM((w, LEFT + s), BF16),
                        pltpu.VMEM((N_HEADS, BAND_K, BAND_Q), F32), pltpu.VMEM((N_HEADS, BAND_K, BAND_Q), BF16),
                        pltpu.VMEM((w, BAND_Q), F32)],
        compiler_params=pltpu.CompilerParams(
            dimension_semantics=("parallel", "arbitrary"), vmem_limit_bytes=VMEM_LIMIT),
        name="band_prompt",
    )(qxb, kxb, vbt, bias_t, g_out)


def _row_to_col(row):
    n = row.shape[-1]
    r = lax.broadcasted_iota(jnp.int32, (n, n), 0)
    c = lax.broadcasted_iota(jnp.int32, (n, n), 1)
    return jnp.sum(jnp.where(r == c, jnp.broadcast_to(row, (n, n)), 0.0), axis=-1, keepdims=True)


def _fox_sample_body(q_ref, kn_ref, vn_ref, lft_ref, kc_ref, vc_ref, clft_ref, g_ref, o_ref,
                     cct_scr, cn_scr, m_scr, acc_scr, o_scr, s_scr, p_scr, alpha_scr, *, t_new, pt):
    p_idx = pl.program_id(1)
    n_p = pl.num_programs(1)
    even = _pair_masks()

    @pl.when(p_idx == 0)
    def _():
        cct = _lane_cumsum(clft_ref[...])
        cct_scr[...] = cct
        cn_scr[...] = _lane_cumsum(lft_ref[...]) + cct[:, cct.shape[1] - 1:]
        m_scr[...] = jnp.full(m_scr.shape, NEG_INF, F32)
        acc_scr[...] = jnp.zeros(acc_scr.shape, F32)

    start = pl.multiple_of(p_idx * pt, pt)
    pair = lambda h: slice((h // 2) * LANES, (h // 2 + 1) * LANES)
    for h in range(N_HEADS):
        s_scr[h] = lax.dot_general(_head_q(q_ref[:, pair(h)], even, h % 2), kc_ref[:, pair(h)].astype(BF16),
                                   _NT, preferred_element_type=F32)
    for h in range(N_HEADS):
        cq = _row_to_col(cn_scr[h:h + 1, :])
        s = s_scr[h] + cq - cct_scr[h:h + 1, pl.ds(start, pt)]
        m = m_scr[h]
        m_new = jnp.maximum(m, jnp.max(s, axis=-1, keepdims=True))
        p_scr[h] = jnp.exp(s - m_new).astype(BF16)
        alpha_scr[h] = jnp.exp(m - m_new)
        m_scr[h] = m_new
    for h in range(N_HEADS):
        v = _head_v(vc_ref[:, pair(h)].astype(BF16), even, h % 2)
        acc_scr[h] = acc_scr[h] * alpha_scr[h] + jnp.dot(p_scr[h], v, preferred_element_type=F32)

    @pl.when(p_idx == n_p - 1)
    def _():
        row = lax.broadcasted_iota(jnp.int32, (t_new, t_new), 0)
        col = lax.broadcasted_iota(jnp.int32, (t_new, t_new), 1)
        logits = []
        for h in range(N_HEADS):
            cn_row = cn_scr[h:h + 1, :]
            s = lax.dot_general(_head_q(q_ref[:, pair(h)], even, h % 2), kn_ref[:, pair(h)], _NT,
                                preferred_element_type=F32)
            logits.append(jnp.where(col <= row, s + _row_to_col(cn_row) - cn_row, NEG_INF))
        weights, scales = [], []
        for h in range(N_HEADS):
            m = m_scr[h]
            m_new = jnp.maximum(m, jnp.max(logits[h], axis=-1, keepdims=True))
            weights.append(jnp.exp(logits[h] - m_new).astype(BF16))
            scales.append(jnp.exp(m - m_new))
        accs = [acc_scr[h] * scales[h] + jnp.dot(weights[h], _head_v(vn_ref[:, pair(h)], even, h % 2),
                                                 preferred_element_type=F32) for h in range(N_HEADS)]
        for hp in range(N_PAIRS):
            o_scr[:, hp * LANES:(hp + 1) * LANES] = _head_out(accs[2 * hp], accs[2 * hp + 1], even)
        o_ref[...] = _rms(o_scr[...], g_ref[...]).astype(BF16)


def _fox_sample(q, kn, vn, lft, kc, vc, clft, g_out, *, pt):
    b, t, w = q.shape
    p_len = kc.shape[1]
    new = lambda: pl.BlockSpec((None, t, w), lambda i, j: (i, 0, 0))
    cache = lambda: pl.BlockSpec((None, pt, w), lambda i, j: (i, j, 0))
    return pl.pallas_call(
        functools.partial(_fox_sample_body, t_new=t, pt=pt),
        grid=(b, p_len // pt),
        in_specs=[new(), new(), new(),
                  pl.BlockSpec((None, N_HEADS, t), lambda i, j: (i, 0, 0)),
                  cache(), cache(),
                  pl.BlockSpec((None, N_HEADS, p_len), lambda i, j: (i, 0, 0)),
                  pl.BlockSpec((1, w), lambda i, j: (0, 0))],
        out_specs=new(),
        out_shape=jax.ShapeDtypeStruct((b, t, w), BF16),
        scratch_shapes=[pltpu.VMEM((N_HEADS, p_len), F32), pltpu.VMEM((N_HEADS, t), F32),
                        pltpu.VMEM((N_HEADS, t, 1), F32), pltpu.VMEM((N_HEADS, t, LANES), F32),
                        pltpu.VMEM((t, w), F32), pltpu.VMEM((N_HEADS, t, pt), F32),
                        pltpu.VMEM((N_HEADS, t, pt), BF16), pltpu.VMEM((N_HEADS, t, 1), F32)],
        compiler_params=pltpu.CompilerParams(
            dimension_semantics=("parallel", "arbitrary"), vmem_limit_bytes=VMEM_LIMIT),
        name="fox_sample",
    )(q, kn, vn, lft, kc, vc, clft, g_out)


def _band_sample_body(q_ref, kn_ref, vn_ref, knf_ref, vnf_ref, kc_ref, vc_ref, bias_ref, g_ref,
                      o_ref, nk_ref, nv_ref, kcat, vcat, o_scr, *, t_new, bp):
    kcat[0:bp, :] = kc_ref[...].astype(BF16)
    vcat[0:bp, :] = vc_ref[...].astype(BF16)
    kcat[bp:bp + t_new, :] = kn_ref[...]
    vcat[bp:bp + t_new, :] = vn_ref[...]
    nk_ref[0:bp - t_new, :] = kc_ref[t_new:bp, :]
    nv_ref[0:bp - t_new, :] = vc_ref[t_new:bp, :]
    nk_ref[bp - t_new:bp, :] = knf_ref[...]
    nv_ref[bp - t_new:bp, :] = vnf_ref[...]
    even = _pair_masks()
    for hp in range(N_PAIRS):
        lanes = slice(hp * LANES, (hp + 1) * LANES)
        q128 = q_ref[:, lanes]
        k = kcat[:, lanes]
        v = vcat[:, lanes]
        accs = []
        for parity in range(2):
            h = 2 * hp + parity
            s = lax.dot_general(_head_q(q128, even, parity), k, _NT, preferred_element_type=F32)
            s = s + bias_ref[h, 0:t_new, 0:bp + t_new]
            p = jnp.exp(s - jnp.max(s, axis=-1, keepdims=True)).astype(BF16)
            accs.append(jnp.dot(p, _head_v(v, even, parity), preferred_element_type=F32))
        o_scr[:, lanes] = _head_out(accs[0], accs[1], even)
    o_ref[...] = _rms(o_scr[...], g_ref[...]).astype(BF16)


def _band_sample(q, kn, vn, knf, vnf, kc, vc, bias, g_out):
    b, t, w = q.shape
    bp = kc.shape[1]
    assert t == CHUNK and bp == LEFT
    new = lambda: pl.BlockSpec((None, t, w), lambda i: (i, 0, 0))
    buf = lambda: pl.BlockSpec((None, bp, w), lambda i: (i, 0, 0))
    return pl.pallas_call(
        functools.partial(_band_sample_body, t_new=t, bp=bp),
        grid=(b,),
        in_specs=[new(), new(), new(), new(), new(), buf(), buf(),
                  pl.BlockSpec(bias.shape, lambda i: (0, 0, 0)),
                  pl.BlockSpec((1, w), lambda i: (0, 0))],
        out_specs=[new(), buf(), buf()],
        out_shape=[jax.ShapeDtypeStruct((b, t, w), BF16), jax.ShapeDtypeStruct((b, bp, w), F32),
                   jax.ShapeDtypeStruct((b, bp, w), F32)],
        scratch_shapes=[pltpu.VMEM((bp + t, w), BF16), pltpu.VMEM((bp + t, w), BF16), pltpu.VMEM((t, w), F32)],
        compiler_params=pltpu.CompilerParams(dimension_semantics=("parallel",), vmem_limit_bytes=VMEM_LIMIT),
        name="band_sample",
    )(q, kn, vn, knf, vnf, kc, vc, bias, g_out)


N_HEADS_MEM = 4
HEAD_DIM_MEM = 128
W_MEM = N_HEADS_MEM * HEAD_DIM_MEM
MEM_SCALE = HEAD_DIM_MEM ** -0.5


def _memkv_body(m_ref, g_ref, w_ref, kf_ref, vf_ref, k_ref, v_ref):
    h = _rms(m_ref[...], g_ref[...]).astype(BF16)
    z = jnp.dot(h, w_ref[...], preferred_element_type=F32)
    kf_ref[...] = z[:, :W_MEM]
    vf_ref[...] = z[:, W_MEM:]
    k_ref[...] = z[:, :W_MEM].astype(BF16)
    v_ref[...] = z[:, W_MEM:].astype(BF16)


def _mem_kv(mem, g_mem, w_ckv):
    b, n, d = mem.shape
    blk = lambda: pl.BlockSpec((None, n, W_MEM), lambda i: (i, 0, 0))
    return pl.pallas_call(
        _memkv_body,
        grid=(b,),
        in_specs=[pl.BlockSpec((None, n, d), lambda i: (i, 0, 0)),
                  pl.BlockSpec((1, d), lambda i: (0, 0)),
                  pl.BlockSpec(w_ckv.shape, lambda i: (0, 0))],
        out_specs=[blk(), blk(), blk(), blk()],
        out_shape=[jax.ShapeDtypeStruct((b, n, W_MEM), F32)] * 2 + [jax.ShapeDtypeStruct((b, n, W_MEM), BF16)] * 2,
        compiler_params=pltpu.CompilerParams(dimension_semantics=("parallel",), vmem_limit_bytes=VMEM_LIMIT),
        name="mem_kv",
    )(mem, g_mem, w_ckv)


N_GROUPS = 4
EXPERTS_PER_GROUP = 8
N_EXPERTS = N_GROUPS * EXPERTS_PER_GROUP
ROUTE_L2 = N_GROUPS
ROUTE_ROWS = 8
POST_CHAIN = 512
R_EID0, R_EID1, R_RANK0, R_RANK1, R_GATE0, R_GATE1 = range(6)


ROUTE_LOGIT_ROWS = 40


def _row_max(x, mask):
    return jnp.max(jnp.where(mask, x, -jnp.inf), axis=0, keepdims=True)


def _first_row(mask, row):
    return jnp.min(jnp.where(mask, row, LANES), axis=0, keepdims=True)


def _route(logits, row):
    is_l1 = row < N_GROUPS
    m1 = _row_max(logits, is_l1)
    grp = _first_row(is_l1 & (logits == m1), row)
    wg = 1.0 / jnp.sum(jnp.where(is_l1, jnp.exp(logits - m1), 0.0), axis=0, keepdims=True)
    lo = ROUTE_L2 + grp * EXPERTS_PER_GROUP
    in_grp = (row >= lo) & (row < lo + EXPERTS_PER_GROUP)
    v0 = _row_max(logits, in_grp)
    i0 = _first_row(in_grp & (logits == v0), row)
    rest = in_grp & (row != i0)
    v1 = _row_max(logits, rest)
    i1 = _first_row(rest & (logits == v1), row)
    e1 = jnp.exp(v1 - v0)
    den = 1.0 / (1.0 + e1)
    return i0, i1, wg * den, wg * e1 * den


def _post_body(x_ref, a_ref, b_ref, mk_ref, mv_ref, woa_ref, wob_ref, gc_ref, wcq_ref, wco_ref,
               gf_ref, wrt_ref, brt_ref,
               x2_ref, h3_ref, route_ref, routet_ref, cnt_ref, o_scr, *, tm, nsub):
    seq = tm // nsub
    count = jnp.zeros((1, LANES), F32)
    for rows in _chains(tm, POST_CHAIN):
        n = rows.size
        x1 = (x_ref[rows, :] + jnp.dot(a_ref[rows, :], woa_ref[...], preferred_element_type=F32)
              + jnp.dot(b_ref[rows, :], wob_ref[...], preferred_element_type=F32))
        h2 = _rms(x1, gc_ref[...]).astype(BF16)
        qc = (jnp.dot(h2, wcq_ref[...], preferred_element_type=F32) * MEM_SCALE).astype(BF16)
        span = min(seq, n)
        for part in range(n // span):
            sub = (rows.start + part * span) // seq
            rs = slice(part * span, (part + 1) * span)
            orow = pl.ds(rows.start + part * span, span)
            for hm in range(N_HEADS_MEM):
                lanes = slice(hm * HEAD_DIM_MEM, (hm + 1) * HEAD_DIM_MEM)
                s = lax.dot_general(qc[rs, lanes], mk_ref[sub, :, lanes], _NT, preferred_element_type=F32)
                p = jnp.exp(s - jnp.max(s, axis=-1, keepdims=True))
                inv = 1.0 / jnp.sum(p, axis=-1, keepdims=True)
                o_scr[orow, lanes] = jnp.dot(p.astype(BF16), mv_ref[sub, :, lanes], preferred_element_type=F32) * inv
        x2 = x1 + jnp.dot(o_scr[rows, :].astype(BF16), wco_ref[...], preferred_element_type=F32)
        x2_ref[rows, :] = x2
        h3 = _rms(x2, gf_ref[...]).astype(BF16)
        h3_ref[rows, :] = h3

        logits = lax.dot_general(wrt_ref[...], h3, _NT, preferred_element_type=F32) + brt_ref[...]
        row = lax.broadcasted_iota(jnp.int32, (ROUTE_LOGIT_ROWS, n), 0)
        i0, i1, g0, g1 = _route(logits[:ROUTE_LOGIT_ROWS, :], row)
        e0 = i0 - ROUTE_L2
        e1 = i1 - ROUTE_L2
        expert = lax.broadcasted_iota(jnp.int32, (LANES, n), 0)
        hit0 = expert == e0
        hit1 = expert == e1
        onehot = jnp.where(hit0, 1.0, jnp.where(hit1, 1.0, 0.0)).astype(BF16)
        earlier = lax.broadcasted_iota(jnp.int32, (n, n), 0) < lax.broadcasted_iota(jnp.int32, (n, n), 1)
        before = jnp.where(earlier, 1.0, 0.0).astype(BF16)
        seen = jnp.dot(onehot, before, preferred_element_type=F32) + _row_to_col(count)
        rank0 = jnp.sum(jnp.where(hit0, seen, 0.0), axis=0, keepdims=True)
        rank1 = jnp.sum(jnp.where(hit1, seen, 0.0), axis=0, keepdims=True)
        count = count + lax.dot_general(jnp.ones((8, n), BF16), onehot, _NT, preferred_element_type=F32)[0:1, :]

        rec = jnp.concatenate([e0.astype(F32), e1.astype(F32), rank0, rank1, g0, g1, jnp.zeros((2, n), F32)], axis=0)
        routet_ref[:, rows] = rec
        route_ref[rows, :] = jnp.concatenate([rec, jnp.zeros((LANES - ROUTE_ROWS, n), F32)], axis=0).T[:, :ROUTE_ROWS]
    cnt_ref[...] = count


def _post_block(x, a_n, b_n, mk, mv, weights, *, tm):
    b, s, d = x.shape
    if s >= tm:
        nsub, grid = 1, (b, s // tm)
        tok = lambda i, j: (i, j, 0)
        flat = lambda i, j: i * (s // tm) + j
    else:
        nsub = tm // s
        assert b % nsub == 0
        x, a_n, b_n = (t.reshape(b // nsub, tm, t.shape[-1]) for t in (x, a_n, b_n))
        grid = (b // nsub, 1)
        tok = lambda i, j: (i, 0, 0)
        flat = lambda i, j: i
    mem = lambda i, j: (i, 0, 0)
    const = lambda arr: pl.BlockSpec(arr.shape, lambda i, j: (0,) * arr.ndim)
    in_specs = [pl.BlockSpec((None, tm, d), tok),
                pl.BlockSpec((None, tm, W_GROUP), tok), pl.BlockSpec((None, tm, W_GROUP), tok),
                pl.BlockSpec((nsub, mk.shape[1], W_MEM), mem), pl.BlockSpec((nsub, mv.shape[1], W_MEM), mem)]
    in_specs += [const(w) for w in weights]
    n = b * s
    out_shape = [jax.ShapeDtypeStruct((n, d), F32), jax.ShapeDtypeStruct((n, d), BF16),
                 jax.ShapeDtypeStruct((n, ROUTE_ROWS), F32), jax.ShapeDtypeStruct((ROUTE_ROWS, n), F32),
                 jax.ShapeDtypeStruct((n // tm, 1, LANES), F32)]
    out_specs = [pl.BlockSpec((tm, d), lambda i, j: (flat(i, j), 0)),
                 pl.BlockSpec((tm, d), lambda i, j: (flat(i, j), 0)),
                 pl.BlockSpec((tm, ROUTE_ROWS), lambda i, j: (flat(i, j), 0)),
                 pl.BlockSpec((ROUTE_ROWS, tm), lambda i, j: (0, flat(i, j))),
                 pl.BlockSpec((None, 1, LANES), lambda i, j: (flat(i, j), 0, 0))]
    return pl.pallas_call(
        functools.partial(_post_body, tm=tm, nsub=nsub),
        grid=grid,
        in_specs=in_specs,
        out_specs=out_specs,
        out_shape=out_shape,
        scratch_shapes=[pltpu.VMEM((tm, W_MEM), F32)],
        compiler_params=pltpu.CompilerParams(
            dimension_semantics=("parallel", "parallel"), vmem_limit_bytes=VMEM_LIMIT),
        name="post_block",
    )(x, a_n, b_n, mk, mv, *weights)


def _prep_post(w_out, g_cross, w_cq, w_co, g_ffn, w_r1, b_r1, w_r2, b_r2):
    pad = LANES - N_GROUPS - N_EXPERTS
    w_rt = jnp.pad(jnp.concatenate([w_r1, w_r2], axis=1), ((0, 0), (0, pad))).astype(BF16).T
    b_rt = jnp.pad(jnp.concatenate([b_r1, b_r2]).reshape(-1, 1), ((0, pad), (0, 0))).astype(F32)
    return [w_out[:W_GROUP].astype(BF16), w_out[W_GROUP:].astype(BF16), g_cross.reshape(1, -1),
            w_cq.astype(BF16), w_co.astype(BF16), g_ffn.reshape(1, -1), w_rt, b_rt]


D_EXPERT = 512
TOP_K = 2
ROW_TILE = 512
MXU_DIM = 256
RUN_ALIGN = 16
PLAN_TILES = LANES
TILE_TABLE = 2 * LANES


def _local_rows(tm):
    return -(-(TOP_K * tm + N_EXPERTS * (RUN_ALIGN - 1)) // MXU_DIM) * MXU_DIM


def _n_row_tiles(n_tokens, tm):
    rows = n_tokens * TOP_K + (n_tokens // tm) * N_EXPERTS * (RUN_ALIGN - 1) + N_EXPERTS * (ROW_TILE - 1)
    return rows // ROW_TILE


N_CHUNK_LANE = LANES - 1


def _plan_body(cnt_ref, lstart_ref, chunk_ref, offs_ref, te_ref):
    cnt = cnt_ref[...].astype(jnp.int32)
    n16 = ((cnt + (RUN_ALIGN - 1)) & ~(RUN_ALIGN - 1)).astype(F32)
    lend = _lane_cumsum(n16)
    lstart = lend - n16
    earlier = (_lane_cumsum(n16.T) - n16.T).T
    total = jnp.sum(n16, axis=0, keepdims=True).astype(jnp.int32)
    seg = jnp.broadcast_to((total + (ROW_TILE - 1)) & ~(ROW_TILE - 1), (8, LANES)).astype(F32)
    ends = _lane_cumsum(seg)
    offs = ends - seg
    shift = earlier + offs[0:1, :] - lstart
    lstart_ref[...] = lstart.astype(jnp.int32)
    offs_ref[...] = offs[0:1, :].astype(jnp.int32)

    lane = lax.broadcasted_iota(jnp.int32, (PLAN_TILES, LANES), 1)
    local_row = (lane * RUN_ALIGN).astype(F32)
    owner = jnp.zeros((PLAN_TILES, LANES), jnp.int32)
    for e in range(N_EXPERTS):
        owner = owner + jnp.where(lend[:, e:e + 1] <= local_row, 1, 0)
    glob = local_row
    for e in range(N_EXPERTS):
        glob = glob + jnp.where(owner == e, shift[:, e:e + 1], 0.0)
    n_chunks = lend[:, N_EXPERTS - 1:N_EXPERTS] * (1.0 / RUN_ALIGN)
    chunk_ref[...] = jnp.where(lane == N_CHUNK_LANE, n_chunks, glob).astype(jnp.int32)

    tile_start = (lax.broadcasted_iota(jnp.int32, te_ref.shape, 1) * ROW_TILE).astype(F32)
    te = jnp.zeros(te_ref.shape, jnp.int32)
    for e in range(N_EXPERTS):
        end_e = jnp.sum(jnp.where(lane[0:1, :] == e, ends[0:1, :], 0.0), axis=-1, keepdims=True)
        te = te + jnp.where(end_e <= tile_start, 1, 0)
    te_ref[...] = jnp.minimum(te, N_EXPERTS - 1)


def _plan(counts, tm):
    nt = counts.shape[0]
    assert nt <= PLAN_TILES and _local_rows(tm) // RUN_ALIGN <= N_CHUNK_LANE
    cnt = jnp.pad(counts.reshape(nt, LANES), ((0, PLAN_TILES - nt), (0, 0)))
    grid_i32 = jax.ShapeDtypeStruct((PLAN_TILES, LANES), jnp.int32)
    lstart, chunks, offs, te = pl.pallas_call(
        _plan_body,
        out_shape=[grid_i32, grid_i32, jax.ShapeDtypeStruct((1, LANES), jnp.int32),
                   jax.ShapeDtypeStruct((1, TILE_TABLE), jnp.int32)],
        name="moe_plan",
    )(cnt)
    per_tile = lambda t: t[:nt].reshape(nt, 1, LANES)
    return per_tile(lstart), per_tile(chunks), offs.reshape(LANES), te.reshape(TILE_TABLE)


def _for_each_chunk(chunk_ref, fn):
    n = chunk_ref[0, N_CHUNK_LANE]

    def visit(c, priority):
        fn(pl.multiple_of(c * RUN_ALIGN, RUN_ALIGN), pl.multiple_of(chunk_ref[0, c], RUN_ALIGN), priority)

    def body(c, carry):
        visit(2 * c, 0)
        visit(2 * c + 1, 1)
        return carry

    lax.fori_loop(0, n // 2, body, None)

    @pl.when(lax.rem(n, 2) == 1)
    def _():
        visit(n - 1, 0)


def _local_positions_row(rt_ref, lstart_ref):
    pos = []
    for r_eid, r_rank in ((R_EID0, R_RANK0), (R_EID1, R_RANK1)):
        eid = rt_ref[r_eid:r_eid + 1, :].astype(jnp.int32)
        p = rt_ref[r_rank:r_rank + 1, :].astype(jnp.int32)
        for e in range(N_EXPERTS):
            p = p + jnp.where(eid == e, lstart_ref[0, e], 0)
        pos.append(p)
    return pos


def _dispatch_body(offs_ref, lstart_ref, chunk_ref, pchunk_ref,
                   hp_ref, hs_ref, rt_ref, xs_ref, loc, zeros, sems, zsem, *, tm, n_prompt_tiles):
    i = pl.program_id(0)
    n_tiles = xs_ref.shape[0] // ROW_TILE
    half = lax.rem(i, 2)

    @pl.when(i == 0)
    def _():
        zeros[...] = jnp.zeros(zeros.shape, zeros.dtype)
        zero_tile = lambda row: pltpu.make_async_copy(
            zeros, xs_ref.at[pl.ds(pl.multiple_of(row, ROW_TILE), ROW_TILE)], zsem)
        n_used = offs_ref[N_EXPERTS] // ROW_TILE

        def tail(j, carry, op):
            op(zero_tile(j * ROW_TILE))
            return carry

        for op in (lambda c: c.start(), lambda c: c.wait()):
            for e in range(N_EXPERTS):
                @pl.when(offs_ref[e + 1] > offs_ref[e])
                def _():
                    op(zero_tile(offs_ref[e + 1] - ROW_TILE))
            lax.fori_loop(n_used, n_tiles, functools.partial(tail, op=op), None)

    pos0, pos1 = _local_positions_row(rt_ref, lstart_ref)
    used = chunk_ref[0, N_CHUNK_LANE] * RUN_ALIGN
    body_rows = loc.shape[1] - MXU_DIM

    def sort_rows(h_ref, lo, n):
        slot = lo + lax.broadcasted_iota(jnp.int32, (n, tm), 0)
        perm = jnp.where(slot == pos0, 1.0, jnp.where(slot == pos1, 1.0, 0.0)).astype(BF16)
        loc[half, lo:lo + n, :] = jnp.dot(perm, h_ref[...], preferred_element_type=F32).astype(BF16)

    for h_ref, mine in ((hp_ref, i < n_prompt_tiles), (hs_ref, i >= n_prompt_tiles)):
        @pl.when(mine)
        def _():
            sort_rows(h_ref, 0, body_rows)

        @pl.when(mine & (used > body_rows))
        def _():
            sort_rows(h_ref, body_rows, MXU_DIM)

    def chunk(buf, lo, go):
        return pltpu.make_async_copy(loc.at[buf, pl.ds(lo, RUN_ALIGN)], xs_ref.at[pl.ds(go, RUN_ALIGN)],
                                     sems.at[buf])

    @pl.when(i > 0)
    def _():
        _for_each_chunk(pchunk_ref, lambda lo, go, pr: chunk(1 - half, lo, go).wait())

    _for_each_chunk(chunk_ref, lambda lo, go, pr: chunk(half, lo, go).start(priority=pr))

    @pl.when(i == pl.num_programs(0) - 1)
    def _():
        _for_each_chunk(chunk_ref, lambda lo, go, pr: chunk(half, lo, go).wait())


def _dispatch(h3_prompt, h3_sample, route_t, plan, *, tm):
    lstart, chunks, offs, _ = plan
    n_p, n_s, d = h3_prompt.shape[0], h3_sample.shape[0], h3_prompt.shape[-1]
    assert n_p % tm == 0 and n_s % tm == 0
    n = n_p + n_s
    nt = n // tm
    npt = n_p // tm
    n_rows = _n_row_tiles(n, tm) * ROW_TILE
    smem_tile = lambda: pl.BlockSpec((None, 1, LANES), lambda i, offs: (i, 0, 0), memory_space=pltpu.SMEM)
    smem_prev = lambda: pl.BlockSpec((None, 1, LANES), lambda i, offs: (jnp.maximum(i - 1, 0), 0, 0),
                                     memory_space=pltpu.SMEM)
    return pl.pallas_call(
        functools.partial(_dispatch_body, tm=tm, n_prompt_tiles=npt),
        grid_spec=pltpu.PrefetchScalarGridSpec(
            num_scalar_prefetch=1,
            grid=(nt,),
            in_specs=[smem_tile(), smem_tile(), smem_prev(),
                      pl.BlockSpec((tm, d), lambda i, offs: (jnp.minimum(i, npt - 1), 0)),
                      pl.BlockSpec((tm, d), lambda i, offs: (jnp.maximum(i - npt, 0), 0)),
                      pl.BlockSpec((ROUTE_ROWS, tm), lambda i, offs: (0, i))],
            out_specs=pl.BlockSpec(memory_space=pl.ANY),
            scratch_shapes=[pltpu.VMEM((2, _local_rows(tm), d), BF16), pltpu.VMEM((ROW_TILE, d), BF16),
                            pltpu.SemaphoreType.DMA((2,)), pltpu.SemaphoreType.DMA(())]),
        out_shape=jax.ShapeDtypeStruct((n_rows, d), BF16),
        compiler_params=pltpu.CompilerParams(dimension_semantics=("arbitrary",), vmem_limit_bytes=VMEM_LIMIT),
        name="moe_dispatch",
    )(offs, lstart, chunks, chunks, h3_prompt, h3_sample, route_t)


def _experts_body(te_ref, offs_ref, xs_ref, wg_ref, wu_ref, wd_ref, ys_ref,
                  wg_buf, wu_buf, wd_buf, wgu_bf, wd_bf, turn_ref, sems):
    i = pl.program_id(0)
    n_used = offs_ref[N_EXPERTS] // ROW_TILE

    def fetch(expert, half):
        return [pltpu.make_async_copy(src.at[expert], dst.at[half], sems.at[half, k])
                for k, (src, dst) in enumerate(((wg_ref, wg_buf), (wu_ref, wu_buf), (wd_ref, wd_buf)))]

    @pl.when(i == 0)
    def _():
        turn_ref[0] = 0
        for copy in fetch(te_ref[0], 0):
            copy.start()

    @pl.when(i < n_used)
    def _():
        expert = te_ref[i]

        @pl.when((i == 0) | (expert != te_ref[jnp.maximum(i - 1, 0)]))
        def _():
            half = lax.rem(turn_ref[0], 2)
            turn_ref[0] = turn_ref[0] + 1
            for copy in fetch(expert, half):
                copy.wait()
            wgu_bf[:, 0:D_EXPERT] = wg_buf[half].astype(BF16)
            wgu_bf[:, D_EXPERT:] = wu_buf[half].astype(BF16)
            wd_bf[...] = wd_buf[half].astype(BF16)
            following = offs_ref[expert + 1] // ROW_TILE

            @pl.when(following < n_used)
            def _():
                for copy in fetch(te_ref[following], 1 - half):
                    copy.start(priority=1)

        dot = functools.partial(jnp.dot, preferred_element_type=F32)
        for rows in _chains(ROW_TILE, ROW_TILE):
            gate_up = dot(xs_ref[rows, :], wgu_bf[...])
            gate, up = gate_up[:, :D_EXPERT], gate_up[:, D_EXPERT:]
            act = (gate * jax.nn.sigmoid(gate) * up).astype(BF16)
            ys_ref[rows, :] = dot(act, wd_bf[...]).astype(BF16)

    @pl.when(i >= n_used)
    def _():
        ys_ref[...] = jnp.zeros(ys_ref.shape, ys_ref.dtype)


def _experts(xs, te, offs, w_gate, w_up, w_down):
    n_rows, d = xs.shape
    last = lambda i, te, offs: jnp.minimum(i, offs[N_EXPERTS] // ROW_TILE - 1)
    hbm = pl.BlockSpec(memory_space=pl.ANY)
    return pl.pallas_call(
        _experts_body,
        grid_spec=pltpu.PrefetchScalarGridSpec(
            num_scalar_prefetch=2,
            grid=(n_rows // ROW_TILE,),
            in_specs=[pl.BlockSpec((ROW_TILE, d), lambda i, te, offs: (last(i, te, offs), 0)), hbm, hbm, hbm],
            out_specs=pl.BlockSpec((ROW_TILE, d), lambda i, te, offs: (i, 0)),
            scratch_shapes=[pltpu.VMEM((2, d, D_EXPERT), F32), pltpu.VMEM((2, d, D_EXPERT), F32),
                            pltpu.VMEM((2, D_EXPERT, d), F32),
                            pltpu.VMEM((d, 2 * D_EXPERT), BF16), pltpu.VMEM((D_EXPERT, d), BF16),
                            pltpu.SMEM((1,), jnp.int32), pltpu.SemaphoreType.DMA((2, 3))]),
        out_shape=jax.ShapeDtypeStruct((n_rows, d), BF16),
        compiler_params=pltpu.CompilerParams(dimension_semantics=("arbitrary",), vmem_limit_bytes=VMEM_LIMIT),
        name="moe_experts",
    )(te, offs, xs, w_gate, w_up, w_down)


def _combine_body(chunk_ref, nchunk_ref, x2p_ref, x2s_ref, route_ref, lsv_ref, ys_ref, g_ref,
                  yp_ref, ysm_ref, loc, sems, *, tm, n_prompt_tiles):
    i = pl.program_id(0)
    half = lax.rem(i, 2)

    def chunk(buf, lo, go):
        return pltpu.make_async_copy(ys_ref.at[pl.ds(go, RUN_ALIGN)], loc.at[buf, pl.ds(lo, RUN_ALIGN)],
                                     sems.at[buf])

    @pl.when(i == 0)
    def _():
        loc[...] = jnp.zeros(loc.shape, loc.dtype)
        _for_each_chunk(chunk_ref, lambda lo, go, pr: chunk(0, lo, go).start(priority=pr))

    @pl.when(i + 1 < pl.num_programs(0))
    def _():
        _for_each_chunk(nchunk_ref, lambda lo, go, pr: chunk(1 - half, lo, go).start(priority=pr))

    _for_each_chunk(chunk_ref, lambda lo, go, pr: chunk(half, lo, go).wait())

    lane = lax.broadcasted_iota(jnp.int32, (tm, LANES), 1)
    picks = []
    for r_eid, r_rank, r_gate in ((R_EID0, R_RANK0, R_GATE0), (R_EID1, R_RANK1, R_GATE1)):
        eid = route_ref[:, r_eid:r_eid + 1].astype(jnp.int32)
        start = jnp.sum(jnp.where(lane == eid, lsv_ref[...], 0), axis=-1, keepdims=True)
        picks.append((route_ref[:, r_rank:r_rank + 1].astype(jnp.int32) + start, route_ref[:, r_gate:r_gate + 1]))

    slot = lax.broadcasted_iota(jnp.int32, (tm, loc.shape[1]), 1)
    weights = jnp.zeros(slot.shape, F32)
    for pos, gate in picks:
        weights = jnp.where(slot == pos, gate, weights)
    moe = jnp.dot(weights.astype(BF16), loc[half], preferred_element_type=F32)

    @pl.when(i < n_prompt_tiles)
    def _():
        yp_ref[...] = _rms(x2p_ref[...] + moe, g_ref[...])

    @pl.when(i >= n_prompt_tiles)
    def _():
        ysm_ref[...] = _rms(x2s_ref[...] + moe, g_ref[...])


def _combine(x2_prompt, x2_sample, route, plan, ys, g_final, *, tm):
    lstart, chunks, _, _ = plan
    (n_p, d), n_s = x2_prompt.shape, x2_sample.shape[0]
    assert n_p % tm == 0 and n_s % tm == 0
    npt = n_p // tm
    nt = npt + n_s // tm
    smem_tile = lambda: pl.BlockSpec((None, 1, LANES), lambda i: (i, 0, 0), memory_space=pltpu.SMEM)
    smem_next = lambda: pl.BlockSpec((None, 1, LANES), lambda i: (jnp.minimum(i + 1, nt - 1), 0, 0),
                                     memory_space=pltpu.SMEM)
    prompt_tile = lambda: pl.BlockSpec((tm, d), lambda i: (jnp.minimum(i, npt - 1), 0))
    sample_tile = lambda: pl.BlockSpec((tm, d), lambda i: (jnp.maximum(i - npt, 0), 0))
    return pl.pallas_call(
        functools.partial(_combine_body, tm=tm, n_prompt_tiles=npt),
        grid=(nt,),
        in_specs=[smem_tile(), smem_next(), prompt_tile(), sample_tile(),
                  pl.BlockSpec((tm, ROUTE_ROWS), lambda i: (i, 0)),
                  pl.BlockSpec((None, 1, LANES), lambda i: (i, 0, 0)),
                  pl.BlockSpec(memory_space=pl.ANY),
                  pl.BlockSpec((1, d), lambda i: (0, 0))],
        out_specs=[prompt_tile(), sample_tile()],
        out_shape=[jax.ShapeDtypeStruct((n_p, d), F32), jax.ShapeDtypeStruct((n_s, d), F32)],
        scratch_shapes=[pltpu.VMEM((2, _local_rows(tm), d), BF16), pltpu.SemaphoreType.DMA((2,))],
        compiler_params=pltpu.CompilerParams(dimension_semantics=("arbitrary",), vmem_limit_bytes=VMEM_LIMIT),
        name="moe_combine",
    )(chunks, chunks, x2_prompt, x2_sample, route, lstart, ys, g_final)


TOKEN_TILE = 512
FOX_Q_TILE = 256
FOX_CACHE_TILE = 2048


def kernel(x_prompt, x_sample, cache_fox_k, cache_fox_v, cache_fox_logf, cache_band_k, cache_band_v, cache_mem_k, cache_mem_v, mem_prompt, g_mix, w_in, b_forget, g_out_fox, g_out_band, rel_table, w_out, g_cross, g_mem, w_cq, w_ck, w_cv, w_co, g_ffn, w_router1, b_router1, w_router2, b_router2, w_exp_gate, w_exp_up, w_exp_down, g_final):
    assert g_mix.shape[0] == 1, "single-layer model"
    bsz, seq, d = x_prompt.shape
    sb, st, _ = x_sample.shape
    n_s = sb * st
    n_mem = mem_prompt.shape[1]
    row = lambda g: g.reshape(1, -1)

    w_pad, bf_pad, g_mix_r = _prep_proj(w_in[0], b_forget[0], g_mix[0])
    g_of, g_ob = row(g_out_fox[0]), row(g_out_band[0])
    bias_t, bias_s = _band_bias(_prep_band_bias_row(rel_table[0]))

    qx, kx, vat, qxb, kxb, vbt, kaf, vaf, kbf, vbf, logf = _proj(
        x_prompt, g_mix_r, w_pad, bf_pad, tm=TOKEN_TILE, prompt=True)
    a_p = _fox_prompt(qx, kx, vat, g_of, tq=FOX_Q_TILE)
    b_p = _band_prompt(qxb, kxb, vbt, bias_t, g_ob)

    s_out = _proj(x_sample.reshape(1, n_s, d), g_mix_r, w_pad, bf_pad, tm=n_s, prompt=False)
    sqa, ska, sva, sqb, skb, svb, skaf, svaf, skbf, svbf = (t.reshape(sb, st, W_GROUP) for t in s_out[:10])
    slogf = s_out[10].reshape(sb, st, N_HEADS)
    slft = s_out[11].reshape(N_HEADS, sb, st).transpose(1, 0, 2)
    past = cache_fox_k.shape[2]
    a_s = _fox_sample(sqa, ska, sva, slft,
                      cache_fox_k[0].reshape(sb, past, W_GROUP), cache_fox_v[0].reshape(sb, past, W_GROUP),
                      cache_fox_logf[0].transpose(0, 2, 1), g_of, pt=FOX_CACHE_TILE)
    bp = cache_band_k.shape[2]
    b_s, nbk, nbv = _band_sample(sqb, skb, svb, skbf, svbf,
                                 cache_band_k[0].reshape(sb, bp, W_GROUP), cache_band_v[0].reshape(sb, bp, W_GROUP),
                                 bias_s, g_ob)

    w_ckv = jnp.concatenate([w_ck[0], w_cv[0]], axis=1).astype(BF16)
    mkf, mvf, mk, mv = _mem_kv(mem_prompt, row(g_mem[0]), w_ckv)
    post_w = _prep_post(w_out[0], g_cross[0], w_cq[0], w_co[0], g_ffn[0],
                        w_router1[0], b_router1[0], w_router2[0], b_router2[0])
    x2_p, h3_p, route_p, routet_p, cnt_p = _post_block(x_prompt, a_p, b_p, mk, mv, post_w, tm=TOKEN_TILE)
    cmk = cache_mem_k[0].reshape(sb, n_mem, W_MEM).astype(BF16)
    cmv = cache_mem_v[0].reshape(sb, n_mem, W_MEM).astype(BF16)
    x2_s, h3_s, route_s, routet_s, cnt_s = _post_block(x_sample, a_s, b_s, cmk, cmv, post_w, tm=TOKEN_TILE)
    route = jnp.concatenate([route_p, route_s], axis=0)
    route_t = jnp.concatenate([routet_p, routet_s], axis=1)

    plan = _plan(jnp.concatenate([cnt_p, cnt_s], axis=0), TOKEN_TILE)
    xs = _dispatch(h3_p, h3_s, route_t, plan, tm=TOKEN_TILE)
    ys = _experts(xs, plan[3], plan[2], w_exp_gate[0], w_exp_up[0], w_exp_down[0])
    y_p, y_s = _combine(x2_p, x2_s, route, plan, ys, row(g_final), tm=TOKEN_TILE)

    heads = lambda t, n: t.reshape(1, n, -1, N_HEADS, HEAD_DIM)
    mem_heads = lambda t: t.reshape(1, bsz, n_mem, N_HEADS_MEM, HEAD_DIM_MEM)
    return (y_p.reshape(bsz, seq, d), y_s.reshape(sb, st, d),
            heads(kaf, bsz), heads(vaf, bsz), logf.reshape(1, bsz, seq, N_HEADS),
            heads(kbf, bsz), heads(vbf, bsz), mem_heads(mkf), mem_heads(mvf),
            heads(skaf, sb), heads(svaf, sb), slogf.reshape(1, sb, st, N_HEADS),
            heads(nbk, sb), heads(nbv, sb))
```
